```python
import jax, jax.numpy as jnp
from jax import lax
import numpy as np

D_MODEL = 1024
BATCH = 8
SEQ = 8192
DEPTH = 2

N_MIXERS = 2
N_A_LAYERS = (DEPTH + 1) // 2
N_B_LAYERS = DEPTH // 2
EPS = 1e-6

D_FF = ((8 * D_MODEL // 3 + 127) // 128) * 128

A_DK = 128
A_DV = 128
A_HEADS = D_MODEL // A_DK
A_CONV = 4
A_CHUNK = 64

B_HD = 64
B_HEADS = D_MODEL // B_HD
B_KV_HEADS = 4
B_WINDOW = 128
B_BLOCK = 128

kernel_name = "hybrid_gdn_swa_sink_macaron"


def rmsnorm(x, w):
    xf = x.astype(jnp.float32)
    y = xf * lax.rsqrt(jnp.mean(xf * xf, axis=-1, keepdims=True) + EPS) * w.astype(jnp.float32)
    return y.astype(x.dtype)


def l2norm(x):
    xf = x.astype(jnp.float32)
    return xf * lax.rsqrt(jnp.sum(xf * xf, axis=-1, keepdims=True) + EPS)


def swiglu(h, w_gu, w_down):
    gate, up = jnp.split(h @ w_gu, 2, axis=-1)
    return (jax.nn.silu(gate) * up) @ w_down


def causal_depthwise_conv(x, w):
    C = x.shape[-1]
    return lax.conv_general_dilated(
        x, w[:, None, :].astype(x.dtype), window_strides=(1,), padding=[(A_CONV - 1, 0)],
        dimension_numbers=("NWC", "WIO", "NWC"), feature_group_count=C)


def gated_delta_rule_chunked(q, k, v, g, beta):
    Bsz, T, H, DK = q.shape
    DV = v.shape[-1]
    C = A_CHUNK
    N = T // C
    f32 = jnp.float32

    def chunk(t):
        return t.astype(f32).reshape(Bsz, N, C, H, -1).transpose(0, 3, 1, 2, 4)

    q, k, v = chunk(q), chunk(k), chunk(v)
    g = g.astype(f32).reshape(Bsz, N, C, H).transpose(0, 3, 1, 2)
    beta = beta.astype(f32).reshape(Bsz, N, C, H).transpose(0, 3, 1, 2)
    g = jnp.cumsum(g, axis=-1)

    idx = jnp.arange(C)
    lower_incl = idx[:, None] >= idx[None, :]
    strict = idx[:, None] > idx[None, :]
    decay = jnp.exp(jnp.where(lower_incl, g[..., :, None] - g[..., None, :], -jnp.inf))

    kb = k * beta[..., None]
    L = jnp.where(strict, jnp.einsum("bhnid,bhnjd->bhnij", kb, k) * decay, 0.0)
    rhs = jnp.concatenate([v * beta[..., None], kb * jnp.exp(g)[..., None]], axis=-1)
    sol = lax.linalg.triangular_solve(L, rhs, left_side=True, lower=True,
                                      transpose_a=False, conjugate_a=False, unit_diagonal=True)
    u, w = sol[..., :DV], sol[..., DV:]

    a_qk = jnp.einsum("bhnid,bhnjd->bhnij", q, k) * decay
    q_dec = q * jnp.exp(g)[..., None]
    k_dec = k * jnp.exp(g[..., -1:] - g)[..., None]
    g_last = jnp.exp(g[..., -1])

    xs = (jnp.moveaxis(q_dec, 2, 0), jnp.moveaxis(k_dec, 2, 0), jnp.moveaxis(u, 2, 0),
          jnp.moveaxis(w, 2, 0), jnp.moveaxis(a_qk, 2, 0), jnp.moveaxis(g_last, 2, 0))

    def step(S, inp):
        qd, kd, u_c, w_c, a_c, gl = inp
        v_new = u_c - jnp.einsum("bhck,bhkv->bhcv", w_c, S)
        o = jnp.einsum("bhck,bhkv->bhcv", qd, S) + jnp.einsum("bhij,bhjv->bhiv", a_c, v_new)
        S = S * gl[..., None, None] + jnp.einsum("bhck,bhcv->bhkv", kd, v_new)
        return S, o

    S0 = jnp.zeros((Bsz, H, DK, DV), f32)
    _, o = lax.scan(step, S0, xs)
    return o.transpose(1, 0, 3, 2, 4).reshape(Bsz, T, H, DV)


def mixer_gated_deltanet(h, w_in, w_conv, A_log, dt_bias, out_norm, w_out):
    Bsz, T, _ = h.shape
    HK = A_HEADS * A_DK
    HV = A_HEADS * A_DV
    proj = h @ w_in
    qkv = proj[..., :2 * HK + HV]
    z = proj[..., 2 * HK + HV:2 * HK + 2 * HV]
    b = proj[..., 2 * HK + 2 * HV:2 * HK + 2 * HV + A_HEADS]
    a = proj[..., 2 * HK + 2 * HV + A_HEADS:]
    qkv = jax.nn.silu(causal_depthwise_conv(qkv, w_conv))
    q = l2norm(qkv[..., :HK].reshape(Bsz, T, A_HEADS, A_DK)) * (A_DK ** -0.5)
    k = l2norm(qkv[..., HK:2 * HK].reshape(Bsz, T, A_HEADS, A_DK))
    v = qkv[..., 2 * HK:].reshape(Bsz, T, A_HEADS, A_DV)
    beta = jax.nn.sigmoid(b.astype(jnp.float32))
    g = -jnp.exp(A_log.astype(jnp.float32)) * jax.nn.softplus(a.astype(jnp.float32) + dt_bias.astype(jnp.float32))
    o = gated_delta_rule_chunked(q, k, v, g, beta)
    zf = z.reshape(Bsz, T, A_HEADS, A_DV).astype(jnp.float32)
    o = rmsnorm(o, out_norm) * jax.nn.silu(zf)
    return o.reshape(Bsz, T, HV).astype(h.dtype) @ w_out


def mixer_sliding_window_sinks(h, w_in, b_in, sinks, w_out, b_out):
    Bsz, T, _ = h.shape
    G = B_HEADS // B_KV_HEADS
    NB = T // B_BLOCK
    HQ = B_HEADS * B_HD
    HKV = B_KV_HEADS * B_HD
    proj = h @ w_in + b_in
    q = proj[..., :HQ].reshape(Bsz, NB, B_BLOCK, B_KV_HEADS, G, B_HD)
    k = proj[..., HQ:HQ + HKV].reshape(Bsz, NB, B_BLOCK, B_KV_HEADS, B_HD)
    v = proj[..., HQ + HKV:].reshape(Bsz, NB, B_BLOCK, B_KV_HEADS, B_HD)

    def with_prev(t):
        prev = jnp.concatenate([jnp.zeros_like(t[:, :1]), t[:, :-1]], axis=1)
        return jnp.concatenate([prev, t], axis=2)

    kk, vv = with_prev(k), with_prev(v)
    s = jnp.einsum("bnqhgd,bnkhd->bnhgqk", q, kk).astype(jnp.float32) * (B_HD ** -0.5)
    qi = jnp.arange(B_BLOCK)[:, None]
    kj = jnp.arange(2 * B_BLOCK)[None, :]
    rel = qi + B_BLOCK - kj
    band = (rel >= 0) & (rel < B_WINDOW)
    blk = jnp.arange(NB)[:, None, None]
    valid = band[None] & ((blk > 0) | (kj >= B_BLOCK)[None])
    s = jnp.where(valid[None, :, None, None], s, -jnp.inf)
    sink = jnp.broadcast_to(sinks.astype(jnp.float32).reshape(B_KV_HEADS, G)[None, None, :, :, None, None],
                            s.shape[:-1] + (1,))
    p = jax.nn.softmax(jnp.concatenate([s, sink], axis=-1), axis=-1)[..., :-1]
    o = jnp.einsum("bnhgqk,bnkhd->bnqhgd", p.astype(vv.dtype), vv)
    return o.reshape(Bsz, T, HQ) @ w_out + b_out


def _fwd_setup_inputs(seed: int = 0) -> dict:
    key = jax.random.key(seed)
    ks = iter(jax.random.split(key, 32))
    f32 = jnp.float32
    nrm = lambda shape, scale: jax.random.normal(next(ks), shape, f32) * scale
    gain = lambda shape: 1.0 + 0.02 * jax.random.normal(next(ks), shape, f32)
    D = D_MODEL
    a_in_cols = 2 * A_HEADS * A_DK + 2 * A_HEADS * A_DV + 2 * A_HEADS
    b_in_cols = (B_HEADS + 2 * B_KV_HEADS) * B_HD
    dt = jnp.exp(jax.random.uniform(next(ks), (N_A_LAYERS, A_HEADS), f32, np.log(1e-3), np.log(1e-1)))
    return {
        "x": jax.random.normal(next(ks), (BATCH, SEQ, D), f32),
        "ffn1_norm": gain((DEPTH, D)),
        "ffn1_w_gu": nrm((DEPTH, D, 2 * D_FF), D ** -0.5),
        "ffn1_w_down": nrm((DEPTH, D_FF, D), D_FF ** -0.5),
        "mix_norm": gain((DEPTH, D)),
        "ffn2_norm": gain((DEPTH, D)),
        "ffn2_w_gu": nrm((DEPTH, D, 2 * D_FF), D ** -0.5),
        "ffn2_w_down": nrm((DEPTH, D_FF, D), D_FF ** -0.5),
        "a_w_in": nrm((N_A_LAYERS, D, a_in_cols), D ** -0.5),
        "a_w_conv": nrm((N_A_LAYERS, A_CONV, 2 * A_HEADS * A_DK + A_HEADS * A_DV), A_CONV ** -0.5),
        "a_A_log": jnp.log(jax.random.uniform(next(ks), (N_A_LAYERS, A_HEADS), f32, 1.0, 16.0)),
        "a_dt_bias": dt + jnp.log(-jnp.expm1(-dt)),
        "a_out_norm": gain((N_A_LAYERS, A_DV)),
        "a_w_out": nrm((N_A_LAYERS, A_HEADS * A_DV, D), (A_HEADS * A_DV) ** -0.5),
        "b_w_in": nrm((N_B_LAYERS, D, b_in_cols), D ** -0.5),
        "b_b_in": nrm((N_B_LAYERS, b_in_cols), 0.02),
        "b_sinks": nrm((N_B_LAYERS, B_HEADS), 1.0),
        "b_w_out": nrm((N_B_LAYERS, B_HEADS * B_HD, D), (B_HEADS * B_HD) ** -0.5),
        "b_b_out": nrm((N_B_LAYERS, D), 0.02),
        "final_norm": gain((D,)),
    }


def _fwd_reference(x, ffn1_norm, ffn1_w_gu, ffn1_w_down, mix_norm, ffn2_norm, ffn2_w_gu, ffn2_w_down,
              a_w_in, a_w_conv, a_A_log, a_dt_bias, a_out_norm, a_w_out,
              b_w_in, b_b_in, b_sinks, b_w_out, b_b_out, final_norm):
    for layer in range(DEPTH):
        x = x + 0.5 * swiglu(rmsnorm(x, ffn1_norm[layer]), ffn1_w_gu[layer], ffn1_w_down[layer])
        h = rmsnorm(x, mix_norm[layer])
        j = layer // N_MIXERS
        if layer % N_MIXERS == 0:
            y = mixer_gated_deltanet(h, a_w_in[j], a_w_conv[j], a_A_log[j], a_dt_bias[j],
                                     a_out_norm[j], a_w_out[j])
        else:
            y = mixer_sliding_window_sinks(h, b_w_in[j], b_b_in[j], b_sinks[j], b_w_out[j], b_b_out[j])
        x = x + y
        x = x + 0.5 * swiglu(rmsnorm(x, ffn2_norm[layer]), ffn2_w_gu[layer], ffn2_w_down[layer])
    return rmsnorm(x, final_norm)


import jax as _jax
import jax.numpy as _jnp

TWIN_FORMAT = 'train_step'
FWD_PARAMS = ['x', 'ffn1_norm', 'ffn1_w_gu', 'ffn1_w_down', 'mix_norm', 'ffn2_norm', 'ffn2_w_gu', 'ffn2_w_down', 'a_w_in', 'a_w_conv', 'a_A_log', 'a_dt_bias', 'a_out_norm', 'a_w_out', 'b_w_in', 'b_b_in', 'b_sinks', 'b_w_out', 'b_b_out', 'final_norm']
TWIN_WEIGHTS = ['ffn1_norm', 'ffn1_w_gu', 'ffn1_w_down', 'mix_norm', 'ffn2_norm', 'ffn2_w_gu', 'ffn2_w_down', 'a_w_in', 'a_w_conv', 'a_A_log', 'a_dt_bias', 'a_out_norm', 'a_w_out', 'b_w_in', 'b_b_in', 'b_sinks', 'b_w_out', 'b_b_out', 'final_norm']
TWIN_DIFF_INPUT = 'x'
TWIN_INPUTS = ['x', 'ffn1_norm', 'ffn1_w_gu', 'ffn1_w_down', 'mix_norm', 'ffn2_norm', 'ffn2_w_gu', 'ffn2_w_down', 'a_w_in', 'a_w_conv', 'a_A_log', 'a_dt_bias', 'a_out_norm', 'a_w_out', 'b_w_in', 'b_b_in', 'b_sinks', 'b_w_out', 'b_b_out', 'final_norm', 'loss_target', 'm_ffn1_norm', 'm_ffn1_w_gu', 'm_ffn1_w_down', 'm_mix_norm', 'm_ffn2_norm', 'm_ffn2_w_gu', 'm_ffn2_w_down', 'm_a_w_in', 'm_a_w_conv', 'm_a_A_log', 'm_a_dt_bias', 'm_a_out_norm', 'm_a_w_out', 'm_b_w_in', 'm_b_b_in', 'm_b_sinks', 'm_b_w_out', 'm_b_b_out', 'm_final_norm', 'v_ffn1_norm', 'v_ffn1_w_gu', 'v_ffn1_w_down', 'v_mix_norm', 'v_ffn2_norm', 'v_ffn2_w_gu', 'v_ffn2_w_down', 'v_a_w_in', 'v_a_w_conv', 'v_a_A_log', 'v_a_dt_bias', 'v_a_out_norm', 'v_a_w_out', 'v_b_w_in', 'v_b_b_in', 'v_b_sinks', 'v_b_w_out', 'v_b_b_out', 'v_final_norm']
TWIN_OUTPUTS = ['loss', 'grad_x', 'grad_ffn1_norm', 'grad_ffn1_w_gu', 'grad_ffn1_w_down', 'grad_mix_norm', 'grad_ffn2_norm', 'grad_ffn2_w_gu', 'grad_ffn2_w_down', 'grad_a_w_in', 'grad_a_w_conv', 'grad_a_A_log', 'grad_a_dt_bias', 'grad_a_out_norm', 'grad_a_w_out', 'grad_b_w_in', 'grad_b_b_in', 'grad_b_sinks', 'grad_b_w_out', 'grad_b_b_out', 'grad_final_norm', 'delta_ffn1_norm', 'delta_ffn1_w_gu', 'delta_ffn1_w_down', 'delta_mix_norm', 'delta_ffn2_norm', 'delta_ffn2_w_gu', 'delta_ffn2_w_down', 'delta_a_w_in', 'delta_a_w_conv', 'delta_a_A_log', 'delta_a_dt_bias', 'delta_a_out_norm', 'delta_a_w_out', 'delta_b_w_in', 'delta_b_b_in', 'delta_b_sinks', 'delta_b_w_out', 'delta_b_b_out', 'delta_final_norm', 'new_m_ffn1_norm', 'new_m_ffn1_w_gu', 'new_m_ffn1_w_down', 'new_m_mix_norm', 'new_m_ffn2_norm', 'new_m_ffn2_w_gu', 'new_m_ffn2_w_down', 'new_m_a_w_in', 'new_m_a_w_conv', 'new_m_a_A_log', 'new_m_a_dt_bias', 'new_m_a_out_norm', 'new_m_a_w_out', 'new_m_b_w_in', 'new_m_b_b_in', 'new_m_b_sinks', 'new_m_b_w_out', 'new_m_b_b_out', 'new_m_final_norm', 'new_v_ffn1_norm', 'new_v_ffn1_w_gu', 'new_v_ffn1_w_down', 'new_v_mix_norm', 'new_v_ffn2_norm', 'new_v_ffn2_w_gu', 'new_v_ffn2_w_down', 'new_v_a_w_in', 'new_v_a_w_conv', 'new_v_a_A_log', 'new_v_a_dt_bias', 'new_v_a_out_norm', 'new_v_a_w_out', 'new_v_b_w_in', 'new_v_b_b_in', 'new_v_b_sinks', 'new_v_b_w_out', 'new_v_b_b_out', 'new_v_final_norm']
TWIN_LEAF_KINDS = {'loss': 'loss', 'grad_x': 'grad_x', 'grad_ffn1_norm': 'grad_w', 'grad_ffn1_w_gu': 'grad_w', 'grad_ffn1_w_down': 'grad_w', 'grad_mix_norm': 'grad_w', 'grad_ffn2_norm': 'grad_w', 'grad_ffn2_w_gu': 'grad_w', 'grad_ffn2_w_down': 'grad_w', 'grad_a_w_in': 'grad_w', 'grad_a_w_conv': 'grad_w', 'grad_a_A_log': 'grad_w', 'grad_a_dt_bias': 'grad_w', 'grad_a_out_norm': 'grad_w', 'grad_a_w_out': 'grad_w', 'grad_b_w_in': 'grad_w', 'grad_b_b_in': 'grad_w', 'grad_b_sinks': 'grad_w', 'grad_b_w_out': 'grad_w', 'grad_b_b_out': 'grad_w', 'grad_final_norm': 'grad_w', 'delta_ffn1_norm': 'delta_w', 'delta_ffn1_w_gu': 'delta_w', 'delta_ffn1_w_down': 'delta_w', 'delta_mix_norm': 'delta_w', 'delta_ffn2_norm': 'delta_w', 'delta_ffn2_w_gu': 'delta_w', 'delta_ffn2_w_down': 'delta_w', 'delta_a_w_in': 'delta_w', 'delta_a_w_conv': 'delta_w', 'delta_a_A_log': 'delta_w', 'delta_a_dt_bias': 'delta_w', 'delta_a_out_norm': 'delta_w', 'delta_a_w_out': 'delta_w', 'delta_b_w_in': 'delta_w', 'delta_b_b_in': 'delta_w', 'delta_b_sinks': 'delta_w', 'delta_b_w_out': 'delta_w', 'delta_b_b_out': 'delta_w', 'delta_final_norm': 'delta_w', 'new_m_ffn1_norm': 'new_m', 'new_m_ffn1_w_gu': 'new_m', 'new_m_ffn1_w_down': 'new_m', 'new_m_mix_norm': 'new_m', 'new_m_ffn2_norm': 'new_m', 'new_m_ffn2_w_gu': 'new_m', 'new_m_ffn2_w_down': 'new_m', 'new_m_a_w_in': 'new_m', 'new_m_a_w_conv': 'new_m', 'new_m_a_A_log': 'new_m', 'new_m_a_dt_bias': 'new_m', 'new_m_a_out_norm': 'new_m', 'new_m_a_w_out': 'new_m', 'new_m_b_w_in': 'new_m', 'new_m_b_b_in': 'new_m', 'new_m_b_sinks': 'new_m', 'new_m_b_w_out': 'new_m', 'new_m_b_b_out': 'new_m', 'new_m_final_norm': 'new_m', 'new_v_ffn1_norm': 'new_v', 'new_v_ffn1_w_gu': 'new_v', 'new_v_ffn1_w_down': 'new_v', 'new_v_mix_norm': 'new_v', 'new_v_ffn2_norm': 'new_v', 'new_v_ffn2_w_gu': 'new_v', 'new_v_ffn2_w_down': 'new_v', 'new_v_a_w_in': 'new_v', 'new_v_a_w_conv': 'new_v', 'new_v_a_A_log': 'new_v', 'new_v_a_dt_bias': 'new_v', 'new_v_a_out_norm': 'new_v', 'new_v_a_w_out': 'new_v', 'new_v_b_w_in': 'new_v', 'new_v_b_b_in': 'new_v', 'new_v_b_sinks': 'new_v', 'new_v_b_w_out': 'new_v', 'new_v_b_b_out': 'new_v', 'new_v_final_norm': 'new_v'}


def _forward(args):
    return _fwd_reference(*[args[k] for k in FWD_PARAMS])


def _output_shape():
    def fwd():
        inp = _fwd_setup_inputs(0)
        return _fwd_reference(*[inp[k] for k in FWD_PARAMS])
    out = _jax.eval_shape(fwd)
    return out.shape, out.dtype

N_MICROBATCH = 1
ADAM_LR = 0.001
ADAM_B1 = 0.9
ADAM_B2 = 0.999
ADAM_EPS = 1e-08
ADAM_WD = 0.01
ADAM_STEP = 10
PER_EXAMPLE_BATCH_AXIS = {'x': 0, 'loss_target': 0}
SHARED_INPUTS = []
_WEIGHT_DTYPES = {'ffn1_norm': _jnp.float32, 'ffn1_w_gu': _jnp.float32, 'ffn1_w_down': _jnp.float32, 'mix_norm': _jnp.float32, 'ffn2_norm': _jnp.float32, 'ffn2_w_gu': _jnp.float32, 'ffn2_w_down': _jnp.float32, 'a_w_in': _jnp.float32, 'a_w_conv': _jnp.float32, 'a_A_log': _jnp.float32, 'a_dt_bias': _jnp.float32, 'a_out_norm': _jnp.float32, 'a_w_out': _jnp.float32, 'b_w_in': _jnp.float32, 'b_b_in': _jnp.float32, 'b_sinks': _jnp.float32, 'b_w_out': _jnp.float32, 'b_b_out': _jnp.float32, 'final_norm': _jnp.float32}
MOMENT_SCALE = {'ffn1_norm': 1.166819e-01, 'ffn1_w_gu': 4.912406e-02, 'ffn1_w_down': 8.027278e-02, 'mix_norm': 1.769549e-01, 'ffn2_norm': 8.801480e-02, 'ffn2_w_gu': 3.737294e-02, 'ffn2_w_down': 6.103254e-02, 'a_w_in': 1.141804e-01, 'a_w_conv': 1.062892e-01, 'a_A_log': 1.112471e+00, 'a_dt_bias': 1.079794e+00, 'a_out_norm': 4.121558e-01, 'a_w_out': 1.391805e-01, 'b_w_in': 5.431323e-02, 'b_b_in': 2.841690e-01, 'b_sinks': 2.776178e-02, 'b_w_out': 4.437543e-02, 'b_b_out': 2.644016e-01, 'final_norm': 6.399559e+01}


def _to_microbatches(a, axis):
    t = _jnp.moveaxis(a, axis, 0)
    t = t.reshape((N_MICROBATCH, t.shape[0] // N_MICROBATCH) + t.shape[1:])
    return _jnp.moveaxis(t, 1, axis + 1)


def setup_inputs(seed: int = 0) -> dict:
    inp = _fwd_setup_inputs(seed)
    key = _jax.random.fold_in(_jax.random.key(seed), 7919)
    shape, _ = _output_shape()
    out = dict(inp)
    out["loss_target"] = _jax.random.normal(_jax.random.fold_in(key, 0), shape, _jnp.float32)
    for i, name in enumerate(TWIN_WEIGHTS):
        w = inp[name].astype(_jnp.float32)
        if MOMENT_SCALE is None:
            s = _jnp.sqrt(_jnp.mean(_jnp.square(w)) + 1e-30)
        else:
            s = MOMENT_SCALE[name]
        km, kv = _jax.random.split(_jax.random.fold_in(key, i + 1))
        out[name] = w
        out["m_" + name] = s * _jax.random.normal(km, w.shape, _jnp.float32)
        out["v_" + name] = (s * s) * _jax.random.uniform(kv, w.shape, _jnp.float32, 0.5, 1.5)
    if N_MICROBATCH > 1:
        for name, axis in PER_EXAMPLE_BATCH_AXIS.items():
            out[name] = _to_microbatches(out[name], axis)
    return {'x': out['x'], 'ffn1_norm': out['ffn1_norm'], 'ffn1_w_gu': out['ffn1_w_gu'], 'ffn1_w_down': out['ffn1_w_down'], 'mix_norm': out['mix_norm'], 'ffn2_norm': out['ffn2_norm'], 'ffn2_w_gu': out['ffn2_w_gu'], 'ffn2_w_down': out['ffn2_w_down'], 'a_w_in': out['a_w_in'], 'a_w_conv': out['a_w_conv'], 'a_A_log': out['a_A_log'], 'a_dt_bias': out['a_dt_bias'], 'a_out_norm': out['a_out_norm'], 'a_w_out': out['a_w_out'], 'b_w_in': out['b_w_in'], 'b_b_in': out['b_b_in'], 'b_sinks': out['b_sinks'], 'b_w_out': out['b_w_out'], 'b_b_out': out['b_b_out'], 'final_norm': out['final_norm'], 'loss_target': out['loss_target'], 'm_ffn1_norm': out['m_ffn1_norm'], 'm_ffn1_w_gu': out['m_ffn1_w_gu'], 'm_ffn1_w_down': out['m_ffn1_w_down'], 'm_mix_norm': out['m_mix_norm'], 'm_ffn2_norm': out['m_ffn2_norm'], 'm_ffn2_w_gu': out['m_ffn2_w_gu'], 'm_ffn2_w_down': out['m_ffn2_w_down'], 'm_a_w_in': out['m_a_w_in'], 'm_a_w_conv': out['m_a_w_conv'], 'm_a_A_log': out['m_a_A_log'], 'm_a_dt_bias': out['m_a_dt_bias'], 'm_a_out_norm': out['m_a_out_norm'], 'm_a_w_out': out['m_a_w_out'], 'm_b_w_in': out['m_b_w_in'], 'm_b_b_in': out['m_b_b_in'], 'm_b_sinks': out['m_b_sinks'], 'm_b_w_out': out['m_b_w_out'], 'm_b_b_out': out['m_b_b_out'], 'm_final_norm': out['m_final_norm'], 'v_ffn1_norm': out['v_ffn1_norm'], 'v_ffn1_w_gu': out['v_ffn1_w_gu'], 'v_ffn1_w_down': out['v_ffn1_w_down'], 'v_mix_norm': out['v_mix_norm'], 'v_ffn2_norm': out['v_ffn2_norm'], 'v_ffn2_w_gu': out['v_ffn2_w_gu'], 'v_ffn2_w_down': out['v_ffn2_w_down'], 'v_a_w_in': out['v_a_w_in'], 'v_a_w_conv': out['v_a_w_conv'], 'v_a_A_log': out['v_a_A_log'], 'v_a_dt_bias': out['v_a_dt_bias'], 'v_a_out_norm': out['v_a_out_norm'], 'v_a_w_out': out['v_a_w_out'], 'v_b_w_in': out['v_b_w_in'], 'v_b_b_in': out['v_b_b_in'], 'v_b_sinks': out['v_b_sinks'], 'v_b_w_out': out['v_b_w_out'], 'v_b_b_out': out['v_b_b_out'], 'v_final_norm': out['v_final_norm']}


def _loss(weights, diff, rest, loss_target):
    with _jax.named_scope("forward"):
        args = {**rest, TWIN_DIFF_INPUT: diff, **{k: w.astype(_WEIGHT_DTYPES[k]) for k, w in weights.items()}}
        y = _forward(args)
    with _jax.named_scope("loss_head"):
        err = _jnp.square(y.astype(_jnp.float32) - loss_target)
        return 0.5 * _jnp.sum(_jnp.mean(err, axis=-1)) if err.ndim else 0.5 * err


def _adamw(w, g, m, v):
    m = ADAM_B1 * m + (1.0 - ADAM_B1) * g
    v = ADAM_B2 * v + (1.0 - ADAM_B2) * _jnp.square(g)
    m_hat = m / (1.0 - ADAM_B1 ** ADAM_STEP)
    v_hat = v / (1.0 - ADAM_B2 ** ADAM_STEP)
    delta = -ADAM_LR * (m_hat / (_jnp.sqrt(v_hat) + ADAM_EPS) + ADAM_WD * w)
    return delta, m, v


def reference(x, ffn1_norm, ffn1_w_gu, ffn1_w_down, mix_norm, ffn2_norm, ffn2_w_gu, ffn2_w_down, a_w_in, a_w_conv, a_A_log, a_dt_bias, a_out_norm, a_w_out, b_w_in, b_b_in, b_sinks, b_w_out, b_b_out, final_norm, loss_target, m_ffn1_norm, m_ffn1_w_gu, m_ffn1_w_down, m_mix_norm, m_ffn2_norm, m_ffn2_w_gu, m_ffn2_w_down, m_a_w_in, m_a_w_conv, m_a_A_log, m_a_dt_bias, m_a_out_norm, m_a_w_out, m_b_w_in, m_b_b_in, m_b_sinks, m_b_w_out, m_b_b_out, m_final_norm, v_ffn1_norm, v_ffn1_w_gu, v_ffn1_w_down, v_mix_norm, v_ffn2_norm, v_ffn2_w_gu, v_ffn2_w_down, v_a_w_in, v_a_w_conv, v_a_A_log, v_a_dt_bias, v_a_out_norm, v_a_w_out, v_b_w_in, v_b_b_in, v_b_sinks, v_b_w_out, v_b_b_out, v_final_norm):
    given = dict(x=x, ffn1_norm=ffn1_norm, ffn1_w_gu=ffn1_w_gu, ffn1_w_down=ffn1_w_down, mix_norm=mix_norm, ffn2_norm=ffn2_norm, ffn2_w_gu=ffn2_w_gu, ffn2_w_down=ffn2_w_down, a_w_in=a_w_in, a_w_conv=a_w_conv, a_A_log=a_A_log, a_dt_bias=a_dt_bias, a_out_norm=a_out_norm, a_w_out=a_w_out, b_w_in=b_w_in, b_b_in=b_b_in, b_sinks=b_sinks, b_w_out=b_w_out, b_b_out=b_b_out, final_norm=final_norm, loss_target=loss_target, m_ffn1_norm=m_ffn1_norm, m_ffn1_w_gu=m_ffn1_w_gu, m_ffn1_w_down=m_ffn1_w_down, m_mix_norm=m_mix_norm, m_ffn2_norm=m_ffn2_norm, m_ffn2_w_gu=m_ffn2_w_gu, m_ffn2_w_down=m_ffn2_w_down, m_a_w_in=m_a_w_in, m_a_w_conv=m_a_w_conv, m_a_A_log=m_a_A_log, m_a_dt_bias=m_a_dt_bias, m_a_out_norm=m_a_out_norm, m_a_w_out=m_a_w_out, m_b_w_in=m_b_w_in, m_b_b_in=m_b_b_in, m_b_sinks=m_b_sinks, m_b_w_out=m_b_w_out, m_b_b_out=m_b_b_out, m_final_norm=m_final_norm, v_ffn1_norm=v_ffn1_norm, v_ffn1_w_gu=v_ffn1_w_gu, v_ffn1_w_down=v_ffn1_w_down, v_mix_norm=v_mix_norm, v_ffn2_norm=v_ffn2_norm, v_ffn2_w_gu=v_ffn2_w_gu, v_ffn2_w_down=v_ffn2_w_down, v_a_w_in=v_a_w_in, v_a_w_conv=v_a_w_conv, v_a_A_log=v_a_A_log, v_a_dt_bias=v_a_dt_bias, v_a_out_norm=v_a_out_norm, v_a_w_out=v_a_w_out, v_b_w_in=v_b_w_in, v_b_b_in=v_b_b_in, v_b_sinks=v_b_sinks, v_b_w_out=v_b_w_out, v_b_b_out=v_b_b_out, v_final_norm=v_final_norm)
    weights = {n: given[n] for n in TWIN_WEIGHTS}
    shared = {n: given[n] for n in SHARED_INPUTS}
    per_example = {n: given[n] for n in ['x']}
    grad_fn = _jax.value_and_grad(_loss, argnums=(0, 1))

    def one_microbatch(ex, loss_target):
        ex = dict(ex)
        diff = ex.pop(TWIN_DIFF_INPUT)
        return grad_fn(weights, diff, {**shared, **ex}, loss_target)

    if N_MICROBATCH == 1:
        loss, (grad_w, grad_x) = one_microbatch(per_example, given["loss_target"])
    else:
        def body(carry, xs):
            loss_sum, grad_sum = carry
            l_k, (gw_k, gx_k) = one_microbatch(xs[0], xs[1])
            with _jax.named_scope("update"):
                return (loss_sum + l_k, _jax.tree.map(_jnp.add, grad_sum, gw_k)), gx_k

        init = (_jnp.zeros((), _jnp.float32), _jax.tree.map(_jnp.zeros_like, weights))
        (loss, grad_w), grad_x = _jax.lax.scan(body, init, (per_example, given["loss_target"]))
    with _jax.named_scope("update"):
        delta_w, new_m, new_v = {}, {}, {}
        for n in TWIN_WEIGHTS:
            delta_w[n], new_m[n], new_v[n] = _adamw(weights[n], grad_w[n], given["m_" + n], given["v_" + n])
    return (loss, grad_x, *[grad_w[n] for n in TWIN_WEIGHTS], *[delta_w[n] for n in TWIN_WEIGHTS],
            *[new_m[n] for n in TWIN_WEIGHTS], *[new_v[n] for n in TWIN_WEIGHTS])
```

```python
import functools

import jax
import jax.numpy as jnp
from jax import lax
from jax.experimental import pallas as pl
from jax.experimental.pallas import tpu as pltpu

F32, BF16 = jnp.float32, jnp.bfloat16
HI = lax.Precision.HIGHEST
EPS = 1e-6

N_DEV = 8
D = 1024
FB = 704
N_FB = 4
HEADS_A, DK = 8, 128
CHUNK = 64
PREP_T = 512
A_COLS = 4224
B_HD, B_BLK = 64, 128
VMEM_LIMIT_V7X = 60 * 1024 * 1024

ADAM_LR, ADAM_B1, ADAM_B2, ADAM_EPS, ADAM_WD, ADAM_STEP = 0.001, 0.9, 0.999, 1e-08, 0.01, 10

NT = (((1,), (1,)), ((), ()))
TN = (((0,), (0,)), ((), ()))


def _pc(body, *, name, out_shape, grid=(), in_specs=None, out_specs=None, scratch=(), sem=None, **kw):
    params = pltpu.CompilerParams(dimension_semantics=sem, vmem_limit_bytes=VMEM_LIMIT_V7X)
    return pl.pallas_call(body, name=name, out_shape=out_shape, grid=grid, in_specs=in_specs, out_specs=out_specs,
                          scratch_shapes=list(scratch), compiler_params=params, **kw)


def _sds(shape, dtype):
    return jax.ShapeDtypeStruct(tuple(shape), dtype)


def _dot(a, b, dims=None, precision=None):
    if dims is None:
        return jnp.dot(a, b, preferred_element_type=F32, precision=precision)
    return lax.dot_general(a, b, dims, preferred_element_type=F32, precision=precision)


def _sigmoid(x):
    return 1.0 / (1.0 + jnp.exp(-x))


def _softplus(x):
    return jnp.maximum(x, 0.0) + jnp.log(1.0 + jnp.exp(-jnp.abs(x)))


def _rms_fwd(x, w):
    r = lax.rsqrt(jnp.mean(x * x, axis=-1, keepdims=True) + EPS)
    return x * r * w


def _rms_bwd(x, w, dy):
    r = lax.rsqrt(jnp.mean(x * x, axis=-1, keepdims=True) + EPS)
    xh = x * r
    dxh = dy * w
    dx = r * (dxh - xh * jnp.mean(dxh * xh, axis=-1, keepdims=True))
    return dx, jnp.sum(dy * xh, axis=0, keepdims=True)


def _tile(n, want):
    t = min(n, want)
    assert n % t == 0, (n, want)
    return t


def rmsnorm_bf16(x, w, name):
    T = x.shape[0]
    tm = _tile(T, 1024)

    def body(x_ref, w_ref, o_ref):
        o_ref[...] = _rms_fwd(x_ref[...], w_ref[...]).astype(BF16)

    return _pc(body, name=name, out_shape=_sds((T, D), BF16), grid=(T // tm,),
               in_specs=[pl.BlockSpec((tm, D), lambda i: (i, 0)), pl.BlockSpec((1, D), lambda i: (0, 0))],
               out_specs=pl.BlockSpec((tm, D), lambda i: (i, 0)), sem=("parallel",))(x, w)


def rmsnorm_bwd_add(x, w, dxn, dres, name):
    T = x.shape[0]
    tm = _tile(T, 512)

    def body(x_ref, w_ref, dxn_ref, dres_ref, dx_ref, dw_ref):
        dx, dw = _rms_bwd(x_ref[...], w_ref[...], dxn_ref[...])
        dx_ref[...] = dres_ref[...] + dx

        @pl.when(pl.program_id(0) == 0)
        def _():
            dw_ref[...] = jnp.zeros_like(dw_ref)
        dw_ref[...] += dw

    row = pl.BlockSpec((tm, D), lambda i: (i, 0))
    vec = pl.BlockSpec((1, D), lambda i: (0, 0))
    return _pc(body, name=name, out_shape=(_sds((T, D), F32), _sds((1, D), F32)), grid=(T // tm,),
               in_specs=[row, vec, row, row], out_specs=(row, vec), sem=("arbitrary",))(x, w, dxn, dres)


def final_loss(x, w, tgt, name):
    T = x.shape[0]
    tm = _tile(T, 512)

    def body(x_ref, w_ref, t_ref, loss_ref, dx_ref, dw_ref):
        xv, wv = x_ref[...], w_ref[...]
        err = _rms_fwd(xv, wv) - t_ref[...]
        dx, dw = _rms_bwd(xv, wv, err * (1.0 / D))
        dx_ref[...] = dx

        @pl.when(pl.program_id(0) == 0)
        def _():
            dw_ref[...] = jnp.zeros_like(dw_ref)
            loss_ref[...] = jnp.zeros_like(loss_ref)
        dw_ref[...] += dw
        loss_ref[...] += jnp.full((1, 128), 0.5 / D, F32) * jnp.sum(err * err)

    row = pl.BlockSpec((tm, D), lambda i: (i, 0))
    vec = pl.BlockSpec((1, D), lambda i: (0, 0))
    return _pc(body, name=name, out_shape=(_sds((1, 128), F32), _sds((T, D), F32), _sds((1, D), F32)),
               grid=(T // tm,), in_specs=[row, vec, row],
               out_specs=(pl.BlockSpec((1, 128), lambda i: (0, 0)), row, vec), sem=("arbitrary",))(x, w, tgt)


def _col_tile(n):
    for t in (1536, 1408, 1024, 768, 512, 384, 256, 128):
        if n % t == 0:
            return t
    return n


def mm_nn(a, b, name, bias=None, residual=None, out_dtype=F32):
    T, K = a.shape
    N = b.shape[1]
    tm, tn = _tile(T, 512), _col_tile(N)

    def body(a_ref, b_ref, *rest):
        o_ref = rest[-1]
        acc = _dot(a_ref[...].astype(BF16), b_ref[...])
        for extra in rest[:-1]:
            acc = acc + extra[...]
        o_ref[...] = acc.astype(out_dtype)

    in_specs = [pl.BlockSpec((tm, K), lambda j, i: (i, 0)), pl.BlockSpec((K, tn), lambda j, i: (0, j))]
    args = [a, b]
    if bias is not None:
        in_specs.append(pl.BlockSpec((1, tn), lambda j, i: (0, j)))
        args.append(bias)
    if residual is not None:
        in_specs.append(pl.BlockSpec((tm, tn), lambda j, i: (i, j)))
        args.append(residual)
    return _pc(body, name=name, out_shape=_sds((T, N), out_dtype), grid=(N // tn, T // tm), in_specs=in_specs,
               out_specs=pl.BlockSpec((tm, tn), lambda j, i: (i, j)), sem=("parallel", "parallel"))(*args)


def mm_nt(a, b, name, out_dtype=F32):
    T, N = a.shape
    K = b.shape[0]
    tm = _tile(T, 512)

    def body(a_ref, b_ref, o_ref):
        o_ref[...] = _dot(a_ref[...].astype(BF16), b_ref[...], NT).astype(out_dtype)

    return _pc(body, name=name, out_shape=_sds((T, K), out_dtype), grid=(T // tm,),
               in_specs=[pl.BlockSpec((tm, N), lambda i: (i, 0)), pl.BlockSpec((K, N), lambda i: (0, 0))],
               out_specs=pl.BlockSpec((tm, K), lambda i: (i, 0)), sem=("parallel",))(a, b)


def mm_tn(a, b, name):
    T, K = a.shape
    N = b.shape[1]
    tt, tn = _tile(T, 1024), _col_tile(N)

    def body(a_ref, b_ref, o_ref):
        @pl.when(pl.program_id(1) == 0)
        def _():
            o_ref[...] = jnp.zeros_like(o_ref)
        o_ref[...] += _dot(a_ref[...].astype(BF16), b_ref[...].astype(BF16), TN)

    return _pc(body, name=name, out_shape=_sds((K, N), F32), grid=(N // tn, T // tt),
               in_specs=[pl.BlockSpec((tt, K), lambda j, t: (t, 0)), pl.BlockSpec((tt, tn), lambda j, t: (t, j))],
               out_specs=pl.BlockSpec((K, tn), lambda j, t: (0, j)), sem=("parallel", "arbitrary"))(a, b)


def ffn_up(xn, wgu, name):
    T = xn.shape[0]
    tm = _tile(T, 512)

    def body(x_ref, w_ref, gu_ref):
        xv = x_ref[...]
        for j in range(2 * N_FB):
            gu_ref[j] = _dot(xv, w_ref[j]).astype(BF16)

    return _pc(body, name=name, out_shape=_sds((2 * N_FB, T, FB), BF16), grid=(T // tm,),
               in_specs=[pl.BlockSpec((tm, D), lambda i: (i, 0)), pl.BlockSpec((2 * N_FB, D, FB), lambda i: (0, 0, 0))],
               out_specs=pl.BlockSpec((2 * N_FB, tm, FB), lambda i: (0, i, 0)), sem=("parallel",))(xn, wgu)


def ffn_down(gu, wd, x, name):
    T = x.shape[0]
    tm = _tile(T, 512)

    def body(gu_ref, w_ref, x_ref, o_ref):
        acc = jnp.zeros((tm, D), F32)
        for g in range(N_FB):
            gate, up = gu_ref[g].astype(F32), gu_ref[N_FB + g].astype(F32)
            act = (gate * _sigmoid(gate) * up).astype(BF16)
            acc = acc + _dot(act, w_ref[g])
        o_ref[...] = x_ref[...] + 0.5 * acc

    row = pl.BlockSpec((tm, D), lambda i: (i, 0))
    return _pc(body, name=name, out_shape=_sds((T, D), F32), grid=(T // tm,),
               in_specs=[pl.BlockSpec((2 * N_FB, tm, FB), lambda i: (0, i, 0)),
                         pl.BlockSpec((N_FB, FB, D), lambda i: (0, 0, 0)), row],
               out_specs=row, sem=("parallel",))(gu, wd, x)


def ffn_bwd_hidden(dout, wd, gu, name):
    T = dout.shape[0]
    tm = _tile(T, 512)

    def body(d_ref, w_ref, gu_ref, dgu_ref, act_ref):
        dy = (0.5 * d_ref[...]).astype(BF16)
        for g in range(N_FB):
            gate, up = gu_ref[g].astype(F32), gu_ref[N_FB + g].astype(F32)
            sg = _sigmoid(gate)
            silu = gate * sg
            dact = _dot(dy, w_ref[g], NT)
            act_ref[g] = (silu * up).astype(BF16)
            dgu_ref[g] = (dact * up * (sg * (1.0 + gate * (1.0 - sg)))).astype(BF16)
            dgu_ref[N_FB + g] = (dact * silu).astype(BF16)

    return _pc(body, name=name, out_shape=(_sds((2 * N_FB, T, FB), BF16), _sds((N_FB, T, FB), BF16)), grid=(T // tm,),
               in_specs=[pl.BlockSpec((tm, D), lambda i: (i, 0)), pl.BlockSpec((N_FB, FB, D), lambda i: (0, 0, 0)),
                         pl.BlockSpec((2 * N_FB, tm, FB), lambda i: (0, i, 0))],
               out_specs=(pl.BlockSpec((2 * N_FB, tm, FB), lambda i: (0, i, 0)),
                          pl.BlockSpec((N_FB, tm, FB), lambda i: (0, i, 0))), sem=("parallel",))(dout, wd, gu)


def ffn_bwd_input(dgu, wgu, x, dout, nw, name):
    T = x.shape[0]
    tm = _tile(T, 512)

    def body(dgu_ref, w_ref, x_ref, d_ref, nw_ref, dx_ref, dnw_ref):
        dxn = jnp.zeros((tm, D), F32)
        for j in range(2 * N_FB):
            dxn = dxn + _dot(dgu_ref[j], w_ref[j], NT)
        dx, dw = _rms_bwd(x_ref[...], nw_ref[...], dxn)
        dx_ref[...] = d_ref[...] + dx

        @pl.when(pl.program_id(0) == 0)
        def _():
            dnw_ref[...] = jnp.zeros_like(dnw_ref)
        dnw_ref[...] += dw

    row = pl.BlockSpec((tm, D), lambda i: (i, 0))
    vec = pl.BlockSpec((1, D), lambda i: (0, 0))
    return _pc(body, name=name, out_shape=(_sds((T, D), F32), _sds((1, D), F32)), grid=(T // tm,),
               in_specs=[pl.BlockSpec((2 * N_FB, tm, FB), lambda i: (0, i, 0)),
                         pl.BlockSpec((2 * N_FB, D, FB), lambda i: (0, 0, 0)), row, row, vec],
               out_specs=(row, vec), sem=("arbitrary",))(dgu, wgu, x, dout, nw)


def ffn_wgrad_gu(xn, dgu, name):
    T = xn.shape[0]
    tt = _tile(T, 1024)

    def body(x_ref, d_ref, o_ref):
        @pl.when(pl.program_id(1) == 0)
        def _():
            o_ref[...] = jnp.zeros_like(o_ref)
        o_ref[0] += _dot(x_ref[...], d_ref[0], TN)

    return _pc(body, name=name, out_shape=_sds((2 * N_FB, D, FB), F32), grid=(2 * N_FB, T // tt),
               in_specs=[pl.BlockSpec((tt, D), lambda j, t: (t, 0)), pl.BlockSpec((1, tt, FB), lambda j, t: (j, t, 0))],
               out_specs=pl.BlockSpec((1, D, FB), lambda j, t: (j, 0, 0)), sem=("parallel", "arbitrary"))(xn, dgu)


def ffn_wgrad_down(act, dout, name):
    T = dout.shape[0]
    tt = _tile(T, 1024)

    def body(a_ref, d_ref, o_ref):
        @pl.when(pl.program_id(1) == 0)
        def _():
            o_ref[...] = jnp.zeros_like(o_ref)
        o_ref[0] += _dot(a_ref[0], (0.5 * d_ref[...]).astype(BF16), TN)

    return _pc(body, name=name, out_shape=_sds((N_FB, FB, D), F32), grid=(N_FB, T // tt),
               in_specs=[pl.BlockSpec((1, tt, FB), lambda g, t: (g, t, 0)), pl.BlockSpec((tt, D), lambda g, t: (t, 0))],
               out_specs=pl.BlockSpec((1, FB, D), lambda g, t: (g, 0, 0)), sem=("parallel", "arbitrary"))(act, dout)


def ffn_forward(x, nw, wgu, wd, tag):
    xn = rmsnorm_bf16(x, nw, f"{tag}_norm")
    gu = ffn_up(xn, wgu, f"{tag}_up")
    return ffn_down(gu, wd, x, f"{tag}_down"), (x, xn, gu)


def ffn_backward(dout, saved, nw, wgu, wd, tag):
    x, xn, gu = saved
    dgu, act = ffn_bwd_hidden(dout, wd, gu, f"{tag}_bwd_hidden")
    dwd = ffn_wgrad_down(act, dout, f"{tag}_wgrad_down")
    dwgu = ffn_wgrad_gu(xn, dgu, f"{tag}_wgrad_gu")
    dx, dnw = ffn_bwd_input(dgu, wgu, x, dout, nw, f"{tag}_bwd_input")
    return dx, dnw, dwgu, dwd


N_QKV_BLK = 3 * HEADS_A
Z_BLK0 = N_QKV_BLK
BA_BLK = A_COLS // 128 - 1


def _conv_taps(xcat, w, tm):
    c = xcat[8:] * w[3:4]
    for k in range(3):
        c = c + pltpu.roll(xcat, 3 - k, 0)[8:] * w[k:k + 1]
    return c


def gdn_conv_fwd(proj, wconv, name):
    T = proj.shape[0]
    tm = _tile(T, 512)

    def body(cur_ref, prev_ref, w_ref, c_ref, y_ref):
        cb, t = pl.program_id(0), pl.program_id(1)
        prev = jnp.where(t > 0, prev_ref[...], 0.0)
        c = _conv_taps(jnp.concatenate([prev, cur_ref[...]], axis=0), w_ref[...], tm)
        c_ref[...] = c
        s = c * _sigmoid(c)
        r = lax.rsqrt(jnp.sum(s * s, axis=-1, keepdims=True) + EPS)
        y_ref[0] = s * jnp.where(cb < HEADS_A, r * (DK ** -0.5), jnp.where(cb < 2 * HEADS_A, r, 1.0))

    return _pc(body, name=name, out_shape=(_sds((T, 128 * N_QKV_BLK), F32), _sds((N_QKV_BLK, T, 128), F32)),
               grid=(N_QKV_BLK, T // tm),
               in_specs=[pl.BlockSpec((tm, 128), lambda cb, t: (t, cb)),
                         pl.BlockSpec((8, 128), lambda cb, t: (jnp.maximum(t * (tm // 8) - 1, 0), cb)),
                         pl.BlockSpec((4, 128), lambda cb, t: (0, cb))],
               out_specs=(pl.BlockSpec((tm, 128), lambda cb, t: (t, cb)),
                          pl.BlockSpec((1, tm, 128), lambda cb, t: (cb, t, 0))),
               sem=("parallel", "parallel"))(proj, proj, wconv)


def gdn_conv_bwd_act(dqkv, c, name):
    T = c.shape[0]
    tm = _tile(T, 512)

    def body(dy_ref, c_ref, dc_ref):
        cb = pl.program_id(0)
        cv, dy = c_ref[...], dy_ref[0]
        sg = _sigmoid(cv)
        s = cv * sg
        r = lax.rsqrt(jnp.sum(s * s, axis=-1, keepdims=True) + EPS)
        scale = jnp.where(cb < HEADS_A, DK ** -0.5, 1.0)
        ds_norm = scale * r * (dy - (r * r) * s * jnp.sum(dy * s, axis=-1, keepdims=True))
        ds = jnp.where(cb < 2 * HEADS_A, ds_norm, dy)
        dc_ref[...] = ds * (sg * (1.0 + cv * (1.0 - sg)))

    return _pc(body, name=name, out_shape=_sds((T, 128 * N_QKV_BLK), F32), grid=(N_QKV_BLK, T // tm),
               in_specs=[pl.BlockSpec((1, tm, 128), lambda cb, t: (cb, t, 0)), pl.BlockSpec((tm, 128), lambda cb, t: (t, cb))],
               out_specs=pl.BlockSpec((tm, 128), lambda cb, t: (t, cb)), sem=("parallel", "parallel"))(dqkv, c)


def gdn_conv_bwd(dc, proj, wconv, name):
    T = dc.shape[0]
    tm = _tile(T, 512)
    n_t = T // tm

    def body(dcur_ref, dnext_ref, xcur_ref, xprev_ref, w_ref, dx_ref, dw_ref):
        t = pl.program_id(1)
        w = w_ref[...]
        dcur = dcur_ref[...]
        dnext = jnp.where(t < n_t - 1, dnext_ref[...], 0.0)
        dcat = jnp.concatenate([dcur, dnext], axis=0)
        dx = dcur * w[3:4]
        for k in range(3):
            dx = dx + pltpu.roll(dcat, tm + 8 - (3 - k), 0)[:tm] * w[k:k + 1]
        dx_ref[...] = dx.astype(BF16)
        xprev = jnp.where(t > 0, xprev_ref[...], 0.0)
        xcat = jnp.concatenate([xprev, xcur_ref[...]], axis=0)
        rows = [jnp.sum(dcur * pltpu.roll(xcat, 3 - k, 0)[8:], axis=0, keepdims=True) for k in range(3)]
        rows.append(jnp.sum(dcur * xcat[8:], axis=0, keepdims=True))

        @pl.when(t == 0)
        def _():
            dw_ref[...] = jnp.zeros_like(dw_ref)
        dw_ref[...] += jnp.concatenate(rows, axis=0)

    cur = pl.BlockSpec((tm, 128), lambda cb, t: (t, cb))
    return _pc(body, name=name, out_shape=(_sds((T, 128 * N_QKV_BLK), BF16), _sds((4, 128 * N_QKV_BLK), F32)),
               grid=(N_QKV_BLK, n_t),
               in_specs=[cur, pl.BlockSpec((8, 128), lambda cb, t: (jnp.minimum((t + 1) * (tm // 8), T // 8 - 1), cb)),
                         cur, pl.BlockSpec((8, 128), lambda cb, t: (jnp.maximum(t * (tm // 8) - 1, 0), cb)),
                         pl.BlockSpec((4, 128), lambda cb, t: (0, cb))],
               out_specs=(cur, pl.BlockSpec((4, 128), lambda cb, t: (0, cb))),
               sem=("parallel", "arbitrary"))(dc, dc, proj, proj, wconv)


def _chunk_masks(n):
    ri = lax.broadcasted_iota(jnp.int32, (n, n), 0)
    ci = lax.broadcasted_iota(jnp.int32, (n, n), 1)
    same = (ri // CHUNK) == (ci // CHUNK)
    return same & (ri >= ci), same & (ri <= ci)


def gdn_gate_fwd(proj, al, dtb, name):
    T = proj.shape[0]
    tg = _tile(T, PREP_T)

    def body(ba_ref, al_ref, dtb_ref, o_ref):
        x = ba_ref[...]
        lane = lax.broadcasted_iota(jnp.int32, x.shape, 1)
        is_a = (lane >= HEADS_A) & (lane < 2 * HEADS_A)
        g = jnp.where(is_a, -jnp.exp(al_ref[...]) * _softplus(x + dtb_ref[...]), 0.0)
        lower, _ = _chunk_masks(tg)
        gc = _dot(lower.astype(F32), g, precision=HI)
        o_ref[...] = jnp.where(lane < HEADS_A, _sigmoid(x), gc)

    vec = pl.BlockSpec((1, 128), lambda i: (0, 0))
    return _pc(body, name=name, out_shape=_sds((T, 128), F32), grid=(T // tg,),
               in_specs=[pl.BlockSpec((tg, 128), lambda i: (i, BA_BLK)), vec, vec],
               out_specs=pl.BlockSpec((tg, 128), lambda i: (i, 0)), sem=("parallel",))(proj, al, dtb)


def gdn_gate_bwd(proj, al, dtb, dgb, name):
    T = proj.shape[0]
    tg = _tile(T, PREP_T)

    def body(ba_ref, al_ref, dtb_ref, dgb_ref, dba_ref, dal_ref, ddt_ref):
        x, d = ba_ref[...], dgb_ref[...]
        lane = lax.broadcasted_iota(jnp.int32, x.shape, 1)
        is_b = lane < HEADS_A
        is_a = (lane >= HEADS_A) & (lane < 2 * HEADS_A)
        beta = _sigmoid(x)
        e_a = jnp.exp(al_ref[...])
        z = x + dtb_ref[...]
        g = jnp.where(is_a, -e_a * _softplus(z), 0.0)
        _, upper = _chunk_masks(tg)
        dg = _dot(upper.astype(F32), jnp.where(is_a, d, 0.0), precision=HI)
        da = jnp.where(is_a, dg * (-e_a) * _sigmoid(z), 0.0)
        db = jnp.where(is_b, d * beta * (1.0 - beta), 0.0)
        dba_ref[...] = (da + db).astype(BF16)

        @pl.when(pl.program_id(0) == 0)
        def _():
            dal_ref[...] = jnp.zeros_like(dal_ref)
            ddt_ref[...] = jnp.zeros_like(ddt_ref)
        dal_ref[...] += jnp.sum(dg * g, axis=0, keepdims=True)
        ddt_ref[...] += jnp.sum(da, axis=0, keepdims=True)

    vec = pl.BlockSpec((1, 128), lambda i: (0, 0))
    blk = pl.BlockSpec((tg, 128), lambda i: (i, 0))
    return _pc(body, name=name, out_shape=(_sds((T, 128), BF16), _sds((1, 128), F32), _sds((1, 128), F32)),
               grid=(T // tg,), in_specs=[pl.BlockSpec((tg, 128), lambda i: (i, BA_BLK)), vec, vec, blk],
               out_specs=(blk, vec, vec), sem=("arbitrary",))(proj, al, dtb, dgb)


def _bmm(a, b, dims, precision=None):
    return lax.dot_general(a, b, dims, preferred_element_type=F32, precision=precision)


B_NN = (((2,), (1,)), ((0,), (0,)))
B_NT = (((2,), (2,)), ((0,), (0,)))


def _select_lane(x, lane_index):
    lane = lax.broadcasted_iota(jnp.int32, x.shape, x.ndim - 1)
    return jnp.sum(jnp.where(lane == lane_index, x, 0.0), axis=-1, keepdims=True)


def _gdn_prep(q, k, v, gb, h):
    nb = q.shape[0]
    beta = _select_lane(gb, h)
    gc = _select_lane(gb, HEADS_A + h)
    ri = lax.broadcasted_iota(jnp.int32, (nb, CHUNK, CHUNK), 1)
    ci = lax.broadcasted_iota(jnp.int32, (nb, CHUNK, CHUNK), 2)
    lower, strict, eye = ri >= ci, ri > ci, ri == ci
    gcol = jnp.broadcast_to(gc, (nb, CHUNK, CHUNK))
    grow = _bmm(jnp.ones((nb, CHUNK, CHUNK), F32), jnp.where(eye, gcol, 0.0), B_NN, HI)
    decay = jnp.where(lower, jnp.exp(jnp.where(lower, gcol - grow, 0.0)), 0.0)
    kb = k * beta
    kbf = k.astype(BF16)
    lmat = jnp.where(strict, _bmm(kb.astype(BF16), kbf, B_NT) * decay, 0.0)
    inv = jnp.where(eye, 1.0, 0.0) - lmat
    power = lmat
    for _ in range(5):
        power = _bmm(power, power, B_NN, HI)
        inv = inv + _bmm(inv, power, B_NN, HI)
    eg = jnp.exp(gc)
    sol = _bmm(inv, jnp.concatenate([v * beta, kb * eg], axis=-1), B_NN, HI)
    aqk = _bmm(q.astype(BF16), kbf, B_NT) * decay
    g_last = gc[:, CHUNK - 1:CHUNK, :]
    gl = jnp.broadcast_to(jnp.exp(g_last), (nb, 1, 128))
    return sol[..., :DK], sol[..., DK:], q * eg, k * jnp.exp(g_last - gc), aqk, gl


def gdn_prep_fwd(qkv, gb, name):
    T = qkv.shape[1]
    tp = _tile(T, PREP_T)
    nb = tp // CHUNK

    def body(q_ref, k_ref, v_ref, gb_ref, u_ref, w_ref, qd_ref, kd_ref, a_ref, gl_ref):
        h = pl.program_id(1)
        shp = (nb, CHUNK, 128)
        u, w, qd, kd, aqk, gl = _gdn_prep(q_ref[0].reshape(shp), k_ref[0].reshape(shp), v_ref[0].reshape(shp),
                                          gb_ref[...].reshape(shp), h)
        u_ref[0] = u.reshape(tp, 128)
        w_ref[0] = w.reshape(tp, 128)
        qd_ref[0] = qd.reshape(tp, 128)
        kd_ref[0] = kd.reshape(tp, 128)
        a_ref[0] = aqk.reshape(tp, CHUNK)
        gl_ref[0] = gl.reshape(nb, 1, 128)

    def head(off):
        return pl.BlockSpec((1, tp, 128), lambda n, h: (h + off, n, 0))

    per_head = _sds((HEADS_A, T, 128), F32)
    return _pc(body, name=name,
               out_shape=(per_head, per_head, per_head, per_head, _sds((HEADS_A, T, CHUNK), F32),
                          _sds((HEADS_A, T // CHUNK, 1, 128), F32)),
               grid=(T // tp, HEADS_A),
               in_specs=[head(0), head(HEADS_A), head(2 * HEADS_A), pl.BlockSpec((tp, 128), lambda n, h: (n, 0))],
               out_specs=(head(0), head(0), head(0), head(0), pl.BlockSpec((1, tp, CHUNK), lambda n, h: (h, n, 0)),
                          pl.BlockSpec((1, nb, 1, 128), lambda n, h: (h, n, 0, 0))),
               sem=("parallel", "parallel"))(qkv, qkv, qkv, gb)


def gdn_prep_bwd(qkv, gb, du, dw, dqd, dkd, da, dgl, name):
    T = qkv.shape[1]
    tp = _tile(T, PREP_T)
    nb = tp // CHUNK

    def body(q_ref, k_ref, v_ref, gb_ref, du_ref, dw_ref, dqd_ref, dkd_ref, da_ref, dgl_ref, dqkv_ref, dgb_ref):
        h = pl.program_id(1)
        shp = (nb, CHUNK, 128)
        _, vjp = jax.vjp(functools.partial(_gdn_prep, h=h), q_ref[0].reshape(shp), k_ref[0].reshape(shp),
                         v_ref[0].reshape(shp), gb_ref[...].reshape(shp))
        dq, dk, dv, dgb = vjp((du_ref[0].reshape(shp), dw_ref[0].reshape(shp), dqd_ref[0].reshape(shp),
                               dkd_ref[0].reshape(shp), da_ref[0].reshape(nb, CHUNK, CHUNK), dgl_ref[0].reshape(nb, 1, 128)))
        dqkv_ref[h] = dq.reshape(tp, 128)
        dqkv_ref[HEADS_A + h] = dk.reshape(tp, 128)
        dqkv_ref[2 * HEADS_A + h] = dv.reshape(tp, 128)

        @pl.when(h == 0)
        def _():
            dgb_ref[...] = jnp.zeros_like(dgb_ref)
        dgb_ref[...] += dgb.reshape(tp, 128)

    def head(off):
        return pl.BlockSpec((1, tp, 128), lambda n, h: (h + off, n, 0))

    return _pc(body, name=name, out_shape=(_sds((N_QKV_BLK, T, 128), F32), _sds((T, 128), F32)),
               grid=(T // tp, HEADS_A),
               in_specs=[head(0), head(HEADS_A), head(2 * HEADS_A), pl.BlockSpec((tp, 128), lambda n, h: (n, 0)),
                         head(0), head(0), head(0), head(0), pl.BlockSpec((1, tp, CHUNK), lambda n, h: (h, n, 0)),
                         pl.BlockSpec((1, nb, 1, 128), lambda n, h: (h, n, 0, 0))],
               out_specs=(pl.BlockSpec((N_QKV_BLK, tp, 128), lambda n, h: (0, n, 0)),
                          pl.BlockSpec((tp, 128), lambda n, h: (n, 0))),
               sem=("parallel", "arbitrary"))(qkv, qkv, qkv, gb, du, dw, dqd, dkd, da, dgl)


def gdn_scan_fwd(u, w, qd, kd, aqk, gl, name):
    T = u.shape[1]
    n_chunks = T // CHUNK

    def body(u_ref, w_ref, qd_ref, kd_ref, a_ref, gl_ref, o_ref, sin_ref, state):
        @pl.when(pl.program_id(0) == 0)
        def _():
            state[...] = jnp.zeros_like(state)
        for h in range(HEADS_A):
            s = state[h]
            sin_ref[0, h] = s
            sb = s.astype(BF16)
            both = _dot(jnp.concatenate([w_ref[h], qd_ref[h]], axis=0).astype(BF16), sb)
            vn = (u_ref[h] - both[:CHUNK]).astype(BF16)
            o_ref[h] = both[CHUNK:] + _dot(a_ref[h].astype(BF16), vn)
            state[h] = s * gl_ref[h, 0] + _dot(kd_ref[h].astype(BF16), vn, TN)

    blk = pl.BlockSpec((HEADS_A, CHUNK, 128), lambda n: (0, n, 0))
    return _pc(body, name=name,
               out_shape=(_sds((HEADS_A, T, 128), F32), _sds((n_chunks, HEADS_A, DK, 128), F32)), grid=(n_chunks,),
               in_specs=[blk, blk, blk, blk, pl.BlockSpec((HEADS_A, CHUNK, CHUNK), lambda n: (0, n, 0)),
                         pl.BlockSpec((HEADS_A, 1, 1, 128), lambda n: (0, n, 0, 0))],
               out_specs=(blk, pl.BlockSpec((1, HEADS_A, DK, 128), lambda n: (n, 0, 0, 0))),
               scratch=[pltpu.VMEM((HEADS_A, DK, 128), F32)], sem=("arbitrary",))(u, w, qd, kd, aqk, gl)


def gdn_scan_bwd(u, w, qd, kd, aqk, gl, sin, do, name):
    T = u.shape[1]
    n_chunks = T // CHUNK

    def body(u_ref, w_ref, qd_ref, kd_ref, a_ref, gl_ref, sin_ref, do_ref,
             du_ref, dw_ref, dqd_ref, dkd_ref, da_ref, dgl_ref, dstate):
        @pl.when(pl.program_id(0) == 0)
        def _():
            dstate[...] = jnp.zeros_like(dstate)
        lane0 = lax.broadcasted_iota(jnp.int32, (1, 128), 1) == 0
        for h in range(HEADS_A):
            s = sin_ref[0, h]
            sb = s.astype(BF16)
            wb, qdb, kdb = w_ref[h].astype(BF16), qd_ref[h].astype(BF16), kd_ref[h].astype(BF16)
            ab, dob = a_ref[h].astype(BF16), do_ref[h].astype(BF16)
            vn = (u_ref[h] - _dot(wb, sb)).astype(BF16)
            ds_out = dstate[h]
            dsb = ds_out.astype(BF16)
            dqd_ref[h] = _dot(dob, sb, NT)
            da_ref[h] = _dot(dob, vn, NT)
            dv = _dot(ab, dob, TN) + _dot(kdb, dsb)
            dkd_ref[h] = _dot(vn, dsb, NT)
            dgl_ref[h, 0] = jnp.where(lane0, jnp.sum(ds_out * s), 0.0)
            du_ref[h] = dv
            dvb = dv.astype(BF16)
            dw_ref[h] = -_dot(dvb, sb, NT)
            dstate[h] = ds_out * gl_ref[h, 0] + _dot(qdb, dob, TN) - _dot(wb, dvb, TN)

    last = n_chunks - 1
    blk = pl.BlockSpec((HEADS_A, CHUNK, 128), lambda n: (0, last - n, 0))
    ablk = pl.BlockSpec((HEADS_A, CHUNK, CHUNK), lambda n: (0, last - n, 0))
    glblk = pl.BlockSpec((HEADS_A, 1, 1, 128), lambda n: (0, last - n, 0, 0))
    per_head = _sds((HEADS_A, T, 128), F32)
    return _pc(body, name=name,
               out_shape=(per_head, per_head, per_head, per_head, _sds((HEADS_A, T, CHUNK), F32),
                          _sds((HEADS_A, n_chunks, 1, 128), F32)), grid=(n_chunks,),
               in_specs=[blk, blk, blk, blk, ablk, glblk,
                         pl.BlockSpec((1, HEADS_A, DK, 128), lambda n: (last - n, 0, 0, 0)), blk],
               out_specs=(blk, blk, blk, blk, ablk, glblk),
               scratch=[pltpu.VMEM((HEADS_A, DK, 128), F32)], sem=("arbitrary",))(u, w, qd, kd, aqk, gl, sin, do)


def gdn_outnorm_fwd(o, proj, wn, name):
    T = o.shape[1]
    tm = _tile(T, 512)

    def body(o_ref, z_ref, wn_ref, y_ref):
        for h in range(HEADS_A):
            z = z_ref[:, 128 * h:128 * (h + 1)]
            y_ref[:, 128 * h:128 * (h + 1)] = (_rms_fwd(o_ref[h], wn_ref[...]) * (z * _sigmoid(z))).astype(BF16)

    return _pc(body, name=name, out_shape=_sds((T, D), BF16), grid=(T // tm,),
               in_specs=[pl.BlockSpec((HEADS_A, tm, 128), lambda i: (0, i, 0)),
                         pl.BlockSpec((tm, D), lambda i: (i, Z_BLK0 * 128 // D)), pl.BlockSpec((1, 128), lambda i: (0, 0))],
               out_specs=pl.BlockSpec((tm, D), lambda i: (i, 0)), sem=("parallel",))(o, proj, wn)


def gdn_outnorm_bwd(o, proj, wn, dy, name):
    T = o.shape[1]
    tm = _tile(T, 512)

    def body(o_ref, z_ref, wn_ref, dy_ref, do_ref, dz_ref, dwn_ref):
        wn = wn_ref[...]
        acc = jnp.zeros((1, 128), F32)
        for h in range(HEADS_A):
            cols = slice(128 * h, 128 * (h + 1))
            z, dyh, ov = z_ref[:, cols], dy_ref[:, cols], o_ref[h]
            sg = _sigmoid(z)
            do, dwn = _rms_bwd(ov, wn, dyh * (z * sg))
            do_ref[h] = do
            acc = acc + dwn
            dz_ref[:, cols] = (dyh * _rms_fwd(ov, wn) * (sg * (1.0 + z * (1.0 - sg)))).astype(BF16)

        @pl.when(pl.program_id(0) == 0)
        def _():
            dwn_ref[...] = jnp.zeros_like(dwn_ref)
        dwn_ref[...] += acc

    row = pl.BlockSpec((tm, D), lambda i: (i, 0))
    vec = pl.BlockSpec((1, 128), lambda i: (0, 0))
    hblk = pl.BlockSpec((HEADS_A, tm, 128), lambda i: (0, i, 0))
    return _pc(body, name=name, out_shape=(_sds((HEADS_A, T, 128), F32), _sds((T, D), BF16), _sds((1, 128), F32)),
               grid=(T // tm,),
               in_specs=[hblk, pl.BlockSpec((tm, D), lambda i: (i, Z_BLK0 * 128 // D)), vec, row],
               out_specs=(hblk, row, vec), sem=("arbitrary",))(o, proj, wn, dy)


def gdn_forward(x, nw, w_in, wconv, al, dtb, wn, w_out, tag):
    h = rmsnorm_bf16(x, nw, f"{tag}_norm")
    proj = mm_nn(h, w_in, f"{tag}_proj")
    c, qkv = gdn_conv_fwd(proj, wconv, f"{tag}_conv")
    gb = gdn_gate_fwd(proj, al, dtb, f"{tag}_gate")
    u, w, qd, kd, aqk, gl = gdn_prep_fwd(qkv, gb, f"{tag}_prep")
    o, sin = gdn_scan_fwd(u, w, qd, kd, aqk, gl, f"{tag}_scan")
    on = gdn_outnorm_fwd(o, proj, wn, f"{tag}_outnorm")
    y = mm_nn(on, w_out, f"{tag}_out", residual=x)
    return y, (x, h, proj, c, qkv, gb, (u, w, qd, kd, aqk, gl), sin, o, on)


def gdn_backward(dout, saved, nw, w_in, wconv, al, dtb, wn, w_out, tag):
    x, h, proj, c, qkv, gb, prep, sin, o, on = saved
    d_on = mm_nt(dout, w_out, f"{tag}_out_bwd")
    dw_out = mm_tn(on, dout, f"{tag}_out_wgrad")
    do, dz, dwn = gdn_outnorm_bwd(o, proj, wn, d_on, f"{tag}_outnorm_bwd")
    du, dw, dqd, dkd, da, dgl = gdn_scan_bwd(*prep, sin, do, f"{tag}_scan_bwd")
    dqkv, dgb = gdn_prep_bwd(qkv, gb, du, dw, dqd, dkd, da, dgl, f"{tag}_prep_bwd")
    dba, dal, ddt = gdn_gate_bwd(proj, al, dtb, dgb, f"{tag}_gate_bwd")
    dc = gdn_conv_bwd_act(dqkv, c, f"{tag}_conv_bwd_act")
    dpre, dwconv = gdn_conv_bwd(dc, proj, wconv, f"{tag}_conv_bwd")
    dproj = jnp.concatenate([dpre, dz, dba], axis=1)
    dw_in = mm_tn(h, dproj, f"{tag}_proj_wgrad")
    dh = mm_nt(dproj, w_in, f"{tag}_proj_bwd")
    dx, dnw = rmsnorm_bwd_add(x, nw, dh, dout, f"{tag}_norm_bwd")
    return dx, dnw, dw_in, dwconv, dal, ddt, dwn, dw_out


N_KV, GROUP = 4, 4
KV_COLS = 2 * N_KV * B_HD
B_COLS = D + KV_COLS


def _swa_block(q, kp, kc, vp, vc, sk, first):
    kk = jnp.concatenate([kp, kc], axis=0)
    vv = jnp.concatenate([vp, vc], axis=0)
    rows = GROUP * B_BLK
    qi = lax.broadcasted_iota(jnp.int32, (rows, 2 * B_BLK), 0) % B_BLK
    kj = lax.broadcasted_iota(jnp.int32, (rows, 2 * B_BLK), 1)
    rel = qi + B_BLK - kj
    valid = (rel >= 0) & (rel < B_BLK) & (jnp.logical_not(first) | (kj >= B_BLK))
    outs = []
    for j in range(N_KV):
        heads = range(GROUP * j, GROUP * (j + 1))
        qs = jnp.concatenate([q[:, hq * B_HD:(hq + 1) * B_HD] for hq in heads], axis=0).astype(BF16)
        s = _dot(qs, kk[:, j * B_HD:(j + 1) * B_HD].astype(BF16), NT) * (B_HD ** -0.5)
        s = jnp.where(valid, s, -1e30)
        sink = jnp.concatenate([jnp.broadcast_to(sk[:, hq:hq + 1], (B_BLK, 1)) for hq in heads], axis=0)
        m = lax.stop_gradient(jnp.maximum(jnp.max(s, axis=-1, keepdims=True), sink))
        p = jnp.exp(s - m)
        den = jnp.sum(p, axis=-1, keepdims=True) + jnp.exp(sink - m)
        o = _dot((p / den).astype(BF16), vv[:, j * B_HD:(j + 1) * B_HD].astype(BF16))
        outs += [o[g * B_BLK:(g + 1) * B_BLK] for g in range(GROUP)]
    return jnp.concatenate(outs, axis=1)


def swa_core_fwd(proj, sk, name):
    T = proj.shape[0]
    half = N_KV * B_HD

    def body(q_ref, kvc_ref, kvp_ref, sk_ref, o_ref):
        kvc, kvp = kvc_ref[...], kvp_ref[...]
        o_ref[...] = _swa_block(q_ref[...], kvp[:, :half], kvc[:, :half], kvp[:, half:], kvc[:, half:], sk_ref[...],
                                pl.program_id(0) == 0).astype(BF16)

    return _pc(body, name=name, out_shape=_sds((T, D), BF16), grid=(T // B_BLK,),
               in_specs=[pl.BlockSpec((B_BLK, D), lambda n: (n, 0)),
                         pl.BlockSpec((B_BLK, KV_COLS), lambda n: (n, D // KV_COLS)),
                         pl.BlockSpec((B_BLK, KV_COLS), lambda n: (jnp.maximum(n - 1, 0), D // KV_COLS)),
                         pl.BlockSpec((1, 128), lambda n: (0, 0))],
               out_specs=pl.BlockSpec((B_BLK, D), lambda n: (n, 0)), sem=("parallel",))(proj, proj, proj, sk)


def swa_core_bwd(proj, sk, do, name):
    T = proj.shape[0]
    last = T // B_BLK - 1
    half = N_KV * B_HD

    def body(q_ref, kvc_ref, kvp_ref, sk_ref, do_ref, dproj_ref, dbias_ref, dsk_ref, carry):
        step = pl.program_id(0)
        first = step == last

        @pl.when(step == 0)
        def _():
            carry[...] = jnp.zeros_like(carry)
            dbias_ref[...] = jnp.zeros_like(dbias_ref)
            dsk_ref[...] = jnp.zeros_like(dsk_ref)
        kvc, kvp = kvc_ref[...], kvp_ref[...]
        _, vjp = jax.vjp(functools.partial(_swa_block, first=first), q_ref[...], kvp[:, :half], kvc[:, :half],
                         kvp[:, half:], kvc[:, half:], sk_ref[...])
        dq, dkp, dkc, dvp, dvc, dsk = vjp(do_ref[...])
        dkv = jnp.concatenate([dkc, dvc], axis=1) + carry[...]
        carry[...] = jnp.concatenate([dkp, dvp], axis=1)
        row = jnp.concatenate([dq, dkv], axis=1)
        dproj_ref[...] = row.astype(BF16)
        dbias_ref[...] += jnp.sum(row, axis=0, keepdims=True)
        dsk_ref[...] += dsk

    return _pc(body, name=name, out_shape=(_sds((T, B_COLS), BF16), _sds((1, B_COLS), F32), _sds((1, 128), F32)),
               grid=(T // B_BLK,),
               in_specs=[pl.BlockSpec((B_BLK, D), lambda n: (last - n, 0)),
                         pl.BlockSpec((B_BLK, KV_COLS), lambda n: (last - n, D // KV_COLS)),
                         pl.BlockSpec((B_BLK, KV_COLS), lambda n: (jnp.maximum(last - n - 1, 0), D // KV_COLS)),
                         pl.BlockSpec((1, 128), lambda n: (0, 0)), pl.BlockSpec((B_BLK, D), lambda n: (last - n, 0))],
               out_specs=(pl.BlockSpec((B_BLK, B_COLS), lambda n: (last - n, 0)),
                          pl.BlockSpec((1, B_COLS), lambda n: (0, 0)), pl.BlockSpec((1, 128), lambda n: (0, 0))),
               scratch=[pltpu.VMEM((B_BLK, KV_COLS), F32)], sem=("arbitrary",))(proj, proj, proj, sk, do)


def col_sum(a, name):
    T, N = a.shape
    tm = _tile(T, 1024)

    def body(a_ref, o_ref):
        @pl.when(pl.program_id(0) == 0)
        def _():
            o_ref[...] = jnp.zeros_like(o_ref)
        o_ref[...] += jnp.sum(a_ref[...].astype(F32), axis=0, keepdims=True)

    return _pc(body, name=name, out_shape=_sds((1, N), F32), grid=(T // tm,),
               in_specs=[pl.BlockSpec((tm, N), lambda i: (i, 0))], out_specs=pl.BlockSpec((1, N), lambda i: (0, 0)),
               sem=("arbitrary",))(a)


def swa_forward(x, nw, w_in, b_in, sk, w_out, b_out, tag):
    h = rmsnorm_bf16(x, nw, f"{tag}_norm")
    proj = mm_nn(h, w_in, f"{tag}_proj", bias=b_in)
    o = swa_core_fwd(proj, sk, f"{tag}_core")
    y = mm_nn(o, w_out, f"{tag}_out", bias=b_out, residual=x)
    return y, (x, h, proj, o)


def swa_backward(dout, saved, nw, w_in, b_in, sk, w_out, b_out, tag):
    x, h, proj, o = saved
    do = mm_nt(dout, w_out, f"{tag}_out_bwd")
    dw_out = mm_tn(o, dout, f"{tag}_out_wgrad")
    db_out = col_sum(dout, f"{tag}_out_bias_grad")
    dproj, db_in, dsk = swa_core_bwd(proj, sk, do, f"{tag}_core_bwd")
    dw_in = mm_tn(h, dproj, f"{tag}_proj_wgrad")
    dh = mm_nt(dproj, w_in, f"{tag}_proj_bwd")
    dx, dnw = rmsnorm_bwd_add(x, nw, dh, dout, f"{tag}_norm_bwd")
    return dx, dnw, dw_in, db_in, dsk, dw_out, db_out


MESH = pl.DeviceIdType.MESH
IN_HBM = pl.BlockSpec(memory_space=pl.ANY)


def _position():
    return lax.axis_index("x"), lax.axis_index("y"), lax.axis_index("c")


def _slot(x, y, c):
    return 4 * x + 2 * y + c


def _peer(x, y, c, k):
    return (1 - x if k & 4 else x, 1 - y if k & 2 else y, 1 - c if k & 1 else c)


def all_gather(shards, name):
    n = len(shards)

    def body(*refs):
        ins, outs = refs[:n], refs[n:2 * n]
        send_sems, recv_sems, local_sems = refs[2 * n:]
        x, y, c = _position()
        me, sibling = (x, y, c), (x, y, 1 - c)
        chips = [(1 - x, y), (x, 1 - y), (1 - x, 1 - y)]

        def copy(a, k, block, to, src=None):
            dst = outs[a].at[_slot(*block)]
            return pltpu.make_async_remote_copy(src_ref=dst if src is None else src, dst_ref=dst,
                                                send_sem=send_sems.at[a, k], recv_sem=recv_sems.at[a, k],
                                                device_id=to, device_id_type=MESH)

        mine = [pltpu.make_async_copy(ins[a], outs[a].at[_slot(*me)], local_sems.at[a]) for a in range(n)]
        for cp in mine:
            cp.start()
        first = []
        for a in range(n):
            first.append(copy(a, 0, me, sibling, src=ins[a]))
            first += [copy(a, 1 + j, me, (*chip, c), src=ins[a]) for j, chip in enumerate(chips)]
        for cp in first:
            cp.start()
        passed = []
        for j, chip in enumerate(chips):
            for a in range(n):
                copy(a, 1 + j, (*chip, c), me).wait_recv()
                passed.append(copy(a, 4 + j, (*chip, c), sibling))
                passed[-1].start()
        for a in range(n):
            copy(a, 0, sibling, me).wait_recv()
            for j, chip in enumerate(chips):
                copy(a, 4 + j, (*chip, 1 - c), me).wait_recv()
        for cp in first + passed:
            cp.wait_send()
        for cp in mine:
            cp.wait()

    return _pc(body, name=name, out_shape=[_sds((N_DEV,) + s.shape, s.dtype) for s in shards],
               in_specs=[IN_HBM] * n, out_specs=[IN_HBM] * n,
               scratch=[pltpu.SemaphoreType.DMA((n, 7)), pltpu.SemaphoreType.DMA((n, 7)), pltpu.SemaphoreType.DMA((n,))],
               )(*shards)


def exchange_partials(parts, name):
    n = len(parts)

    def body(*refs):
        ins, outs = refs[:n], refs[n:2 * n]
        send_sems, recv_sems, local_sems = refs[2 * n:]
        x, y, c = _position()
        me = _slot(x, y, c)
        local = [pltpu.make_async_copy(ins[a].at[me], outs[a].at[me], local_sems.at[a]) for a in range(n)]
        for cp in local:
            cp.start()
        copies = []
        for k in (1, 2, 4, 3, 5, 6, 7):
            peer = _peer(x, y, c, k)
            for a in range(n):
                copies.append(pltpu.make_async_remote_copy(
                    src_ref=ins[a].at[_slot(*peer)], dst_ref=outs[a].at[me], send_sem=send_sems.at[a, k - 1],
                    recv_sem=recv_sems.at[a, k - 1], device_id=peer, device_id_type=MESH))
                copies[-1].start()
        for cp in copies:
            cp.wait()
        for cp in local:
            cp.wait()

    return _pc(body, name=name, out_shape=[_sds(p.shape, p.dtype) for p in parts],
               in_specs=[IN_HBM] * n, out_specs=[IN_HBM] * n,
               scratch=[pltpu.SemaphoreType.DMA((n, 7)), pltpu.SemaphoreType.DMA((n, 7)), pltpu.SemaphoreType.DMA((n,))],
               )(*parts)


def all_reduce_small(part, name):
    R, C = part.shape

    def body(p_ref, o_ref, buf, send_sems, recv_sems):
        x, y, c = _position()
        me = _slot(x, y, c)
        copies = []
        for k in range(1, N_DEV):
            copies.append(pltpu.make_async_remote_copy(
                src_ref=p_ref, dst_ref=buf.at[me], send_sem=send_sems.at[k - 1], recv_sem=recv_sems.at[k - 1],
                device_id=_peer(x, y, c, k), device_id_type=MESH))
            copies[-1].start()
        buf[me] = p_ref[...]
        for cp in copies:
            cp.wait()
        acc = buf[0]
        for s in range(1, N_DEV):
            acc = acc + buf[s]
        o_ref[...] = acc

    vmem = pl.BlockSpec(memory_space=pltpu.VMEM)
    return _pc(body, name=name, out_shape=_sds((R, C), F32), in_specs=[vmem], out_specs=vmem,
               scratch=[pltpu.VMEM((N_DEV, R, C), F32), pltpu.SemaphoreType.DMA((7,)), pltpu.SemaphoreType.DMA((7,))],
               )(part)


def _row_tile(rows, cols):
    best = rows
    for t in range(16, rows, 16):
        if rows % t == 0 and t * cols * 4 <= (1 << 20):
            best = t
    return best


def adam_update(parts, w, m, v, name):
    P, R, C = parts.shape
    tr = _row_tile(R, C)

    def body(p_ref, w_ref, m_ref, v_ref, g_ref, d_ref, nm_ref, nv_ref):
        g = p_ref[0].astype(F32)
        for s in range(1, P):
            g = g + p_ref[s].astype(F32)
        new_m = ADAM_B1 * m_ref[...] + (1.0 - ADAM_B1) * g
        new_v = ADAM_B2 * v_ref[...] + (1.0 - ADAM_B2) * (g * g)
        m_hat = new_m / (1.0 - ADAM_B1 ** ADAM_STEP)
        v_hat = new_v / (1.0 - ADAM_B2 ** ADAM_STEP)
        g_ref[...] = g
        d_ref[...] = -ADAM_LR * (m_hat / (jnp.sqrt(v_hat) + ADAM_EPS) + ADAM_WD * w_ref[...])
        nm_ref[...] = new_m
        nv_ref[...] = new_v

    blk = pl.BlockSpec((tr, C), lambda i: (i, 0))
    out = _sds((R, C), F32)
    return _pc(body, name=name, out_shape=(out, out, out, out), grid=(R // tr,),
               in_specs=[pl.BlockSpec((P, tr, C), lambda i: (0, i, 0)), blk, blk, blk],
               out_specs=(blk, blk, blk, blk), sem=("parallel",))(parts, w, m, v)


WEIGHTS = ("ffn1_norm", "ffn1_w_gu", "ffn1_w_down", "mix_norm", "ffn2_norm", "ffn2_w_gu", "ffn2_w_down", "a_w_in",
           "a_w_conv", "a_A_log", "a_dt_bias", "a_out_norm", "a_w_out", "b_w_in", "b_b_in", "b_sinks", "b_w_out",
           "b_b_out", "final_norm")
SHARDED = ("ffn1_w_gu", "ffn1_w_down", "ffn2_w_gu", "ffn2_w_down", "a_w_in", "a_w_conv", "a_w_out", "b_w_in", "b_b_in",
           "b_w_out", "b_b_out")
SENT_AS_BF16 = ("ffn1_w_gu", "ffn1_w_down", "ffn2_w_gu", "ffn2_w_down", "a_w_in", "a_w_out", "b_w_in", "b_w_out")
MISC_LANES = dict(a_A_log=(0, 8), a_dt_bias=(8, 16), b_sinks=(16, 32), a_out_norm=(128, 256))
LOSS_LANE = 256


def _pack_small(t):
    misc = jnp.zeros((D,), F32)
    for key, (lo, hi) in MISC_LANES.items():
        misc = misc.at[lo:hi].set(t[key].reshape(-1))
    if "loss" in t:
        misc = misc.at[LOSS_LANE].set(t["loss"])
    return jnp.concatenate([t["ffn1_norm"], t["mix_norm"], t["ffn2_norm"], t["final_norm"].reshape(1, D), misc[None]], axis=0)


def _unpack_small(p, like):
    out = dict(ffn1_norm=p[0:2], mix_norm=p[2:4], ffn2_norm=p[4:6], final_norm=p[6])
    for key, (lo, hi) in MISC_LANES.items():
        out[key] = p[7, lo:hi].reshape(like[key].shape)
    return out


def kernel(x, ffn1_norm, ffn1_w_gu, ffn1_w_down, mix_norm, ffn2_norm, ffn2_w_gu, ffn2_w_down, a_w_in, a_w_conv, a_A_log, a_dt_bias, a_out_norm, a_w_out, b_w_in, b_b_in, b_sinks, b_w_out, b_b_out, final_norm, loss_target, m_ffn1_norm, m_ffn1_w_gu, m_ffn1_w_down, m_mix_norm, m_ffn2_norm, m_ffn2_w_gu, m_ffn2_w_down, m_a_w_in, m_a_w_conv, m_a_A_log, m_a_dt_bias, m_a_out_norm, m_a_w_out, m_b_w_in, m_b_b_in, m_b_sinks, m_b_w_out, m_b_b_out, m_final_norm, v_ffn1_norm, v_ffn1_w_gu, v_ffn1_w_down, v_mix_norm, v_ffn2_norm, v_ffn2_w_gu, v_ffn2_w_down, v_a_w_in, v_a_w_conv, v_a_A_log, v_a_dt_bias, v_a_out_norm, v_a_w_out, v_b_w_in, v_b_b_in, v_b_sinks, v_b_w_out, v_b_b_out, v_final_norm):
    w = dict(ffn1_norm=ffn1_norm, ffn1_w_gu=ffn1_w_gu, ffn1_w_down=ffn1_w_down, mix_norm=mix_norm, ffn2_norm=ffn2_norm, ffn2_w_gu=ffn2_w_gu, ffn2_w_down=ffn2_w_down, a_w_in=a_w_in, a_w_conv=a_w_conv, a_A_log=a_A_log, a_dt_bias=a_dt_bias, a_out_norm=a_out_norm, a_w_out=a_w_out, b_w_in=b_w_in, b_b_in=b_b_in, b_sinks=b_sinks, b_w_out=b_w_out, b_b_out=b_b_out, final_norm=final_norm)
    m = dict(ffn1_norm=m_ffn1_norm, ffn1_w_gu=m_ffn1_w_gu, ffn1_w_down=m_ffn1_w_down, mix_norm=m_mix_norm, ffn2_norm=m_ffn2_norm, ffn2_w_gu=m_ffn2_w_gu, ffn2_w_down=m_ffn2_w_down, a_w_in=m_a_w_in, a_w_conv=m_a_w_conv, a_A_log=m_a_A_log, a_dt_bias=m_a_dt_bias, a_out_norm=m_a_out_norm, a_w_out=m_a_w_out, b_w_in=m_b_w_in, b_b_in=m_b_b_in, b_sinks=m_b_sinks, b_w_out=m_b_w_out, b_b_out=m_b_b_out, final_norm=m_final_norm)
    v = dict(ffn1_norm=v_ffn1_norm, ffn1_w_gu=v_ffn1_w_gu, ffn1_w_down=v_ffn1_w_down, mix_norm=v_mix_norm, ffn2_norm=v_ffn2_norm, ffn2_w_gu=v_ffn2_w_gu, ffn2_w_down=v_ffn2_w_down, a_w_in=v_a_w_in, a_w_conv=v_a_w_conv, a_A_log=v_a_A_log, a_dt_bias=v_a_dt_bias, a_out_norm=v_a_out_norm, a_w_out=v_a_w_out, b_w_in=v_b_w_in, b_b_in=v_b_b_in, b_sinks=v_b_sinks, b_w_out=v_b_w_out, b_b_out=v_b_b_out, final_norm=v_final_norm)
    T = x.shape[1]
    x0, tgt = x.reshape(T, D), loss_target.reshape(T, D)

    shard_list = []
    for key in ("ffn1_w_gu", "ffn1_w_down", "ffn2_w_gu", "ffn2_w_down"):
        shard_list += [w[key][0].astype(BF16), w[key][1].astype(BF16)]
    shard_list += [a_w_in[0].astype(BF16), a_w_conv[0], a_w_out[0].astype(BF16), b_w_in[0].astype(BF16), b_b_in,
                   b_w_out[0].astype(BF16), b_b_out]
    g = all_gather(shard_list, "gather_weights")
    wgu = {("ffn1", 0): g[0], ("ffn1", 1): g[1], ("ffn2", 0): g[4], ("ffn2", 1): g[5]}
    wdn = {("ffn1", 0): g[2], ("ffn1", 1): g[3], ("ffn2", 0): g[6], ("ffn2", 1): g[7]}
    wdn = {k_: t.reshape(N_FB, FB, D) for k_, t in wdn.items()}
    a_in_cols = a_w_in.shape[-1] * N_DEV
    a_in_full = jnp.pad(g[8].transpose(1, 0, 2).reshape(D, a_in_cols), ((0, 0), (0, A_COLS - a_in_cols)))
    a_conv_full = g[9].transpose(1, 0, 2).reshape(4, 128 * N_QKV_BLK)
    a_out_full = g[10].reshape(D, D)
    b_in_full = g[11].transpose(1, 0, 2).reshape(D, B_COLS)
    b_bias_in = g[12].reshape(1, B_COLS)
    b_out_full = g[13].reshape(D, D)
    b_bias_out = g[14].reshape(1, D)
    a_log_row = jnp.zeros((1, 128), F32).at[0, HEADS_A:2 * HEADS_A].set(a_A_log[0])
    dt_row = jnp.zeros((1, 128), F32).at[0, HEADS_A:2 * HEADS_A].set(a_dt_bias[0])
    sink_row = jnp.zeros((1, 128), F32).at[0, :b_sinks.shape[1]].set(b_sinks[0])
    gdn_args = (mix_norm[0:1], a_in_full, a_conv_full, a_log_row, dt_row, a_out_norm, a_out_full)
    swa_args = (mix_norm[1:2], b_in_full, b_bias_in, sink_row, b_out_full, b_bias_out)

    xs, saved = x0, []
    for layer in range(2):
        xs, s1 = ffn_forward(xs, ffn1_norm[layer:layer + 1], wgu["ffn1", layer], wdn["ffn1", layer], f"l{layer}_ffn1")
        if layer == 0:
            xs, sm = gdn_forward(xs, *gdn_args, "gdn")
        else:
            xs, sm = swa_forward(xs, *swa_args, "swa")
        xs, s2 = ffn_forward(xs, ffn2_norm[layer:layer + 1], wgu["ffn2", layer], wdn["ffn2", layer], f"l{layer}_ffn2")
        saved.append((s1, sm, s2))
    loss_row, dx, d_final_norm = final_loss(xs, final_norm.reshape(1, D), tgt, "final_loss")

    gw = {}
    d_norm = {"ffn1_norm": [None, None], "mix_norm": [None, None], "ffn2_norm": [None, None]}
    d_gu = {"ffn1": [None, None], "ffn2": [None, None]}
    d_dn = {"ffn1": [None, None], "ffn2": [None, None]}
    for layer in (1, 0):
        s1, sm, s2 = saved[layer]
        dx, d_norm["ffn2_norm"][layer], d_gu["ffn2"][layer], d_dn["ffn2"][layer] = ffn_backward(
            dx, s2, ffn2_norm[layer:layer + 1], wgu["ffn2", layer], wdn["ffn2", layer], f"l{layer}_ffn2")
        if layer == 0:
            dx, d_norm["mix_norm"][0], d_a_in, d_a_conv, d_alog, d_dt, d_onorm, d_a_out = gdn_backward(dx, sm, *gdn_args, "gdn")
        else:
            dx, d_norm["mix_norm"][1], d_b_in, d_b_bias_in, d_sinks, d_b_out, d_b_bias_out = swa_backward(dx, sm, *swa_args, "swa")
        dx, d_norm["ffn1_norm"][layer], d_gu["ffn1"][layer], d_dn["ffn1"][layer] = ffn_backward(
            dx, s1, ffn1_norm[layer:layer + 1], wgu["ffn1", layer], wdn["ffn1", layer], f"l{layer}_ffn1")
    grad_x = dx.reshape(x.shape)

    part = {}
    for f in ("ffn1", "ffn2"):
        part[f + "_w_gu"] = jnp.stack(d_gu[f], axis=1)
        part[f + "_w_down"] = jnp.stack([t.reshape(N_DEV, FB // 2, D) for t in d_dn[f]], axis=1)
    part["a_w_in"] = d_a_in[:, :a_in_cols].reshape(D, N_DEV, -1).transpose(1, 0, 2)[:, None]
    part["a_w_conv"] = d_a_conv.reshape(4, N_DEV, -1).transpose(1, 0, 2)[:, None]
    part["a_w_out"] = d_a_out.reshape(N_DEV, 1, D // N_DEV, D)
    part["b_w_in"] = d_b_in.reshape(D, N_DEV, -1).transpose(1, 0, 2)[:, None]
    part["b_b_in"] = d_b_bias_in.reshape(N_DEV, 1, -1)
    part["b_w_out"] = d_b_out.reshape(N_DEV, 1, D // N_DEV, D)
    part["b_b_out"] = d_b_bias_out.reshape(N_DEV, 1, -1)
    send = [part[key].astype(BF16) if key in SENT_AS_BF16 else part[key] for key in SHARDED]
    recv = exchange_partials(send, "exchange_grads")

    grads, deltas, new_m, new_v = {}, {}, {}, {}
    for key, r in zip(SHARDED, recv):
        shape = w[key].shape
        cols = shape[-1]
        two_d = lambda t: t.reshape(-1, cols)
        out = adam_update(r.reshape(N_DEV, -1, cols), two_d(w[key]), two_d(m[key]), two_d(v[key]), f"adam_{key}")
        grads[key], deltas[key], new_m[key], new_v[key] = (t.reshape(shape) for t in out)

    small = dict(ffn1_norm=jnp.concatenate(d_norm["ffn1_norm"], axis=0), mix_norm=jnp.concatenate(d_norm["mix_norm"], axis=0),
                 ffn2_norm=jnp.concatenate(d_norm["ffn2_norm"], axis=0), final_norm=d_final_norm,
                 a_A_log=d_alog[0, HEADS_A:2 * HEADS_A], a_dt_bias=d_dt[0, HEADS_A:2 * HEADS_A],
                 b_sinks=d_sinks[0, :b_sinks.shape[1]], a_out_norm=d_onorm, loss=loss_row[0, 0])
    total = all_reduce_small(_pack_small(small), "allreduce_small")
    out = adam_update(total[None], _pack_small(w), _pack_small(m), _pack_small(v), "adam_small")
    for dst, packed in zip((grads, deltas, new_m, new_v), out):
        dst.update(_unpack_small(packed, w))
    loss = total[7, LOSS_LANE]

    return (loss, grad_x, *[grads[k_] for k_ in WEIGHTS], *[deltas[k_] for k_ in WEIGHTS],
            *[new_m[k_] for k_ in WEIGHTS], *[new_v[k_] for k_ in WEIGHTS])
```

```python
import functools

import jax
import jax.numpy as jnp
from jax import lax
from jax.experimental import pallas as pl
from jax.experimental.pallas import tpu as pltpu

F32, BF16 = jnp.float32, jnp.bfloat16
HI = lax.Precision.HIGHEST
EPS = 1e-6

N_DEV = 8
D = 1024
FB = 704
N_FB = 4
HEADS_A, DK = 8, 128
CHUNK = 64
PREP_T = 512
A_COLS = 4224
B_HD, B_BLK = 64, 128
VMEM_LIMIT_V7X = 60 * 1024 * 1024

ADAM_LR, ADAM_B1, ADAM_B2, ADAM_EPS, ADAM_WD, ADAM_STEP = 0.001, 0.9, 0.999, 1e-08, 0.01, 10

NT = (((1,), (1,)), ((), ()))
TN = (((0,), (0,)), ((), ()))


def _pc(body, *, name, out_shape, grid=(), in_specs=None, out_specs=None, scratch=(), sem=None, **kw):
    params = pltpu.CompilerParams(dimension_semantics=sem, vmem_limit_bytes=VMEM_LIMIT_V7X)
    return pl.pallas_call(body, name=name, out_shape=out_shape, grid=grid, in_specs=in_specs, out_specs=out_specs,
                          scratch_shapes=list(scratch), compiler_params=params, **kw)


def _sds(shape, dtype):
    return jax.ShapeDtypeStruct(tuple(shape), dtype)


def _dot(a, b, dims=None, precision=None):
    if dims is None:
        return jnp.dot(a, b, preferred_element_type=F32, precision=precision)
    return lax.dot_general(a, b, dims, preferred_element_type=F32, precision=precision)


def _sigmoid(x):
    return 1.0 / (1.0 + jnp.exp(-x))


def _softplus(x):
    return jnp.maximum(x, 0.0) + jnp.log(1.0 + jnp.exp(-jnp.abs(x)))


def _rms_fwd(x, w):
    r = lax.rsqrt(jnp.mean(x * x, axis=-1, keepdims=True) + EPS)
    return x * r * w


def _rms_bwd(x, w, dy):
    r = lax.rsqrt(jnp.mean(x * x, axis=-1, keepdims=True) + EPS)
    xh = x * r
    dxh = dy * w
    dx = r * (dxh - xh * jnp.mean(dxh * xh, axis=-1, keepdims=True))
    return dx, jnp.sum(dy * xh, axis=0, keepdims=True)


def _tile(n, want):
    t = min(n, want)
    assert n % t == 0, (n, want)
    return t


def rmsnorm_bf16(x, w, name):
    T = x.shape[0]
    tm = _tile(T, 1024)

    def body(x_ref, w_ref, o_ref):
        o_ref[...] = _rms_fwd(x_ref[...], w_ref[...]).astype(BF16)

    return _pc(body, name=name, out_shape=_sds((T, D), BF16), grid=(T // tm,),
               in_specs=[pl.BlockSpec((tm, D), lambda i: (i, 0)), pl.BlockSpec((1, D), lambda i: (0, 0))],
               out_specs=pl.BlockSpec((tm, D), lambda i: (i, 0)), sem=("parallel",))(x, w)


def rmsnorm_bwd_add(x, w, dxn, dres, name):
    T = x.shape[0]
    tm = _tile(T, 512)

    def body(x_ref, w_ref, dxn_ref, dres_ref, dx_ref, dw_ref):
        dx, dw = _rms_bwd(x_ref[...], w_ref[...], dxn_ref[...])
        dx_ref[...] = dres_ref[...] + dx

        @pl.when(pl.program_id(0) == 0)
        def _():
            dw_ref[...] = jnp.zeros_like(dw_ref)
        dw_ref[...] += dw

    row = pl.BlockSpec((tm, D), lambda i: (i, 0))
    vec = pl.BlockSpec((1, D), lambda i: (0, 0))
    return _pc(body, name=name, out_shape=(_sds((T, D), F32), _sds((1, D), F32)), grid=(T // tm,),
               in_specs=[row, vec, row, row], out_specs=(row, vec), sem=("arbitrary",))(x, w, dxn, dres)


def final_loss(x, w, tgt, name):
    T = x.shape[0]
    tm = _tile(T, 512)

    def body(x_ref, w_ref, t_ref, loss_ref, dx_ref, dw_ref):
        xv, wv = x_ref[...], w_ref[...]
        err = _rms_fwd(xv, wv) - t_ref[...]
        dx, dw = _rms_bwd(xv, wv, err * (1.0 / D))
        dx_ref[...] = dx

        @pl.when(pl.program_id(0) == 0)
        def _():
            dw_ref[...] = jnp.zeros_like(dw_ref)
            loss_ref[...] = jnp.zeros_like(loss_ref)
        dw_ref[...] += dw
        loss_ref[...] += jnp.full((1, 128), 0.5 / D, F32) * jnp.sum(err * err)

    row = pl.BlockSpec((tm, D), lambda i: (i, 0))
    vec = pl.BlockSpec((1, D), lambda i: (0, 0))
    return _pc(body, name=name, out_shape=(_sds((1, 128), F32), _sds((T, D), F32), _sds((1, D), F32)),
               grid=(T // tm,), in_specs=[row, vec, row],
               out_specs=(pl.BlockSpec((1, 128), lambda i: (0, 0)), row, vec), sem=("arbitrary",))(x, w, tgt)


def _col_tile(n):
    for t in (1536, 1408, 1024, 768, 512, 384, 256, 128):
        if n % t == 0:
            return t
    return n


def mm_nn(a, b, name, bias=None, residual=None, out_dtype=F32):
    T, K = a.shape
    N = b.shape[1]
    tm, tn = _tile(T, 512), _col_tile(N)

    def body(a_ref, b_ref, *rest):
        o_ref = rest[-1]
        acc = _dot(a_ref[...].astype(BF16), b_ref[...])
        for extra in rest[:-1]:
            acc = acc + extra[...]
        o_ref[...] = acc.astype(out_dtype)

    in_specs = [pl.BlockSpec((tm, K), lambda j, i: (i, 0)), pl.BlockSpec((K, tn), lambda j, i: (0, j))]
    args = [a, b]
    if bias is not None:
        in_specs.append(pl.BlockSpec((1, tn), lambda j, i: (0, j)))
        args.append(bias)
    if residual is not None:
        in_specs.append(pl.BlockSpec((tm, tn), lambda j, i: (i, j)))
        args.append(residual)
    return _pc(body, name=name, out_shape=_sds((T, N), out_dtype), grid=(N // tn, T // tm), in_specs=in_specs,
               out_specs=pl.BlockSpec((tm, tn), lambda j, i: (i, j)), sem=("parallel", "parallel"))(*args)


def mm_nt(a, b, name, out_dtype=F32):
    T, N = a.shape
    K = b.shape[0]
    tm = _tile(T, 512)

    def body(a_ref, b_ref, o_ref):
        o_ref[...] = _dot(a_ref[...].astype(BF16), b_ref[...], NT).astype(out_dtype)

    return _pc(body, name=name, out_shape=_sds((T, K), out_dtype), grid=(T // tm,),
               in_specs=[pl.BlockSpec((tm, N), lambda i: (i, 0)), pl.BlockSpec((K, N), lambda i: (0, 0))],
               out_specs=pl.BlockSpec((tm, K), lambda i: (i, 0)), sem=("parallel",))(a, b)


def mm_tn(a, b, name):
    T, K = a.shape
    N = b.shape[1]
    tt, tn = _tile(T, 1024), _col_tile(N)

    def body(a_ref, b_ref, o_ref):
        @pl.when(pl.program_id(1) == 0)
        def _():
            o_ref[...] = jnp.zeros_like(o_ref)
        o_ref[...] += _dot(a_ref[...].astype(BF16), b_ref[...].astype(BF16), TN)

    return _pc(body, name=name, out_shape=_sds((K, N), F32), grid=(N // tn, T // tt),
               in_specs=[pl.BlockSpec((tt, K), lambda j, t: (t, 0)), pl.BlockSpec((tt, tn), lambda j, t: (t, j))],
               out_specs=pl.BlockSpec((K, tn), lambda j, t: (0, j)), sem=("parallel", "arbitrary"))(a, b)


def ffn_up(xn, wgu, name):
    T = xn.shape[0]
    tm = _tile(T, 512)

    def body(x_ref, w_ref, gu_ref):
        xv = x_ref[...]
        for j in range(2 * N_FB):
            gu_ref[j] = _dot(xv, w_ref[j]).astype(BF16)

    return _pc(body, name=name, out_shape=_sds((2 * N_FB, T, FB), BF16), grid=(T // tm,),
               in_specs=[pl.BlockSpec((tm, D), lambda i: (i, 0)), pl.BlockSpec((2 * N_FB, D, FB), lambda i: (0, 0, 0))],
               out_specs=pl.BlockSpec((2 * N_FB, tm, FB), lambda i: (0, i, 0)), sem=("parallel",))(xn, wgu)


def ffn_down(gu, wd, x, name):
    T = x.shape[0]
    tm = _tile(T, 512)

    def body(gu_ref, w_ref, x_ref, o_ref):
        acc = jnp.zeros((tm, D), F32)
        for g in range(N_FB):
            gate, up = gu_ref[g].astype(F32), gu_ref[N_FB + g].astype(F32)
            act = (gate * _sigmoid(gate) * up).astype(BF16)
            acc = acc + _dot(act, w_ref[g])
        o_ref[...] = x_ref[...] + 0.5 * acc

    row = pl.BlockSpec((tm, D), lambda i: (i, 0))
    return _pc(body, name=name, out_shape=_sds((T, D), F32), grid=(T // tm,),
               in_specs=[pl.BlockSpec((2 * N_FB, tm, FB), lambda i: (0, i, 0)),
                         pl.BlockSpec((N_FB, FB, D), lambda i: (0, 0, 0)), row],
               out_specs=row, sem=("parallel",))(gu, wd, x)


def ffn_bwd_hidden(dout, wd, gu, name):
    T = dout.shape[0]
    tm = _tile(T, 512)

    def body(d_ref, w_ref, gu_ref, dgu_ref, act_ref):
        dy = (0.5 * d_ref[...]).astype(BF16)
        for g in range(N_FB):
            gate, up = gu_ref[g].astype(F32), gu_ref[N_FB + g].astype(F32)
            sg = _sigmoid(gate)
            silu = gate * sg
            dact = _dot(dy, w_ref[g], NT)
            act_ref[g] = (silu * up).astype(BF16)
            dgu_ref[g] = (dact * up * (sg * (1.0 + gate * (1.0 - sg)))).astype(BF16)
            dgu_ref[N_FB + g] = (dact * silu).astype(BF16)

    return _pc(body, name=name, out_shape=(_sds((2 * N_FB, T, FB), BF16), _sds((N_FB, T, FB), BF16)), grid=(T // tm,),
               in_specs=[pl.BlockSpec((tm, D), lambda i: (i, 0)), pl.BlockSpec((N_FB, FB, D), lambda i: (0, 0, 0)),
                         pl.BlockSpec((2 * N_FB, tm, FB), lambda i: (0, i, 0))],
               out_specs=(pl.BlockSpec((2 * N_FB, tm, FB), lambda i: (0, i, 0)),
                          pl.BlockSpec((N_FB, tm, FB), lambda i: (0, i, 0))), sem=("parallel",))(dout, wd, gu)


def ffn_bwd_input(dgu, wgu, x, dout, nw, name):
    T = x.shape[0]
    tm = _tile(T, 512)

    def body(dgu_ref, w_ref, x_ref, d_ref, nw_ref, dx_ref, dnw_ref):
        dxn = jnp.zeros((tm, D), F32)
        for j in range(2 * N_FB):
            dxn = dxn + _dot(dgu_ref[j], w_ref[j], NT)
        dx, dw = _rms_bwd(x_ref[...], nw_ref[...], dxn)
        dx_ref[...] = d_ref[...] + dx

        @pl.when(pl.program_id(0) == 0)
        def _():
            dnw_ref[...] = jnp.zeros_like(dnw_ref)
        dnw_ref[...] += dw

    row = pl.BlockSpec((tm, D), lambda i: (i, 0))
    vec = pl.BlockSpec((1, D), lambda i: (0, 0))
    return _pc(body, name=name, out_shape=(_sds((T, D), F32), _sds((1, D), F32)), grid=(T // tm,),
               in_specs=[pl.BlockSpec((2 * N_FB, tm, FB), lambda i: (0, i, 0)),
                         pl.BlockSpec((2 * N_FB, D, FB), lambda i: (0, 0, 0)), row, row, vec],
               out_specs=(row, vec), sem=("arbitrary",))(dgu, wgu, x, dout, nw)


def ffn_wgrad_gu(xn, dgu, name):
    T = xn.shape[0]
    tt = _tile(T, 1024)

    def body(x_ref, d_ref, o_ref):
        @pl.when(pl.program_id(1) == 0)
        def _():
            o_ref[...] = jnp.zeros_like(o_ref)
        o_ref[0] += _dot(x_ref[...], d_ref[0], TN)

    return _pc(body, name=name, out_shape=_sds((2 * N_FB, D, FB), F32), grid=(2 * N_FB, T // tt),
               in_specs=[pl.BlockSpec((tt, D), lambda j, t: (t, 0)), pl.BlockSpec((1, tt, FB), lambda j, t: (j, t, 0))],
               out_specs=pl.BlockSpec((1, D, FB), lambda j, t: (j, 0, 0)), sem=("parallel", "arbitrary"))(xn, dgu)


def ffn_wgrad_down(act, dout, name):
    T = dout.shape[0]
    tt = _tile(T, 1024)

    def body(a_ref, d_ref, o_ref):
        @pl.when(pl.program_id(1) == 0)
        def _():
            o_ref[...] = jnp.zeros_like(o_ref)
        o_ref[0] += _dot(a_ref[0], (0.5 * d_ref[...]).astype(BF16), TN)

    return _pc(body, name=name, out_shape=_sds((N_FB, FB, D), F32), grid=(N_FB, T // tt),
               in_specs=[pl.BlockSpec((1, tt, FB), lambda g, t: (g, t, 0)), pl.BlockSpec((tt, D), lambda g, t: (t, 0))],
               out_specs=pl.BlockSpec((1, FB, D), lambda g, t: (g, 0, 0)), sem=("parallel", "arbitrary"))(act, dout)


def ffn_forward(x, nw, wgu, wd, tag):
    xn = rmsnorm_bf16(x, nw, f"{tag}_norm")
    gu = ffn_up(xn, wgu, f"{tag}_up")
    return ffn_down(gu, wd, x, f"{tag}_down"), (x, xn, gu)


def ffn_backward(dout, saved, nw, wgu, wd, tag):
    x, xn, gu = saved
    dgu, act = ffn_bwd_hidden(dout, wd, gu, f"{tag}_bwd_hidden")
    dwd = ffn_wgrad_down(act, dout, f"{tag}_wgrad_down")
    dwgu = ffn_wgrad_gu(xn, dgu, f"{tag}_wgrad_gu")
    dx, dnw = ffn_bwd_input(dgu, wgu, x, dout, nw, f"{tag}_bwd_input")
    return dx, dnw, dwgu, dwd


N_QKV_BLK = 3 * HEADS_A
Z_BLK0 = N_QKV_BLK
BA_BLK = A_COLS // 128 - 1


def _conv_taps(xcat, w):
    c = xcat[8:] * w[3:4]
    for k in range(3):
        c = c + pltpu.roll(xcat, 3 - k, 0)[8:] * w[k:k + 1]
    return c


def _head_cols(h):
    return slice(128 * h, 128 * (h + 1))


def gdn_conv_fwd(proj, wconv, name):
    T = proj.shape[0]
    tm = _tile(T, 512)

    def body(cur_ref, prev_ref, w_ref, c_ref, y_ref):
        kind, t = pl.program_id(0), pl.program_id(1)
        prev = jnp.where(t > 0, prev_ref[...], 0.0)
        c = _conv_taps(jnp.concatenate([prev, cur_ref[...]], axis=0), w_ref[...])
        c_ref[...] = c
        s = c * _sigmoid(c)
        scale = jnp.where(kind == 0, DK ** -0.5, 1.0)
        for h in range(HEADS_A):
            sh = s[:, _head_cols(h)]
            r = lax.rsqrt(jnp.sum(sh * sh, axis=-1, keepdims=True) + EPS)
            y_ref[h] = sh * jnp.where(kind < 2, r * scale, 1.0)

    return _pc(body, name=name, out_shape=(_sds((T, 3 * D), F32), _sds((N_QKV_BLK, T, 128), F32)),
               grid=(3, T // tm),
               in_specs=[pl.BlockSpec((tm, D), lambda kd, t: (t, kd)),
                         pl.BlockSpec((8, D), lambda kd, t: (jnp.maximum(t * (tm // 8) - 1, 0), kd)),
                         pl.BlockSpec((4, D), lambda kd, t: (0, kd))],
               out_specs=(pl.BlockSpec((tm, D), lambda kd, t: (t, kd)),
                          pl.BlockSpec((HEADS_A, tm, 128), lambda kd, t: (kd, t, 0))),
               sem=("parallel", "parallel"))(proj, proj, wconv)


def gdn_conv_bwd(dqkv, c, proj, wconv, name):
    T = c.shape[0]
    tm = _tile(T, 512)
    n_t = T // tm

    def body(dy_ref, dyn_ref, c_ref, cn_ref, x_ref, xp_ref, w_ref, dx_ref, dw_ref):
        kind, t = pl.program_id(0), pl.program_id(1)
        scale = jnp.where(kind == 0, DK ** -0.5, 1.0)

        def act_bwd(dy, cv):
            sg = _sigmoid(cv)
            s = cv * sg
            parts = []
            for h in range(HEADS_A):
                sh, dyh = s[:, _head_cols(h)], dy[h]
                r = lax.rsqrt(jnp.sum(sh * sh, axis=-1, keepdims=True) + EPS)
                ds_norm = scale * r * (dyh - (r * r) * sh * jnp.sum(dyh * sh, axis=-1, keepdims=True))
                parts.append(jnp.where(kind < 2, ds_norm, dyh))
            return jnp.concatenate(parts, axis=1) * (sg * (1.0 + cv * (1.0 - sg)))

        w = w_ref[...]
        dcur = act_bwd(dy_ref[...], c_ref[...])
        dnext = jnp.where(t < n_t - 1, act_bwd(dyn_ref[...], cn_ref[...]), 0.0)
        dcat = jnp.concatenate([dcur, dnext], axis=0)
        dx = dcur * w[3:4]
        for k in range(3):
            dx = dx + pltpu.roll(dcat, tm + 8 - (3 - k), 0)[:tm] * w[k:k + 1]
        dx_ref[...] = dx.astype(BF16)
        xprev = jnp.where(t > 0, xp_ref[...], 0.0)
        xcat = jnp.concatenate([xprev, x_ref[...]], axis=0)
        rows = [jnp.sum(dcur * pltpu.roll(xcat, 3 - k, 0)[8:], axis=0, keepdims=True) for k in range(3)]
        rows.append(jnp.sum(dcur * xcat[8:], axis=0, keepdims=True))

        @pl.when(t == 0)
        def _():
            dw_ref[...] = jnp.zeros_like(dw_ref)
        dw_ref[...] += jnp.concatenate(rows, axis=0)

    def nxt(t):
        return jnp.minimum((t + 1) * (tm // 8), T // 8 - 1)

    cur = pl.BlockSpec((tm, D), lambda kd, t: (t, kd))
    return _pc(body, name=name, out_shape=(_sds((T, 3 * D), BF16), _sds((4, 3 * D), F32)), grid=(3, n_t),
               in_specs=[pl.BlockSpec((HEADS_A, tm, 128), lambda kd, t: (kd, t, 0)),
                         pl.BlockSpec((HEADS_A, 8, 128), lambda kd, t: (kd, nxt(t), 0)),
                         cur, pl.BlockSpec((8, D), lambda kd, t: (nxt(t), kd)),
                         cur, pl.BlockSpec((8, D), lambda kd, t: (jnp.maximum(t * (tm // 8) - 1, 0), kd)),
                         pl.BlockSpec((4, D), lambda kd, t: (0, kd))],
               out_specs=(cur, pl.BlockSpec((4, D), lambda kd, t: (0, kd))),
               sem=("parallel", "arbitrary"))(dqkv, dqkv, c, c, proj, proj, wconv)


def _chunk_masks(n):
    ri = lax.broadcasted_iota(jnp.int32, (n, n), 0)
    ci = lax.broadcasted_iota(jnp.int32, (n, n), 1)
    same = (ri // CHUNK) == (ci // CHUNK)
    return same & (ri >= ci), same & (ri <= ci)


def gdn_gate_fwd(proj, al, dtb, name):
    T = proj.shape[0]
    tg = _tile(T, PREP_T)

    def body(ba_ref, al_ref, dtb_ref, o_ref):
        x = ba_ref[...]
        lane = lax.broadcasted_iota(jnp.int32, x.shape, 1)
        is_a = (lane >= HEADS_A) & (lane < 2 * HEADS_A)
        g = jnp.where(is_a, -jnp.exp(al_ref[...]) * _softplus(x + dtb_ref[...]), 0.0)
        lower, _ = _chunk_masks(tg)
        gc = _dot(lower.astype(F32), g, precision=HI)
        o_ref[...] = jnp.where(lane < HEADS_A, _sigmoid(x), gc)

    vec = pl.BlockSpec((1, 128), lambda i: (0, 0))
    return _pc(body, name=name, out_shape=_sds((T, 128), F32), grid=(T // tg,),
               in_specs=[pl.BlockSpec((tg, 128), lambda i: (i, BA_BLK)), vec, vec],
               out_specs=pl.BlockSpec((tg, 128), lambda i: (i, 0)), sem=("parallel",))(proj, al, dtb)


def gdn_gate_bwd(proj, al, dtb, dgb, name):
    T = proj.shape[0]
    tg = _tile(T, PREP_T)

    def body(ba_ref, al_ref, dtb_ref, dgb_ref, dba_ref, dal_ref, ddt_ref):
        x, d = ba_ref[...], dgb_ref[...]
        lane = lax.broadcasted_iota(jnp.int32, x.shape, 1)
        is_b = lane < HEADS_A
        is_a = (lane >= HEADS_A) & (lane < 2 * HEADS_A)
        beta = _sigmoid(x)
        e_a = jnp.exp(al_ref[...])
        z = x + dtb_ref[...]
        g = jnp.where(is_a, -e_a * _softplus(z), 0.0)
        _, upper = _chunk_masks(tg)
        dg = _dot(upper.astype(F32), jnp.where(is_a, d, 0.0), precision=HI)
        da = jnp.where(is_a, dg * (-e_a) * _sigmoid(z), 0.0)
        db = jnp.where(is_b, d * beta * (1.0 - beta), 0.0)
        dba_ref[...] = (da + db).astype(BF16)

        @pl.when(pl.program_id(0) == 0)
        def _():
            dal_ref[...] = jnp.zeros_like(dal_ref)
            ddt_ref[...] = jnp.zeros_like(ddt_ref)
        dal_ref[...] += jnp.sum(dg * g, axis=0, keepdims=True)
        ddt_ref[...] += jnp.sum(da, axis=0, keepdims=True)

    vec = pl.BlockSpec((1, 128), lambda i: (0, 0))
    blk = pl.BlockSpec((tg, 128), lambda i: (i, 0))
    return _pc(body, name=name, out_shape=(_sds((T, 128), BF16), _sds((1, 128), F32), _sds((1, 128), F32)),
               grid=(T // tg,), in_specs=[pl.BlockSpec((tg, 128), lambda i: (i, BA_BLK)), vec, vec, blk],
               out_specs=(blk, vec, vec), sem=("arbitrary",))(proj, al, dtb, dgb)


def _bmm(a, b, dims, precision=None):
    return lax.dot_general(a, b, dims, preferred_element_type=F32, precision=precision)


B_NN = (((2,), (1,)), ((0,), (0,)))
B_NT = (((2,), (2,)), ((0,), (0,)))


def _select_lane(x, lane_index):
    lane = lax.broadcasted_iota(jnp.int32, x.shape, x.ndim - 1)
    return jnp.sum(jnp.where(lane == lane_index, x, 0.0), axis=-1, keepdims=True)


B_TN = (((1,), (1,)), ((0,), (0,)))


def _bmm_split(a, b, dims):
    ah, bh = a.astype(BF16), b.astype(BF16)
    al, bl = (a - ah.astype(F32)).astype(BF16), (b - bh.astype(F32)).astype(BF16)
    return _bmm(ah, bh, dims) + (_bmm(ah, bl, dims) + _bmm(al, bh, dims))


@jax.custom_vjp
def _bmm_f32(a, b):
    return _bmm_split(a, b, B_NN)


def _bmm_f32_fwd(a, b):
    return _bmm_split(a, b, B_NN), (a, b)


def _bmm_f32_bwd(res, dc):
    a, b = res
    return _bmm_split(dc, b, B_NT), _bmm_split(a, dc, B_TN)


_bmm_f32.defvjp(_bmm_f32_fwd, _bmm_f32_bwd)


def _tri_inverse(lmat):
    ri = lax.broadcasted_iota(jnp.int32, lmat.shape, 1)
    ci = lax.broadcasted_iota(jnp.int32, lmat.shape, 2)
    inv = jnp.where(ri == ci, 1.0, 0.0) - lmat
    power = lmat
    for _ in range(5):
        power = _bmm_split(power, power, B_NN)
        inv = inv + _bmm_split(inv, power, B_NN)
    return inv


def _stored_inverse(x):
    @jax.custom_vjp
    def inverse(lmat):
        return x

    def fwd(lmat):
        return x, None

    def bwd(_, dx):
        return (-_bmm_split(_bmm_split(x, dx, B_TN), x, B_NT),)

    inverse.defvjp(fwd, bwd)
    return inverse


def _gdn_prep(q, k, v, gb, h, inverse):
    nb = q.shape[0]
    beta = _select_lane(gb, h)
    gc = _select_lane(gb, HEADS_A + h)
    ri = lax.broadcasted_iota(jnp.int32, (nb, CHUNK, CHUNK), 1)
    ci = lax.broadcasted_iota(jnp.int32, (nb, CHUNK, CHUNK), 2)
    lower, strict, eye = ri >= ci, ri > ci, ri == ci
    gcol = jnp.broadcast_to(gc, (nb, CHUNK, CHUNK))
    grow = _bmm_f32(jnp.ones((nb, CHUNK, CHUNK), F32), jnp.where(eye, gcol, 0.0))
    decay = jnp.where(lower, jnp.exp(jnp.where(lower, gcol - grow, 0.0)), 0.0)
    kb = k * beta
    kbf = k.astype(BF16)
    inv = inverse(jnp.where(strict, _bmm(kb.astype(BF16), kbf, B_NT) * decay, 0.0))
    eg = jnp.exp(gc)
    sol = _bmm_f32(inv, jnp.concatenate([v * beta, kb * eg], axis=-1))
    aqk = _bmm(q.astype(BF16), kbf, B_NT) * decay
    g_last = gc[:, CHUNK - 1:CHUNK, :]
    gl = jnp.broadcast_to(jnp.exp(g_last), (nb, 1, 128))
    return (sol[..., :DK], sol[..., DK:], q * eg, k * jnp.exp(g_last - gc), aqk, gl), inv


def gdn_prep_fwd(qkv, gb, name):
    T = qkv.shape[1]
    tp = _tile(T, PREP_T)
    nb = tp // CHUNK

    def body(q_ref, k_ref, v_ref, gb_ref, u_ref, w_ref, qd_ref, kd_ref, a_ref, gl_ref, inv_ref):
        h = pl.program_id(1)
        shp = (nb, CHUNK, 128)
        (u, w, qd, kd, aqk, gl), inv = _gdn_prep(q_ref[0].reshape(shp), k_ref[0].reshape(shp), v_ref[0].reshape(shp),
                                                 gb_ref[...].reshape(shp), h, _tri_inverse)
        u_ref[0] = u.reshape(tp, 128)
        w_ref[0] = w.reshape(tp, 128)
        qd_ref[0] = qd.reshape(tp, 128)
        kd_ref[0] = kd.reshape(tp, 128)
        a_ref[0] = aqk.reshape(tp, CHUNK)
        gl_ref[0] = gl.reshape(nb, 1, 128)
        inv_ref[0] = inv.reshape(tp, CHUNK)

    def head(off):
        return pl.BlockSpec((1, tp, 128), lambda n, h: (h + off, n, 0))

    per_head = _sds((HEADS_A, T, 128), F32)
    narrow = pl.BlockSpec((1, tp, CHUNK), lambda n, h: (h, n, 0))
    return _pc(body, name=name,
               out_shape=(per_head, per_head, per_head, per_head, _sds((HEADS_A, T, CHUNK), F32),
                          _sds((HEADS_A, T // CHUNK, 1, 128), F32), _sds((HEADS_A, T, CHUNK), F32)),
               grid=(T // tp, HEADS_A),
               in_specs=[head(0), head(HEADS_A), head(2 * HEADS_A), pl.BlockSpec((tp, 128), lambda n, h: (n, 0))],
               out_specs=(head(0), head(0), head(0), head(0), narrow,
                          pl.BlockSpec((1, nb, 1, 128), lambda n, h: (h, n, 0, 0)), narrow),
               sem=("parallel", "parallel"))(qkv, qkv, qkv, gb)


def gdn_prep_bwd(qkv, gb, inv, du, dw, dqd, dkd, da, dgl, name):
    T = qkv.shape[1]
    tp = _tile(T, PREP_T)
    nb = tp // CHUNK

    def body(q_ref, k_ref, v_ref, gb_ref, inv_ref, du_ref, dw_ref, dqd_ref, dkd_ref, da_ref, dgl_ref, dqkv_ref, dgb_ref):
        h = pl.program_id(1)
        shp = (nb, CHUNK, 128)
        stored = _stored_inverse(inv_ref[0].reshape(nb, CHUNK, CHUNK))
        _, vjp = jax.vjp(lambda q, k, v, gb: _gdn_prep(q, k, v, gb, h, stored)[0], q_ref[0].reshape(shp),
                         k_ref[0].reshape(shp), v_ref[0].reshape(shp), gb_ref[...].reshape(shp))
        dq, dk, dv, dgb = vjp((du_ref[0].reshape(shp), dw_ref[0].reshape(shp), dqd_ref[0].reshape(shp),
                               dkd_ref[0].reshape(shp), da_ref[0].reshape(nb, CHUNK, CHUNK), dgl_ref[0].reshape(nb, 1, 128)))
        dqkv_ref[h] = dq.reshape(tp, 128)
        dqkv_ref[HEADS_A + h] = dk.reshape(tp, 128)
        dqkv_ref[2 * HEADS_A + h] = dv.reshape(tp, 128)

        @pl.when(h == 0)
        def _():
            dgb_ref[...] = jnp.zeros_like(dgb_ref)
        dgb_ref[...] += dgb.reshape(tp, 128)

    def head(off):
        return pl.BlockSpec((1, tp, 128), lambda n, h: (h + off, n, 0))

    narrow = pl.BlockSpec((1, tp, CHUNK), lambda n, h: (h, n, 0))
    return _pc(body, name=name, out_shape=(_sds((N_QKV_BLK, T, 128), F32), _sds((T, 128), F32)),
               grid=(T // tp, HEADS_A),
               in_specs=[head(0), head(HEADS_A), head(2 * HEADS_A), pl.BlockSpec((tp, 128), lambda n, h: (n, 0)), narrow,
                         head(0), head(0), head(0), head(0), narrow,
                         pl.BlockSpec((1, nb, 1, 128), lambda n, h: (h, n, 0, 0))],
               out_specs=(pl.BlockSpec((N_QKV_BLK, tp, 128), lambda n, h: (0, n, 0)),
                          pl.BlockSpec((tp, 128), lambda n, h: (n, 0))),
               sem=("parallel", "arbitrary"))(qkv, qkv, qkv, gb, inv, du, dw, dqd, dkd, da, dgl)


def gdn_scan_fwd(u, w, qd, kd, aqk, gl, name):
    T = u.shape[1]
    n_chunks = T // CHUNK

    def body(u_ref, w_ref, qd_ref, kd_ref, a_ref, gl_ref, o_ref, sin_ref, state):
        @pl.when(pl.program_id(0) == 0)
        def _():
            state[...] = jnp.zeros_like(state)
        for h in range(HEADS_A):
            s = state[h]
            sin_ref[0, h] = s
            sb = s.astype(BF16)
            both = _dot(jnp.concatenate([w_ref[h], qd_ref[h]], axis=0).astype(BF16), sb)
            vn = (u_ref[h] - both[:CHUNK]).astype(BF16)
            o_ref[h] = both[CHUNK:] + _dot(a_ref[h].astype(BF16), vn)
            state[h] = s * gl_ref[h, 0] + _dot(kd_ref[h].astype(BF16), vn, TN)

    blk = pl.BlockSpec((HEADS_A, CHUNK, 128), lambda n: (0, n, 0))
    return _pc(body, name=name,
               out_shape=(_sds((HEADS_A, T, 128), F32), _sds((n_chunks, HEADS_A, DK, 128), F32)), grid=(n_chunks,),
               in_specs=[blk, blk, blk, blk, pl.BlockSpec((HEADS_A, CHUNK, CHUNK), lambda n: (0, n, 0)),
                         pl.BlockSpec((HEADS_A, 1, 1, 128), lambda n: (0, n, 0, 0))],
               out_specs=(blk, pl.BlockSpec((1, HEADS_A, DK, 128), lambda n: (n, 0, 0, 0))),
               scratch=[pltpu.VMEM((HEADS_A, DK, 128), F32)], sem=("arbitrary",))(u, w, qd, kd, aqk, gl)


def gdn_scan_bwd(u, w, qd, kd, aqk, gl, sin, do, name):
    T = u.shape[1]
    n_chunks = T // CHUNK

    def body(u_ref, w_ref, qd_ref, kd_ref, a_ref, gl_ref, sin_ref, do_ref,
             du_ref, dw_ref, dqd_ref, dkd_ref, da_ref, dgl_ref, dstate):
        @pl.when(pl.program_id(0) == 0)
        def _():
            dstate[...] = jnp.zeros_like(dstate)
        lane0 = lax.broadcasted_iota(jnp.int32, (1, 128), 1) == 0
        for h in range(HEADS_A):
            s = sin_ref[0, h]
            sb = s.astype(BF16)
            wb, qdb, kdb = w_ref[h].astype(BF16), qd_ref[h].astype(BF16), kd_ref[h].astype(BF16)
            ab, dob = a_ref[h].astype(BF16), do_ref[h].astype(BF16)
            vn = (u_ref[h] - _dot(wb, sb)).astype(BF16)
            ds_out = dstate[h]
            dsb = ds_out.astype(BF16)
            dqd_ref[h] = _dot(dob, sb, NT)
            da_ref[h] = _dot(dob, vn, NT)
            dv = _dot(ab, dob, TN) + _dot(kdb, dsb)
            dkd_ref[h] = _dot(vn, dsb, NT)
            dgl_ref[h, 0] = jnp.where(lane0, jnp.sum(ds_out * s), 0.0)
            du_ref[h] = dv
            dvb = dv.astype(BF16)
            dw_ref[h] = -_dot(dvb, sb, NT)
            dstate[h] = ds_out * gl_ref[h, 0] + _dot(qdb, dob, TN) - _dot(wb, dvb, TN)

    last = n_chunks - 1
    blk = pl.BlockSpec((HEADS_A, CHUNK, 128), lambda n: (0, last - n, 0))
    ablk = pl.BlockSpec((HEADS_A, CHUNK, CHUNK), lambda n: (0, last - n, 0))
    glblk = pl.BlockSpec((HEADS_A, 1, 1, 128), lambda n: (0, last - n, 0, 0))
    per_head = _sds((HEADS_A, T, 128), F32)
    return _pc(body, name=name,
               out_shape=(per_head, per_head, per_head, per_head, _sds((HEADS_A, T, CHUNK), F32),
                          _sds((HEADS_A, n_chunks, 1, 128), F32)), grid=(n_chunks,),
               in_specs=[blk, blk, blk, blk, ablk, glblk,
                         pl.BlockSpec((1, HEADS_A, DK, 128), lambda n: (last - n, 0, 0, 0)), blk],
               out_specs=(blk, blk, blk, blk, ablk, glblk),
               scratch=[pltpu.VMEM((HEADS_A, DK, 128), F32)], sem=("arbitrary",))(u, w, qd, kd, aqk, gl, sin, do)


def gdn_outnorm_fwd(o, proj, wn, name):
    T = o.shape[1]
    tm = _tile(T, 512)

    def body(o_ref, z_ref, wn_ref, y_ref):
        for h in range(HEADS_A):
            z = z_ref[:, 128 * h:128 * (h + 1)]
            y_ref[:, 128 * h:128 * (h + 1)] = (_rms_fwd(o_ref[h], wn_ref[...]) * (z * _sigmoid(z))).astype(BF16)

    return _pc(body, name=name, out_shape=_sds((T, D), BF16), grid=(T // tm,),
               in_specs=[pl.BlockSpec((HEADS_A, tm, 128), lambda i: (0, i, 0)),
                         pl.BlockSpec((tm, D), lambda i: (i, Z_BLK0 * 128 // D)), pl.BlockSpec((1, 128), lambda i: (0, 0))],
               out_specs=pl.BlockSpec((tm, D), lambda i: (i, 0)), sem=("parallel",))(o, proj, wn)


def gdn_outnorm_bwd(o, proj, wn, dy, name):
    T = o.shape[1]
    tm = _tile(T, 512)

    def body(o_ref, z_ref, wn_ref, dy_ref, do_ref, dz_ref, dwn_ref):
        wn = wn_ref[...]
        acc = jnp.zeros((1, 128), F32)
        for h in range(HEADS_A):
            cols = slice(128 * h, 128 * (h + 1))
            z, dyh, ov = z_ref[:, cols], dy_ref[:, cols], o_ref[h]
            sg = _sigmoid(z)
            do, dwn = _rms_bwd(ov, wn, dyh * (z * sg))
            do_ref[h] = do
            acc = acc + dwn
            dz_ref[:, cols] = (dyh * _rms_fwd(ov, wn) * (sg * (1.0 + z * (1.0 - sg)))).astype(BF16)

        @pl.when(pl.program_id(0) == 0)
        def _():
            dwn_ref[...] = jnp.zeros_like(dwn_ref)
        dwn_ref[...] += acc

    row = pl.BlockSpec((tm, D), lambda i: (i, 0))
    vec = pl.BlockSpec((1, 128), lambda i: (0, 0))
    hblk = pl.BlockSpec((HEADS_A, tm, 128), lambda i: (0, i, 0))
    return _pc(body, name=name, out_shape=(_sds((HEADS_A, T, 128), F32), _sds((T, D), BF16), _sds((1, 128), F32)),
               grid=(T // tm,),
               in_specs=[hblk, pl.BlockSpec((tm, D), lambda i: (i, Z_BLK0 * 128 // D)), vec, row],
               out_specs=(hblk, row, vec), sem=("arbitrary",))(o, proj, wn, dy)


def gdn_forward(x, nw, w_in, wconv, al, dtb, wn, w_out, tag):
    h = rmsnorm_bf16(x, nw, f"{tag}_norm")
    proj = mm_nn(h, w_in, f"{tag}_proj")
    c, qkv = gdn_conv_fwd(proj, wconv, f"{tag}_conv")
    gb = gdn_gate_fwd(proj, al, dtb, f"{tag}_gate")
    u, w, qd, kd, aqk, gl, inv = gdn_prep_fwd(qkv, gb, f"{tag}_prep")
    o, sin = gdn_scan_fwd(u, w, qd, kd, aqk, gl, f"{tag}_scan")
    on = gdn_outnorm_fwd(o, proj, wn, f"{tag}_outnorm")
    y = mm_nn(on, w_out, f"{tag}_out", residual=x)
    return y, (x, h, proj, c, qkv, gb, inv, (u, w, qd, kd, aqk, gl), sin, o, on)


def gdn_backward(dout, saved, nw, w_in, wconv, al, dtb, wn, w_out, tag):
    x, h, proj, c, qkv, gb, inv, prep, sin, o, on = saved
    d_on = mm_nt(dout, w_out, f"{tag}_out_bwd")
    dw_out = mm_tn(on, dout, f"{tag}_out_wgrad")
    do, dz, dwn = gdn_outnorm_bwd(o, proj, wn, d_on, f"{tag}_outnorm_bwd")
    du, dw, dqd, dkd, da, dgl = gdn_scan_bwd(*prep, sin, do, f"{tag}_scan_bwd")
    dqkv, dgb = gdn_prep_bwd(qkv, gb, inv, du, dw, dqd, dkd, da, dgl, f"{tag}_prep_bwd")
    dba, dal, ddt = gdn_gate_bwd(proj, al, dtb, dgb, f"{tag}_gate_bwd")
    dpre, dwconv = gdn_conv_bwd(dqkv, c, proj, wconv, f"{tag}_conv_bwd")
    dproj = jnp.concatenate([dpre, dz, dba], axis=1)
    dw_in = mm_tn(h, dproj, f"{tag}_proj_wgrad")
    dh = mm_nt(dproj, w_in, f"{tag}_proj_bwd")
    dx, dnw = rmsnorm_bwd_add(x, nw, dh, dout, f"{tag}_norm_bwd")
    return dx, dnw, dw_in, dwconv, dal, ddt, dwn, dw_out


N_KV, GROUP = 4, 4
KV_COLS = 2 * N_KV * B_HD
B_COLS = D + KV_COLS


def _swa_block(q, kp, kc, vp, vc, sk, first):
    kk = jnp.concatenate([kp, kc], axis=0)
    vv = jnp.concatenate([vp, vc], axis=0)
    rows = GROUP * B_BLK
    qi = lax.broadcasted_iota(jnp.int32, (rows, 2 * B_BLK), 0) % B_BLK
    kj = lax.broadcasted_iota(jnp.int32, (rows, 2 * B_BLK), 1)
    rel = qi + B_BLK - kj
    valid = (rel >= 0) & (rel < B_BLK) & (jnp.logical_not(first) | (kj >= B_BLK))
    outs = []
    for j in range(N_KV):
        heads = range(GROUP * j, GROUP * (j + 1))
        qs = jnp.concatenate([q[:, hq * B_HD:(hq + 1) * B_HD] for hq in heads], axis=0).astype(BF16)
        s = _dot(qs, kk[:, j * B_HD:(j + 1) * B_HD].astype(BF16), NT) * (B_HD ** -0.5)
        s = jnp.where(valid, s, -1e30)
        sink = jnp.concatenate([jnp.broadcast_to(sk[:, hq:hq + 1], (B_BLK, 1)) for hq in heads], axis=0)
        m = lax.stop_gradient(jnp.maximum(jnp.max(s, axis=-1, keepdims=True), sink))
        p = jnp.exp(s - m)
        den = jnp.sum(p, axis=-1, keepdims=True) + jnp.exp(sink - m)
        o = _dot((p / den).astype(BF16), vv[:, j * B_HD:(j + 1) * B_HD].astype(BF16))
        outs += [o[g * B_BLK:(g + 1) * B_BLK] for g in range(GROUP)]
    return jnp.concatenate(outs, axis=1)


def swa_core_fwd(proj, sk, name):
    T = proj.shape[0]
    half = N_KV * B_HD

    def body(q_ref, kvc_ref, kvp_ref, sk_ref, o_ref):
        kvc, kvp = kvc_ref[...], kvp_ref[...]
        o_ref[...] = _swa_block(q_ref[...], kvp[:, :half], kvc[:, :half], kvp[:, half:], kvc[:, half:], sk_ref[...],
                                pl.program_id(0) == 0).astype(BF16)

    return _pc(body, name=name, out_shape=_sds((T, D), BF16), grid=(T // B_BLK,),
               in_specs=[pl.BlockSpec((B_BLK, D), lambda n: (n, 0)),
                         pl.BlockSpec((B_BLK, KV_COLS), lambda n: (n, D // KV_COLS)),
                         pl.BlockSpec((B_BLK, KV_COLS), lambda n: (jnp.maximum(n - 1, 0), D // KV_COLS)),
                         pl.BlockSpec((1, 128), lambda n: (0, 0))],
               out_specs=pl.BlockSpec((B_BLK, D), lambda n: (n, 0)), sem=("parallel",))(proj, proj, proj, sk)


def swa_core_bwd(proj, sk, do, name):
    T = proj.shape[0]
    last = T // B_BLK - 1
    half = N_KV * B_HD

    def body(q_ref, kvc_ref, kvp_ref, sk_ref, do_ref, dproj_ref, dbias_ref, dsk_ref, carry):
        step = pl.program_id(0)
        first = step == last

        @pl.when(step == 0)
        def _():
            carry[...] = jnp.zeros_like(carry)
            dbias_ref[...] = jnp.zeros_like(dbias_ref)
            dsk_ref[...] = jnp.zeros_like(dsk_ref)
        kvc, kvp = kvc_ref[...], kvp_ref[...]
        _, vjp = jax.vjp(functools.partial(_swa_block, first=first), q_ref[...], kvp[:, :half], kvc[:, :half],
                         kvp[:, half:], kvc[:, half:], sk_ref[...])
        dq, dkp, dkc, dvp, dvc, dsk = vjp(do_ref[...])
        dkv = jnp.concatenate([dkc, dvc], axis=1) + carry[...]
        carry[...] = jnp.concatenate([dkp, dvp], axis=1)
        row = jnp.concatenate([dq, dkv], axis=1)
        dproj_ref[...] = row.astype(BF16)
        dbias_ref[...] += jnp.sum(row, axis=0, keepdims=True)
        dsk_ref[...] += dsk

    return _pc(body, name=name, out_shape=(_sds((T, B_COLS), BF16), _sds((1, B_COLS), F32), _sds((1, 128), F32)),
               grid=(T // B_BLK,),
               in_specs=[pl.BlockSpec((B_BLK, D), lambda n: (last - n, 0)),
                         pl.BlockSpec((B_BLK, KV_COLS), lambda n: (last - n, D // KV_COLS)),
                         pl.BlockSpec((B_BLK, KV_COLS), lambda n: (jnp.maximum(last - n - 1, 0), D // KV_COLS)),
                         pl.BlockSpec((1, 128), lambda n: (0, 0)), pl.BlockSpec((B_BLK, D), lambda n: (last - n, 0))],
               out_specs=(pl.BlockSpec((B_BLK, B_COLS), lambda n: (last - n, 0)),
                          pl.BlockSpec((1, B_COLS), lambda n: (0, 0)), pl.BlockSpec((1, 128), lambda n: (0, 0))),
               scratch=[pltpu.VMEM((B_BLK, KV_COLS), F32)], sem=("arbitrary",))(proj, proj, proj, sk, do)


def col_sum(a, name):
    T, N = a.shape
    tm = _tile(T, 1024)

    def body(a_ref, o_ref):
        @pl.when(pl.program_id(0) == 0)
        def _():
            o_ref[...] = jnp.zeros_like(o_ref)
        o_ref[...] += jnp.sum(a_ref[...].astype(F32), axis=0, keepdims=True)

    return _pc(body, name=name, out_shape=_sds((1, N), F32), grid=(T // tm,),
               in_specs=[pl.BlockSpec((tm, N), lambda i: (i, 0))], out_specs=pl.BlockSpec((1, N), lambda i: (0, 0)),
               sem=("arbitrary",))(a)


def swa_forward(x, nw, w_in, b_in, sk, w_out, b_out, tag):
    h = rmsnorm_bf16(x, nw, f"{tag}_norm")
    proj = mm_nn(h, w_in, f"{tag}_proj", bias=b_in)
    o = swa_core_fwd(proj, sk, f"{tag}_core")
    y = mm_nn(o, w_out, f"{tag}_out", bias=b_out, residual=x)
    return y, (x, h, proj, o)


def swa_backward(dout, saved, nw, w_in, b_in, sk, w_out, b_out, tag):
    x, h, proj, o = saved
    do = mm_nt(dout, w_out, f"{tag}_out_bwd")
    dw_out = mm_tn(o, dout, f"{tag}_out_wgrad")
    db_out = col_sum(dout, f"{tag}_out_bias_grad")
    dproj, db_in, dsk = swa_core_bwd(proj, sk, do, f"{tag}_core_bwd")
    dw_in = mm_tn(h, dproj, f"{tag}_proj_wgrad")
    dh = mm_nt(dproj, w_in, f"{tag}_proj_bwd")
    dx, dnw = rmsnorm_bwd_add(x, nw, dh, dout, f"{tag}_norm_bwd")
    return dx, dnw, dw_in, db_in, dsk, dw_out, db_out


MESH = pl.DeviceIdType.MESH
IN_HBM = pl.BlockSpec(memory_space=pl.ANY)


def _position():
    return lax.axis_index("x"), lax.axis_index("y"), lax.axis_index("c")


def _slot(x, y, c):
    return 4 * x + 2 * y + c


def _peer(x, y, c, k):
    return (1 - x if k & 4 else x, 1 - y if k & 2 else y, 1 - c if k & 1 else c)


def all_gather(shards, name):
    n = len(shards)

    def body(*refs):
        ins, outs = refs[:n], refs[n:2 * n]
        send_sems, recv_sems, local_sems = refs[2 * n:]
        x, y, c = _position()
        me, sibling = (x, y, c), (x, y, 1 - c)
        chips = [(1 - x, y), (x, 1 - y), (1 - x, 1 - y)]

        def copy(a, k, block, to, src=None):
            dst = outs[a].at[_slot(*block)]
            return pltpu.make_async_remote_copy(src_ref=dst if src is None else src, dst_ref=dst,
                                                send_sem=send_sems.at[a, k], recv_sem=recv_sems.at[a, k],
                                                device_id=to, device_id_type=MESH)

        mine = [pltpu.make_async_copy(ins[a], outs[a].at[_slot(*me)], local_sems.at[a]) for a in range(n)]
        for cp in mine:
            cp.start()
        first = []
        for a in range(n):
            first.append(copy(a, 0, me, sibling, src=ins[a]))
            first += [copy(a, 1 + j, me, (*chip, c), src=ins[a]) for j, chip in enumerate(chips)]
        for cp in first:
            cp.start()
        passed = []
        for j, chip in enumerate(chips):
            for a in range(n):
                copy(a, 1 + j, (*chip, c), me).wait_recv()
                passed.append(copy(a, 4 + j, (*chip, c), sibling))
                passed[-1].start()
        for a in range(n):
            copy(a, 0, sibling, me).wait_recv()
            for j, chip in enumerate(chips):
                copy(a, 4 + j, (*chip, 1 - c), me).wait_recv()
        for cp in first + passed:
            cp.wait_send()
        for cp in mine:
            cp.wait()

    return _pc(body, name=name, out_shape=[_sds((N_DEV,) + s.shape, s.dtype) for s in shards],
               in_specs=[IN_HBM] * n, out_specs=[IN_HBM] * n,
               scratch=[pltpu.SemaphoreType.DMA((n, 7)), pltpu.SemaphoreType.DMA((n, 7)), pltpu.SemaphoreType.DMA((n,))],
               )(*shards)


def exchange_partials(parts, name):
    n = len(parts)

    def body(*refs):
        ins, outs = refs[:n], refs[n:2 * n]
        send_sems, recv_sems, local_sems = refs[2 * n:]
        x, y, c = _position()
        me = _slot(x, y, c)
        local = [pltpu.make_async_copy(ins[a].at[me], outs[a].at[me], local_sems.at[a]) for a in range(n)]
        for cp in local:
            cp.start()
        copies = []
        for k in (1, 2, 4, 3, 5, 6, 7):
            peer = _peer(x, y, c, k)
            for a in range(n):
                copies.append(pltpu.make_async_remote_copy(
                    src_ref=ins[a].at[_slot(*peer)], dst_ref=outs[a].at[me], send_sem=send_sems.at[a, k - 1],
                    recv_sem=recv_sems.at[a, k - 1], device_id=peer, device_id_type=MESH))
                copies[-1].start()
        for cp in copies:
            cp.wait()
        for cp in local:
            cp.wait()

    return _pc(body, name=name, out_shape=[_sds(p.shape, p.dtype) for p in parts],
               in_specs=[IN_HBM] * n, out_specs=[IN_HBM] * n,
               scratch=[pltpu.SemaphoreType.DMA((n, 7)), pltpu.SemaphoreType.DMA((n, 7)), pltpu.SemaphoreType.DMA((n,))],
               )(*parts)


def all_reduce_small(part, name):
    R, C = part.shape

    def body(p_ref, o_ref, buf, send_sems, recv_sems):
        x, y, c = _position()
        me = _slot(x, y, c)
        copies = []
        for k in range(1, N_DEV):
            copies.append(pltpu.make_async_remote_copy(
                src_ref=p_ref, dst_ref=buf.at[me], send_sem=send_sems.at[k - 1], recv_sem=recv_sems.at[k - 1],
                device_id=_peer(x, y, c, k), device_id_type=MESH))
            copies[-1].start()
        buf[me] = p_ref[...]
        for cp in copies:
            cp.wait()
        acc = buf[0]
        for s in range(1, N_DEV):
            acc = acc + buf[s]
        o_ref[...] = acc

    vmem = pl.BlockSpec(memory_space=pltpu.VMEM)
    return _pc(body, name=name, out_shape=_sds((R, C), F32), in_specs=[vmem], out_specs=vmem,
               scratch=[pltpu.VMEM((N_DEV, R, C), F32), pltpu.SemaphoreType.DMA((7,)), pltpu.SemaphoreType.DMA((7,))],
               )(part)


def _row_tile(rows, cols):
    best = rows
    for t in range(16, rows, 16):
        if rows % t == 0 and t * cols * 4 <= (1 << 20):
            best = t
    return best


def adam_update(parts, w, m, v, name):
    P, R, C = parts.shape
    tr = _row_tile(R, C)

    def body(p_ref, w_ref, m_ref, v_ref, g_ref, d_ref, nm_ref, nv_ref):
        g = p_ref[0].astype(F32)
        for s in range(1, P):
            g = g + p_ref[s].astype(F32)
        new_m = ADAM_B1 * m_ref[...] + (1.0 - ADAM_B1) * g
        new_v = ADAM_B2 * v_ref[...] + (1.0 - ADAM_B2) * (g * g)
        m_hat = new_m / (1.0 - ADAM_B1 ** ADAM_STEP)
        v_hat = new_v / (1.0 - ADAM_B2 ** ADAM_STEP)
        g_ref[...] = g
        d_ref[...] = -ADAM_LR * (m_hat / (jnp.sqrt(v_hat) + ADAM_EPS) + ADAM_WD * w_ref[...])
        nm_ref[...] = new_m
        nv_ref[...] = new_v

    blk = pl.BlockSpec((tr, C), lambda i: (i, 0))
    out = _sds((R, C), F32)
    return _pc(body, name=name, out_shape=(out, out, out, out), grid=(R // tr,),
               in_specs=[pl.BlockSpec((P, tr, C), lambda i: (0, i, 0)), blk, blk, blk],
               out_specs=(blk, blk, blk, blk), sem=("parallel",))(parts, w, m, v)


WEIGHTS = ("ffn1_norm", "ffn1_w_gu", "ffn1_w_down", "mix_norm", "ffn2_norm", "ffn2_w_gu", "ffn2_w_down", "a_w_in",
           "a_w_conv", "a_A_log", "a_dt_bias", "a_out_norm", "a_w_out", "b_w_in", "b_b_in", "b_sinks", "b_w_out",
           "b_b_out", "final_norm")
SHARDED = ("ffn1_w_gu", "ffn1_w_down", "ffn2_w_gu", "ffn2_w_down", "a_w_in", "a_w_conv", "a_w_out", "b_w_in", "b_b_in",
           "b_w_out", "b_b_out")
SENT_AS_BF16 = ("ffn1_w_gu", "ffn1_w_down", "ffn2_w_gu", "ffn2_w_down", "a_w_in", "a_w_out", "b_w_in", "b_w_out")
MISC_LANES = dict(a_A_log=(0, 8), a_dt_bias=(8, 16), b_sinks=(16, 32), a_out_norm=(128, 256))
LOSS_LANE = 256


def _pack_small(t):
    misc = jnp.zeros((D,), F32)
    for key, (lo, hi) in MISC_LANES.items():
        misc = misc.at[lo:hi].set(t[key].reshape(-1))
    if "loss" in t:
        misc = misc.at[LOSS_LANE].set(t["loss"])
    return jnp.concatenate([t["ffn1_norm"], t["mix_norm"], t["ffn2_norm"], t["final_norm"].reshape(1, D), misc[None]], axis=0)


def _unpack_small(p, like):
    out = dict(ffn1_norm=p[0:2], mix_norm=p[2:4], ffn2_norm=p[4:6], final_norm=p[6])
    for key, (lo, hi) in MISC_LANES.items():
        out[key] = p[7, lo:hi].reshape(like[key].shape)
    return out


def kernel(x, ffn1_norm, ffn1_w_gu, ffn1_w_down, mix_norm, ffn2_norm, ffn2_w_gu, ffn2_w_down, a_w_in, a_w_conv, a_A_log, a_dt_bias, a_out_norm, a_w_out, b_w_in, b_b_in, b_sinks, b_w_out, b_b_out, final_norm, loss_target, m_ffn1_norm, m_ffn1_w_gu, m_ffn1_w_down, m_mix_norm, m_ffn2_norm, m_ffn2_w_gu, m_ffn2_w_down, m_a_w_in, m_a_w_conv, m_a_A_log, m_a_dt_bias, m_a_out_norm, m_a_w_out, m_b_w_in, m_b_b_in, m_b_sinks, m_b_w_out, m_b_b_out, m_final_norm, v_ffn1_norm, v_ffn1_w_gu, v_ffn1_w_down, v_mix_norm, v_ffn2_norm, v_ffn2_w_gu, v_ffn2_w_down, v_a_w_in, v_a_w_conv, v_a_A_log, v_a_dt_bias, v_a_out_norm, v_a_w_out, v_b_w_in, v_b_b_in, v_b_sinks, v_b_w_out, v_b_b_out, v_final_norm):
    w = dict(ffn1_norm=ffn1_norm, ffn1_w_gu=ffn1_w_gu, ffn1_w_down=ffn1_w_down, mix_norm=mix_norm, ffn2_norm=ffn2_norm, ffn2_w_gu=ffn2_w_gu, ffn2_w_down=ffn2_w_down, a_w_in=a_w_in, a_w_conv=a_w_conv, a_A_log=a_A_log, a_dt_bias=a_dt_bias, a_out_norm=a_out_norm, a_w_out=a_w_out, b_w_in=b_w_in, b_b_in=b_b_in, b_sinks=b_sinks, b_w_out=b_w_out, b_b_out=b_b_out, final_norm=final_norm)
    m = dict(ffn1_norm=m_ffn1_norm, ffn1_w_gu=m_ffn1_w_gu, ffn1_w_down=m_ffn1_w_down, mix_norm=m_mix_norm, ffn2_norm=m_ffn2_norm, ffn2_w_gu=m_ffn2_w_gu, ffn2_w_down=m_ffn2_w_down, a_w_in=m_a_w_in, a_w_conv=m_a_w_conv, a_A_log=m_a_A_log, a_dt_bias=m_a_dt_bias, a_out_norm=m_a_out_norm, a_w_out=m_a_w_out, b_w_in=m_b_w_in, b_b_in=m_b_b_in, b_sinks=m_b_sinks, b_w_out=m_b_w_out, b_b_out=m_b_b_out, final_norm=m_final_norm)
    v = dict(ffn1_norm=v_ffn1_norm, ffn1_w_gu=v_ffn1_w_gu, ffn1_w_down=v_ffn1_w_down, mix_norm=v_mix_norm, ffn2_norm=v_ffn2_norm, ffn2_w_gu=v_ffn2_w_gu, ffn2_w_down=v_ffn2_w_down, a_w_in=v_a_w_in, a_w_conv=v_a_w_conv, a_A_log=v_a_A_log, a_dt_bias=v_a_dt_bias, a_out_norm=v_a_out_norm, a_w_out=v_a_w_out, b_w_in=v_b_w_in, b_b_in=v_b_b_in, b_sinks=v_b_sinks, b_w_out=v_b_w_out, b_b_out=v_b_b_out, final_norm=v_final_norm)
    T = x.shape[1]
    x0, tgt = x.reshape(T, D), loss_target.reshape(T, D)

    shard_list = []
    for key in ("ffn1_w_gu", "ffn1_w_down", "ffn2_w_gu", "ffn2_w_down"):
        shard_list += [w[key][0].astype(BF16), w[key][1].astype(BF16)]
    shard_list += [a_w_in[0].astype(BF16), a_w_conv[0], a_w_out[0].astype(BF16), b_w_in[0].astype(BF16), b_b_in,
                   b_w_out[0].astype(BF16), b_b_out]
    g = all_gather(shard_list, "gather_weights")
    wgu = {("ffn1", 0): g[0], ("ffn1", 1): g[1], ("ffn2", 0): g[4], ("ffn2", 1): g[5]}
    wdn = {("ffn1", 0): g[2], ("ffn1", 1): g[3], ("ffn2", 0): g[6], ("ffn2", 1): g[7]}
    wdn = {k_: t.reshape(N_FB, FB, D) for k_, t in wdn.items()}
    a_in_cols = a_w_in.shape[-1] * N_DEV
    a_in_full = jnp.pad(g[8].transpose(1, 0, 2).reshape(D, a_in_cols), ((0, 0), (0, A_COLS - a_in_cols)))
    a_conv_full = g[9].transpose(1, 0, 2).reshape(4, 128 * N_QKV_BLK)
    a_out_full = g[10].reshape(D, D)
    b_in_full = g[11].transpose(1, 0, 2).reshape(D, B_COLS)
    b_bias_in = g[12].reshape(1, B_COLS)
    b_out_full = g[13].reshape(D, D)
    b_bias_out = g[14].reshape(1, D)
    a_log_row = jnp.zeros((1, 128), F32).at[0, HEADS_A:2 * HEADS_A].set(a_A_log[0])
    dt_row = jnp.zeros((1, 128), F32).at[0, HEADS_A:2 * HEADS_A].set(a_dt_bias[0])
    sink_row = jnp.zeros((1, 128), F32).at[0, :b_sinks.shape[1]].set(b_sinks[0])
    gdn_args = (mix_norm[0:1], a_in_full, a_conv_full, a_log_row, dt_row, a_out_norm, a_out_full)
    swa_args = (mix_norm[1:2], b_in_full, b_bias_in, sink_row, b_out_full, b_bias_out)

    xs, saved = x0, []
    for layer in range(2):
        xs, s1 = ffn_forward(xs, ffn1_norm[layer:layer + 1], wgu["ffn1", layer], wdn["ffn1", layer], f"l{layer}_ffn1")
        if layer == 0:
            xs, sm = gdn_forward(xs, *gdn_args, "gdn")
        else:
            xs, sm = swa_forward(xs, *swa_args, "swa")
        xs, s2 = ffn_forward(xs, ffn2_norm[layer:layer + 1], wgu["ffn2", layer], wdn["ffn2", layer], f"l{layer}_ffn2")
        saved.append((s1, sm, s2))
    loss_row, dx, d_final_norm = final_loss(xs, final_norm.reshape(1, D), tgt, "final_loss")

    gw = {}
    d_norm = {"ffn1_norm": [None, None], "mix_norm": [None, None], "ffn2_norm": [None, None]}
    d_gu = {"ffn1": [None, None], "ffn2": [None, None]}
    d_dn = {"ffn1": [None, None], "ffn2": [None, None]}
    for layer in (1, 0):
        s1, sm, s2 = saved[layer]
        dx, d_norm["ffn2_norm"][layer], d_gu["ffn2"][layer], d_dn["ffn2"][layer] = ffn_backward(
            dx, s2, ffn2_norm[layer:layer + 1], wgu["ffn2", layer], wdn["ffn2", layer], f"l{layer}_ffn2")
        if layer == 0:
            dx, d_norm["mix_norm"][0], d_a_in, d_a_conv, d_alog, d_dt, d_onorm, d_a_out = gdn_backward(dx, sm, *gdn_args, "gdn")
        else:
            dx, d_norm["mix_norm"][1], d_b_in, d_b_bias_in, d_sinks, d_b_out, d_b_bias_out = swa_backward(dx, sm, *swa_args, "swa")
        dx, d_norm["ffn1_norm"][layer], d_gu["ffn1"][layer], d_dn["ffn1"][layer] = ffn_backward(
            dx, s1, ffn1_norm[layer:layer + 1], wgu["ffn1", layer], wdn["ffn1", layer], f"l{layer}_ffn1")
    grad_x = dx.reshape(x.shape)

    part = {}
    for f in ("ffn1", "ffn2"):
        part[f + "_w_gu"] = jnp.stack(d_gu[f], axis=1)
        part[f + "_w_down"] = jnp.stack([t.reshape(N_DEV, FB // 2, D) for t in d_dn[f]], axis=1)
    part["a_w_in"] = d_a_in[:, :a_in_cols].reshape(D, N_DEV, -1).transpose(1, 0, 2)[:, None]
    part["a_w_conv"] = d_a_conv.reshape(4, N_DEV, -1).transpose(1, 0, 2)[:, None]
    part["a_w_out"] = d_a_out.reshape(N_DEV, 1, D // N_DEV, D)
    part["b_w_in"] = d_b_in.reshape(D, N_DEV, -1).transpose(1, 0, 2)[:, None]
    part["b_b_in"] = d_b_bias_in.reshape(N_DEV, 1, -1)
    part["b_w_out"] = d_b_out.reshape(N_DEV, 1, D // N_DEV, D)
    part["b_b_out"] = d_b_bias_out.reshape(N_DEV, 1, -1)
    send = [part[key].astype(BF16) if key in SENT_AS_BF16 else part[key] for key in SHARDED]
    recv = exchange_partials(send, "exchange_grads")

    grads, deltas, new_m, new_v = {}, {}, {}, {}
    for key, r in zip(SHARDED, recv):
        shape = w[key].shape
        cols = shape[-1]
        two_d = lambda t: t.reshape(-1, cols)
        out = adam_update(r.reshape(N_DEV, -1, cols), two_d(w[key]), two_d(m[key]), two_d(v[key]), f"adam_{key}")
        grads[key], deltas[key], new_m[key], new_v[key] = (t.reshape(shape) for t in out)

    small = dict(ffn1_norm=jnp.concatenate(d_norm["ffn1_norm"], axis=0), mix_norm=jnp.concatenate(d_norm["mix_norm"], axis=0),
                 ffn2_norm=jnp.concatenate(d_norm["ffn2_norm"], axis=0), final_norm=d_final_norm,
                 a_A_log=d_alog[0, HEADS_A:2 * HEADS_A], a_dt_bias=d_dt[0, HEADS_A:2 * HEADS_A],
                 b_sinks=d_sinks[0, :b_sinks.shape[1]], a_out_norm=d_onorm, loss=loss_row[0, 0])
    total = all_reduce_small(_pack_small(small), "allreduce_small")
    out = adam_update(total[None], _pack_small(w), _pack_small(m), _pack_small(v), "adam_small")
    for dst, packed in zip((grads, deltas, new_m, new_v), out):
        dst.update(_unpack_small(packed, w))
    loss = total[7, LOSS_LANE]

    return (loss, grad_x, *[grads[k_] for k_ in WEIGHTS], *[deltas[k_] for k_ in WEIGHTS],
            *[new_m[k_] for k_ in WEIGHTS], *[new_v[k_] for k_ in WEIGHTS])
```

```python
import functools

import jax
import jax.numpy as jnp
from jax import lax
from jax.experimental import pallas as pl
from jax.experimental.pallas import tpu as pltpu

F32, BF16 = jnp.float32, jnp.bfloat16
HI = lax.Precision.HIGHEST
EPS = 1e-6

N_DEV = 8
D = 1024
FB = 704
N_FB = 4
HEADS_A, DK = 8, 128
CHUNK = 64
PREP_T = 512
A_COLS = 4224
B_HD, B_BLK = 64, 128
VMEM_LIMIT_V7X = 60 * 1024 * 1024

ADAM_LR, ADAM_B1, ADAM_B2, ADAM_EPS, ADAM_WD, ADAM_STEP = 0.001, 0.9, 0.999, 1e-08, 0.01, 10

NT = (((1,), (1,)), ((), ()))
TN = (((0,), (0,)), ((), ()))


def _pc(body, *, name, out_shape, grid=(), in_specs=None, out_specs=None, scratch=(), sem=None, **kw):
    params = pltpu.CompilerParams(dimension_semantics=sem, vmem_limit_bytes=VMEM_LIMIT_V7X)
    return pl.pallas_call(body, name=name, out_shape=out_shape, grid=grid, in_specs=in_specs, out_specs=out_specs,
                          scratch_shapes=list(scratch), compiler_params=params, **kw)


def _sds(shape, dtype):
    return jax.ShapeDtypeStruct(tuple(shape), dtype)


def _dot(a, b, dims=None, precision=None):
    if dims is None:
        return jnp.dot(a, b, preferred_element_type=F32, precision=precision)
    return lax.dot_general(a, b, dims, preferred_element_type=F32, precision=precision)


def _sigmoid(x):
    return 1.0 / (1.0 + jnp.exp(-x))


def _softplus(x):
    return jnp.maximum(x, 0.0) + jnp.log(1.0 + jnp.exp(-jnp.abs(x)))


def _rms_fwd(x, w):
    r = lax.rsqrt(jnp.mean(x * x, axis=-1, keepdims=True) + EPS)
    return x * r * w


def _rms_bwd(x, w, dy):
    r = lax.rsqrt(jnp.mean(x * x, axis=-1, keepdims=True) + EPS)
    xh = x * r
    dxh = dy * w
    dx = r * (dxh - xh * jnp.mean(dxh * xh, axis=-1, keepdims=True))
    return dx, jnp.sum(dy * xh, axis=0, keepdims=True)


def _tile(n, want):
    t = min(n, want)
    assert n % t == 0, (n, want)
    return t


def rmsnorm_bf16(x, w, name, deps=()):
    T = x.shape[0]
    tm = _tile(T, 1024)

    def body(x_ref, w_ref, *rest):
        rest[-1][...] = _rms_fwd(x_ref[...], w_ref[...]).astype(BF16)

    return _pc(body, name=name, out_shape=_sds((T, D), BF16), grid=(T // tm,),
               in_specs=[pl.BlockSpec((tm, D), lambda i: (i, 0)), pl.BlockSpec((1, D), lambda i: (0, 0))] + [DEP_SPEC] * len(deps),
               out_specs=pl.BlockSpec((tm, D), lambda i: (i, 0)), sem=("parallel",))(x, w, *deps)


def rmsnorm_bwd_add(x, w, dxn, dres, name):
    T = x.shape[0]
    tm = _tile(T, 512)

    def body(x_ref, w_ref, dxn_ref, dres_ref, dx_ref, dw_ref):
        dx, dw = _rms_bwd(x_ref[...], w_ref[...], dxn_ref[...])
        dx_ref[...] = dres_ref[...] + dx

        @pl.when(pl.program_id(0) == 0)
        def _():
            dw_ref[...] = jnp.zeros_like(dw_ref)
        dw_ref[...] += dw

    row = pl.BlockSpec((tm, D), lambda i: (i, 0))
    vec = pl.BlockSpec((1, D), lambda i: (0, 0))
    return _pc(body, name=name, out_shape=(_sds((T, D), F32), _sds((1, D), F32)), grid=(T // tm,),
               in_specs=[row, vec, row, row], out_specs=(row, vec), sem=("arbitrary",))(x, w, dxn, dres)


def final_loss(x, w, tgt, name):
    T = x.shape[0]
    tm = _tile(T, 512)

    def body(x_ref, w_ref, t_ref, loss_ref, dx_ref, dw_ref):
        xv, wv = x_ref[...], w_ref[...]
        err = _rms_fwd(xv, wv) - t_ref[...]
        dx, dw = _rms_bwd(xv, wv, err * (1.0 / D))
        dx_ref[...] = dx

        @pl.when(pl.program_id(0) == 0)
        def _():
            dw_ref[...] = jnp.zeros_like(dw_ref)
            loss_ref[...] = jnp.zeros_like(loss_ref)
        dw_ref[...] += dw
        loss_ref[...] += jnp.full((1, 128), 0.5 / D, F32) * jnp.sum(err * err)

    row = pl.BlockSpec((tm, D), lambda i: (i, 0))
    vec = pl.BlockSpec((1, D), lambda i: (0, 0))
    return _pc(body, name=name, out_shape=(_sds((1, 128), F32), _sds((T, D), F32), _sds((1, D), F32)),
               grid=(T // tm,), in_specs=[row, vec, row],
               out_specs=(pl.BlockSpec((1, 128), lambda i: (0, 0)), row, vec), sem=("arbitrary",))(x, w, tgt)


def _col_tile(n):
    for t in (1536, 1408, 1024, 768, 512, 384, 256, 128):
        if n % t == 0:
            return t
    return n


def mm_nn(a, b, name, bias=None, residual=None, out_dtype=F32):
    T, K = a.shape
    N = b.shape[1]
    tm, tn = _tile(T, 512), _col_tile(N)

    def body(a_ref, b_ref, *rest):
        o_ref = rest[-1]
        acc = _dot(a_ref[...].astype(BF16), b_ref[...])
        for extra in rest[:-1]:
            acc = acc + extra[...]
        o_ref[...] = acc.astype(out_dtype)

    in_specs = [pl.BlockSpec((tm, K), lambda j, i: (i, 0)), pl.BlockSpec((K, tn), lambda j, i: (0, j))]
    args = [a, b]
    if bias is not None:
        in_specs.append(pl.BlockSpec((1, tn), lambda j, i: (0, j)))
        args.append(bias)
    if residual is not None:
        in_specs.append(pl.BlockSpec((tm, tn), lambda j, i: (i, j)))
        args.append(residual)
    return _pc(body, name=name, out_shape=_sds((T, N), out_dtype), grid=(N // tn, T // tm), in_specs=in_specs,
               out_specs=pl.BlockSpec((tm, tn), lambda j, i: (i, j)), sem=("parallel", "parallel"))(*args)


def mm_nt(a, b, name, out_dtype=F32):
    T, N = a.shape
    K = b.shape[0]
    tm = _tile(T, 512)

    def body(a_ref, b_ref, o_ref):
        o_ref[...] = _dot(a_ref[...].astype(BF16), b_ref[...], NT).astype(out_dtype)

    return _pc(body, name=name, out_shape=_sds((T, K), out_dtype), grid=(T // tm,),
               in_specs=[pl.BlockSpec((tm, N), lambda i: (i, 0)), pl.BlockSpec((K, N), lambda i: (0, 0))],
               out_specs=pl.BlockSpec((tm, K), lambda i: (i, 0)), sem=("parallel",))(a, b)


def mm_tn(a, b, name):
    T, K = a.shape
    N = b.shape[1]
    tt, tn = _tile(T, 1024), _col_tile(N)

    def body(a_ref, b_ref, o_ref):
        @pl.when(pl.program_id(1) == 0)
        def _():
            o_ref[...] = jnp.zeros_like(o_ref)
        o_ref[...] += _dot(a_ref[...].astype(BF16), b_ref[...].astype(BF16), TN)

    return _pc(body, name=name, out_shape=_sds((K, N), F32), grid=(N // tn, T // tt),
               in_specs=[pl.BlockSpec((tt, K), lambda j, t: (t, 0)), pl.BlockSpec((tt, tn), lambda j, t: (t, j))],
               out_specs=pl.BlockSpec((K, tn), lambda j, t: (0, j)), sem=("parallel", "arbitrary"))(a, b)


def ffn_up(xn, wgu, name):
    T = xn.shape[0]
    tm = _tile(T, 512)

    def body(x_ref, w_ref, gu_ref):
        xv = x_ref[...]
        for j in range(2 * N_FB):
            gu_ref[j] = _dot(xv, w_ref[j]).astype(BF16)

    return _pc(body, name=name, out_shape=_sds((2 * N_FB, T, FB), BF16), grid=(T // tm,),
               in_specs=[pl.BlockSpec((tm, D), lambda i: (i, 0)), pl.BlockSpec((2 * N_FB, D, FB), lambda i: (0, 0, 0))],
               out_specs=pl.BlockSpec((2 * N_FB, tm, FB), lambda i: (0, i, 0)), sem=("parallel",))(xn, wgu)


def ffn_down(gu, wd, x, name):
    T = x.shape[0]
    tm = _tile(T, 512)

    def body(gu_ref, w_ref, x_ref, o_ref):
        acc = jnp.zeros((tm, D), F32)
        for g in range(N_FB):
            gate, up = gu_ref[g].astype(F32), gu_ref[N_FB + g].astype(F32)
            act = (gate * _sigmoid(gate) * up).astype(BF16)
            acc = acc + _dot(act, w_ref[g])
        o_ref[...] = x_ref[...] + 0.5 * acc

    row = pl.BlockSpec((tm, D), lambda i: (i, 0))
    return _pc(body, name=name, out_shape=_sds((T, D), F32), grid=(T // tm,),
               in_specs=[pl.BlockSpec((2 * N_FB, tm, FB), lambda i: (0, i, 0)),
                         pl.BlockSpec((N_FB, FB, D), lambda i: (0, 0, 0)), row],
               out_specs=row, sem=("parallel",))(gu, wd, x)


def ffn_bwd_hidden(dout, wd, gu, name, deps=()):
    T = dout.shape[0]
    tm = _tile(T, 512)

    def body(d_ref, w_ref, gu_ref, *rest):
        dgu_ref, act_ref = rest[-2:]
        dy = (0.5 * d_ref[...]).astype(BF16)
        for g in range(N_FB):
            gate, up = gu_ref[g].astype(F32), gu_ref[N_FB + g].astype(F32)
            sg = _sigmoid(gate)
            silu = gate * sg
            dact = _dot(dy, w_ref[g], NT)
            act_ref[g] = (silu * up).astype(BF16)
            dgu_ref[g] = (dact * up * (sg * (1.0 + gate * (1.0 - sg)))).astype(BF16)
            dgu_ref[N_FB + g] = (dact * silu).astype(BF16)

    return _pc(body, name=name, out_shape=(_sds((2 * N_FB, T, FB), BF16), _sds((N_FB, T, FB), BF16)), grid=(T // tm,),
               in_specs=[pl.BlockSpec((tm, D), lambda i: (i, 0)), pl.BlockSpec((N_FB, FB, D), lambda i: (0, 0, 0)),
                         pl.BlockSpec((2 * N_FB, tm, FB), lambda i: (0, i, 0))] + [DEP_SPEC] * len(deps),
               out_specs=(pl.BlockSpec((2 * N_FB, tm, FB), lambda i: (0, i, 0)),
                          pl.BlockSpec((N_FB, tm, FB), lambda i: (0, i, 0))), sem=("parallel",))(dout, wd, gu, *deps)


def ffn_bwd_input(dgu, wgu, x, dout, nw, name):
    T = x.shape[0]
    tm = _tile(T, 512)

    def body(dgu_ref, w_ref, x_ref, d_ref, nw_ref, dx_ref, dnw_ref):
        dxn = jnp.zeros((tm, D), F32)
        for j in range(2 * N_FB):
            dxn = dxn + _dot(dgu_ref[j], w_ref[j], NT)
        dx, dw = _rms_bwd(x_ref[...], nw_ref[...], dxn)
        dx_ref[...] = d_ref[...] + dx

        @pl.when(pl.program_id(0) == 0)
        def _():
            dnw_ref[...] = jnp.zeros_like(dnw_ref)
        dnw_ref[...] += dw

    row = pl.BlockSpec((tm, D), lambda i: (i, 0))
    vec = pl.BlockSpec((1, D), lambda i: (0, 0))
    return _pc(body, name=name, out_shape=(_sds((T, D), F32), _sds((1, D), F32)), grid=(T // tm,),
               in_specs=[pl.BlockSpec((2 * N_FB, tm, FB), lambda i: (0, i, 0)),
                         pl.BlockSpec((2 * N_FB, D, FB), lambda i: (0, 0, 0)), row, row, vec],
               out_specs=(row, vec), sem=("arbitrary",))(dgu, wgu, x, dout, nw)


def ffn_wgrad_gu(xn, dgu, name):
    T = xn.shape[0]
    tt = _tile(T, 1024)

    def body(x_ref, d_ref, o_ref):
        @pl.when(pl.program_id(1) == 0)
        def _():
            o_ref[...] = jnp.zeros_like(o_ref)
        o_ref[0] += _dot(x_ref[...], d_ref[0], TN)

    return _pc(body, name=name, out_shape=_sds((2 * N_FB, D, FB), F32), grid=(2 * N_FB, T // tt),
               in_specs=[pl.BlockSpec((tt, D), lambda j, t: (t, 0)), pl.BlockSpec((1, tt, FB), lambda j, t: (j, t, 0))],
               out_specs=pl.BlockSpec((1, D, FB), lambda j, t: (j, 0, 0)), sem=("parallel", "arbitrary"))(xn, dgu)


def ffn_wgrad_down(act, dout, name):
    T = dout.shape[0]
    tt = _tile(T, 1024)

    def body(a_ref, d_ref, o_ref):
        @pl.when(pl.program_id(1) == 0)
        def _():
            o_ref[...] = jnp.zeros_like(o_ref)
        o_ref[0] += _dot(a_ref[0], (0.5 * d_ref[...]).astype(BF16), TN)

    return _pc(body, name=name, out_shape=_sds((N_FB, FB, D), F32), grid=(N_FB, T // tt),
               in_specs=[pl.BlockSpec((1, tt, FB), lambda g, t: (g, t, 0)), pl.BlockSpec((tt, D), lambda g, t: (t, 0))],
               out_specs=pl.BlockSpec((1, FB, D), lambda g, t: (g, 0, 0)), sem=("parallel", "arbitrary"))(act, dout)


def ffn_forward(x, nw, wgu, wd, tag, deps=()):
    xn = rmsnorm_bf16(x, nw, f"{tag}_norm", deps)
    gu = ffn_up(xn, wgu, f"{tag}_up")
    return ffn_down(gu, wd, x, f"{tag}_down"), (x, xn, gu)


def ffn_backward(dout, saved, nw, wgu, wd, tag, deps=()):
    x, xn, gu = saved
    dgu, act = ffn_bwd_hidden(dout, wd, gu, f"{tag}_bwd_hidden", deps)
    dwd = ffn_wgrad_down(act, dout, f"{tag}_wgrad_down")
    dwgu = ffn_wgrad_gu(xn, dgu, f"{tag}_wgrad_gu")
    dx, dnw = ffn_bwd_input(dgu, wgu, x, dout, nw, f"{tag}_bwd_input")
    return dx, dnw, dwgu, dwd


N_QKV_BLK = 3 * HEADS_A
Z_BLK0 = N_QKV_BLK
BA_BLK = A_COLS // 128 - 1


def _conv_taps(xcat, w):
    c = xcat[8:] * w[3:4]
    for k in range(3):
        c = c + pltpu.roll(xcat, 3 - k, 0)[8:] * w[k:k + 1]
    return c


def _head_cols(h):
    return slice(128 * h, 128 * (h + 1))


def gdn_conv_fwd(proj, wconv, name):
    T = proj.shape[0]
    tm = _tile(T, 512)

    def body(cur_ref, prev_ref, w_ref, c_ref, y_ref):
        kind, t = pl.program_id(0), pl.program_id(1)
        prev = jnp.where(t > 0, prev_ref[...], 0.0)
        c = _conv_taps(jnp.concatenate([prev, cur_ref[...]], axis=0), w_ref[...])
        c_ref[...] = c
        s = c * _sigmoid(c)
        scale = jnp.where(kind == 0, DK ** -0.5, 1.0)
        for h in range(HEADS_A):
            sh = s[:, _head_cols(h)]
            r = lax.rsqrt(jnp.sum(sh * sh, axis=-1, keepdims=True) + EPS)
            y_ref[h] = sh * jnp.where(kind < 2, r * scale, 1.0)

    return _pc(body, name=name, out_shape=(_sds((T, 3 * D), F32), _sds((N_QKV_BLK, T, 128), F32)),
               grid=(3, T // tm),
               in_specs=[pl.BlockSpec((tm, D), lambda kd, t: (t, kd)),
                         pl.BlockSpec((8, D), lambda kd, t: (jnp.maximum(t * (tm // 8) - 1, 0), kd)),
                         pl.BlockSpec((4, D), lambda kd, t: (0, kd))],
               out_specs=(pl.BlockSpec((tm, D), lambda kd, t: (t, kd)),
                          pl.BlockSpec((HEADS_A, tm, 128), lambda kd, t: (kd, t, 0))),
               sem=("parallel", "parallel"))(proj, proj, wconv)


def gdn_conv_bwd(dqkv, c, proj, wconv, name):
    T = c.shape[0]
    tm = _tile(T, 512)
    n_t = T // tm

    def body(dy_ref, dyn_ref, c_ref, cn_ref, x_ref, xp_ref, w_ref, dx_ref, dw_ref):
        kind, t = pl.program_id(0), pl.program_id(1)
        scale = jnp.where(kind == 0, DK ** -0.5, 1.0)

        def act_bwd(dy, cv):
            sg = _sigmoid(cv)
            s = cv * sg
            parts = []
            for h in range(HEADS_A):
                sh, dyh = s[:, _head_cols(h)], dy[h]
                r = lax.rsqrt(jnp.sum(sh * sh, axis=-1, keepdims=True) + EPS)
                ds_norm = scale * r * (dyh - (r * r) * sh * jnp.sum(dyh * sh, axis=-1, keepdims=True))
                parts.append(jnp.where(kind < 2, ds_norm, dyh))
            return jnp.concatenate(parts, axis=1) * (sg * (1.0 + cv * (1.0 - sg)))

        w = w_ref[...]
        dcur = act_bwd(dy_ref[...], c_ref[...])
        dnext = jnp.where(t < n_t - 1, act_bwd(dyn_ref[...], cn_ref[...]), 0.0)
        dcat = jnp.concatenate([dcur, dnext], axis=0)
        dx = dcur * w[3:4]
        for k in range(3):
            dx = dx + pltpu.roll(dcat, tm + 8 - (3 - k), 0)[:tm] * w[k:k + 1]
        dx_ref[...] = dx.astype(BF16)
        xprev = jnp.where(t > 0, xp_ref[...], 0.0)
        xcat = jnp.concatenate([xprev, x_ref[...]], axis=0)
        rows = [jnp.sum(dcur * pltpu.roll(xcat, 3 - k, 0)[8:], axis=0, keepdims=True) for k in range(3)]
        rows.append(jnp.sum(dcur * xcat[8:], axis=0, keepdims=True))

        @pl.when(t == 0)
        def _():
            dw_ref[...] = jnp.zeros_like(dw_ref)
        dw_ref[...] += jnp.concatenate(rows, axis=0)

    def nxt(t):
        return jnp.minimum((t + 1) * (tm // 8), T // 8 - 1)

    cur = pl.BlockSpec((tm, D), lambda kd, t: (t, kd))
    return _pc(body, name=name, out_shape=(_sds((T, 3 * D), BF16), _sds((4, 3 * D), F32)), grid=(3, n_t),
               in_specs=[pl.BlockSpec((HEADS_A, tm, 128), lambda kd, t: (kd, t, 0)),
                         pl.BlockSpec((HEADS_A, 8, 128), lambda kd, t: (kd, nxt(t), 0)),
                         cur, pl.BlockSpec((8, D), lambda kd, t: (nxt(t), kd)),
                         cur, pl.BlockSpec((8, D), lambda kd, t: (jnp.maximum(t * (tm // 8) - 1, 0), kd)),
                         pl.BlockSpec((4, D), lambda kd, t: (0, kd))],
               out_specs=(cur, pl.BlockSpec((4, D), lambda kd, t: (0, kd))),
               sem=("parallel", "arbitrary"))(dqkv, dqkv, c, c, proj, proj, wconv)


def _chunk_masks(n):
    ri = lax.broadcasted_iota(jnp.int32, (n, n), 0)
    ci = lax.broadcasted_iota(jnp.int32, (n, n), 1)
    same = (ri // CHUNK) == (ci // CHUNK)
    return same & (ri >= ci), same & (ri <= ci)


def gdn_gate_fwd(proj, al, dtb, name):
    T = proj.shape[0]
    tg = _tile(T, PREP_T)

    def body(ba_ref, al_ref, dtb_ref, o_ref):
        x = ba_ref[...]
        lane = lax.broadcasted_iota(jnp.int32, x.shape, 1)
        is_a = (lane >= HEADS_A) & (lane < 2 * HEADS_A)
        g = jnp.where(is_a, -jnp.exp(al_ref[...]) * _softplus(x + dtb_ref[...]), 0.0)
        lower, _ = _chunk_masks(tg)
        gc = _dot(lower.astype(F32), g, precision=HI)
        o_ref[...] = jnp.where(lane < HEADS_A, _sigmoid(x), gc)

    vec = pl.BlockSpec((1, 128), lambda i: (0, 0))
    return _pc(body, name=name, out_shape=_sds((T, 128), F32), grid=(T // tg,),
               in_specs=[pl.BlockSpec((tg, 128), lambda i: (i, BA_BLK)), vec, vec],
               out_specs=pl.BlockSpec((tg, 128), lambda i: (i, 0)), sem=("parallel",))(proj, al, dtb)


def gdn_gate_bwd(proj, al, dtb, dgb, name):
    T = proj.shape[0]
    tg = _tile(T, PREP_T)

    def body(ba_ref, al_ref, dtb_ref, dgb_ref, dba_ref, dal_ref, ddt_ref):
        x, d = ba_ref[...], dgb_ref[...]
        lane = lax.broadcasted_iota(jnp.int32, x.shape, 1)
        is_b = lane < HEADS_A
        is_a = (lane >= HEADS_A) & (lane < 2 * HEADS_A)
        beta = _sigmoid(x)
        e_a = jnp.exp(al_ref[...])
        z = x + dtb_ref[...]
        g = jnp.where(is_a, -e_a * _softplus(z), 0.0)
        _, upper = _chunk_masks(tg)
        dg = _dot(upper.astype(F32), jnp.where(is_a, d, 0.0), precision=HI)
        da = jnp.where(is_a, dg * (-e_a) * _sigmoid(z), 0.0)
        db = jnp.where(is_b, d * beta * (1.0 - beta), 0.0)
        dba_ref[...] = (da + db).astype(BF16)

        @pl.when(pl.program_id(0) == 0)
        def _():
            dal_ref[...] = jnp.zeros_like(dal_ref)
            ddt_ref[...] = jnp.zeros_like(ddt_ref)
        dal_ref[...] += jnp.sum(dg * g, axis=0, keepdims=True)
        ddt_ref[...] += jnp.sum(da, axis=0, keepdims=True)

    vec = pl.BlockSpec((1, 128), lambda i: (0, 0))
    blk = pl.BlockSpec((tg, 128), lambda i: (i, 0))
    return _pc(body, name=name, out_shape=(_sds((T, 128), BF16), _sds((1, 128), F32), _sds((1, 128), F32)),
               grid=(T // tg,), in_specs=[pl.BlockSpec((tg, 128), lambda i: (i, BA_BLK)), vec, vec, blk],
               out_specs=(blk, vec, vec), sem=("arbitrary",))(proj, al, dtb, dgb)


def _bmm(a, b, dims, precision=None):
    return lax.dot_general(a, b, dims, preferred_element_type=F32, precision=precision)


B_NN = (((2,), (1,)), ((0,), (0,)))
B_NT = (((2,), (2,)), ((0,), (0,)))


def _select_lane(x, lane_index):
    lane = lax.broadcasted_iota(jnp.int32, x.shape, x.ndim - 1)
    return jnp.sum(jnp.where(lane == lane_index, x, 0.0), axis=-1, keepdims=True)


B_TN = (((1,), (1,)), ((0,), (0,)))


def _bmm_split(a, b, dims):
    ah, bh = a.astype(BF16), b.astype(BF16)
    al, bl = (a - ah.astype(F32)).astype(BF16), (b - bh.astype(F32)).astype(BF16)
    return _bmm(ah, bh, dims) + (_bmm(ah, bl, dims) + _bmm(al, bh, dims))


@jax.custom_vjp
def _bmm_f32(a, b):
    return _bmm_split(a, b, B_NN)


def _bmm_f32_fwd(a, b):
    return _bmm_split(a, b, B_NN), (a, b)


def _bmm_f32_bwd(res, dc):
    a, b = res
    return _bmm_split(dc, b, B_NT), _bmm_split(a, dc, B_TN)


_bmm_f32.defvjp(_bmm_f32_fwd, _bmm_f32_bwd)


def _tri_inverse(lmat):
    ri = lax.broadcasted_iota(jnp.int32, lmat.shape, 1)
    ci = lax.broadcasted_iota(jnp.int32, lmat.shape, 2)
    inv = jnp.where(ri == ci, 1.0, 0.0) - lmat
    power = lmat
    for _ in range(5):
        power = _bmm_split(power, power, B_NN)
        inv = inv + _bmm_split(inv, power, B_NN)
    return inv


def _stored_inverse(x):
    @jax.custom_vjp
    def inverse(lmat):
        return x

    def fwd(lmat):
        return x, None

    def bwd(_, dx):
        return (-_bmm_split(_bmm_split(x, dx, B_TN), x, B_NT),)

    inverse.defvjp(fwd, bwd)
    return inverse


def _gdn_prep(q, k, v, gb, h, inverse):
    nb = q.shape[0]
    beta = _select_lane(gb, h)
    gc = _select_lane(gb, HEADS_A + h)
    ri = lax.broadcasted_iota(jnp.int32, (nb, CHUNK, CHUNK), 1)
    ci = lax.broadcasted_iota(jnp.int32, (nb, CHUNK, CHUNK), 2)
    lower, strict, eye = ri >= ci, ri > ci, ri == ci
    gcol = jnp.broadcast_to(gc, (nb, CHUNK, CHUNK))
    grow = _bmm_f32(jnp.ones((nb, CHUNK, CHUNK), F32), jnp.where(eye, gcol, 0.0))
    decay = jnp.where(lower, jnp.exp(jnp.where(lower, gcol - grow, 0.0)), 0.0)
    kb = k * beta
    kbf = k.astype(BF16)
    inv = inverse(jnp.where(strict, _bmm(kb.astype(BF16), kbf, B_NT) * decay, 0.0))
    eg = jnp.exp(gc)
    sol = _bmm_f32(inv, jnp.concatenate([v * beta, kb * eg], axis=-1))
    aqk = _bmm(q.astype(BF16), kbf, B_NT) * decay
    g_last = gc[:, CHUNK - 1:CHUNK, :]
    gl = jnp.broadcast_to(jnp.exp(g_last), (nb, 1, 128))
    return (sol[..., :DK], sol[..., DK:], q * eg, k * jnp.exp(g_last - gc), aqk, gl), inv


def gdn_prep_fwd(qkv, gb, name):
    T = qkv.shape[1]
    tp = _tile(T, PREP_T)
    nb = tp // CHUNK

    def body(q_ref, k_ref, v_ref, gb_ref, u_ref, w_ref, qd_ref, kd_ref, a_ref, gl_ref, inv_ref):
        h = pl.program_id(1)
        shp = (nb, CHUNK, 128)
        (u, w, qd, kd, aqk, gl), inv = _gdn_prep(q_ref[0].reshape(shp), k_ref[0].reshape(shp), v_ref[0].reshape(shp),
                                                 gb_ref[...].reshape(shp), h, _tri_inverse)
        u_ref[0] = u.reshape(tp, 128)
        w_ref[0] = w.reshape(tp, 128)
        qd_ref[0] = qd.reshape(tp, 128)
        kd_ref[0] = kd.reshape(tp, 128)
        a_ref[0] = aqk.reshape(tp, CHUNK)
        gl_ref[0] = gl.reshape(nb, 1, 128)
        inv_ref[0] = inv.reshape(tp, CHUNK)

    def head(off):
        return pl.BlockSpec((1, tp, 128), lambda n, h: (h + off, n, 0))

    per_head = _sds((HEADS_A, T, 128), F32)
    narrow = pl.BlockSpec((1, tp, CHUNK), lambda n, h: (h, n, 0))
    return _pc(body, name=name,
               out_shape=(per_head, per_head, per_head, per_head, _sds((HEADS_A, T, CHUNK), F32),
                          _sds((HEADS_A, T // CHUNK, 1, 128), F32), _sds((HEADS_A, T, CHUNK), F32)),
               grid=(T // tp, HEADS_A),
               in_specs=[head(0), head(HEADS_A), head(2 * HEADS_A), pl.BlockSpec((tp, 128), lambda n, h: (n, 0))],
               out_specs=(head(0), head(0), head(0), head(0), narrow,
                          pl.BlockSpec((1, nb, 1, 128), lambda n, h: (h, n, 0, 0)), narrow),
               sem=("parallel", "parallel"))(qkv, qkv, qkv, gb)


def gdn_prep_bwd(qkv, gb, inv, du, dw, dqd, dkd, da, dgl, name):
    T = qkv.shape[1]
    tp = _tile(T, PREP_T)
    nb = tp // CHUNK

    def body(q_ref, k_ref, v_ref, gb_ref, inv_ref, du_ref, dw_ref, dqd_ref, dkd_ref, da_ref, dgl_ref, dqkv_ref, dgb_ref):
        h = pl.program_id(1)
        shp = (nb, CHUNK, 128)
        stored = _stored_inverse(inv_ref[0].reshape(nb, CHUNK, CHUNK))
        _, vjp = jax.vjp(lambda q, k, v, gb: _gdn_prep(q, k, v, gb, h, stored)[0], q_ref[0].reshape(shp),
                         k_ref[0].reshape(shp), v_ref[0].reshape(shp), gb_ref[...].reshape(shp))
        dq, dk, dv, dgb = vjp((du_ref[0].reshape(shp), dw_ref[0].reshape(shp), dqd_ref[0].reshape(shp),
                               dkd_ref[0].reshape(shp), da_ref[0].reshape(nb, CHUNK, CHUNK), dgl_ref[0].reshape(nb, 1, 128)))
        dqkv_ref[h] = dq.reshape(tp, 128)
        dqkv_ref[HEADS_A + h] = dk.reshape(tp, 128)
        dqkv_ref[2 * HEADS_A + h] = dv.reshape(tp, 128)

        @pl.when(h == 0)
        def _():
            dgb_ref[...] = jnp.zeros_like(dgb_ref)
        dgb_ref[...] += dgb.reshape(tp, 128)

    def head(off):
        return pl.BlockSpec((1, tp, 128), lambda n, h: (h + off, n, 0))

    narrow = pl.BlockSpec((1, tp, CHUNK), lambda n, h: (h, n, 0))
    return _pc(body, name=name, out_shape=(_sds((N_QKV_BLK, T, 128), F32), _sds((T, 128), F32)),
               grid=(T // tp, HEADS_A),
               in_specs=[head(0), head(HEADS_A), head(2 * HEADS_A), pl.BlockSpec((tp, 128), lambda n, h: (n, 0)), narrow,
                         head(0), head(0), head(0), head(0), narrow,
                         pl.BlockSpec((1, nb, 1, 128), lambda n, h: (h, n, 0, 0))],
               out_specs=(pl.BlockSpec((N_QKV_BLK, tp, 128), lambda n, h: (0, n, 0)),
                          pl.BlockSpec((tp, 128), lambda n, h: (n, 0))),
               sem=("parallel", "arbitrary"))(qkv, qkv, qkv, gb, inv, du, dw, dqd, dkd, da, dgl)


def gdn_scan_fwd(u, w, qd, kd, aqk, gl, name):
    T = u.shape[1]
    n_chunks = T // CHUNK

    def body(u_ref, w_ref, qd_ref, kd_ref, a_ref, gl_ref, o_ref, sin_ref, state):
        @pl.when(pl.program_id(0) == 0)
        def _():
            state[...] = jnp.zeros_like(state)
        for h in range(HEADS_A):
            s = state[h]
            sin_ref[0, h] = s
            sb = s.astype(BF16)
            both = _dot(jnp.concatenate([w_ref[h], qd_ref[h]], axis=0).astype(BF16), sb)
            vn = (u_ref[h] - both[:CHUNK]).astype(BF16)
            o_ref[h] = both[CHUNK:] + _dot(a_ref[h].astype(BF16), vn)
            state[h] = s * gl_ref[h, 0] + _dot(kd_ref[h].astype(BF16), vn, TN)

    blk = pl.BlockSpec((HEADS_A, CHUNK, 128), lambda n: (0, n, 0))
    return _pc(body, name=name,
               out_shape=(_sds((HEADS_A, T, 128), F32), _sds((n_chunks, HEADS_A, DK, 128), F32)), grid=(n_chunks,),
               in_specs=[blk, blk, blk, blk, pl.BlockSpec((HEADS_A, CHUNK, CHUNK), lambda n: (0, n, 0)),
                         pl.BlockSpec((HEADS_A, 1, 1, 128), lambda n: (0, n, 0, 0))],
               out_specs=(blk, pl.BlockSpec((1, HEADS_A, DK, 128), lambda n: (n, 0, 0, 0))),
               scratch=[pltpu.VMEM((HEADS_A, DK, 128), F32)], sem=("arbitrary",))(u, w, qd, kd, aqk, gl)


def gdn_scan_bwd(u, w, qd, kd, aqk, gl, sin, do, name):
    T = u.shape[1]
    n_chunks = T // CHUNK

    def body(u_ref, w_ref, qd_ref, kd_ref, a_ref, gl_ref, sin_ref, do_ref,
             du_ref, dw_ref, dqd_ref, dkd_ref, da_ref, dgl_ref, dstate):
        @pl.when(pl.program_id(0) == 0)
        def _():
            dstate[...] = jnp.zeros_like(dstate)
        lane0 = lax.broadcasted_iota(jnp.int32, (1, 128), 1) == 0
        for h in range(HEADS_A):
            s = sin_ref[0, h]
            sb = s.astype(BF16)
            wb, qdb, kdb = w_ref[h].astype(BF16), qd_ref[h].astype(BF16), kd_ref[h].astype(BF16)
            ab, dob = a_ref[h].astype(BF16), do_ref[h].astype(BF16)
            vn = (u_ref[h] - _dot(wb, sb)).astype(BF16)
            ds_out = dstate[h]
            dsb = ds_out.astype(BF16)
            dqd_ref[h] = _dot(dob, sb, NT)
            da_ref[h] = _dot(dob, vn, NT)
            dv = _dot(ab, dob, TN) + _dot(kdb, dsb)
            dkd_ref[h] = _dot(vn, dsb, NT)
            dgl_ref[h, 0] = jnp.where(lane0, jnp.sum(ds_out * s), 0.0)
            du_ref[h] = dv
            dvb = dv.astype(BF16)
            dw_ref[h] = -_dot(dvb, sb, NT)
            dstate[h] = ds_out * gl_ref[h, 0] + _dot(qdb, dob, TN) - _dot(wb, dvb, TN)

    last = n_chunks - 1
    blk = pl.BlockSpec((HEADS_A, CHUNK, 128), lambda n: (0, last - n, 0))
    ablk = pl.BlockSpec((HEADS_A, CHUNK, CHUNK), lambda n: (0, last - n, 0))
    glblk = pl.BlockSpec((HEADS_A, 1, 1, 128), lambda n: (0, last - n, 0, 0))
    per_head = _sds((HEADS_A, T, 128), F32)
    return _pc(body, name=name,
               out_shape=(per_head, per_head, per_head, per_head, _sds((HEADS_A, T, CHUNK), F32),
                          _sds((HEADS_A, n_chunks, 1, 128), F32)), grid=(n_chunks,),
               in_specs=[blk, blk, blk, blk, ablk, glblk,
                         pl.BlockSpec((1, HEADS_A, DK, 128), lambda n: (last - n, 0, 0, 0)), blk],
               out_specs=(blk, blk, blk, blk, ablk, glblk),
               scratch=[pltpu.VMEM((HEADS_A, DK, 128), F32)], sem=("arbitrary",))(u, w, qd, kd, aqk, gl, sin, do)


def gdn_outnorm_fwd(o, proj, wn, name):
    T = o.shape[1]
    tm = _tile(T, 512)

    def body(o_ref, z_ref, wn_ref, y_ref):
        for h in range(HEADS_A):
            z = z_ref[:, 128 * h:128 * (h + 1)]
            y_ref[:, 128 * h:128 * (h + 1)] = (_rms_fwd(o_ref[h], wn_ref[...]) * (z * _sigmoid(z))).astype(BF16)

    return _pc(body, name=name, out_shape=_sds((T, D), BF16), grid=(T // tm,),
               in_specs=[pl.BlockSpec((HEADS_A, tm, 128), lambda i: (0, i, 0)),
                         pl.BlockSpec((tm, D), lambda i: (i, Z_BLK0 * 128 // D)), pl.BlockSpec((1, 128), lambda i: (0, 0))],
               out_specs=pl.BlockSpec((tm, D), lambda i: (i, 0)), sem=("parallel",))(o, proj, wn)


def gdn_outnorm_bwd(o, proj, wn, dy, name):
    T = o.shape[1]
    tm = _tile(T, 512)

    def body(o_ref, z_ref, wn_ref, dy_ref, do_ref, dz_ref, dwn_ref):
        wn = wn_ref[...]
        acc = jnp.zeros((1, 128), F32)
        for h in range(HEADS_A):
            cols = slice(128 * h, 128 * (h + 1))
            z, dyh, ov = z_ref[:, cols], dy_ref[:, cols], o_ref[h]
            sg = _sigmoid(z)
            do, dwn = _rms_bwd(ov, wn, dyh * (z * sg))
            do_ref[h] = do
            acc = acc + dwn
            dz_ref[:, cols] = (dyh * _rms_fwd(ov, wn) * (sg * (1.0 + z * (1.0 - sg)))).astype(BF16)

        @pl.when(pl.program_id(0) == 0)
        def _():
            dwn_ref[...] = jnp.zeros_like(dwn_ref)
        dwn_ref[...] += acc

    row = pl.BlockSpec((tm, D), lambda i: (i, 0))
    vec = pl.BlockSpec((1, 128), lambda i: (0, 0))
    hblk = pl.BlockSpec((HEADS_A, tm, 128), lambda i: (0, i, 0))
    return _pc(body, name=name, out_shape=(_sds((HEADS_A, T, 128), F32), _sds((T, D), BF16), _sds((1, 128), F32)),
               grid=(T // tm,),
               in_specs=[hblk, pl.BlockSpec((tm, D), lambda i: (i, Z_BLK0 * 128 // D)), vec, row],
               out_specs=(hblk, row, vec), sem=("arbitrary",))(o, proj, wn, dy)


def gdn_forward(x, nw, w_in, wconv, al, dtb, wn, w_out, tag):
    h = rmsnorm_bf16(x, nw, f"{tag}_norm")
    proj = mm_nn(h, w_in, f"{tag}_proj")
    c, qkv = gdn_conv_fwd(proj, wconv, f"{tag}_conv")
    gb = gdn_gate_fwd(proj, al, dtb, f"{tag}_gate")
    u, w, qd, kd, aqk, gl, inv = gdn_prep_fwd(qkv, gb, f"{tag}_prep")
    o, sin = gdn_scan_fwd(u, w, qd, kd, aqk, gl, f"{tag}_scan")
    on = gdn_outnorm_fwd(o, proj, wn, f"{tag}_outnorm")
    y = mm_nn(on, w_out, f"{tag}_out", residual=x)
    return y, (x, h, proj, c, qkv, gb, inv, (u, w, qd, kd, aqk, gl), sin, o, on)


def gdn_backward(dout, saved, nw, w_in, wconv, al, dtb, wn, w_out, tag):
    x, h, proj, c, qkv, gb, inv, prep, sin, o, on = saved
    d_on = mm_nt(dout, w_out, f"{tag}_out_bwd")
    dw_out = mm_tn(on, dout, f"{tag}_out_wgrad")
    do, dz, dwn = gdn_outnorm_bwd(o, proj, wn, d_on, f"{tag}_outnorm_bwd")
    du, dw, dqd, dkd, da, dgl = gdn_scan_bwd(*prep, sin, do, f"{tag}_scan_bwd")
    dqkv, dgb = gdn_prep_bwd(qkv, gb, inv, du, dw, dqd, dkd, da, dgl, f"{tag}_prep_bwd")
    dba, dal, ddt = gdn_gate_bwd(proj, al, dtb, dgb, f"{tag}_gate_bwd")
    dpre, dwconv = gdn_conv_bwd(dqkv, c, proj, wconv, f"{tag}_conv_bwd")
    dproj = jnp.concatenate([dpre, dz, dba], axis=1)
    dw_in = mm_tn(h, dproj, f"{tag}_proj_wgrad")
    dh = mm_nt(dproj, w_in, f"{tag}_proj_bwd")
    dx, dnw = rmsnorm_bwd_add(x, nw, dh, dout, f"{tag}_norm_bwd")
    return dx, dnw, dw_in, dwconv, dal, ddt, dwn, dw_out


N_KV, GROUP = 4, 4
KV_COLS = 2 * N_KV * B_HD
B_COLS = D + KV_COLS


def _swa_block(q, kp, kc, vp, vc, sk, first):
    kk = jnp.concatenate([kp, kc], axis=0)
    vv = jnp.concatenate([vp, vc], axis=0)
    rows = GROUP * B_BLK
    qi = lax.broadcasted_iota(jnp.int32, (rows, 2 * B_BLK), 0) % B_BLK
    kj = lax.broadcasted_iota(jnp.int32, (rows, 2 * B_BLK), 1)
    rel = qi + B_BLK - kj
    valid = (rel >= 0) & (rel < B_BLK) & (jnp.logical_not(first) | (kj >= B_BLK))
    outs = []
    for j in range(N_KV):
        heads = range(GROUP * j, GROUP * (j + 1))
        qs = jnp.concatenate([q[:, hq * B_HD:(hq + 1) * B_HD] for hq in heads], axis=0).astype(BF16)
        s = _dot(qs, kk[:, j * B_HD:(j + 1) * B_HD].astype(BF16), NT) * (B_HD ** -0.5)
        s = jnp.where(valid, s, -1e30)
        sink = jnp.concatenate([jnp.broadcast_to(sk[:, hq:hq + 1], (B_BLK, 1)) for hq in heads], axis=0)
        m = lax.stop_gradient(jnp.maximum(jnp.max(s, axis=-1, keepdims=True), sink))
        p = jnp.exp(s - m)
        den = jnp.sum(p, axis=-1, keepdims=True) + jnp.exp(sink - m)
        o = _dot((p / den).astype(BF16), vv[:, j * B_HD:(j + 1) * B_HD].astype(BF16))
        outs += [o[g * B_BLK:(g + 1) * B_BLK] for g in range(GROUP)]
    return jnp.concatenate(outs, axis=1)


def swa_core_fwd(proj, sk, name):
    T = proj.shape[0]
    half = N_KV * B_HD

    def body(q_ref, kvc_ref, kvp_ref, sk_ref, o_ref):
        kvc, kvp = kvc_ref[...], kvp_ref[...]
        o_ref[...] = _swa_block(q_ref[...], kvp[:, :half], kvc[:, :half], kvp[:, half:], kvc[:, half:], sk_ref[...],
                                pl.program_id(0) == 0).astype(BF16)

    return _pc(body, name=name, out_shape=_sds((T, D), BF16), grid=(T // B_BLK,),
               in_specs=[pl.BlockSpec((B_BLK, D), lambda n: (n, 0)),
                         pl.BlockSpec((B_BLK, KV_COLS), lambda n: (n, D // KV_COLS)),
                         pl.BlockSpec((B_BLK, KV_COLS), lambda n: (jnp.maximum(n - 1, 0), D // KV_COLS)),
                         pl.BlockSpec((1, 128), lambda n: (0, 0))],
               out_specs=pl.BlockSpec((B_BLK, D), lambda n: (n, 0)), sem=("parallel",))(proj, proj, proj, sk)


def swa_core_bwd(proj, sk, do, name):
    T = proj.shape[0]
    last = T // B_BLK - 1
    half = N_KV * B_HD

    def body(q_ref, kvc_ref, kvp_ref, sk_ref, do_ref, dproj_ref, dbias_ref, dsk_ref, carry):
        step = pl.program_id(0)
        first = step == last

        @pl.when(step == 0)
        def _():
            carry[...] = jnp.zeros_like(carry)
            dbias_ref[...] = jnp.zeros_like(dbias_ref)
            dsk_ref[...] = jnp.zeros_like(dsk_ref)
        kvc, kvp = kvc_ref[...], kvp_ref[...]
        _, vjp = jax.vjp(functools.partial(_swa_block, first=first), q_ref[...], kvp[:, :half], kvc[:, :half],
                         kvp[:, half:], kvc[:, half:], sk_ref[...])
        dq, dkp, dkc, dvp, dvc, dsk = vjp(do_ref[...])
        dkv = jnp.concatenate([dkc, dvc], axis=1) + carry[...]
        carry[...] = jnp.concatenate([dkp, dvp], axis=1)
        row = jnp.concatenate([dq, dkv], axis=1)
        dproj_ref[...] = row.astype(BF16)
        dbias_ref[...] += jnp.sum(row, axis=0, keepdims=True)
        dsk_ref[...] += dsk

    return _pc(body, name=name, out_shape=(_sds((T, B_COLS), BF16), _sds((1, B_COLS), F32), _sds((1, 128), F32)),
               grid=(T // B_BLK,),
               in_specs=[pl.BlockSpec((B_BLK, D), lambda n: (last - n, 0)),
                         pl.BlockSpec((B_BLK, KV_COLS), lambda n: (last - n, D // KV_COLS)),
                         pl.BlockSpec((B_BLK, KV_COLS), lambda n: (jnp.maximum(last - n - 1, 0), D // KV_COLS)),
                         pl.BlockSpec((1, 128), lambda n: (0, 0)), pl.BlockSpec((B_BLK, D), lambda n: (last - n, 0))],
               out_specs=(pl.BlockSpec((B_BLK, B_COLS), lambda n: (last - n, 0)),
                          pl.BlockSpec((1, B_COLS), lambda n: (0, 0)), pl.BlockSpec((1, 128), lambda n: (0, 0))),
               scratch=[pltpu.VMEM((B_BLK, KV_COLS), F32)], sem=("arbitrary",))(proj, proj, proj, sk, do)


def col_sum(a, name):
    T, N = a.shape
    tm = _tile(T, 1024)

    def body(a_ref, o_ref):
        @pl.when(pl.program_id(0) == 0)
        def _():
            o_ref[...] = jnp.zeros_like(o_ref)
        o_ref[...] += jnp.sum(a_ref[...].astype(F32), axis=0, keepdims=True)

    return _pc(body, name=name, out_shape=_sds((1, N), F32), grid=(T // tm,),
               in_specs=[pl.BlockSpec((tm, N), lambda i: (i, 0))], out_specs=pl.BlockSpec((1, N), lambda i: (0, 0)),
               sem=("arbitrary",))(a)


def swa_forward(x, nw, w_in, b_in, sk, w_out, b_out, tag):
    h = rmsnorm_bf16(x, nw, f"{tag}_norm")
    proj = mm_nn(h, w_in, f"{tag}_proj", bias=b_in)
    o = swa_core_fwd(proj, sk, f"{tag}_core")
    y = mm_nn(o, w_out, f"{tag}_out", bias=b_out, residual=x)
    return y, (x, h, proj, o)


def swa_backward(dout, saved, nw, w_in, b_in, sk, w_out, b_out, tag):
    x, h, proj, o = saved
    do = mm_nt(dout, w_out, f"{tag}_out_bwd")
    dw_out = mm_tn(o, dout, f"{tag}_out_wgrad")
    db_out = col_sum(dout, f"{tag}_out_bias_grad")
    dproj, db_in, dsk = swa_core_bwd(proj, sk, do, f"{tag}_core_bwd")
    dw_in = mm_tn(h, dproj, f"{tag}_proj_wgrad")
    dh = mm_nt(dproj, w_in, f"{tag}_proj_bwd")
    dx, dnw = rmsnorm_bwd_add(x, nw, dh, dout, f"{tag}_norm_bwd")
    return dx, dnw, dw_in, db_in, dsk, dw_out, db_out


MESH = pl.DeviceIdType.MESH
IN_HBM = pl.BlockSpec(memory_space=pl.ANY)


def _position():
    return lax.axis_index("x"), lax.axis_index("y"), lax.axis_index("c")


def _slot(x, y, c):
    return 4 * x + 2 * y + c


def _peer(x, y, c, k):
    return (1 - x if k & 4 else x, 1 - y if k & 2 else y, 1 - c if k & 1 else c)


HBM_SPEC = pl.BlockSpec(memory_space=pltpu.HBM)
SEM_SPEC = pl.BlockSpec(memory_space=pltpu.SEMAPHORE)
DEP_SPEC = pl.BlockSpec(memory_space=pl.ANY)
SIDE_EFFECT = pltpu.SideEffectType.DATAFLOW_SIDE_EFFECTING
N_PEERS = N_DEV - 1


def _push_copies(srcs, lands, send_sems, recv_sems, scatter):
    x, y, c = _position()
    me = _slot(x, y, c)
    copies = []
    for k in (1, 2, 4, 3, 5, 6, 7):
        peer = _peer(x, y, c, k)
        for a in range(len(srcs)):
            copies.append(pltpu.make_async_remote_copy(
                src_ref=srcs[a].at[_slot(*peer)] if scatter else srcs[a], dst_ref=lands[a].at[me],
                send_sem=send_sems.at[N_PEERS * a + k - 1], recv_sem=recv_sems.at[N_PEERS * a + k - 1],
                device_id=peer, device_id_type=MESH))
    return copies


def push_start(srcs, lands, name, scatter):
    n = len(srcs)

    def body(*refs):
        for cp in _push_copies(refs[:n], refs[n:2 * n], refs[2 * n], refs[2 * n + 1], scatter):
            cp.start()
        refs[-1][...] = jnp.zeros_like(refs[-1])

    passed = [pltpu.HBM(t.shape, t.dtype) for t in list(srcs) + list(lands)]
    res = pl.pallas_call(
        body, name=name,
        out_shape=(pltpu.SemaphoreType.DMA((N_PEERS * n,)), pltpu.SemaphoreType.DMA((N_PEERS * n,)), *passed, _sds((8, 128), F32)),
        in_specs=[HBM_SPEC] * (2 * n),
        out_specs=(SEM_SPEC, SEM_SPEC, *([HBM_SPEC] * (2 * n)), pl.BlockSpec(memory_space=pltpu.VMEM)),
        input_output_aliases={i: 2 + i for i in range(2 * n)},
        compiler_params=pltpu.CompilerParams(has_side_effects=SIDE_EFFECT),
    )(*[pltpu.with_memory_space_constraint(t, pltpu.HBM) for t in list(srcs) + list(lands)])
    return (res[0], res[1], list(res[2:2 + n]), list(res[2 + n:2 + 2 * n])), res[-1]


def push_wait(handles, after, name, scatter):
    send_sems, recv_sems, srcs, lands = handles
    n = len(srcs)

    def body(*refs):
        for cp in _push_copies(refs[:n], refs[n:2 * n], refs[2 * n], refs[2 * n + 1], scatter):
            cp.wait_send()
            cp.wait_recv()

    res = pl.pallas_call(
        body, name=name, out_shape=tuple(pltpu.HBM(t.shape, t.dtype) for t in srcs + lands),
        in_specs=[HBM_SPEC] * (2 * n) + [SEM_SPEC, SEM_SPEC, DEP_SPEC], out_specs=tuple([HBM_SPEC] * (2 * n)),
        input_output_aliases={i: i for i in range(2 * n)},
        compiler_params=pltpu.CompilerParams(has_side_effects=SIDE_EFFECT),
    )(*srcs, *lands, send_sems, recv_sems, after)
    return list(res[n:])


def gather_start(shards, name):
    me = _slot(*_position())
    lands = [lax.dynamic_update_slice(lax.empty((N_DEV,) + t.shape, t.dtype), t[None], (me,) + (0,) * t.ndim) for t in shards]
    return push_start(shards, lands, name, scatter=False)


def exchange_start(parts, name):
    me = _slot(*_position())
    lands = [lax.dynamic_update_slice(lax.empty(t.shape, t.dtype), lax.dynamic_index_in_dim(t, me, 0, keepdims=True),
                                      (me,) + (0,) * (t.ndim - 1)) for t in parts]
    return push_start(parts, lands, name, scatter=True)


def all_reduce_small(part, name):
    R, C = part.shape

    def body(p_ref, o_ref, buf, send_sems, recv_sems):
        x, y, c = _position()
        me = _slot(x, y, c)
        copies = []
        for k in range(1, N_DEV):
            copies.append(pltpu.make_async_remote_copy(
                src_ref=p_ref, dst_ref=buf.at[me], send_sem=send_sems.at[k - 1], recv_sem=recv_sems.at[k - 1],
                device_id=_peer(x, y, c, k), device_id_type=MESH))
            copies[-1].start()
        buf[me] = p_ref[...]
        for cp in copies:
            cp.wait()
        acc = buf[0]
        for s in range(1, N_DEV):
            acc = acc + buf[s]
        o_ref[...] = acc

    vmem = pl.BlockSpec(memory_space=pltpu.VMEM)
    return _pc(body, name=name, out_shape=_sds((R, C), F32), in_specs=[vmem], out_specs=vmem,
               scratch=[pltpu.VMEM((N_DEV, R, C), F32), pltpu.SemaphoreType.DMA((7,)), pltpu.SemaphoreType.DMA((7,))],
               )(part)


def _row_tile(rows, cols):
    best = rows
    for t in range(16, rows, 16):
        if rows % t == 0 and t * cols * 4 <= (1 << 20):
            best = t
    return best


def adam_update(parts, w, m, v, name):
    n_layers = len(parts)
    P, R, C = parts[0].shape
    tr = _row_tile(R, C)
    n_t = R // tr

    def body(*refs):
        p_refs = refs[:n_layers]
        w_ref, m_ref, v_ref, g_ref, d_ref, nm_ref, nv_ref = refs[n_layers:]
        for layer in range(n_layers):
            @pl.when(pl.program_id(0) == layer)
            def _(p_ref=p_refs[layer]):
                g = p_ref[0].astype(F32)
                for s in range(1, P):
                    g = g + p_ref[s].astype(F32)
                new_m = ADAM_B1 * m_ref[0] + (1.0 - ADAM_B1) * g
                new_v = ADAM_B2 * v_ref[0] + (1.0 - ADAM_B2) * (g * g)
                m_hat = new_m / (1.0 - ADAM_B1 ** ADAM_STEP)
                v_hat = new_v / (1.0 - ADAM_B2 ** ADAM_STEP)
                g_ref[0] = g
                d_ref[0] = -ADAM_LR * (m_hat / (jnp.sqrt(v_hat) + ADAM_EPS) + ADAM_WD * w_ref[0])
                nm_ref[0] = new_m
                nv_ref[0] = new_v

    def part_spec(layer):
        return pl.BlockSpec((P, tr, C), lambda l_, i: (0, jnp.where(l_ == layer, i, jnp.where(l_ < layer, 0, n_t - 1)), 0))

    blk = pl.BlockSpec((1, tr, C), lambda l_, i: (l_, i, 0))
    out = _sds((n_layers, R, C), F32)
    return _pc(body, name=name, out_shape=(out, out, out, out), grid=(n_layers, n_t),
               in_specs=[part_spec(layer) for layer in range(n_layers)] + [blk, blk, blk],
               out_specs=(blk, blk, blk, blk), sem=("arbitrary", "arbitrary"))(*parts, w, m, v)


WEIGHTS = ("ffn1_norm", "ffn1_w_gu", "ffn1_w_down", "mix_norm", "ffn2_norm", "ffn2_w_gu", "ffn2_w_down", "a_w_in",
           "a_w_conv", "a_A_log", "a_dt_bias", "a_out_norm", "a_w_out", "b_w_in", "b_b_in", "b_sinks", "b_w_out",
           "b_b_out", "final_norm")
SHARDED = ("ffn1_w_gu", "ffn1_w_down", "ffn2_w_gu", "ffn2_w_down", "a_w_in", "a_w_conv", "a_w_out", "b_w_in", "b_b_in",
           "b_w_out", "b_b_out")
SENT_AS_BF16 = ("ffn1_w_gu", "ffn1_w_down", "ffn2_w_gu", "ffn2_w_down", "a_w_in", "a_w_out", "b_w_in", "b_w_out")
MISC_LANES = dict(a_A_log=(0, 8), a_dt_bias=(8, 16), b_sinks=(16, 32), a_out_norm=(128, 256))
LOSS_LANE = 256


def _pack_small(t):
    misc = jnp.zeros((D,), F32)
    for key, (lo, hi) in MISC_LANES.items():
        misc = misc.at[lo:hi].set(t[key].reshape(-1))
    if "loss" in t:
        misc = misc.at[LOSS_LANE].set(t["loss"])
    return jnp.concatenate([t["ffn1_norm"], t["mix_norm"], t["ffn2_norm"], t["final_norm"].reshape(1, D), misc[None]], axis=0)


def _unpack_small(p, like):
    out = dict(ffn1_norm=p[0:2], mix_norm=p[2:4], ffn2_norm=p[4:6], final_norm=p[6])
    for key, (lo, hi) in MISC_LANES.items():
        out[key] = p[7, lo:hi].reshape(like[key].shape)
    return out


def kernel(x, ffn1_norm, ffn1_w_gu, ffn1_w_down, mix_norm, ffn2_norm, ffn2_w_gu, ffn2_w_down, a_w_in, a_w_conv, a_A_log, a_dt_bias, a_out_norm, a_w_out, b_w_in, b_b_in, b_sinks, b_w_out, b_b_out, final_norm, loss_target, m_ffn1_norm, m_ffn1_w_gu, m_ffn1_w_down, m_mix_norm, m_ffn2_norm, m_ffn2_w_gu, m_ffn2_w_down, m_a_w_in, m_a_w_conv, m_a_A_log, m_a_dt_bias, m_a_out_norm, m_a_w_out, m_b_w_in, m_b_b_in, m_b_sinks, m_b_w_out, m_b_b_out, m_final_norm, v_ffn1_norm, v_ffn1_w_gu, v_ffn1_w_down, v_mix_norm, v_ffn2_norm, v_ffn2_w_gu, v_ffn2_w_down, v_a_w_in, v_a_w_conv, v_a_A_log, v_a_dt_bias, v_a_out_norm, v_a_w_out, v_b_w_in, v_b_b_in, v_b_sinks, v_b_w_out, v_b_b_out, v_final_norm):
    w = dict(ffn1_norm=ffn1_norm, ffn1_w_gu=ffn1_w_gu, ffn1_w_down=ffn1_w_down, mix_norm=mix_norm, ffn2_norm=ffn2_norm, ffn2_w_gu=ffn2_w_gu, ffn2_w_down=ffn2_w_down, a_w_in=a_w_in, a_w_conv=a_w_conv, a_A_log=a_A_log, a_dt_bias=a_dt_bias, a_out_norm=a_out_norm, a_w_out=a_w_out, b_w_in=b_w_in, b_b_in=b_b_in, b_sinks=b_sinks, b_w_out=b_w_out, b_b_out=b_b_out, final_norm=final_norm)
    m = dict(ffn1_norm=m_ffn1_norm, ffn1_w_gu=m_ffn1_w_gu, ffn1_w_down=m_ffn1_w_down, mix_norm=m_mix_norm, ffn2_norm=m_ffn2_norm, ffn2_w_gu=m_ffn2_w_gu, ffn2_w_down=m_ffn2_w_down, a_w_in=m_a_w_in, a_w_conv=m_a_w_conv, a_A_log=m_a_A_log, a_dt_bias=m_a_dt_bias, a_out_norm=m_a_out_norm, a_w_out=m_a_w_out, b_w_in=m_b_w_in, b_b_in=m_b_b_in, b_sinks=m_b_sinks, b_w_out=m_b_w_out, b_b_out=m_b_b_out, final_norm=m_final_norm)
    v = dict(ffn1_norm=v_ffn1_norm, ffn1_w_gu=v_ffn1_w_gu, ffn1_w_down=v_ffn1_w_down, mix_norm=v_mix_norm, ffn2_norm=v_ffn2_norm, ffn2_w_gu=v_ffn2_w_gu, ffn2_w_down=v_ffn2_w_down, a_w_in=v_a_w_in, a_w_conv=v_a_w_conv, a_A_log=v_a_A_log, a_dt_bias=v_a_dt_bias, a_out_norm=v_a_out_norm, a_w_out=v_a_w_out, b_w_in=v_b_w_in, b_b_in=v_b_b_in, b_sinks=v_b_sinks, b_w_out=v_b_w_out, b_b_out=v_b_b_out, final_norm=v_final_norm)
    T = x.shape[1]
    x0, tgt = x.reshape(T, D), loss_target.reshape(T, D)

    def cast(t):
        return t.astype(BF16)

    g0 = [cast(ffn1_w_gu[0]), cast(ffn1_w_down[0])]
    g1 = [cast(a_w_in[0]), a_w_conv[0], cast(a_w_out[0]), cast(ffn2_w_gu[0]), cast(ffn2_w_down[0])]
    g2 = [cast(ffn1_w_gu[1]), cast(ffn1_w_down[1]), cast(b_w_in[0]), b_b_in, cast(b_w_out[0]), b_b_out,
          cast(ffn2_w_gu[1]), cast(ffn2_w_down[1])]
    h0, t0 = gather_start(g0, "gather0_start")
    h1, t1 = gather_start(g1, "gather1_start")
    h2, t2 = gather_start(g2, "gather2_start")
    started = (t0, t1, t2)
    a_log_row = jnp.zeros((1, 128), F32).at[0, HEADS_A:2 * HEADS_A].set(a_A_log[0])
    dt_row = jnp.zeros((1, 128), F32).at[0, HEADS_A:2 * HEADS_A].set(a_dt_bias[0])
    sink_row = jnp.zeros((1, 128), F32).at[0, :b_sinks.shape[1]].set(b_sinks[0])
    a_in_cols = a_w_in.shape[-1] * N_DEV

    def down_blocks(t):
        return t.reshape(N_FB, FB, D)

    wgu, wdn, saved = {}, {}, []
    got = push_wait(h0, t2, "gather0_wait", scatter=False)
    wgu["ffn1", 0], wdn["ffn1", 0] = got[0], down_blocks(got[1])
    xs, s1 = ffn_forward(x0, ffn1_norm[0:1], wgu["ffn1", 0], wdn["ffn1", 0], "l0_ffn1", deps=started)
    got = push_wait(h1, xs, "gather1_wait", scatter=False)
    a_in_full = jnp.pad(got[0].transpose(1, 0, 2).reshape(D, a_in_cols), ((0, 0), (0, A_COLS - a_in_cols)))
    gdn_args = (mix_norm[0:1], a_in_full, got[1].transpose(1, 0, 2).reshape(4, 3 * D), a_log_row, dt_row, a_out_norm,
                got[2].reshape(D, D))
    wgu["ffn2", 0], wdn["ffn2", 0] = got[3], down_blocks(got[4])
    xs, sm = gdn_forward(xs, *gdn_args, "gdn")
    xs, s2 = ffn_forward(xs, ffn2_norm[0:1], wgu["ffn2", 0], wdn["ffn2", 0], "l0_ffn2")
    saved.append((s1, sm, s2))
    got = push_wait(h2, xs, "gather2_wait", scatter=False)
    wgu["ffn1", 1], wdn["ffn1", 1] = got[0], down_blocks(got[1])
    swa_args = (mix_norm[1:2], got[2].transpose(1, 0, 2).reshape(D, B_COLS), got[3].reshape(1, B_COLS), sink_row,
                got[4].reshape(D, D), got[5].reshape(1, D))
    wgu["ffn2", 1], wdn["ffn2", 1] = got[6], down_blocks(got[7])
    xs, s1 = ffn_forward(xs, ffn1_norm[1:2], wgu["ffn1", 1], wdn["ffn1", 1], "l1_ffn1")
    xs, sm = swa_forward(xs, *swa_args, "swa")
    xs, s2 = ffn_forward(xs, ffn2_norm[1:2], wgu["ffn2", 1], wdn["ffn2", 1], "l1_ffn2")
    saved.append((s1, sm, s2))
    loss_row, dx, d_final_norm = final_loss(xs, final_norm.reshape(1, D), tgt, "final_loss")

    def down_slots(t):
        return cast(t.reshape(N_DEV, FB // 2, D))

    def col_slots(t, dtype=BF16):
        return t.reshape(t.shape[0], N_DEV, -1).transpose(1, 0, 2).astype(dtype)

    d_norm = {"ffn1_norm": [None, None], "mix_norm": [None, None], "ffn2_norm": [None, None]}
    s1, sm, s2 = saved[1]
    dx, d_norm["ffn2_norm"][1], d_gu, d_dn = ffn_backward(dx, s2, ffn2_norm[1:2], wgu["ffn2", 1], wdn["ffn2", 1], "l1_ffn2")
    sent1 = [cast(d_gu), down_slots(d_dn)]
    dx, d_norm["mix_norm"][1], d_b_in, d_b_bias_in, d_sinks, d_b_out, d_b_bias_out = swa_backward(dx, sm, *swa_args, "swa")
    sent1 += [col_slots(d_b_in), d_b_bias_in.reshape(N_DEV, 1, -1), cast(d_b_out.reshape(N_DEV, D // N_DEV, D)),
              d_b_bias_out.reshape(N_DEV, 1, -1)]
    dx, d_norm["ffn1_norm"][1], d_gu, d_dn = ffn_backward(dx, s1, ffn1_norm[1:2], wgu["ffn1", 1], wdn["ffn1", 1], "l1_ffn1")
    sent1 += [cast(d_gu), down_slots(d_dn)]
    x1, tx1 = exchange_start(sent1, "exchange1_start")

    s1, sm, s2 = saved[0]
    dx, d_norm["ffn2_norm"][0], d_gu, d_dn = ffn_backward(dx, s2, ffn2_norm[0:1], wgu["ffn2", 0], wdn["ffn2", 0], "l0_ffn2",
                                                           deps=(tx1,))
    sent2 = [cast(d_gu), down_slots(d_dn)]
    dx, d_norm["mix_norm"][0], d_a_in, d_a_conv, d_alog, d_dt, d_onorm, d_a_out = gdn_backward(dx, sm, *gdn_args, "gdn")
    sent2 += [col_slots(d_a_in[:, :a_in_cols]), col_slots(d_a_conv, F32), cast(d_a_out.reshape(N_DEV, D // N_DEV, D))]
    x2, tx2 = exchange_start(sent2, "exchange2_start")
    dx, d_norm["ffn1_norm"][0], d_gu, d_dn = ffn_backward(dx, s1, ffn1_norm[0:1], wgu["ffn1", 0], wdn["ffn1", 0], "l0_ffn1",
                                                           deps=(tx2,))
    x3, tx3 = exchange_start([cast(d_gu), down_slots(d_dn)], "exchange3_start")
    grad_x = dx.reshape(x.shape)
    r1 = push_wait(x1, dx, "exchange1_wait", scatter=True)
    r2 = push_wait(x2, dx, "exchange2_wait", scatter=True)
    r3 = push_wait(x3, tx3, "exchange3_wait", scatter=True)
    received = dict(ffn2_w_gu=[r2[0], r1[0]], ffn2_w_down=[r2[1], r1[1]], ffn1_w_gu=[r3[0], r1[6]], ffn1_w_down=[r3[1], r1[7]],
                    b_w_in=[r1[2]], b_b_in=[r1[3]], b_w_out=[r1[4]], b_b_out=[r1[5]],
                    a_w_in=[r2[2]], a_w_conv=[r2[3]], a_w_out=[r2[4]])

    grads, deltas, new_m, new_v = {}, {}, {}, {}
    for key in SHARDED:
        shape = w[key].shape
        cols = shape[-1]
        layers = lambda t: t.reshape(shape[0], -1, cols)
        out = adam_update([r.reshape(N_DEV, -1, cols) for r in received[key]], layers(w[key]), layers(m[key]), layers(v[key]),
                          f"adam_{key}")
        grads[key], deltas[key], new_m[key], new_v[key] = (t.reshape(shape) for t in out)

    small = dict(ffn1_norm=jnp.concatenate(d_norm["ffn1_norm"], axis=0), mix_norm=jnp.concatenate(d_norm["mix_norm"], axis=0),
                 ffn2_norm=jnp.concatenate(d_norm["ffn2_norm"], axis=0), final_norm=d_final_norm,
                 a_A_log=d_alog[0, HEADS_A:2 * HEADS_A], a_dt_bias=d_dt[0, HEADS_A:2 * HEADS_A],
                 b_sinks=d_sinks[0, :b_sinks.shape[1]], a_out_norm=d_onorm, loss=loss_row[0, 0])
    total = all_reduce_small(_pack_small(small), "allreduce_small")
    out = adam_update([total[None]], _pack_small(w)[None], _pack_small(m)[None], _pack_small(v)[None], "adam_small")
    for dst, packed in zip((grads, deltas, new_m, new_v), out):
        dst.update(_unpack_small(packed[0], w))
    loss = total[7, LOSS_LANE]

    return (loss, grad_x, *[grads[k_] for k_ in WEIGHTS], *[deltas[k_] for k_ in WEIGHTS],
            *[new_m[k_] for k_ in WEIGHTS], *[new_v[k_] for k_ in WEIGHTS])
```

```python
import functools

import jax
import jax.numpy as jnp
from jax import lax
from jax.experimental import pallas as pl
from jax.experimental.pallas import tpu as pltpu

F32, BF16 = jnp.float32, jnp.bfloat16
HI = lax.Precision.HIGHEST
EPS = 1e-6

N_DEV = 8
D = 1024
FB = 704
N_FB = 4
HEADS_A, DK = 8, 128
CHUNK = 64
PREP_T = 512
A_COLS = 4224
B_HD, B_BLK = 64, 128
VMEM_LIMIT_V7X = 60 * 1024 * 1024

ADAM_LR, ADAM_B1, ADAM_B2, ADAM_EPS, ADAM_WD, ADAM_STEP = 0.001, 0.9, 0.999, 1e-08, 0.01, 10

NT = (((1,), (1,)), ((), ()))
TN = (((0,), (0,)), ((), ()))


def _pc(body, *, name, out_shape, grid=(), in_specs=None, out_specs=None, scratch=(), sem=None, **kw):
    params = pltpu.CompilerParams(dimension_semantics=sem, vmem_limit_bytes=VMEM_LIMIT_V7X)
    return pl.pallas_call(body, name=name, out_shape=out_shape, grid=grid, in_specs=in_specs, out_specs=out_specs,
                          scratch_shapes=list(scratch), compiler_params=params, **kw)


def _sds(shape, dtype):
    return jax.ShapeDtypeStruct(tuple(shape), dtype)


def _dot(a, b, dims=None, precision=None):
    if dims is None:
        return jnp.dot(a, b, preferred_element_type=F32, precision=precision)
    return lax.dot_general(a, b, dims, preferred_element_type=F32, precision=precision)


def _sigmoid(x):
    return 1.0 / (1.0 + jnp.exp(-x))


def _softplus(x):
    return jnp.maximum(x, 0.0) + jnp.log(1.0 + jnp.exp(-jnp.abs(x)))


def _rms_fwd(x, w):
    r = lax.rsqrt(jnp.mean(x * x, axis=-1, keepdims=True) + EPS)
    return x * r * w


def _rms_bwd(x, w, dy):
    r = lax.rsqrt(jnp.mean(x * x, axis=-1, keepdims=True) + EPS)
    xh = x * r
    dxh = dy * w
    dx = r * (dxh - xh * jnp.mean(dxh * xh, axis=-1, keepdims=True))
    return dx, jnp.sum(dy * xh, axis=0, keepdims=True)


def _tile(n, want):
    t = min(n, want)
    assert n % t == 0, (n, want)
    return t


def rmsnorm_bf16(x, w, name, deps=()):
    T = x.shape[0]
    tm = _tile(T, 1024)

    def body(x_ref, w_ref, *rest):
        rest[-1][...] = _rms_fwd(x_ref[...], w_ref[...]).astype(BF16)

    return _pc(body, name=name, out_shape=_sds((T, D), BF16), grid=(T // tm,),
               in_specs=[pl.BlockSpec((tm, D), lambda i: (i, 0)), pl.BlockSpec((1, D), lambda i: (0, 0))] + [DEP_SPEC] * len(deps),
               out_specs=pl.BlockSpec((tm, D), lambda i: (i, 0)), sem=("parallel",))(x, w, *deps)


def rmsnorm_bwd_add(x, w, dxn, dres, name):
    T = x.shape[0]
    tm = _tile(T, 512)

    def body(x_ref, w_ref, dxn_ref, dres_ref, dx_ref, dw_ref):
        dx, dw = _rms_bwd(x_ref[...], w_ref[...], dxn_ref[...])
        dx_ref[...] = dres_ref[...] + dx

        @pl.when(pl.program_id(0) == 0)
        def _():
            dw_ref[...] = jnp.zeros_like(dw_ref)
        dw_ref[...] += dw

    row = pl.BlockSpec((tm, D), lambda i: (i, 0))
    vec = pl.BlockSpec((1, D), lambda i: (0, 0))
    return _pc(body, name=name, out_shape=(_sds((T, D), F32), _sds((1, D), F32)), grid=(T // tm,),
               in_specs=[row, vec, row, row], out_specs=(row, vec), sem=("arbitrary",))(x, w, dxn, dres)


def final_loss(x, w, tgt, name):
    T = x.shape[0]
    tm = _tile(T, 512)

    def body(x_ref, w_ref, t_ref, loss_ref, dx_ref, dw_ref):
        xv, wv = x_ref[...], w_ref[...]
        err = _rms_fwd(xv, wv) - t_ref[...]
        dx, dw = _rms_bwd(xv, wv, err * (1.0 / D))
        dx_ref[...] = dx

        @pl.when(pl.program_id(0) == 0)
        def _():
            dw_ref[...] = jnp.zeros_like(dw_ref)
            loss_ref[...] = jnp.zeros_like(loss_ref)
        dw_ref[...] += dw
        loss_ref[...] += jnp.full((1, 128), 0.5 / D, F32) * jnp.sum(err * err)

    row = pl.BlockSpec((tm, D), lambda i: (i, 0))
    vec = pl.BlockSpec((1, D), lambda i: (0, 0))
    return _pc(body, name=name, out_shape=(_sds((1, 128), F32), _sds((T, D), F32), _sds((1, D), F32)),
               grid=(T // tm,), in_specs=[row, vec, row],
               out_specs=(pl.BlockSpec((1, 128), lambda i: (0, 0)), row, vec), sem=("arbitrary",))(x, w, tgt)


def _col_tile(n):
    for t in (1536, 1408, 1024, 768, 512, 384, 256, 128):
        if n % t == 0:
            return t
    return n


def mm_nn(a, b, name, bias=None, residual=None, out_dtype=F32):
    T, K = a.shape
    N = b.shape[1]
    tm, tn = _tile(T, 512), _col_tile(N)

    def body(a_ref, b_ref, *rest):
        o_ref = rest[-1]
        acc = _dot(a_ref[...].astype(BF16), b_ref[...])
        for extra in rest[:-1]:
            acc = acc + extra[...]
        o_ref[...] = acc.astype(out_dtype)

    in_specs = [pl.BlockSpec((tm, K), lambda j, i: (i, 0)), pl.BlockSpec((K, tn), lambda j, i: (0, j))]
    args = [a, b]
    if bias is not None:
        in_specs.append(pl.BlockSpec((1, tn), lambda j, i: (0, j)))
        args.append(bias)
    if residual is not None:
        in_specs.append(pl.BlockSpec((tm, tn), lambda j, i: (i, j)))
        args.append(residual)
    return _pc(body, name=name, out_shape=_sds((T, N), out_dtype), grid=(N // tn, T // tm), in_specs=in_specs,
               out_specs=pl.BlockSpec((tm, tn), lambda j, i: (i, j)), sem=("parallel", "parallel"))(*args)


def mm_nt(a, b, name, out_dtype=F32):
    T, N = a.shape
    K = b.shape[0]
    tm = _tile(T, 512)

    def body(a_ref, b_ref, o_ref):
        o_ref[...] = _dot(a_ref[...].astype(BF16), b_ref[...], NT).astype(out_dtype)

    return _pc(body, name=name, out_shape=_sds((T, K), out_dtype), grid=(T // tm,),
               in_specs=[pl.BlockSpec((tm, N), lambda i: (i, 0)), pl.BlockSpec((K, N), lambda i: (0, 0))],
               out_specs=pl.BlockSpec((tm, K), lambda i: (i, 0)), sem=("parallel",))(a, b)


def mm_tn(a, b, name):
    T, K = a.shape
    N = b.shape[1]
    tt, tn = _tile(T, 1024), _col_tile(N)

    def body(a_ref, b_ref, o_ref):
        @pl.when(pl.program_id(1) == 0)
        def _():
            o_ref[...] = jnp.zeros_like(o_ref)
        o_ref[...] += _dot(a_ref[...].astype(BF16), b_ref[...].astype(BF16), TN)

    return _pc(body, name=name, out_shape=_sds((K, N), F32), grid=(N // tn, T // tt),
               in_specs=[pl.BlockSpec((tt, K), lambda j, t: (t, 0)), pl.BlockSpec((tt, tn), lambda j, t: (t, j))],
               out_specs=pl.BlockSpec((K, tn), lambda j, t: (0, j)), sem=("parallel", "arbitrary"))(a, b)


def ffn_up(xn, wgu, name):
    T = xn.shape[0]
    tm = _tile(T, 512)

    def body(x_ref, w_ref, gu_ref):
        xv = x_ref[...]
        for j in range(2 * N_FB):
            gu_ref[j] = _dot(xv, w_ref[j]).astype(BF16)

    return _pc(body, name=name, out_shape=_sds((2 * N_FB, T, FB), BF16), grid=(T // tm,),
               in_specs=[pl.BlockSpec((tm, D), lambda i: (i, 0)), pl.BlockSpec((2 * N_FB, D, FB), lambda i: (0, 0, 0))],
               out_specs=pl.BlockSpec((2 * N_FB, tm, FB), lambda i: (0, i, 0)), sem=("parallel",))(xn, wgu)


def ffn_down(gu, wd, x, name):
    T = x.shape[0]
    tm = _tile(T, 512)

    def body(gu_ref, w_ref, x_ref, o_ref):
        acc = jnp.zeros((tm, D), F32)
        for g in range(N_FB):
            gate, up = gu_ref[g], gu_ref[N_FB + g]
            acc = acc + _dot(gate * _sigmoid(gate) * up, w_ref[g])
        o_ref[...] = x_ref[...] + 0.5 * acc

    row = pl.BlockSpec((tm, D), lambda i: (i, 0))
    return _pc(body, name=name, out_shape=_sds((T, D), F32), grid=(T // tm,),
               in_specs=[pl.BlockSpec((2 * N_FB, tm, FB), lambda i: (0, i, 0)),
                         pl.BlockSpec((N_FB, FB, D), lambda i: (0, 0, 0)), row],
               out_specs=row, sem=("parallel",))(gu, wd, x)


def ffn_bwd_hidden(dout, wd, gu, name, deps=()):
    T = dout.shape[0]
    tm = _tile(T, 512)

    def body(d_ref, w_ref, gu_ref, *rest):
        dgu_ref, act_ref = rest[-2:]
        dy = (0.5 * d_ref[...]).astype(BF16)
        for g in range(N_FB):
            gate, up = gu_ref[g], gu_ref[N_FB + g]
            sg = _sigmoid(gate)
            silu = gate * sg
            dact = _dot(dy, w_ref[g], NT).astype(BF16)
            act_ref[g] = silu * up
            dgu_ref[g] = dact * up * (sg * (1.0 + gate * (1.0 - sg)))
            dgu_ref[N_FB + g] = dact * silu

    return _pc(body, name=name, out_shape=(_sds((2 * N_FB, T, FB), BF16), _sds((N_FB, T, FB), BF16)), grid=(T // tm,),
               in_specs=[pl.BlockSpec((tm, D), lambda i: (i, 0)), pl.BlockSpec((N_FB, FB, D), lambda i: (0, 0, 0)),
                         pl.BlockSpec((2 * N_FB, tm, FB), lambda i: (0, i, 0))] + [DEP_SPEC] * len(deps),
               out_specs=(pl.BlockSpec((2 * N_FB, tm, FB), lambda i: (0, i, 0)),
                          pl.BlockSpec((N_FB, tm, FB), lambda i: (0, i, 0))), sem=("parallel",))(dout, wd, gu, *deps)


def ffn_bwd_input(dgu, wgu, x, dout, nw, name, deps=()):
    T = x.shape[0]
    tm = _tile(T, 512)

    def body(dgu_ref, w_ref, x_ref, d_ref, nw_ref, *rest):
        dx_ref, dnw_ref = rest[-2:]
        dxn = jnp.zeros((tm, D), F32)
        for j in range(2 * N_FB):
            dxn = dxn + _dot(dgu_ref[j], w_ref[j], NT)
        dx, dw = _rms_bwd(x_ref[...], nw_ref[...], dxn)
        dx_ref[...] = d_ref[...] + dx

        @pl.when(pl.program_id(0) == 0)
        def _():
            dnw_ref[...] = jnp.zeros_like(dnw_ref)
        dnw_ref[...] += dw

    row = pl.BlockSpec((tm, D), lambda i: (i, 0))
    vec = pl.BlockSpec((1, D), lambda i: (0, 0))
    return _pc(body, name=name, out_shape=(_sds((T, D), F32), _sds((1, D), F32)), grid=(T // tm,),
               in_specs=[pl.BlockSpec((2 * N_FB, tm, FB), lambda i: (0, i, 0)),
                         pl.BlockSpec((2 * N_FB, D, FB), lambda i: (0, 0, 0)), row, row, vec] + [DEP_SPEC] * len(deps),
               out_specs=(row, vec), sem=("arbitrary",))(dgu, wgu, x, dout, nw, *deps)


def ffn_wgrad_gu(xn, dgu, name):
    T = xn.shape[0]
    tt = _tile(T, 1024)

    def body(x_ref, d_ref, o_ref):
        @pl.when(pl.program_id(1) == 0)
        def _():
            o_ref[...] = jnp.zeros_like(o_ref)
        o_ref[0] += _dot(x_ref[...], d_ref[0], TN)

    return _pc(body, name=name, out_shape=_sds((2 * N_FB, D, FB), F32), grid=(2 * N_FB, T // tt),
               in_specs=[pl.BlockSpec((tt, D), lambda j, t: (t, 0)), pl.BlockSpec((1, tt, FB), lambda j, t: (j, t, 0))],
               out_specs=pl.BlockSpec((1, D, FB), lambda j, t: (j, 0, 0)), sem=("parallel", "arbitrary"))(xn, dgu)


def ffn_wgrad_down(act, dout, name):
    T = dout.shape[0]
    tt = _tile(T, 1024)

    def body(a_ref, d_ref, o_ref):
        @pl.when(pl.program_id(1) == 0)
        def _():
            o_ref[...] = jnp.zeros_like(o_ref)
        o_ref[0] += _dot(a_ref[0], (0.5 * d_ref[...]).astype(BF16), TN)

    return _pc(body, name=name, out_shape=_sds((N_FB, FB, D), F32), grid=(N_FB, T // tt),
               in_specs=[pl.BlockSpec((1, tt, FB), lambda g, t: (g, t, 0)), pl.BlockSpec((tt, D), lambda g, t: (t, 0))],
               out_specs=pl.BlockSpec((1, FB, D), lambda g, t: (g, 0, 0)), sem=("parallel", "arbitrary"))(act, dout)


def ffn_forward(x, nw, wgu, wd, tag, deps=()):
    xn = rmsnorm_bf16(x, nw, f"{tag}_norm", deps)
    gu = ffn_up(xn, wgu, f"{tag}_up")
    return ffn_down(gu, wd, x, f"{tag}_down"), (x, xn, gu)


def ffn_backward(dout, saved, nw, wgu, wd, tag, deps=(), on_grads=None):
    x, xn, gu = saved
    dgu, act = ffn_bwd_hidden(dout, wd, gu, f"{tag}_bwd_hidden", deps)
    dwd = ffn_wgrad_down(act, dout, f"{tag}_wgrad_down")
    dwgu = ffn_wgrad_gu(xn, dgu, f"{tag}_wgrad_gu")
    late = on_grads(dwgu, dwd) if on_grads else ()
    dx, dnw = ffn_bwd_input(dgu, wgu, x, dout, nw, f"{tag}_bwd_input", late)
    return dx, dnw, dwgu, dwd


N_QKV_BLK = 3 * HEADS_A
Z_BLK0 = N_QKV_BLK
BA_BLK = A_COLS // 128 - 1


def _conv_taps(xcat, w):
    c = xcat[8:] * w[3:4]
    for k in range(3):
        c = c + pltpu.roll(xcat, 3 - k, 0)[8:] * w[k:k + 1]
    return c


def _head_cols(h):
    return slice(128 * h, 128 * (h + 1))


def gdn_conv_fwd(proj, wconv, name):
    T = proj.shape[0]
    tm = _tile(T, 512)

    def body(cur_ref, prev_ref, w_ref, c_ref, y_ref):
        kind, t = pl.program_id(0), pl.program_id(1)
        prev = jnp.where(t > 0, prev_ref[...], 0.0)
        c = _conv_taps(jnp.concatenate([prev, cur_ref[...]], axis=0), w_ref[...])
        c_ref[...] = c
        s = c * _sigmoid(c)
        scale = jnp.where(kind == 0, DK ** -0.5, 1.0)
        for h in range(HEADS_A):
            sh = s[:, _head_cols(h)]
            r = lax.rsqrt(jnp.sum(sh * sh, axis=-1, keepdims=True) + EPS)
            y_ref[h] = sh * jnp.where(kind < 2, r * scale, 1.0)

    return _pc(body, name=name, out_shape=(_sds((T, 3 * D), F32), _sds((N_QKV_BLK, T, 128), F32)),
               grid=(3, T // tm),
               in_specs=[pl.BlockSpec((tm, D), lambda kd, t: (t, kd)),
                         pl.BlockSpec((8, D), lambda kd, t: (jnp.maximum(t * (tm // 8) - 1, 0), kd)),
                         pl.BlockSpec((4, D), lambda kd, t: (0, kd))],
               out_specs=(pl.BlockSpec((tm, D), lambda kd, t: (t, kd)),
                          pl.BlockSpec((HEADS_A, tm, 128), lambda kd, t: (kd, t, 0))),
               sem=("parallel", "parallel"))(proj, proj, wconv)


def gdn_conv_bwd(dqkv, c, proj, wconv, name):
    T = c.shape[0]
    tm = _tile(T, 512)
    n_t = T // tm

    def body(dy_ref, dyn_ref, c_ref, cn_ref, x_ref, xp_ref, w_ref, dx_ref, dw_ref):
        kind, t = pl.program_id(0), pl.program_id(1)
        scale = jnp.where(kind == 0, DK ** -0.5, 1.0)

        def act_bwd(dy, cv):
            sg = _sigmoid(cv)
            s = cv * sg
            parts = []
            for h in range(HEADS_A):
                sh, dyh = s[:, _head_cols(h)], dy[h]
                r = lax.rsqrt(jnp.sum(sh * sh, axis=-1, keepdims=True) + EPS)
                ds_norm = scale * r * (dyh - (r * r) * sh * jnp.sum(dyh * sh, axis=-1, keepdims=True))
                parts.append(jnp.where(kind < 2, ds_norm, dyh))
            return jnp.concatenate(parts, axis=1) * (sg * (1.0 + cv * (1.0 - sg)))

        w = w_ref[...]
        dcur = act_bwd(dy_ref[...], c_ref[...])
        dnext = jnp.where(t < n_t - 1, act_bwd(dyn_ref[...], cn_ref[...]), 0.0)
        dcat = jnp.concatenate([dcur, dnext], axis=0)
        dx = dcur * w[3:4]
        for k in range(3):
            dx = dx + pltpu.roll(dcat, tm + 8 - (3 - k), 0)[:tm] * w[k:k + 1]
        dx_ref[...] = dx.astype(BF16)
        xprev = jnp.where(t > 0, xp_ref[...], 0.0)
        xcat = jnp.concatenate([xprev, x_ref[...]], axis=0)
        rows = [jnp.sum(dcur * pltpu.roll(xcat, 3 - k, 0)[8:], axis=0, keepdims=True) for k in range(3)]
        rows.append(jnp.sum(dcur * xcat[8:], axis=0, keepdims=True))

        @pl.when(t == 0)
        def _():
            dw_ref[...] = jnp.zeros_like(dw_ref)
        dw_ref[...] += jnp.concatenate(rows, axis=0)

    def nxt(t):
        return jnp.minimum((t + 1) * (tm // 8), T // 8 - 1)

    cur = pl.BlockSpec((tm, D), lambda kd, t: (t, kd))
    return _pc(body, name=name, out_shape=(_sds((T, 3 * D), BF16), _sds((4, 3 * D), F32)), grid=(3, n_t),
               in_specs=[pl.BlockSpec((HEADS_A, tm, 128), lambda kd, t: (kd, t, 0)),
                         pl.BlockSpec((HEADS_A, 8, 128), lambda kd, t: (kd, nxt(t), 0)),
                         cur, pl.BlockSpec((8, D), lambda kd, t: (nxt(t), kd)),
                         cur, pl.BlockSpec((8, D), lambda kd, t: (jnp.maximum(t * (tm // 8) - 1, 0), kd)),
                         pl.BlockSpec((4, D), lambda kd, t: (0, kd))],
               out_specs=(cur, pl.BlockSpec((4, D), lambda kd, t: (0, kd))),
               sem=("parallel", "arbitrary"))(dqkv, dqkv, c, c, proj, proj, wconv)


def _chunk_masks(n):
    ri = lax.broadcasted_iota(jnp.int32, (n, n), 0)
    ci = lax.broadcasted_iota(jnp.int32, (n, n), 1)
    same = (ri // CHUNK) == (ci // CHUNK)
    return same & (ri >= ci), same & (ri <= ci)


def gdn_gate_fwd(proj, al, dtb, name):
    T = proj.shape[0]
    tg = _tile(T, PREP_T)

    def body(ba_ref, al_ref, dtb_ref, o_ref):
        x = ba_ref[...]
        lane = lax.broadcasted_iota(jnp.int32, x.shape, 1)
        is_a = (lane >= HEADS_A) & (lane < 2 * HEADS_A)
        g = jnp.where(is_a, -jnp.exp(al_ref[...]) * _softplus(x + dtb_ref[...]), 0.0)
        lower, _ = _chunk_masks(tg)
        gc = _dot(lower.astype(F32), g, precision=HI)
        o_ref[...] = jnp.where(lane < HEADS_A, _sigmoid(x), gc)

    vec = pl.BlockSpec((1, 128), lambda i: (0, 0))
    return _pc(body, name=name, out_shape=_sds((T, 128), F32), grid=(T // tg,),
               in_specs=[pl.BlockSpec((tg, 128), lambda i: (i, BA_BLK)), vec, vec],
               out_specs=pl.BlockSpec((tg, 128), lambda i: (i, 0)), sem=("parallel",))(proj, al, dtb)


def gdn_gate_bwd(proj, al, dtb, dgb, name):
    T = proj.shape[0]
    tg = _tile(T, PREP_T)

    def body(ba_ref, al_ref, dtb_ref, dgb_ref, dba_ref, dal_ref, ddt_ref):
        x, d = ba_ref[...], dgb_ref[...]
        lane = lax.broadcasted_iota(jnp.int32, x.shape, 1)
        is_b = lane < HEADS_A
        is_a = (lane >= HEADS_A) & (lane < 2 * HEADS_A)
        beta = _sigmoid(x)
        e_a = jnp.exp(al_ref[...])
        z = x + dtb_ref[...]
        g = jnp.where(is_a, -e_a * _softplus(z), 0.0)
        _, upper = _chunk_masks(tg)
        dg = _dot(upper.astype(F32), jnp.where(is_a, d, 0.0), precision=HI)
        da = jnp.where(is_a, dg * (-e_a) * _sigmoid(z), 0.0)
        db = jnp.where(is_b, d * beta * (1.0 - beta), 0.0)
        dba_ref[...] = (da + db).astype(BF16)

        @pl.when(pl.program_id(0) == 0)
        def _():
            dal_ref[...] = jnp.zeros_like(dal_ref)
            ddt_ref[...] = jnp.zeros_like(ddt_ref)
        dal_ref[...] += jnp.sum(dg * g, axis=0, keepdims=True)
        ddt_ref[...] += jnp.sum(da, axis=0, keepdims=True)

    vec = pl.BlockSpec((1, 128), lambda i: (0, 0))
    blk = pl.BlockSpec((tg, 128), lambda i: (i, 0))
    return _pc(body, name=name, out_shape=(_sds((T, 128), BF16), _sds((1, 128), F32), _sds((1, 128), F32)),
               grid=(T // tg,), in_specs=[pl.BlockSpec((tg, 128), lambda i: (i, BA_BLK)), vec, vec, blk],
               out_specs=(blk, vec, vec), sem=("arbitrary",))(proj, al, dtb, dgb)


def _bmm(a, b, dims, precision=None):
    return lax.dot_general(a, b, dims, preferred_element_type=F32, precision=precision)


B_NN = (((2,), (1,)), ((0,), (0,)))
B_NT = (((2,), (2,)), ((0,), (0,)))


def _select_lane(x, lane_index):
    lane = lax.broadcasted_iota(jnp.int32, x.shape, x.ndim - 1)
    return jnp.sum(jnp.where(lane == lane_index, x, 0.0), axis=-1, keepdims=True)


B_TN = (((1,), (1,)), ((0,), (0,)))


def _bmm_split(a, b, dims):
    ah, bh = a.astype(BF16), b.astype(BF16)
    al, bl = (a - ah.astype(F32)).astype(BF16), (b - bh.astype(F32)).astype(BF16)
    return _bmm(ah, bh, dims) + (_bmm(ah, bl, dims) + _bmm(al, bh, dims))


@jax.custom_vjp
def _bmm_f32(a, b):
    return _bmm_split(a, b, B_NN)


def _bmm_f32_fwd(a, b):
    return _bmm_split(a, b, B_NN), (a, b)


def _bmm_f32_bwd(res, dc):
    a, b = res
    return _bmm_split(dc, b, B_NT), _bmm_split(a, dc, B_TN)


_bmm_f32.defvjp(_bmm_f32_fwd, _bmm_f32_bwd)


def _tri_inverse(lmat):
    ri = lax.broadcasted_iota(jnp.int32, lmat.shape, 1)
    ci = lax.broadcasted_iota(jnp.int32, lmat.shape, 2)
    inv = jnp.where(ri == ci, 1.0, 0.0) - lmat
    power = lmat
    for _ in range(5):
        power = _bmm_split(power, power, B_NN)
        inv = inv + _bmm_split(inv, power, B_NN)
    return inv


def _stored_inverse(x):
    @jax.custom_vjp
    def inverse(lmat):
        return x

    def fwd(lmat):
        return x, None

    def bwd(_, dx):
        return (-_bmm_split(_bmm_split(x, dx, B_TN), x, B_NT),)

    inverse.defvjp(fwd, bwd)
    return inverse


def _gdn_prep(q, k, v, gb, h, inverse):
    nb = q.shape[0]
    beta = _select_lane(gb, h)
    gc = _select_lane(gb, HEADS_A + h)
    ri = lax.broadcasted_iota(jnp.int32, (nb, CHUNK, CHUNK), 1)
    ci = lax.broadcasted_iota(jnp.int32, (nb, CHUNK, CHUNK), 2)
    lower, strict, eye = ri >= ci, ri > ci, ri == ci
    gcol = jnp.broadcast_to(gc, (nb, CHUNK, CHUNK))
    grow = _bmm_f32(jnp.ones((nb, CHUNK, CHUNK), F32), jnp.where(eye, gcol, 0.0))
    decay = jnp.where(lower, jnp.exp(jnp.where(lower, gcol - grow, 0.0)), 0.0)
    kb = k * beta
    kbf = k.astype(BF16)
    inv = inverse(jnp.where(strict, _bmm(kb.astype(BF16), kbf, B_NT) * decay, 0.0))
    eg = jnp.exp(gc)
    sol = _bmm_f32(inv, jnp.concatenate([v * beta, kb * eg], axis=-1))
    aqk = _bmm(q.astype(BF16), kbf, B_NT) * decay
    g_last = gc[:, CHUNK - 1:CHUNK, :]
    gl = jnp.broadcast_to(jnp.exp(g_last), (nb, 1, 128))
    return (sol[..., :DK], sol[..., DK:], q * eg, k * jnp.exp(g_last - gc), aqk, gl), inv


def gdn_prep_fwd(qkv, gb, name):
    T = qkv.shape[1]
    tp = _tile(T, PREP_T)
    nb = tp // CHUNK

    def body(q_ref, k_ref, v_ref, gb_ref, u_ref, w_ref, qd_ref, kd_ref, a_ref, gl_ref, inv_ref):
        h = pl.program_id(1)
        shp = (nb, CHUNK, 128)
        (u, w, qd, kd, aqk, gl), inv = _gdn_prep(q_ref[0].reshape(shp), k_ref[0].reshape(shp), v_ref[0].reshape(shp),
                                                 gb_ref[...].reshape(shp), h, _tri_inverse)
        u_ref[0] = u.reshape(tp, 128)
        w_ref[0] = w.reshape(tp, 128)
        qd_ref[0] = qd.reshape(tp, 128)
        kd_ref[0] = kd.reshape(tp, 128)
        a_ref[0] = aqk.reshape(tp, CHUNK)
        gl_ref[0] = gl.reshape(nb, 1, 128)
        inv_ref[0] = inv.reshape(tp, CHUNK)

    def head(off):
        return pl.BlockSpec((1, tp, 128), lambda n, h: (h + off, n, 0))

    per_head = _sds((HEADS_A, T, 128), F32)
    narrow = pl.BlockSpec((1, tp, CHUNK), lambda n, h: (h, n, 0))
    return _pc(body, name=name,
               out_shape=(per_head, per_head, per_head, per_head, _sds((HEADS_A, T, CHUNK), F32),
                          _sds((HEADS_A, T // CHUNK, 1, 128), F32), _sds((HEADS_A, T, CHUNK), F32)),
               grid=(T // tp, HEADS_A),
               in_specs=[head(0), head(HEADS_A), head(2 * HEADS_A), pl.BlockSpec((tp, 128), lambda n, h: (n, 0))],
               out_specs=(head(0), head(0), head(0), head(0), narrow,
                          pl.BlockSpec((1, nb, 1, 128), lambda n, h: (h, n, 0, 0)), narrow),
               sem=("parallel", "parallel"))(qkv, qkv, qkv, gb)


def gdn_prep_bwd(qkv, gb, inv, du, dw, dqd, dkd, da, dgl, name):
    T = qkv.shape[1]
    tp = _tile(T, PREP_T)
    nb = tp // CHUNK

    def body(q_ref, k_ref, v_ref, gb_ref, inv_ref, du_ref, dw_ref, dqd_ref, dkd_ref, da_ref, dgl_ref, dqkv_ref, dgb_ref):
        h = pl.program_id(1)
        shp = (nb, CHUNK, 128)
        stored = _stored_inverse(inv_ref[0].reshape(nb, CHUNK, CHUNK))
        _, vjp = jax.vjp(lambda q, k, v, gb: _gdn_prep(q, k, v, gb, h, stored)[0], q_ref[0].reshape(shp),
                         k_ref[0].reshape(shp), v_ref[0].reshape(shp), gb_ref[...].reshape(shp))
        dq, dk, dv, dgb = vjp((du_ref[0].reshape(shp), dw_ref[0].reshape(shp), dqd_ref[0].reshape(shp),
                               dkd_ref[0].reshape(shp), da_ref[0].reshape(nb, CHUNK, CHUNK), dgl_ref[0].reshape(nb, 1, 128)))
        dqkv_ref[h] = dq.reshape(tp, 128)
        dqkv_ref[HEADS_A + h] = dk.reshape(tp, 128)
        dqkv_ref[2 * HEADS_A + h] = dv.reshape(tp, 128)

        @pl.when(h == 0)
        def _():
            dgb_ref[...] = jnp.zeros_like(dgb_ref)
        dgb_ref[...] += dgb.reshape(tp, 128)

    def head(off):
        return pl.BlockSpec((1, tp, 128), lambda n, h: (h + off, n, 0))

    narrow = pl.BlockSpec((1, tp, CHUNK), lambda n, h: (h, n, 0))
    return _pc(body, name=name, out_shape=(_sds((N_QKV_BLK, T, 128), F32), _sds((T, 128), F32)),
               grid=(T // tp, HEADS_A),
               in_specs=[head(0), head(HEADS_A), head(2 * HEADS_A), pl.BlockSpec((tp, 128), lambda n, h: (n, 0)), narrow,
                         head(0), head(0), head(0), head(0), narrow,
                         pl.BlockSpec((1, nb, 1, 128), lambda n, h: (h, n, 0, 0))],
               out_specs=(pl.BlockSpec((N_QKV_BLK, tp, 128), lambda n, h: (0, n, 0)),
                          pl.BlockSpec((tp, 128), lambda n, h: (n, 0))),
               sem=("parallel", "arbitrary"))(qkv, qkv, qkv, gb, inv, du, dw, dqd, dkd, da, dgl)


def gdn_scan_fwd(u, w, qd, kd, aqk, gl, name):
    T = u.shape[1]
    n_chunks = T // CHUNK

    def body(u_ref, w_ref, qd_ref, kd_ref, a_ref, gl_ref, o_ref, sin_ref, state):
        @pl.when(pl.program_id(0) == 0)
        def _():
            state[...] = jnp.zeros_like(state)
        for h in range(HEADS_A):
            s = state[h]
            sin_ref[0, h] = s
            sb = s.astype(BF16)
            both = _dot(jnp.concatenate([w_ref[h], qd_ref[h]], axis=0).astype(BF16), sb)
            vn = (u_ref[h] - both[:CHUNK]).astype(BF16)
            o_ref[h] = both[CHUNK:] + _dot(a_ref[h].astype(BF16), vn)
            state[h] = s * gl_ref[h, 0] + _dot(kd_ref[h].astype(BF16), vn, TN)

    blk = pl.BlockSpec((HEADS_A, CHUNK, 128), lambda n: (0, n, 0))
    return _pc(body, name=name,
               out_shape=(_sds((HEADS_A, T, 128), F32), _sds((n_chunks, HEADS_A, DK, 128), F32)), grid=(n_chunks,),
               in_specs=[blk, blk, blk, blk, pl.BlockSpec((HEADS_A, CHUNK, CHUNK), lambda n: (0, n, 0)),
                         pl.BlockSpec((HEADS_A, 1, 1, 128), lambda n: (0, n, 0, 0))],
               out_specs=(blk, pl.BlockSpec((1, HEADS_A, DK, 128), lambda n: (n, 0, 0, 0))),
               scratch=[pltpu.VMEM((HEADS_A, DK, 128), F32)], sem=("arbitrary",))(u, w, qd, kd, aqk, gl)


def gdn_scan_bwd(u, w, qd, kd, aqk, gl, sin, do, name):
    T = u.shape[1]
    n_chunks = T // CHUNK

    def body(u_ref, w_ref, qd_ref, kd_ref, a_ref, gl_ref, sin_ref, do_ref,
             du_ref, dw_ref, dqd_ref, dkd_ref, da_ref, dgl_ref, dstate):
        @pl.when(pl.program_id(0) == 0)
        def _():
            dstate[...] = jnp.zeros_like(dstate)
        lane0 = lax.broadcasted_iota(jnp.int32, (1, 128), 1) == 0
        for h in range(HEADS_A):
            s = sin_ref[0, h]
            sb = s.astype(BF16)
            wb, qdb, kdb = w_ref[h].astype(BF16), qd_ref[h].astype(BF16), kd_ref[h].astype(BF16)
            ab, dob = a_ref[h].astype(BF16), do_ref[h].astype(BF16)
            vn = (u_ref[h] - _dot(wb, sb)).astype(BF16)
            ds_out = dstate[h]
            dsb = ds_out.astype(BF16)
            dqd_ref[h] = _dot(dob, sb, NT)
            da_ref[h] = _dot(dob, vn, NT)
            dv = _dot(ab, dob, TN) + _dot(kdb, dsb)
            dkd_ref[h] = _dot(vn, dsb, NT)
            dgl_ref[h, 0] = jnp.where(lane0, jnp.sum(ds_out * s), 0.0)
            du_ref[h] = dv
            dvb = dv.astype(BF16)
            dw_ref[h] = -_dot(dvb, sb, NT)
            dstate[h] = ds_out * gl_ref[h, 0] + _dot(qdb, dob, TN) - _dot(wb, dvb, TN)

    last = n_chunks - 1
    blk = pl.BlockSpec((HEADS_A, CHUNK, 128), lambda n: (0, last - n, 0))
    ablk = pl.BlockSpec((HEADS_A, CHUNK, CHUNK), lambda n: (0, last - n, 0))
    glblk = pl.BlockSpec((HEADS_A, 1, 1, 128), lambda n: (0, last - n, 0, 0))
    per_head = _sds((HEADS_A, T, 128), F32)
    return _pc(body, name=name,
               out_shape=(per_head, per_head, per_head, per_head, _sds((HEADS_A, T, CHUNK), F32),
                          _sds((HEADS_A, n_chunks, 1, 128), F32)), grid=(n_chunks,),
               in_specs=[blk, blk, blk, blk, ablk, glblk,
                         pl.BlockSpec((1, HEADS_A, DK, 128), lambda n: (last - n, 0, 0, 0)), blk],
               out_specs=(blk, blk, blk, blk, ablk, glblk),
               scratch=[pltpu.VMEM((HEADS_A, DK, 128), F32)], sem=("arbitrary",))(u, w, qd, kd, aqk, gl, sin, do)


def gdn_outnorm_fwd(o, proj, wn, name):
    T = o.shape[1]
    tm = _tile(T, 512)

    def body(o_ref, z_ref, wn_ref, y_ref):
        for h in range(HEADS_A):
            z = z_ref[:, 128 * h:128 * (h + 1)]
            y_ref[:, 128 * h:128 * (h + 1)] = (_rms_fwd(o_ref[h], wn_ref[...]) * (z * _sigmoid(z))).astype(BF16)

    return _pc(body, name=name, out_shape=_sds((T, D), BF16), grid=(T // tm,),
               in_specs=[pl.BlockSpec((HEADS_A, tm, 128), lambda i: (0, i, 0)),
                         pl.BlockSpec((tm, D), lambda i: (i, Z_BLK0 * 128 // D)), pl.BlockSpec((1, 128), lambda i: (0, 0))],
               out_specs=pl.BlockSpec((tm, D), lambda i: (i, 0)), sem=("parallel",))(o, proj, wn)


def gdn_outnorm_bwd(o, proj, wn, dy, name):
    T = o.shape[1]
    tm = _tile(T, 512)

    def body(o_ref, z_ref, wn_ref, dy_ref, do_ref, dz_ref, dwn_ref):
        wn = wn_ref[...]
        acc = jnp.zeros((1, 128), F32)
        for h in range(HEADS_A):
            cols = slice(128 * h, 128 * (h + 1))
            z, dyh, ov = z_ref[:, cols], dy_ref[:, cols], o_ref[h]
            sg = _sigmoid(z)
            do, dwn = _rms_bwd(ov, wn, dyh * (z * sg))
            do_ref[h] = do
            acc = acc + dwn
            dz_ref[:, cols] = (dyh * _rms_fwd(ov, wn) * (sg * (1.0 + z * (1.0 - sg)))).astype(BF16)

        @pl.when(pl.program_id(0) == 0)
        def _():
            dwn_ref[...] = jnp.zeros_like(dwn_ref)
        dwn_ref[...] += acc

    row = pl.BlockSpec((tm, D), lambda i: (i, 0))
    vec = pl.BlockSpec((1, 128), lambda i: (0, 0))
    hblk = pl.BlockSpec((HEADS_A, tm, 128), lambda i: (0, i, 0))
    return _pc(body, name=name, out_shape=(_sds((HEADS_A, T, 128), F32), _sds((T, D), BF16), _sds((1, 128), F32)),
               grid=(T // tm,),
               in_specs=[hblk, pl.BlockSpec((tm, D), lambda i: (i, Z_BLK0 * 128 // D)), vec, row],
               out_specs=(hblk, row, vec), sem=("arbitrary",))(o, proj, wn, dy)


def gdn_forward(x, nw, w_in, wconv, al, dtb, wn, w_out, tag):
    h = rmsnorm_bf16(x, nw, f"{tag}_norm")
    proj = mm_nn(h, w_in, f"{tag}_proj")
    c, qkv = gdn_conv_fwd(proj, wconv, f"{tag}_conv")
    gb = gdn_gate_fwd(proj, al, dtb, f"{tag}_gate")
    u, w, qd, kd, aqk, gl, inv = gdn_prep_fwd(qkv, gb, f"{tag}_prep")
    o, sin = gdn_scan_fwd(u, w, qd, kd, aqk, gl, f"{tag}_scan")
    on = gdn_outnorm_fwd(o, proj, wn, f"{tag}_outnorm")
    y = mm_nn(on, w_out, f"{tag}_out", residual=x)
    return y, (x, h, proj, c, qkv, gb, inv, (u, w, qd, kd, aqk, gl), sin, o, on)


def gdn_backward(dout, saved, nw, w_in, wconv, al, dtb, wn, w_out, tag):
    x, h, proj, c, qkv, gb, inv, prep, sin, o, on = saved
    d_on = mm_nt(dout, w_out, f"{tag}_out_bwd")
    dw_out = mm_tn(on, dout, f"{tag}_out_wgrad")
    do, dz, dwn = gdn_outnorm_bwd(o, proj, wn, d_on, f"{tag}_outnorm_bwd")
    du, dw, dqd, dkd, da, dgl = gdn_scan_bwd(*prep, sin, do, f"{tag}_scan_bwd")
    dqkv, dgb = gdn_prep_bwd(qkv, gb, inv, du, dw, dqd, dkd, da, dgl, f"{tag}_prep_bwd")
    dba, dal, ddt = gdn_gate_bwd(proj, al, dtb, dgb, f"{tag}_gate_bwd")
    dpre, dwconv = gdn_conv_bwd(dqkv, c, proj, wconv, f"{tag}_conv_bwd")
    dproj = jnp.concatenate([dpre, dz, dba], axis=1)
    dw_in = mm_tn(h, dproj, f"{tag}_proj_wgrad")
    dh = mm_nt(dproj, w_in, f"{tag}_proj_bwd")
    dx, dnw = rmsnorm_bwd_add(x, nw, dh, dout, f"{tag}_norm_bwd")
    return dx, dnw, dw_in, dwconv, dal, ddt, dwn, dw_out


N_KV, GROUP = 4, 4
KV_COLS = 2 * N_KV * B_HD
B_COLS = D + KV_COLS


def _swa_block(q, kp, kc, vp, vc, sk, first):
    rows = GROUP * B_BLK
    qi = lax.broadcasted_iota(jnp.int32, (rows, B_BLK), 0) % B_BLK
    kj = lax.broadcasted_iota(jnp.int32, (rows, B_BLK), 1)
    from_cur = kj <= qi
    outs = []
    for j in range(N_KV):
        heads = range(GROUP * j, GROUP * (j + 1))
        cols = slice(j * B_HD, (j + 1) * B_HD)
        qs = jnp.concatenate([q[:, hq * B_HD:(hq + 1) * B_HD] for hq in heads], axis=0).astype(BF16)
        s_cur = _dot(qs, kc[:, cols].astype(BF16), NT)
        s_prev = jnp.where(first, -1e30, _dot(qs, kp[:, cols].astype(BF16), NT))
        s = jnp.where(from_cur, s_cur, s_prev) * (B_HD ** -0.5)
        sink = jnp.concatenate([jnp.broadcast_to(sk[:, hq:hq + 1], (B_BLK, 1)) for hq in heads], axis=0)
        m = lax.stop_gradient(jnp.maximum(jnp.max(s, axis=-1, keepdims=True), sink))
        p = jnp.exp(s - m)
        p = p / (jnp.sum(p, axis=-1, keepdims=True) + jnp.exp(sink - m))
        o = (_dot(jnp.where(from_cur, p, 0.0).astype(BF16), vc[:, cols].astype(BF16))
             + _dot(jnp.where(from_cur, 0.0, p).astype(BF16), vp[:, cols].astype(BF16)))
        outs += [o[g * B_BLK:(g + 1) * B_BLK] for g in range(GROUP)]
    return jnp.concatenate(outs, axis=1)


def swa_core_fwd(proj, sk, name):
    T = proj.shape[0]
    half = N_KV * B_HD

    def body(q_ref, kvc_ref, kvp_ref, sk_ref, o_ref):
        kvc, kvp = kvc_ref[...], kvp_ref[...]
        o_ref[...] = _swa_block(q_ref[...], kvp[:, :half], kvc[:, :half], kvp[:, half:], kvc[:, half:], sk_ref[...],
                                pl.program_id(0) == 0).astype(BF16)

    return _pc(body, name=name, out_shape=_sds((T, D), BF16), grid=(T // B_BLK,),
               in_specs=[pl.BlockSpec((B_BLK, D), lambda n: (n, 0)),
                         pl.BlockSpec((B_BLK, KV_COLS), lambda n: (n, D // KV_COLS)),
                         pl.BlockSpec((B_BLK, KV_COLS), lambda n: (jnp.maximum(n - 1, 0), D // KV_COLS)),
                         pl.BlockSpec((1, 128), lambda n: (0, 0))],
               out_specs=pl.BlockSpec((B_BLK, D), lambda n: (n, 0)), sem=("parallel",))(proj, proj, proj, sk)


def swa_core_bwd(proj, sk, do, name):
    T = proj.shape[0]
    last = T // B_BLK - 1
    half = N_KV * B_HD

    def body(q_ref, kvc_ref, kvp_ref, sk_ref, do_ref, dproj_ref, dbias_ref, dsk_ref, carry):
        step = pl.program_id(0)
        first = step == last

        @pl.when(step == 0)
        def _():
            carry[...] = jnp.zeros_like(carry)
            dbias_ref[...] = jnp.zeros_like(dbias_ref)
            dsk_ref[...] = jnp.zeros_like(dsk_ref)
        kvc, kvp = kvc_ref[...], kvp_ref[...]
        _, vjp = jax.vjp(functools.partial(_swa_block, first=first), q_ref[...], kvp[:, :half], kvc[:, :half],
                         kvp[:, half:], kvc[:, half:], sk_ref[...])
        dq, dkp, dkc, dvp, dvc, dsk = vjp(do_ref[...])
        dkv = jnp.concatenate([dkc, dvc], axis=1) + carry[...]
        carry[...] = jnp.concatenate([dkp, dvp], axis=1)
        row = jnp.concatenate([dq, dkv], axis=1)
        dproj_ref[...] = row.astype(BF16)
        dbias_ref[...] += jnp.sum(row, axis=0, keepdims=True)
        dsk_ref[...] += dsk

    return _pc(body, name=name, out_shape=(_sds((T, B_COLS), BF16), _sds((1, B_COLS), F32), _sds((1, 128), F32)),
               grid=(T // B_BLK,),
               in_specs=[pl.BlockSpec((B_BLK, D), lambda n: (last - n, 0)),
                         pl.BlockSpec((B_BLK, KV_COLS), lambda n: (last - n, D // KV_COLS)),
                         pl.BlockSpec((B_BLK, KV_COLS), lambda n: (jnp.maximum(last - n - 1, 0), D // KV_COLS)),
                         pl.BlockSpec((1, 128), lambda n: (0, 0)), pl.BlockSpec((B_BLK, D), lambda n: (last - n, 0))],
               out_specs=(pl.BlockSpec((B_BLK, B_COLS), lambda n: (last - n, 0)),
                          pl.BlockSpec((1, B_COLS), lambda n: (0, 0)), pl.BlockSpec((1, 128), lambda n: (0, 0))),
               scratch=[pltpu.VMEM((B_BLK, KV_COLS), F32)], sem=("arbitrary",))(proj, proj, proj, sk, do)


def col_sum(a, name):
    T, N = a.shape
    tm = _tile(T, 1024)

    def body(a_ref, o_ref):
        @pl.when(pl.program_id(0) == 0)
        def _():
            o_ref[...] = jnp.zeros_like(o_ref)
        o_ref[...] += jnp.sum(a_ref[...].astype(F32), axis=0, keepdims=True)

    return _pc(body, name=name, out_shape=_sds((1, N), F32), grid=(T // tm,),
               in_specs=[pl.BlockSpec((tm, N), lambda i: (i, 0))], out_specs=pl.BlockSpec((1, N), lambda i: (0, 0)),
               sem=("arbitrary",))(a)


def swa_forward(x, nw, w_in, b_in, sk, w_out, b_out, tag):
    h = rmsnorm_bf16(x, nw, f"{tag}_norm")
    proj = mm_nn(h, w_in, f"{tag}_proj", bias=b_in)
    o = swa_core_fwd(proj, sk, f"{tag}_core")
    y = mm_nn(o, w_out, f"{tag}_out", bias=b_out, residual=x)
    return y, (x, h, proj, o)


def swa_backward(dout, saved, nw, w_in, b_in, sk, w_out, b_out, tag):
    x, h, proj, o = saved
    do = mm_nt(dout, w_out, f"{tag}_out_bwd")
    dw_out = mm_tn(o, dout, f"{tag}_out_wgrad")
    db_out = col_sum(dout, f"{tag}_out_bias_grad")
    dproj, db_in, dsk = swa_core_bwd(proj, sk, do, f"{tag}_core_bwd")
    dw_in = mm_tn(h, dproj, f"{tag}_proj_wgrad")
    dh = mm_nt(dproj, w_in, f"{tag}_proj_bwd")
    dx, dnw = rmsnorm_bwd_add(x, nw, dh, dout, f"{tag}_norm_bwd")
    return dx, dnw, dw_in, db_in, dsk, dw_out, db_out


MESH = pl.DeviceIdType.MESH


def _position():
    return lax.axis_index("x"), lax.axis_index("y"), lax.axis_index("c")


def _slot(x, y, c):
    return 4 * x + 2 * y + c


def _peer(x, y, c, k):
    return (1 - x if k & 4 else x, 1 - y if k & 2 else y, 1 - c if k & 1 else c)


HBM_SPEC = pl.BlockSpec(memory_space=pltpu.HBM)
SEM_SPEC = pl.BlockSpec(memory_space=pltpu.SEMAPHORE)
DEP_SPEC = pl.BlockSpec(memory_space=pl.ANY)
SIDE_EFFECT = pltpu.SideEffectType.DATAFLOW_SIDE_EFFECTING
N_PEERS = N_DEV - 1


def _push_copies(srcs, lands, send_sems, recv_sems, scatter):
    x, y, c = _position()
    me = _slot(x, y, c)
    copies = []
    for k in (1, 2, 4, 3, 5, 6, 7):
        peer = _peer(x, y, c, k)
        for a in range(len(srcs)):
            copies.append(pltpu.make_async_remote_copy(
                src_ref=srcs[a].at[_slot(*peer)] if scatter else srcs[a], dst_ref=lands[a].at[me],
                send_sem=send_sems.at[N_PEERS * a + k - 1], recv_sem=recv_sems.at[N_PEERS * a + k - 1],
                device_id=peer, device_id_type=MESH))
    return copies


def push_start(srcs, lands, name, scatter):
    n = len(srcs)

    def body(*refs):
        for cp in _push_copies(refs[:n], refs[n:2 * n], refs[2 * n], refs[2 * n + 1], scatter):
            cp.start()
        refs[-1][...] = jnp.zeros_like(refs[-1])

    passed = [pltpu.HBM(t.shape, t.dtype) for t in list(srcs) + list(lands)]
    res = pl.pallas_call(
        body, name=name,
        out_shape=(pltpu.SemaphoreType.DMA((N_PEERS * n,)), pltpu.SemaphoreType.DMA((N_PEERS * n,)), *passed, _sds((8, 128), F32)),
        in_specs=[HBM_SPEC] * (2 * n),
        out_specs=(SEM_SPEC, SEM_SPEC, *([HBM_SPEC] * (2 * n)), pl.BlockSpec(memory_space=pltpu.VMEM)),
        input_output_aliases={i: 2 + i for i in range(2 * n)},
        compiler_params=pltpu.CompilerParams(has_side_effects=SIDE_EFFECT),
    )(*[pltpu.with_memory_space_constraint(t, pltpu.HBM) for t in list(srcs) + list(lands)])
    return (res[0], res[1], list(res[2:2 + n]), list(res[2 + n:2 + 2 * n])), res[-1]


def push_wait(handles, after, name, scatter):
    send_sems, recv_sems, srcs, lands = handles
    n = len(srcs)
    after = tuple(after) if isinstance(after, (tuple, list)) else (after,)

    def body(*refs):
        for cp in _push_copies(refs[:n], refs[n:2 * n], refs[2 * n], refs[2 * n + 1], scatter):
            cp.wait_send()
            cp.wait_recv()

    res = pl.pallas_call(
        body, name=name, out_shape=tuple(pltpu.HBM(t.shape, t.dtype) for t in srcs + lands),
        in_specs=[HBM_SPEC] * (2 * n) + [SEM_SPEC, SEM_SPEC] + [DEP_SPEC] * len(after), out_specs=tuple([HBM_SPEC] * (2 * n)),
        input_output_aliases={i: i for i in range(2 * n)},
        compiler_params=pltpu.CompilerParams(has_side_effects=SIDE_EFFECT),
    )(*srcs, *lands, send_sems, recv_sems, *after)
    return list(res[n:])


def gather_start(shards, name):
    me = _slot(*_position())
    lands = [lax.dynamic_update_slice(lax.empty((N_DEV,) + t.shape, t.dtype), t[None], (me,) + (0,) * t.ndim) for t in shards]
    return push_start(shards, lands, name, scatter=False)


def exchange_start(parts, name):
    me = _slot(*_position())
    lands = [lax.dynamic_update_slice(lax.empty(t.shape, t.dtype), lax.dynamic_index_in_dim(t, me, 0, keepdims=True),
                                      (me,) + (0,) * (t.ndim - 1)) for t in parts]
    return push_start(parts, lands, name, scatter=True)


def all_reduce_small(part, name):
    R, C = part.shape

    def body(p_ref, o_ref, buf, send_sems, recv_sems):
        x, y, c = _position()
        me = _slot(x, y, c)
        copies = []
        for k in range(1, N_DEV):
            copies.append(pltpu.make_async_remote_copy(
                src_ref=p_ref, dst_ref=buf.at[me], send_sem=send_sems.at[k - 1], recv_sem=recv_sems.at[k - 1],
                device_id=_peer(x, y, c, k), device_id_type=MESH))
            copies[-1].start()
        buf[me] = p_ref[...]
        for cp in copies:
            cp.wait()
        acc = buf[0]
        for s in range(1, N_DEV):
            acc = acc + buf[s]
        o_ref[...] = acc

    vmem = pl.BlockSpec(memory_space=pltpu.VMEM)
    return _pc(body, name=name, out_shape=_sds((R, C), F32), in_specs=[vmem], out_specs=vmem,
               scratch=[pltpu.VMEM((N_DEV, R, C), F32), pltpu.SemaphoreType.DMA((7,)), pltpu.SemaphoreType.DMA((7,))],
               )(part)


def _row_tile(rows, cols):
    best = rows
    for t in range(16, rows, 16):
        if rows % t == 0 and t * cols * 4 <= (1 << 20):
            best = t
    return best


def adam_update(parts, w, m, v, name):
    n_layers = len(parts)
    P, R, C = parts[0].shape
    tr = _row_tile(R, C)
    n_t = R // tr

    def body(*refs):
        p_refs = refs[:n_layers]
        w_ref, m_ref, v_ref, g_ref, d_ref, nm_ref, nv_ref = refs[n_layers:]
        for layer in range(n_layers):
            @pl.when(pl.program_id(0) == layer)
            def _(p_ref=p_refs[layer]):
                g = p_ref[0].astype(F32)
                for s in range(1, P):
                    g = g + p_ref[s].astype(F32)
                new_m = ADAM_B1 * m_ref[0] + (1.0 - ADAM_B1) * g
                new_v = ADAM_B2 * v_ref[0] + (1.0 - ADAM_B2) * (g * g)
                m_hat = new_m / (1.0 - ADAM_B1 ** ADAM_STEP)
                v_hat = new_v / (1.0 - ADAM_B2 ** ADAM_STEP)
                g_ref[0] = g
                d_ref[0] = -ADAM_LR * (m_hat / (jnp.sqrt(v_hat) + ADAM_EPS) + ADAM_WD * w_ref[0])
                nm_ref[0] = new_m
                nv_ref[0] = new_v

    def part_spec(layer):
        return pl.BlockSpec((P, tr, C), lambda l_, i: (0, jnp.where(l_ == layer, i, jnp.where(l_ < layer, 0, n_t - 1)), 0))

    blk = pl.BlockSpec((1, tr, C), lambda l_, i: (l_, i, 0))
    out = _sds((n_layers, R, C), F32)
    return _pc(body, name=name, out_shape=(out, out, out, out), grid=(n_layers, n_t),
               in_specs=[part_spec(layer) for layer in range(n_layers)] + [blk, blk, blk],
               out_specs=(blk, blk, blk, blk), sem=("arbitrary", "arbitrary"))(*parts, w, m, v)


WEIGHTS = ("ffn1_norm", "ffn1_w_gu", "ffn1_w_down", "mix_norm", "ffn2_norm", "ffn2_w_gu", "ffn2_w_down", "a_w_in",
           "a_w_conv", "a_A_log", "a_dt_bias", "a_out_norm", "a_w_out", "b_w_in", "b_b_in", "b_sinks", "b_w_out",
           "b_b_out", "final_norm")
SHARDED = ("ffn1_w_gu", "ffn1_w_down", "ffn2_w_gu", "ffn2_w_down", "a_w_in", "a_w_conv", "a_w_out", "b_w_in", "b_b_in",
           "b_w_out", "b_b_out")
MISC_LANES = dict(a_A_log=(0, 8), a_dt_bias=(8, 16), b_sinks=(16, 32), a_out_norm=(128, 256))
LOSS_LANE = 256


def _pack_small(t):
    misc = jnp.zeros((D,), F32)
    for key, (lo, hi) in MISC_LANES.items():
        misc = misc.at[lo:hi].set(t[key].reshape(-1))
    if "loss" in t:
        misc = misc.at[LOSS_LANE].set(t["loss"])
    return jnp.concatenate([t["ffn1_norm"], t["mix_norm"], t["ffn2_norm"], t["final_norm"].reshape(1, D), misc[None]], axis=0)


def _unpack_small(p, like):
    out = dict(ffn1_norm=p[0:2], mix_norm=p[2:4], ffn2_norm=p[4:6], final_norm=p[6])
    for key, (lo, hi) in MISC_LANES.items():
        out[key] = p[7, lo:hi].reshape(like[key].shape)
    return out


def kernel(x, ffn1_norm, ffn1_w_gu, ffn1_w_down, mix_norm, ffn2_norm, ffn2_w_gu, ffn2_w_down, a_w_in, a_w_conv, a_A_log, a_dt_bias, a_out_norm, a_w_out, b_w_in, b_b_in, b_sinks, b_w_out, b_b_out, final_norm, loss_target, m_ffn1_norm, m_ffn1_w_gu, m_ffn1_w_down, m_mix_norm, m_ffn2_norm, m_ffn2_w_gu, m_ffn2_w_down, m_a_w_in, m_a_w_conv, m_a_A_log, m_a_dt_bias, m_a_out_norm, m_a_w_out, m_b_w_in, m_b_b_in, m_b_sinks, m_b_w_out, m_b_b_out, m_final_norm, v_ffn1_norm, v_ffn1_w_gu, v_ffn1_w_down, v_mix_norm, v_ffn2_norm, v_ffn2_w_gu, v_ffn2_w_down, v_a_w_in, v_a_w_conv, v_a_A_log, v_a_dt_bias, v_a_out_norm, v_a_w_out, v_b_w_in, v_b_b_in, v_b_sinks, v_b_w_out, v_b_b_out, v_final_norm):
    w = dict(ffn1_norm=ffn1_norm, ffn1_w_gu=ffn1_w_gu, ffn1_w_down=ffn1_w_down, mix_norm=mix_norm, ffn2_norm=ffn2_norm, ffn2_w_gu=ffn2_w_gu, ffn2_w_down=ffn2_w_down, a_w_in=a_w_in, a_w_conv=a_w_conv, a_A_log=a_A_log, a_dt_bias=a_dt_bias, a_out_norm=a_out_norm, a_w_out=a_w_out, b_w_in=b_w_in, b_b_in=b_b_in, b_sinks=b_sinks, b_w_out=b_w_out, b_b_out=b_b_out, final_norm=final_norm)
    m = dict(ffn1_norm=m_ffn1_norm, ffn1_w_gu=m_ffn1_w_gu, ffn1_w_down=m_ffn1_w_down, mix_norm=m_mix_norm, ffn2_norm=m_ffn2_norm, ffn2_w_gu=m_ffn2_w_gu, ffn2_w_down=m_ffn2_w_down, a_w_in=m_a_w_in, a_w_conv=m_a_w_conv, a_A_log=m_a_A_log, a_dt_bias=m_a_dt_bias, a_out_norm=m_a_out_norm, a_w_out=m_a_w_out, b_w_in=m_b_w_in, b_b_in=m_b_b_in, b_sinks=m_b_sinks, b_w_out=m_b_w_out, b_b_out=m_b_b_out, final_norm=m_final_norm)
    v = dict(ffn1_norm=v_ffn1_norm, ffn1_w_gu=v_ffn1_w_gu, ffn1_w_down=v_ffn1_w_down, mix_norm=v_mix_norm, ffn2_norm=v_ffn2_norm, ffn2_w_gu=v_ffn2_w_gu, ffn2_w_down=v_ffn2_w_down, a_w_in=v_a_w_in, a_w_conv=v_a_w_conv, a_A_log=v_a_A_log, a_dt_bias=v_a_dt_bias, a_out_norm=v_a_out_norm, a_w_out=v_a_w_out, b_w_in=v_b_w_in, b_b_in=v_b_b_in, b_sinks=v_b_sinks, b_w_out=v_b_w_out, b_b_out=v_b_b_out, final_norm=v_final_norm)
    T = x.shape[1]
    x0, tgt = x.reshape(T, D), loss_target.reshape(T, D)

    def cast(t):
        return t.astype(BF16)

    h0, t0 = gather_start([cast(ffn1_w_gu[0])], "gather0_start")
    h0d, t0d = gather_start([cast(ffn1_w_down[0])], "gather0d_start")
    h1, t1 = gather_start([cast(a_w_in[0]), a_w_conv[0], cast(a_w_out[0])], "gather1_start")
    h1f, t1f = gather_start([cast(ffn2_w_gu[0]), cast(ffn2_w_down[0])], "gather1f_start")
    g2 = [cast(ffn1_w_gu[1]), cast(ffn1_w_down[1]), cast(b_w_in[0]), b_b_in, cast(b_w_out[0]), b_b_out,
          cast(ffn2_w_gu[1]), cast(ffn2_w_down[1])]
    h2, t2 = gather_start(g2, "gather2_start")
    started = (t0, t0d, t1, t1f, t2)
    a_log_row = jnp.zeros((1, 128), F32).at[0, HEADS_A:2 * HEADS_A].set(a_A_log[0])
    dt_row = jnp.zeros((1, 128), F32).at[0, HEADS_A:2 * HEADS_A].set(a_dt_bias[0])
    sink_row = jnp.zeros((1, 128), F32).at[0, :b_sinks.shape[1]].set(b_sinks[0])
    a_in_cols = a_w_in.shape[-1] * N_DEV

    def down_blocks(t):
        return t.reshape(N_FB, FB, D)

    wgu, wdn, saved = {}, {}, []
    xn = rmsnorm_bf16(x0, ffn1_norm[0:1], "l0_ffn1_norm", started)
    wgu["ffn1", 0] = push_wait(h0, xn, "gather0_wait", scatter=False)[0]
    gu = ffn_up(xn, wgu["ffn1", 0], "l0_ffn1_up")
    wdn["ffn1", 0] = down_blocks(push_wait(h0d, gu, "gather0d_wait", scatter=False)[0])
    xs, s1 = ffn_down(gu, wdn["ffn1", 0], x0, "l0_ffn1_down"), (x0, xn, gu)
    got = push_wait(h1, xs, "gather1_wait", scatter=False)
    a_in_full = jnp.pad(got[0].transpose(1, 0, 2).reshape(D, a_in_cols), ((0, 0), (0, A_COLS - a_in_cols)))
    gdn_args = (mix_norm[0:1], a_in_full, got[1].transpose(1, 0, 2).reshape(4, 3 * D), a_log_row, dt_row, a_out_norm,
                got[2].reshape(D, D))
    xs, sm = gdn_forward(xs, *gdn_args, "gdn")
    got = push_wait(h1f, xs, "gather1f_wait", scatter=False)
    wgu["ffn2", 0], wdn["ffn2", 0] = got[0], down_blocks(got[1])
    xs, s2 = ffn_forward(xs, ffn2_norm[0:1], wgu["ffn2", 0], wdn["ffn2", 0], "l0_ffn2")
    saved.append((s1, sm, s2))
    got = push_wait(h2, xs, "gather2_wait", scatter=False)
    wgu["ffn1", 1], wdn["ffn1", 1] = got[0], down_blocks(got[1])
    swa_args = (mix_norm[1:2], got[2].transpose(1, 0, 2).reshape(D, B_COLS), got[3].reshape(1, B_COLS), sink_row,
                got[4].reshape(D, D), got[5].reshape(1, D))
    wgu["ffn2", 1], wdn["ffn2", 1] = got[6], down_blocks(got[7])
    xs, s1 = ffn_forward(xs, ffn1_norm[1:2], wgu["ffn1", 1], wdn["ffn1", 1], "l1_ffn1")
    xs, sm = swa_forward(xs, *swa_args, "swa")
    xs, s2 = ffn_forward(xs, ffn2_norm[1:2], wgu["ffn2", 1], wdn["ffn2", 1], "l1_ffn2")
    saved.append((s1, sm, s2))
    loss_row, dx, d_final_norm = final_loss(xs, final_norm.reshape(1, D), tgt, "final_loss")

    def down_slots(t):
        return cast(t.reshape(N_DEV, FB // 2, D))

    def col_slots(t, dtype=BF16):
        return t.reshape(t.shape[0], N_DEV, -1).transpose(1, 0, 2).astype(dtype)

    d_norm = {"ffn1_norm": [None, None], "mix_norm": [None, None], "ffn2_norm": [None, None]}
    s1, sm, s2 = saved[1]
    dx, d_norm["ffn2_norm"][1], d_gu, d_dn = ffn_backward(dx, s2, ffn2_norm[1:2], wgu["ffn2", 1], wdn["ffn2", 1], "l1_ffn2")
    sent1 = [cast(d_gu), down_slots(d_dn)]
    dx, d_norm["mix_norm"][1], d_b_in, d_b_bias_in, d_sinks, d_b_out, d_b_bias_out = swa_backward(dx, sm, *swa_args, "swa")
    sent1 += [col_slots(d_b_in), d_b_bias_in.reshape(N_DEV, 1, -1), cast(d_b_out.reshape(N_DEV, D // N_DEV, D)),
              d_b_bias_out.reshape(N_DEV, 1, -1)]
    dx, d_norm["ffn1_norm"][1], d_gu, d_dn = ffn_backward(dx, s1, ffn1_norm[1:2], wgu["ffn1", 1], wdn["ffn1", 1], "l1_ffn1")
    sent1 += [cast(d_gu), down_slots(d_dn)]
    x1, tx1 = exchange_start(sent1, "exchange1_start")

    s1, sm, s2 = saved[0]
    dx, d_norm["ffn2_norm"][0], d_gu, d_dn = ffn_backward(dx, s2, ffn2_norm[0:1], wgu["ffn2", 0], wdn["ffn2", 0], "l0_ffn2",
                                                           deps=(tx1,))
    sent2 = [cast(d_gu), down_slots(d_dn)]
    dx, d_norm["mix_norm"][0], d_a_in, d_a_conv, d_alog, d_dt, d_onorm, d_a_out = gdn_backward(dx, sm, *gdn_args, "gdn")
    sent2 += [col_slots(d_a_in[:, :a_in_cols]), col_slots(d_a_conv, F32), cast(d_a_out.reshape(N_DEV, D // N_DEV, D))]
    x2, tx2 = exchange_start(sent2, "exchange2_start")
    last = {}

    def send_last(d_gu, d_dn):
        last["handles"], token = exchange_start([cast(d_gu), down_slots(d_dn)], "exchange3_start")
        return (token,)

    dx, d_norm["ffn1_norm"][0], _, _ = ffn_backward(dx, s1, ffn1_norm[0:1], wgu["ffn1", 0], wdn["ffn1", 0], "l0_ffn1",
                                                    deps=(tx2,), on_grads=send_last)
    grad_x = dx.reshape(x.shape)
    r1 = push_wait(x1, dx, "exchange1_wait", scatter=True)
    r2 = push_wait(x2, dx, "exchange2_wait", scatter=True)
    received = dict(ffn2_w_gu=[r2[0], r1[0]], ffn2_w_down=[r2[1], r1[1]],
                    b_w_in=[r1[2]], b_b_in=[r1[3]], b_w_out=[r1[4]], b_b_out=[r1[5]],
                    a_w_in=[r2[2]], a_w_conv=[r2[3]], a_w_out=[r2[4]])

    grads, deltas, new_m, new_v = {}, {}, {}, {}

    def update(key):
        shape = w[key].shape
        cols = shape[-1]
        layers = lambda t: t.reshape(shape[0], -1, cols)
        out = adam_update([r.reshape(N_DEV, -1, cols) for r in received[key]], layers(w[key]), layers(m[key]), layers(v[key]),
                          f"adam_{key}")
        grads[key], deltas[key], new_m[key], new_v[key] = (t.reshape(shape) for t in out)

    for key in SHARDED:
        if key in received:
            update(key)
    done_first = [deltas[key] for key in received]

    small = dict(ffn1_norm=jnp.concatenate(d_norm["ffn1_norm"], axis=0), mix_norm=jnp.concatenate(d_norm["mix_norm"], axis=0),
                 ffn2_norm=jnp.concatenate(d_norm["ffn2_norm"], axis=0), final_norm=d_final_norm,
                 a_A_log=d_alog[0, HEADS_A:2 * HEADS_A], a_dt_bias=d_dt[0, HEADS_A:2 * HEADS_A],
                 b_sinks=d_sinks[0, :b_sinks.shape[1]], a_out_norm=d_onorm, loss=loss_row[0, 0])
    total = all_reduce_small(_pack_small(small), "allreduce_small")
    out = adam_update([total[None]], _pack_small(w)[None], _pack_small(m)[None], _pack_small(v)[None], "adam_small")
    for dst, packed in zip((grads, deltas, new_m, new_v), out):
        dst.update(_unpack_small(packed[0], w))
    loss = total[7, LOSS_LANE]

    r3 = push_wait(last["handles"], done_first + [out[1]], "exchange3_wait", scatter=True)
    received.update(ffn1_w_gu=[r3[0], r1[6]], ffn1_w_down=[r3[1], r1[7]])
    update("ffn1_w_gu")
    update("ffn1_w_down")

    return (loss, grad_x, *[grads[k_] for k_ in WEIGHTS], *[deltas[k_] for k_ in WEIGHTS],
            *[new_m[k_] for k_ in WEIGHTS], *[new_v[k_] for k_ in WEIGHTS])
```

```python
import functools

import jax
import jax.numpy as jnp
from jax import lax
from jax.experimental import pallas as pl
from jax.experimental.pallas import tpu as pltpu

F32, BF16 = jnp.float32, jnp.bfloat16
HI = lax.Precision.HIGHEST
EPS = 1e-6

N_DEV = 8
D = 1024
FB = 704
N_FB = 4
HEADS_A, DK = 8, 128
CHUNK = 64
PREP_T = 512
A_COLS = 4224
B_HD, B_BLK = 64, 128
VMEM_LIMIT_V7X = 60 * 1024 * 1024

ADAM_LR, ADAM_B1, ADAM_B2, ADAM_EPS, ADAM_WD, ADAM_STEP = 0.001, 0.9, 0.999, 1e-08, 0.01, 10

NT = (((1,), (1,)), ((), ()))
TN = (((0,), (0,)), ((), ()))


def _pc(body, *, name, out_shape, grid=(), in_specs=None, out_specs=None, scratch=(), sem=None, **kw):
    params = pltpu.CompilerParams(dimension_semantics=sem, vmem_limit_bytes=VMEM_LIMIT_V7X)
    return pl.pallas_call(body, name=name, out_shape=out_shape, grid=grid, in_specs=in_specs, out_specs=out_specs,
                          scratch_shapes=list(scratch), compiler_params=params, **kw)


def _sds(shape, dtype):
    return jax.ShapeDtypeStruct(tuple(shape), dtype)


def _dot(a, b, dims=None, precision=None):
    if dims is None:
        return jnp.dot(a, b, preferred_element_type=F32, precision=precision)
    return lax.dot_general(a, b, dims, preferred_element_type=F32, precision=precision)


def _sigmoid(x):
    return 1.0 / (1.0 + jnp.exp(-x))


def _softplus(x):
    return jnp.maximum(x, 0.0) + jnp.log(1.0 + jnp.exp(-jnp.abs(x)))


def _rms_fwd(x, w):
    r = lax.rsqrt(jnp.mean(x * x, axis=-1, keepdims=True) + EPS)
    return x * r * w


def _rms_bwd(x, w, dy):
    r = lax.rsqrt(jnp.mean(x * x, axis=-1, keepdims=True) + EPS)
    xh = x * r
    dxh = dy * w
    dx = r * (dxh - xh * jnp.mean(dxh * xh, axis=-1, keepdims=True))
    return dx, jnp.sum(dy * xh, axis=0, keepdims=True)


def _tile(n, want):
    t = min(n, want)
    assert n % t == 0, (n, want)
    return t


def rmsnorm_bf16(x, w, name, deps=()):
    T = x.shape[0]
    tm = _tile(T, 1024)

    def body(x_ref, w_ref, *rest):
        rest[-1][...] = _rms_fwd(x_ref[...], w_ref[...]).astype(BF16)

    return _pc(body, name=name, out_shape=_sds((T, D), BF16), grid=(T // tm,),
               in_specs=[pl.BlockSpec((tm, D), lambda i: (i, 0)), pl.BlockSpec((1, D), lambda i: (0, 0))] + [DEP_SPEC] * len(deps),
               out_specs=pl.BlockSpec((tm, D), lambda i: (i, 0)), sem=("parallel",))(x, w, *deps)


def rmsnorm_bwd_add(x, w, dxn, dres, name):
    T = x.shape[0]
    tm = _tile(T, 512)

    def body(x_ref, w_ref, dxn_ref, dres_ref, dx_ref, dw_ref):
        dx, dw = _rms_bwd(x_ref[...], w_ref[...], dxn_ref[...])
        dx_ref[...] = dres_ref[...] + dx

        @pl.when(pl.program_id(0) == 0)
        def _():
            dw_ref[...] = jnp.zeros_like(dw_ref)
        dw_ref[...] += dw

    row = pl.BlockSpec((tm, D), lambda i: (i, 0))
    vec = pl.BlockSpec((1, D), lambda i: (0, 0))
    return _pc(body, name=name, out_shape=(_sds((T, D), F32), _sds((1, D), F32)), grid=(T // tm,),
               in_specs=[row, vec, row, row], out_specs=(row, vec), sem=("arbitrary",))(x, w, dxn, dres)


def final_loss(x, w, tgt, name):
    T = x.shape[0]
    tm = _tile(T, 512)

    def body(x_ref, w_ref, t_ref, loss_ref, dx_ref, dw_ref):
        xv, wv = x_ref[...], w_ref[...]
        err = _rms_fwd(xv, wv) - t_ref[...]
        dx, dw = _rms_bwd(xv, wv, err * (1.0 / D))
        dx_ref[...] = dx

        @pl.when(pl.program_id(0) == 0)
        def _():
            dw_ref[...] = jnp.zeros_like(dw_ref)
            loss_ref[...] = jnp.zeros_like(loss_ref)
        dw_ref[...] += dw
        loss_ref[...] += jnp.full((1, 128), 0.5 / D, F32) * jnp.sum(err * err)

    row = pl.BlockSpec((tm, D), lambda i: (i, 0))
    vec = pl.BlockSpec((1, D), lambda i: (0, 0))
    return _pc(body, name=name, out_shape=(_sds((1, 128), F32), _sds((T, D), F32), _sds((1, D), F32)),
               grid=(T // tm,), in_specs=[row, vec, row],
               out_specs=(pl.BlockSpec((1, 128), lambda i: (0, 0)), row, vec), sem=("arbitrary",))(x, w, tgt)


def _col_tile(n):
    for t in (1536, 1408, 1024, 768, 512, 384, 256, 128):
        if n % t == 0:
            return t
    return n


def mm_nn(a, b, name, bias=None, residual=None, out_dtype=F32):
    T, K = a.shape
    N = b.shape[1]
    tm, tn = _tile(T, 512), _col_tile(N)

    def body(a_ref, b_ref, *rest):
        o_ref = rest[-1]
        acc = _dot(a_ref[...].astype(BF16), b_ref[...])
        for extra in rest[:-1]:
            acc = acc + extra[...]
        o_ref[...] = acc.astype(out_dtype)

    in_specs = [pl.BlockSpec((tm, K), lambda j, i: (i, 0)), pl.BlockSpec((K, tn), lambda j, i: (0, j))]
    args = [a, b]
    if bias is not None:
        in_specs.append(pl.BlockSpec((1, tn), lambda j, i: (0, j)))
        args.append(bias)
    if residual is not None:
        in_specs.append(pl.BlockSpec((tm, tn), lambda j, i: (i, j)))
        args.append(residual)
    return _pc(body, name=name, out_shape=_sds((T, N), out_dtype), grid=(N // tn, T // tm), in_specs=in_specs,
               out_specs=pl.BlockSpec((tm, tn), lambda j, i: (i, j)), sem=("parallel", "parallel"))(*args)


def mm_nt(a, b, name, out_dtype=F32):
    T, N = a.shape
    K = b.shape[0]
    tm = _tile(T, 512)

    def body(a_ref, b_ref, o_ref):
        o_ref[...] = _dot(a_ref[...].astype(BF16), b_ref[...], NT).astype(out_dtype)

    return _pc(body, name=name, out_shape=_sds((T, K), out_dtype), grid=(T // tm,),
               in_specs=[pl.BlockSpec((tm, N), lambda i: (i, 0)), pl.BlockSpec((K, N), lambda i: (0, 0))],
               out_specs=pl.BlockSpec((tm, K), lambda i: (i, 0)), sem=("parallel",))(a, b)


def mm_tn(a, b, name):
    T, K = a.shape
    N = b.shape[1]
    tt, tn = _tile(T, 1024), _col_tile(N)

    def body(a_ref, b_ref, o_ref):
        @pl.when(pl.program_id(1) == 0)
        def _():
            o_ref[...] = jnp.zeros_like(o_ref)
        o_ref[...] += _dot(a_ref[...].astype(BF16), b_ref[...].astype(BF16), TN)

    return _pc(body, name=name, out_shape=_sds((K, N), F32), grid=(N // tn, T // tt),
               in_specs=[pl.BlockSpec((tt, K), lambda j, t: (t, 0)), pl.BlockSpec((tt, tn), lambda j, t: (t, j))],
               out_specs=pl.BlockSpec((K, tn), lambda j, t: (0, j)), sem=("parallel", "arbitrary"))(a, b)


def ffn_up(xn, wgu, name, deps=()):
    T = xn.shape[0]
    tm = _tile(T, 512)

    def body(x_ref, w_ref, *rest):
        xv = x_ref[...]
        for j in range(2 * N_FB):
            rest[-1][j] = _dot(xv, w_ref[j]).astype(BF16)

    return _pc(body, name=name, out_shape=_sds((2 * N_FB, T, FB), BF16), grid=(T // tm,),
               in_specs=[pl.BlockSpec((tm, D), lambda i: (i, 0)), pl.BlockSpec((2 * N_FB, D, FB), lambda i: (0, 0, 0))]
               + [DEP_SPEC] * len(deps),
               out_specs=pl.BlockSpec((2 * N_FB, tm, FB), lambda i: (0, i, 0)), sem=("parallel",))(xn, wgu, *deps)


def ffn_down(gu, wd, x, name):
    T = x.shape[0]
    tm = _tile(T, 512)

    def body(gu_ref, w_ref, x_ref, o_ref):
        acc = jnp.zeros((tm, D), F32)
        for g in range(N_FB):
            gate, up = gu_ref[g], gu_ref[N_FB + g]
            acc = acc + _dot(gate * _sigmoid(gate) * up, w_ref[g])
        o_ref[...] = x_ref[...] + 0.5 * acc

    row = pl.BlockSpec((tm, D), lambda i: (i, 0))
    return _pc(body, name=name, out_shape=_sds((T, D), F32), grid=(T // tm,),
               in_specs=[pl.BlockSpec((2 * N_FB, tm, FB), lambda i: (0, i, 0)),
                         pl.BlockSpec((N_FB, FB, D), lambda i: (0, 0, 0)), row],
               out_specs=row, sem=("parallel",))(gu, wd, x)


def ffn_bwd_hidden(dout, wd, gu, name, deps=()):
    T = dout.shape[0]
    tm = _tile(T, 512)

    def body(d_ref, w_ref, gu_ref, *rest):
        dgu_ref, act_ref = rest[-2:]
        dy = (0.5 * d_ref[...]).astype(BF16)
        for g in range(N_FB):
            gate, up = gu_ref[g], gu_ref[N_FB + g]
            sg = _sigmoid(gate)
            silu = gate * sg
            dact = _dot(dy, w_ref[g], NT).astype(BF16)
            act_ref[g] = silu * up
            dgu_ref[g] = dact * up * (sg * (1.0 + gate * (1.0 - sg)))
            dgu_ref[N_FB + g] = dact * silu

    return _pc(body, name=name, out_shape=(_sds((2 * N_FB, T, FB), BF16), _sds((N_FB, T, FB), BF16)), grid=(T // tm,),
               in_specs=[pl.BlockSpec((tm, D), lambda i: (i, 0)), pl.BlockSpec((N_FB, FB, D), lambda i: (0, 0, 0)),
                         pl.BlockSpec((2 * N_FB, tm, FB), lambda i: (0, i, 0))] + [DEP_SPEC] * len(deps),
               out_specs=(pl.BlockSpec((2 * N_FB, tm, FB), lambda i: (0, i, 0)),
                          pl.BlockSpec((N_FB, tm, FB), lambda i: (0, i, 0))), sem=("parallel",))(dout, wd, gu, *deps)


def ffn_bwd_input(dgu, wgu, x, dout, nw, name, deps=()):
    T = x.shape[0]
    tm = _tile(T, 512)

    def body(dgu_ref, w_ref, x_ref, d_ref, nw_ref, *rest):
        dx_ref, dnw_ref = rest[-2:]
        dxn = jnp.zeros((tm, D), F32)
        for j in range(2 * N_FB):
            dxn = dxn + _dot(dgu_ref[j], w_ref[j], NT)
        dx, dw = _rms_bwd(x_ref[...], nw_ref[...], dxn)
        dx_ref[...] = d_ref[...] + dx

        @pl.when(pl.program_id(0) == 0)
        def _():
            dnw_ref[...] = jnp.zeros_like(dnw_ref)
        dnw_ref[...] += dw

    row = pl.BlockSpec((tm, D), lambda i: (i, 0))
    vec = pl.BlockSpec((1, D), lambda i: (0, 0))
    return _pc(body, name=name, out_shape=(_sds((T, D), F32), _sds((1, D), F32)), grid=(T // tm,),
               in_specs=[pl.BlockSpec((2 * N_FB, tm, FB), lambda i: (0, i, 0)),
                         pl.BlockSpec((2 * N_FB, D, FB), lambda i: (0, 0, 0)), row, row, vec] + [DEP_SPEC] * len(deps),
               out_specs=(row, vec), sem=("arbitrary",))(dgu, wgu, x, dout, nw, *deps)


def ffn_wgrad_gu(xn, dgu, name):
    T = xn.shape[0]
    tt = _tile(T, 1024)

    def body(x_ref, d_ref, o_ref):
        @pl.when(pl.program_id(1) == 0)
        def _():
            o_ref[...] = jnp.zeros_like(o_ref)
        o_ref[0] += _dot(x_ref[...], d_ref[0], TN)

    return _pc(body, name=name, out_shape=_sds((2 * N_FB, D, FB), F32), grid=(2 * N_FB, T // tt),
               in_specs=[pl.BlockSpec((tt, D), lambda j, t: (t, 0)), pl.BlockSpec((1, tt, FB), lambda j, t: (j, t, 0))],
               out_specs=pl.BlockSpec((1, D, FB), lambda j, t: (j, 0, 0)), sem=("parallel", "arbitrary"))(xn, dgu)


def ffn_wgrad_down(act, dout, name):
    T = dout.shape[0]
    tt = _tile(T, 1024)

    def body(a_ref, d_ref, o_ref):
        @pl.when(pl.program_id(1) == 0)
        def _():
            o_ref[...] = jnp.zeros_like(o_ref)
        o_ref[0] += _dot(a_ref[0], (0.5 * d_ref[...]).astype(BF16), TN)

    return _pc(body, name=name, out_shape=_sds((N_FB, FB, D), F32), grid=(N_FB, T // tt),
               in_specs=[pl.BlockSpec((1, tt, FB), lambda g, t: (g, t, 0)), pl.BlockSpec((tt, D), lambda g, t: (t, 0))],
               out_specs=pl.BlockSpec((1, FB, D), lambda g, t: (g, 0, 0)), sem=("parallel", "arbitrary"))(act, dout)


def ffn_forward(x, nw, wgu, wd, tag, deps=()):
    xn = rmsnorm_bf16(x, nw, f"{tag}_norm", deps)
    gu = ffn_up(xn, wgu, f"{tag}_up")
    return ffn_down(gu, wd, x, f"{tag}_down"), (x, xn, gu)


def ffn_backward(dout, saved, nw, wgu, wd, tag, deps=(), on_grads=None):
    x, xn, gu = saved
    dgu, act = ffn_bwd_hidden(dout, wd, gu, f"{tag}_bwd_hidden", deps)
    dwd = ffn_wgrad_down(act, dout, f"{tag}_wgrad_down")
    dwgu = ffn_wgrad_gu(xn, dgu, f"{tag}_wgrad_gu")
    late = on_grads(dwgu, dwd) if on_grads else ()
    dx, dnw = ffn_bwd_input(dgu, wgu, x, dout, nw, f"{tag}_bwd_input", late)
    return dx, dnw, dwgu, dwd


N_QKV_BLK = 3 * HEADS_A
Z_BLK0 = N_QKV_BLK
BA_BLK = A_COLS // 128 - 1


def _conv_taps(xcat, w):
    c = xcat[8:] * w[3:4]
    for k in range(3):
        c = c + pltpu.roll(xcat, 3 - k, 0)[8:] * w[k:k + 1]
    return c


def _head_cols(h):
    return slice(128 * h, 128 * (h + 1))


def gdn_conv_fwd(proj, wconv, name):
    T = proj.shape[0]
    tm = _tile(T, 512)

    def body(cur_ref, prev_ref, w_ref, c_ref, y_ref):
        kind, t = pl.program_id(0), pl.program_id(1)
        prev = jnp.where(t > 0, prev_ref[...], 0.0)
        c = _conv_taps(jnp.concatenate([prev, cur_ref[...]], axis=0), w_ref[...])
        c_ref[...] = c
        s = c * _sigmoid(c)
        scale = jnp.where(kind == 0, DK ** -0.5, 1.0)
        for h in range(HEADS_A):
            sh = s[:, _head_cols(h)]
            r = lax.rsqrt(jnp.sum(sh * sh, axis=-1, keepdims=True) + EPS)
            y_ref[h] = sh * jnp.where(kind < 2, r * scale, 1.0)

    return _pc(body, name=name, out_shape=(_sds((T, 3 * D), F32), _sds((N_QKV_BLK, T, 128), F32)),
               grid=(3, T // tm),
               in_specs=[pl.BlockSpec((tm, D), lambda kd, t: (t, kd)),
                         pl.BlockSpec((8, D), lambda kd, t: (jnp.maximum(t * (tm // 8) - 1, 0), kd)),
                         pl.BlockSpec((4, D), lambda kd, t: (0, kd))],
               out_specs=(pl.BlockSpec((tm, D), lambda kd, t: (t, kd)),
                          pl.BlockSpec((HEADS_A, tm, 128), lambda kd, t: (kd, t, 0))),
               sem=("parallel", "parallel"))(proj, proj, wconv)


def gdn_conv_bwd(dqkv, c, proj, wconv, name):
    T = c.shape[0]
    tm = _tile(T, 512)
    n_t = T // tm

    def body(dy_ref, dyn_ref, c_ref, cn_ref, x_ref, xp_ref, w_ref, dx_ref, dw_ref):
        kind, t = pl.program_id(0), pl.program_id(1)
        scale = jnp.where(kind == 0, DK ** -0.5, 1.0)

        def act_bwd(dy, cv):
            sg = _sigmoid(cv)
            s = cv * sg
            parts = []
            for h in range(HEADS_A):
                sh, dyh = s[:, _head_cols(h)], dy[h]
                r = lax.rsqrt(jnp.sum(sh * sh, axis=-1, keepdims=True) + EPS)
                ds_norm = scale * r * (dyh - (r * r) * sh * jnp.sum(dyh * sh, axis=-1, keepdims=True))
                parts.append(jnp.where(kind < 2, ds_norm, dyh))
            return jnp.concatenate(parts, axis=1) * (sg * (1.0 + cv * (1.0 - sg)))

        w = w_ref[...]
        dcur = act_bwd(dy_ref[...], c_ref[...])
        dnext = jnp.where(t < n_t - 1, act_bwd(dyn_ref[...], cn_ref[...]), 0.0)
        dcat = jnp.concatenate([dcur, dnext], axis=0)
        dx = dcur * w[3:4]
        for k in range(3):
            dx = dx + pltpu.roll(dcat, tm + 8 - (3 - k), 0)[:tm] * w[k:k + 1]
        dx_ref[...] = dx.astype(BF16)
        xprev = jnp.where(t > 0, xp_ref[...], 0.0)
        xcat = jnp.concatenate([xprev, x_ref[...]], axis=0)
        rows = [jnp.sum(dcur * pltpu.roll(xcat, 3 - k, 0)[8:], axis=0, keepdims=True) for k in range(3)]
        rows.append(jnp.sum(dcur * xcat[8:], axis=0, keepdims=True))

        @pl.when(t == 0)
        def _():
            dw_ref[...] = jnp.zeros_like(dw_ref)
        dw_ref[...] += jnp.concatenate(rows, axis=0)

    def nxt(t):
        return jnp.minimum((t + 1) * (tm // 8), T // 8 - 1)

    cur = pl.BlockSpec((tm, D), lambda kd, t: (t, kd))
    return _pc(body, name=name, out_shape=(_sds((T, 3 * D), BF16), _sds((4, 3 * D), F32)), grid=(3, n_t),
               in_specs=[pl.BlockSpec((HEADS_A, tm, 128), lambda kd, t: (kd, t, 0)),
                         pl.BlockSpec((HEADS_A, 8, 128), lambda kd, t: (kd, nxt(t), 0)),
                         cur, pl.BlockSpec((8, D), lambda kd, t: (nxt(t), kd)),
                         cur, pl.BlockSpec((8, D), lambda kd, t: (jnp.maximum(t * (tm // 8) - 1, 0), kd)),
                         pl.BlockSpec((4, D), lambda kd, t: (0, kd))],
               out_specs=(cur, pl.BlockSpec((4, D), lambda kd, t: (0, kd))),
               sem=("parallel", "arbitrary"))(dqkv, dqkv, c, c, proj, proj, wconv)


def _chunk_masks(n):
    ri = lax.broadcasted_iota(jnp.int32, (n, n), 0)
    ci = lax.broadcasted_iota(jnp.int32, (n, n), 1)
    same = (ri // CHUNK) == (ci // CHUNK)
    return same & (ri >= ci), same & (ri <= ci)


def gdn_gate_fwd(proj, al, dtb, name):
    T = proj.shape[0]
    tg = _tile(T, PREP_T)

    def body(ba_ref, al_ref, dtb_ref, o_ref):
        x = ba_ref[...]
        lane = lax.broadcasted_iota(jnp.int32, x.shape, 1)
        is_a = (lane >= HEADS_A) & (lane < 2 * HEADS_A)
        g = jnp.where(is_a, -jnp.exp(al_ref[...]) * _softplus(x + dtb_ref[...]), 0.0)
        lower, _ = _chunk_masks(tg)
        gc = _dot(lower.astype(F32), g, precision=HI)
        o_ref[...] = jnp.where(lane < HEADS_A, _sigmoid(x), gc)

    vec = pl.BlockSpec((1, 128), lambda i: (0, 0))
    return _pc(body, name=name, out_shape=_sds((T, 128), F32), grid=(T // tg,),
               in_specs=[pl.BlockSpec((tg, 128), lambda i: (i, BA_BLK)), vec, vec],
               out_specs=pl.BlockSpec((tg, 128), lambda i: (i, 0)), sem=("parallel",))(proj, al, dtb)


def gdn_gate_bwd(proj, al, dtb, dgb, name):
    T = proj.shape[0]
    tg = _tile(T, PREP_T)

    def body(ba_ref, al_ref, dtb_ref, dgb_ref, dba_ref, dal_ref, ddt_ref):
        x, d = ba_ref[...], dgb_ref[...]
        lane = lax.broadcasted_iota(jnp.int32, x.shape, 1)
        is_b = lane < HEADS_A
        is_a = (lane >= HEADS_A) & (lane < 2 * HEADS_A)
        beta = _sigmoid(x)
        e_a = jnp.exp(al_ref[...])
        z = x + dtb_ref[...]
        g = jnp.where(is_a, -e_a * _softplus(z), 0.0)
        _, upper = _chunk_masks(tg)
        dg = _dot(upper.astype(F32), jnp.where(is_a, d, 0.0), precision=HI)
        da = jnp.where(is_a, dg * (-e_a) * _sigmoid(z), 0.0)
        db = jnp.where(is_b, d * beta * (1.0 - beta), 0.0)
        dba_ref[...] = (da + db).astype(BF16)

        @pl.when(pl.program_id(0) == 0)
        def _():
            dal_ref[...] = jnp.zeros_like(dal_ref)
            ddt_ref[...] = jnp.zeros_like(ddt_ref)
        dal_ref[...] += jnp.sum(dg * g, axis=0, keepdims=True)
        ddt_ref[...] += jnp.sum(da, axis=0, keepdims=True)

    vec = pl.BlockSpec((1, 128), lambda i: (0, 0))
    blk = pl.BlockSpec((tg, 128), lambda i: (i, 0))
    return _pc(body, name=name, out_shape=(_sds((T, 128), BF16), _sds((1, 128), F32), _sds((1, 128), F32)),
               grid=(T // tg,), in_specs=[pl.BlockSpec((tg, 128), lambda i: (i, BA_BLK)), vec, vec, blk],
               out_specs=(blk, vec, vec), sem=("arbitrary",))(proj, al, dtb, dgb)


def _bmm(a, b, dims, precision=None):
    return lax.dot_general(a, b, dims, preferred_element_type=F32, precision=precision)


B_NN = (((2,), (1,)), ((0,), (0,)))
B_NT = (((2,), (2,)), ((0,), (0,)))


def _select_lane(x, lane_index):
    lane = lax.broadcasted_iota(jnp.int32, x.shape, x.ndim - 1)
    return jnp.sum(jnp.where(lane == lane_index, x, 0.0), axis=-1, keepdims=True)


B_TN = (((1,), (1,)), ((0,), (0,)))


def _bmm_split(a, b, dims):
    ah, bh = a.astype(BF16), b.astype(BF16)
    al, bl = (a - ah.astype(F32)).astype(BF16), (b - bh.astype(F32)).astype(BF16)
    return _bmm(ah, bh, dims) + (_bmm(ah, bl, dims) + _bmm(al, bh, dims))


@jax.custom_vjp
def _bmm_f32(a, b):
    return _bmm_split(a, b, B_NN)


def _bmm_f32_fwd(a, b):
    return _bmm_split(a, b, B_NN), (a, b)


def _bmm_f32_bwd(res, dc):
    a, b = res
    return _bmm_split(dc, b, B_NT), _bmm_split(a, dc, B_TN)


_bmm_f32.defvjp(_bmm_f32_fwd, _bmm_f32_bwd)


def _tri_inverse(lmat):
    ri = lax.broadcasted_iota(jnp.int32, lmat.shape, 1)
    ci = lax.broadcasted_iota(jnp.int32, lmat.shape, 2)
    inv = jnp.where(ri == ci, 1.0, 0.0) - lmat
    power = lmat
    for _ in range(5):
        power = _bmm_split(power, power, B_NN)
        inv = inv + _bmm_split(inv, power, B_NN)
    return inv


def _stored_inverse(x):
    @jax.custom_vjp
    def inverse(lmat):
        return x

    def fwd(lmat):
        return x, None

    def bwd(_, dx):
        return (-_bmm_split(_bmm_split(x, dx, B_TN), x, B_NT),)

    inverse.defvjp(fwd, bwd)
    return inverse


def _gdn_prep(q, k, v, gb, h, inverse):
    nb = q.shape[0]
    beta = _select_lane(gb, h)
    gc = _select_lane(gb, HEADS_A + h)
    ri = lax.broadcasted_iota(jnp.int32, (nb, CHUNK, CHUNK), 1)
    ci = lax.broadcasted_iota(jnp.int32, (nb, CHUNK, CHUNK), 2)
    lower, strict, eye = ri >= ci, ri > ci, ri == ci
    gcol = jnp.broadcast_to(gc, (nb, CHUNK, CHUNK))
    grow = _bmm_f32(jnp.ones((nb, CHUNK, CHUNK), F32), jnp.where(eye, gcol, 0.0))
    decay = jnp.where(lower, jnp.exp(jnp.where(lower, gcol - grow, 0.0)), 0.0)
    kb = k * beta
    kbf = k.astype(BF16)
    inv = inverse(jnp.where(strict, _bmm(kb.astype(BF16), kbf, B_NT) * decay, 0.0))
    eg = jnp.exp(gc)
    sol = _bmm_f32(inv, jnp.concatenate([v * beta, kb * eg], axis=-1))
    aqk = _bmm(q.astype(BF16), kbf, B_NT) * decay
    g_last = gc[:, CHUNK - 1:CHUNK, :]
    gl = jnp.broadcast_to(jnp.exp(g_last), (nb, 1, 128))
    return (sol[..., :DK], sol[..., DK:], q * eg, k * jnp.exp(g_last - gc), aqk, gl), inv


def gdn_prep_fwd(qkv, gb, name):
    T = qkv.shape[1]
    tp = _tile(T, PREP_T)
    nb = tp // CHUNK

    def body(q_ref, k_ref, v_ref, gb_ref, u_ref, w_ref, qd_ref, kd_ref, a_ref, gl_ref, inv_ref):
        h = pl.program_id(1)
        shp = (nb, CHUNK, 128)
        (u, w, qd, kd, aqk, gl), inv = _gdn_prep(q_ref[0].reshape(shp), k_ref[0].reshape(shp), v_ref[0].reshape(shp),
                                                 gb_ref[...].reshape(shp), h, _tri_inverse)
        u_ref[0] = u.reshape(tp, 128)
        w_ref[0] = w.reshape(tp, 128)
        qd_ref[0] = qd.reshape(tp, 128)
        kd_ref[0] = kd.reshape(tp, 128)
        a_ref[0] = aqk.reshape(tp, CHUNK)
        gl_ref[0] = gl.reshape(nb, 1, 128)
        inv_ref[0] = inv.reshape(tp, CHUNK)

    def head(off):
        return pl.BlockSpec((1, tp, 128), lambda n, h: (h + off, n, 0))

    per_head = _sds((HEADS_A, T, 128), F32)
    narrow = pl.BlockSpec((1, tp, CHUNK), lambda n, h: (h, n, 0))
    return _pc(body, name=name,
               out_shape=(per_head, per_head, per_head, per_head, _sds((HEADS_A, T, CHUNK), F32),
                          _sds((HEADS_A, T // CHUNK, 1, 128), F32), _sds((HEADS_A, T, CHUNK), F32)),
               grid=(T // tp, HEADS_A),
               in_specs=[head(0), head(HEADS_A), head(2 * HEADS_A), pl.BlockSpec((tp, 128), lambda n, h: (n, 0))],
               out_specs=(head(0), head(0), head(0), head(0), narrow,
                          pl.BlockSpec((1, nb, 1, 128), lambda n, h: (h, n, 0, 0)), narrow),
               sem=("parallel", "parallel"))(qkv, qkv, qkv, gb)


def gdn_prep_bwd(qkv, gb, inv, du, dw, dqd, dkd, da, dgl, name):
    T = qkv.shape[1]
    tp = _tile(T, PREP_T)
    nb = tp // CHUNK

    def body(q_ref, k_ref, v_ref, gb_ref, inv_ref, du_ref, dw_ref, dqd_ref, dkd_ref, da_ref, dgl_ref, dqkv_ref, dgb_ref):
        h = pl.program_id(1)
        shp = (nb, CHUNK, 128)
        stored = _stored_inverse(inv_ref[0].reshape(nb, CHUNK, CHUNK))
        _, vjp = jax.vjp(lambda q, k, v, gb: _gdn_prep(q, k, v, gb, h, stored)[0], q_ref[0].reshape(shp),
                         k_ref[0].reshape(shp), v_ref[0].reshape(shp), gb_ref[...].reshape(shp))
        dq, dk, dv, dgb = vjp((du_ref[0].reshape(shp), dw_ref[0].reshape(shp), dqd_ref[0].reshape(shp),
                               dkd_ref[0].reshape(shp), da_ref[0].reshape(nb, CHUNK, CHUNK), dgl_ref[0].reshape(nb, 1, 128)))
        dqkv_ref[h] = dq.reshape(tp, 128)
        dqkv_ref[HEADS_A + h] = dk.reshape(tp, 128)
        dqkv_ref[2 * HEADS_A + h] = dv.reshape(tp, 128)

        @pl.when(h == 0)
        def _():
            dgb_ref[...] = jnp.zeros_like(dgb_ref)
        dgb_ref[...] += dgb.reshape(tp, 128)

    def head(off):
        return pl.BlockSpec((1, tp, 128), lambda n, h: (h + off, n, 0))

    narrow = pl.BlockSpec((1, tp, CHUNK), lambda n, h: (h, n, 0))
    return _pc(body, name=name, out_shape=(_sds((N_QKV_BLK, T, 128), F32), _sds((T, 128), F32)),
               grid=(T // tp, HEADS_A),
               in_specs=[head(0), head(HEADS_A), head(2 * HEADS_A), pl.BlockSpec((tp, 128), lambda n, h: (n, 0)), narrow,
                         head(0), head(0), head(0), head(0), narrow,
                         pl.BlockSpec((1, nb, 1, 128), lambda n, h: (h, n, 0, 0))],
               out_specs=(pl.BlockSpec((N_QKV_BLK, tp, 128), lambda n, h: (0, n, 0)),
                          pl.BlockSpec((tp, 128), lambda n, h: (n, 0))),
               sem=("parallel", "arbitrary"))(qkv, qkv, qkv, gb, inv, du, dw, dqd, dkd, da, dgl)


def gdn_scan_fwd(u, w, qd, kd, aqk, gl, name):
    T = u.shape[1]
    n_chunks = T // CHUNK

    def body(u_ref, w_ref, qd_ref, kd_ref, a_ref, gl_ref, o_ref, sin_ref, state):
        @pl.when(pl.program_id(0) == 0)
        def _():
            state[...] = jnp.zeros_like(state)
        for h in range(HEADS_A):
            s = state[h]
            sin_ref[0, h] = s
            sb = s.astype(BF16)
            both = _dot(jnp.concatenate([w_ref[h], qd_ref[h]], axis=0).astype(BF16), sb)
            vn = (u_ref[h] - both[:CHUNK]).astype(BF16)
            o_ref[h] = both[CHUNK:] + _dot(a_ref[h].astype(BF16), vn)
            state[h] = s * gl_ref[h, 0] + _dot(kd_ref[h].astype(BF16), vn, TN)

    blk = pl.BlockSpec((HEADS_A, CHUNK, 128), lambda n: (0, n, 0))
    return _pc(body, name=name,
               out_shape=(_sds((HEADS_A, T, 128), F32), _sds((n_chunks, HEADS_A, DK, 128), F32)), grid=(n_chunks,),
               in_specs=[blk, blk, blk, blk, pl.BlockSpec((HEADS_A, CHUNK, CHUNK), lambda n: (0, n, 0)),
                         pl.BlockSpec((HEADS_A, 1, 1, 128), lambda n: (0, n, 0, 0))],
               out_specs=(blk, pl.BlockSpec((1, HEADS_A, DK, 128), lambda n: (n, 0, 0, 0))),
               scratch=[pltpu.VMEM((HEADS_A, DK, 128), F32)], sem=("arbitrary",))(u, w, qd, kd, aqk, gl)


def gdn_scan_bwd(u, w, qd, kd, aqk, gl, sin, do, name):
    T = u.shape[1]
    n_chunks = T // CHUNK

    def body(u_ref, w_ref, qd_ref, kd_ref, a_ref, gl_ref, sin_ref, do_ref,
             du_ref, dw_ref, dqd_ref, dkd_ref, da_ref, dgl_ref, dstate):
        @pl.when(pl.program_id(0) == 0)
        def _():
            dstate[...] = jnp.zeros_like(dstate)
        lane0 = lax.broadcasted_iota(jnp.int32, (1, 128), 1) == 0
        for h in range(HEADS_A):
            s = sin_ref[0, h]
            sb = s.astype(BF16)
            wb, qdb, kdb = w_ref[h].astype(BF16), qd_ref[h].astype(BF16), kd_ref[h].astype(BF16)
            ab, dob = a_ref[h].astype(BF16), do_ref[h].astype(BF16)
            vn = (u_ref[h] - _dot(wb, sb)).astype(BF16)
            ds_out = dstate[h]
            dsb = ds_out.astype(BF16)
            dqd_ref[h] = _dot(dob, sb, NT)
            da_ref[h] = _dot(dob, vn, NT)
            dv = _dot(ab, dob, TN) + _dot(kdb, dsb)
            dkd_ref[h] = _dot(vn, dsb, NT)
            dgl_ref[h, 0] = jnp.where(lane0, jnp.sum(ds_out * s), 0.0)
            du_ref[h] = dv
            dvb = dv.astype(BF16)
            dw_ref[h] = -_dot(dvb, sb, NT)
            dstate[h] = ds_out * gl_ref[h, 0] + _dot(qdb, dob, TN) - _dot(wb, dvb, TN)

    last = n_chunks - 1
    blk = pl.BlockSpec((HEADS_A, CHUNK, 128), lambda n: (0, last - n, 0))
    ablk = pl.BlockSpec((HEADS_A, CHUNK, CHUNK), lambda n: (0, last - n, 0))
    glblk = pl.BlockSpec((HEADS_A, 1, 1, 128), lambda n: (0, last - n, 0, 0))
    per_head = _sds((HEADS_A, T, 128), F32)
    return _pc(body, name=name,
               out_shape=(per_head, per_head, per_head, per_head, _sds((HEADS_A, T, CHUNK), F32),
                          _sds((HEADS_A, n_chunks, 1, 128), F32)), grid=(n_chunks,),
               in_specs=[blk, blk, blk, blk, ablk, glblk,
                         pl.BlockSpec((1, HEADS_A, DK, 128), lambda n: (last - n, 0, 0, 0)), blk],
               out_specs=(blk, blk, blk, blk, ablk, glblk),
               scratch=[pltpu.VMEM((HEADS_A, DK, 128), F32)], sem=("arbitrary",))(u, w, qd, kd, aqk, gl, sin, do)


def gdn_outnorm_fwd(o, proj, wn, name):
    T = o.shape[1]
    tm = _tile(T, 512)

    def body(o_ref, z_ref, wn_ref, y_ref):
        for h in range(HEADS_A):
            z = z_ref[:, 128 * h:128 * (h + 1)]
            y_ref[:, 128 * h:128 * (h + 1)] = (_rms_fwd(o_ref[h], wn_ref[...]) * (z * _sigmoid(z))).astype(BF16)

    return _pc(body, name=name, out_shape=_sds((T, D), BF16), grid=(T // tm,),
               in_specs=[pl.BlockSpec((HEADS_A, tm, 128), lambda i: (0, i, 0)),
                         pl.BlockSpec((tm, D), lambda i: (i, Z_BLK0 * 128 // D)), pl.BlockSpec((1, 128), lambda i: (0, 0))],
               out_specs=pl.BlockSpec((tm, D), lambda i: (i, 0)), sem=("parallel",))(o, proj, wn)


def gdn_outnorm_bwd(o, proj, wn, dy, name):
    T = o.shape[1]
    tm = _tile(T, 512)

    def body(o_ref, z_ref, wn_ref, dy_ref, do_ref, dz_ref, dwn_ref):
        wn = wn_ref[...]
        acc = jnp.zeros((1, 128), F32)
        for h in range(HEADS_A):
            cols = slice(128 * h, 128 * (h + 1))
            z, dyh, ov = z_ref[:, cols], dy_ref[:, cols], o_ref[h]
            sg = _sigmoid(z)
            do, dwn = _rms_bwd(ov, wn, dyh * (z * sg))
            do_ref[h] = do
            acc = acc + dwn
            dz_ref[:, cols] = (dyh * _rms_fwd(ov, wn) * (sg * (1.0 + z * (1.0 - sg)))).astype(BF16)

        @pl.when(pl.program_id(0) == 0)
        def _():
            dwn_ref[...] = jnp.zeros_like(dwn_ref)
        dwn_ref[...] += acc

    row = pl.BlockSpec((tm, D), lambda i: (i, 0))
    vec = pl.BlockSpec((1, 128), lambda i: (0, 0))
    hblk = pl.BlockSpec((HEADS_A, tm, 128), lambda i: (0, i, 0))
    return _pc(body, name=name, out_shape=(_sds((HEADS_A, T, 128), F32), _sds((T, D), BF16), _sds((1, 128), F32)),
               grid=(T // tm,),
               in_specs=[hblk, pl.BlockSpec((tm, D), lambda i: (i, Z_BLK0 * 128 // D)), vec, row],
               out_specs=(hblk, row, vec), sem=("arbitrary",))(o, proj, wn, dy)


def gdn_forward(x, nw, w_in, wconv, al, dtb, wn, w_out, tag, deps=()):
    h = rmsnorm_bf16(x, nw, f"{tag}_norm", deps)
    proj = mm_nn(h, w_in, f"{tag}_proj")
    c, qkv = gdn_conv_fwd(proj, wconv, f"{tag}_conv")
    gb = gdn_gate_fwd(proj, al, dtb, f"{tag}_gate")
    u, w, qd, kd, aqk, gl, inv = gdn_prep_fwd(qkv, gb, f"{tag}_prep")
    o, sin = gdn_scan_fwd(u, w, qd, kd, aqk, gl, f"{tag}_scan")
    on = gdn_outnorm_fwd(o, proj, wn, f"{tag}_outnorm")
    y = mm_nn(on, w_out, f"{tag}_out", residual=x)
    return y, (x, h, proj, c, qkv, gb, inv, (u, w, qd, kd, aqk, gl), sin, o, on)


def gdn_backward(dout, saved, nw, w_in, wconv, al, dtb, wn, w_out, tag):
    x, h, proj, c, qkv, gb, inv, prep, sin, o, on = saved
    d_on = mm_nt(dout, w_out, f"{tag}_out_bwd")
    dw_out = mm_tn(on, dout, f"{tag}_out_wgrad")
    do, dz, dwn = gdn_outnorm_bwd(o, proj, wn, d_on, f"{tag}_outnorm_bwd")
    du, dw, dqd, dkd, da, dgl = gdn_scan_bwd(*prep, sin, do, f"{tag}_scan_bwd")
    dqkv, dgb = gdn_prep_bwd(qkv, gb, inv, du, dw, dqd, dkd, da, dgl, f"{tag}_prep_bwd")
    dba, dal, ddt = gdn_gate_bwd(proj, al, dtb, dgb, f"{tag}_gate_bwd")
    dpre, dwconv = gdn_conv_bwd(dqkv, c, proj, wconv, f"{tag}_conv_bwd")
    dproj = jnp.concatenate([dpre, dz, dba], axis=1)
    dw_in = mm_tn(h, dproj, f"{tag}_proj_wgrad")
    dh = mm_nt(dproj, w_in, f"{tag}_proj_bwd")
    dx, dnw = rmsnorm_bwd_add(x, nw, dh, dout, f"{tag}_norm_bwd")
    return dx, dnw, dw_in, dwconv, dal, ddt, dwn, dw_out


N_KV, GROUP = 4, 4
KV_COLS = 2 * N_KV * B_HD
B_COLS = D + KV_COLS


def _swa_block(q, kp, kc, vp, vc, sk, first):
    rows = GROUP * B_BLK
    qi = lax.broadcasted_iota(jnp.int32, (rows, B_BLK), 0) % B_BLK
    kj = lax.broadcasted_iota(jnp.int32, (rows, B_BLK), 1)
    from_cur = kj <= qi
    outs = []
    for j in range(N_KV):
        heads = range(GROUP * j, GROUP * (j + 1))
        cols = slice(j * B_HD, (j + 1) * B_HD)
        qs = jnp.concatenate([q[:, hq * B_HD:(hq + 1) * B_HD] for hq in heads], axis=0).astype(BF16)
        s_cur = _dot(qs, kc[:, cols].astype(BF16), NT)
        s_prev = jnp.where(first, -1e30, _dot(qs, kp[:, cols].astype(BF16), NT))
        s = jnp.where(from_cur, s_cur, s_prev) * (B_HD ** -0.5)
        sink = jnp.concatenate([jnp.broadcast_to(sk[:, hq:hq + 1], (B_BLK, 1)) for hq in heads], axis=0)
        m = lax.stop_gradient(jnp.maximum(jnp.max(s, axis=-1, keepdims=True), sink))
        p = jnp.exp(s - m)
        p = p / (jnp.sum(p, axis=-1, keepdims=True) + jnp.exp(sink - m))
        o = (_dot(jnp.where(from_cur, p, 0.0).astype(BF16), vc[:, cols].astype(BF16))
             + _dot(jnp.where(from_cur, 0.0, p).astype(BF16), vp[:, cols].astype(BF16)))
        outs += [o[g * B_BLK:(g + 1) * B_BLK] for g in range(GROUP)]
    return jnp.concatenate(outs, axis=1)


def swa_core_fwd(proj, sk, name):
    T = proj.shape[0]
    half = N_KV * B_HD

    def body(q_ref, kvc_ref, kvp_ref, sk_ref, o_ref):
        kvc, kvp = kvc_ref[...], kvp_ref[...]
        o_ref[...] = _swa_block(q_ref[...], kvp[:, :half], kvc[:, :half], kvp[:, half:], kvc[:, half:], sk_ref[...],
                                pl.program_id(0) == 0).astype(BF16)

    return _pc(body, name=name, out_shape=_sds((T, D), BF16), grid=(T // B_BLK,),
               in_specs=[pl.BlockSpec((B_BLK, D), lambda n: (n, 0)),
                         pl.BlockSpec((B_BLK, KV_COLS), lambda n: (n, D // KV_COLS)),
                         pl.BlockSpec((B_BLK, KV_COLS), lambda n: (jnp.maximum(n - 1, 0), D // KV_COLS)),
                         pl.BlockSpec((1, 128), lambda n: (0, 0))],
               out_specs=pl.BlockSpec((B_BLK, D), lambda n: (n, 0)), sem=("parallel",))(proj, proj, proj, sk)


def swa_core_bwd(proj, sk, do, name):
    T = proj.shape[0]
    last = T // B_BLK - 1
    half = N_KV * B_HD

    def body(q_ref, kvc_ref, kvp_ref, sk_ref, do_ref, dproj_ref, dbias_ref, dsk_ref, carry):
        step = pl.program_id(0)
        first = step == last

        @pl.when(step == 0)
        def _():
            carry[...] = jnp.zeros_like(carry)
            dbias_ref[...] = jnp.zeros_like(dbias_ref)
            dsk_ref[...] = jnp.zeros_like(dsk_ref)
        kvc, kvp = kvc_ref[...], kvp_ref[...]
        _, vjp = jax.vjp(functools.partial(_swa_block, first=first), q_ref[...], kvp[:, :half], kvc[:, :half],
                         kvp[:, half:], kvc[:, half:], sk_ref[...])
        dq, dkp, dkc, dvp, dvc, dsk = vjp(do_ref[...])
        dkv = jnp.concatenate([dkc, dvc], axis=1) + carry[...]
        carry[...] = jnp.concatenate([dkp, dvp], axis=1)
        row = jnp.concatenate([dq, dkv], axis=1)
        dproj_ref[...] = row.astype(BF16)
        dbias_ref[...] += jnp.sum(row, axis=0, keepdims=True)
        dsk_ref[...] += dsk

    return _pc(body, name=name, out_shape=(_sds((T, B_COLS), BF16), _sds((1, B_COLS), F32), _sds((1, 128), F32)),
               grid=(T // B_BLK,),
               in_specs=[pl.BlockSpec((B_BLK, D), lambda n: (last - n, 0)),
                         pl.BlockSpec((B_BLK, KV_COLS), lambda n: (last - n, D // KV_COLS)),
                         pl.BlockSpec((B_BLK, KV_COLS), lambda n: (jnp.maximum(last - n - 1, 0), D // KV_COLS)),
                         pl.BlockSpec((1, 128), lambda n: (0, 0)), pl.BlockSpec((B_BLK, D), lambda n: (last - n, 0))],
               out_specs=(pl.BlockSpec((B_BLK, B_COLS), lambda n: (last - n, 0)),
                          pl.BlockSpec((1, B_COLS), lambda n: (0, 0)), pl.BlockSpec((1, 128), lambda n: (0, 0))),
               scratch=[pltpu.VMEM((B_BLK, KV_COLS), F32)], sem=("arbitrary",))(proj, proj, proj, sk, do)


def col_sum(a, name):
    T, N = a.shape
    tm = _tile(T, 1024)

    def body(a_ref, o_ref):
        @pl.when(pl.program_id(0) == 0)
        def _():
            o_ref[...] = jnp.zeros_like(o_ref)
        o_ref[...] += jnp.sum(a_ref[...].astype(F32), axis=0, keepdims=True)

    return _pc(body, name=name, out_shape=_sds((1, N), F32), grid=(T // tm,),
               in_specs=[pl.BlockSpec((tm, N), lambda i: (i, 0))], out_specs=pl.BlockSpec((1, N), lambda i: (0, 0)),
               sem=("arbitrary",))(a)


def swa_forward(x, nw, w_in, b_in, sk, w_out, b_out, tag):
    h = rmsnorm_bf16(x, nw, f"{tag}_norm")
    proj = mm_nn(h, w_in, f"{tag}_proj", bias=b_in)
    o = swa_core_fwd(proj, sk, f"{tag}_core")
    y = mm_nn(o, w_out, f"{tag}_out", bias=b_out, residual=x)
    return y, (x, h, proj, o)


def swa_backward(dout, saved, nw, w_in, b_in, sk, w_out, b_out, tag):
    x, h, proj, o = saved
    do = mm_nt(dout, w_out, f"{tag}_out_bwd")
    dw_out = mm_tn(o, dout, f"{tag}_out_wgrad")
    db_out = col_sum(dout, f"{tag}_out_bias_grad")
    dproj, db_in, dsk = swa_core_bwd(proj, sk, do, f"{tag}_core_bwd")
    dw_in = mm_tn(h, dproj, f"{tag}_proj_wgrad")
    dh = mm_nt(dproj, w_in, f"{tag}_proj_bwd")
    dx, dnw = rmsnorm_bwd_add(x, nw, dh, dout, f"{tag}_norm_bwd")
    return dx, dnw, dw_in, db_in, dsk, dw_out, db_out


MESH = pl.DeviceIdType.MESH


def _position():
    return lax.axis_index("x"), lax.axis_index("y"), lax.axis_index("c")


def _slot(x, y, c):
    return 4 * x + 2 * y + c


def _peer(x, y, c, k):
    return (1 - x if k & 4 else x, 1 - y if k & 2 else y, 1 - c if k & 1 else c)


HBM_SPEC = pl.BlockSpec(memory_space=pltpu.HBM)
SEM_SPEC = pl.BlockSpec(memory_space=pltpu.SEMAPHORE)
DEP_SPEC = pl.BlockSpec(memory_space=pl.ANY)
SIDE_EFFECT = pltpu.SideEffectType.DATAFLOW_SIDE_EFFECTING
N_PEERS = N_DEV - 1


def _push_copies(srcs, lands, send_sems, recv_sems, scatter):
    x, y, c = _position()
    me = _slot(x, y, c)
    copies = []
    for k in (1, 2, 4, 3, 5, 6, 7):
        peer = _peer(x, y, c, k)
        for a in range(len(srcs)):
            copies.append(pltpu.make_async_remote_copy(
                src_ref=srcs[a].at[_slot(*peer)] if scatter else srcs[a], dst_ref=lands[a].at[me],
                send_sem=send_sems.at[N_PEERS * a + k - 1], recv_sem=recv_sems.at[N_PEERS * a + k - 1],
                device_id=peer, device_id_type=MESH))
    return copies


def push_start(srcs, lands, name, scatter, deps=()):
    n = len(srcs)
    first_out = 2 * n + len(deps)

    def body(*refs):
        for cp in _push_copies(refs[:n], refs[n:2 * n], refs[first_out], refs[first_out + 1], scatter):
            cp.start()
        refs[-1][...] = jnp.zeros_like(refs[-1])

    passed = [pltpu.HBM(t.shape, t.dtype) for t in list(srcs) + list(lands)]
    res = pl.pallas_call(
        body, name=name,
        out_shape=(pltpu.SemaphoreType.DMA((N_PEERS * n,)), pltpu.SemaphoreType.DMA((N_PEERS * n,)), *passed, _sds((8, 128), F32)),
        in_specs=[HBM_SPEC] * (2 * n) + [DEP_SPEC] * len(deps),
        out_specs=(SEM_SPEC, SEM_SPEC, *([HBM_SPEC] * (2 * n)), pl.BlockSpec(memory_space=pltpu.VMEM)),
        input_output_aliases={i: 2 + i for i in range(2 * n)},
        compiler_params=pltpu.CompilerParams(has_side_effects=SIDE_EFFECT),
    )(*[pltpu.with_memory_space_constraint(t, pltpu.HBM) for t in list(srcs) + list(lands)], *deps)
    return (res[0], res[1], list(res[2:2 + n]), list(res[2 + n:2 + 2 * n])), res[-1]


def push_wait(handles, after, name, scatter):
    send_sems, recv_sems, srcs, lands = handles
    n = len(srcs)
    after = tuple(after) if isinstance(after, (tuple, list)) else (after,)

    def body(*refs):
        for cp in _push_copies(refs[:n], refs[n:2 * n], refs[2 * n], refs[2 * n + 1], scatter):
            cp.wait_send()
            cp.wait_recv()

    res = pl.pallas_call(
        body, name=name, out_shape=tuple(pltpu.HBM(t.shape, t.dtype) for t in srcs + lands),
        in_specs=[HBM_SPEC] * (2 * n) + [SEM_SPEC, SEM_SPEC] + [DEP_SPEC] * len(after), out_specs=tuple([HBM_SPEC] * (2 * n)),
        input_output_aliases={i: i for i in range(2 * n)},
        compiler_params=pltpu.CompilerParams(has_side_effects=SIDE_EFFECT),
    )(*srcs, *lands, send_sems, recv_sems, *after)
    return list(res[n:])


def gather_start(shards, name, deps=()):
    me = _slot(*_position())
    lands = [lax.dynamic_update_slice(lax.empty((N_DEV,) + t.shape, t.dtype), t[None], (me,) + (0,) * t.ndim) for t in shards]
    return push_start(shards, lands, name, scatter=False, deps=deps)


def exchange_start(parts, name):
    me = _slot(*_position())
    lands = [lax.dynamic_update_slice(lax.empty(t.shape, t.dtype), lax.dynamic_index_in_dim(t, me, 0, keepdims=True),
                                      (me,) + (0,) * (t.ndim - 1)) for t in parts]
    return push_start(parts, lands, name, scatter=True)


def _row_tile(rows, cols):
    best = rows
    for t in range(16, rows, 16):
        if rows % t == 0 and t * cols * 4 <= (1 << 20):
            best = t
    return best


def adam_update(parts, w, m, v, name):
    n_layers = len(parts)
    P, R, C = parts[0].shape
    tr = _row_tile(R, C)
    n_t = R // tr

    def body(*refs):
        p_refs = refs[:n_layers]
        w_ref, m_ref, v_ref, g_ref, d_ref, nm_ref, nv_ref = refs[n_layers:]
        for layer in range(n_layers):
            @pl.when(pl.program_id(0) == layer)
            def _(p_ref=p_refs[layer]):
                g = p_ref[0].astype(F32)
                for s in range(1, P):
                    g = g + p_ref[s].astype(F32)
                new_m = ADAM_B1 * m_ref[0] + (1.0 - ADAM_B1) * g
                new_v = ADAM_B2 * v_ref[0] + (1.0 - ADAM_B2) * (g * g)
                m_hat = new_m / (1.0 - ADAM_B1 ** ADAM_STEP)
                v_hat = new_v / (1.0 - ADAM_B2 ** ADAM_STEP)
                g_ref[0] = g
                d_ref[0] = -ADAM_LR * (m_hat / (jnp.sqrt(v_hat) + ADAM_EPS) + ADAM_WD * w_ref[0])
                nm_ref[0] = new_m
                nv_ref[0] = new_v

    def part_spec(layer):
        return pl.BlockSpec((P, tr, C), lambda l_, i: (0, jnp.where(l_ == layer, i, jnp.where(l_ < layer, 0, n_t - 1)), 0))

    blk = pl.BlockSpec((1, tr, C), lambda l_, i: (l_, i, 0))
    out = _sds((n_layers, R, C), F32)
    return _pc(body, name=name, out_shape=(out, out, out, out), grid=(n_layers, n_t),
               in_specs=[part_spec(layer) for layer in range(n_layers)] + [blk, blk, blk],
               out_specs=(blk, blk, blk, blk), sem=("arbitrary", "arbitrary"))(*parts, w, m, v)


WEIGHTS = ("ffn1_norm", "ffn1_w_gu", "ffn1_w_down", "mix_norm", "ffn2_norm", "ffn2_w_gu", "ffn2_w_down", "a_w_in",
           "a_w_conv", "a_A_log", "a_dt_bias", "a_out_norm", "a_w_out", "b_w_in", "b_b_in", "b_sinks", "b_w_out",
           "b_b_out", "final_norm")
SHARDED = ("ffn1_w_gu", "ffn1_w_down", "ffn2_w_gu", "ffn2_w_down", "a_w_in", "a_w_conv", "a_w_out", "b_w_in", "b_b_in",
           "b_w_out", "b_b_out")
MISC_LANES = dict(a_A_log=(0, 8), a_dt_bias=(8, 16), b_sinks=(16, 32), a_out_norm=(128, 256))
LOSS_LANE = 256


def _pack_small(t):
    misc = jnp.zeros((D,), F32)
    for key, (lo, hi) in MISC_LANES.items():
        misc = misc.at[lo:hi].set(t[key].reshape(-1))
    if "loss" in t:
        misc = misc.at[LOSS_LANE].set(t["loss"])
    return jnp.concatenate([t["ffn1_norm"], t["mix_norm"], t["ffn2_norm"], t["final_norm"].reshape(1, D), misc[None]], axis=0)


def _unpack_small(p, like):
    out = dict(ffn1_norm=p[0:2], mix_norm=p[2:4], ffn2_norm=p[4:6], final_norm=p[6])
    for key, (lo, hi) in MISC_LANES.items():
        out[key] = p[7, lo:hi].reshape(like[key].shape)
    return out


def kernel(x, ffn1_norm, ffn1_w_gu, ffn1_w_down, mix_norm, ffn2_norm, ffn2_w_gu, ffn2_w_down, a_w_in, a_w_conv, a_A_log, a_dt_bias, a_out_norm, a_w_out, b_w_in, b_b_in, b_sinks, b_w_out, b_b_out, final_norm, loss_target, m_ffn1_norm, m_ffn1_w_gu, m_ffn1_w_down, m_mix_norm, m_ffn2_norm, m_ffn2_w_gu, m_ffn2_w_down, m_a_w_in, m_a_w_conv, m_a_A_log, m_a_dt_bias, m_a_out_norm, m_a_w_out, m_b_w_in, m_b_b_in, m_b_sinks, m_b_w_out, m_b_b_out, m_final_norm, v_ffn1_norm, v_ffn1_w_gu, v_ffn1_w_down, v_mix_norm, v_ffn2_norm, v_ffn2_w_gu, v_ffn2_w_down, v_a_w_in, v_a_w_conv, v_a_A_log, v_a_dt_bias, v_a_out_norm, v_a_w_out, v_b_w_in, v_b_b_in, v_b_sinks, v_b_w_out, v_b_b_out, v_final_norm):
    w = dict(ffn1_norm=ffn1_norm, ffn1_w_gu=ffn1_w_gu, ffn1_w_down=ffn1_w_down, mix_norm=mix_norm, ffn2_norm=ffn2_norm, ffn2_w_gu=ffn2_w_gu, ffn2_w_down=ffn2_w_down, a_w_in=a_w_in, a_w_conv=a_w_conv, a_A_log=a_A_log, a_dt_bias=a_dt_bias, a_out_norm=a_out_norm, a_w_out=a_w_out, b_w_in=b_w_in, b_b_in=b_b_in, b_sinks=b_sinks, b_w_out=b_w_out, b_b_out=b_b_out, final_norm=final_norm)
    m = dict(ffn1_norm=m_ffn1_norm, ffn1_w_gu=m_ffn1_w_gu, ffn1_w_down=m_ffn1_w_down, mix_norm=m_mix_norm, ffn2_norm=m_ffn2_norm, ffn2_w_gu=m_ffn2_w_gu, ffn2_w_down=m_ffn2_w_down, a_w_in=m_a_w_in, a_w_conv=m_a_w_conv, a_A_log=m_a_A_log, a_dt_bias=m_a_dt_bias, a_out_norm=m_a_out_norm, a_w_out=m_a_w_out, b_w_in=m_b_w_in, b_b_in=m_b_b_in, b_sinks=m_b_sinks, b_w_out=m_b_w_out, b_b_out=m_b_b_out, final_norm=m_final_norm)
    v = dict(ffn1_norm=v_ffn1_norm, ffn1_w_gu=v_ffn1_w_gu, ffn1_w_down=v_ffn1_w_down, mix_norm=v_mix_norm, ffn2_norm=v_ffn2_norm, ffn2_w_gu=v_ffn2_w_gu, ffn2_w_down=v_ffn2_w_down, a_w_in=v_a_w_in, a_w_conv=v_a_w_conv, a_A_log=v_a_A_log, a_dt_bias=v_a_dt_bias, a_out_norm=v_a_out_norm, a_w_out=v_a_w_out, b_w_in=v_b_w_in, b_b_in=v_b_b_in, b_sinks=v_b_sinks, b_w_out=v_b_w_out, b_b_out=v_b_b_out, final_norm=v_final_norm)
    T = x.shape[1]
    x0, tgt = x.reshape(T, D), loss_target.reshape(T, D)

    def cast(t):
        return t.astype(BF16)

    h0, t0 = gather_start([cast(ffn1_w_gu[0])], "gather0_start")
    a_log_row = jnp.zeros((1, 128), F32).at[0, HEADS_A:2 * HEADS_A].set(a_A_log[0])
    dt_row = jnp.zeros((1, 128), F32).at[0, HEADS_A:2 * HEADS_A].set(a_dt_bias[0])
    sink_row = jnp.zeros((1, 128), F32).at[0, :b_sinks.shape[1]].set(b_sinks[0])
    a_in_cols = a_w_in.shape[-1] * N_DEV

    def down_blocks(t):
        return t.reshape(N_FB, FB, D)

    wgu, wdn, saved = {}, {}, []
    xn = rmsnorm_bf16(x0, ffn1_norm[0:1], "l0_ffn1_norm", (t0,))
    wgu["ffn1", 0] = push_wait(h0, xn, "gather0_wait", scatter=False)[0]
    h0d, t0d = gather_start([cast(ffn1_w_down[0])], "gather0d_start", deps=(wgu["ffn1", 0],))
    h1, t1 = gather_start([cast(a_w_in[0]), a_w_conv[0], cast(a_w_out[0])], "gather1_start", deps=(t0d,))
    gu = ffn_up(xn, wgu["ffn1", 0], "l0_ffn1_up", deps=(t0d, t1))
    wdn["ffn1", 0] = down_blocks(push_wait(h0d, gu, "gather0d_wait", scatter=False)[0])
    xs, s1 = ffn_down(gu, wdn["ffn1", 0], x0, "l0_ffn1_down"), (x0, xn, gu)
    got = push_wait(h1, xs, "gather1_wait", scatter=False)
    h1f, t1f = gather_start([cast(ffn2_w_gu[0]), cast(ffn2_w_down[0])], "gather1f_start", deps=(got[0],))
    g2 = [cast(ffn1_w_gu[1]), cast(ffn1_w_down[1]), cast(b_w_in[0]), b_b_in, cast(b_w_out[0]), b_b_out,
          cast(ffn2_w_gu[1]), cast(ffn2_w_down[1])]
    h2, t2 = gather_start(g2, "gather2_start", deps=(t1f,))
    a_in_full = jnp.pad(got[0].transpose(1, 0, 2).reshape(D, a_in_cols), ((0, 0), (0, A_COLS - a_in_cols)))
    gdn_args = (mix_norm[0:1], a_in_full, got[1].transpose(1, 0, 2).reshape(4, 3 * D), a_log_row, dt_row, a_out_norm,
                got[2].reshape(D, D))
    xs, sm = gdn_forward(xs, *gdn_args, "gdn", deps=(t1f, t2))
    got = push_wait(h1f, xs, "gather1f_wait", scatter=False)
    wgu["ffn2", 0], wdn["ffn2", 0] = got[0], down_blocks(got[1])
    xs, s2 = ffn_forward(xs, ffn2_norm[0:1], wgu["ffn2", 0], wdn["ffn2", 0], "l0_ffn2")
    saved.append((s1, sm, s2))
    got = push_wait(h2, xs, "gather2_wait", scatter=False)
    wgu["ffn1", 1], wdn["ffn1", 1] = got[0], down_blocks(got[1])
    swa_args = (mix_norm[1:2], got[2].transpose(1, 0, 2).reshape(D, B_COLS), got[3].reshape(1, B_COLS), sink_row,
                got[4].reshape(D, D), got[5].reshape(1, D))
    wgu["ffn2", 1], wdn["ffn2", 1] = got[6], down_blocks(got[7])
    xs, s1 = ffn_forward(xs, ffn1_norm[1:2], wgu["ffn1", 1], wdn["ffn1", 1], "l1_ffn1")
    xs, sm = swa_forward(xs, *swa_args, "swa")
    xs, s2 = ffn_forward(xs, ffn2_norm[1:2], wgu["ffn2", 1], wdn["ffn2", 1], "l1_ffn2")
    saved.append((s1, sm, s2))
    loss_row, dx, d_final_norm = final_loss(xs, final_norm.reshape(1, D), tgt, "final_loss")

    def down_slots(t):
        return cast(t.reshape(N_DEV, FB // 2, D))

    def col_slots(t, dtype=BF16):
        return t.reshape(t.shape[0], N_DEV, -1).transpose(1, 0, 2).astype(dtype)

    d_norm = {"ffn1_norm": [None, None], "mix_norm": [None, None], "ffn2_norm": [None, None]}
    s1, sm, s2 = saved[1]
    dx, d_norm["ffn2_norm"][1], d_gu, d_dn = ffn_backward(dx, s2, ffn2_norm[1:2], wgu["ffn2", 1], wdn["ffn2", 1], "l1_ffn2")
    sent1 = [cast(d_gu), down_slots(d_dn)]
    dx, d_norm["mix_norm"][1], d_b_in, d_b_bias_in, d_sinks, d_b_out, d_b_bias_out = swa_backward(dx, sm, *swa_args, "swa")
    sent1 += [col_slots(d_b_in), d_b_bias_in.reshape(N_DEV, 1, -1), cast(d_b_out.reshape(N_DEV, D // N_DEV, D)),
              d_b_bias_out.reshape(N_DEV, 1, -1)]
    dx, d_norm["ffn1_norm"][1], d_gu, d_dn = ffn_backward(dx, s1, ffn1_norm[1:2], wgu["ffn1", 1], wdn["ffn1", 1], "l1_ffn1")
    sent1 += [cast(d_gu), down_slots(d_dn)]
    x1, tx1 = exchange_start(sent1, "exchange1_start")

    s1, sm, s2 = saved[0]
    dx, d_norm["ffn2_norm"][0], d_gu, d_dn = ffn_backward(dx, s2, ffn2_norm[0:1], wgu["ffn2", 0], wdn["ffn2", 0], "l0_ffn2",
                                                           deps=(tx1,))
    sent2 = [cast(d_gu), down_slots(d_dn)]
    dx, d_norm["mix_norm"][0], d_a_in, d_a_conv, d_alog, d_dt, d_onorm, d_a_out = gdn_backward(dx, sm, *gdn_args, "gdn")
    sent2 += [col_slots(d_a_in[:, :a_in_cols]), col_slots(d_a_conv, F32), cast(d_a_out.reshape(N_DEV, D // N_DEV, D))]
    x2, tx2 = exchange_start(sent2, "exchange2_start")
    last = {}

    def send_last(d_gu, d_dn):
        last["handles"], token = exchange_start([cast(d_gu), down_slots(d_dn)], "exchange3_start")
        return (token,)

    dx, d_norm["ffn1_norm"][0], _, _ = ffn_backward(dx, s1, ffn1_norm[0:1], wgu["ffn1", 0], wdn["ffn1", 0], "l0_ffn1",
                                                    deps=(tx2,), on_grads=send_last)
    grad_x = dx.reshape(x.shape)
    r1 = push_wait(x1, dx, "exchange1_wait", scatter=True)
    r2 = push_wait(x2, dx, "exchange2_wait", scatter=True)
    received = dict(ffn2_w_gu=[r2[0], r1[0]], ffn2_w_down=[r2[1], r1[1]],
                    b_w_in=[r1[2]], b_b_in=[r1[3]], b_w_out=[r1[4]], b_b_out=[r1[5]],
                    a_w_in=[r2[2]], a_w_conv=[r2[3]], a_w_out=[r2[4]])

    grads, deltas, new_m, new_v = {}, {}, {}, {}

    def update(key):
        shape = w[key].shape
        cols = shape[-1]
        layers = lambda t: t.reshape(shape[0], -1, cols)
        out = adam_update([r.reshape(N_DEV, -1, cols) for r in received[key]], layers(w[key]), layers(m[key]), layers(v[key]),
                          f"adam_{key}")
        grads[key], deltas[key], new_m[key], new_v[key] = (t.reshape(shape) for t in out)

    for key in SHARDED:
        if key in received:
            update(key)
    done_first = [deltas[key] for key in received]

    small = dict(ffn1_norm=jnp.concatenate(d_norm["ffn1_norm"], axis=0), mix_norm=jnp.concatenate(d_norm["mix_norm"], axis=0),
                 ffn2_norm=jnp.concatenate(d_norm["ffn2_norm"], axis=0), final_norm=d_final_norm,
                 a_A_log=d_alog[0, HEADS_A:2 * HEADS_A], a_dt_bias=d_dt[0, HEADS_A:2 * HEADS_A],
                 b_sinks=d_sinks[0, :b_sinks.shape[1]], a_out_norm=d_onorm, loss=loss_row[0, 0])
    hs, ts = gather_start([_pack_small(small)], "gather_small_start")
    r3 = push_wait(last["handles"], done_first + [ts], "exchange3_wait", scatter=True)
    received.update(ffn1_w_gu=[r3[0], r1[6]], ffn1_w_down=[r3[1], r1[7]])
    update("ffn1_w_gu")
    update("ffn1_w_down")
    every = push_wait(hs, deltas["ffn1_w_down"], "gather_small_wait", scatter=False)[0]
    out = adam_update([every], _pack_small(w)[None], _pack_small(m)[None], _pack_small(v)[None], "adam_small")
    for dst, packed in zip((grads, deltas, new_m, new_v), out):
        dst.update(_unpack_small(packed[0], w))
    loss = out[0][0, 7, LOSS_LANE]

    return (loss, grad_x, *[grads[k_] for k_ in WEIGHTS], *[deltas[k_] for k_ in WEIGHTS],
            *[new_m[k_] for k_ in WEIGHTS], *[new_v[k_] for k_ in WEIGHTS])
```

```python
import functools

import jax
import jax.numpy as jnp
from jax import lax
from jax.experimental import pallas as pl
from jax.experimental.pallas import tpu as pltpu

F32, BF16 = jnp.float32, jnp.bfloat16
HI = lax.Precision.HIGHEST
EPS = 1e-6

N_DEV = 8
D = 1024
FB = 704
N_FB = 4
HEADS_A, DK = 8, 128
CHUNK = 64
PREP_T = 512
A_COLS = 4224
B_HD, B_BLK = 64, 128
VMEM_LIMIT_V7X = 60 * 1024 * 1024

ADAM_LR, ADAM_B1, ADAM_B2, ADAM_EPS, ADAM_WD, ADAM_STEP = 0.001, 0.9, 0.999, 1e-08, 0.01, 10

NT = (((1,), (1,)), ((), ()))
TN = (((0,), (0,)), ((), ()))


def _pc(body, *, name, out_shape, grid=(), in_specs=None, out_specs=None, scratch=(), sem=None, **kw):
    params = pltpu.CompilerParams(dimension_semantics=sem, vmem_limit_bytes=VMEM_LIMIT_V7X)
    return pl.pallas_call(body, name=name, out_shape=out_shape, grid=grid, in_specs=in_specs, out_specs=out_specs,
                          scratch_shapes=list(scratch), compiler_params=params, **kw)


def _sds(shape, dtype):
    return jax.ShapeDtypeStruct(tuple(shape), dtype)


def _dot(a, b, dims=None, precision=None):
    if dims is None:
        return jnp.dot(a, b, preferred_element_type=F32, precision=precision)
    return lax.dot_general(a, b, dims, preferred_element_type=F32, precision=precision)


def _sigmoid(x):
    return 1.0 / (1.0 + jnp.exp(-x))


def _softplus(x):
    return jnp.maximum(x, 0.0) + jnp.log(1.0 + jnp.exp(-jnp.abs(x)))


def _rms_fwd(x, w):
    r = lax.rsqrt(jnp.mean(x * x, axis=-1, keepdims=True) + EPS)
    return x * r * w


def _rms_bwd(x, w, dy):
    r = lax.rsqrt(jnp.mean(x * x, axis=-1, keepdims=True) + EPS)
    xh = x * r
    dxh = dy * w
    dx = r * (dxh - xh * jnp.mean(dxh * xh, axis=-1, keepdims=True))
    return dx, jnp.sum(dy * xh, axis=0, keepdims=True)


def _tile(n, want):
    t = min(n, want)
    assert n % t == 0, (n, want)
    return t


def rmsnorm_bf16(x, w, name, deps=()):
    T = x.shape[0]
    tm = _tile(T, 1024)

    def body(x_ref, w_ref, *rest):
        rest[-1][...] = _rms_fwd(x_ref[...], w_ref[...]).astype(BF16)

    return _pc(body, name=name, out_shape=_sds((T, D), BF16), grid=(T // tm,),
               in_specs=[pl.BlockSpec((tm, D), lambda i: (i, 0)), pl.BlockSpec((1, D), lambda i: (0, 0))] + [DEP_SPEC] * len(deps),
               out_specs=pl.BlockSpec((tm, D), lambda i: (i, 0)), sem=("parallel",))(x, w, *deps)


def rmsnorm_bwd_add(x, w, dxn, dres, name):
    T = x.shape[0]
    tm = _tile(T, 512)

    def body(x_ref, w_ref, dxn_ref, dres_ref, dx_ref, dw_ref):
        dx, dw = _rms_bwd(x_ref[...], w_ref[...], dxn_ref[...])
        dx_ref[...] = dres_ref[...] + dx

        @pl.when(pl.program_id(0) == 0)
        def _():
            dw_ref[...] = jnp.zeros_like(dw_ref)
        dw_ref[...] += dw

    row = pl.BlockSpec((tm, D), lambda i: (i, 0))
    vec = pl.BlockSpec((1, D), lambda i: (0, 0))
    return _pc(body, name=name, out_shape=(_sds((T, D), F32), _sds((1, D), F32)), grid=(T // tm,),
               in_specs=[row, vec, row, row], out_specs=(row, vec), sem=("arbitrary",))(x, w, dxn, dres)


def final_loss(x, w, tgt, name):
    T = x.shape[0]
    tm = _tile(T, 512)

    def body(x_ref, w_ref, t_ref, loss_ref, dx_ref, dw_ref):
        xv, wv = x_ref[...], w_ref[...]
        err = _rms_fwd(xv, wv) - t_ref[...]
        dx, dw = _rms_bwd(xv, wv, err * (1.0 / D))
        dx_ref[...] = dx

        @pl.when(pl.program_id(0) == 0)
        def _():
            dw_ref[...] = jnp.zeros_like(dw_ref)
            loss_ref[...] = jnp.zeros_like(loss_ref)
        dw_ref[...] += dw
        loss_ref[...] += jnp.full((1, 128), 0.5 / D, F32) * jnp.sum(err * err)

    row = pl.BlockSpec((tm, D), lambda i: (i, 0))
    vec = pl.BlockSpec((1, D), lambda i: (0, 0))
    return _pc(body, name=name, out_shape=(_sds((1, 128), F32), _sds((T, D), F32), _sds((1, D), F32)),
               grid=(T // tm,), in_specs=[row, vec, row],
               out_specs=(pl.BlockSpec((1, 128), lambda i: (0, 0)), row, vec), sem=("arbitrary",))(x, w, tgt)


def _col_tile(n):
    for t in (1536, 1408, 1024, 768, 512, 384, 256, 128):
        if n % t == 0:
            return t
    return n


def mm_nn(a, b, name, bias=None, residual=None, out_dtype=F32):
    T, K = a.shape
    N = b.shape[1]
    tm, tn = _tile(T, 512), _col_tile(N)

    def body(a_ref, b_ref, *rest):
        o_ref = rest[-1]
        acc = _dot(a_ref[...].astype(BF16), b_ref[...])
        for extra in rest[:-1]:
            acc = acc + extra[...]
        o_ref[...] = acc.astype(out_dtype)

    in_specs = [pl.BlockSpec((tm, K), lambda j, i: (i, 0)), pl.BlockSpec((K, tn), lambda j, i: (0, j))]
    args = [a, b]
    if bias is not None:
        in_specs.append(pl.BlockSpec((1, tn), lambda j, i: (0, j)))
        args.append(bias)
    if residual is not None:
        in_specs.append(pl.BlockSpec((tm, tn), lambda j, i: (i, j)))
        args.append(residual)
    return _pc(body, name=name, out_shape=_sds((T, N), out_dtype), grid=(N // tn, T // tm), in_specs=in_specs,
               out_specs=pl.BlockSpec((tm, tn), lambda j, i: (i, j)), sem=("parallel", "parallel"))(*args)


def mm_nt(a, b, name, out_dtype=F32):
    T, N = a.shape
    K = b.shape[0]
    tm = _tile(T, 512)

    def body(a_ref, b_ref, o_ref):
        o_ref[...] = _dot(a_ref[...].astype(BF16), b_ref[...], NT).astype(out_dtype)

    return _pc(body, name=name, out_shape=_sds((T, K), out_dtype), grid=(T // tm,),
               in_specs=[pl.BlockSpec((tm, N), lambda i: (i, 0)), pl.BlockSpec((K, N), lambda i: (0, 0))],
               out_specs=pl.BlockSpec((tm, K), lambda i: (i, 0)), sem=("parallel",))(a, b)


def mm_tn(a, b, name):
    T, K = a.shape
    N = b.shape[1]
    tt, tn = _tile(T, 1024), _col_tile(N)

    def body(a_ref, b_ref, o_ref):
        @pl.when(pl.program_id(1) == 0)
        def _():
            o_ref[...] = jnp.zeros_like(o_ref)
        o_ref[...] += _dot(a_ref[...].astype(BF16), b_ref[...].astype(BF16), TN)

    return _pc(body, name=name, out_shape=_sds((K, N), F32), grid=(N // tn, T // tt),
               in_specs=[pl.BlockSpec((tt, K), lambda j, t: (t, 0)), pl.BlockSpec((tt, tn), lambda j, t: (t, j))],
               out_specs=pl.BlockSpec((K, tn), lambda j, t: (0, j)), sem=("parallel", "arbitrary"))(a, b)


def ffn_up(xn, wgu, name, deps=()):
    T = xn.shape[0]
    tm = _tile(T, 512)

    def body(x_ref, w_ref, *rest):
        xv = x_ref[...]
        for j in range(2 * N_FB):
            rest[-1][j] = _dot(xv, w_ref[j]).astype(BF16)

    return _pc(body, name=name, out_shape=_sds((2 * N_FB, T, FB), BF16), grid=(T // tm,),
               in_specs=[pl.BlockSpec((tm, D), lambda i: (i, 0)), pl.BlockSpec((2 * N_FB, D, FB), lambda i: (0, 0, 0))]
               + [DEP_SPEC] * len(deps),
               out_specs=pl.BlockSpec((2 * N_FB, tm, FB), lambda i: (0, i, 0)), sem=("parallel",))(xn, wgu, *deps)


def ffn_down(gu, wd, x, name):
    T = x.shape[0]
    tm = _tile(T, 512)

    def body(gu_ref, w_ref, x_ref, o_ref):
        acc = jnp.zeros((tm, D), F32)
        for g in range(N_FB):
            gate, up = gu_ref[g], gu_ref[N_FB + g]
            acc = acc + _dot(gate * _sigmoid(gate) * up, w_ref[g])
        o_ref[...] = x_ref[...] + 0.5 * acc

    row = pl.BlockSpec((tm, D), lambda i: (i, 0))
    return _pc(body, name=name, out_shape=_sds((T, D), F32), grid=(T // tm,),
               in_specs=[pl.BlockSpec((2 * N_FB, tm, FB), lambda i: (0, i, 0)),
                         pl.BlockSpec((N_FB, FB, D), lambda i: (0, 0, 0)), row],
               out_specs=row, sem=("parallel",))(gu, wd, x)


def ffn_bwd_hidden(dout, wd, gu, name, deps=()):
    T = dout.shape[0]
    tm = _tile(T, 512)

    def body(d_ref, w_ref, gu_ref, *rest):
        dgu_ref, act_ref = rest[-2:]
        dy = (0.5 * d_ref[...]).astype(BF16)
        for g in range(N_FB):
            gate, up = gu_ref[g], gu_ref[N_FB + g]
            sg = _sigmoid(gate)
            silu = gate * sg
            dact = _dot(dy, w_ref[g], NT).astype(BF16)
            act_ref[g] = silu * up
            dgu_ref[g] = dact * up * (sg * (1.0 + gate * (1.0 - sg)))
            dgu_ref[N_FB + g] = dact * silu

    return _pc(body, name=name, out_shape=(_sds((2 * N_FB, T, FB), BF16), _sds((N_FB, T, FB), BF16)), grid=(T // tm,),
               in_specs=[pl.BlockSpec((tm, D), lambda i: (i, 0)), pl.BlockSpec((N_FB, FB, D), lambda i: (0, 0, 0)),
                         pl.BlockSpec((2 * N_FB, tm, FB), lambda i: (0, i, 0))] + [DEP_SPEC] * len(deps),
               out_specs=(pl.BlockSpec((2 * N_FB, tm, FB), lambda i: (0, i, 0)),
                          pl.BlockSpec((N_FB, tm, FB), lambda i: (0, i, 0))), sem=("parallel",))(dout, wd, gu, *deps)


def ffn_bwd_input(dgu, wgu, x, dout, nw, name, deps=()):
    T = x.shape[0]
    tm = _tile(T, 512)

    def body(dgu_ref, w_ref, x_ref, d_ref, nw_ref, *rest):
        dx_ref, dnw_ref = rest[-2:]
        dxn = jnp.zeros((tm, D), F32)
        for j in range(2 * N_FB):
            dxn = dxn + _dot(dgu_ref[j], w_ref[j], NT)
        dx, dw = _rms_bwd(x_ref[...], nw_ref[...], dxn)
        dx_ref[...] = d_ref[...] + dx

        @pl.when(pl.program_id(0) == 0)
        def _():
            dnw_ref[...] = jnp.zeros_like(dnw_ref)
        dnw_ref[...] += dw

    row = pl.BlockSpec((tm, D), lambda i: (i, 0))
    vec = pl.BlockSpec((1, D), lambda i: (0, 0))
    return _pc(body, name=name, out_shape=(_sds((T, D), F32), _sds((1, D), F32)), grid=(T // tm,),
               in_specs=[pl.BlockSpec((2 * N_FB, tm, FB), lambda i: (0, i, 0)),
                         pl.BlockSpec((2 * N_FB, D, FB), lambda i: (0, 0, 0)), row, row, vec] + [DEP_SPEC] * len(deps),
               out_specs=(row, vec), sem=("arbitrary",))(dgu, wgu, x, dout, nw, *deps)


def ffn_wgrad_gu(xn, dgu, name):
    T = xn.shape[0]
    tt = _tile(T, 1024)

    def body(x_ref, d_ref, o_ref):
        @pl.when(pl.program_id(1) == 0)
        def _():
            o_ref[...] = jnp.zeros_like(o_ref)
        o_ref[0] += _dot(x_ref[...], d_ref[0], TN)

    return _pc(body, name=name, out_shape=_sds((2 * N_FB, D, FB), F32), grid=(2 * N_FB, T // tt),
               in_specs=[pl.BlockSpec((tt, D), lambda j, t: (t, 0)), pl.BlockSpec((1, tt, FB), lambda j, t: (j, t, 0))],
               out_specs=pl.BlockSpec((1, D, FB), lambda j, t: (j, 0, 0)), sem=("parallel", "arbitrary"))(xn, dgu)


def ffn_wgrad_down(act, dout, name):
    T = dout.shape[0]
    tt = _tile(T, 1024)

    def body(a_ref, d_ref, o_ref):
        @pl.when(pl.program_id(1) == 0)
        def _():
            o_ref[...] = jnp.zeros_like(o_ref)
        o_ref[0] += _dot(a_ref[0], (0.5 * d_ref[...]).astype(BF16), TN)

    return _pc(body, name=name, out_shape=_sds((N_FB, FB, D), F32), grid=(N_FB, T // tt),
               in_specs=[pl.BlockSpec((1, tt, FB), lambda g, t: (g, t, 0)), pl.BlockSpec((tt, D), lambda g, t: (t, 0))],
               out_specs=pl.BlockSpec((1, FB, D), lambda g, t: (g, 0, 0)), sem=("parallel", "arbitrary"))(act, dout)


def ffn_forward(x, nw, wgu, wd, tag, deps=()):
    xn = rmsnorm_bf16(x, nw, f"{tag}_norm", deps)
    gu = ffn_up(xn, wgu, f"{tag}_up")
    return ffn_down(gu, wd, x, f"{tag}_down"), (x, xn, gu)


def ffn_backward(dout, saved, nw, wgu, wd, tag, deps=(), on_grads=None):
    x, xn, gu = saved
    dgu, act = ffn_bwd_hidden(dout, wd, gu, f"{tag}_bwd_hidden", deps)
    dwd = ffn_wgrad_down(act, dout, f"{tag}_wgrad_down")
    dwgu = ffn_wgrad_gu(xn, dgu, f"{tag}_wgrad_gu")
    late = on_grads(dwgu, dwd) if on_grads else ()
    dx, dnw = ffn_bwd_input(dgu, wgu, x, dout, nw, f"{tag}_bwd_input", late)
    return dx, dnw, dwgu, dwd


N_QKV_BLK = 3 * HEADS_A
Z_BLK0 = N_QKV_BLK
BA_BLK = A_COLS // 128 - 1


def _conv_taps(xcat, w):
    c = xcat[8:] * w[3:4]
    for k in range(3):
        c = c + pltpu.roll(xcat, 3 - k, 0)[8:] * w[k:k + 1]
    return c


def _head_cols(h):
    return slice(128 * h, 128 * (h + 1))


def gdn_conv_fwd(proj, wconv, name):
    T = proj.shape[0]
    tm = _tile(T, 512)

    def body(cur_ref, prev_ref, w_ref, c_ref, y_ref):
        kind, t = pl.program_id(0), pl.program_id(1)
        prev = jnp.where(t > 0, prev_ref[...], 0.0)
        c = _conv_taps(jnp.concatenate([prev, cur_ref[...]], axis=0), w_ref[...])
        c_ref[...] = c
        s = c * _sigmoid(c)
        scale = jnp.where(kind == 0, DK ** -0.5, 1.0)
        for h in range(HEADS_A):
            sh = s[:, _head_cols(h)]
            r = lax.rsqrt(jnp.sum(sh * sh, axis=-1, keepdims=True) + EPS)
            y_ref[h] = sh * jnp.where(kind < 2, r * scale, 1.0)

    return _pc(body, name=name, out_shape=(_sds((T, 3 * D), F32), _sds((N_QKV_BLK, T, 128), F32)),
               grid=(3, T // tm),
               in_specs=[pl.BlockSpec((tm, D), lambda kd, t: (t, kd)),
                         pl.BlockSpec((8, D), lambda kd, t: (jnp.maximum(t * (tm // 8) - 1, 0), kd)),
                         pl.BlockSpec((4, D), lambda kd, t: (0, kd))],
               out_specs=(pl.BlockSpec((tm, D), lambda kd, t: (t, kd)),
                          pl.BlockSpec((HEADS_A, tm, 128), lambda kd, t: (kd, t, 0))),
               sem=("parallel", "parallel"))(proj, proj, wconv)


def gdn_conv_bwd(dqkv, c, proj, wconv, name):
    T = c.shape[0]
    tm = _tile(T, 512)
    n_t = T // tm

    def body(dy_ref, dyn_ref, c_ref, cn_ref, x_ref, xp_ref, w_ref, dx_ref, dw_ref):
        kind, t = pl.program_id(0), pl.program_id(1)
        scale = jnp.where(kind == 0, DK ** -0.5, 1.0)

        def act_bwd(dy, cv):
            sg = _sigmoid(cv)
            s = cv * sg
            parts = []
            for h in range(HEADS_A):
                sh, dyh = s[:, _head_cols(h)], dy[h]
                r = lax.rsqrt(jnp.sum(sh * sh, axis=-1, keepdims=True) + EPS)
                ds_norm = scale * r * (dyh - (r * r) * sh * jnp.sum(dyh * sh, axis=-1, keepdims=True))
                parts.append(jnp.where(kind < 2, ds_norm, dyh))
            return jnp.concatenate(parts, axis=1) * (sg * (1.0 + cv * (1.0 - sg)))

        w = w_ref[...]
        dcur = act_bwd(dy_ref[...], c_ref[...])
        dnext = jnp.where(t < n_t - 1, act_bwd(dyn_ref[...], cn_ref[...]), 0.0)
        dcat = jnp.concatenate([dcur, dnext], axis=0)
        dx = dcur * w[3:4]
        for k in range(3):
            dx = dx + pltpu.roll(dcat, tm + 8 - (3 - k), 0)[:tm] * w[k:k + 1]
        dx_ref[...] = dx.astype(BF16)
        xprev = jnp.where(t > 0, xp_ref[...], 0.0)
        xcat = jnp.concatenate([xprev, x_ref[...]], axis=0)
        rows = [jnp.sum(dcur * pltpu.roll(xcat, 3 - k, 0)[8:], axis=0, keepdims=True) for k in range(3)]
        rows.append(jnp.sum(dcur * xcat[8:], axis=0, keepdims=True))

        @pl.when(t == 0)
        def _():
            dw_ref[...] = jnp.zeros_like(dw_ref)
        dw_ref[...] += jnp.concatenate(rows, axis=0)

    def nxt(t):
        return jnp.minimum((t + 1) * (tm // 8), T // 8 - 1)

    cur = pl.BlockSpec((tm, D), lambda kd, t: (t, kd))
    return _pc(body, name=name, out_shape=(_sds((T, 3 * D), BF16), _sds((4, 3 * D), F32)), grid=(3, n_t),
               in_specs=[pl.BlockSpec((HEADS_A, tm, 128), lambda kd, t: (kd, t, 0)),
                         pl.BlockSpec((HEADS_A, 8, 128), lambda kd, t: (kd, nxt(t), 0)),
                         cur, pl.BlockSpec((8, D), lambda kd, t: (nxt(t), kd)),
                         cur, pl.BlockSpec((8, D), lambda kd, t: (jnp.maximum(t * (tm // 8) - 1, 0), kd)),
                         pl.BlockSpec((4, D), lambda kd, t: (0, kd))],
               out_specs=(cur, pl.BlockSpec((4, D), lambda kd, t: (0, kd))),
               sem=("parallel", "arbitrary"))(dqkv, dqkv, c, c, proj, proj, wconv)


def _chunk_masks(n):
    ri = lax.broadcasted_iota(jnp.int32, (n, n), 0)
    ci = lax.broadcasted_iota(jnp.int32, (n, n), 1)
    same = (ri // CHUNK) == (ci // CHUNK)
    return same & (ri >= ci), same & (ri <= ci)


def gdn_gate_fwd(proj, al, dtb, name):
    T = proj.shape[0]
    tg = _tile(T, PREP_T)

    def body(ba_ref, al_ref, dtb_ref, o_ref):
        x = ba_ref[...]
        lane = lax.broadcasted_iota(jnp.int32, x.shape, 1)
        is_a = (lane >= HEADS_A) & (lane < 2 * HEADS_A)
        g = jnp.where(is_a, -jnp.exp(al_ref[...]) * _softplus(x + dtb_ref[...]), 0.0)
        lower, _ = _chunk_masks(tg)
        gc = _dot(lower.astype(F32), g, precision=HI)
        o_ref[...] = jnp.where(lane < HEADS_A, _sigmoid(x), gc)

    vec = pl.BlockSpec((1, 128), lambda i: (0, 0))
    return _pc(body, name=name, out_shape=_sds((T, 128), F32), grid=(T // tg,),
               in_specs=[pl.BlockSpec((tg, 128), lambda i: (i, BA_BLK)), vec, vec],
               out_specs=pl.BlockSpec((tg, 128), lambda i: (i, 0)), sem=("parallel",))(proj, al, dtb)


def gdn_gate_bwd(proj, al, dtb, dgb, name):
    T = proj.shape[0]
    tg = _tile(T, PREP_T)

    def body(ba_ref, al_ref, dtb_ref, dgb_ref, dba_ref, dal_ref, ddt_ref):
        x, d = ba_ref[...], dgb_ref[...]
        lane = lax.broadcasted_iota(jnp.int32, x.shape, 1)
        is_b = lane < HEADS_A
        is_a = (lane >= HEADS_A) & (lane < 2 * HEADS_A)
        beta = _sigmoid(x)
        e_a = jnp.exp(al_ref[...])
        z = x + dtb_ref[...]
        g = jnp.where(is_a, -e_a * _softplus(z), 0.0)
        _, upper = _chunk_masks(tg)
        dg = _dot(upper.astype(F32), jnp.where(is_a, d, 0.0), precision=HI)
        da = jnp.where(is_a, dg * (-e_a) * _sigmoid(z), 0.0)
        db = jnp.where(is_b, d * beta * (1.0 - beta), 0.0)
        dba_ref[...] = (da + db).astype(BF16)

        @pl.when(pl.program_id(0) == 0)
        def _():
            dal_ref[...] = jnp.zeros_like(dal_ref)
            ddt_ref[...] = jnp.zeros_like(ddt_ref)
        dal_ref[...] += jnp.sum(dg * g, axis=0, keepdims=True)
        ddt_ref[...] += jnp.sum(da, axis=0, keepdims=True)

    vec = pl.BlockSpec((1, 128), lambda i: (0, 0))
    blk = pl.BlockSpec((tg, 128), lambda i: (i, 0))
    return _pc(body, name=name, out_shape=(_sds((T, 128), BF16), _sds((1, 128), F32), _sds((1, 128), F32)),
               grid=(T // tg,), in_specs=[pl.BlockSpec((tg, 128), lambda i: (i, BA_BLK)), vec, vec, blk],
               out_specs=(blk, vec, vec), sem=("arbitrary",))(proj, al, dtb, dgb)


def _bmm(a, b, dims, precision=None):
    return lax.dot_general(a, b, dims, preferred_element_type=F32, precision=precision)


B_NN = (((2,), (1,)), ((0,), (0,)))
B_NT = (((2,), (2,)), ((0,), (0,)))


def _select_lane(x, lane_index):
    lane = lax.broadcasted_iota(jnp.int32, x.shape, x.ndim - 1)
    return jnp.sum(jnp.where(lane == lane_index, x, 0.0), axis=-1, keepdims=True)


B_TN = (((1,), (1,)), ((0,), (0,)))


def _bmm_split(a, b, dims):
    ah, bh = a.astype(BF16), b.astype(BF16)
    al, bl = (a - ah.astype(F32)).astype(BF16), (b - bh.astype(F32)).astype(BF16)
    return _bmm(ah, bh, dims) + (_bmm(ah, bl, dims) + _bmm(al, bh, dims))


@jax.custom_vjp
def _bmm_f32(a, b):
    return _bmm_split(a, b, B_NN)


def _bmm_f32_fwd(a, b):
    return _bmm_split(a, b, B_NN), (a, b)


def _bmm_f32_bwd(res, dc):
    a, b = res
    return _bmm_split(dc, b, B_NT), _bmm_split(a, dc, B_TN)


_bmm_f32.defvjp(_bmm_f32_fwd, _bmm_f32_bwd)


def _tri_inverse(lmat):
    ri = lax.broadcasted_iota(jnp.int32, lmat.shape, 1)
    ci = lax.broadcasted_iota(jnp.int32, lmat.shape, 2)
    inv = jnp.where(ri == ci, 1.0, 0.0) - lmat
    power = lmat
    for _ in range(5):
        power = _bmm_split(power, power, B_NN)
        inv = inv + _bmm_split(inv, power, B_NN)
    return inv


def _stored_inverse(x):
    @jax.custom_vjp
    def inverse(lmat):
        return x

    def fwd(lmat):
        return x, None

    def bwd(_, dx):
        return (-_bmm_split(_bmm_split(x, dx, B_TN), x, B_NT),)

    inverse.defvjp(fwd, bwd)
    return inverse


def _gdn_prep(q, k, v, gb, h, inverse):
    nb = q.shape[0]
    beta = _select_lane(gb, h)
    gc = _select_lane(gb, HEADS_A + h)
    ri = lax.broadcasted_iota(jnp.int32, (nb, CHUNK, CHUNK), 1)
    ci = lax.broadcasted_iota(jnp.int32, (nb, CHUNK, CHUNK), 2)
    lower, strict, eye = ri >= ci, ri > ci, ri == ci
    gcol = jnp.broadcast_to(gc, (nb, CHUNK, CHUNK))
    grow = _bmm_f32(jnp.ones((nb, CHUNK, CHUNK), F32), jnp.where(eye, gcol, 0.0))
    decay = jnp.where(lower, jnp.exp(jnp.where(lower, gcol - grow, 0.0)), 0.0)
    kb = k * beta
    kbf = k.astype(BF16)
    inv = inverse(jnp.where(strict, _bmm(kb.astype(BF16), kbf, B_NT) * decay, 0.0))
    eg = jnp.exp(gc)
    sol = _bmm_f32(inv, jnp.concatenate([v * beta, kb * eg], axis=-1))
    aqk = _bmm(q.astype(BF16), kbf, B_NT) * decay
    g_last = gc[:, CHUNK - 1:CHUNK, :]
    gl = jnp.broadcast_to(jnp.exp(g_last), (nb, 1, 128))
    return (sol[..., :DK], sol[..., DK:], q * eg, k * jnp.exp(g_last - gc), aqk, gl), inv


def gdn_prep_fwd(qkv, gb, name):
    T = qkv.shape[1]
    tp = _tile(T, PREP_T)
    nb = tp // CHUNK

    def body(q_ref, k_ref, v_ref, gb_ref, u_ref, w_ref, qd_ref, kd_ref, a_ref, gl_ref, inv_ref):
        h = pl.program_id(1)
        shp = (nb, CHUNK, 128)
        (u, w, qd, kd, aqk, gl), inv = _gdn_prep(q_ref[0].reshape(shp), k_ref[0].reshape(shp), v_ref[0].reshape(shp),
                                                 gb_ref[...].reshape(shp), h, _tri_inverse)
        u_ref[0] = u.reshape(tp, 128)
        w_ref[0] = w.reshape(tp, 128)
        qd_ref[0] = qd.reshape(tp, 128)
        kd_ref[0] = kd.reshape(tp, 128)
        a_ref[0] = aqk.reshape(tp, CHUNK)
        gl_ref[0] = gl.reshape(nb, 1, 128)
        inv_ref[0] = inv.reshape(tp, CHUNK)

    def head(off):
        return pl.BlockSpec((1, tp, 128), lambda n, h: (h + off, n, 0))

    per_head = _sds((HEADS_A, T, 128), F32)
    narrow = pl.BlockSpec((1, tp, CHUNK), lambda n, h: (h, n, 0))
    return _pc(body, name=name,
               out_shape=(per_head, per_head, per_head, per_head, _sds((HEADS_A, T, CHUNK), F32),
                          _sds((HEADS_A, T // CHUNK, 1, 128), F32), _sds((HEADS_A, T, CHUNK), F32)),
               grid=(T // tp, HEADS_A),
               in_specs=[head(0), head(HEADS_A), head(2 * HEADS_A), pl.BlockSpec((tp, 128), lambda n, h: (n, 0))],
               out_specs=(head(0), head(0), head(0), head(0), narrow,
                          pl.BlockSpec((1, nb, 1, 128), lambda n, h: (h, n, 0, 0)), narrow),
               sem=("parallel", "parallel"))(qkv, qkv, qkv, gb)


def gdn_prep_bwd(qkv, gb, inv, du, dw, dqd, dkd, da, dgl, name):
    T = qkv.shape[1]
    tp = _tile(T, PREP_T)
    nb = tp // CHUNK

    def body(q_ref, k_ref, v_ref, gb_ref, inv_ref, du_ref, dw_ref, dqd_ref, dkd_ref, da_ref, dgl_ref, dqkv_ref, dgb_ref):
        h = pl.program_id(1)
        shp = (nb, CHUNK, 128)
        stored = _stored_inverse(inv_ref[0].reshape(nb, CHUNK, CHUNK))
        _, vjp = jax.vjp(lambda q, k, v, gb: _gdn_prep(q, k, v, gb, h, stored)[0], q_ref[0].reshape(shp),
                         k_ref[0].reshape(shp), v_ref[0].reshape(shp), gb_ref[...].reshape(shp))
        dq, dk, dv, dgb = vjp((du_ref[0].reshape(shp), dw_ref[0].reshape(shp), dqd_ref[0].reshape(shp),
                               dkd_ref[0].reshape(shp), da_ref[0].reshape(nb, CHUNK, CHUNK), dgl_ref[0].reshape(nb, 1, 128)))
        dqkv_ref[h] = dq.reshape(tp, 128)
        dqkv_ref[HEADS_A + h] = dk.reshape(tp, 128)
        dqkv_ref[2 * HEADS_A + h] = dv.reshape(tp, 128)

        @pl.when(h == 0)
        def _():
            dgb_ref[...] = jnp.zeros_like(dgb_ref)
        dgb_ref[...] += dgb.reshape(tp, 128)

    def head(off):
        return pl.BlockSpec((1, tp, 128), lambda n, h: (h + off, n, 0))

    narrow = pl.BlockSpec((1, tp, CHUNK), lambda n, h: (h, n, 0))
    return _pc(body, name=name, out_shape=(_sds((N_QKV_BLK, T, 128), F32), _sds((T, 128), F32)),
               grid=(T // tp, HEADS_A),
               in_specs=[head(0), head(HEADS_A), head(2 * HEADS_A), pl.BlockSpec((tp, 128), lambda n, h: (n, 0)), narrow,
                         head(0), head(0), head(0), head(0), narrow,
                         pl.BlockSpec((1, nb, 1, 128), lambda n, h: (h, n, 0, 0))],
               out_specs=(pl.BlockSpec((N_QKV_BLK, tp, 128), lambda n, h: (0, n, 0)),
                          pl.BlockSpec((tp, 128), lambda n, h: (n, 0))),
               sem=("parallel", "arbitrary"))(qkv, qkv, qkv, gb, inv, du, dw, dqd, dkd, da, dgl)


def gdn_scan_fwd(u, w, qd, kd, aqk, gl, name):
    T = u.shape[1]
    n_chunks = T // CHUNK

    def body(u_ref, w_ref, qd_ref, kd_ref, a_ref, gl_ref, o_ref, sin_ref, state):
        @pl.when(pl.program_id(0) == 0)
        def _():
            state[...] = jnp.zeros_like(state)
        s = state[...]
        sin_ref[0] = s
        sb = s.astype(BF16)
        both = _bmm(jnp.concatenate([w_ref[...], qd_ref[...]], axis=1).astype(BF16), sb, B_NN)
        vn = (u_ref[...] - both[:, :CHUNK]).astype(BF16)
        o_ref[...] = both[:, CHUNK:] + _bmm(a_ref[...].astype(BF16), vn, B_NN)
        state[...] = s * gl_ref[:, 0] + _bmm(kd_ref[...].astype(BF16), vn, B_TN)

    blk = pl.BlockSpec((HEADS_A, CHUNK, 128), lambda n: (0, n, 0))
    return _pc(body, name=name,
               out_shape=(_sds((HEADS_A, T, 128), F32), _sds((n_chunks, HEADS_A, DK, 128), F32)), grid=(n_chunks,),
               in_specs=[blk, blk, blk, blk, pl.BlockSpec((HEADS_A, CHUNK, CHUNK), lambda n: (0, n, 0)),
                         pl.BlockSpec((HEADS_A, 1, 1, 128), lambda n: (0, n, 0, 0))],
               out_specs=(blk, pl.BlockSpec((1, HEADS_A, DK, 128), lambda n: (n, 0, 0, 0))),
               scratch=[pltpu.VMEM((HEADS_A, DK, 128), F32)], sem=("arbitrary",))(u, w, qd, kd, aqk, gl)


def gdn_scan_bwd(u, w, qd, kd, aqk, gl, sin, do, name):
    T = u.shape[1]
    n_chunks = T // CHUNK

    def body(u_ref, w_ref, qd_ref, kd_ref, a_ref, gl_ref, sin_ref, do_ref,
             du_ref, dw_ref, dqd_ref, dkd_ref, da_ref, dgl_ref, dstate):
        @pl.when(pl.program_id(0) == 0)
        def _():
            dstate[...] = jnp.zeros_like(dstate)
        lane0 = lax.broadcasted_iota(jnp.int32, (HEADS_A, 1, 128), 2) == 0
        s = sin_ref[0]
        sb = s.astype(BF16)
        wb, qdb, kdb = w_ref[...].astype(BF16), qd_ref[...].astype(BF16), kd_ref[...].astype(BF16)
        ab, dob = a_ref[...].astype(BF16), do_ref[...].astype(BF16)
        vn = (u_ref[...] - _bmm(wb, sb, B_NN)).astype(BF16)
        ds_out = dstate[...]
        dsb = ds_out.astype(BF16)
        dqd_ref[...] = _bmm(dob, sb, B_NT)
        da_ref[...] = _bmm(dob, vn, B_NT)
        dv = _bmm(ab, dob, B_TN) + _bmm(kdb, dsb, B_NN)
        dkd_ref[...] = _bmm(vn, dsb, B_NT)
        dgl = jnp.sum(jnp.sum(ds_out * s, axis=2, keepdims=True), axis=1, keepdims=True)
        dgl_ref[:, 0] = jnp.where(lane0, dgl, 0.0)
        du_ref[...] = dv
        dvb = dv.astype(BF16)
        dw_ref[...] = -_bmm(dvb, sb, B_NT)
        dstate[...] = ds_out * gl_ref[:, 0] + _bmm(qdb, dob, B_TN) - _bmm(wb, dvb, B_TN)

    last = n_chunks - 1
    blk = pl.BlockSpec((HEADS_A, CHUNK, 128), lambda n: (0, last - n, 0))
    ablk = pl.BlockSpec((HEADS_A, CHUNK, CHUNK), lambda n: (0, last - n, 0))
    glblk = pl.BlockSpec((HEADS_A, 1, 1, 128), lambda n: (0, last - n, 0, 0))
    per_head = _sds((HEADS_A, T, 128), F32)
    return _pc(body, name=name,
               out_shape=(per_head, per_head, per_head, per_head, _sds((HEADS_A, T, CHUNK), F32),
                          _sds((HEADS_A, n_chunks, 1, 128), F32)), grid=(n_chunks,),
               in_specs=[blk, blk, blk, blk, ablk, glblk,
                         pl.BlockSpec((1, HEADS_A, DK, 128), lambda n: (last - n, 0, 0, 0)), blk],
               out_specs=(blk, blk, blk, blk, ablk, glblk),
               scratch=[pltpu.VMEM((HEADS_A, DK, 128), F32)], sem=("arbitrary",))(u, w, qd, kd, aqk, gl, sin, do)


def gdn_outnorm_fwd(o, proj, wn, name):
    T = o.shape[1]
    tm = _tile(T, 512)

    def body(o_ref, z_ref, wn_ref, y_ref):
        for h in range(HEADS_A):
            z = z_ref[:, 128 * h:128 * (h + 1)]
            y_ref[:, 128 * h:128 * (h + 1)] = (_rms_fwd(o_ref[h], wn_ref[...]) * (z * _sigmoid(z))).astype(BF16)

    return _pc(body, name=name, out_shape=_sds((T, D), BF16), grid=(T // tm,),
               in_specs=[pl.BlockSpec((HEADS_A, tm, 128), lambda i: (0, i, 0)),
                         pl.BlockSpec((tm, D), lambda i: (i, Z_BLK0 * 128 // D)), pl.BlockSpec((1, 128), lambda i: (0, 0))],
               out_specs=pl.BlockSpec((tm, D), lambda i: (i, 0)), sem=("parallel",))(o, proj, wn)


def gdn_outnorm_bwd(o, proj, wn, dy, name):
    T = o.shape[1]
    tm = _tile(T, 512)

    def body(o_ref, z_ref, wn_ref, dy_ref, do_ref, dz_ref, dwn_ref):
        wn = wn_ref[...]
        acc = jnp.zeros((1, 128), F32)
        for h in range(HEADS_A):
            cols = slice(128 * h, 128 * (h + 1))
            z, dyh, ov = z_ref[:, cols], dy_ref[:, cols], o_ref[h]
            sg = _sigmoid(z)
            do, dwn = _rms_bwd(ov, wn, dyh * (z * sg))
            do_ref[h] = do
            acc = acc + dwn
            dz_ref[:, cols] = (dyh * _rms_fwd(ov, wn) * (sg * (1.0 + z * (1.0 - sg)))).astype(BF16)

        @pl.when(pl.program_id(0) == 0)
        def _():
            dwn_ref[...] = jnp.zeros_like(dwn_ref)
        dwn_ref[...] += acc

    row = pl.BlockSpec((tm, D), lambda i: (i, 0))
    vec = pl.BlockSpec((1, 128), lambda i: (0, 0))
    hblk = pl.BlockSpec((HEADS_A, tm, 128), lambda i: (0, i, 0))
    return _pc(body, name=name, out_shape=(_sds((HEADS_A, T, 128), F32), _sds((T, D), BF16), _sds((1, 128), F32)),
               grid=(T // tm,),
               in_specs=[hblk, pl.BlockSpec((tm, D), lambda i: (i, Z_BLK0 * 128 // D)), vec, row],
               out_specs=(hblk, row, vec), sem=("arbitrary",))(o, proj, wn, dy)


def gdn_forward(x, nw, w_in, wconv, al, dtb, wn, w_out, tag, deps=()):
    h = rmsnorm_bf16(x, nw, f"{tag}_norm", deps)
    proj = mm_nn(h, w_in, f"{tag}_proj")
    c, qkv = gdn_conv_fwd(proj, wconv, f"{tag}_conv")
    gb = gdn_gate_fwd(proj, al, dtb, f"{tag}_gate")
    u, w, qd, kd, aqk, gl, inv = gdn_prep_fwd(qkv, gb, f"{tag}_prep")
    o, sin = gdn_scan_fwd(u, w, qd, kd, aqk, gl, f"{tag}_scan")
    on = gdn_outnorm_fwd(o, proj, wn, f"{tag}_outnorm")
    y = mm_nn(on, w_out, f"{tag}_out", residual=x)
    return y, (x, h, proj, c, qkv, gb, inv, (u, w, qd, kd, aqk, gl), sin, o, on)


def gdn_backward(dout, saved, nw, w_in, wconv, al, dtb, wn, w_out, tag):
    x, h, proj, c, qkv, gb, inv, prep, sin, o, on = saved
    d_on = mm_nt(dout, w_out, f"{tag}_out_bwd")
    dw_out = mm_tn(on, dout, f"{tag}_out_wgrad")
    do, dz, dwn = gdn_outnorm_bwd(o, proj, wn, d_on, f"{tag}_outnorm_bwd")
    du, dw, dqd, dkd, da, dgl = gdn_scan_bwd(*prep, sin, do, f"{tag}_scan_bwd")
    dqkv, dgb = gdn_prep_bwd(qkv, gb, inv, du, dw, dqd, dkd, da, dgl, f"{tag}_prep_bwd")
    dba, dal, ddt = gdn_gate_bwd(proj, al, dtb, dgb, f"{tag}_gate_bwd")
    dpre, dwconv = gdn_conv_bwd(dqkv, c, proj, wconv, f"{tag}_conv_bwd")
    dproj = jnp.concatenate([dpre, dz, dba], axis=1)
    dw_in = mm_tn(h, dproj, f"{tag}_proj_wgrad")
    dh = mm_nt(dproj, w_in, f"{tag}_proj_bwd")
    dx, dnw = rmsnorm_bwd_add(x, nw, dh, dout, f"{tag}_norm_bwd")
    return dx, dnw, dw_in, dwconv, dal, ddt, dwn, dw_out


N_KV, GROUP = 4, 4
KV_COLS = 2 * N_KV * B_HD
B_COLS = D + KV_COLS


@jax.custom_vjp
def _swap_lane_halves(x):
    return pltpu.roll(x, 64, 1)


_swap_lane_halves.defvjp(lambda x: (pltpu.roll(x, 64, 1), None), lambda _, g: (pltpu.roll(g, 64, 1),))


def _swa_block(q, kp, kc, vp, vc, sk, first):
    rows = GROUP * B_BLK
    qi = lax.broadcasted_iota(jnp.int32, (N_KV, rows, B_BLK), 1) % B_BLK
    kj = lax.broadcasted_iota(jnp.int32, (N_KV, rows, B_BLK), 2)
    from_cur = kj <= qi

    def per_kv(cur, prev):
        return jnp.stack([jnp.concatenate([cur[:, j * B_HD:(j + 1) * B_HD], prev[:, j * B_HD:(j + 1) * B_HD]], axis=0)
                          for j in range(N_KV)]).astype(BF16)

    qs = jnp.stack([jnp.concatenate([q[:, hq * B_HD:(hq + 1) * B_HD] for hq in range(GROUP * j, GROUP * (j + 1))], axis=0)
                    for j in range(N_KV)]).astype(BF16)
    sink = jnp.stack([jnp.concatenate([jnp.broadcast_to(sk[:, hq:hq + 1], (B_BLK, 1))
                                       for hq in range(GROUP * j, GROUP * (j + 1))], axis=0) for j in range(N_KV)])
    both = _bmm(qs, per_kv(kc, kp), B_NT)
    s = jnp.where(from_cur, both[..., :B_BLK], jnp.where(first, -1e30, both[..., B_BLK:])) * (B_HD ** -0.5)
    m = lax.stop_gradient(jnp.maximum(jnp.max(s, axis=-1, keepdims=True), sink))
    e = jnp.exp(s - m)
    p = e * (1.0 / (jnp.sum(e, axis=-1, keepdims=True) + jnp.exp(sink - m)))
    p_both = jnp.concatenate([jnp.where(from_cur, p, 0.0), jnp.where(from_cur, 0.0, p)], axis=-1).astype(BF16)
    o = _bmm(p_both, per_kv(vc, vp), B_NN)
    return jnp.concatenate([o[j, g * B_BLK:(g + 1) * B_BLK] for j in range(N_KV) for g in range(GROUP)], axis=1)


def swa_core_fwd(proj, sk, name):
    T = proj.shape[0]
    half = N_KV * B_HD

    def body(q_ref, kvc_ref, kvp_ref, sk_ref, o_ref):
        kvc, kvp = kvc_ref[...], kvp_ref[...]
        o_ref[...] = _swa_block(q_ref[...], kvp[:, :half], kvc[:, :half], kvp[:, half:], kvc[:, half:], sk_ref[...],
                                pl.program_id(0) == 0).astype(BF16)

    return _pc(body, name=name, out_shape=_sds((T, D), BF16), grid=(T // B_BLK,),
               in_specs=[pl.BlockSpec((B_BLK, D), lambda n: (n, 0)),
                         pl.BlockSpec((B_BLK, KV_COLS), lambda n: (n, D // KV_COLS)),
                         pl.BlockSpec((B_BLK, KV_COLS), lambda n: (jnp.maximum(n - 1, 0), D // KV_COLS)),
                         pl.BlockSpec((1, 128), lambda n: (0, 0))],
               out_specs=pl.BlockSpec((B_BLK, D), lambda n: (n, 0)), sem=("parallel",))(proj, proj, proj, sk)


def swa_core_bwd(proj, sk, do, name):
    T = proj.shape[0]
    last = T // B_BLK - 1
    half = N_KV * B_HD

    def body(q_ref, kvc_ref, kvp_ref, sk_ref, do_ref, dproj_ref, dbias_ref, dsk_ref, carry):
        step = pl.program_id(0)
        first = step == last

        @pl.when(step == 0)
        def _():
            carry[...] = jnp.zeros_like(carry)
            dbias_ref[...] = jnp.zeros_like(dbias_ref)
            dsk_ref[...] = jnp.zeros_like(dsk_ref)
        kvc, kvp = kvc_ref[...], kvp_ref[...]
        _, vjp = jax.vjp(functools.partial(_swa_block, first=first), q_ref[...], kvp[:, :half], kvc[:, :half],
                         kvp[:, half:], kvc[:, half:], sk_ref[...])
        dq, dkp, dkc, dvp, dvc, dsk = vjp(do_ref[...])
        dkv = jnp.concatenate([dkc, dvc], axis=1) + carry[...]
        carry[...] = jnp.concatenate([dkp, dvp], axis=1)
        row = jnp.concatenate([dq, dkv], axis=1)
        dproj_ref[...] = row.astype(BF16)
        dbias_ref[...] += jnp.sum(row, axis=0, keepdims=True)
        dsk_ref[...] += dsk

    return _pc(body, name=name, out_shape=(_sds((T, B_COLS), BF16), _sds((1, B_COLS), F32), _sds((1, 128), F32)),
               grid=(T // B_BLK,),
               in_specs=[pl.BlockSpec((B_BLK, D), lambda n: (last - n, 0)),
                         pl.BlockSpec((B_BLK, KV_COLS), lambda n: (last - n, D // KV_COLS)),
                         pl.BlockSpec((B_BLK, KV_COLS), lambda n: (jnp.maximum(last - n - 1, 0), D // KV_COLS)),
                         pl.BlockSpec((1, 128), lambda n: (0, 0)), pl.BlockSpec((B_BLK, D), lambda n: (last - n, 0))],
               out_specs=(pl.BlockSpec((B_BLK, B_COLS), lambda n: (last - n, 0)),
                          pl.BlockSpec((1, B_COLS), lambda n: (0, 0)), pl.BlockSpec((1, 128), lambda n: (0, 0))),
               scratch=[pltpu.VMEM((B_BLK, KV_COLS), F32)], sem=("arbitrary",))(proj, proj, proj, sk, do)


def col_sum(a, name):
    T, N = a.shape
    tm = _tile(T, 1024)

    def body(a_ref, o_ref):
        @pl.when(pl.program_id(0) == 0)
        def _():
            o_ref[...] = jnp.zeros_like(o_ref)
        o_ref[...] += jnp.sum(a_ref[...].astype(F32), axis=0, keepdims=True)

    return _pc(body, name=name, out_shape=_sds((1, N), F32), grid=(T // tm,),
               in_specs=[pl.BlockSpec((tm, N), lambda i: (i, 0))], out_specs=pl.BlockSpec((1, N), lambda i: (0, 0)),
               sem=("arbitrary",))(a)


def swa_forward(x, nw, w_in, b_in, sk, w_out, b_out, tag):
    h = rmsnorm_bf16(x, nw, f"{tag}_norm")
    proj = mm_nn(h, w_in, f"{tag}_proj", bias=b_in)
    o = swa_core_fwd(proj, sk, f"{tag}_core")
    y = mm_nn(o, w_out, f"{tag}_out", bias=b_out, residual=x)
    return y, (x, h, proj, o)


def swa_backward(dout, saved, nw, w_in, b_in, sk, w_out, b_out, tag):
    x, h, proj, o = saved
    do = mm_nt(dout, w_out, f"{tag}_out_bwd")
    dw_out = mm_tn(o, dout, f"{tag}_out_wgrad")
    db_out = col_sum(dout, f"{tag}_out_bias_grad")
    dproj, db_in, dsk = swa_core_bwd(proj, sk, do, f"{tag}_core_bwd")
    dw_in = mm_tn(h, dproj, f"{tag}_proj_wgrad")
    dh = mm_nt(dproj, w_in, f"{tag}_proj_bwd")
    dx, dnw = rmsnorm_bwd_add(x, nw, dh, dout, f"{tag}_norm_bwd")
    return dx, dnw, dw_in, db_in, dsk, dw_out, db_out


MESH = pl.DeviceIdType.MESH


def _position():
    return lax.axis_index("x"), lax.axis_index("y"), lax.axis_index("c")


def _slot(x, y, c):
    return 4 * x + 2 * y + c


def _peer(x, y, c, k):
    return (1 - x if k & 4 else x, 1 - y if k & 2 else y, 1 - c if k & 1 else c)


HBM_SPEC = pl.BlockSpec(memory_space=pltpu.HBM)
SEM_SPEC = pl.BlockSpec(memory_space=pltpu.SEMAPHORE)
DEP_SPEC = pl.BlockSpec(memory_space=pl.ANY)
SIDE_EFFECT = pltpu.SideEffectType.DATAFLOW_SIDE_EFFECTING
N_PEERS = N_DEV - 1


def _push_copies(srcs, lands, send_sems, recv_sems, scatter):
    x, y, c = _position()
    me = _slot(x, y, c)
    copies = []
    for k in (1, 2, 4, 3, 5, 6, 7):
        peer = _peer(x, y, c, k)
        for a in range(len(srcs)):
            copies.append(pltpu.make_async_remote_copy(
                src_ref=srcs[a].at[_slot(*peer)] if scatter else srcs[a], dst_ref=lands[a].at[me],
                send_sem=send_sems.at[N_PEERS * a + k - 1], recv_sem=recv_sems.at[N_PEERS * a + k - 1],
                device_id=peer, device_id_type=MESH))
    return copies


def push_start(srcs, lands, name, scatter, deps=()):
    n = len(srcs)
    first_out = 2 * n + len(deps)

    def body(*refs):
        for cp in _push_copies(refs[:n], refs[n:2 * n], refs[first_out], refs[first_out + 1], scatter):
            cp.start()
        refs[-1][...] = jnp.zeros_like(refs[-1])

    passed = [pltpu.HBM(t.shape, t.dtype) for t in list(srcs) + list(lands)]
    res = pl.pallas_call(
        body, name=name,
        out_shape=(pltpu.SemaphoreType.DMA((N_PEERS * n,)), pltpu.SemaphoreType.DMA((N_PEERS * n,)), *passed, _sds((8, 128), F32)),
        in_specs=[HBM_SPEC] * (2 * n) + [DEP_SPEC] * len(deps),
        out_specs=(SEM_SPEC, SEM_SPEC, *([HBM_SPEC] * (2 * n)), pl.BlockSpec(memory_space=pltpu.VMEM)),
        input_output_aliases={i: 2 + i for i in range(2 * n)},
        compiler_params=pltpu.CompilerParams(has_side_effects=SIDE_EFFECT),
    )(*[pltpu.with_memory_space_constraint(t, pltpu.HBM) for t in list(srcs) + list(lands)], *deps)
    return (res[0], res[1], list(res[2:2 + n]), list(res[2 + n:2 + 2 * n])), res[-1]


def push_wait(handles, after, name, scatter):
    send_sems, recv_sems, srcs, lands = handles
    n = len(srcs)
    after = tuple(after) if isinstance(after, (tuple, list)) else (after,)

    def body(*refs):
        for cp in _push_copies(refs[:n], refs[n:2 * n], refs[2 * n], refs[2 * n + 1], scatter):
            cp.wait_send()
            cp.wait_recv()

    res = pl.pallas_call(
        body, name=name, out_shape=tuple(pltpu.HBM(t.shape, t.dtype) for t in srcs + lands),
        in_specs=[HBM_SPEC] * (2 * n) + [SEM_SPEC, SEM_SPEC] + [DEP_SPEC] * len(after), out_specs=tuple([HBM_SPEC] * (2 * n)),
        input_output_aliases={i: i for i in range(2 * n)},
        compiler_params=pltpu.CompilerParams(has_side_effects=SIDE_EFFECT),
    )(*srcs, *lands, send_sems, recv_sems, *after)
    return list(res[n:])


def gather_start(shards, name, deps=()):
    me = _slot(*_position())
    lands = [lax.dynamic_update_slice(lax.empty((N_DEV,) + t.shape, t.dtype), t[None], (me,) + (0,) * t.ndim) for t in shards]
    return push_start(shards, lands, name, scatter=False, deps=deps)


def exchange_start(parts, name):
    me = _slot(*_position())
    lands = [lax.dynamic_update_slice(lax.empty(t.shape, t.dtype), lax.dynamic_index_in_dim(t, me, 0, keepdims=True),
                                      (me,) + (0,) * (t.ndim - 1)) for t in parts]
    return push_start(parts, lands, name, scatter=True)


def _row_tile(rows, cols):
    best = rows
    for t in range(16, rows, 16):
        if rows % t == 0 and t * cols * 4 <= (1 << 20):
            best = t
    return best


def adam_update(parts, w, m, v, name):
    n_layers = len(parts)
    P, R, C = parts[0].shape
    tr = _row_tile(R, C)
    n_t = R // tr

    def body(*refs):
        p_refs = refs[:n_layers]
        w_ref, m_ref, v_ref, g_ref, d_ref, nm_ref, nv_ref = refs[n_layers:]
        for layer in range(n_layers):
            @pl.when(pl.program_id(0) == layer)
            def _(p_ref=p_refs[layer]):
                g = p_ref[0].astype(F32)
                for s in range(1, P):
                    g = g + p_ref[s].astype(F32)
                new_m = ADAM_B1 * m_ref[0] + (1.0 - ADAM_B1) * g
                new_v = ADAM_B2 * v_ref[0] + (1.0 - ADAM_B2) * (g * g)
                m_hat = new_m / (1.0 - ADAM_B1 ** ADAM_STEP)
                v_hat = new_v / (1.0 - ADAM_B2 ** ADAM_STEP)
                g_ref[0] = g
                d_ref[0] = -ADAM_LR * (m_hat / (jnp.sqrt(v_hat) + ADAM_EPS) + ADAM_WD * w_ref[0])
                nm_ref[0] = new_m
                nv_ref[0] = new_v

    def part_spec(layer):
        return pl.BlockSpec((P, tr, C), lambda l_, i: (0, jnp.where(l_ == layer, i, jnp.where(l_ < layer, 0, n_t - 1)), 0))

    blk = pl.BlockSpec((1, tr, C), lambda l_, i: (l_, i, 0))
    out = _sds((n_layers, R, C), F32)
    return _pc(body, name=name, out_shape=(out, out, out, out), grid=(n_layers, n_t),
               in_specs=[part_spec(layer) for layer in range(n_layers)] + [blk, blk, blk],
               out_specs=(blk, blk, blk, blk), sem=("arbitrary", "arbitrary"))(*parts, w, m, v)


WEIGHTS = ("ffn1_norm", "ffn1_w_gu", "ffn1_w_down", "mix_norm", "ffn2_norm", "ffn2_w_gu", "ffn2_w_down", "a_w_in",
           "a_w_conv", "a_A_log", "a_dt_bias", "a_out_norm", "a_w_out", "b_w_in", "b_b_in", "b_sinks", "b_w_out",
           "b_b_out", "final_norm")
SHARDED = ("ffn1_w_gu", "ffn1_w_down", "ffn2_w_gu", "ffn2_w_down", "a_w_in", "a_w_conv", "a_w_out", "b_w_in", "b_b_in",
           "b_w_out", "b_b_out")
MISC_LANES = dict(a_A_log=(0, 8), a_dt_bias=(8, 16), b_sinks=(16, 32), a_out_norm=(128, 256))
LOSS_LANE = 256


def _pack_small(t):
    misc = jnp.zeros((D,), F32)
    for key, (lo, hi) in MISC_LANES.items():
        misc = misc.at[lo:hi].set(t[key].reshape(-1))
    if "loss" in t:
        misc = misc.at[LOSS_LANE].set(t["loss"])
    return jnp.concatenate([t["ffn1_norm"], t["mix_norm"], t["ffn2_norm"], t["final_norm"].reshape(1, D), misc[None]], axis=0)


def _unpack_small(p, like):
    out = dict(ffn1_norm=p[0:2], mix_norm=p[2:4], ffn2_norm=p[4:6], final_norm=p[6])
    for key, (lo, hi) in MISC_LANES.items():
        out[key] = p[7, lo:hi].reshape(like[key].shape)
    return out


def kernel(x, ffn1_norm, ffn1_w_gu, ffn1_w_down, mix_norm, ffn2_norm, ffn2_w_gu, ffn2_w_down, a_w_in, a_w_conv, a_A_log, a_dt_bias, a_out_norm, a_w_out, b_w_in, b_b_in, b_sinks, b_w_out, b_b_out, final_norm, loss_target, m_ffn1_norm, m_ffn1_w_gu, m_ffn1_w_down, m_mix_norm, m_ffn2_norm, m_ffn2_w_gu, m_ffn2_w_down, m_a_w_in, m_a_w_conv, m_a_A_log, m_a_dt_bias, m_a_out_norm, m_a_w_out, m_b_w_in, m_b_b_in, m_b_sinks, m_b_w_out, m_b_b_out, m_final_norm, v_ffn1_norm, v_ffn1_w_gu, v_ffn1_w_down, v_mix_norm, v_ffn2_norm, v_ffn2_w_gu, v_ffn2_w_down, v_a_w_in, v_a_w_conv, v_a_A_log, v_a_dt_bias, v_a_out_norm, v_a_w_out, v_b_w_in, v_b_b_in, v_b_sinks, v_b_w_out, v_b_b_out, v_final_norm):
    w = dict(ffn1_norm=ffn1_norm, ffn1_w_gu=ffn1_w_gu, ffn1_w_down=ffn1_w_down, mix_norm=mix_norm, ffn2_norm=ffn2_norm, ffn2_w_gu=ffn2_w_gu, ffn2_w_down=ffn2_w_down, a_w_in=a_w_in, a_w_conv=a_w_conv, a_A_log=a_A_log, a_dt_bias=a_dt_bias, a_out_norm=a_out_norm, a_w_out=a_w_out, b_w_in=b_w_in, b_b_in=b_b_in, b_sinks=b_sinks, b_w_out=b_w_out, b_b_out=b_b_out, final_norm=final_norm)
    m = dict(ffn1_norm=m_ffn1_norm, ffn1_w_gu=m_ffn1_w_gu, ffn1_w_down=m_ffn1_w_down, mix_norm=m_mix_norm, ffn2_norm=m_ffn2_norm, ffn2_w_gu=m_ffn2_w_gu, ffn2_w_down=m_ffn2_w_down, a_w_in=m_a_w_in, a_w_conv=m_a_w_conv, a_A_log=m_a_A_log, a_dt_bias=m_a_dt_bias, a_out_norm=m_a_out_norm, a_w_out=m_a_w_out, b_w_in=m_b_w_in, b_b_in=m_b_b_in, b_sinks=m_b_sinks, b_w_out=m_b_w_out, b_b_out=m_b_b_out, final_norm=m_final_norm)
    v = dict(ffn1_norm=v_ffn1_norm, ffn1_w_gu=v_ffn1_w_gu, ffn1_w_down=v_ffn1_w_down, mix_norm=v_mix_norm, ffn2_norm=v_ffn2_norm, ffn2_w_gu=v_ffn2_w_gu, ffn2_w_down=v_ffn2_w_down, a_w_in=v_a_w_in, a_w_conv=v_a_w_conv, a_A_log=v_a_A_log, a_dt_bias=v_a_dt_bias, a_out_norm=v_a_out_norm, a_w_out=v_a_w_out, b_w_in=v_b_w_in, b_b_in=v_b_b_in, b_sinks=v_b_sinks, b_w_out=v_b_w_out, b_b_out=v_b_b_out, final_norm=v_final_norm)
    T = x.shape[1]
    x0, tgt = x.reshape(T, D), loss_target.reshape(T, D)

    def cast(t):
        return t.astype(BF16)

    h0, t0 = gather_start([cast(ffn1_w_gu[0])], "gather0_start")
    a_log_row = jnp.zeros((1, 128), F32).at[0, HEADS_A:2 * HEADS_A].set(a_A_log[0])
    dt_row = jnp.zeros((1, 128), F32).at[0, HEADS_A:2 * HEADS_A].set(a_dt_bias[0])
    sink_row = jnp.zeros((1, 128), F32).at[0, :b_sinks.shape[1]].set(b_sinks[0])
    a_in_cols = a_w_in.shape[-1] * N_DEV

    def down_blocks(t):
        return t.reshape(N_FB, FB, D)

    wgu, wdn, saved = {}, {}, []
    xn = rmsnorm_bf16(x0, ffn1_norm[0:1], "l0_ffn1_norm", (t0,))
    wgu["ffn1", 0] = push_wait(h0, xn, "gather0_wait", scatter=False)[0]
    h0d, t0d = gather_start([cast(ffn1_w_down[0])], "gather0d_start", deps=(wgu["ffn1", 0],))
    h1, t1 = gather_start([cast(a_w_in[0]), a_w_conv[0], cast(a_w_out[0])], "gather1_start", deps=(t0d,))
    gu = ffn_up(xn, wgu["ffn1", 0], "l0_ffn1_up", deps=(t0d, t1))
    wdn["ffn1", 0] = down_blocks(push_wait(h0d, gu, "gather0d_wait", scatter=False)[0])
    xs, s1 = ffn_down(gu, wdn["ffn1", 0], x0, "l0_ffn1_down"), (x0, xn, gu)
    got = push_wait(h1, xs, "gather1_wait", scatter=False)
    h1f, t1f = gather_start([cast(ffn2_w_gu[0]), cast(ffn2_w_down[0])], "gather1f_start", deps=(got[0],))
    g2 = [cast(ffn1_w_gu[1]), cast(ffn1_w_down[1]), cast(b_w_in[0]), b_b_in, cast(b_w_out[0]), b_b_out,
          cast(ffn2_w_gu[1]), cast(ffn2_w_down[1])]
    h2, t2 = gather_start(g2, "gather2_start", deps=(t1f,))
    a_in_full = jnp.pad(got[0].transpose(1, 0, 2).reshape(D, a_in_cols), ((0, 0), (0, A_COLS - a_in_cols)))
    gdn_args = (mix_norm[0:1], a_in_full, got[1].transpose(1, 0, 2).reshape(4, 3 * D), a_log_row, dt_row, a_out_norm,
                got[2].reshape(D, D))
    xs, sm = gdn_forward(xs, *gdn_args, "gdn", deps=(t1f, t2))
    got = push_wait(h1f, xs, "gather1f_wait", scatter=False)
    wgu["ffn2", 0], wdn["ffn2", 0] = got[0], down_blocks(got[1])
    xs, s2 = ffn_forward(xs, ffn2_norm[0:1], wgu["ffn2", 0], wdn["ffn2", 0], "l0_ffn2")
    saved.append((s1, sm, s2))
    got = push_wait(h2, xs, "gather2_wait", scatter=False)
    wgu["ffn1", 1], wdn["ffn1", 1] = got[0], down_blocks(got[1])
    swa_args = (mix_norm[1:2], got[2].transpose(1, 0, 2).reshape(D, B_COLS), got[3].reshape(1, B_COLS), sink_row,
                got[4].reshape(D, D), got[5].reshape(1, D))
    wgu["ffn2", 1], wdn["ffn2", 1] = got[6], down_blocks(got[7])
    xs, s1 = ffn_forward(xs, ffn1_norm[1:2], wgu["ffn1", 1], wdn["ffn1", 1], "l1_ffn1")
    xs, sm = swa_forward(xs, *swa_args, "swa")
    xs, s2 = ffn_forward(xs, ffn2_norm[1:2], wgu["ffn2", 1], wdn["ffn2", 1], "l1_ffn2")
    saved.append((s1, sm, s2))
    loss_row, dx, d_final_norm = final_loss(xs, final_norm.reshape(1, D), tgt, "final_loss")

    def down_slots(t):
        return cast(t.reshape(N_DEV, FB // 2, D))

    def col_slots(t, dtype=BF16):
        return t.reshape(t.shape[0], N_DEV, -1).transpose(1, 0, 2).astype(dtype)

    d_norm = {"ffn1_norm": [None, None], "mix_norm": [None, None], "ffn2_norm": [None, None]}
    s1, sm, s2 = saved[1]
    dx, d_norm["ffn2_norm"][1], d_gu, d_dn = ffn_backward(dx, s2, ffn2_norm[1:2], wgu["ffn2", 1], wdn["ffn2", 1], "l1_ffn2")
    sent1 = [cast(d_gu), down_slots(d_dn)]
    dx, d_norm["mix_norm"][1], d_b_in, d_b_bias_in, d_sinks, d_b_out, d_b_bias_out = swa_backward(dx, sm, *swa_args, "swa")
    sent1 += [col_slots(d_b_in), d_b_bias_in.reshape(N_DEV, 1, -1), cast(d_b_out.reshape(N_DEV, D // N_DEV, D)),
              d_b_bias_out.reshape(N_DEV, 1, -1)]
    dx, d_norm["ffn1_norm"][1], d_gu, d_dn = ffn_backward(dx, s1, ffn1_norm[1:2], wgu["ffn1", 1], wdn["ffn1", 1], "l1_ffn1")
    sent1 += [cast(d_gu), down_slots(d_dn)]
    x1, tx1 = exchange_start(sent1, "exchange1_start")

    s1, sm, s2 = saved[0]
    dx, d_norm["ffn2_norm"][0], d_gu, d_dn = ffn_backward(dx, s2, ffn2_norm[0:1], wgu["ffn2", 0], wdn["ffn2", 0], "l0_ffn2",
                                                           deps=(tx1,))
    sent2 = [cast(d_gu), down_slots(d_dn)]
    dx, d_norm["mix_norm"][0], d_a_in, d_a_conv, d_alog, d_dt, d_onorm, d_a_out = gdn_backward(dx, sm, *gdn_args, "gdn")
    sent2 += [col_slots(d_a_in[:, :a_in_cols]), col_slots(d_a_conv, F32), cast(d_a_out.reshape(N_DEV, D // N_DEV, D))]
    x2, tx2 = exchange_start(sent2, "exchange2_start")
    last = {}

    def send_last(d_gu, d_dn):
        last["handles"], token = exchange_start([cast(d_gu), down_slots(d_dn)], "exchange3_start")
        return (token,)

    dx, d_norm["ffn1_norm"][0], _, _ = ffn_backward(dx, s1, ffn1_norm[0:1], wgu["ffn1", 0], wdn["ffn1", 0], "l0_ffn1",
                                                    deps=(tx2,), on_grads=send_last)
    grad_x = dx.reshape(x.shape)
    r1 = push_wait(x1, dx, "exchange1_wait", scatter=True)
    r2 = push_wait(x2, dx, "exchange2_wait", scatter=True)
    received = dict(ffn2_w_gu=[r2[0], r1[0]], ffn2_w_down=[r2[1], r1[1]],
                    b_w_in=[r1[2]], b_b_in=[r1[3]], b_w_out=[r1[4]], b_b_out=[r1[5]],
                    a_w_in=[r2[2]], a_w_conv=[r2[3]], a_w_out=[r2[4]])

    grads, deltas, new_m, new_v = {}, {}, {}, {}

    def update(key):
        shape = w[key].shape
        cols = shape[-1]
        layers = lambda t: t.reshape(shape[0], -1, cols)
        out = adam_update([r.reshape(N_DEV, -1, cols) for r in received[key]], layers(w[key]), layers(m[key]), layers(v[key]),
                          f"adam_{key}")
        grads[key], deltas[key], new_m[key], new_v[key] = (t.reshape(shape) for t in out)

    for key in SHARDED:
        if key in received:
            update(key)
    done_first = [deltas[key] for key in received]

    small = dict(ffn1_norm=jnp.concatenate(d_norm["ffn1_norm"], axis=0), mix_norm=jnp.concatenate(d_norm["mix_norm"], axis=0),
                 ffn2_norm=jnp.concatenate(d_norm["ffn2_norm"], axis=0), final_norm=d_final_norm,
                 a_A_log=d_alog[0, HEADS_A:2 * HEADS_A], a_dt_bias=d_dt[0, HEADS_A:2 * HEADS_A],
                 b_sinks=d_sinks[0, :b_sinks.shape[1]], a_out_norm=d_onorm, loss=loss_row[0, 0])
    hs, ts = gather_start([_pack_small(small)], "gather_small_start")
    r3 = push_wait(last["handles"], done_first + [ts], "exchange3_wait", scatter=True)
    received.update(ffn1_w_gu=[r3[0], r1[6]], ffn1_w_down=[r3[1], r1[7]])
    update("ffn1_w_gu")
    update("ffn1_w_down")
    every = push_wait(hs, deltas["ffn1_w_down"], "gather_small_wait", scatter=False)[0]
    out = adam_update([every], _pack_small(w)[None], _pack_small(m)[None], _pack_small(v)[None], "adam_small")
    for dst, packed in zip((grads, deltas, new_m, new_v), out):
        dst.update(_unpack_small(packed[0], w))
    loss = out[0][0, 7, LOSS_LANE]

    return (loss, grad_x, *[grads[k_] for k_ in WEIGHTS], *[deltas[k_] for k_ in WEIGHTS],
            *[new_m[k_] for k_ in WEIGHTS], *[new_v[k_] for k_ in WEIGHTS])
```

```python
import functools

import jax
import jax.numpy as jnp
from jax import lax
from jax.experimental import pallas as pl
from jax.experimental.pallas import tpu as pltpu

F32, BF16 = jnp.float32, jnp.bfloat16
HI = lax.Precision.HIGHEST
EPS = 1e-6

N_DEV = 8
D = 1024
FB = 704
N_FB = 4
HEADS_A, DK = 8, 128
CHUNK = 64
PREP_T = 512
A_COLS = 4224
B_HD, B_BLK = 64, 128
VMEM_LIMIT_V7X = 60 * 1024 * 1024

ADAM_LR, ADAM_B1, ADAM_B2, ADAM_EPS, ADAM_WD, ADAM_STEP = 0.001, 0.9, 0.999, 1e-08, 0.01, 10

NT = (((1,), (1,)), ((), ()))
TN = (((0,), (0,)), ((), ()))


def _pc(body, *, name, out_shape, grid=(), in_specs=None, out_specs=None, scratch=(), sem=None, **kw):
    params = pltpu.CompilerParams(dimension_semantics=sem, vmem_limit_bytes=VMEM_LIMIT_V7X)
    return pl.pallas_call(body, name=name, out_shape=out_shape, grid=grid, in_specs=in_specs, out_specs=out_specs,
                          scratch_shapes=list(scratch), compiler_params=params, **kw)


def _sds(shape, dtype):
    return jax.ShapeDtypeStruct(tuple(shape), dtype)


def _dot(a, b, dims=None, precision=None):
    if dims is None:
        return jnp.dot(a, b, preferred_element_type=F32, precision=precision)
    return lax.dot_general(a, b, dims, preferred_element_type=F32, precision=precision)


def _sigmoid(x):
    return 1.0 / (1.0 + jnp.exp(-x))


def _softplus(x):
    return jnp.maximum(x, 0.0) + jnp.log(1.0 + jnp.exp(-jnp.abs(x)))


def _rms_fwd(x, w):
    r = lax.rsqrt(jnp.mean(x * x, axis=-1, keepdims=True) + EPS)
    return x * r * w


def _rms_bwd(x, w, dy):
    r = lax.rsqrt(jnp.mean(x * x, axis=-1, keepdims=True) + EPS)
    xh = x * r
    dxh = dy * w
    dx = r * (dxh - xh * jnp.mean(dxh * xh, axis=-1, keepdims=True))
    return dx, jnp.sum(dy * xh, axis=0, keepdims=True)


def _tile(n, want):
    t = min(n, want)
    assert n % t == 0, (n, want)
    return t


def rmsnorm_bf16(x, w, name, deps=()):
    T = x.shape[0]
    tm = _tile(T, 1024)

    def body(x_ref, w_ref, *rest):
        rest[-1][...] = _rms_fwd(x_ref[...], w_ref[...]).astype(BF16)

    return _pc(body, name=name, out_shape=_sds((T, D), BF16), grid=(T // tm,),
               in_specs=[pl.BlockSpec((tm, D), lambda i: (i, 0)), pl.BlockSpec((1, D), lambda i: (0, 0))] + [DEP_SPEC] * len(deps),
               out_specs=pl.BlockSpec((tm, D), lambda i: (i, 0)), sem=("parallel",))(x, w, *deps)


def rmsnorm_bwd_add(x, w, dxn, dres, name):
    T = x.shape[0]
    tm = _tile(T, 512)

    def body(x_ref, w_ref, dxn_ref, dres_ref, dx_ref, dw_ref):
        dx, dw = _rms_bwd(x_ref[...], w_ref[...], dxn_ref[...])
        dx_ref[...] = dres_ref[...] + dx

        @pl.when(pl.program_id(0) == 0)
        def _():
            dw_ref[...] = jnp.zeros_like(dw_ref)
        dw_ref[...] += dw

    row = pl.BlockSpec((tm, D), lambda i: (i, 0))
    vec = pl.BlockSpec((1, D), lambda i: (0, 0))
    return _pc(body, name=name, out_shape=(_sds((T, D), F32), _sds((1, D), F32)), grid=(T // tm,),
               in_specs=[row, vec, row, row], out_specs=(row, vec), sem=("arbitrary",))(x, w, dxn, dres)


def final_loss(x, w, tgt, name):
    T = x.shape[0]
    tm = _tile(T, 512)

    def body(x_ref, w_ref, t_ref, loss_ref, dx_ref, dw_ref):
        xv, wv = x_ref[...], w_ref[...]
        err = _rms_fwd(xv, wv) - t_ref[...]
        dx, dw = _rms_bwd(xv, wv, err * (1.0 / D))
        dx_ref[...] = dx

        @pl.when(pl.program_id(0) == 0)
        def _():
            dw_ref[...] = jnp.zeros_like(dw_ref)
            loss_ref[...] = jnp.zeros_like(loss_ref)
        dw_ref[...] += dw
        loss_ref[...] += jnp.full((1, 128), 0.5 / D, F32) * jnp.sum(err * err)

    row = pl.BlockSpec((tm, D), lambda i: (i, 0))
    vec = pl.BlockSpec((1, D), lambda i: (0, 0))
    return _pc(body, name=name, out_shape=(_sds((1, 128), F32), _sds((T, D), F32), _sds((1, D), F32)),
               grid=(T // tm,), in_specs=[row, vec, row],
               out_specs=(pl.BlockSpec((1, 128), lambda i: (0, 0)), row, vec), sem=("arbitrary",))(x, w, tgt)


def _col_tile(n):
    for t in (1536, 1408, 1024, 768, 512, 384, 256, 128):
        if n % t == 0:
            return t
    return n


def mm_nn(a, b, name, bias=None, residual=None, out_dtype=F32):
    T, K = a.shape
    N = b.shape[1]
    tm, tn = _tile(T, 512), _col_tile(N)

    def body(a_ref, b_ref, *rest):
        o_ref = rest[-1]
        acc = _dot(a_ref[...].astype(BF16), b_ref[...])
        for extra in rest[:-1]:
            acc = acc + extra[...]
        o_ref[...] = acc.astype(out_dtype)

    in_specs = [pl.BlockSpec((tm, K), lambda j, i: (i, 0)), pl.BlockSpec((K, tn), lambda j, i: (0, j))]
    args = [a, b]
    if bias is not None:
        in_specs.append(pl.BlockSpec((1, tn), lambda j, i: (0, j)))
        args.append(bias)
    if residual is not None:
        in_specs.append(pl.BlockSpec((tm, tn), lambda j, i: (i, j)))
        args.append(residual)
    return _pc(body, name=name, out_shape=_sds((T, N), out_dtype), grid=(N // tn, T // tm), in_specs=in_specs,
               out_specs=pl.BlockSpec((tm, tn), lambda j, i: (i, j)), sem=("parallel", "parallel"))(*args)


def mm_nt(a, b, name, out_dtype=F32):
    T, N = a.shape
    K = b.shape[0]
    tm = _tile(T, 512)

    def body(a_ref, b_ref, o_ref):
        o_ref[...] = _dot(a_ref[...].astype(BF16), b_ref[...], NT).astype(out_dtype)

    return _pc(body, name=name, out_shape=_sds((T, K), out_dtype), grid=(T // tm,),
               in_specs=[pl.BlockSpec((tm, N), lambda i: (i, 0)), pl.BlockSpec((K, N), lambda i: (0, 0))],
               out_specs=pl.BlockSpec((tm, K), lambda i: (i, 0)), sem=("parallel",))(a, b)


def mm_tn(a, b, name):
    T, K = a.shape
    N = b.shape[1]
    tt, tn = _tile(T, 1024), _col_tile(N)

    def body(a_ref, b_ref, o_ref):
        @pl.when(pl.program_id(1) == 0)
        def _():
            o_ref[...] = jnp.zeros_like(o_ref)
        o_ref[...] += _dot(a_ref[...].astype(BF16), b_ref[...].astype(BF16), TN)

    return _pc(body, name=name, out_shape=_sds((K, N), F32), grid=(N // tn, T // tt),
               in_specs=[pl.BlockSpec((tt, K), lambda j, t: (t, 0)), pl.BlockSpec((tt, tn), lambda j, t: (t, j))],
               out_specs=pl.BlockSpec((K, tn), lambda j, t: (0, j)), sem=("parallel", "arbitrary"))(a, b)


def ffn_up(xn, wgu, name, deps=()):
    T = xn.shape[0]
    tm = _tile(T, 1024)

    def body(x_ref, w_ref, *rest):
        xv = x_ref[...]
        for j in range(2 * N_FB):
            rest[-1][j] = _dot(xv, w_ref[j]).astype(BF16)

    return _pc(body, name=name, out_shape=_sds((2 * N_FB, T, FB), BF16), grid=(T // tm,),
               in_specs=[pl.BlockSpec((tm, D), lambda i: (i, 0)), _resident((2 * N_FB, D, FB))] + [DEP_SPEC] * len(deps),
               out_specs=pl.BlockSpec((2 * N_FB, tm, FB), lambda i: (0, i, 0)), sem=("parallel",))(xn, wgu, *deps)


def ffn_down(gu, wd, x, name):
    T = x.shape[0]
    tm = _tile(T, 512)

    def body(gu_ref, w_ref, x_ref, o_ref):
        acc = jnp.zeros((tm, D), F32)
        for g in range(N_FB):
            gate, up = gu_ref[g], gu_ref[N_FB + g]
            acc = acc + _dot(gate * _sigmoid(gate) * up, w_ref[g])
        o_ref[...] = x_ref[...] + 0.5 * acc

    row = pl.BlockSpec((tm, D), lambda i: (i, 0))
    return _pc(body, name=name, out_shape=_sds((T, D), F32), grid=(T // tm,),
               in_specs=[pl.BlockSpec((2 * N_FB, tm, FB), lambda i: (0, i, 0)),
                         _resident((N_FB, FB, D)), row],
               out_specs=row, sem=("parallel",))(gu, wd, x)


def _resident(shape):
    return pl.BlockSpec(shape, lambda *_: (0,) * len(shape), pipeline_mode=pl.Buffered(1))


def _store_blocks_bf16(acc, out_hbm, stage, sem):
    for j in range(acc.shape[0]):
        stage[...] = acc[j].astype(BF16)
        copy = pltpu.make_async_copy(stage, out_hbm.at[j], sem)
        copy.start()
        copy.wait()


def ffn_bwd_hidden(dout, wd, gu, name, deps=()):
    T = dout.shape[0]
    tm = _tile(T, 512)
    n_t = T // tm

    def body(d_ref, w_ref, gu_ref, *rest):
        dgu_ref, dwd_hbm, acc, stage, sem = rest[-5:]
        t = pl.program_id(0)

        @pl.when(t == 0)
        def _():
            acc[...] = jnp.zeros_like(acc)
        dy = (0.5 * d_ref[...]).astype(BF16)
        for g in range(N_FB):
            gate, up = gu_ref[g], gu_ref[N_FB + g]
            sg = _sigmoid(gate)
            silu = gate * sg
            dact = _dot(dy, w_ref[g], NT).astype(BF16)
            acc[g] += _dot(silu * up, dy, TN)
            dgu_ref[g] = dact * up * (sg * (1.0 + gate * (1.0 - sg)))
            dgu_ref[N_FB + g] = dact * silu

        @pl.when(t == n_t - 1)
        def _():
            _store_blocks_bf16(acc, dwd_hbm, stage, sem)

    return _pc(body, name=name, out_shape=(_sds((2 * N_FB, T, FB), BF16), _sds((N_FB, FB, D), BF16)), grid=(n_t,),
               in_specs=[pl.BlockSpec((tm, D), lambda i: (i, 0)), _resident((N_FB, FB, D)),
                         pl.BlockSpec((2 * N_FB, tm, FB), lambda i: (0, i, 0))] + [DEP_SPEC] * len(deps),
               out_specs=(pl.BlockSpec((2 * N_FB, tm, FB), lambda i: (0, i, 0)), pl.BlockSpec(memory_space=pl.ANY)),
               scratch=[pltpu.VMEM((N_FB, FB, D), F32), pltpu.VMEM((FB, D), BF16), pltpu.SemaphoreType.DMA],
               sem=("arbitrary",))(dout, wd, gu, *deps)


def ffn_bwd_input(dgu, wgu, x, dout, nw, name, deps=()):
    T = x.shape[0]
    tm = _tile(T, 512)

    def body(dgu_ref, w_ref, x_ref, d_ref, nw_ref, *rest):
        dx_ref, dnw_ref = rest[-2:]
        dxn = jnp.zeros((tm, D), F32)
        for j in range(2 * N_FB):
            dxn = dxn + _dot(dgu_ref[j], w_ref[j], NT)
        dx, dw = _rms_bwd(x_ref[...], nw_ref[...], dxn)
        dx_ref[...] = d_ref[...] + dx

        @pl.when(pl.program_id(0) == 0)
        def _():
            dnw_ref[...] = jnp.zeros_like(dnw_ref)
        dnw_ref[...] += dw

    row = pl.BlockSpec((tm, D), lambda i: (i, 0))
    vec = pl.BlockSpec((1, D), lambda i: (0, 0))
    return _pc(body, name=name, out_shape=(_sds((T, D), F32), _sds((1, D), F32)), grid=(T // tm,),
               in_specs=[pl.BlockSpec((2 * N_FB, tm, FB), lambda i: (0, i, 0)), _resident((2 * N_FB, D, FB)),
                         row, row, vec] + [DEP_SPEC] * len(deps),
               out_specs=(row, vec), sem=("arbitrary",))(dgu, wgu, x, dout, nw, *deps)


def ffn_wgrad_gu(xn, dgu, name):
    T = xn.shape[0]
    tt = _tile(T, 1024)
    n_t = T // tt

    def body(x_ref, d_ref, dw_hbm, acc, stage, sem):
        t = pl.program_id(0)

        @pl.when(t == 0)
        def _():
            acc[...] = jnp.zeros_like(acc)
        xn_tile = x_ref[...]
        for j in range(2 * N_FB):
            acc[j] += _dot(xn_tile, d_ref[j], TN)

        @pl.when(t == n_t - 1)
        def _():
            _store_blocks_bf16(acc, dw_hbm, stage, sem)

    return _pc(body, name=name, out_shape=_sds((2 * N_FB, D, FB), BF16), grid=(n_t,),
               in_specs=[pl.BlockSpec((tt, D), lambda t: (t, 0)), pl.BlockSpec((2 * N_FB, tt, FB), lambda t: (0, t, 0))],
               out_specs=pl.BlockSpec(memory_space=pl.ANY),
               scratch=[pltpu.VMEM((2 * N_FB, D, FB), F32), pltpu.VMEM((D, FB), BF16), pltpu.SemaphoreType.DMA],
               sem=("arbitrary",))(xn, dgu)


def ffn_forward(x, nw, wgu, wd, tag, deps=()):
    xn = rmsnorm_bf16(x, nw, f"{tag}_norm", deps)
    gu = ffn_up(xn, wgu, f"{tag}_up")
    return ffn_down(gu, wd, x, f"{tag}_down"), (x, xn, gu)


def ffn_backward(dout, saved, nw, wgu, wd, tag, deps=(), on_grads=None):
    x, xn, gu = saved
    dgu, dwd = ffn_bwd_hidden(dout, wd, gu, f"{tag}_bwd_hidden", deps)
    dwgu = ffn_wgrad_gu(xn, dgu, f"{tag}_wgrad_gu")
    late = on_grads(dwgu, dwd) if on_grads else ()
    dx, dnw = ffn_bwd_input(dgu, wgu, x, dout, nw, f"{tag}_bwd_input", late)
    return dx, dnw, dwgu, dwd


N_QKV_BLK = 3 * HEADS_A
Z_BLK0 = N_QKV_BLK
BA_BLK = A_COLS // 128 - 1


def _conv_taps(xcat, w):
    c = xcat[8:] * w[3:4]
    for k in range(3):
        c = c + pltpu.roll(xcat, 3 - k, 0)[8:] * w[k:k + 1]
    return c


def _head_cols(h):
    return slice(128 * h, 128 * (h + 1))


def gdn_conv_fwd(proj, wconv, name):
    T = proj.shape[0]
    tm = _tile(T, 512)

    def body(cur_ref, prev_ref, w_ref, c_ref, y_ref):
        kind, t = pl.program_id(0), pl.program_id(1)
        prev = jnp.where(t > 0, prev_ref[...], 0.0)
        c = _conv_taps(jnp.concatenate([prev, cur_ref[...]], axis=0), w_ref[...])
        c_ref[...] = c
        s = c * _sigmoid(c)
        scale = jnp.where(kind == 0, DK ** -0.5, 1.0)
        for h in range(HEADS_A):
            sh = s[:, _head_cols(h)]
            r = lax.rsqrt(jnp.sum(sh * sh, axis=-1, keepdims=True) + EPS)
            y_ref[h] = sh * jnp.where(kind < 2, r * scale, 1.0)

    return _pc(body, name=name, out_shape=(_sds((T, 3 * D), F32), _sds((N_QKV_BLK, T, 128), F32)),
               grid=(3, T // tm),
               in_specs=[pl.BlockSpec((tm, D), lambda kd, t: (t, kd)),
                         pl.BlockSpec((8, D), lambda kd, t: (jnp.maximum(t * (tm // 8) - 1, 0), kd)),
                         pl.BlockSpec((4, D), lambda kd, t: (0, kd))],
               out_specs=(pl.BlockSpec((tm, D), lambda kd, t: (t, kd)),
                          pl.BlockSpec((HEADS_A, tm, 128), lambda kd, t: (kd, t, 0))),
               sem=("parallel", "parallel"))(proj, proj, wconv)


def gdn_conv_bwd(dqkv, c, proj, wconv, name):
    T = c.shape[0]
    tm = _tile(T, 512)
    n_t = T // tm

    def body(dy_ref, dyn_ref, c_ref, cn_ref, x_ref, xp_ref, w_ref, dx_ref, dw_ref):
        kind, t = pl.program_id(0), pl.program_id(1)
        scale = jnp.where(kind == 0, DK ** -0.5, 1.0)

        def act_bwd(dy, cv):
            sg = _sigmoid(cv)
            s = cv * sg
            parts = []
            for h in range(HEADS_A):
                sh, dyh = s[:, _head_cols(h)], dy[h]
                r = lax.rsqrt(jnp.sum(sh * sh, axis=-1, keepdims=True) + EPS)
                ds_norm = scale * r * (dyh - (r * r) * sh * jnp.sum(dyh * sh, axis=-1, keepdims=True))
                parts.append(jnp.where(kind < 2, ds_norm, dyh))
            return jnp.concatenate(parts, axis=1) * (sg * (1.0 + cv * (1.0 - sg)))

        w = w_ref[...]
        dcur = act_bwd(dy_ref[...], c_ref[...])
        dnext = jnp.where(t < n_t - 1, act_bwd(dyn_ref[...], cn_ref[...]), 0.0)
        dcat = jnp.concatenate([dcur, dnext], axis=0)
        dx = dcur * w[3:4]
        for k in range(3):
            dx = dx + pltpu.roll(dcat, tm + 8 - (3 - k), 0)[:tm] * w[k:k + 1]
        dx_ref[...] = dx.astype(BF16)
        xprev = jnp.where(t > 0, xp_ref[...], 0.0)
        xcat = jnp.concatenate([xprev, x_ref[...]], axis=0)
        rows = [jnp.sum(dcur * pltpu.roll(xcat, 3 - k, 0)[8:], axis=0, keepdims=True) for k in range(3)]
        rows.append(jnp.sum(dcur * xcat[8:], axis=0, keepdims=True))

        @pl.when(t == 0)
        def _():
            dw_ref[...] = jnp.zeros_like(dw_ref)
        dw_ref[...] += jnp.concatenate(rows, axis=0)

    def nxt(t):
        return jnp.minimum((t + 1) * (tm // 8), T // 8 - 1)

    cur = pl.BlockSpec((tm, D), lambda kd, t: (t, kd))
    return _pc(body, name=name, out_shape=(_sds((T, 3 * D), BF16), _sds((4, 3 * D), F32)), grid=(3, n_t),
               in_specs=[pl.BlockSpec((HEADS_A, tm, 128), lambda kd, t: (kd, t, 0)),
                         pl.BlockSpec((HEADS_A, 8, 128), lambda kd, t: (kd, nxt(t), 0)),
                         cur, pl.BlockSpec((8, D), lambda kd, t: (nxt(t), kd)),
                         cur, pl.BlockSpec((8, D), lambda kd, t: (jnp.maximum(t * (tm // 8) - 1, 0), kd)),
                         pl.BlockSpec((4, D), lambda kd, t: (0, kd))],
               out_specs=(cur, pl.BlockSpec((4, D), lambda kd, t: (0, kd))),
               sem=("parallel", "arbitrary"))(dqkv, dqkv, c, c, proj, proj, wconv)


def _chunk_masks(n):
    ri = lax.broadcasted_iota(jnp.int32, (n, n), 0)
    ci = lax.broadcasted_iota(jnp.int32, (n, n), 1)
    same = (ri // CHUNK) == (ci // CHUNK)
    return same & (ri >= ci), same & (ri <= ci)


def gdn_gate_fwd(proj, al, dtb, name):
    T = proj.shape[0]
    tg = _tile(T, PREP_T)

    def body(ba_ref, al_ref, dtb_ref, o_ref):
        x = ba_ref[...]
        lane = lax.broadcasted_iota(jnp.int32, x.shape, 1)
        is_a = (lane >= HEADS_A) & (lane < 2 * HEADS_A)
        g = jnp.where(is_a, -jnp.exp(al_ref[...]) * _softplus(x + dtb_ref[...]), 0.0)
        lower, _ = _chunk_masks(tg)
        gc = _dot(lower.astype(F32), g, precision=HI)
        o_ref[...] = jnp.where(lane < HEADS_A, _sigmoid(x), gc)

    vec = pl.BlockSpec((1, 128), lambda i: (0, 0))
    return _pc(body, name=name, out_shape=_sds((T, 128), F32), grid=(T // tg,),
               in_specs=[pl.BlockSpec((tg, 128), lambda i: (i, BA_BLK)), vec, vec],
               out_specs=pl.BlockSpec((tg, 128), lambda i: (i, 0)), sem=("parallel",))(proj, al, dtb)


def gdn_gate_bwd(proj, al, dtb, dgb, name):
    T = proj.shape[0]
    tg = _tile(T, PREP_T)

    def body(ba_ref, al_ref, dtb_ref, dgb_ref, dba_ref, dal_ref, ddt_ref):
        x, d = ba_ref[...], dgb_ref[...]
        lane = lax.broadcasted_iota(jnp.int32, x.shape, 1)
        is_b = lane < HEADS_A
        is_a = (lane >= HEADS_A) & (lane < 2 * HEADS_A)
        beta = _sigmoid(x)
        e_a = jnp.exp(al_ref[...])
        z = x + dtb_ref[...]
        g = jnp.where(is_a, -e_a * _softplus(z), 0.0)
        _, upper = _chunk_masks(tg)
        dg = _dot(upper.astype(F32), jnp.where(is_a, d, 0.0), precision=HI)
        da = jnp.where(is_a, dg * (-e_a) * _sigmoid(z), 0.0)
        db = jnp.where(is_b, d * beta * (1.0 - beta), 0.0)
        dba_ref[...] = (da + db).astype(BF16)

        @pl.when(pl.program_id(0) == 0)
        def _():
            dal_ref[...] = jnp.zeros_like(dal_ref)
            ddt_ref[...] = jnp.zeros_like(ddt_ref)
        dal_ref[...] += jnp.sum(dg * g, axis=0, keepdims=True)
        ddt_ref[...] += jnp.sum(da, axis=0, keepdims=True)

    vec = pl.BlockSpec((1, 128), lambda i: (0, 0))
    blk = pl.BlockSpec((tg, 128), lambda i: (i, 0))
    return _pc(body, name=name, out_shape=(_sds((T, 128), BF16), _sds((1, 128), F32), _sds((1, 128), F32)),
               grid=(T // tg,), in_specs=[pl.BlockSpec((tg, 128), lambda i: (i, BA_BLK)), vec, vec, blk],
               out_specs=(blk, vec, vec), sem=("arbitrary",))(proj, al, dtb, dgb)


def _bmm(a, b, dims, precision=None):
    return lax.dot_general(a, b, dims, preferred_element_type=F32, precision=precision)


B_NN = (((2,), (1,)), ((0,), (0,)))
B_NT = (((2,), (2,)), ((0,), (0,)))


def _select_lane(x, lane_index):
    lane = lax.broadcasted_iota(jnp.int32, x.shape, x.ndim - 1)
    return jnp.sum(jnp.where(lane == lane_index, x, 0.0), axis=-1, keepdims=True)


B_TN = (((1,), (1,)), ((0,), (0,)))


def _bmm_split(a, b, dims):
    ah, bh = a.astype(BF16), b.astype(BF16)
    al, bl = (a - ah.astype(F32)).astype(BF16), (b - bh.astype(F32)).astype(BF16)
    return _bmm(ah, bh, dims) + (_bmm(ah, bl, dims) + _bmm(al, bh, dims))


@jax.custom_vjp
def _bmm_f32(a, b):
    return _bmm_split(a, b, B_NN)


def _bmm_f32_fwd(a, b):
    return _bmm_split(a, b, B_NN), (a, b)


def _bmm_f32_bwd(res, dc):
    a, b = res
    return _bmm_split(dc, b, B_NT), _bmm_split(a, dc, B_TN)


_bmm_f32.defvjp(_bmm_f32_fwd, _bmm_f32_bwd)


def _tri_inverse(lmat):
    ri = lax.broadcasted_iota(jnp.int32, lmat.shape, 1)
    ci = lax.broadcasted_iota(jnp.int32, lmat.shape, 2)
    inv = jnp.where(ri == ci, 1.0, 0.0) - lmat
    power = lmat
    for _ in range(5):
        power = _bmm_split(power, power, B_NN)
        inv = inv + _bmm_split(inv, power, B_NN)
    return inv


def _stored_inverse(x):
    @jax.custom_vjp
    def inverse(lmat):
        return x

    def fwd(lmat):
        return x, None

    def bwd(_, dx):
        return (-_bmm_split(_bmm_split(x, dx, B_TN), x, B_NT),)

    inverse.defvjp(fwd, bwd)
    return inverse


def _gdn_prep(q, k, v, gb, h, inverse):
    nb = q.shape[0]
    beta = _select_lane(gb, h)
    gc = _select_lane(gb, HEADS_A + h)
    ri = lax.broadcasted_iota(jnp.int32, (nb, CHUNK, CHUNK), 1)
    ci = lax.broadcasted_iota(jnp.int32, (nb, CHUNK, CHUNK), 2)
    lower, strict, eye = ri >= ci, ri > ci, ri == ci
    gcol = jnp.broadcast_to(gc, (nb, CHUNK, CHUNK))
    grow = _bmm_f32(jnp.ones((nb, CHUNK, CHUNK), F32), jnp.where(eye, gcol, 0.0))
    decay = jnp.where(lower, jnp.exp(jnp.where(lower, gcol - grow, 0.0)), 0.0)
    kb = k * beta
    kbf = k.astype(BF16)
    inv = inverse(jnp.where(strict, _bmm(kb.astype(BF16), kbf, B_NT) * decay, 0.0))
    eg = jnp.exp(gc)
    sol = _bmm_f32(inv, jnp.concatenate([v * beta, kb * eg], axis=-1))
    aqk = _bmm(q.astype(BF16), kbf, B_NT) * decay
    g_last = gc[:, CHUNK - 1:CHUNK, :]
    gl = jnp.broadcast_to(jnp.exp(g_last), (nb, 1, 128))
    return (sol[..., :DK], sol[..., DK:], q * eg, k * jnp.exp(g_last - gc), aqk, gl), inv


def gdn_prep_fwd(qkv, gb, name):
    T = qkv.shape[1]
    tp = _tile(T, PREP_T)
    nb = tp // CHUNK

    def body(q_ref, k_ref, v_ref, gb_ref, u_ref, w_ref, qd_ref, kd_ref, a_ref, gl_ref, inv_ref):
        h = pl.program_id(1)
        shp = (nb, CHUNK, 128)
        (u, w, qd, kd, aqk, gl), inv = _gdn_prep(q_ref[0].reshape(shp), k_ref[0].reshape(shp), v_ref[0].reshape(shp),
                                                 gb_ref[...].reshape(shp), h, _tri_inverse)
        u_ref[0] = u.reshape(tp, 128)
        w_ref[0] = w.reshape(tp, 128).astype(BF16)
        qd_ref[0] = qd.reshape(tp, 128).astype(BF16)
        kd_ref[0] = kd.reshape(tp, 128).astype(BF16)
        a_ref[0] = aqk.reshape(tp, CHUNK).astype(BF16)
        gl_ref[0] = gl.reshape(nb, 1, 128)
        inv_ref[0] = inv.reshape(tp, CHUNK)

    def head(off):
        return pl.BlockSpec((1, tp, 128), lambda n, h: (h + off, n, 0))

    matmul_only = _sds((HEADS_A, T, 128), BF16)
    narrow = pl.BlockSpec((1, tp, CHUNK), lambda n, h: (h, n, 0))
    return _pc(body, name=name,
               out_shape=(_sds((HEADS_A, T, 128), F32), matmul_only, matmul_only, matmul_only, _sds((HEADS_A, T, CHUNK), BF16),
                          _sds((HEADS_A, T // CHUNK, 1, 128), F32), _sds((HEADS_A, T, CHUNK), F32)),
               grid=(T // tp, HEADS_A),
               in_specs=[head(0), head(HEADS_A), head(2 * HEADS_A), pl.BlockSpec((tp, 128), lambda n, h: (n, 0))],
               out_specs=(head(0), head(0), head(0), head(0), narrow,
                          pl.BlockSpec((1, nb, 1, 128), lambda n, h: (h, n, 0, 0)), narrow),
               sem=("parallel", "parallel"))(qkv, qkv, qkv, gb)


def gdn_prep_bwd(qkv, gb, inv, du, dw, dqd, dkd, da, dgl, name):
    T = qkv.shape[1]
    tp = _tile(T, PREP_T)
    nb = tp // CHUNK

    def body(q_ref, k_ref, v_ref, gb_ref, inv_ref, du_ref, dw_ref, dqd_ref, dkd_ref, da_ref, dgl_ref, dqkv_ref, dgb_ref):
        h = pl.program_id(1)
        shp = (nb, CHUNK, 128)
        stored = _stored_inverse(inv_ref[0].reshape(nb, CHUNK, CHUNK))
        _, vjp = jax.vjp(lambda q, k, v, gb: _gdn_prep(q, k, v, gb, h, stored)[0], q_ref[0].reshape(shp),
                         k_ref[0].reshape(shp), v_ref[0].reshape(shp), gb_ref[...].reshape(shp))
        dq, dk, dv, dgb = vjp((du_ref[0].reshape(shp), dw_ref[0].reshape(shp), dqd_ref[0].reshape(shp),
                               dkd_ref[0].reshape(shp), da_ref[0].reshape(nb, CHUNK, CHUNK), dgl_ref[0].reshape(nb, 1, 128)))
        dqkv_ref[h] = dq.reshape(tp, 128)
        dqkv_ref[HEADS_A + h] = dk.reshape(tp, 128)
        dqkv_ref[2 * HEADS_A + h] = dv.reshape(tp, 128)

        @pl.when(h == 0)
        def _():
            dgb_ref[...] = jnp.zeros_like(dgb_ref)
        dgb_ref[...] += dgb.reshape(tp, 128)

    def head(off):
        return pl.BlockSpec((1, tp, 128), lambda n, h: (h + off, n, 0))

    narrow = pl.BlockSpec((1, tp, CHUNK), lambda n, h: (h, n, 0))
    return _pc(body, name=name, out_shape=(_sds((N_QKV_BLK, T, 128), F32), _sds((T, 128), F32)),
               grid=(T // tp, HEADS_A),
               in_specs=[head(0), head(HEADS_A), head(2 * HEADS_A), pl.BlockSpec((tp, 128), lambda n, h: (n, 0)), narrow,
                         head(0), head(0), head(0), head(0), narrow,
                         pl.BlockSpec((1, nb, 1, 128), lambda n, h: (h, n, 0, 0))],
               out_specs=(pl.BlockSpec((N_QKV_BLK, tp, 128), lambda n, h: (0, n, 0)),
                          pl.BlockSpec((tp, 128), lambda n, h: (n, 0))),
               sem=("parallel", "arbitrary"))(qkv, qkv, qkv, gb, inv, du, dw, dqd, dkd, da, dgl)


def gdn_scan_fwd(u, w, qd, kd, aqk, gl, name):
    T = u.shape[1]
    n_chunks = T // CHUNK

    def body(u_ref, w_ref, qd_ref, kd_ref, a_ref, gl_ref, o_ref, sin_ref, state):
        @pl.when(pl.program_id(0) == 0)
        def _():
            state[...] = jnp.zeros_like(state)
        s = state[...]
        sb = s.astype(BF16)
        sin_ref[0] = sb
        both = _bmm(jnp.concatenate([w_ref[...], qd_ref[...]], axis=1).astype(BF16), sb, B_NN)
        vn = (u_ref[...] - both[:, :CHUNK]).astype(BF16)
        o_ref[...] = both[:, CHUNK:] + _bmm(a_ref[...].astype(BF16), vn, B_NN)
        state[...] = s * gl_ref[:, 0] + _bmm(kd_ref[...].astype(BF16), vn, B_TN)

    blk = pl.BlockSpec((HEADS_A, CHUNK, 128), lambda n: (0, n, 0))
    return _pc(body, name=name,
               out_shape=(_sds((HEADS_A, T, 128), F32), _sds((n_chunks, HEADS_A, DK, 128), BF16)), grid=(n_chunks,),
               in_specs=[blk, blk, blk, blk, pl.BlockSpec((HEADS_A, CHUNK, CHUNK), lambda n: (0, n, 0)),
                         pl.BlockSpec((HEADS_A, 1, 1, 128), lambda n: (0, n, 0, 0))],
               out_specs=(blk, pl.BlockSpec((1, HEADS_A, DK, 128), lambda n: (n, 0, 0, 0))),
               scratch=[pltpu.VMEM((HEADS_A, DK, 128), F32)], sem=("arbitrary",))(u, w, qd, kd, aqk, gl)


def gdn_scan_bwd(u, w, qd, kd, aqk, gl, sin, do, name):
    T = u.shape[1]
    n_chunks = T // CHUNK

    def body(u_ref, w_ref, qd_ref, kd_ref, a_ref, gl_ref, sin_ref, do_ref,
             du_ref, dw_ref, dqd_ref, dkd_ref, da_ref, dgl_ref, dstate):
        @pl.when(pl.program_id(0) == 0)
        def _():
            dstate[...] = jnp.zeros_like(dstate)
        lane0 = lax.broadcasted_iota(jnp.int32, (HEADS_A, 1, 128), 2) == 0
        sb = sin_ref[0]
        s = sb.astype(F32)
        wb, qdb, kdb = w_ref[...].astype(BF16), qd_ref[...].astype(BF16), kd_ref[...].astype(BF16)
        ab, dob = a_ref[...].astype(BF16), do_ref[...].astype(BF16)
        vn = (u_ref[...] - _bmm(wb, sb, B_NN)).astype(BF16)
        ds_out = dstate[...]
        dsb = ds_out.astype(BF16)
        dqd_ref[...] = _bmm(dob, sb, B_NT)
        da_ref[...] = _bmm(dob, vn, B_NT)
        dv = _bmm(ab, dob, B_TN) + _bmm(kdb, dsb, B_NN)
        dkd_ref[...] = _bmm(vn, dsb, B_NT)
        dgl = jnp.sum(jnp.sum(ds_out * s, axis=2, keepdims=True), axis=1, keepdims=True)
        dgl_ref[:, 0] = jnp.where(lane0, dgl, 0.0)
        du_ref[...] = dv
        dvb = dv.astype(BF16)
        dw_ref[...] = -_bmm(dvb, sb, B_NT)
        dstate[...] = ds_out * gl_ref[:, 0] + _bmm(qdb, dob, B_TN) - _bmm(wb, dvb, B_TN)

    last = n_chunks - 1
    blk = pl.BlockSpec((HEADS_A, CHUNK, 128), lambda n: (0, last - n, 0))
    ablk = pl.BlockSpec((HEADS_A, CHUNK, CHUNK), lambda n: (0, last - n, 0))
    glblk = pl.BlockSpec((HEADS_A, 1, 1, 128), lambda n: (0, last - n, 0, 0))
    per_head = _sds((HEADS_A, T, 128), F32)
    return _pc(body, name=name,
               out_shape=(per_head, per_head, per_head, per_head, _sds((HEADS_A, T, CHUNK), F32),
                          _sds((HEADS_A, n_chunks, 1, 128), F32)), grid=(n_chunks,),
               in_specs=[blk, blk, blk, blk, ablk, glblk,
                         pl.BlockSpec((1, HEADS_A, DK, 128), lambda n: (last - n, 0, 0, 0)), blk],
               out_specs=(blk, blk, blk, blk, ablk, glblk),
               scratch=[pltpu.VMEM((HEADS_A, DK, 128), F32)], sem=("arbitrary",))(u, w, qd, kd, aqk, gl, sin, do)


def gdn_outnorm_fwd(o, proj, wn, name):
    T = o.shape[1]
    tm = _tile(T, 512)

    def body(o_ref, z_ref, wn_ref, y_ref):
        for h in range(HEADS_A):
            z = z_ref[:, 128 * h:128 * (h + 1)]
            y_ref[:, 128 * h:128 * (h + 1)] = (_rms_fwd(o_ref[h], wn_ref[...]) * (z * _sigmoid(z))).astype(BF16)

    return _pc(body, name=name, out_shape=_sds((T, D), BF16), grid=(T // tm,),
               in_specs=[pl.BlockSpec((HEADS_A, tm, 128), lambda i: (0, i, 0)),
                         pl.BlockSpec((tm, D), lambda i: (i, Z_BLK0 * 128 // D)), pl.BlockSpec((1, 128), lambda i: (0, 0))],
               out_specs=pl.BlockSpec((tm, D), lambda i: (i, 0)), sem=("parallel",))(o, proj, wn)


def gdn_outnorm_bwd(o, proj, wn, dy, name):
    T = o.shape[1]
    tm = _tile(T, 512)

    def body(o_ref, z_ref, wn_ref, dy_ref, do_ref, dz_ref, dwn_ref):
        wn = wn_ref[...]
        acc = jnp.zeros((1, 128), F32)
        for h in range(HEADS_A):
            cols = slice(128 * h, 128 * (h + 1))
            z, dyh, ov = z_ref[:, cols], dy_ref[:, cols], o_ref[h]
            sg = _sigmoid(z)
            do, dwn = _rms_bwd(ov, wn, dyh * (z * sg))
            do_ref[h] = do
            acc = acc + dwn
            dz_ref[:, cols] = (dyh * _rms_fwd(ov, wn) * (sg * (1.0 + z * (1.0 - sg)))).astype(BF16)

        @pl.when(pl.program_id(0) == 0)
        def _():
            dwn_ref[...] = jnp.zeros_like(dwn_ref)
        dwn_ref[...] += acc

    row = pl.BlockSpec((tm, D), lambda i: (i, 0))
    vec = pl.BlockSpec((1, 128), lambda i: (0, 0))
    hblk = pl.BlockSpec((HEADS_A, tm, 128), lambda i: (0, i, 0))
    return _pc(body, name=name, out_shape=(_sds((HEADS_A, T, 128), F32), _sds((T, D), BF16), _sds((1, 128), F32)),
               grid=(T // tm,),
               in_specs=[hblk, pl.BlockSpec((tm, D), lambda i: (i, Z_BLK0 * 128 // D)), vec, row],
               out_specs=(hblk, row, vec), sem=("arbitrary",))(o, proj, wn, dy)


def gdn_forward(x, nw, w_in, wconv, al, dtb, wn, w_out, tag, deps=()):
    h = rmsnorm_bf16(x, nw, f"{tag}_norm", deps)
    proj = mm_nn(h, w_in, f"{tag}_proj")
    c, qkv = gdn_conv_fwd(proj, wconv, f"{tag}_conv")
    gb = gdn_gate_fwd(proj, al, dtb, f"{tag}_gate")
    u, w, qd, kd, aqk, gl, inv = gdn_prep_fwd(qkv, gb, f"{tag}_prep")
    o, sin = gdn_scan_fwd(u, w, qd, kd, aqk, gl, f"{tag}_scan")
    on = gdn_outnorm_fwd(o, proj, wn, f"{tag}_outnorm")
    y = mm_nn(on, w_out, f"{tag}_out", residual=x)
    return y, (x, h, proj, c, qkv, gb, inv, (u, w, qd, kd, aqk, gl), sin, o, on)


def gdn_backward(dout, saved, nw, w_in, wconv, al, dtb, wn, w_out, tag):
    x, h, proj, c, qkv, gb, inv, prep, sin, o, on = saved
    d_on = mm_nt(dout, w_out, f"{tag}_out_bwd")
    dw_out = mm_tn(on, dout, f"{tag}_out_wgrad")
    do, dz, dwn = gdn_outnorm_bwd(o, proj, wn, d_on, f"{tag}_outnorm_bwd")
    du, dw, dqd, dkd, da, dgl = gdn_scan_bwd(*prep, sin, do, f"{tag}_scan_bwd")
    dqkv, dgb = gdn_prep_bwd(qkv, gb, inv, du, dw, dqd, dkd, da, dgl, f"{tag}_prep_bwd")
    dba, dal, ddt = gdn_gate_bwd(proj, al, dtb, dgb, f"{tag}_gate_bwd")
    dpre, dwconv = gdn_conv_bwd(dqkv, c, proj, wconv, f"{tag}_conv_bwd")
    dproj = jnp.concatenate([dpre, dz, dba], axis=1)
    dw_in = mm_tn(h, dproj, f"{tag}_proj_wgrad")
    dh = mm_nt(dproj, w_in, f"{tag}_proj_bwd")
    dx, dnw = rmsnorm_bwd_add(x, nw, dh, dout, f"{tag}_norm_bwd")
    return dx, dnw, dw_in, dwconv, dal, ddt, dwn, dw_out


N_KV, GROUP = 4, 4
KV_COLS = 2 * N_KV * B_HD
B_COLS = D + KV_COLS


@jax.custom_vjp
def _swap_lane_halves(x):
    return pltpu.roll(x, 64, 1)


_swap_lane_halves.defvjp(lambda x: (pltpu.roll(x, 64, 1), None), lambda _, g: (pltpu.roll(g, 64, 1),))


def _swa_block(q, kp, kc, vp, vc, sk, first):
    rows = GROUP * B_BLK
    qi = lax.broadcasted_iota(jnp.int32, (N_KV, rows, B_BLK), 1) % B_BLK
    kj = lax.broadcasted_iota(jnp.int32, (N_KV, rows, B_BLK), 2)
    from_cur = kj <= qi

    def batch(parts):
        return jnp.concatenate([part[None] for part in parts], axis=0)

    def per_kv(cur, prev):
        return batch([jnp.concatenate([cur[:, j * B_HD:(j + 1) * B_HD], prev[:, j * B_HD:(j + 1) * B_HD]], axis=0)
                      for j in range(N_KV)]).astype(BF16)

    qs = batch([jnp.concatenate([q[:, hq * B_HD:(hq + 1) * B_HD] for hq in range(GROUP * j, GROUP * (j + 1))], axis=0)
                for j in range(N_KV)]).astype(BF16)
    sink = batch([jnp.concatenate([jnp.broadcast_to(sk[:, hq:hq + 1], (B_BLK, 1))
                                   for hq in range(GROUP * j, GROUP * (j + 1))], axis=0) for j in range(N_KV)])
    both = _bmm(qs, per_kv(kc, kp), B_NT)
    s = jnp.where(from_cur, both[..., :B_BLK], jnp.where(first, -1e30, both[..., B_BLK:])) * (B_HD ** -0.5)
    m = lax.stop_gradient(jnp.maximum(jnp.max(s, axis=-1, keepdims=True), sink))
    e = jnp.exp(s - m)
    p = e * (1.0 / (jnp.sum(e, axis=-1, keepdims=True) + jnp.exp(sink - m)))
    p_both = jnp.concatenate([jnp.where(from_cur, p, 0.0), jnp.where(from_cur, 0.0, p)], axis=-1).astype(BF16)
    o = _bmm(p_both, per_kv(vc, vp), B_NN)
    return jnp.concatenate([o[j, g * B_BLK:(g + 1) * B_BLK] for j in range(N_KV) for g in range(GROUP)], axis=1)


def swa_core_fwd(proj, sk, name):
    T = proj.shape[0]
    half = N_KV * B_HD

    def body(q_ref, kvc_ref, kvp_ref, sk_ref, o_ref):
        kvc, kvp = kvc_ref[...], kvp_ref[...]
        o_ref[...] = _swa_block(q_ref[...], kvp[:, :half], kvc[:, :half], kvp[:, half:], kvc[:, half:], sk_ref[...],
                                pl.program_id(0) == 0).astype(BF16)

    return _pc(body, name=name, out_shape=_sds((T, D), BF16), grid=(T // B_BLK,),
               in_specs=[pl.BlockSpec((B_BLK, D), lambda n: (n, 0)),
                         pl.BlockSpec((B_BLK, KV_COLS), lambda n: (n, D // KV_COLS)),
                         pl.BlockSpec((B_BLK, KV_COLS), lambda n: (jnp.maximum(n - 1, 0), D // KV_COLS)),
                         pl.BlockSpec((1, 128), lambda n: (0, 0))],
               out_specs=pl.BlockSpec((B_BLK, D), lambda n: (n, 0)), sem=("parallel",))(proj, proj, proj, sk)


def swa_core_bwd(proj, sk, do, name):
    T = proj.shape[0]
    last = T // B_BLK - 1
    half = N_KV * B_HD

    def body(q_ref, kvc_ref, kvp_ref, sk_ref, do_ref, dproj_ref, dbias_ref, dsk_ref, carry):
        step = pl.program_id(0)
        first = step == last

        @pl.when(step == 0)
        def _():
            carry[...] = jnp.zeros_like(carry)
            dbias_ref[...] = jnp.zeros_like(dbias_ref)
            dsk_ref[...] = jnp.zeros_like(dsk_ref)
        kvc, kvp = kvc_ref[...], kvp_ref[...]
        _, vjp = jax.vjp(functools.partial(_swa_block, first=first), q_ref[...], kvp[:, :half], kvc[:, :half],
                         kvp[:, half:], kvc[:, half:], sk_ref[...])
        dq, dkp, dkc, dvp, dvc, dsk = vjp(do_ref[...])
        dkv = jnp.concatenate([dkc, dvc], axis=1) + carry[...]
        carry[...] = jnp.concatenate([dkp, dvp], axis=1)
        row = jnp.concatenate([dq, dkv], axis=1)
        dproj_ref[...] = row.astype(BF16)
        dbias_ref[...] += jnp.sum(row, axis=0, keepdims=True)
        dsk_ref[...] += dsk

    return _pc(body, name=name, out_shape=(_sds((T, B_COLS), BF16), _sds((1, B_COLS), F32), _sds((1, 128), F32)),
               grid=(T // B_BLK,),
               in_specs=[pl.BlockSpec((B_BLK, D), lambda n: (last - n, 0)),
                         pl.BlockSpec((B_BLK, KV_COLS), lambda n: (last - n, D // KV_COLS)),
                         pl.BlockSpec((B_BLK, KV_COLS), lambda n: (jnp.maximum(last - n - 1, 0), D // KV_COLS)),
                         pl.BlockSpec((1, 128), lambda n: (0, 0)), pl.BlockSpec((B_BLK, D), lambda n: (last - n, 0))],
               out_specs=(pl.BlockSpec((B_BLK, B_COLS), lambda n: (last - n, 0)),
                          pl.BlockSpec((1, B_COLS), lambda n: (0, 0)), pl.BlockSpec((1, 128), lambda n: (0, 0))),
               scratch=[pltpu.VMEM((B_BLK, KV_COLS), F32)], sem=("arbitrary",))(proj, proj, proj, sk, do)


def col_sum(a, name):
    T, N = a.shape
    tm = _tile(T, 1024)

    def body(a_ref, o_ref):
        @pl.when(pl.program_id(0) == 0)
        def _():
            o_ref[...] = jnp.zeros_like(o_ref)
        o_ref[...] += jnp.sum(a_ref[...].astype(F32), axis=0, keepdims=True)

    return _pc(body, name=name, out_shape=_sds((1, N), F32), grid=(T // tm,),
               in_specs=[pl.BlockSpec((tm, N), lambda i: (i, 0))], out_specs=pl.BlockSpec((1, N), lambda i: (0, 0)),
               sem=("arbitrary",))(a)


def swa_forward(x, nw, w_in, b_in, sk, w_out, b_out, tag):
    h = rmsnorm_bf16(x, nw, f"{tag}_norm")
    proj = mm_nn(h, w_in, f"{tag}_proj", bias=b_in)
    o = swa_core_fwd(proj, sk, f"{tag}_core")
    y = mm_nn(o, w_out, f"{tag}_out", bias=b_out, residual=x)
    return y, (x, h, proj, o)


def swa_backward(dout, saved, nw, w_in, b_in, sk, w_out, b_out, tag):
    x, h, proj, o = saved
    do = mm_nt(dout, w_out, f"{tag}_out_bwd")
    dw_out = mm_tn(o, dout, f"{tag}_out_wgrad")
    db_out = col_sum(dout, f"{tag}_out_bias_grad")
    dproj, db_in, dsk = swa_core_bwd(proj, sk, do, f"{tag}_core_bwd")
    dw_in = mm_tn(h, dproj, f"{tag}_proj_wgrad")
    dh = mm_nt(dproj, w_in, f"{tag}_proj_bwd")
    dx, dnw = rmsnorm_bwd_add(x, nw, dh, dout, f"{tag}_norm_bwd")
    return dx, dnw, dw_in, db_in, dsk, dw_out, db_out


MESH = pl.DeviceIdType.MESH


def _position():
    return lax.axis_index("x"), lax.axis_index("y"), lax.axis_index("c")


def _slot(x, y, c):
    return 4 * x + 2 * y + c


def _peer(x, y, c, k):
    return (1 - x if k & 4 else x, 1 - y if k & 2 else y, 1 - c if k & 1 else c)


HBM_SPEC = pl.BlockSpec(memory_space=pltpu.HBM)
SEM_SPEC = pl.BlockSpec(memory_space=pltpu.SEMAPHORE)
DEP_SPEC = pl.BlockSpec(memory_space=pl.ANY)
SIDE_EFFECT = pltpu.SideEffectType.DATAFLOW_SIDE_EFFECTING
N_PEERS = N_DEV - 1


def _push_copies(srcs, lands, send_sems, recv_sems, scatter):
    x, y, c = _position()
    me = _slot(x, y, c)
    copies = []
    for k in (1, 2, 4, 3, 5, 6, 7):
        peer = _peer(x, y, c, k)
        for a in range(len(srcs)):
            copies.append(pltpu.make_async_remote_copy(
                src_ref=srcs[a].at[_slot(*peer)] if scatter else srcs[a], dst_ref=lands[a].at[me],
                send_sem=send_sems.at[N_PEERS * a + k - 1], recv_sem=recv_sems.at[N_PEERS * a + k - 1],
                device_id=peer, device_id_type=MESH))
    return copies


def push_start(srcs, lands, name, scatter, deps=()):
    n = len(srcs)
    first_out = 2 * n + len(deps)

    def body(*refs):
        for cp in _push_copies(refs[:n], refs[n:2 * n], refs[first_out], refs[first_out + 1], scatter):
            cp.start()
        refs[-1][...] = jnp.zeros_like(refs[-1])

    passed = [pltpu.HBM(t.shape, t.dtype) for t in list(srcs) + list(lands)]
    res = pl.pallas_call(
        body, name=name,
        out_shape=(pltpu.SemaphoreType.DMA((N_PEERS * n,)), pltpu.SemaphoreType.DMA((N_PEERS * n,)), *passed, _sds((8, 128), F32)),
        in_specs=[HBM_SPEC] * (2 * n) + [DEP_SPEC] * len(deps),
        out_specs=(SEM_SPEC, SEM_SPEC, *([HBM_SPEC] * (2 * n)), pl.BlockSpec(memory_space=pltpu.VMEM)),
        input_output_aliases={i: 2 + i for i in range(2 * n)},
        compiler_params=pltpu.CompilerParams(has_side_effects=SIDE_EFFECT),
    )(*[pltpu.with_memory_space_constraint(t, pltpu.HBM) for t in list(srcs) + list(lands)], *deps)
    return (res[0], res[1], list(res[2:2 + n]), list(res[2 + n:2 + 2 * n])), res[-1]


def push_wait(handles, after, name, scatter):
    send_sems, recv_sems, srcs, lands = handles
    n = len(srcs)
    after = tuple(after) if isinstance(after, (tuple, list)) else (after,)

    def body(*refs):
        for cp in _push_copies(refs[:n], refs[n:2 * n], refs[2 * n], refs[2 * n + 1], scatter):
            cp.wait_send()
            cp.wait_recv()

    res = pl.pallas_call(
        body, name=name, out_shape=tuple(pltpu.HBM(t.shape, t.dtype) for t in srcs + lands),
        in_specs=[HBM_SPEC] * (2 * n) + [SEM_SPEC, SEM_SPEC] + [DEP_SPEC] * len(after), out_specs=tuple([HBM_SPEC] * (2 * n)),
        input_output_aliases={i: i for i in range(2 * n)},
        compiler_params=pltpu.CompilerParams(has_side_effects=SIDE_EFFECT),
    )(*srcs, *lands, send_sems, recv_sems, *after)
    return list(res[n:])


def gather_start(shards, name, deps=()):
    me = _slot(*_position())
    lands = [lax.dynamic_update_slice(lax.empty((N_DEV,) + t.shape, t.dtype), t[None], (me,) + (0,) * t.ndim) for t in shards]
    return push_start(shards, lands, name, scatter=False, deps=deps)


def exchange_start(parts, name):
    me = _slot(*_position())
    lands = [lax.dynamic_update_slice(lax.empty(t.shape, t.dtype), lax.dynamic_index_in_dim(t, me, 0, keepdims=True),
                                      (me,) + (0,) * (t.ndim - 1)) for t in parts]
    return push_start(parts, lands, name, scatter=True)


def _row_tile(rows, cols):
    best = rows
    for t in range(16, rows, 16):
        if rows % t == 0 and t * cols * 4 <= (1 << 20):
            best = t
    return best


def adam_update(parts, w, m, v, name):
    n_layers = len(parts)
    P, R, C = parts[0].shape
    tr = _row_tile(R, C)
    n_t = R // tr

    def body(*refs):
        p_refs = refs[:n_layers]
        w_ref, m_ref, v_ref, g_ref, d_ref, nm_ref, nv_ref = refs[n_layers:]
        for layer in range(n_layers):
            @pl.when(pl.program_id(0) == layer)
            def _(p_ref=p_refs[layer]):
                g = p_ref[0].astype(F32)
                for s in range(1, P):
                    g = g + p_ref[s].astype(F32)
                new_m = ADAM_B1 * m_ref[0] + (1.0 - ADAM_B1) * g
                new_v = ADAM_B2 * v_ref[0] + (1.0 - ADAM_B2) * (g * g)
                m_hat = new_m / (1.0 - ADAM_B1 ** ADAM_STEP)
                v_hat = new_v / (1.0 - ADAM_B2 ** ADAM_STEP)
                g_ref[0] = g
                d_ref[0] = -ADAM_LR * (m_hat / (jnp.sqrt(v_hat) + ADAM_EPS) + ADAM_WD * w_ref[0])
                nm_ref[0] = new_m
                nv_ref[0] = new_v

    def part_spec(layer):
        return pl.BlockSpec((P, tr, C), lambda l_, i: (0, jnp.where(l_ == layer, i, jnp.where(l_ < layer, 0, n_t - 1)), 0))

    blk = pl.BlockSpec((1, tr, C), lambda l_, i: (l_, i, 0))
    out = _sds((n_layers, R, C), F32)
    return _pc(body, name=name, out_shape=(out, out, out, out), grid=(n_layers, n_t),
               in_specs=[part_spec(layer) for layer in range(n_layers)] + [blk, blk, blk],
               out_specs=(blk, blk, blk, blk), sem=("arbitrary", "arbitrary"))(*parts, w, m, v)


WEIGHTS = ("ffn1_norm", "ffn1_w_gu", "ffn1_w_down", "mix_norm", "ffn2_norm", "ffn2_w_gu", "ffn2_w_down", "a_w_in",
           "a_w_conv", "a_A_log", "a_dt_bias", "a_out_norm", "a_w_out", "b_w_in", "b_b_in", "b_sinks", "b_w_out",
           "b_b_out", "final_norm")
SHARDED = ("ffn1_w_gu", "ffn1_w_down", "ffn2_w_gu", "ffn2_w_down", "a_w_in", "a_w_conv", "a_w_out", "b_w_in", "b_b_in",
           "b_w_out", "b_b_out")
MISC_LANES = dict(a_A_log=(0, 8), a_dt_bias=(8, 16), b_sinks=(16, 32), a_out_norm=(128, 256))
LOSS_LANE = 256


def _pack_small(t):
    misc = jnp.zeros((D,), F32)
    for key, (lo, hi) in MISC_LANES.items():
        misc = misc.at[lo:hi].set(t[key].reshape(-1))
    if "loss" in t:
        misc = misc.at[LOSS_LANE].set(t["loss"])
    return jnp.concatenate([t["ffn1_norm"], t["mix_norm"], t["ffn2_norm"], t["final_norm"].reshape(1, D), misc[None]], axis=0)


def _unpack_small(p, like):
    out = dict(ffn1_norm=p[0:2], mix_norm=p[2:4], ffn2_norm=p[4:6], final_norm=p[6])
    for key, (lo, hi) in MISC_LANES.items():
        out[key] = p[7, lo:hi].reshape(like[key].shape)
    return out


def kernel(x, ffn1_norm, ffn1_w_gu, ffn1_w_down, mix_norm, ffn2_norm, ffn2_w_gu, ffn2_w_down, a_w_in, a_w_conv, a_A_log, a_dt_bias, a_out_norm, a_w_out, b_w_in, b_b_in, b_sinks, b_w_out, b_b_out, final_norm, loss_target, m_ffn1_norm, m_ffn1_w_gu, m_ffn1_w_down, m_mix_norm, m_ffn2_norm, m_ffn2_w_gu, m_ffn2_w_down, m_a_w_in, m_a_w_conv, m_a_A_log, m_a_dt_bias, m_a_out_norm, m_a_w_out, m_b_w_in, m_b_b_in, m_b_sinks, m_b_w_out, m_b_b_out, m_final_norm, v_ffn1_norm, v_ffn1_w_gu, v_ffn1_w_down, v_mix_norm, v_ffn2_norm, v_ffn2_w_gu, v_ffn2_w_down, v_a_w_in, v_a_w_conv, v_a_A_log, v_a_dt_bias, v_a_out_norm, v_a_w_out, v_b_w_in, v_b_b_in, v_b_sinks, v_b_w_out, v_b_b_out, v_final_norm):
    w = dict(ffn1_norm=ffn1_norm, ffn1_w_gu=ffn1_w_gu, ffn1_w_down=ffn1_w_down, mix_norm=mix_norm, ffn2_norm=ffn2_norm, ffn2_w_gu=ffn2_w_gu, ffn2_w_down=ffn2_w_down, a_w_in=a_w_in, a_w_conv=a_w_conv, a_A_log=a_A_log, a_dt_bias=a_dt_bias, a_out_norm=a_out_norm, a_w_out=a_w_out, b_w_in=b_w_in, b_b_in=b_b_in, b_sinks=b_sinks, b_w_out=b_w_out, b_b_out=b_b_out, final_norm=final_norm)
    m = dict(ffn1_norm=m_ffn1_norm, ffn1_w_gu=m_ffn1_w_gu, ffn1_w_down=m_ffn1_w_down, mix_norm=m_mix_norm, ffn2_norm=m_ffn2_norm, ffn2_w_gu=m_ffn2_w_gu, ffn2_w_down=m_ffn2_w_down, a_w_in=m_a_w_in, a_w_conv=m_a_w_conv, a_A_log=m_a_A_log, a_dt_bias=m_a_dt_bias, a_out_norm=m_a_out_norm, a_w_out=m_a_w_out, b_w_in=m_b_w_in, b_b_in=m_b_b_in, b_sinks=m_b_sinks, b_w_out=m_b_w_out, b_b_out=m_b_b_out, final_norm=m_final_norm)
    v = dict(ffn1_norm=v_ffn1_norm, ffn1_w_gu=v_ffn1_w_gu, ffn1_w_down=v_ffn1_w_down, mix_norm=v_mix_norm, ffn2_norm=v_ffn2_norm, ffn2_w_gu=v_ffn2_w_gu, ffn2_w_down=v_ffn2_w_down, a_w_in=v_a_w_in, a_w_conv=v_a_w_conv, a_A_log=v_a_A_log, a_dt_bias=v_a_dt_bias, a_out_norm=v_a_out_norm, a_w_out=v_a_w_out, b_w_in=v_b_w_in, b_b_in=v_b_b_in, b_sinks=v_b_sinks, b_w_out=v_b_w_out, b_b_out=v_b_b_out, final_norm=v_final_norm)
    T = x.shape[1]
    x0, tgt = x.reshape(T, D), loss_target.reshape(T, D)

    def cast(t):
        return t.astype(BF16)

    h0, t0 = gather_start([cast(ffn1_w_gu[0])], "gather0_start")
    a_log_row = jnp.zeros((1, 128), F32).at[0, HEADS_A:2 * HEADS_A].set(a_A_log[0])
    dt_row = jnp.zeros((1, 128), F32).at[0, HEADS_A:2 * HEADS_A].set(a_dt_bias[0])
    sink_row = jnp.zeros((1, 128), F32).at[0, :b_sinks.shape[1]].set(b_sinks[0])
    a_in_cols = a_w_in.shape[-1] * N_DEV

    def down_blocks(t):
        return t.reshape(N_FB, FB, D)

    wgu, wdn, saved = {}, {}, []
    xn = rmsnorm_bf16(x0, ffn1_norm[0:1], "l0_ffn1_norm", (t0,))
    wgu["ffn1", 0] = push_wait(h0, xn, "gather0_wait", scatter=False)[0]
    h0d, t0d = gather_start([cast(ffn1_w_down[0])], "gather0d_start", deps=(wgu["ffn1", 0],))
    h1, t1 = gather_start([cast(a_w_in[0]), a_w_conv[0], cast(a_w_out[0])], "gather1_start", deps=(t0d,))
    gu = ffn_up(xn, wgu["ffn1", 0], "l0_ffn1_up", deps=(t0d, t1))
    wdn["ffn1", 0] = down_blocks(push_wait(h0d, gu, "gather0d_wait", scatter=False)[0])
    xs, s1 = ffn_down(gu, wdn["ffn1", 0], x0, "l0_ffn1_down"), (x0, xn, gu)
    got = push_wait(h1, xs, "gather1_wait", scatter=False)
    h1f, t1f = gather_start([cast(ffn2_w_gu[0]), cast(ffn2_w_down[0])], "gather1f_start", deps=(got[0],))
    g2 = [cast(ffn1_w_gu[1]), cast(ffn1_w_down[1]), cast(b_w_in[0]), b_b_in, cast(b_w_out[0]), b_b_out,
          cast(ffn2_w_gu[1]), cast(ffn2_w_down[1])]
    h2, t2 = gather_start(g2, "gather2_start", deps=(t1f,))
    a_in_full = jnp.pad(got[0].transpose(1, 0, 2).reshape(D, a_in_cols), ((0, 0), (0, A_COLS - a_in_cols)))
    gdn_args = (mix_norm[0:1], a_in_full, got[1].transpose(1, 0, 2).reshape(4, 3 * D), a_log_row, dt_row, a_out_norm,
                got[2].reshape(D, D))
    xs, sm = gdn_forward(xs, *gdn_args, "gdn", deps=(t1f, t2))
    got = push_wait(h1f, xs, "gather1f_wait", scatter=False)
    wgu["ffn2", 0], wdn["ffn2", 0] = got[0], down_blocks(got[1])
    xs, s2 = ffn_forward(xs, ffn2_norm[0:1], wgu["ffn2", 0], wdn["ffn2", 0], "l0_ffn2")
    saved.append((s1, sm, s2))
    got = push_wait(h2, xs, "gather2_wait", scatter=False)
    wgu["ffn1", 1], wdn["ffn1", 1] = got[0], down_blocks(got[1])
    swa_args = (mix_norm[1:2], got[2].transpose(1, 0, 2).reshape(D, B_COLS), got[3].reshape(1, B_COLS), sink_row,
                got[4].reshape(D, D), got[5].reshape(1, D))
    wgu["ffn2", 1], wdn["ffn2", 1] = got[6], down_blocks(got[7])
    xs, s1 = ffn_forward(xs, ffn1_norm[1:2], wgu["ffn1", 1], wdn["ffn1", 1], "l1_ffn1")
    xs, sm = swa_forward(xs, *swa_args, "swa")
    xs, s2 = ffn_forward(xs, ffn2_norm[1:2], wgu["ffn2", 1], wdn["ffn2", 1], "l1_ffn2")
    saved.append((s1, sm, s2))
    loss_row, dx, d_final_norm = final_loss(xs, final_norm.reshape(1, D), tgt, "final_loss")

    def down_slots(t):
        return cast(t.reshape(N_DEV, FB // 2, D))

    def col_slots(t, dtype=BF16):
        return t.reshape(t.shape[0], N_DEV, -1).transpose(1, 0, 2).astype(dtype)

    d_norm = {"ffn1_norm": [None, None], "mix_norm": [None, None], "ffn2_norm": [None, None]}
    s1, sm, s2 = saved[1]
    dx, d_norm["ffn2_norm"][1], d_gu, d_dn = ffn_backward(dx, s2, ffn2_norm[1:2], wgu["ffn2", 1], wdn["ffn2", 1], "l1_ffn2")
    sent1 = [cast(d_gu), down_slots(d_dn)]
    dx, d_norm["mix_norm"][1], d_b_in, d_b_bias_in, d_sinks, d_b_out, d_b_bias_out = swa_backward(dx, sm, *swa_args, "swa")
    sent1 += [col_slots(d_b_in), d_b_bias_in.reshape(N_DEV, 1, -1), cast(d_b_out.reshape(N_DEV, D // N_DEV, D)),
              d_b_bias_out.reshape(N_DEV, 1, -1)]
    dx, d_norm["ffn1_norm"][1], d_gu, d_dn = ffn_backward(dx, s1, ffn1_norm[1:2], wgu["ffn1", 1], wdn["ffn1", 1], "l1_ffn1")
    sent1 += [cast(d_gu), down_slots(d_dn)]
    x1, tx1 = exchange_start(sent1, "exchange1_start")

    s1, sm, s2 = saved[0]
    dx, d_norm["ffn2_norm"][0], d_gu, d_dn = ffn_backward(dx, s2, ffn2_norm[0:1], wgu["ffn2", 0], wdn["ffn2", 0], "l0_ffn2",
                                                           deps=(tx1,))
    sent2 = [cast(d_gu), down_slots(d_dn)]
    dx, d_norm["mix_norm"][0], d_a_in, d_a_conv, d_alog, d_dt, d_onorm, d_a_out = gdn_backward(dx, sm, *gdn_args, "gdn")
    sent2 += [col_slots(d_a_in[:, :a_in_cols]), col_slots(d_a_conv, F32), cast(d_a_out.reshape(N_DEV, D // N_DEV, D))]
    x2, tx2 = exchange_start(sent2, "exchange2_start")
    last = {}

    def send_last(d_gu, d_dn):
        last["handles"], token = exchange_start([cast(d_gu), down_slots(d_dn)], "exchange3_start")
        return (token,)

    dx, d_norm["ffn1_norm"][0], _, _ = ffn_backward(dx, s1, ffn1_norm[0:1], wgu["ffn1", 0], wdn["ffn1", 0], "l0_ffn1",
                                                    deps=(tx2,), on_grads=send_last)
    grad_x = dx.reshape(x.shape)
    r1 = push_wait(x1, dx, "exchange1_wait", scatter=True)
    r2 = push_wait(x2, dx, "exchange2_wait", scatter=True)
    received = dict(ffn2_w_gu=[r2[0], r1[0]], ffn2_w_down=[r2[1], r1[1]],
                    b_w_in=[r1[2]], b_b_in=[r1[3]], b_w_out=[r1[4]], b_b_out=[r1[5]],
                    a_w_in=[r2[2]], a_w_conv=[r2[3]], a_w_out=[r2[4]])

    grads, deltas, new_m, new_v = {}, {}, {}, {}

    def update(key):
        shape = w[key].shape
        cols = shape[-1]
        layers = lambda t: t.reshape(shape[0], -1, cols)
        out = adam_update([r.reshape(N_DEV, -1, cols) for r in received[key]], layers(w[key]), layers(m[key]), layers(v[key]),
                          f"adam_{key}")
        grads[key], deltas[key], new_m[key], new_v[key] = (t.reshape(shape) for t in out)

    for key in SHARDED:
        if key in received:
            update(key)
    done_first = [deltas[key] for key in received]

    small = dict(ffn1_norm=jnp.concatenate(d_norm["ffn1_norm"], axis=0), mix_norm=jnp.concatenate(d_norm["mix_norm"], axis=0),
                 ffn2_norm=jnp.concatenate(d_norm["ffn2_norm"], axis=0), final_norm=d_final_norm,
                 a_A_log=d_alog[0, HEADS_A:2 * HEADS_A], a_dt_bias=d_dt[0, HEADS_A:2 * HEADS_A],
                 b_sinks=d_sinks[0, :b_sinks.shape[1]], a_out_norm=d_onorm, loss=loss_row[0, 0])
    hs, ts = gather_start([_pack_small(small)], "gather_small_start")
    r3 = push_wait(last["handles"], done_first + [ts], "exchange3_wait", scatter=True)
    received.update(ffn1_w_gu=[r3[0], r1[6]], ffn1_w_down=[r3[1], r1[7]])
    update("ffn1_w_gu")
    update("ffn1_w_down")
    every = push_wait(hs, deltas["ffn1_w_down"], "gather_small_wait", scatter=False)[0]
    out = adam_update([every], _pack_small(w)[None], _pack_small(m)[None], _pack_small(v)[None], "adam_small")
    for dst, packed in zip((grads, deltas, new_m, new_v), out):
        dst.update(_unpack_small(packed[0], w))
    loss = out[0][0, 7, LOSS_LANE]

    return (loss, grad_x, *[grads[k_] for k_ in WEIGHTS], *[deltas[k_] for k_ in WEIGHTS],
            *[new_m[k_] for k_ in WEIGHTS], *[new_v[k_] for k_ in WEIGHTS])
```

```python
import functools

import jax
import jax.numpy as jnp
from jax import lax
from jax.experimental import pallas as pl
from jax.experimental.pallas import tpu as pltpu

F32, BF16 = jnp.float32, jnp.bfloat16
HI = lax.Precision.HIGHEST
EPS = 1e-6

N_DEV = 8
D = 1024
FB = 704
N_FB = 4
HEADS_A, DK = 8, 128
CHUNK = 64
PREP_T = 512
A_COLS = 4224
B_HD, B_BLK = 64, 128
VMEM_LIMIT_V7X = 60 * 1024 * 1024

ADAM_LR, ADAM_B1, ADAM_B2, ADAM_EPS, ADAM_WD, ADAM_STEP = 0.001, 0.9, 0.999, 1e-08, 0.01, 10

NT = (((1,), (1,)), ((), ()))
TN = (((0,), (0,)), ((), ()))


def _pc(body, *, name, out_shape, grid=(), in_specs=None, out_specs=None, scratch=(), sem=None, **kw):
    params = pltpu.CompilerParams(dimension_semantics=sem, vmem_limit_bytes=VMEM_LIMIT_V7X)
    return pl.pallas_call(body, name=name, out_shape=out_shape, grid=grid, in_specs=in_specs, out_specs=out_specs,
                          scratch_shapes=list(scratch), compiler_params=params, **kw)


def _sds(shape, dtype):
    return jax.ShapeDtypeStruct(tuple(shape), dtype)


def _dot(a, b, dims=None, precision=None):
    if dims is None:
        return jnp.dot(a, b, preferred_element_type=F32, precision=precision)
    return lax.dot_general(a, b, dims, preferred_element_type=F32, precision=precision)


def _sigmoid(x):
    return 1.0 / (1.0 + jnp.exp(-x))


def _softplus(x):
    return jnp.maximum(x, 0.0) + jnp.log(1.0 + jnp.exp(-jnp.abs(x)))


def _rms_fwd(x, w):
    r = lax.rsqrt(jnp.mean(x * x, axis=-1, keepdims=True) + EPS)
    return x * r * w


def _rms_bwd(x, w, dy):
    r = lax.rsqrt(jnp.mean(x * x, axis=-1, keepdims=True) + EPS)
    xh = x * r
    dxh = dy * w
    dx = r * (dxh - xh * jnp.mean(dxh * xh, axis=-1, keepdims=True))
    return dx, jnp.sum(dy * xh, axis=0, keepdims=True)


def _tile(n, want):
    t = min(n, want)
    assert n % t == 0, (n, want)
    return t


def rmsnorm_bf16(x, w, name, deps=()):
    T = x.shape[0]
    tm = _tile(T, 1024)

    def body(x_ref, w_ref, *rest):
        rest[-1][...] = _rms_fwd(x_ref[...], w_ref[...]).astype(BF16)

    return _pc(body, name=name, out_shape=_sds((T, D), BF16), grid=(T // tm,),
               in_specs=[pl.BlockSpec((tm, D), lambda i: (i, 0)), pl.BlockSpec((1, D), lambda i: (0, 0))] + [DEP_SPEC] * len(deps),
               out_specs=pl.BlockSpec((tm, D), lambda i: (i, 0)), sem=("parallel",))(x, w, *deps)


def rmsnorm_bwd_add(x, w, dxn, dres, name):
    T = x.shape[0]
    tm = _tile(T, 512)

    def body(x_ref, w_ref, dxn_ref, dres_ref, dx_ref, dw_ref):
        dx, dw = _rms_bwd(x_ref[...], w_ref[...], dxn_ref[...])
        dx_ref[...] = dres_ref[...] + dx

        @pl.when(pl.program_id(0) == 0)
        def _():
            dw_ref[...] = jnp.zeros_like(dw_ref)
        dw_ref[...] += dw

    row = pl.BlockSpec((tm, D), lambda i: (i, 0))
    vec = pl.BlockSpec((1, D), lambda i: (0, 0))
    return _pc(body, name=name, out_shape=(_sds((T, D), F32), _sds((1, D), F32)), grid=(T // tm,),
               in_specs=[row, vec, row, row], out_specs=(row, vec), sem=("arbitrary",))(x, w, dxn, dres)


def final_loss(x, w, tgt, name):
    T = x.shape[0]
    tm = _tile(T, 512)

    def body(x_ref, w_ref, t_ref, loss_ref, dx_ref, dw_ref):
        xv, wv = x_ref[...], w_ref[...]
        err = _rms_fwd(xv, wv) - t_ref[...]
        dx, dw = _rms_bwd(xv, wv, err * (1.0 / D))
        dx_ref[...] = dx

        @pl.when(pl.program_id(0) == 0)
        def _():
            dw_ref[...] = jnp.zeros_like(dw_ref)
            loss_ref[...] = jnp.zeros_like(loss_ref)
        dw_ref[...] += dw
        loss_ref[...] += jnp.full((1, 128), 0.5 / D, F32) * jnp.sum(err * err)

    row = pl.BlockSpec((tm, D), lambda i: (i, 0))
    vec = pl.BlockSpec((1, D), lambda i: (0, 0))
    return _pc(body, name=name, out_shape=(_sds((1, 128), F32), _sds((T, D), F32), _sds((1, D), F32)),
               grid=(T // tm,), in_specs=[row, vec, row],
               out_specs=(pl.BlockSpec((1, 128), lambda i: (0, 0)), row, vec), sem=("arbitrary",))(x, w, tgt)


def _col_tile(n):
    for t in (1536, 1408, 1024, 768, 512, 384, 256, 128):
        if n % t == 0:
            return t
    return n


def mm_nn(a, b, name, bias=None, residual=None, out_dtype=F32, cols=None):
    T, K = a.shape
    first, end = cols or (0, b.shape[1])
    N = end - first
    tm, tn = _tile(T, 512), _col_tile(N)
    assert first % tn == 0 and (cols is None or (bias is None and residual is None))
    j0 = first // tn

    def body(a_ref, b_ref, *rest):
        o_ref = rest[-1]
        acc = _dot(a_ref[...].astype(BF16), b_ref[...])
        for extra in rest[:-1]:
            acc = acc + extra[...]
        o_ref[...] = acc.astype(out_dtype)

    in_specs = [pl.BlockSpec((tm, K), lambda j, i: (i, 0)), pl.BlockSpec((K, tn), lambda j, i: (0, j0 + j))]
    args = [a, b]
    if bias is not None:
        in_specs.append(pl.BlockSpec((1, tn), lambda j, i: (0, j)))
        args.append(bias)
    if residual is not None:
        in_specs.append(pl.BlockSpec((tm, tn), lambda j, i: (i, j)))
        args.append(residual)
    return _pc(body, name=name, out_shape=_sds((T, N), out_dtype), grid=(N // tn, T // tm), in_specs=in_specs,
               out_specs=pl.BlockSpec((tm, tn), lambda j, i: (i, j)), sem=("parallel", "parallel"))(*args)


def mm_nt(a, b, name, out_dtype=F32):
    T, N = a.shape
    K = b.shape[0]
    tm = _tile(T, 512)

    def body(a_ref, b_ref, o_ref):
        o_ref[...] = _dot(a_ref[...].astype(BF16), b_ref[...], NT).astype(out_dtype)

    return _pc(body, name=name, out_shape=_sds((T, K), out_dtype), grid=(T // tm,),
               in_specs=[pl.BlockSpec((tm, N), lambda i: (i, 0)), pl.BlockSpec((K, N), lambda i: (0, 0))],
               out_specs=pl.BlockSpec((tm, K), lambda i: (i, 0)), sem=("parallel",))(a, b)


def mm_tn(a, b, name):
    T, K = a.shape
    N = b.shape[1]
    tt, tn = _tile(T, 1024), _col_tile(N)

    def body(a_ref, b_ref, o_ref):
        @pl.when(pl.program_id(1) == 0)
        def _():
            o_ref[...] = jnp.zeros_like(o_ref)
        o_ref[...] += _dot(a_ref[...].astype(BF16), b_ref[...].astype(BF16), TN)

    return _pc(body, name=name, out_shape=_sds((K, N), F32), grid=(N // tn, T // tt),
               in_specs=[pl.BlockSpec((tt, K), lambda j, t: (t, 0)), pl.BlockSpec((tt, tn), lambda j, t: (t, j))],
               out_specs=pl.BlockSpec((K, tn), lambda j, t: (0, j)), sem=("parallel", "arbitrary"))(a, b)


def ffn_up(xn, wgu, name, deps=()):
    T = xn.shape[0]
    tm = _tile(T, 1024)

    def body(x_ref, w_ref, *rest):
        xv = x_ref[...]
        for j in range(2 * N_FB):
            rest[-1][j] = _dot(xv, w_ref[j]).astype(BF16)

    return _pc(body, name=name, out_shape=_sds((2 * N_FB, T, FB), BF16), grid=(T // tm,),
               in_specs=[pl.BlockSpec((tm, D), lambda i: (i, 0)), _resident((2 * N_FB, D, FB))] + [DEP_SPEC] * len(deps),
               out_specs=pl.BlockSpec((2 * N_FB, tm, FB), lambda i: (0, i, 0)), sem=("parallel",))(xn, wgu, *deps)


def ffn_down(gu, wd, x, name):
    T = x.shape[0]
    tm = _tile(T, 512)

    def body(gu_ref, w_ref, x_ref, o_ref):
        acc = jnp.zeros((tm, D), F32)
        for g in range(N_FB):
            gate, up = gu_ref[g], gu_ref[N_FB + g]
            acc = acc + _dot(gate * _sigmoid(gate) * up, w_ref[g])
        o_ref[...] = x_ref[...] + 0.5 * acc

    row = pl.BlockSpec((tm, D), lambda i: (i, 0))
    return _pc(body, name=name, out_shape=_sds((T, D), F32), grid=(T // tm,),
               in_specs=[pl.BlockSpec((2 * N_FB, tm, FB), lambda i: (0, i, 0)),
                         _resident((N_FB, FB, D)), row],
               out_specs=row, sem=("parallel",))(gu, wd, x)


def _resident(shape):
    return pl.BlockSpec(shape, lambda *_: (0,) * len(shape), pipeline_mode=pl.Buffered(1))


def _store_blocks_bf16(acc, out_hbm, stage, sem):
    for j in range(acc.shape[0]):
        stage[...] = acc[j].astype(BF16)
        copy = pltpu.make_async_copy(stage, out_hbm.at[j], sem)
        copy.start()
        copy.wait()


def ffn_bwd_hidden(dout, wd, gu, name, deps=()):
    T = dout.shape[0]
    tm = _tile(T, 512)
    n_t = T // tm

    def body(d_ref, w_ref, gu_ref, *rest):
        dgu_ref, dwd_hbm, acc, stage, sem = rest[-5:]
        t = pl.program_id(0)

        @pl.when(t == 0)
        def _():
            acc[...] = jnp.zeros_like(acc)
        dy = (0.5 * d_ref[...]).astype(BF16)
        for g in range(N_FB):
            gate, up = gu_ref[g], gu_ref[N_FB + g]
            sg = _sigmoid(gate)
            silu = gate * sg
            dact = _dot(dy, w_ref[g], NT).astype(BF16)
            acc[g] += _dot(silu * up, dy, TN)
            dgu_ref[g] = dact * up * (sg * (1.0 + gate * (1.0 - sg)))
            dgu_ref[N_FB + g] = dact * silu

        @pl.when(t == n_t - 1)
        def _():
            _store_blocks_bf16(acc, dwd_hbm, stage, sem)

    return _pc(body, name=name, out_shape=(_sds((2 * N_FB, T, FB), BF16), _sds((N_FB, FB, D), BF16)), grid=(n_t,),
               in_specs=[pl.BlockSpec((tm, D), lambda i: (i, 0)), _resident((N_FB, FB, D)),
                         pl.BlockSpec((2 * N_FB, tm, FB), lambda i: (0, i, 0))] + [DEP_SPEC] * len(deps),
               out_specs=(pl.BlockSpec((2 * N_FB, tm, FB), lambda i: (0, i, 0)), pl.BlockSpec(memory_space=pl.ANY)),
               scratch=[pltpu.VMEM((N_FB, FB, D), F32), pltpu.VMEM((FB, D), BF16), pltpu.SemaphoreType.DMA],
               sem=("arbitrary",))(dout, wd, gu, *deps)


def ffn_bwd_input(dgu, wgu, x, dout, nw, name, deps=()):
    T = x.shape[0]
    tm = _tile(T, 512)

    def body(dgu_ref, w_ref, x_ref, d_ref, nw_ref, *rest):
        dx_ref, dnw_ref = rest[-2:]
        dxn = jnp.zeros((tm, D), F32)
        for j in range(2 * N_FB):
            dxn = dxn + _dot(dgu_ref[j], w_ref[j], NT)
        dx, dw = _rms_bwd(x_ref[...], nw_ref[...], dxn)
        dx_ref[...] = d_ref[...] + dx

        @pl.when(pl.program_id(0) == 0)
        def _():
            dnw_ref[...] = jnp.zeros_like(dnw_ref)
        dnw_ref[...] += dw

    row = pl.BlockSpec((tm, D), lambda i: (i, 0))
    vec = pl.BlockSpec((1, D), lambda i: (0, 0))
    return _pc(body, name=name, out_shape=(_sds((T, D), F32), _sds((1, D), F32)), grid=(T // tm,),
               in_specs=[pl.BlockSpec((2 * N_FB, tm, FB), lambda i: (0, i, 0)), _resident((2 * N_FB, D, FB)),
                         row, row, vec] + [DEP_SPEC] * len(deps),
               out_specs=(row, vec), sem=("arbitrary",))(dgu, wgu, x, dout, nw, *deps)


def ffn_wgrad_gu(xn, dgu, name):
    T = xn.shape[0]
    tt = _tile(T, 1024)
    n_t = T // tt

    def body(x_ref, d_ref, dw_hbm, acc, stage, sem):
        t = pl.program_id(0)

        @pl.when(t == 0)
        def _():
            acc[...] = jnp.zeros_like(acc)
        xn_tile = x_ref[...]
        for j in range(2 * N_FB):
            acc[j] += _dot(xn_tile, d_ref[j], TN)

        @pl.when(t == n_t - 1)
        def _():
            _store_blocks_bf16(acc, dw_hbm, stage, sem)

    return _pc(body, name=name, out_shape=_sds((2 * N_FB, D, FB), BF16), grid=(n_t,),
               in_specs=[pl.BlockSpec((tt, D), lambda t: (t, 0)), pl.BlockSpec((2 * N_FB, tt, FB), lambda t: (0, t, 0))],
               out_specs=pl.BlockSpec(memory_space=pl.ANY),
               scratch=[pltpu.VMEM((2 * N_FB, D, FB), F32), pltpu.VMEM((D, FB), BF16), pltpu.SemaphoreType.DMA],
               sem=("arbitrary",))(xn, dgu)


def ffn_forward(x, nw, wgu, wd, tag, deps=()):
    xn = rmsnorm_bf16(x, nw, f"{tag}_norm", deps)
    gu = ffn_up(xn, wgu, f"{tag}_up")
    return ffn_down(gu, wd, x, f"{tag}_down"), (x, xn, gu)


def ffn_backward(dout, saved, nw, wgu, wd, tag, deps=(), on_grads=None):
    x, xn, gu = saved
    dgu, dwd = ffn_bwd_hidden(dout, wd, gu, f"{tag}_bwd_hidden", deps)
    dwgu = ffn_wgrad_gu(xn, dgu, f"{tag}_wgrad_gu")
    late = on_grads(dwgu, dwd) if on_grads else ()
    dx, dnw = ffn_bwd_input(dgu, wgu, x, dout, nw, f"{tag}_bwd_input", late)
    return dx, dnw, dwgu, dwd


N_QKV_BLK = 3 * HEADS_A
Z_BLK0 = N_QKV_BLK
MAIN_COLS = 4 * D
HALO = 16


def _conv_taps(xcat, w):
    c = xcat[HALO:] * w[3:4]
    for k in range(3):
        c = c + pltpu.roll(xcat, 3 - k, 0)[HALO:] * w[k:k + 1]
    return c


def _head_cols(h):
    return slice(128 * h, 128 * (h + 1))


def gdn_conv_fwd(proj, wconv, name):
    T = proj.shape[0]
    tm = _tile(T, 512)

    def body(cur_ref, prev_ref, w_ref, c_ref, y_ref):
        kind, t = pl.program_id(0), pl.program_id(1)
        prev = jnp.where(t > 0, prev_ref[...].astype(F32), 0.0)
        c = _conv_taps(jnp.concatenate([prev, cur_ref[...].astype(F32)], axis=0), w_ref[...])
        c_ref[...] = c.astype(BF16)
        s = c * _sigmoid(c)
        scale = jnp.where(kind == 0, DK ** -0.5, 1.0)
        for h in range(HEADS_A):
            sh = s[:, _head_cols(h)]
            r = lax.rsqrt(jnp.sum(sh * sh, axis=-1, keepdims=True) + EPS)
            y_ref[h] = (sh * jnp.where(kind < 2, r * scale, 1.0)).astype(BF16)

    return _pc(body, name=name, out_shape=(_sds((T, 3 * D), BF16), _sds((N_QKV_BLK, T, 128), BF16)),
               grid=(3, T // tm),
               in_specs=[pl.BlockSpec((tm, D), lambda kd, t: (t, kd)),
                         pl.BlockSpec((HALO, D), lambda kd, t: (jnp.maximum(t * (tm // HALO) - 1, 0), kd)),
                         pl.BlockSpec((4, D), lambda kd, t: (0, kd))],
               out_specs=(pl.BlockSpec((tm, D), lambda kd, t: (t, kd)),
                          pl.BlockSpec((HEADS_A, tm, 128), lambda kd, t: (kd, t, 0))),
               sem=("parallel", "parallel"))(proj, proj, wconv)


def gdn_conv_bwd(dqkv, c, proj, wconv, name):
    T = c.shape[0]
    tm = _tile(T, 512)
    n_t = T // tm

    def body(dy_ref, dyn_ref, c_ref, cn_ref, x_ref, xp_ref, w_ref, dx_ref, dw_ref):
        kind, t = pl.program_id(0), pl.program_id(1)
        scale = jnp.where(kind == 0, DK ** -0.5, 1.0)

        def act_bwd(dy, cv):
            sg = _sigmoid(cv)
            s = cv * sg
            parts = []
            for h in range(HEADS_A):
                sh, dyh = s[:, _head_cols(h)], dy[h]
                r = lax.rsqrt(jnp.sum(sh * sh, axis=-1, keepdims=True) + EPS)
                ds_norm = scale * r * (dyh - (r * r) * sh * jnp.sum(dyh * sh, axis=-1, keepdims=True))
                parts.append(jnp.where(kind < 2, ds_norm, dyh))
            return jnp.concatenate(parts, axis=1) * (sg * (1.0 + cv * (1.0 - sg)))

        w = w_ref[...]
        dcur = act_bwd(dy_ref[...].astype(F32), c_ref[...].astype(F32))
        dnext = jnp.where(t < n_t - 1, act_bwd(dyn_ref[...].astype(F32), cn_ref[...].astype(F32)), 0.0)
        dcat = jnp.concatenate([dcur, dnext], axis=0)
        dx = dcur * w[3:4]
        for k in range(3):
            dx = dx + pltpu.roll(dcat, tm + HALO - (3 - k), 0)[:tm] * w[k:k + 1]
        dx_ref[...] = dx.astype(BF16)
        xprev = jnp.where(t > 0, xp_ref[...].astype(F32), 0.0)
        xcat = jnp.concatenate([xprev, x_ref[...].astype(F32)], axis=0)
        rows = [jnp.sum(dcur * pltpu.roll(xcat, 3 - k, 0)[HALO:], axis=0, keepdims=True) for k in range(3)]
        rows.append(jnp.sum(dcur * xcat[HALO:], axis=0, keepdims=True))

        @pl.when(t == 0)
        def _():
            dw_ref[...] = jnp.zeros_like(dw_ref)
        dw_ref[...] += jnp.concatenate(rows, axis=0)

    def nxt(t):
        return jnp.minimum((t + 1) * (tm // HALO), T // HALO - 1)

    cur = pl.BlockSpec((tm, D), lambda kd, t: (t, kd))
    return _pc(body, name=name, out_shape=(_sds((T, 3 * D), BF16), _sds((4, 3 * D), F32)), grid=(3, n_t),
               in_specs=[pl.BlockSpec((HEADS_A, tm, 128), lambda kd, t: (kd, t, 0)),
                         pl.BlockSpec((HEADS_A, HALO, 128), lambda kd, t: (kd, nxt(t), 0)),
                         cur, pl.BlockSpec((HALO, D), lambda kd, t: (nxt(t), kd)),
                         cur, pl.BlockSpec((HALO, D), lambda kd, t: (jnp.maximum(t * (tm // HALO) - 1, 0), kd)),
                         pl.BlockSpec((4, D), lambda kd, t: (0, kd))],
               out_specs=(cur, pl.BlockSpec((4, D), lambda kd, t: (0, kd))),
               sem=("parallel", "arbitrary"))(dqkv, dqkv, c, c, proj, proj, wconv)


def _chunk_masks(n):
    ri = lax.broadcasted_iota(jnp.int32, (n, n), 0)
    ci = lax.broadcasted_iota(jnp.int32, (n, n), 1)
    same = (ri // CHUNK) == (ci // CHUNK)
    return same & (ri >= ci), same & (ri <= ci)


def gdn_gate_fwd(ba, al, dtb, name):
    T = ba.shape[0]
    tg = _tile(T, PREP_T)

    def body(ba_ref, al_ref, dtb_ref, o_ref):
        x = ba_ref[...]
        lane = lax.broadcasted_iota(jnp.int32, x.shape, 1)
        is_a = (lane >= HEADS_A) & (lane < 2 * HEADS_A)
        g = jnp.where(is_a, -jnp.exp(al_ref[...]) * _softplus(x + dtb_ref[...]), 0.0)
        lower, _ = _chunk_masks(tg)
        gc = _dot(lower.astype(F32), g, precision=HI)
        o_ref[...] = jnp.where(lane < HEADS_A, _sigmoid(x), gc)

    vec = pl.BlockSpec((1, 128), lambda i: (0, 0))
    return _pc(body, name=name, out_shape=_sds((T, 128), F32), grid=(T // tg,),
               in_specs=[pl.BlockSpec((tg, 128), lambda i: (i, 0)), vec, vec],
               out_specs=pl.BlockSpec((tg, 128), lambda i: (i, 0)), sem=("parallel",))(ba, al, dtb)


def gdn_gate_bwd(ba, al, dtb, dgb, name):
    T = ba.shape[0]
    tg = _tile(T, PREP_T)

    def body(ba_ref, al_ref, dtb_ref, dgb_ref, dba_ref, dal_ref, ddt_ref):
        x, d = ba_ref[...], dgb_ref[...]
        lane = lax.broadcasted_iota(jnp.int32, x.shape, 1)
        is_b = lane < HEADS_A
        is_a = (lane >= HEADS_A) & (lane < 2 * HEADS_A)
        beta = _sigmoid(x)
        e_a = jnp.exp(al_ref[...])
        z = x + dtb_ref[...]
        g = jnp.where(is_a, -e_a * _softplus(z), 0.0)
        _, upper = _chunk_masks(tg)
        dg = _dot(upper.astype(F32), jnp.where(is_a, d, 0.0), precision=HI)
        da = jnp.where(is_a, dg * (-e_a) * _sigmoid(z), 0.0)
        db = jnp.where(is_b, d * beta * (1.0 - beta), 0.0)
        dba_ref[...] = (da + db).astype(BF16)

        @pl.when(pl.program_id(0) == 0)
        def _():
            dal_ref[...] = jnp.zeros_like(dal_ref)
            ddt_ref[...] = jnp.zeros_like(ddt_ref)
        dal_ref[...] += jnp.sum(dg * g, axis=0, keepdims=True)
        ddt_ref[...] += jnp.sum(da, axis=0, keepdims=True)

    vec = pl.BlockSpec((1, 128), lambda i: (0, 0))
    blk = pl.BlockSpec((tg, 128), lambda i: (i, 0))
    return _pc(body, name=name, out_shape=(_sds((T, 128), BF16), _sds((1, 128), F32), _sds((1, 128), F32)),
               grid=(T // tg,), in_specs=[blk, vec, vec, blk],
               out_specs=(blk, vec, vec), sem=("arbitrary",))(ba, al, dtb, dgb)


def _bmm(a, b, dims, precision=None):
    return lax.dot_general(a, b, dims, preferred_element_type=F32, precision=precision)


B_NN = (((2,), (1,)), ((0,), (0,)))
B_NT = (((2,), (2,)), ((0,), (0,)))


def _select_lane(x, lane_index):
    lane = lax.broadcasted_iota(jnp.int32, x.shape, x.ndim - 1)
    return jnp.sum(jnp.where(lane == lane_index, x, 0.0), axis=-1, keepdims=True)


B_TN = (((1,), (1,)), ((0,), (0,)))


def _bmm_split(a, b, dims):
    ah, bh = a.astype(BF16), b.astype(BF16)
    al, bl = (a - ah.astype(F32)).astype(BF16), (b - bh.astype(F32)).astype(BF16)
    return _bmm(ah, bh, dims) + (_bmm(ah, bl, dims) + _bmm(al, bh, dims))


@jax.custom_vjp
def _bmm_f32(a, b):
    return _bmm_split(a, b, B_NN)


def _bmm_f32_fwd(a, b):
    return _bmm_split(a, b, B_NN), (a, b)


def _bmm_bf16(a, b, dims):
    return _bmm(a.astype(BF16), b.astype(BF16), dims)


def _bmm_f32_bwd(res, dc):
    a, b = res
    return _bmm_bf16(dc, b, B_NT), _bmm_bf16(a, dc, B_TN)


_bmm_f32.defvjp(_bmm_f32_fwd, _bmm_f32_bwd)


def _tri_inverse(lmat):
    ri = lax.broadcasted_iota(jnp.int32, lmat.shape, 1)
    ci = lax.broadcasted_iota(jnp.int32, lmat.shape, 2)
    eye = jnp.where(ri == ci, 1.0, 0.0)
    inv = eye - lmat
    power = lmat
    for _ in range(5):
        power = _bmm_bf16(power, power, B_NN)
        inv = inv + _bmm_bf16(inv, power, B_NN)
    return _bmm_split(inv, 2.0 * eye - _bmm_split(eye + lmat, inv, B_NN), B_NN)


def _stored_inverse(x):
    @jax.custom_vjp
    def inverse(lmat):
        return x

    def fwd(lmat):
        return x, None

    def bwd(_, dx):
        return (-_bmm_bf16(_bmm_bf16(x, dx, B_TN), x, B_NT),)

    inverse.defvjp(fwd, bwd)
    return inverse


def _gdn_prep(q, k, v, gb, h, inverse):
    nb = q.shape[0]
    beta = _select_lane(gb, h)
    gc = _select_lane(gb, HEADS_A + h)
    ri = lax.broadcasted_iota(jnp.int32, (nb, CHUNK, CHUNK), 1)
    ci = lax.broadcasted_iota(jnp.int32, (nb, CHUNK, CHUNK), 2)
    lower, strict, eye = ri >= ci, ri > ci, ri == ci
    gcol = jnp.broadcast_to(gc, (nb, CHUNK, CHUNK))
    grow = jnp.swapaxes(gcol, 1, 2)
    decay = jnp.where(lower, jnp.exp(jnp.where(lower, gcol - grow, 0.0)), 0.0)
    kb = k * beta
    kbf = k.astype(BF16)
    inv = inverse(jnp.where(strict, _bmm(kb.astype(BF16), kbf, B_NT) * decay, 0.0))
    eg = jnp.exp(gc)
    sol = _bmm_f32(inv, jnp.concatenate([v * beta, kb * eg], axis=-1))
    aqk = _bmm(q.astype(BF16), kbf, B_NT) * decay
    g_last = gc[:, CHUNK - 1:CHUNK, :]
    gl = jnp.broadcast_to(jnp.exp(g_last), (nb, 1, 128))
    return (sol[..., :DK], sol[..., DK:], q * eg, k * jnp.exp(g_last - gc), aqk, gl), inv


def gdn_prep_fwd(qkv, gb, name):
    T = qkv.shape[1]
    tp = _tile(T, PREP_T)
    nb = tp // CHUNK

    def body(q_ref, k_ref, v_ref, gb_ref, u_ref, w_ref, qd_ref, kd_ref, a_ref, gl_ref, inv_ref):
        h = pl.program_id(1)
        shp = (nb, CHUNK, 128)
        q, k, v = (ref[0].astype(F32).reshape(shp) for ref in (q_ref, k_ref, v_ref))
        (u, w, qd, kd, aqk, gl), inv = _gdn_prep(q, k, v, gb_ref[...].reshape(shp), h, _tri_inverse)
        u_ref[0] = u.reshape(tp, 128)
        w_ref[0] = w.reshape(tp, 128).astype(BF16)
        qd_ref[0] = qd.reshape(tp, 128).astype(BF16)
        kd_ref[0] = kd.reshape(tp, 128).astype(BF16)
        a_ref[0] = aqk.reshape(tp, CHUNK).astype(BF16)
        gl_ref[0] = gl.reshape(nb, 1, 128)
        inv_ref[0] = inv.reshape(tp, CHUNK)

    def head(off):
        return pl.BlockSpec((1, tp, 128), lambda n, h: (h + off, n, 0))

    matmul_only = _sds((HEADS_A, T, 128), BF16)
    narrow = pl.BlockSpec((1, tp, CHUNK), lambda n, h: (h, n, 0))
    return _pc(body, name=name,
               out_shape=(_sds((HEADS_A, T, 128), F32), matmul_only, matmul_only, matmul_only, _sds((HEADS_A, T, CHUNK), BF16),
                          _sds((HEADS_A, T // CHUNK, 1, 128), F32), _sds((HEADS_A, T, CHUNK), F32)),
               grid=(T // tp, HEADS_A),
               in_specs=[head(0), head(HEADS_A), head(2 * HEADS_A), pl.BlockSpec((tp, 128), lambda n, h: (n, 0))],
               out_specs=(head(0), head(0), head(0), head(0), narrow,
                          pl.BlockSpec((1, nb, 1, 128), lambda n, h: (h, n, 0, 0)), narrow),
               sem=("parallel", "parallel"))(qkv, qkv, qkv, gb)


def gdn_prep_bwd(qkv, gb, inv, du, dw, dqd, dkd, da, dgl, name):
    T = qkv.shape[1]
    tp = _tile(T, PREP_T)
    nb = tp // CHUNK

    def body(q_ref, k_ref, v_ref, gb_ref, inv_ref, du_ref, dw_ref, dqd_ref, dkd_ref, da_ref, dgl_ref, dqkv_ref, dgb_ref):
        h = pl.program_id(1)
        shp = (nb, CHUNK, 128)
        stored = _stored_inverse(inv_ref[0].reshape(nb, CHUNK, CHUNK))
        q, k, v = (ref[0].astype(F32).reshape(shp) for ref in (q_ref, k_ref, v_ref))
        _, vjp = jax.vjp(lambda q, k, v, gb: _gdn_prep(q, k, v, gb, h, stored)[0], q, k, v, gb_ref[...].reshape(shp))
        dq, dk, dv, dgb = vjp((du_ref[0].reshape(shp), dw_ref[0].reshape(shp), dqd_ref[0].reshape(shp),
                               dkd_ref[0].reshape(shp), da_ref[0].reshape(nb, CHUNK, CHUNK), dgl_ref[0].reshape(nb, 1, 128)))
        dqkv_ref[h] = dq.reshape(tp, 128).astype(BF16)
        dqkv_ref[HEADS_A + h] = dk.reshape(tp, 128).astype(BF16)
        dqkv_ref[2 * HEADS_A + h] = dv.reshape(tp, 128).astype(BF16)

        @pl.when(h == 0)
        def _():
            dgb_ref[...] = jnp.zeros_like(dgb_ref)
        dgb_ref[...] += dgb.reshape(tp, 128)

    def head(off):
        return pl.BlockSpec((1, tp, 128), lambda n, h: (h + off, n, 0))

    narrow = pl.BlockSpec((1, tp, CHUNK), lambda n, h: (h, n, 0))
    return _pc(body, name=name, out_shape=(_sds((N_QKV_BLK, T, 128), BF16), _sds((T, 128), F32)),
               grid=(T // tp, HEADS_A),
               in_specs=[head(0), head(HEADS_A), head(2 * HEADS_A), pl.BlockSpec((tp, 128), lambda n, h: (n, 0)), narrow,
                         head(0), head(0), head(0), head(0), narrow,
                         pl.BlockSpec((1, nb, 1, 128), lambda n, h: (h, n, 0, 0))],
               out_specs=(pl.BlockSpec((N_QKV_BLK, tp, 128), lambda n, h: (0, n, 0)),
                          pl.BlockSpec((tp, 128), lambda n, h: (n, 0))),
               sem=("parallel", "arbitrary"))(qkv, qkv, qkv, gb, inv, du, dw, dqd, dkd, da, dgl)


def gdn_scan_fwd(u, w, qd, kd, aqk, gl, name):
    T = u.shape[1]
    n_chunks = T // CHUNK

    def body(u_ref, w_ref, qd_ref, kd_ref, a_ref, gl_ref, o_ref, sin_ref, state):
        @pl.when(pl.program_id(0) == 0)
        def _():
            state[...] = jnp.zeros_like(state)
        s = state[...]
        sb = s.astype(BF16)
        sin_ref[0] = sb
        both = _bmm(jnp.concatenate([w_ref[...], qd_ref[...]], axis=1).astype(BF16), sb, B_NN)
        vn = (u_ref[...] - both[:, :CHUNK]).astype(BF16)
        o_ref[...] = both[:, CHUNK:] + _bmm(a_ref[...].astype(BF16), vn, B_NN)
        state[...] = s * gl_ref[:, 0] + _bmm(kd_ref[...].astype(BF16), vn, B_TN)

    blk = pl.BlockSpec((HEADS_A, CHUNK, 128), lambda n: (0, n, 0))
    return _pc(body, name=name,
               out_shape=(_sds((HEADS_A, T, 128), F32), _sds((n_chunks, HEADS_A, DK, 128), BF16)), grid=(n_chunks,),
               in_specs=[blk, blk, blk, blk, pl.BlockSpec((HEADS_A, CHUNK, CHUNK), lambda n: (0, n, 0)),
                         pl.BlockSpec((HEADS_A, 1, 1, 128), lambda n: (0, n, 0, 0))],
               out_specs=(blk, pl.BlockSpec((1, HEADS_A, DK, 128), lambda n: (n, 0, 0, 0))),
               scratch=[pltpu.VMEM((HEADS_A, DK, 128), F32)], sem=("arbitrary",))(u, w, qd, kd, aqk, gl)


def gdn_scan_bwd(u, w, qd, kd, aqk, gl, sin, do, name):
    T = u.shape[1]
    n_chunks = T // CHUNK

    def body(u_ref, w_ref, qd_ref, kd_ref, a_ref, gl_ref, sin_ref, do_ref,
             du_ref, dw_ref, dqd_ref, dkd_ref, da_ref, dgl_ref, dstate):
        @pl.when(pl.program_id(0) == 0)
        def _():
            dstate[...] = jnp.zeros_like(dstate)
        lane0 = lax.broadcasted_iota(jnp.int32, (HEADS_A, 1, 128), 2) == 0
        sb = sin_ref[0]
        s = sb.astype(F32)
        wb, qdb, kdb = w_ref[...].astype(BF16), qd_ref[...].astype(BF16), kd_ref[...].astype(BF16)
        ab, dob = a_ref[...].astype(BF16), do_ref[...].astype(BF16)
        vn = (u_ref[...] - _bmm(wb, sb, B_NN)).astype(BF16)
        ds_out = dstate[...]
        dsb = ds_out.astype(BF16)
        dqd_ref[...] = _bmm(dob, sb, B_NT)
        da_ref[...] = _bmm(dob, vn, B_NT)
        dv = _bmm(ab, dob, B_TN) + _bmm(kdb, dsb, B_NN)
        dkd_ref[...] = _bmm(vn, dsb, B_NT)
        dgl = jnp.sum(jnp.sum(ds_out * s, axis=2, keepdims=True), axis=1, keepdims=True)
        dgl_ref[:, 0] = jnp.where(lane0, dgl, 0.0)
        du_ref[...] = dv
        dvb = dv.astype(BF16)
        dw_ref[...] = -_bmm(dvb, sb, B_NT)
        dstate[...] = ds_out * gl_ref[:, 0] + _bmm(qdb, dob, B_TN) - _bmm(wb, dvb, B_TN)

    last = n_chunks - 1
    blk = pl.BlockSpec((HEADS_A, CHUNK, 128), lambda n: (0, last - n, 0))
    ablk = pl.BlockSpec((HEADS_A, CHUNK, CHUNK), lambda n: (0, last - n, 0))
    glblk = pl.BlockSpec((HEADS_A, 1, 1, 128), lambda n: (0, last - n, 0, 0))
    per_head = _sds((HEADS_A, T, 128), F32)
    return _pc(body, name=name,
               out_shape=(per_head, per_head, per_head, per_head, _sds((HEADS_A, T, CHUNK), F32),
                          _sds((HEADS_A, n_chunks, 1, 128), F32)), grid=(n_chunks,),
               in_specs=[blk, blk, blk, blk, ablk, glblk,
                         pl.BlockSpec((1, HEADS_A, DK, 128), lambda n: (last - n, 0, 0, 0)), blk],
               out_specs=(blk, blk, blk, blk, ablk, glblk),
               scratch=[pltpu.VMEM((HEADS_A, DK, 128), F32)], sem=("arbitrary",))(u, w, qd, kd, aqk, gl, sin, do)


def gdn_outnorm_fwd(o, proj, wn, name):
    T = o.shape[1]
    tm = _tile(T, 512)

    def body(o_ref, z_ref, wn_ref, y_ref):
        for h in range(HEADS_A):
            z = z_ref[:, 128 * h:128 * (h + 1)].astype(F32)
            y_ref[:, 128 * h:128 * (h + 1)] = (_rms_fwd(o_ref[h], wn_ref[...]) * (z * _sigmoid(z))).astype(BF16)

    return _pc(body, name=name, out_shape=_sds((T, D), BF16), grid=(T // tm,),
               in_specs=[pl.BlockSpec((HEADS_A, tm, 128), lambda i: (0, i, 0)),
                         pl.BlockSpec((tm, D), lambda i: (i, Z_BLK0 * 128 // D)), pl.BlockSpec((1, 128), lambda i: (0, 0))],
               out_specs=pl.BlockSpec((tm, D), lambda i: (i, 0)), sem=("parallel",))(o, proj, wn)


def gdn_outnorm_bwd(o, proj, wn, dy, name):
    T = o.shape[1]
    tm = _tile(T, 512)

    def body(o_ref, z_ref, wn_ref, dy_ref, do_ref, dz_ref, dwn_ref):
        wn = wn_ref[...]
        acc = jnp.zeros((1, 128), F32)
        for h in range(HEADS_A):
            cols = slice(128 * h, 128 * (h + 1))
            z, dyh, ov = z_ref[:, cols].astype(F32), dy_ref[:, cols], o_ref[h]
            sg = _sigmoid(z)
            do, dwn = _rms_bwd(ov, wn, dyh * (z * sg))
            do_ref[h] = do
            acc = acc + dwn
            dz_ref[:, cols] = (dyh * _rms_fwd(ov, wn) * (sg * (1.0 + z * (1.0 - sg)))).astype(BF16)

        @pl.when(pl.program_id(0) == 0)
        def _():
            dwn_ref[...] = jnp.zeros_like(dwn_ref)
        dwn_ref[...] += acc

    row = pl.BlockSpec((tm, D), lambda i: (i, 0))
    vec = pl.BlockSpec((1, 128), lambda i: (0, 0))
    hblk = pl.BlockSpec((HEADS_A, tm, 128), lambda i: (0, i, 0))
    return _pc(body, name=name, out_shape=(_sds((HEADS_A, T, 128), F32), _sds((T, D), BF16), _sds((1, 128), F32)),
               grid=(T // tm,),
               in_specs=[hblk, pl.BlockSpec((tm, D), lambda i: (i, Z_BLK0 * 128 // D)), vec, row],
               out_specs=(hblk, row, vec), sem=("arbitrary",))(o, proj, wn, dy)


def gdn_forward(x, nw, w_in, wconv, al, dtb, wn, w_out, tag, deps=()):
    h = rmsnorm_bf16(x, nw, f"{tag}_norm", deps)
    proj = mm_nn(h, w_in, f"{tag}_proj", out_dtype=BF16, cols=(0, MAIN_COLS))
    ba = mm_nn(h, w_in, f"{tag}_proj_ba", cols=(MAIN_COLS, A_COLS))
    c, qkv = gdn_conv_fwd(proj, wconv, f"{tag}_conv")
    gb = gdn_gate_fwd(ba, al, dtb, f"{tag}_gate")
    u, w, qd, kd, aqk, gl, inv = gdn_prep_fwd(qkv, gb, f"{tag}_prep")
    o, sin = gdn_scan_fwd(u, w, qd, kd, aqk, gl, f"{tag}_scan")
    on = gdn_outnorm_fwd(o, proj, wn, f"{tag}_outnorm")
    y = mm_nn(on, w_out, f"{tag}_out", residual=x)
    return y, (x, h, proj, ba, c, qkv, gb, inv, (u, w, qd, kd, aqk, gl), sin, o, on)


def gdn_backward(dout, saved, nw, w_in, wconv, al, dtb, wn, w_out, tag):
    x, h, proj, ba, c, qkv, gb, inv, prep, sin, o, on = saved
    d_on = mm_nt(dout, w_out, f"{tag}_out_bwd")
    dw_out = mm_tn(on, dout, f"{tag}_out_wgrad")
    do, dz, dwn = gdn_outnorm_bwd(o, proj, wn, d_on, f"{tag}_outnorm_bwd")
    du, dw, dqd, dkd, da, dgl = gdn_scan_bwd(*prep, sin, do, f"{tag}_scan_bwd")
    dqkv, dgb = gdn_prep_bwd(qkv, gb, inv, du, dw, dqd, dkd, da, dgl, f"{tag}_prep_bwd")
    dba, dal, ddt = gdn_gate_bwd(ba, al, dtb, dgb, f"{tag}_gate_bwd")
    dpre, dwconv = gdn_conv_bwd(dqkv, c, proj, wconv, f"{tag}_conv_bwd")
    dproj = jnp.concatenate([dpre, dz, dba], axis=1)
    dw_in = mm_tn(h, dproj, f"{tag}_proj_wgrad")
    dh = mm_nt(dproj, w_in, f"{tag}_proj_bwd")
    dx, dnw = rmsnorm_bwd_add(x, nw, dh, dout, f"{tag}_norm_bwd")
    return dx, dnw, dw_in, dwconv, dal, ddt, dwn, dw_out


N_KV, GROUP = 4, 4
KV_COLS = 2 * N_KV * B_HD
B_COLS = D + KV_COLS


@jax.custom_vjp
def _swap_lane_halves(x):
    return pltpu.roll(x, 64, 1)


_swap_lane_halves.defvjp(lambda x: (pltpu.roll(x, 64, 1), None), lambda _, g: (pltpu.roll(g, 64, 1),))


def _swa_block(q, kp, kc, vp, vc, sk, first):
    rows = GROUP * B_BLK
    qi = lax.broadcasted_iota(jnp.int32, (N_KV, rows, B_BLK), 1) % B_BLK
    kj = lax.broadcasted_iota(jnp.int32, (N_KV, rows, B_BLK), 2)
    from_cur = kj <= qi

    def batch(parts):
        return jnp.concatenate([part[None] for part in parts], axis=0)

    def per_kv(cur, prev):
        return batch([jnp.concatenate([cur[:, j * B_HD:(j + 1) * B_HD], prev[:, j * B_HD:(j + 1) * B_HD]], axis=0)
                      for j in range(N_KV)]).astype(BF16)

    qs = batch([jnp.concatenate([q[:, hq * B_HD:(hq + 1) * B_HD] for hq in range(GROUP * j, GROUP * (j + 1))], axis=0)
                for j in range(N_KV)]).astype(BF16)
    sink = batch([jnp.concatenate([jnp.broadcast_to(sk[:, hq:hq + 1], (B_BLK, 1))
                                   for hq in range(GROUP * j, GROUP * (j + 1))], axis=0) for j in range(N_KV)])
    both = _bmm(qs, per_kv(kc, kp), B_NT)
    s = jnp.where(from_cur, both[..., :B_BLK], jnp.where(first, -1e30, both[..., B_BLK:])) * (B_HD ** -0.5)
    m = lax.stop_gradient(jnp.maximum(jnp.max(s, axis=-1, keepdims=True), sink))
    e = jnp.exp(s - m)
    p = e * (1.0 / (jnp.sum(e, axis=-1, keepdims=True) + jnp.exp(sink - m)))
    p_both = jnp.concatenate([jnp.where(from_cur, p, 0.0), jnp.where(from_cur, 0.0, p)], axis=-1).astype(BF16)
    o = _bmm(p_both, per_kv(vc, vp), B_NN)
    return jnp.concatenate([o[j, g * B_BLK:(g + 1) * B_BLK] for j in range(N_KV) for g in range(GROUP)], axis=1)


def swa_core_fwd(proj, sk, name):
    T = proj.shape[0]
    half = N_KV * B_HD

    def body(q_ref, kvc_ref, kvp_ref, sk_ref, o_ref):
        kvc, kvp = kvc_ref[...], kvp_ref[...]
        o_ref[...] = _swa_block(q_ref[...], kvp[:, :half], kvc[:, :half], kvp[:, half:], kvc[:, half:], sk_ref[...],
                                pl.program_id(0) == 0).astype(BF16)

    return _pc(body, name=name, out_shape=_sds((T, D), BF16), grid=(T // B_BLK,),
               in_specs=[pl.BlockSpec((B_BLK, D), lambda n: (n, 0)),
                         pl.BlockSpec((B_BLK, KV_COLS), lambda n: (n, D // KV_COLS)),
                         pl.BlockSpec((B_BLK, KV_COLS), lambda n: (jnp.maximum(n - 1, 0), D // KV_COLS)),
                         pl.BlockSpec((1, 128), lambda n: (0, 0))],
               out_specs=pl.BlockSpec((B_BLK, D), lambda n: (n, 0)), sem=("parallel",))(proj, proj, proj, sk)


def swa_core_bwd(proj, sk, do, name):
    T = proj.shape[0]
    last = T // B_BLK - 1
    half = N_KV * B_HD

    def body(q_ref, kvc_ref, kvp_ref, sk_ref, do_ref, dproj_ref, dbias_ref, dsk_ref, carry):
        step = pl.program_id(0)
        first = step == last

        @pl.when(step == 0)
        def _():
            carry[...] = jnp.zeros_like(carry)
            dbias_ref[...] = jnp.zeros_like(dbias_ref)
            dsk_ref[...] = jnp.zeros_like(dsk_ref)
        kvc, kvp = kvc_ref[...], kvp_ref[...]
        _, vjp = jax.vjp(functools.partial(_swa_block, first=first), q_ref[...], kvp[:, :half], kvc[:, :half],
                         kvp[:, half:], kvc[:, half:], sk_ref[...])
        dq, dkp, dkc, dvp, dvc, dsk = vjp(do_ref[...])
        dkv = jnp.concatenate([dkc, dvc], axis=1) + carry[...]
        carry[...] = jnp.concatenate([dkp, dvp], axis=1)
        row = jnp.concatenate([dq, dkv], axis=1)
        dproj_ref[...] = row.astype(BF16)
        dbias_ref[...] += jnp.sum(row, axis=0, keepdims=True)
        dsk_ref[...] += dsk

    return _pc(body, name=name, out_shape=(_sds((T, B_COLS), BF16), _sds((1, B_COLS), F32), _sds((1, 128), F32)),
               grid=(T // B_BLK,),
               in_specs=[pl.BlockSpec((B_BLK, D), lambda n: (last - n, 0)),
                         pl.BlockSpec((B_BLK, KV_COLS), lambda n: (last - n, D // KV_COLS)),
                         pl.BlockSpec((B_BLK, KV_COLS), lambda n: (jnp.maximum(last - n - 1, 0), D // KV_COLS)),
                         pl.BlockSpec((1, 128), lambda n: (0, 0)), pl.BlockSpec((B_BLK, D), lambda n: (last - n, 0))],
               out_specs=(pl.BlockSpec((B_BLK, B_COLS), lambda n: (last - n, 0)),
                          pl.BlockSpec((1, B_COLS), lambda n: (0, 0)), pl.BlockSpec((1, 128), lambda n: (0, 0))),
               scratch=[pltpu.VMEM((B_BLK, KV_COLS), F32)], sem=("arbitrary",))(proj, proj, proj, sk, do)


def col_sum(a, name):
    T, N = a.shape
    tm = _tile(T, 1024)

    def body(a_ref, o_ref):
        @pl.when(pl.program_id(0) == 0)
        def _():
            o_ref[...] = jnp.zeros_like(o_ref)
        o_ref[...] += jnp.sum(a_ref[...].astype(F32), axis=0, keepdims=True)

    return _pc(body, name=name, out_shape=_sds((1, N), F32), grid=(T // tm,),
               in_specs=[pl.BlockSpec((tm, N), lambda i: (i, 0))], out_specs=pl.BlockSpec((1, N), lambda i: (0, 0)),
               sem=("arbitrary",))(a)


def swa_forward(x, nw, w_in, b_in, sk, w_out, b_out, tag):
    h = rmsnorm_bf16(x, nw, f"{tag}_norm")
    proj = mm_nn(h, w_in, f"{tag}_proj", bias=b_in)
    o = swa_core_fwd(proj, sk, f"{tag}_core")
    y = mm_nn(o, w_out, f"{tag}_out", bias=b_out, residual=x)
    return y, (x, h, proj, o)


def swa_backward(dout, saved, nw, w_in, b_in, sk, w_out, b_out, tag):
    x, h, proj, o = saved
    do = mm_nt(dout, w_out, f"{tag}_out_bwd")
    dw_out = mm_tn(o, dout, f"{tag}_out_wgrad")
    db_out = col_sum(dout, f"{tag}_out_bias_grad")
    dproj, db_in, dsk = swa_core_bwd(proj, sk, do, f"{tag}_core_bwd")
    dw_in = mm_tn(h, dproj, f"{tag}_proj_wgrad")
    dh = mm_nt(dproj, w_in, f"{tag}_proj_bwd")
    dx, dnw = rmsnorm_bwd_add(x, nw, dh, dout, f"{tag}_norm_bwd")
    return dx, dnw, dw_in, db_in, dsk, dw_out, db_out


MESH = pl.DeviceIdType.MESH


def _position():
    return lax.axis_index("x"), lax.axis_index("y"), lax.axis_index("c")


def _slot(x, y, c):
    return 4 * x + 2 * y + c


def _peer(x, y, c, k):
    return (1 - x if k & 4 else x, 1 - y if k & 2 else y, 1 - c if k & 1 else c)


HBM_SPEC = pl.BlockSpec(memory_space=pltpu.HBM)
SEM_SPEC = pl.BlockSpec(memory_space=pltpu.SEMAPHORE)
DEP_SPEC = pl.BlockSpec(memory_space=pl.ANY)
SIDE_EFFECT = pltpu.SideEffectType.DATAFLOW_SIDE_EFFECTING
N_PEERS = N_DEV - 1


def _push_copies(srcs, lands, send_sems, recv_sems, scatter):
    x, y, c = _position()
    me = _slot(x, y, c)
    copies = []
    for k in (1, 2, 4, 3, 5, 6, 7):
        peer = _peer(x, y, c, k)
        for a in range(len(srcs)):
            copies.append(pltpu.make_async_remote_copy(
                src_ref=srcs[a].at[_slot(*peer)] if scatter else srcs[a], dst_ref=lands[a].at[me],
                send_sem=send_sems.at[N_PEERS * a + k - 1], recv_sem=recv_sems.at[N_PEERS * a + k - 1],
                device_id=peer, device_id_type=MESH))
    return copies


def push_start(srcs, lands, name, scatter, deps=()):
    n = len(srcs)
    first_out = 2 * n + len(deps)

    def body(*refs):
        for cp in _push_copies(refs[:n], refs[n:2 * n], refs[first_out], refs[first_out + 1], scatter):
            cp.start()
        refs[-1][...] = jnp.zeros_like(refs[-1])

    passed = [pltpu.HBM(t.shape, t.dtype) for t in list(srcs) + list(lands)]
    res = pl.pallas_call(
        body, name=name,
        out_shape=(pltpu.SemaphoreType.DMA((N_PEERS * n,)), pltpu.SemaphoreType.DMA((N_PEERS * n,)), *passed, _sds((8, 128), F32)),
        in_specs=[HBM_SPEC] * (2 * n) + [DEP_SPEC] * len(deps),
        out_specs=(SEM_SPEC, SEM_SPEC, *([HBM_SPEC] * (2 * n)), pl.BlockSpec(memory_space=pltpu.VMEM)),
        input_output_aliases={i: 2 + i for i in range(2 * n)},
        compiler_params=pltpu.CompilerParams(has_side_effects=SIDE_EFFECT),
    )(*[pltpu.with_memory_space_constraint(t, pltpu.HBM) for t in list(srcs) + list(lands)], *deps)
    return (res[0], res[1], list(res[2:2 + n]), list(res[2 + n:2 + 2 * n])), res[-1]


def push_wait(handles, after, name, scatter):
    send_sems, recv_sems, srcs, lands = handles
    n = len(srcs)
    after = tuple(after) if isinstance(after, (tuple, list)) else (after,)

    def body(*refs):
        for cp in _push_copies(refs[:n], refs[n:2 * n], refs[2 * n], refs[2 * n + 1], scatter):
            cp.wait_send()
            cp.wait_recv()

    res = pl.pallas_call(
        body, name=name, out_shape=tuple(pltpu.HBM(t.shape, t.dtype) for t in srcs + lands),
        in_specs=[HBM_SPEC] * (2 * n) + [SEM_SPEC, SEM_SPEC] + [DEP_SPEC] * len(after), out_specs=tuple([HBM_SPEC] * (2 * n)),
        input_output_aliases={i: i for i in range(2 * n)},
        compiler_params=pltpu.CompilerParams(has_side_effects=SIDE_EFFECT),
    )(*srcs, *lands, send_sems, recv_sems, *after)
    return list(res[n:])


def gather_start(shards, name, deps=()):
    me = _slot(*_position())
    lands = [lax.dynamic_update_slice(lax.empty((N_DEV,) + t.shape, t.dtype), t[None], (me,) + (0,) * t.ndim) for t in shards]
    return push_start(shards, lands, name, scatter=False, deps=deps)


def exchange_start(parts, name):
    me = _slot(*_position())
    lands = [lax.dynamic_update_slice(lax.empty(t.shape, t.dtype), lax.dynamic_index_in_dim(t, me, 0, keepdims=True),
                                      (me,) + (0,) * (t.ndim - 1)) for t in parts]
    return push_start(parts, lands, name, scatter=True)


def _row_tile(rows, cols):
    best = rows
    for t in range(16, rows, 16):
        if rows % t == 0 and t * cols * 4 <= (1 << 20):
            best = t
    return best


def adam_update(parts, w, m, v, name):
    n_layers = len(parts)
    P, R, C = parts[0].shape
    tr = _row_tile(R, C)
    n_t = R // tr

    def body(*refs):
        p_refs = refs[:n_layers]
        w_ref, m_ref, v_ref, g_ref, d_ref, nm_ref, nv_ref = refs[n_layers:]
        for layer in range(n_layers):
            @pl.when(pl.program_id(0) == layer)
            def _(p_ref=p_refs[layer]):
                g = p_ref[0].astype(F32)
                for s in range(1, P):
                    g = g + p_ref[s].astype(F32)
                new_m = ADAM_B1 * m_ref[0] + (1.0 - ADAM_B1) * g
                new_v = ADAM_B2 * v_ref[0] + (1.0 - ADAM_B2) * (g * g)
                m_hat = new_m / (1.0 - ADAM_B1 ** ADAM_STEP)
                v_hat = new_v / (1.0 - ADAM_B2 ** ADAM_STEP)
                g_ref[0] = g
                d_ref[0] = -ADAM_LR * (m_hat / (jnp.sqrt(v_hat) + ADAM_EPS) + ADAM_WD * w_ref[0])
                nm_ref[0] = new_m
                nv_ref[0] = new_v

    def part_spec(layer):
        return pl.BlockSpec((P, tr, C), lambda l_, i: (0, jnp.where(l_ == layer, i, jnp.where(l_ < layer, 0, n_t - 1)), 0))

    blk = pl.BlockSpec((1, tr, C), lambda l_, i: (l_, i, 0))
    out = _sds((n_layers, R, C), F32)
    return _pc(body, name=name, out_shape=(out, out, out, out), grid=(n_layers, n_t),
               in_specs=[part_spec(layer) for layer in range(n_layers)] + [blk, blk, blk],
               out_specs=(blk, blk, blk, blk), sem=("arbitrary", "arbitrary"))(*parts, w, m, v)


WEIGHTS = ("ffn1_norm", "ffn1_w_gu", "ffn1_w_down", "mix_norm", "ffn2_norm", "ffn2_w_gu", "ffn2_w_down", "a_w_in",
           "a_w_conv", "a_A_log", "a_dt_bias", "a_out_norm", "a_w_out", "b_w_in", "b_b_in", "b_sinks", "b_w_out",
           "b_b_out", "final_norm")
SHARDED = ("ffn1_w_gu", "ffn1_w_down", "ffn2_w_gu", "ffn2_w_down", "a_w_in", "a_w_conv", "a_w_out", "b_w_in", "b_b_in",
           "b_w_out", "b_b_out")
MISC_LANES = dict(a_A_log=(0, 8), a_dt_bias=(8, 16), b_sinks=(16, 32), a_out_norm=(128, 256))
LOSS_LANE = 256


def _pack_small(t):
    misc = jnp.zeros((D,), F32)
    for key, (lo, hi) in MISC_LANES.items():
        misc = misc.at[lo:hi].set(t[key].reshape(-1))
    if "loss" in t:
        misc = misc.at[LOSS_LANE].set(t["loss"])
    return jnp.concatenate([t["ffn1_norm"], t["mix_norm"], t["ffn2_norm"], t["final_norm"].reshape(1, D), misc[None]], axis=0)


def _unpack_small(p, like):
    out = dict(ffn1_norm=p[0:2], mix_norm=p[2:4], ffn2_norm=p[4:6], final_norm=p[6])
    for key, (lo, hi) in MISC_LANES.items():
        out[key] = p[7, lo:hi].reshape(like[key].shape)
    return out


def kernel(x, ffn1_norm, ffn1_w_gu, ffn1_w_down, mix_norm, ffn2_norm, ffn2_w_gu, ffn2_w_down, a_w_in, a_w_conv, a_A_log, a_dt_bias, a_out_norm, a_w_out, b_w_in, b_b_in, b_sinks, b_w_out, b_b_out, final_norm, loss_target, m_ffn1_norm, m_ffn1_w_gu, m_ffn1_w_down, m_mix_norm, m_ffn2_norm, m_ffn2_w_gu, m_ffn2_w_down, m_a_w_in, m_a_w_conv, m_a_A_log, m_a_dt_bias, m_a_out_norm, m_a_w_out, m_b_w_in, m_b_b_in, m_b_sinks, m_b_w_out, m_b_b_out, m_final_norm, v_ffn1_norm, v_ffn1_w_gu, v_ffn1_w_down, v_mix_norm, v_ffn2_norm, v_ffn2_w_gu, v_ffn2_w_down, v_a_w_in, v_a_w_conv, v_a_A_log, v_a_dt_bias, v_a_out_norm, v_a_w_out, v_b_w_in, v_b_b_in, v_b_sinks, v_b_w_out, v_b_b_out, v_final_norm):
    w = dict(ffn1_norm=ffn1_norm, ffn1_w_gu=ffn1_w_gu, ffn1_w_down=ffn1_w_down, mix_norm=mix_norm, ffn2_norm=ffn2_norm, ffn2_w_gu=ffn2_w_gu, ffn2_w_down=ffn2_w_down, a_w_in=a_w_in, a_w_conv=a_w_conv, a_A_log=a_A_log, a_dt_bias=a_dt_bias, a_out_norm=a_out_norm, a_w_out=a_w_out, b_w_in=b_w_in, b_b_in=b_b_in, b_sinks=b_sinks, b_w_out=b_w_out, b_b_out=b_b_out, final_norm=final_norm)
    m = dict(ffn1_norm=m_ffn1_norm, ffn1_w_gu=m_ffn1_w_gu, ffn1_w_down=m_ffn1_w_down, mix_norm=m_mix_norm, ffn2_norm=m_ffn2_norm, ffn2_w_gu=m_ffn2_w_gu, ffn2_w_down=m_ffn2_w_down, a_w_in=m_a_w_in, a_w_conv=m_a_w_conv, a_A_log=m_a_A_log, a_dt_bias=m_a_dt_bias, a_out_norm=m_a_out_norm, a_w_out=m_a_w_out, b_w_in=m_b_w_in, b_b_in=m_b_b_in, b_sinks=m_b_sinks, b_w_out=m_b_w_out, b_b_out=m_b_b_out, final_norm=m_final_norm)
    v = dict(ffn1_norm=v_ffn1_norm, ffn1_w_gu=v_ffn1_w_gu, ffn1_w_down=v_ffn1_w_down, mix_norm=v_mix_norm, ffn2_norm=v_ffn2_norm, ffn2_w_gu=v_ffn2_w_gu, ffn2_w_down=v_ffn2_w_down, a_w_in=v_a_w_in, a_w_conv=v_a_w_conv, a_A_log=v_a_A_log, a_dt_bias=v_a_dt_bias, a_out_norm=v_a_out_norm, a_w_out=v_a_w_out, b_w_in=v_b_w_in, b_b_in=v_b_b_in, b_sinks=v_b_sinks, b_w_out=v_b_w_out, b_b_out=v_b_b_out, final_norm=v_final_norm)
    T = x.shape[1]
    x0, tgt = x.reshape(T, D), loss_target.reshape(T, D)

    def cast(t):
        return t.astype(BF16)

    h0, t0 = gather_start([cast(ffn1_w_gu[0])], "gather0_start")
    a_log_row = jnp.zeros((1, 128), F32).at[0, HEADS_A:2 * HEADS_A].set(a_A_log[0])
    dt_row = jnp.zeros((1, 128), F32).at[0, HEADS_A:2 * HEADS_A].set(a_dt_bias[0])
    sink_row = jnp.zeros((1, 128), F32).at[0, :b_sinks.shape[1]].set(b_sinks[0])
    a_in_cols = a_w_in.shape[-1] * N_DEV

    def down_blocks(t):
        return t.reshape(N_FB, FB, D)

    wgu, wdn, saved = {}, {}, []
    xn = rmsnorm_bf16(x0, ffn1_norm[0:1], "l0_ffn1_norm", (t0,))
    wgu["ffn1", 0] = push_wait(h0, xn, "gather0_wait", scatter=False)[0]
    h0d, t0d = gather_start([cast(ffn1_w_down[0])], "gather0d_start", deps=(wgu["ffn1", 0],))
    h1, t1 = gather_start([cast(a_w_in[0]), a_w_conv[0], cast(a_w_out[0])], "gather1_start", deps=(t0d,))
    gu = ffn_up(xn, wgu["ffn1", 0], "l0_ffn1_up", deps=(t0d, t1))
    wdn["ffn1", 0] = down_blocks(push_wait(h0d, gu, "gather0d_wait", scatter=False)[0])
    xs, s1 = ffn_down(gu, wdn["ffn1", 0], x0, "l0_ffn1_down"), (x0, xn, gu)
    got = push_wait(h1, xs, "gather1_wait", scatter=False)
    h1f, t1f = gather_start([cast(ffn2_w_gu[0]), cast(ffn2_w_down[0])], "gather1f_start", deps=(got[0],))
    g2 = [cast(ffn1_w_gu[1]), cast(ffn1_w_down[1]), cast(b_w_in[0]), b_b_in, cast(b_w_out[0]), b_b_out,
          cast(ffn2_w_gu[1]), cast(ffn2_w_down[1])]
    h2, t2 = gather_start(g2, "gather2_start", deps=(t1f,))
    a_in_full = jnp.pad(got[0].transpose(1, 0, 2).reshape(D, a_in_cols), ((0, 0), (0, A_COLS - a_in_cols)))
    gdn_args = (mix_norm[0:1], a_in_full, got[1].transpose(1, 0, 2).reshape(4, 3 * D), a_log_row, dt_row, a_out_norm,
                got[2].reshape(D, D))
    xs, sm = gdn_forward(xs, *gdn_args, "gdn", deps=(t1f, t2))
    got = push_wait(h1f, xs, "gather1f_wait", scatter=False)
    wgu["ffn2", 0], wdn["ffn2", 0] = got[0], down_blocks(got[1])
    xs, s2 = ffn_forward(xs, ffn2_norm[0:1], wgu["ffn2", 0], wdn["ffn2", 0], "l0_ffn2")
    saved.append((s1, sm, s2))
    got = push_wait(h2, xs, "gather2_wait", scatter=False)
    wgu["ffn1", 1], wdn["ffn1", 1] = got[0], down_blocks(got[1])
    swa_args = (mix_norm[1:2], got[2].transpose(1, 0, 2).reshape(D, B_COLS), got[3].reshape(1, B_COLS), sink_row,
                got[4].reshape(D, D), got[5].reshape(1, D))
    wgu["ffn2", 1], wdn["ffn2", 1] = got[6], down_blocks(got[7])
    xs, s1 = ffn_forward(xs, ffn1_norm[1:2], wgu["ffn1", 1], wdn["ffn1", 1], "l1_ffn1")
    xs, sm = swa_forward(xs, *swa_args, "swa")
    xs, s2 = ffn_forward(xs, ffn2_norm[1:2], wgu["ffn2", 1], wdn["ffn2", 1], "l1_ffn2")
    saved.append((s1, sm, s2))
    loss_row, dx, d_final_norm = final_loss(xs, final_norm.reshape(1, D), tgt, "final_loss")

    def down_slots(t):
        return cast(t.reshape(N_DEV, FB // 2, D))

    def col_slots(t, dtype=BF16):
        return t.reshape(t.shape[0], N_DEV, -1).transpose(1, 0, 2).astype(dtype)

    d_norm = {"ffn1_norm": [None, None], "mix_norm": [None, None], "ffn2_norm": [None, None]}
    s1, sm, s2 = saved[1]
    dx, d_norm["ffn2_norm"][1], d_gu, d_dn = ffn_backward(dx, s2, ffn2_norm[1:2], wgu["ffn2", 1], wdn["ffn2", 1], "l1_ffn2")
    sent1 = [cast(d_gu), down_slots(d_dn)]
    dx, d_norm["mix_norm"][1], d_b_in, d_b_bias_in, d_sinks, d_b_out, d_b_bias_out = swa_backward(dx, sm, *swa_args, "swa")
    sent1 += [col_slots(d_b_in), d_b_bias_in.reshape(N_DEV, 1, -1), cast(d_b_out.reshape(N_DEV, D // N_DEV, D)),
              d_b_bias_out.reshape(N_DEV, 1, -1)]
    dx, d_norm["ffn1_norm"][1], d_gu, d_dn = ffn_backward(dx, s1, ffn1_norm[1:2], wgu["ffn1", 1], wdn["ffn1", 1], "l1_ffn1")
    sent1 += [cast(d_gu), down_slots(d_dn)]
    x1, tx1 = exchange_start(sent1, "exchange1_start")

    s1, sm, s2 = saved[0]
    dx, d_norm["ffn2_norm"][0], d_gu, d_dn = ffn_backward(dx, s2, ffn2_norm[0:1], wgu["ffn2", 0], wdn["ffn2", 0], "l0_ffn2",
                                                           deps=(tx1,))
    sent2 = [cast(d_gu), down_slots(d_dn)]
    dx, d_norm["mix_norm"][0], d_a_in, d_a_conv, d_alog, d_dt, d_onorm, d_a_out = gdn_backward(dx, sm, *gdn_args, "gdn")
    sent2 += [col_slots(d_a_in[:, :a_in_cols]), col_slots(d_a_conv, F32), cast(d_a_out.reshape(N_DEV, D // N_DEV, D))]
    x2, tx2 = exchange_start(sent2, "exchange2_start")
    last = {}

    def send_last(d_gu, d_dn):
        last["handles"], token = exchange_start([cast(d_gu), down_slots(d_dn)], "exchange3_start")
        return (token,)

    dx, d_norm["ffn1_norm"][0], _, _ = ffn_backward(dx, s1, ffn1_norm[0:1], wgu["ffn1", 0], wdn["ffn1", 0], "l0_ffn1",
                                                    deps=(tx2,), on_grads=send_last)
    grad_x = dx.reshape(x.shape)
    r1 = push_wait(x1, dx, "exchange1_wait", scatter=True)
    r2 = push_wait(x2, dx, "exchange2_wait", scatter=True)
    received = dict(ffn2_w_gu=[r2[0], r1[0]], ffn2_w_down=[r2[1], r1[1]],
                    b_w_in=[r1[2]], b_b_in=[r1[3]], b_w_out=[r1[4]], b_b_out=[r1[5]],
                    a_w_in=[r2[2]], a_w_conv=[r2[3]], a_w_out=[r2[4]])

    grads, deltas, new_m, new_v = {}, {}, {}, {}

    def update(key):
        shape = w[key].shape
        cols = shape[-1]
        layers = lambda t: t.reshape(shape[0], -1, cols)
        out = adam_update([r.reshape(N_DEV, -1, cols) for r in received[key]], layers(w[key]), layers(m[key]), layers(v[key]),
                          f"adam_{key}")
        grads[key], deltas[key], new_m[key], new_v[key] = (t.reshape(shape) for t in out)

    for key in SHARDED:
        if key in received:
            update(key)
    done_first = [deltas[key] for key in received]

    small = dict(ffn1_norm=jnp.concatenate(d_norm["ffn1_norm"], axis=0), mix_norm=jnp.concatenate(d_norm["mix_norm"], axis=0),
                 ffn2_norm=jnp.concatenate(d_norm["ffn2_norm"], axis=0), final_norm=d_final_norm,
                 a_A_log=d_alog[0, HEADS_A:2 * HEADS_A], a_dt_bias=d_dt[0, HEADS_A:2 * HEADS_A],
                 b_sinks=d_sinks[0, :b_sinks.shape[1]], a_out_norm=d_onorm, loss=loss_row[0, 0])
    hs, ts = gather_start([_pack_small(small)], "gather_small_start")
    r3 = push_wait(last["handles"], done_first + [ts], "exchange3_wait", scatter=True)
    received.update(ffn1_w_gu=[r3[0], r1[6]], ffn1_w_down=[r3[1], r1[7]])
    update("ffn1_w_gu")
    update("ffn1_w_down")
    every = push_wait(hs, deltas["ffn1_w_down"], "gather_small_wait", scatter=False)[0]
    out = adam_update([every], _pack_small(w)[None], _pack_small(m)[None], _pack_small(v)[None], "adam_small")
    for dst, packed in zip((grads, deltas, new_m, new_v), out):
        dst.update(_unpack_small(packed[0], w))
    loss = out[0][0, 7, LOSS_LANE]

    return (loss, grad_x, *[grads[k_] for k_ in WEIGHTS], *[deltas[k_] for k_ in WEIGHTS],
            *[new_m[k_] for k_ in WEIGHTS], *[new_v[k_] for k_ in WEIGHTS])
```

```python
import functools

import jax
import jax.numpy as jnp
from jax import lax
from jax.experimental import pallas as pl
from jax.experimental.pallas import tpu as pltpu

F32, BF16 = jnp.float32, jnp.bfloat16
HI = lax.Precision.HIGHEST
EPS = 1e-6

N_DEV = 8
D = 1024
FB = 704
N_FB = 4
HEADS_A, DK = 8, 128
CHUNK = 64
PREP_T = 512
A_COLS = 4224
B_HD, B_BLK = 64, 128
VMEM_LIMIT_V7X = 60 * 1024 * 1024

ADAM_LR, ADAM_B1, ADAM_B2, ADAM_EPS, ADAM_WD, ADAM_STEP = 0.001, 0.9, 0.999, 1e-08, 0.01, 10

NT = (((1,), (1,)), ((), ()))
TN = (((0,), (0,)), ((), ()))


def _pc(body, *, name, out_shape, grid=(), in_specs=None, out_specs=None, scratch=(), sem=None, **kw):
    params = pltpu.CompilerParams(dimension_semantics=sem, vmem_limit_bytes=VMEM_LIMIT_V7X)
    return pl.pallas_call(body, name=name, out_shape=out_shape, grid=grid, in_specs=in_specs, out_specs=out_specs,
                          scratch_shapes=list(scratch), compiler_params=params, **kw)


def _sds(shape, dtype):
    return jax.ShapeDtypeStruct(tuple(shape), dtype)


def _dot(a, b, dims=None, precision=None):
    if dims is None:
        return jnp.dot(a, b, preferred_element_type=F32, precision=precision)
    return lax.dot_general(a, b, dims, preferred_element_type=F32, precision=precision)


def _sigmoid(x):
    return 1.0 / (1.0 + jnp.exp(-x))


def _softplus(x):
    return jnp.maximum(x, 0.0) + jnp.log(1.0 + jnp.exp(-jnp.abs(x)))


def _rms_fwd(x, w):
    r = lax.rsqrt(jnp.mean(x * x, axis=-1, keepdims=True) + EPS)
    return x * r * w


def _rms_bwd(x, w, dy):
    r = lax.rsqrt(jnp.mean(x * x, axis=-1, keepdims=True) + EPS)
    xh = x * r
    dxh = dy * w
    dx = r * (dxh - xh * jnp.mean(dxh * xh, axis=-1, keepdims=True))
    return dx, jnp.sum(dy * xh, axis=0, keepdims=True)


def _tile(n, want):
    t = min(n, want)
    assert n % t == 0, (n, want)
    return t


def rmsnorm_bf16(x, w, name, deps=()):
    T = x.shape[0]
    tm = _tile(T, 1024)

    def body(x_ref, w_ref, *rest):
        rest[-1][...] = _rms_fwd(x_ref[...], w_ref[...]).astype(BF16)

    return _pc(body, name=name, out_shape=_sds((T, D), BF16), grid=(T // tm,),
               in_specs=[pl.BlockSpec((tm, D), lambda i: (i, 0)), pl.BlockSpec((1, D), lambda i: (0, 0))] + [DEP_SPEC] * len(deps),
               out_specs=pl.BlockSpec((tm, D), lambda i: (i, 0)), sem=("parallel",))(x, w, *deps)


def rmsnorm_bwd_add(x, w, dxn, dres, name):
    T = x.shape[0]
    tm = _tile(T, 512)

    def body(x_ref, w_ref, dxn_ref, dres_ref, dx_ref, dw_ref):
        dx, dw = _rms_bwd(x_ref[...], w_ref[...], dxn_ref[...])
        dx_ref[...] = dres_ref[...] + dx

        @pl.when(pl.program_id(0) == 0)
        def _():
            dw_ref[...] = jnp.zeros_like(dw_ref)
        dw_ref[...] += dw

    row = pl.BlockSpec((tm, D), lambda i: (i, 0))
    vec = pl.BlockSpec((1, D), lambda i: (0, 0))
    return _pc(body, name=name, out_shape=(_sds((T, D), F32), _sds((1, D), F32)), grid=(T // tm,),
               in_specs=[row, vec, row, row], out_specs=(row, vec), sem=("arbitrary",))(x, w, dxn, dres)


def final_loss(x, w, tgt, name):
    T = x.shape[0]
    tm = _tile(T, 512)

    def body(x_ref, w_ref, t_ref, loss_ref, dx_ref, dw_ref):
        xv, wv = x_ref[...], w_ref[...]
        err = _rms_fwd(xv, wv) - t_ref[...]
        dx, dw = _rms_bwd(xv, wv, err * (1.0 / D))
        dx_ref[...] = dx

        @pl.when(pl.program_id(0) == 0)
        def _():
            dw_ref[...] = jnp.zeros_like(dw_ref)
            loss_ref[...] = jnp.zeros_like(loss_ref)
        dw_ref[...] += dw
        loss_ref[...] += jnp.full((1, 128), 0.5 / D, F32) * jnp.sum(err * err)

    row = pl.BlockSpec((tm, D), lambda i: (i, 0))
    vec = pl.BlockSpec((1, D), lambda i: (0, 0))
    return _pc(body, name=name, out_shape=(_sds((1, 128), F32), _sds((T, D), F32), _sds((1, D), F32)),
               grid=(T // tm,), in_specs=[row, vec, row],
               out_specs=(pl.BlockSpec((1, 128), lambda i: (0, 0)), row, vec), sem=("arbitrary",))(x, w, tgt)


def _col_tile(n):
    for t in (1536, 1408, 1024, 768, 512, 384, 256, 128):
        if n % t == 0:
            return t
    return n


def mm_nn(a, b, name, bias=None, residual=None, out_dtype=F32, cols=None):
    T, K = a.shape
    first, end = cols or (0, b.shape[1])
    N = end - first
    tm, tn = _tile(T, 512), _col_tile(N)
    assert first % tn == 0 and (cols is None or (bias is None and residual is None))
    j0 = first // tn

    def body(a_ref, b_ref, *rest):
        o_ref = rest[-1]
        acc = _dot(a_ref[...].astype(BF16), b_ref[...])
        for extra in rest[:-1]:
            acc = acc + extra[...]
        o_ref[...] = acc.astype(out_dtype)

    in_specs = [pl.BlockSpec((tm, K), lambda j, i: (i, 0)), pl.BlockSpec((K, tn), lambda j, i: (0, j0 + j))]
    args = [a, b]
    if bias is not None:
        in_specs.append(pl.BlockSpec((1, tn), lambda j, i: (0, j)))
        args.append(bias)
    if residual is not None:
        in_specs.append(pl.BlockSpec((tm, tn), lambda j, i: (i, j)))
        args.append(residual)
    return _pc(body, name=name, out_shape=_sds((T, N), out_dtype), grid=(N // tn, T // tm), in_specs=in_specs,
               out_specs=pl.BlockSpec((tm, tn), lambda j, i: (i, j)), sem=("parallel", "parallel"))(*args)


def mm_nt(a, b, name, out_dtype=F32):
    T, N = a.shape
    K = b.shape[0]
    tm = _tile(T, 512)

    def body(a_ref, b_ref, o_ref):
        o_ref[...] = _dot(a_ref[...].astype(BF16), b_ref[...], NT).astype(out_dtype)

    return _pc(body, name=name, out_shape=_sds((T, K), out_dtype), grid=(T // tm,),
               in_specs=[pl.BlockSpec((tm, N), lambda i: (i, 0)), pl.BlockSpec((K, N), lambda i: (0, 0))],
               out_specs=pl.BlockSpec((tm, K), lambda i: (i, 0)), sem=("parallel",))(a, b)


def mm_tn(a, b, name):
    T, K = a.shape
    N = b.shape[1]
    tt, tn = _tile(T, 1024), _col_tile(N)

    def body(a_ref, b_ref, o_ref):
        @pl.when(pl.program_id(1) == 0)
        def _():
            o_ref[...] = jnp.zeros_like(o_ref)
        o_ref[...] += _dot(a_ref[...].astype(BF16), b_ref[...].astype(BF16), TN)

    return _pc(body, name=name, out_shape=_sds((K, N), F32), grid=(N // tn, T // tt),
               in_specs=[pl.BlockSpec((tt, K), lambda j, t: (t, 0)), pl.BlockSpec((tt, tn), lambda j, t: (t, j))],
               out_specs=pl.BlockSpec((K, tn), lambda j, t: (0, j)), sem=("parallel", "arbitrary"))(a, b)


def ffn_up(xn, wgu, name, deps=()):
    T = xn.shape[0]
    tm = _tile(T, 1024)

    def body(x_ref, w_ref, *rest):
        xv = x_ref[...]
        for j in range(2 * N_FB):
            rest[-1][j] = _dot(xv, w_ref[j]).astype(BF16)

    return _pc(body, name=name, out_shape=_sds((2 * N_FB, T, FB), BF16), grid=(T // tm,),
               in_specs=[pl.BlockSpec((tm, D), lambda i: (i, 0)), _resident((2 * N_FB, D, FB))] + [DEP_SPEC] * len(deps),
               out_specs=pl.BlockSpec((2 * N_FB, tm, FB), lambda i: (0, i, 0)), sem=("parallel",))(xn, wgu, *deps)


def ffn_down(gu, wd, x, name):
    T = x.shape[0]
    tm = _tile(T, 512)

    def body(gu_ref, w_ref, x_ref, o_ref):
        acc = jnp.zeros((tm, D), F32)
        for g in range(N_FB):
            gate, up = gu_ref[g], gu_ref[N_FB + g]
            acc = acc + _dot(gate * _sigmoid(gate) * up, w_ref[g])
        o_ref[...] = x_ref[...] + 0.5 * acc

    row = pl.BlockSpec((tm, D), lambda i: (i, 0))
    return _pc(body, name=name, out_shape=_sds((T, D), F32), grid=(T // tm,),
               in_specs=[pl.BlockSpec((2 * N_FB, tm, FB), lambda i: (0, i, 0)),
                         _resident((N_FB, FB, D)), row],
               out_specs=row, sem=("parallel",))(gu, wd, x)


def _resident(shape):
    return pl.BlockSpec(shape, lambda *_: (0,) * len(shape), pipeline_mode=pl.Buffered(1))


def _store_blocks_bf16(acc, out_hbm, stage, sem):
    for j in range(acc.shape[0]):
        stage[...] = acc[j].astype(BF16)
        copy = pltpu.make_async_copy(stage, out_hbm.at[j], sem)
        copy.start()
        copy.wait()


def ffn_bwd_hidden(dout, wd, gu, name, deps=()):
    T = dout.shape[0]
    tm = _tile(T, 512)
    n_t = T // tm

    def body(d_ref, w_ref, gu_ref, *rest):
        dgu_ref, dwd_hbm, acc, stage, sem = rest[-5:]
        t = pl.program_id(0)

        @pl.when(t == 0)
        def _():
            acc[...] = jnp.zeros_like(acc)
        dy = (0.5 * d_ref[...]).astype(BF16)
        for g in range(N_FB):
            gate, up = gu_ref[g], gu_ref[N_FB + g]
            sg = _sigmoid(gate)
            silu = gate * sg
            dact = _dot(dy, w_ref[g], NT).astype(BF16)
            acc[g] += _dot(silu * up, dy, TN)
            dgu_ref[g] = dact * up * (sg * (1.0 + gate * (1.0 - sg)))
            dgu_ref[N_FB + g] = dact * silu

        @pl.when(t == n_t - 1)
        def _():
            _store_blocks_bf16(acc, dwd_hbm, stage, sem)

    return _pc(body, name=name, out_shape=(_sds((2 * N_FB, T, FB), BF16), _sds((N_FB, FB, D), BF16)), grid=(n_t,),
               in_specs=[pl.BlockSpec((tm, D), lambda i: (i, 0)), _resident((N_FB, FB, D)),
                         pl.BlockSpec((2 * N_FB, tm, FB), lambda i: (0, i, 0))] + [DEP_SPEC] * len(deps),
               out_specs=(pl.BlockSpec((2 * N_FB, tm, FB), lambda i: (0, i, 0)), pl.BlockSpec(memory_space=pl.ANY)),
               scratch=[pltpu.VMEM((N_FB, FB, D), F32), pltpu.VMEM((FB, D), BF16), pltpu.SemaphoreType.DMA],
               sem=("arbitrary",))(dout, wd, gu, *deps)


def ffn_bwd_input(dgu, wgu, x, dout, nw, name, deps=()):
    T = x.shape[0]
    tm = _tile(T, 512)

    def body(dgu_ref, w_ref, x_ref, d_ref, nw_ref, *rest):
        dx_ref, dnw_ref = rest[-2:]
        dxn = jnp.zeros((tm, D), F32)
        for j in range(2 * N_FB):
            dxn = dxn + _dot(dgu_ref[j], w_ref[j], NT)
        dx, dw = _rms_bwd(x_ref[...], nw_ref[...], dxn)
        dx_ref[...] = d_ref[...] + dx

        @pl.when(pl.program_id(0) == 0)
        def _():
            dnw_ref[...] = jnp.zeros_like(dnw_ref)
        dnw_ref[...] += dw

    row = pl.BlockSpec((tm, D), lambda i: (i, 0))
    vec = pl.BlockSpec((1, D), lambda i: (0, 0))
    return _pc(body, name=name, out_shape=(_sds((T, D), F32), _sds((1, D), F32)), grid=(T // tm,),
               in_specs=[pl.BlockSpec((2 * N_FB, tm, FB), lambda i: (0, i, 0)), _resident((2 * N_FB, D, FB)),
                         row, row, vec] + [DEP_SPEC] * len(deps),
               out_specs=(row, vec), sem=("arbitrary",))(dgu, wgu, x, dout, nw, *deps)


def ffn_wgrad_gu(xn, dgu, name):
    T = xn.shape[0]
    tt = _tile(T, 1024)
    n_t = T // tt

    def body(x_ref, d_ref, dw_hbm, acc, stage, sem):
        t = pl.program_id(0)

        @pl.when(t == 0)
        def _():
            acc[...] = jnp.zeros_like(acc)
        xn_tile = x_ref[...]
        for j in range(2 * N_FB):
            acc[j] += _dot(xn_tile, d_ref[j], TN)

        @pl.when(t == n_t - 1)
        def _():
            _store_blocks_bf16(acc, dw_hbm, stage, sem)

    return _pc(body, name=name, out_shape=_sds((2 * N_FB, D, FB), BF16), grid=(n_t,),
               in_specs=[pl.BlockSpec((tt, D), lambda t: (t, 0)), pl.BlockSpec((2 * N_FB, tt, FB), lambda t: (0, t, 0))],
               out_specs=pl.BlockSpec(memory_space=pl.ANY),
               scratch=[pltpu.VMEM((2 * N_FB, D, FB), F32), pltpu.VMEM((D, FB), BF16), pltpu.SemaphoreType.DMA],
               sem=("arbitrary",))(xn, dgu)


def ffn_forward(x, nw, wgu, wd, tag):
    T = x.shape[0]
    tm = _tile(T, 512)

    def body(x_ref, nw_ref, wgu_ref, wd_ref, o_ref, xn_ref, gu_ref):
        xv = x_ref[...]
        xn = _rms_fwd(xv, nw_ref[...]).astype(BF16)
        xn_ref[...] = xn
        acc = jnp.zeros((tm, D), F32)
        for g in range(N_FB):
            gate = _dot(xn, wgu_ref[g]).astype(BF16)
            up = _dot(xn, wgu_ref[N_FB + g]).astype(BF16)
            gu_ref[g] = gate
            gu_ref[N_FB + g] = up
            acc = acc + _dot(gate * _sigmoid(gate) * up, wd_ref[g])
        o_ref[...] = xv + 0.5 * acc

    row = pl.BlockSpec((tm, D), lambda i: (i, 0))
    out, xn, gu = _pc(body, name=f"{tag}_fwd",
                      out_shape=(_sds((T, D), F32), _sds((T, D), BF16), _sds((2 * N_FB, T, FB), BF16)), grid=(T // tm,),
                      in_specs=[row, pl.BlockSpec((1, D), lambda i: (0, 0)), _resident((2 * N_FB, D, FB)),
                                _resident((N_FB, FB, D))],
                      out_specs=(row, row, pl.BlockSpec((2 * N_FB, tm, FB), lambda i: (0, i, 0))),
                      sem=("parallel",))(x, nw, wgu, wd)
    return out, (x, xn, gu)


def ffn_backward(dout, saved, nw, wgu, wd, tag, deps=(), on_grads=None):
    x, xn, gu = saved
    dgu, dwd = ffn_bwd_hidden(dout, wd, gu, f"{tag}_bwd_hidden", deps)
    dwgu = ffn_wgrad_gu(xn, dgu, f"{tag}_wgrad_gu")
    late = on_grads(dwgu, dwd) if on_grads else ()
    dx, dnw = ffn_bwd_input(dgu, wgu, x, dout, nw, f"{tag}_bwd_input", late)
    return dx, dnw, dwgu, dwd


N_QKV_BLK = 3 * HEADS_A
Z_BLK0 = N_QKV_BLK
MAIN_COLS = 4 * D
HALO = 16


def _conv_taps(xcat, w):
    c = xcat[HALO:] * w[3:4]
    for k in range(3):
        c = c + pltpu.roll(xcat, 3 - k, 0)[HALO:] * w[k:k + 1]
    return c


def _head_cols(h):
    return slice(128 * h, 128 * (h + 1))


def gdn_conv_fwd(proj, wconv, name):
    T = proj.shape[0]
    tm = _tile(T, 512)

    def body(cur_ref, prev_ref, w_ref, c_ref, y_ref):
        kind, t = pl.program_id(0), pl.program_id(1)
        prev = jnp.where(t > 0, prev_ref[...].astype(F32), 0.0)
        c = _conv_taps(jnp.concatenate([prev, cur_ref[...].astype(F32)], axis=0), w_ref[...])
        c_ref[...] = c.astype(BF16)
        s = c * _sigmoid(c)
        scale = jnp.where(kind == 0, DK ** -0.5, 1.0)
        for h in range(HEADS_A):
            sh = s[:, _head_cols(h)]
            r = lax.rsqrt(jnp.sum(sh * sh, axis=-1, keepdims=True) + EPS)
            y_ref[h] = (sh * jnp.where(kind < 2, r * scale, 1.0)).astype(BF16)

    return _pc(body, name=name, out_shape=(_sds((T, 3 * D), BF16), _sds((N_QKV_BLK, T, 128), BF16)),
               grid=(3, T // tm),
               in_specs=[pl.BlockSpec((tm, D), lambda kd, t: (t, kd)),
                         pl.BlockSpec((HALO, D), lambda kd, t: (jnp.maximum(t * (tm // HALO) - 1, 0), kd)),
                         pl.BlockSpec((4, D), lambda kd, t: (0, kd))],
               out_specs=(pl.BlockSpec((tm, D), lambda kd, t: (t, kd)),
                          pl.BlockSpec((HEADS_A, tm, 128), lambda kd, t: (kd, t, 0))),
               sem=("parallel", "parallel"))(proj, proj, wconv)


def gdn_conv_bwd(dqkv, c, proj, wconv, name):
    T = c.shape[0]
    tm = _tile(T, 512)
    n_t = T // tm

    def body(dy_ref, dyn_ref, c_ref, cn_ref, x_ref, xp_ref, w_ref, dx_ref, dw_ref):
        kind, t = pl.program_id(0), pl.program_id(1)
        scale = jnp.where(kind == 0, DK ** -0.5, 1.0)

        def act_bwd(dy, cv):
            sg = _sigmoid(cv)
            s = cv * sg
            parts = []
            for h in range(HEADS_A):
                sh, dyh = s[:, _head_cols(h)], dy[h]
                r = lax.rsqrt(jnp.sum(sh * sh, axis=-1, keepdims=True) + EPS)
                ds_norm = scale * r * (dyh - (r * r) * sh * jnp.sum(dyh * sh, axis=-1, keepdims=True))
                parts.append(jnp.where(kind < 2, ds_norm, dyh))
            return jnp.concatenate(parts, axis=1) * (sg * (1.0 + cv * (1.0 - sg)))

        w = w_ref[...]
        dcur = act_bwd(dy_ref[...].astype(F32), c_ref[...].astype(F32))
        dnext = jnp.where(t < n_t - 1, act_bwd(dyn_ref[...].astype(F32), cn_ref[...].astype(F32)), 0.0)
        dcat = jnp.concatenate([dcur, dnext], axis=0)
        dx = dcur * w[3:4]
        for k in range(3):
            dx = dx + pltpu.roll(dcat, tm + HALO - (3 - k), 0)[:tm] * w[k:k + 1]
        dx_ref[...] = dx.astype(BF16)
        xprev = jnp.where(t > 0, xp_ref[...].astype(F32), 0.0)
        xcat = jnp.concatenate([xprev, x_ref[...].astype(F32)], axis=0)
        rows = [jnp.sum(dcur * pltpu.roll(xcat, 3 - k, 0)[HALO:], axis=0, keepdims=True) for k in range(3)]
        rows.append(jnp.sum(dcur * xcat[HALO:], axis=0, keepdims=True))

        @pl.when(t == 0)
        def _():
            dw_ref[...] = jnp.zeros_like(dw_ref)
        dw_ref[...] += jnp.concatenate(rows, axis=0)

    def nxt(t):
        return jnp.minimum((t + 1) * (tm // HALO), T // HALO - 1)

    cur = pl.BlockSpec((tm, D), lambda kd, t: (t, kd))
    return _pc(body, name=name, out_shape=(_sds((T, 3 * D), BF16), _sds((4, 3 * D), F32)), grid=(3, n_t),
               in_specs=[pl.BlockSpec((HEADS_A, tm, 128), lambda kd, t: (kd, t, 0)),
                         pl.BlockSpec((HEADS_A, HALO, 128), lambda kd, t: (kd, nxt(t), 0)),
                         cur, pl.BlockSpec((HALO, D), lambda kd, t: (nxt(t), kd)),
                         cur, pl.BlockSpec((HALO, D), lambda kd, t: (jnp.maximum(t * (tm // HALO) - 1, 0), kd)),
                         pl.BlockSpec((4, D), lambda kd, t: (0, kd))],
               out_specs=(cur, pl.BlockSpec((4, D), lambda kd, t: (0, kd))),
               sem=("parallel", "arbitrary"))(dqkv, dqkv, c, c, proj, proj, wconv)


def _chunk_masks(n):
    ri = lax.broadcasted_iota(jnp.int32, (n, n), 0)
    ci = lax.broadcasted_iota(jnp.int32, (n, n), 1)
    same = (ri // CHUNK) == (ci // CHUNK)
    return same & (ri >= ci), same & (ri <= ci)


def gdn_gate_fwd(ba, al, dtb, name):
    T = ba.shape[0]
    tg = _tile(T, PREP_T)

    def body(ba_ref, al_ref, dtb_ref, o_ref):
        x = ba_ref[...]
        lane = lax.broadcasted_iota(jnp.int32, x.shape, 1)
        is_a = (lane >= HEADS_A) & (lane < 2 * HEADS_A)
        g = jnp.where(is_a, -jnp.exp(al_ref[...]) * _softplus(x + dtb_ref[...]), 0.0)
        lower, _ = _chunk_masks(tg)
        gc = _dot(lower.astype(F32), g, precision=HI)
        o_ref[...] = jnp.where(lane < HEADS_A, _sigmoid(x), gc)

    vec = pl.BlockSpec((1, 128), lambda i: (0, 0))
    return _pc(body, name=name, out_shape=_sds((T, 128), F32), grid=(T // tg,),
               in_specs=[pl.BlockSpec((tg, 128), lambda i: (i, 0)), vec, vec],
               out_specs=pl.BlockSpec((tg, 128), lambda i: (i, 0)), sem=("parallel",))(ba, al, dtb)


def gdn_gate_bwd(ba, al, dtb, dgb, name):
    T = ba.shape[0]
    tg = _tile(T, PREP_T)

    def body(ba_ref, al_ref, dtb_ref, dgb_ref, dba_ref, dal_ref, ddt_ref):
        x, d = ba_ref[...], dgb_ref[...]
        lane = lax.broadcasted_iota(jnp.int32, x.shape, 1)
        is_b = lane < HEADS_A
        is_a = (lane >= HEADS_A) & (lane < 2 * HEADS_A)
        beta = _sigmoid(x)
        e_a = jnp.exp(al_ref[...])
        z = x + dtb_ref[...]
        g = jnp.where(is_a, -e_a * _softplus(z), 0.0)
        _, upper = _chunk_masks(tg)
        dg = _dot(upper.astype(F32), jnp.where(is_a, d, 0.0), precision=HI)
        da = jnp.where(is_a, dg * (-e_a) * _sigmoid(z), 0.0)
        db = jnp.where(is_b, d * beta * (1.0 - beta), 0.0)
        dba_ref[...] = (da + db).astype(BF16)

        @pl.when(pl.program_id(0) == 0)
        def _():
            dal_ref[...] = jnp.zeros_like(dal_ref)
            ddt_ref[...] = jnp.zeros_like(ddt_ref)
        dal_ref[...] += jnp.sum(dg * g, axis=0, keepdims=True)
        ddt_ref[...] += jnp.sum(da, axis=0, keepdims=True)

    vec = pl.BlockSpec((1, 128), lambda i: (0, 0))
    blk = pl.BlockSpec((tg, 128), lambda i: (i, 0))
    return _pc(body, name=name, out_shape=(_sds((T, 128), BF16), _sds((1, 128), F32), _sds((1, 128), F32)),
               grid=(T // tg,), in_specs=[blk, vec, vec, blk],
               out_specs=(blk, vec, vec), sem=("arbitrary",))(ba, al, dtb, dgb)


def _bmm(a, b, dims, precision=None):
    return lax.dot_general(a, b, dims, preferred_element_type=F32, precision=precision)


B_NN = (((2,), (1,)), ((0,), (0,)))
B_NT = (((2,), (2,)), ((0,), (0,)))


def _select_lane(x, lane_index):
    lane = lax.broadcasted_iota(jnp.int32, x.shape, x.ndim - 1)
    return jnp.sum(jnp.where(lane == lane_index, x, 0.0), axis=-1, keepdims=True)


B_TN = (((1,), (1,)), ((0,), (0,)))


def _bmm_split(a, b, dims):
    ah, bh = a.astype(BF16), b.astype(BF16)
    al, bl = (a - ah.astype(F32)).astype(BF16), (b - bh.astype(F32)).astype(BF16)
    return _bmm(ah, bh, dims) + (_bmm(ah, bl, dims) + _bmm(al, bh, dims))


@jax.custom_vjp
def _bmm_f32(a, b):
    return _bmm_split(a, b, B_NN)


def _bmm_f32_fwd(a, b):
    return _bmm_split(a, b, B_NN), (a, b)


def _bmm_bf16(a, b, dims):
    return _bmm(a.astype(BF16), b.astype(BF16), dims)


def _bmm_f32_bwd(res, dc):
    a, b = res
    return _bmm_bf16(dc, b, B_NT), _bmm_bf16(a, dc, B_TN)


_bmm_f32.defvjp(_bmm_f32_fwd, _bmm_f32_bwd)


def _tri_inverse(lmat):
    ri = lax.broadcasted_iota(jnp.int32, lmat.shape, 1)
    ci = lax.broadcasted_iota(jnp.int32, lmat.shape, 2)
    eye = jnp.where(ri == ci, 1.0, 0.0)
    inv = eye - lmat
    power = lmat
    for _ in range(5):
        power = _bmm_bf16(power, power, B_NN)
        inv = inv + _bmm_bf16(inv, power, B_NN)
    return _bmm_split(inv, 2.0 * eye - _bmm_split(eye + lmat, inv, B_NN), B_NN)


def _stored_inverse(x):
    @jax.custom_vjp
    def inverse(lmat):
        return x

    def fwd(lmat):
        return x, None

    def bwd(_, dx):
        return (-_bmm_bf16(_bmm_bf16(x, dx, B_TN), x, B_NT),)

    inverse.defvjp(fwd, bwd)
    return inverse


def _gdn_prep(q, k, v, gb, h, inverse):
    nb = q.shape[0]
    beta = _select_lane(gb, h)
    gc = _select_lane(gb, HEADS_A + h)
    ri = lax.broadcasted_iota(jnp.int32, (nb, CHUNK, CHUNK), 1)
    ci = lax.broadcasted_iota(jnp.int32, (nb, CHUNK, CHUNK), 2)
    lower, strict, eye = ri >= ci, ri > ci, ri == ci
    gcol = jnp.broadcast_to(gc, (nb, CHUNK, CHUNK))
    grow = jnp.swapaxes(gcol, 1, 2)
    decay = jnp.where(lower, jnp.exp(jnp.where(lower, gcol - grow, 0.0)), 0.0)
    kb = k * beta
    kbf = k.astype(BF16)
    inv = inverse(jnp.where(strict, _bmm(kb.astype(BF16), kbf, B_NT) * decay, 0.0))
    eg = jnp.exp(gc)
    sol = _bmm_f32(inv, jnp.concatenate([v * beta, kb * eg], axis=-1))
    aqk = _bmm(q.astype(BF16), kbf, B_NT) * decay
    g_last = gc[:, CHUNK - 1:CHUNK, :]
    gl = jnp.broadcast_to(jnp.exp(g_last), (nb, 1, 128))
    return (sol[..., :DK], sol[..., DK:], q * eg, k * jnp.exp(g_last - gc), aqk, gl), inv


def gdn_prep_fwd(qkv, gb, name):
    T = qkv.shape[1]
    tp = _tile(T, PREP_T)
    nb = tp // CHUNK

    def body(q_ref, k_ref, v_ref, gb_ref, u_ref, w_ref, qd_ref, kd_ref, a_ref, gl_ref, inv_ref):
        h = pl.program_id(1)
        shp = (nb, CHUNK, 128)
        q, k, v = (ref[0].astype(F32).reshape(shp) for ref in (q_ref, k_ref, v_ref))
        (u, w, qd, kd, aqk, gl), inv = _gdn_prep(q, k, v, gb_ref[...].reshape(shp), h, _tri_inverse)
        u_ref[0] = u.reshape(tp, 128)
        w_ref[0] = w.reshape(tp, 128).astype(BF16)
        qd_ref[0] = qd.reshape(tp, 128).astype(BF16)
        kd_ref[0] = kd.reshape(tp, 128).astype(BF16)
        a_ref[0] = aqk.reshape(tp, CHUNK).astype(BF16)
        gl_ref[0] = gl.reshape(nb, 1, 128)
        inv_ref[0] = inv.reshape(tp, CHUNK)

    def head(off):
        return pl.BlockSpec((1, tp, 128), lambda n, h: (h + off, n, 0))

    matmul_only = _sds((HEADS_A, T, 128), BF16)
    narrow = pl.BlockSpec((1, tp, CHUNK), lambda n, h: (h, n, 0))
    return _pc(body, name=name,
               out_shape=(_sds((HEADS_A, T, 128), F32), matmul_only, matmul_only, matmul_only, _sds((HEADS_A, T, CHUNK), BF16),
                          _sds((HEADS_A, T // CHUNK, 1, 128), F32), _sds((HEADS_A, T, CHUNK), F32)),
               grid=(T // tp, HEADS_A),
               in_specs=[head(0), head(HEADS_A), head(2 * HEADS_A), pl.BlockSpec((tp, 128), lambda n, h: (n, 0))],
               out_specs=(head(0), head(0), head(0), head(0), narrow,
                          pl.BlockSpec((1, nb, 1, 128), lambda n, h: (h, n, 0, 0)), narrow),
               sem=("parallel", "parallel"))(qkv, qkv, qkv, gb)


def gdn_prep_bwd(qkv, gb, inv, du, dw, dqd, dkd, da, dgl, name):
    T = qkv.shape[1]
    tp = _tile(T, PREP_T)
    nb = tp // CHUNK

    def body(q_ref, k_ref, v_ref, gb_ref, inv_ref, du_ref, dw_ref, dqd_ref, dkd_ref, da_ref, dgl_ref, dqkv_ref, dgb_ref):
        h = pl.program_id(1)
        shp = (nb, CHUNK, 128)
        stored = _stored_inverse(inv_ref[0].reshape(nb, CHUNK, CHUNK))
        q, k, v = (ref[0].astype(F32).reshape(shp) for ref in (q_ref, k_ref, v_ref))
        _, vjp = jax.vjp(lambda q, k, v, gb: _gdn_prep(q, k, v, gb, h, stored)[0], q, k, v, gb_ref[...].reshape(shp))
        dq, dk, dv, dgb = vjp((du_ref[0].reshape(shp), dw_ref[0].reshape(shp), dqd_ref[0].reshape(shp),
                               dkd_ref[0].reshape(shp), da_ref[0].reshape(nb, CHUNK, CHUNK), dgl_ref[0].reshape(nb, 1, 128)))
        dqkv_ref[h] = dq.reshape(tp, 128).astype(BF16)
        dqkv_ref[HEADS_A + h] = dk.reshape(tp, 128).astype(BF16)
        dqkv_ref[2 * HEADS_A + h] = dv.reshape(tp, 128).astype(BF16)

        @pl.when(h == 0)
        def _():
            dgb_ref[...] = jnp.zeros_like(dgb_ref)
        dgb_ref[...] += dgb.reshape(tp, 128)

    def head(off):
        return pl.BlockSpec((1, tp, 128), lambda n, h: (h + off, n, 0))

    narrow = pl.BlockSpec((1, tp, CHUNK), lambda n, h: (h, n, 0))
    return _pc(body, name=name, out_shape=(_sds((N_QKV_BLK, T, 128), BF16), _sds((T, 128), F32)),
               grid=(T // tp, HEADS_A),
               in_specs=[head(0), head(HEADS_A), head(2 * HEADS_A), pl.BlockSpec((tp, 128), lambda n, h: (n, 0)), narrow,
                         head(0), head(0), head(0), head(0), narrow,
                         pl.BlockSpec((1, nb, 1, 128), lambda n, h: (h, n, 0, 0))],
               out_specs=(pl.BlockSpec((N_QKV_BLK, tp, 128), lambda n, h: (0, n, 0)),
                          pl.BlockSpec((tp, 128), lambda n, h: (n, 0))),
               sem=("parallel", "arbitrary"))(qkv, qkv, qkv, gb, inv, du, dw, dqd, dkd, da, dgl)


def gdn_scan_fwd(u, w, qd, kd, aqk, gl, name):
    T = u.shape[1]
    n_chunks = T // CHUNK

    def body(u_ref, w_ref, qd_ref, kd_ref, a_ref, gl_ref, o_ref, sin_ref, state):
        @pl.when(pl.program_id(0) == 0)
        def _():
            state[...] = jnp.zeros_like(state)
        s = state[...]
        sb = s.astype(BF16)
        sin_ref[0] = sb
        both = _bmm(jnp.concatenate([w_ref[...], qd_ref[...]], axis=1).astype(BF16), sb, B_NN)
        vn = (u_ref[...] - both[:, :CHUNK]).astype(BF16)
        o_ref[...] = both[:, CHUNK:] + _bmm(a_ref[...].astype(BF16), vn, B_NN)
        state[...] = s * gl_ref[:, 0] + _bmm(kd_ref[...].astype(BF16), vn, B_TN)

    blk = pl.BlockSpec((HEADS_A, CHUNK, 128), lambda n: (0, n, 0))
    return _pc(body, name=name,
               out_shape=(_sds((HEADS_A, T, 128), F32), _sds((n_chunks, HEADS_A, DK, 128), BF16)), grid=(n_chunks,),
               in_specs=[blk, blk, blk, blk, pl.BlockSpec((HEADS_A, CHUNK, CHUNK), lambda n: (0, n, 0)),
                         pl.BlockSpec((HEADS_A, 1, 1, 128), lambda n: (0, n, 0, 0))],
               out_specs=(blk, pl.BlockSpec((1, HEADS_A, DK, 128), lambda n: (n, 0, 0, 0))),
               scratch=[pltpu.VMEM((HEADS_A, DK, 128), F32)], sem=("arbitrary",))(u, w, qd, kd, aqk, gl)


def gdn_scan_bwd(u, w, qd, kd, aqk, gl, sin, do, name):
    T = u.shape[1]
    n_chunks = T // CHUNK

    def body(u_ref, w_ref, qd_ref, kd_ref, a_ref, gl_ref, sin_ref, do_ref,
             du_ref, dw_ref, dqd_ref, dkd_ref, da_ref, dgl_ref, dstate):
        @pl.when(pl.program_id(0) == 0)
        def _():
            dstate[...] = jnp.zeros_like(dstate)
        lane0 = lax.broadcasted_iota(jnp.int32, (HEADS_A, 1, 128), 2) == 0
        sb = sin_ref[0]
        s = sb.astype(F32)
        wb, qdb, kdb = w_ref[...].astype(BF16), qd_ref[...].astype(BF16), kd_ref[...].astype(BF16)
        ab, dob = a_ref[...].astype(BF16), do_ref[...].astype(BF16)
        vn = (u_ref[...] - _bmm(wb, sb, B_NN)).astype(BF16)
        ds_out = dstate[...]
        dsb = ds_out.astype(BF16)
        dqd_ref[...] = _bmm(dob, sb, B_NT)
        da_ref[...] = _bmm(dob, vn, B_NT)
        dv = _bmm(ab, dob, B_TN) + _bmm(kdb, dsb, B_NN)
        dkd_ref[...] = _bmm(vn, dsb, B_NT)
        dgl = jnp.sum(jnp.sum(ds_out * s, axis=2, keepdims=True), axis=1, keepdims=True)
        dgl_ref[:, 0] = jnp.where(lane0, dgl, 0.0)
        du_ref[...] = dv
        dvb = dv.astype(BF16)
        dw_ref[...] = -_bmm(dvb, sb, B_NT)
        dstate[...] = ds_out * gl_ref[:, 0] + _bmm(qdb, dob, B_TN) - _bmm(wb, dvb, B_TN)

    last = n_chunks - 1
    blk = pl.BlockSpec((HEADS_A, CHUNK, 128), lambda n: (0, last - n, 0))
    ablk = pl.BlockSpec((HEADS_A, CHUNK, CHUNK), lambda n: (0, last - n, 0))
    glblk = pl.BlockSpec((HEADS_A, 1, 1, 128), lambda n: (0, last - n, 0, 0))
    per_head = _sds((HEADS_A, T, 128), F32)
    return _pc(body, name=name,
               out_shape=(per_head, per_head, per_head, per_head, _sds((HEADS_A, T, CHUNK), F32),
                          _sds((HEADS_A, n_chunks, 1, 128), F32)), grid=(n_chunks,),
               in_specs=[blk, blk, blk, blk, ablk, glblk,
                         pl.BlockSpec((1, HEADS_A, DK, 128), lambda n: (last - n, 0, 0, 0)), blk],
               out_specs=(blk, blk, blk, blk, ablk, glblk),
               scratch=[pltpu.VMEM((HEADS_A, DK, 128), F32)], sem=("arbitrary",))(u, w, qd, kd, aqk, gl, sin, do)


def gdn_outnorm_fwd(o, proj, wn, name):
    T = o.shape[1]
    tm = _tile(T, 512)

    def body(o_ref, z_ref, wn_ref, y_ref):
        for h in range(HEADS_A):
            z = z_ref[:, 128 * h:128 * (h + 1)].astype(F32)
            y_ref[:, 128 * h:128 * (h + 1)] = (_rms_fwd(o_ref[h], wn_ref[...]) * (z * _sigmoid(z))).astype(BF16)

    return _pc(body, name=name, out_shape=_sds((T, D), BF16), grid=(T // tm,),
               in_specs=[pl.BlockSpec((HEADS_A, tm, 128), lambda i: (0, i, 0)),
                         pl.BlockSpec((tm, D), lambda i: (i, Z_BLK0 * 128 // D)), pl.BlockSpec((1, 128), lambda i: (0, 0))],
               out_specs=pl.BlockSpec((tm, D), lambda i: (i, 0)), sem=("parallel",))(o, proj, wn)


def gdn_outnorm_bwd(o, proj, wn, dy, name):
    T = o.shape[1]
    tm = _tile(T, 512)

    def body(o_ref, z_ref, wn_ref, dy_ref, do_ref, dz_ref, dwn_ref):
        wn = wn_ref[...]
        acc = jnp.zeros((1, 128), F32)
        for h in range(HEADS_A):
            cols = slice(128 * h, 128 * (h + 1))
            z, dyh, ov = z_ref[:, cols].astype(F32), dy_ref[:, cols], o_ref[h]
            sg = _sigmoid(z)
            do, dwn = _rms_bwd(ov, wn, dyh * (z * sg))
            do_ref[h] = do
            acc = acc + dwn
            dz_ref[:, cols] = (dyh * _rms_fwd(ov, wn) * (sg * (1.0 + z * (1.0 - sg)))).astype(BF16)

        @pl.when(pl.program_id(0) == 0)
        def _():
            dwn_ref[...] = jnp.zeros_like(dwn_ref)
        dwn_ref[...] += acc

    row = pl.BlockSpec((tm, D), lambda i: (i, 0))
    vec = pl.BlockSpec((1, 128), lambda i: (0, 0))
    hblk = pl.BlockSpec((HEADS_A, tm, 128), lambda i: (0, i, 0))
    return _pc(body, name=name, out_shape=(_sds((HEADS_A, T, 128), F32), _sds((T, D), BF16), _sds((1, 128), F32)),
               grid=(T // tm,),
               in_specs=[hblk, pl.BlockSpec((tm, D), lambda i: (i, Z_BLK0 * 128 // D)), vec, row],
               out_specs=(hblk, row, vec), sem=("arbitrary",))(o, proj, wn, dy)


def gdn_forward(x, nw, w_in, wconv, al, dtb, wn, w_out, tag, deps=()):
    h = rmsnorm_bf16(x, nw, f"{tag}_norm", deps)
    proj = mm_nn(h, w_in, f"{tag}_proj", out_dtype=BF16, cols=(0, MAIN_COLS))
    ba = mm_nn(h, w_in, f"{tag}_proj_ba", cols=(MAIN_COLS, A_COLS))
    c, qkv = gdn_conv_fwd(proj, wconv, f"{tag}_conv")
    gb = gdn_gate_fwd(ba, al, dtb, f"{tag}_gate")
    u, w, qd, kd, aqk, gl, inv = gdn_prep_fwd(qkv, gb, f"{tag}_prep")
    o, sin = gdn_scan_fwd(u, w, qd, kd, aqk, gl, f"{tag}_scan")
    on = gdn_outnorm_fwd(o, proj, wn, f"{tag}_outnorm")
    y = mm_nn(on, w_out, f"{tag}_out", residual=x)
    return y, (x, h, proj, ba, c, qkv, gb, inv, (u, w, qd, kd, aqk, gl), sin, o, on)


def gdn_backward(dout, saved, nw, w_in, wconv, al, dtb, wn, w_out, tag):
    x, h, proj, ba, c, qkv, gb, inv, prep, sin, o, on = saved
    d_on = mm_nt(dout, w_out, f"{tag}_out_bwd")
    dw_out = mm_tn(on, dout, f"{tag}_out_wgrad")
    do, dz, dwn = gdn_outnorm_bwd(o, proj, wn, d_on, f"{tag}_outnorm_bwd")
    du, dw, dqd, dkd, da, dgl = gdn_scan_bwd(*prep, sin, do, f"{tag}_scan_bwd")
    dqkv, dgb = gdn_prep_bwd(qkv, gb, inv, du, dw, dqd, dkd, da, dgl, f"{tag}_prep_bwd")
    dba, dal, ddt = gdn_gate_bwd(ba, al, dtb, dgb, f"{tag}_gate_bwd")
    dpre, dwconv = gdn_conv_bwd(dqkv, c, proj, wconv, f"{tag}_conv_bwd")
    dproj = jnp.concatenate([dpre, dz, dba], axis=1)
    dw_in = mm_tn(h, dproj, f"{tag}_proj_wgrad")
    dh = mm_nt(dproj, w_in, f"{tag}_proj_bwd")
    dx, dnw = rmsnorm_bwd_add(x, nw, dh, dout, f"{tag}_norm_bwd")
    return dx, dnw, dw_in, dwconv, dal, ddt, dwn, dw_out


N_KV, GROUP = 4, 4
KV_COLS = 2 * N_KV * B_HD
B_COLS = D + KV_COLS


@jax.custom_vjp
def _swap_lane_halves(x):
    return pltpu.roll(x, 64, 1)


_swap_lane_halves.defvjp(lambda x: (pltpu.roll(x, 64, 1), None), lambda _, g: (pltpu.roll(g, 64, 1),))


def _swa_block(q, kp, kc, vp, vc, sk, first):
    rows = GROUP * B_BLK
    qi = lax.broadcasted_iota(jnp.int32, (N_KV, rows, B_BLK), 1) % B_BLK
    kj = lax.broadcasted_iota(jnp.int32, (N_KV, rows, B_BLK), 2)
    from_cur = kj <= qi

    def batch(parts):
        return jnp.concatenate([part[None] for part in parts], axis=0)

    def per_kv(cur, prev):
        return batch([jnp.concatenate([cur[:, j * B_HD:(j + 1) * B_HD], prev[:, j * B_HD:(j + 1) * B_HD]], axis=0)
                      for j in range(N_KV)]).astype(BF16)

    qs = batch([jnp.concatenate([q[:, hq * B_HD:(hq + 1) * B_HD] for hq in range(GROUP * j, GROUP * (j + 1))], axis=0)
                for j in range(N_KV)]).astype(BF16)
    sink = batch([jnp.concatenate([jnp.broadcast_to(sk[:, hq:hq + 1], (B_BLK, 1))
                                   for hq in range(GROUP * j, GROUP * (j + 1))], axis=0) for j in range(N_KV)])
    both = _bmm(qs, per_kv(kc, kp), B_NT)
    s = jnp.where(from_cur, both[..., :B_BLK], jnp.where(first, -1e30, both[..., B_BLK:])) * (B_HD ** -0.5)
    m = lax.stop_gradient(jnp.maximum(jnp.max(s, axis=-1, keepdims=True), sink))
    e = jnp.exp((s - m).astype(BF16))
    den = jnp.sum(e.astype(F32), axis=-1, keepdims=True) + jnp.exp(sink - m)
    p = e * (1.0 / den).astype(BF16)
    zero = jnp.zeros_like(p)
    p_both = jnp.concatenate([jnp.where(from_cur, p, zero), jnp.where(from_cur, zero, p)], axis=-1)
    o = _bmm(p_both, per_kv(vc, vp), B_NN)
    return jnp.concatenate([o[j, g * B_BLK:(g + 1) * B_BLK] for j in range(N_KV) for g in range(GROUP)], axis=1)


def swa_core_fwd(proj, sk, name):
    T = proj.shape[0]
    half = N_KV * B_HD

    def body(q_ref, kvc_ref, kvp_ref, sk_ref, o_ref):
        kvc, kvp = kvc_ref[...], kvp_ref[...]
        o_ref[...] = _swa_block(q_ref[...], kvp[:, :half], kvc[:, :half], kvp[:, half:], kvc[:, half:], sk_ref[...],
                                pl.program_id(0) == 0).astype(BF16)

    return _pc(body, name=name, out_shape=_sds((T, D), BF16), grid=(T // B_BLK,),
               in_specs=[pl.BlockSpec((B_BLK, D), lambda n: (n, 0)),
                         pl.BlockSpec((B_BLK, KV_COLS), lambda n: (n, D // KV_COLS)),
                         pl.BlockSpec((B_BLK, KV_COLS), lambda n: (jnp.maximum(n - 1, 0), D // KV_COLS)),
                         pl.BlockSpec((1, 128), lambda n: (0, 0))],
               out_specs=pl.BlockSpec((B_BLK, D), lambda n: (n, 0)), sem=("parallel",))(proj, proj, proj, sk)


def swa_core_bwd(proj, sk, do, name):
    T = proj.shape[0]
    last = T // B_BLK - 1
    half = N_KV * B_HD

    def body(q_ref, kvc_ref, kvp_ref, sk_ref, do_ref, dproj_ref, dbias_ref, dsk_ref, carry):
        step = pl.program_id(0)
        first = step == last

        @pl.when(step == 0)
        def _():
            carry[...] = jnp.zeros_like(carry)
            dbias_ref[...] = jnp.zeros_like(dbias_ref)
            dsk_ref[...] = jnp.zeros_like(dsk_ref)
        kvc, kvp = kvc_ref[...], kvp_ref[...]
        _, vjp = jax.vjp(functools.partial(_swa_block, first=first), q_ref[...], kvp[:, :half], kvc[:, :half],
                         kvp[:, half:], kvc[:, half:], sk_ref[...])
        dq, dkp, dkc, dvp, dvc, dsk = vjp(do_ref[...])
        dkv = jnp.concatenate([dkc, dvc], axis=1) + carry[...]
        carry[...] = jnp.concatenate([dkp, dvp], axis=1)
        row = jnp.concatenate([dq, dkv], axis=1)
        dproj_ref[...] = row.astype(BF16)
        dbias_ref[...] += jnp.sum(row, axis=0, keepdims=True)
        dsk_ref[...] += dsk

    return _pc(body, name=name, out_shape=(_sds((T, B_COLS), BF16), _sds((1, B_COLS), F32), _sds((1, 128), F32)),
               grid=(T // B_BLK,),
               in_specs=[pl.BlockSpec((B_BLK, D), lambda n: (last - n, 0)),
                         pl.BlockSpec((B_BLK, KV_COLS), lambda n: (last - n, D // KV_COLS)),
                         pl.BlockSpec((B_BLK, KV_COLS), lambda n: (jnp.maximum(last - n - 1, 0), D // KV_COLS)),
                         pl.BlockSpec((1, 128), lambda n: (0, 0)), pl.BlockSpec((B_BLK, D), lambda n: (last - n, 0))],
               out_specs=(pl.BlockSpec((B_BLK, B_COLS), lambda n: (last - n, 0)),
                          pl.BlockSpec((1, B_COLS), lambda n: (0, 0)), pl.BlockSpec((1, 128), lambda n: (0, 0))),
               scratch=[pltpu.VMEM((B_BLK, KV_COLS), F32)], sem=("arbitrary",))(proj, proj, proj, sk, do)


def col_sum(a, name):
    T, N = a.shape
    tm = _tile(T, 1024)

    def body(a_ref, o_ref):
        @pl.when(pl.program_id(0) == 0)
        def _():
            o_ref[...] = jnp.zeros_like(o_ref)
        o_ref[...] += jnp.sum(a_ref[...].astype(F32), axis=0, keepdims=True)

    return _pc(body, name=name, out_shape=_sds((1, N), F32), grid=(T // tm,),
               in_specs=[pl.BlockSpec((tm, N), lambda i: (i, 0))], out_specs=pl.BlockSpec((1, N), lambda i: (0, 0)),
               sem=("arbitrary",))(a)


def swa_forward(x, nw, w_in, b_in, sk, w_out, b_out, tag):
    h = rmsnorm_bf16(x, nw, f"{tag}_norm")
    proj = mm_nn(h, w_in, f"{tag}_proj", bias=b_in)
    o = swa_core_fwd(proj, sk, f"{tag}_core")
    y = mm_nn(o, w_out, f"{tag}_out", bias=b_out, residual=x)
    return y, (x, h, proj, o)


def swa_backward(dout, saved, nw, w_in, b_in, sk, w_out, b_out, tag):
    x, h, proj, o = saved
    do = mm_nt(dout, w_out, f"{tag}_out_bwd")
    dw_out = mm_tn(o, dout, f"{tag}_out_wgrad")
    db_out = col_sum(dout, f"{tag}_out_bias_grad")
    dproj, db_in, dsk = swa_core_bwd(proj, sk, do, f"{tag}_core_bwd")
    dw_in = mm_tn(h, dproj, f"{tag}_proj_wgrad")
    dh = mm_nt(dproj, w_in, f"{tag}_proj_bwd")
    dx, dnw = rmsnorm_bwd_add(x, nw, dh, dout, f"{tag}_norm_bwd")
    return dx, dnw, dw_in, db_in, dsk, dw_out, db_out


MESH = pl.DeviceIdType.MESH


def _position():
    return lax.axis_index("x"), lax.axis_index("y"), lax.axis_index("c")


def _slot(x, y, c):
    return 4 * x + 2 * y + c


def _peer(x, y, c, k):
    return (1 - x if k & 4 else x, 1 - y if k & 2 else y, 1 - c if k & 1 else c)


HBM_SPEC = pl.BlockSpec(memory_space=pltpu.HBM)
SEM_SPEC = pl.BlockSpec(memory_space=pltpu.SEMAPHORE)
DEP_SPEC = pl.BlockSpec(memory_space=pl.ANY)
SIDE_EFFECT = pltpu.SideEffectType.DATAFLOW_SIDE_EFFECTING
N_PEERS = N_DEV - 1


def _push_copies(srcs, lands, send_sems, recv_sems, scatter):
    x, y, c = _position()
    me = _slot(x, y, c)
    copies = []
    for k in (1, 2, 4, 3, 5, 6, 7):
        peer = _peer(x, y, c, k)
        for a in range(len(srcs)):
            copies.append(pltpu.make_async_remote_copy(
                src_ref=srcs[a].at[_slot(*peer)] if scatter else srcs[a], dst_ref=lands[a].at[me],
                send_sem=send_sems.at[N_PEERS * a + k - 1], recv_sem=recv_sems.at[N_PEERS * a + k - 1],
                device_id=peer, device_id_type=MESH))
    return copies


def push_start(srcs, lands, name, scatter, deps=()):
    n = len(srcs)
    first_out = 2 * n + len(deps)

    def body(*refs):
        for cp in _push_copies(refs[:n], refs[n:2 * n], refs[first_out], refs[first_out + 1], scatter):
            cp.start()
        refs[-1][...] = jnp.zeros_like(refs[-1])

    passed = [pltpu.HBM(t.shape, t.dtype) for t in list(srcs) + list(lands)]
    res = pl.pallas_call(
        body, name=name,
        out_shape=(pltpu.SemaphoreType.DMA((N_PEERS * n,)), pltpu.SemaphoreType.DMA((N_PEERS * n,)), *passed, _sds((8, 128), F32)),
        in_specs=[HBM_SPEC] * (2 * n) + [DEP_SPEC] * len(deps),
        out_specs=(SEM_SPEC, SEM_SPEC, *([HBM_SPEC] * (2 * n)), pl.BlockSpec(memory_space=pltpu.VMEM)),
        input_output_aliases={i: 2 + i for i in range(2 * n)},
        compiler_params=pltpu.CompilerParams(has_side_effects=SIDE_EFFECT),
    )(*[pltpu.with_memory_space_constraint(t, pltpu.HBM) for t in list(srcs) + list(lands)], *deps)
    return (res[0], res[1], list(res[2:2 + n]), list(res[2 + n:2 + 2 * n])), res[-1]


def push_wait(handles, after, name, scatter):
    send_sems, recv_sems, srcs, lands = handles
    n = len(srcs)
    after = tuple(after) if isinstance(after, (tuple, list)) else (after,)

    def body(*refs):
        for cp in _push_copies(refs[:n], refs[n:2 * n], refs[2 * n], refs[2 * n + 1], scatter):
            cp.wait_send()
            cp.wait_recv()

    res = pl.pallas_call(
        body, name=name, out_shape=tuple(pltpu.HBM(t.shape, t.dtype) for t in srcs + lands),
        in_specs=[HBM_SPEC] * (2 * n) + [SEM_SPEC, SEM_SPEC] + [DEP_SPEC] * len(after), out_specs=tuple([HBM_SPEC] * (2 * n)),
        input_output_aliases={i: i for i in range(2 * n)},
        compiler_params=pltpu.CompilerParams(has_side_effects=SIDE_EFFECT),
    )(*srcs, *lands, send_sems, recv_sems, *after)
    return list(res[n:])


def gather_start(shards, name, deps=()):
    me = _slot(*_position())
    lands = [lax.dynamic_update_slice(lax.empty((N_DEV,) + t.shape, t.dtype), t[None], (me,) + (0,) * t.ndim) for t in shards]
    return push_start(shards, lands, name, scatter=False, deps=deps)


def exchange_start(parts, name):
    me = _slot(*_position())
    lands = [lax.dynamic_update_slice(lax.empty(t.shape, t.dtype), lax.dynamic_index_in_dim(t, me, 0, keepdims=True),
                                      (me,) + (0,) * (t.ndim - 1)) for t in parts]
    return push_start(parts, lands, name, scatter=True)


def _row_tile(rows, cols):
    best = rows
    for t in range(16, rows, 16):
        if rows % t == 0 and t * cols * 4 <= (1 << 20):
            best = t
    return best


def adam_update(parts, w, m, v, name):
    n_layers = len(parts)
    P, R, C = parts[0].shape
    tr = _row_tile(R, C)
    n_t = R // tr

    def body(*refs):
        p_refs = refs[:n_layers]
        w_ref, m_ref, v_ref, g_ref, d_ref, nm_ref, nv_ref = refs[n_layers:]
        for layer in range(n_layers):
            @pl.when(pl.program_id(0) == layer)
            def _(p_ref=p_refs[layer]):
                g = p_ref[0].astype(F32)
                for s in range(1, P):
                    g = g + p_ref[s].astype(F32)
                new_m = ADAM_B1 * m_ref[0] + (1.0 - ADAM_B1) * g
                new_v = ADAM_B2 * v_ref[0] + (1.0 - ADAM_B2) * (g * g)
                m_hat = new_m / (1.0 - ADAM_B1 ** ADAM_STEP)
                v_hat = new_v / (1.0 - ADAM_B2 ** ADAM_STEP)
                g_ref[0] = g
                d_ref[0] = -ADAM_LR * (m_hat / (jnp.sqrt(v_hat) + ADAM_EPS) + ADAM_WD * w_ref[0])
                nm_ref[0] = new_m
                nv_ref[0] = new_v

    def part_spec(layer):
        return pl.BlockSpec((P, tr, C), lambda l_, i: (0, jnp.where(l_ == layer, i, jnp.where(l_ < layer, 0, n_t - 1)), 0))

    blk = pl.BlockSpec((1, tr, C), lambda l_, i: (l_, i, 0))
    out = _sds((n_layers, R, C), F32)
    return _pc(body, name=name, out_shape=(out, out, out, out), grid=(n_layers, n_t),
               in_specs=[part_spec(layer) for layer in range(n_layers)] + [blk, blk, blk],
               out_specs=(blk, blk, blk, blk), sem=("arbitrary", "arbitrary"))(*parts, w, m, v)


WEIGHTS = ("ffn1_norm", "ffn1_w_gu", "ffn1_w_down", "mix_norm", "ffn2_norm", "ffn2_w_gu", "ffn2_w_down", "a_w_in",
           "a_w_conv", "a_A_log", "a_dt_bias", "a_out_norm", "a_w_out", "b_w_in", "b_b_in", "b_sinks", "b_w_out",
           "b_b_out", "final_norm")
SHARDED = ("ffn1_w_gu", "ffn1_w_down", "ffn2_w_gu", "ffn2_w_down", "a_w_in", "a_w_conv", "a_w_out", "b_w_in", "b_b_in",
           "b_w_out", "b_b_out")
MISC_LANES = dict(a_A_log=(0, 8), a_dt_bias=(8, 16), b_sinks=(16, 32), a_out_norm=(128, 256))
LOSS_LANE = 256


def _pack_small(t):
    misc = jnp.zeros((D,), F32)
    for key, (lo, hi) in MISC_LANES.items():
        misc = misc.at[lo:hi].set(t[key].reshape(-1))
    if "loss" in t:
        misc = misc.at[LOSS_LANE].set(t["loss"])
    return jnp.concatenate([t["ffn1_norm"], t["mix_norm"], t["ffn2_norm"], t["final_norm"].reshape(1, D), misc[None]], axis=0)


def _unpack_small(p, like):
    out = dict(ffn1_norm=p[0:2], mix_norm=p[2:4], ffn2_norm=p[4:6], final_norm=p[6])
    for key, (lo, hi) in MISC_LANES.items():
        out[key] = p[7, lo:hi].reshape(like[key].shape)
    return out


def kernel(x, ffn1_norm, ffn1_w_gu, ffn1_w_down, mix_norm, ffn2_norm, ffn2_w_gu, ffn2_w_down, a_w_in, a_w_conv, a_A_log, a_dt_bias, a_out_norm, a_w_out, b_w_in, b_b_in, b_sinks, b_w_out, b_b_out, final_norm, loss_target, m_ffn1_norm, m_ffn1_w_gu, m_ffn1_w_down, m_mix_norm, m_ffn2_norm, m_ffn2_w_gu, m_ffn2_w_down, m_a_w_in, m_a_w_conv, m_a_A_log, m_a_dt_bias, m_a_out_norm, m_a_w_out, m_b_w_in, m_b_b_in, m_b_sinks, m_b_w_out, m_b_b_out, m_final_norm, v_ffn1_norm, v_ffn1_w_gu, v_ffn1_w_down, v_mix_norm, v_ffn2_norm, v_ffn2_w_gu, v_ffn2_w_down, v_a_w_in, v_a_w_conv, v_a_A_log, v_a_dt_bias, v_a_out_norm, v_a_w_out, v_b_w_in, v_b_b_in, v_b_sinks, v_b_w_out, v_b_b_out, v_final_norm):
    w = dict(ffn1_norm=ffn1_norm, ffn1_w_gu=ffn1_w_gu, ffn1_w_down=ffn1_w_down, mix_norm=mix_norm, ffn2_norm=ffn2_norm, ffn2_w_gu=ffn2_w_gu, ffn2_w_down=ffn2_w_down, a_w_in=a_w_in, a_w_conv=a_w_conv, a_A_log=a_A_log, a_dt_bias=a_dt_bias, a_out_norm=a_out_norm, a_w_out=a_w_out, b_w_in=b_w_in, b_b_in=b_b_in, b_sinks=b_sinks, b_w_out=b_w_out, b_b_out=b_b_out, final_norm=final_norm)
    m = dict(ffn1_norm=m_ffn1_norm, ffn1_w_gu=m_ffn1_w_gu, ffn1_w_down=m_ffn1_w_down, mix_norm=m_mix_norm, ffn2_norm=m_ffn2_norm, ffn2_w_gu=m_ffn2_w_gu, ffn2_w_down=m_ffn2_w_down, a_w_in=m_a_w_in, a_w_conv=m_a_w_conv, a_A_log=m_a_A_log, a_dt_bias=m_a_dt_bias, a_out_norm=m_a_out_norm, a_w_out=m_a_w_out, b_w_in=m_b_w_in, b_b_in=m_b_b_in, b_sinks=m_b_sinks, b_w_out=m_b_w_out, b_b_out=m_b_b_out, final_norm=m_final_norm)
    v = dict(ffn1_norm=v_ffn1_norm, ffn1_w_gu=v_ffn1_w_gu, ffn1_w_down=v_ffn1_w_down, mix_norm=v_mix_norm, ffn2_norm=v_ffn2_norm, ffn2_w_gu=v_ffn2_w_gu, ffn2_w_down=v_ffn2_w_down, a_w_in=v_a_w_in, a_w_conv=v_a_w_conv, a_A_log=v_a_A_log, a_dt_bias=v_a_dt_bias, a_out_norm=v_a_out_norm, a_w_out=v_a_w_out, b_w_in=v_b_w_in, b_b_in=v_b_b_in, b_sinks=v_b_sinks, b_w_out=v_b_w_out, b_b_out=v_b_b_out, final_norm=v_final_norm)
    T = x.shape[1]
    x0, tgt = x.reshape(T, D), loss_target.reshape(T, D)

    def cast(t):
        return t.astype(BF16)

    h0, t0 = gather_start([cast(ffn1_w_gu[0])], "gather0_start")
    a_log_row = jnp.zeros((1, 128), F32).at[0, HEADS_A:2 * HEADS_A].set(a_A_log[0])
    dt_row = jnp.zeros((1, 128), F32).at[0, HEADS_A:2 * HEADS_A].set(a_dt_bias[0])
    sink_row = jnp.zeros((1, 128), F32).at[0, :b_sinks.shape[1]].set(b_sinks[0])
    a_in_cols = a_w_in.shape[-1] * N_DEV

    def down_blocks(t):
        return t.reshape(N_FB, FB, D)

    wgu, wdn, saved = {}, {}, []
    xn = rmsnorm_bf16(x0, ffn1_norm[0:1], "l0_ffn1_norm", (t0,))
    wgu["ffn1", 0] = push_wait(h0, xn, "gather0_wait", scatter=False)[0]
    h0d, t0d = gather_start([cast(ffn1_w_down[0])], "gather0d_start", deps=(wgu["ffn1", 0],))
    h1, t1 = gather_start([cast(a_w_in[0]), a_w_conv[0], cast(a_w_out[0])], "gather1_start", deps=(t0d,))
    gu = ffn_up(xn, wgu["ffn1", 0], "l0_ffn1_up", deps=(t0d, t1))
    wdn["ffn1", 0] = down_blocks(push_wait(h0d, gu, "gather0d_wait", scatter=False)[0])
    xs, s1 = ffn_down(gu, wdn["ffn1", 0], x0, "l0_ffn1_down"), (x0, xn, gu)
    got = push_wait(h1, xs, "gather1_wait", scatter=False)
    h1f, t1f = gather_start([cast(ffn2_w_gu[0]), cast(ffn2_w_down[0])], "gather1f_start", deps=(got[0],))
    g2 = [cast(ffn1_w_gu[1]), cast(ffn1_w_down[1]), cast(b_w_in[0]), b_b_in, cast(b_w_out[0]), b_b_out,
          cast(ffn2_w_gu[1]), cast(ffn2_w_down[1])]
    h2, t2 = gather_start(g2, "gather2_start", deps=(t1f,))
    a_in_full = jnp.pad(got[0].transpose(1, 0, 2).reshape(D, a_in_cols), ((0, 0), (0, A_COLS - a_in_cols)))
    gdn_args = (mix_norm[0:1], a_in_full, got[1].transpose(1, 0, 2).reshape(4, 3 * D), a_log_row, dt_row, a_out_norm,
                got[2].reshape(D, D))
    xs, sm = gdn_forward(xs, *gdn_args, "gdn", deps=(t1f, t2))
    got = push_wait(h1f, xs, "gather1f_wait", scatter=False)
    wgu["ffn2", 0], wdn["ffn2", 0] = got[0], down_blocks(got[1])
    xs, s2 = ffn_forward(xs, ffn2_norm[0:1], wgu["ffn2", 0], wdn["ffn2", 0], "l0_ffn2")
    saved.append((s1, sm, s2))
    got = push_wait(h2, xs, "gather2_wait", scatter=False)
    wgu["ffn1", 1], wdn["ffn1", 1] = got[0], down_blocks(got[1])
    swa_args = (mix_norm[1:2], got[2].transpose(1, 0, 2).reshape(D, B_COLS), got[3].reshape(1, B_COLS), sink_row,
                got[4].reshape(D, D), got[5].reshape(1, D))
    wgu["ffn2", 1], wdn["ffn2", 1] = got[6], down_blocks(got[7])
    xs, s1 = ffn_forward(xs, ffn1_norm[1:2], wgu["ffn1", 1], wdn["ffn1", 1], "l1_ffn1")
    xs, sm = swa_forward(xs, *swa_args, "swa")
    xs, s2 = ffn_forward(xs, ffn2_norm[1:2], wgu["ffn2", 1], wdn["ffn2", 1], "l1_ffn2")
    saved.append((s1, sm, s2))
    loss_row, dx, d_final_norm = final_loss(xs, final_norm.reshape(1, D), tgt, "final_loss")

    def down_slots(t):
        return cast(t.reshape(N_DEV, FB // 2, D))

    def col_slots(t, dtype=BF16):
        return t.reshape(t.shape[0], N_DEV, -1).transpose(1, 0, 2).astype(dtype)

    d_norm = {"ffn1_norm": [None, None], "mix_norm": [None, None], "ffn2_norm": [None, None]}
    s1, sm, s2 = saved[1]
    dx, d_norm["ffn2_norm"][1], d_gu, d_dn = ffn_backward(dx, s2, ffn2_norm[1:2], wgu["ffn2", 1], wdn["ffn2", 1], "l1_ffn2")
    sent1 = [cast(d_gu), down_slots(d_dn)]
    dx, d_norm["mix_norm"][1], d_b_in, d_b_bias_in, d_sinks, d_b_out, d_b_bias_out = swa_backward(dx, sm, *swa_args, "swa")
    sent1 += [col_slots(d_b_in), d_b_bias_in.reshape(N_DEV, 1, -1), cast(d_b_out.reshape(N_DEV, D // N_DEV, D)),
              d_b_bias_out.reshape(N_DEV, 1, -1)]
    dx, d_norm["ffn1_norm"][1], d_gu, d_dn = ffn_backward(dx, s1, ffn1_norm[1:2], wgu["ffn1", 1], wdn["ffn1", 1], "l1_ffn1")
    sent1 += [cast(d_gu), down_slots(d_dn)]
    x1, tx1 = exchange_start(sent1, "exchange1_start")

    s1, sm, s2 = saved[0]
    dx, d_norm["ffn2_norm"][0], d_gu, d_dn = ffn_backward(dx, s2, ffn2_norm[0:1], wgu["ffn2", 0], wdn["ffn2", 0], "l0_ffn2",
                                                           deps=(tx1,))
    sent2 = [cast(d_gu), down_slots(d_dn)]
    dx, d_norm["mix_norm"][0], d_a_in, d_a_conv, d_alog, d_dt, d_onorm, d_a_out = gdn_backward(dx, sm, *gdn_args, "gdn")
    sent2 += [col_slots(d_a_in[:, :a_in_cols]), col_slots(d_a_conv, F32), cast(d_a_out.reshape(N_DEV, D // N_DEV, D))]
    x2, tx2 = exchange_start(sent2, "exchange2_start")
    last = {}

    def send_last(d_gu, d_dn):
        last["handles"], token = exchange_start([cast(d_gu), down_slots(d_dn)], "exchange3_start")
        return (token,)

    dx, d_norm["ffn1_norm"][0], _, _ = ffn_backward(dx, s1, ffn1_norm[0:1], wgu["ffn1", 0], wdn["ffn1", 0], "l0_ffn1",
                                                    deps=(tx2,), on_grads=send_last)
    grad_x = dx.reshape(x.shape)
    r1 = push_wait(x1, dx, "exchange1_wait", scatter=True)
    r2 = push_wait(x2, dx, "exchange2_wait", scatter=True)
    received = dict(ffn2_w_gu=[r2[0], r1[0]], ffn2_w_down=[r2[1], r1[1]],
                    b_w_in=[r1[2]], b_b_in=[r1[3]], b_w_out=[r1[4]], b_b_out=[r1[5]],
                    a_w_in=[r2[2]], a_w_conv=[r2[3]], a_w_out=[r2[4]])

    grads, deltas, new_m, new_v = {}, {}, {}, {}

    def update(key):
        shape = w[key].shape
        cols = shape[-1]
        layers = lambda t: t.reshape(shape[0], -1, cols)
        out = adam_update([r.reshape(N_DEV, -1, cols) for r in received[key]], layers(w[key]), layers(m[key]), layers(v[key]),
                          f"adam_{key}")
        grads[key], deltas[key], new_m[key], new_v[key] = (t.reshape(shape) for t in out)

    for key in SHARDED:
        if key in received:
            update(key)
    done_first = [deltas[key] for key in received]

    small = dict(ffn1_norm=jnp.concatenate(d_norm["ffn1_norm"], axis=0), mix_norm=jnp.concatenate(d_norm["mix_norm"], axis=0),
                 ffn2_norm=jnp.concatenate(d_norm["ffn2_norm"], axis=0), final_norm=d_final_norm,
                 a_A_log=d_alog[0, HEADS_A:2 * HEADS_A], a_dt_bias=d_dt[0, HEADS_A:2 * HEADS_A],
                 b_sinks=d_sinks[0, :b_sinks.shape[1]], a_out_norm=d_onorm, loss=loss_row[0, 0])
    hs, ts = gather_start([_pack_small(small)], "gather_small_start")
    r3 = push_wait(last["handles"], done_first + [ts], "exchange3_wait", scatter=True)
    received.update(ffn1_w_gu=[r3[0], r1[6]], ffn1_w_down=[r3[1], r1[7]])
    update("ffn1_w_gu")
    update("ffn1_w_down")
    every = push_wait(hs, deltas["ffn1_w_down"], "gather_small_wait", scatter=False)[0]
    out = adam_update([every], _pack_small(w)[None], _pack_small(m)[None], _pack_small(v)[None], "adam_small")
    for dst, packed in zip((grads, deltas, new_m, new_v), out):
        dst.update(_unpack_small(packed[0], w))
    loss = out[0][0, 7, LOSS_LANE]

    return (loss, grad_x, *[grads[k_] for k_ in WEIGHTS], *[deltas[k_] for k_ in WEIGHTS],
            *[new_m[k_] for k_ in WEIGHTS], *[new_v[k_] for k_ in WEIGHTS])
```

```python
import functools

import jax
import jax.numpy as jnp
from jax import lax
from jax.experimental import pallas as pl
from jax.experimental.pallas import tpu as pltpu

F32, BF16 = jnp.float32, jnp.bfloat16
HI = lax.Precision.HIGHEST
EPS = 1e-6

N_DEV = 8
D = 1024
FB = 704
N_FB = 4
HEADS_A, DK = 8, 128
CHUNK = 64
PREP_T = 512
A_COLS = 4224
B_HD, B_BLK = 64, 128
VMEM_LIMIT_V7X = 60 * 1024 * 1024

ADAM_LR, ADAM_B1, ADAM_B2, ADAM_EPS, ADAM_WD, ADAM_STEP = 0.001, 0.9, 0.999, 1e-08, 0.01, 10

NT = (((1,), (1,)), ((), ()))
TN = (((0,), (0,)), ((), ()))


def _pc(body, *, name, out_shape, grid=(), in_specs=None, out_specs=None, scratch=(), sem=None, **kw):
    params = pltpu.CompilerParams(dimension_semantics=sem, vmem_limit_bytes=VMEM_LIMIT_V7X)
    return pl.pallas_call(body, name=name, out_shape=out_shape, grid=grid, in_specs=in_specs, out_specs=out_specs,
                          scratch_shapes=list(scratch), compiler_params=params, **kw)


def _sds(shape, dtype):
    return jax.ShapeDtypeStruct(tuple(shape), dtype)


def _dot(a, b, dims=None, precision=None):
    if dims is None:
        return jnp.dot(a, b, preferred_element_type=F32, precision=precision)
    return lax.dot_general(a, b, dims, preferred_element_type=F32, precision=precision)


def _sigmoid(x):
    return 1.0 / (1.0 + jnp.exp(-x))


def _softplus(x):
    return jnp.maximum(x, 0.0) + jnp.log(1.0 + jnp.exp(-jnp.abs(x)))


def _rms_fwd(x, w):
    r = lax.rsqrt(jnp.mean(x * x, axis=-1, keepdims=True) + EPS)
    return x * r * w


def _rms_bwd(x, w, dy):
    r = lax.rsqrt(jnp.mean(x * x, axis=-1, keepdims=True) + EPS)
    xh = x * r
    dxh = dy * w
    dx = r * (dxh - xh * jnp.mean(dxh * xh, axis=-1, keepdims=True))
    return dx, jnp.sum(dy * xh, axis=0, keepdims=True)


def _tile(n, want):
    t = min(n, want)
    assert n % t == 0, (n, want)
    return t


def rmsnorm_bf16(x, w, name, deps=()):
    T = x.shape[0]
    tm = _tile(T, 1024)

    def body(x_ref, w_ref, *rest):
        rest[-1][...] = _rms_fwd(x_ref[...], w_ref[...]).astype(BF16)

    return _pc(body, name=name, out_shape=_sds((T, D), BF16), grid=(T // tm,),
               in_specs=[pl.BlockSpec((tm, D), lambda i: (i, 0)), pl.BlockSpec((1, D), lambda i: (0, 0))] + [DEP_SPEC] * len(deps),
               out_specs=pl.BlockSpec((tm, D), lambda i: (i, 0)), sem=("parallel",))(x, w, *deps)


def rmsnorm_bwd_add(x, w, dxn, dres, name):
    T = x.shape[0]
    tm = _tile(T, 512)

    def body(x_ref, w_ref, dxn_ref, dres_ref, dx_ref, dw_ref):
        dx, dw = _rms_bwd(x_ref[...], w_ref[...], dxn_ref[...])
        dx_ref[...] = dres_ref[...] + dx

        @pl.when(pl.program_id(0) == 0)
        def _():
            dw_ref[...] = jnp.zeros_like(dw_ref)
        dw_ref[...] += dw

    row = pl.BlockSpec((tm, D), lambda i: (i, 0))
    vec = pl.BlockSpec((1, D), lambda i: (0, 0))
    return _pc(body, name=name, out_shape=(_sds((T, D), F32), _sds((1, D), F32)), grid=(T // tm,),
               in_specs=[row, vec, row, row], out_specs=(row, vec), sem=("arbitrary",))(x, w, dxn, dres)


def final_loss(x, w, tgt, name):
    T = x.shape[0]
    tm = _tile(T, 512)

    def body(x_ref, w_ref, t_ref, loss_ref, dx_ref, dw_ref):
        xv, wv = x_ref[...], w_ref[...]
        err = _rms_fwd(xv, wv) - t_ref[...]
        dx, dw = _rms_bwd(xv, wv, err * (1.0 / D))
        dx_ref[...] = dx

        @pl.when(pl.program_id(0) == 0)
        def _():
            dw_ref[...] = jnp.zeros_like(dw_ref)
            loss_ref[...] = jnp.zeros_like(loss_ref)
        dw_ref[...] += dw
        loss_ref[...] += jnp.full((1, 128), 0.5 / D, F32) * jnp.sum(err * err)

    row = pl.BlockSpec((tm, D), lambda i: (i, 0))
    vec = pl.BlockSpec((1, D), lambda i: (0, 0))
    return _pc(body, name=name, out_shape=(_sds((1, 128), F32), _sds((T, D), F32), _sds((1, D), F32)),
               grid=(T // tm,), in_specs=[row, vec, row],
               out_specs=(pl.BlockSpec((1, 128), lambda i: (0, 0)), row, vec), sem=("arbitrary",))(x, w, tgt)


def _col_tile(n):
    for t in (1536, 1408, 1024, 768, 512, 384, 256, 128):
        if n % t == 0:
            return t
    return n


def mm_nn(a, b, name, bias=None, residual=None, out_dtype=F32, cols=None):
    T, K = a.shape
    first, end = cols or (0, b.shape[1])
    N = end - first
    tm, tn = _tile(T, 512), _col_tile(N)
    assert first % tn == 0 and (cols is None or (bias is None and residual is None))
    j0 = first // tn

    def body(a_ref, b_ref, *rest):
        o_ref = rest[-1]
        acc = _dot(a_ref[...].astype(BF16), b_ref[...])
        for extra in rest[:-1]:
            acc = acc + extra[...]
        o_ref[...] = acc.astype(out_dtype)

    in_specs = [pl.BlockSpec((tm, K), lambda j, i: (i, 0)), pl.BlockSpec((K, tn), lambda j, i: (0, j0 + j))]
    args = [a, b]
    if bias is not None:
        in_specs.append(pl.BlockSpec((1, tn), lambda j, i: (0, j)))
        args.append(bias)
    if residual is not None:
        in_specs.append(pl.BlockSpec((tm, tn), lambda j, i: (i, j)))
        args.append(residual)
    return _pc(body, name=name, out_shape=_sds((T, N), out_dtype), grid=(N // tn, T // tm), in_specs=in_specs,
               out_specs=pl.BlockSpec((tm, tn), lambda j, i: (i, j)), sem=("parallel", "parallel"))(*args)


def mm_nt(a, b, name, out_dtype=F32):
    T, N = a.shape
    K = b.shape[0]
    tm = _tile(T, 512)

    def body(a_ref, b_ref, o_ref):
        o_ref[...] = _dot(a_ref[...].astype(BF16), b_ref[...], NT).astype(out_dtype)

    return _pc(body, name=name, out_shape=_sds((T, K), out_dtype), grid=(T // tm,),
               in_specs=[pl.BlockSpec((tm, N), lambda i: (i, 0)), pl.BlockSpec((K, N), lambda i: (0, 0))],
               out_specs=pl.BlockSpec((tm, K), lambda i: (i, 0)), sem=("parallel",))(a, b)


def mm_tn(a, b, name):
    T, K = a.shape
    N = b.shape[1]
    tt, tn = _tile(T, 1024), _col_tile(N)

    def body(a_ref, b_ref, o_ref):
        @pl.when(pl.program_id(1) == 0)
        def _():
            o_ref[...] = jnp.zeros_like(o_ref)
        o_ref[...] += _dot(a_ref[...].astype(BF16), b_ref[...].astype(BF16), TN)

    return _pc(body, name=name, out_shape=_sds((K, N), F32), grid=(N // tn, T // tt),
               in_specs=[pl.BlockSpec((tt, K), lambda j, t: (t, 0)), pl.BlockSpec((tt, tn), lambda j, t: (t, j))],
               out_specs=pl.BlockSpec((K, tn), lambda j, t: (0, j)), sem=("parallel", "arbitrary"))(a, b)


def ffn_up(xn, wgu, name, deps=()):
    T = xn.shape[0]
    tm = _tile(T, 1024)

    def body(x_ref, w_ref, *rest):
        xv = x_ref[...]
        for j in range(2 * N_FB):
            rest[-1][j] = _dot(xv, w_ref[j]).astype(BF16)

    return _pc(body, name=name, out_shape=_sds((2 * N_FB, T, FB), BF16), grid=(T // tm,),
               in_specs=[pl.BlockSpec((tm, D), lambda i: (i, 0)), _resident((2 * N_FB, D, FB))] + [DEP_SPEC] * len(deps),
               out_specs=pl.BlockSpec((2 * N_FB, tm, FB), lambda i: (0, i, 0)), sem=("parallel",))(xn, wgu, *deps)


def ffn_down(gu, wd, x, name):
    T = x.shape[0]
    tm = _tile(T, 512)

    def body(gu_ref, w_ref, x_ref, o_ref):
        acc = jnp.zeros((tm, D), F32)
        for g in range(N_FB):
            gate, up = gu_ref[g], gu_ref[N_FB + g]
            acc = acc + _dot(gate * _sigmoid(gate) * up, w_ref[g])
        o_ref[...] = x_ref[...] + 0.5 * acc

    row = pl.BlockSpec((tm, D), lambda i: (i, 0))
    return _pc(body, name=name, out_shape=_sds((T, D), F32), grid=(T // tm,),
               in_specs=[pl.BlockSpec((2 * N_FB, tm, FB), lambda i: (0, i, 0)),
                         _resident((N_FB, FB, D)), row],
               out_specs=row, sem=("parallel",))(gu, wd, x)


def _resident(shape):
    return pl.BlockSpec(shape, lambda *_: (0,) * len(shape), pipeline_mode=pl.Buffered(1))


def _store_blocks_bf16(acc, out_hbm, stage, sem):
    for j in range(acc.shape[0]):
        stage[...] = acc[j].astype(BF16)
        copy = pltpu.make_async_copy(stage, out_hbm.at[j], sem)
        copy.start()
        copy.wait()


def ffn_bwd_hidden(dout, wd, gu, name, deps=()):
    T = dout.shape[0]
    tm = _tile(T, 512)
    n_t = T // tm

    def body(d_ref, w_ref, gu_ref, *rest):
        dgu_ref, dwd_hbm, acc, stage, sem = rest[-5:]
        t = pl.program_id(0)

        @pl.when(t == 0)
        def _():
            acc[...] = jnp.zeros_like(acc)
        dy = (0.5 * d_ref[...]).astype(BF16)
        for g in range(N_FB):
            gate, up = gu_ref[g], gu_ref[N_FB + g]
            sg = _sigmoid(gate)
            silu = gate * sg
            dact = _dot(dy, w_ref[g], NT).astype(BF16)
            acc[g] += _dot(silu * up, dy, TN)
            dgu_ref[g] = dact * up * (sg * (1.0 + gate * (1.0 - sg)))
            dgu_ref[N_FB + g] = dact * silu

        @pl.when(t == n_t - 1)
        def _():
            _store_blocks_bf16(acc, dwd_hbm, stage, sem)

    return _pc(body, name=name, out_shape=(_sds((2 * N_FB, T, FB), BF16), _sds((N_FB, FB, D), BF16)), grid=(n_t,),
               in_specs=[pl.BlockSpec((tm, D), lambda i: (i, 0)), _resident((N_FB, FB, D)),
                         pl.BlockSpec((2 * N_FB, tm, FB), lambda i: (0, i, 0))] + [DEP_SPEC] * len(deps),
               out_specs=(pl.BlockSpec((2 * N_FB, tm, FB), lambda i: (0, i, 0)), pl.BlockSpec(memory_space=pl.ANY)),
               scratch=[pltpu.VMEM((N_FB, FB, D), F32), pltpu.VMEM((FB, D), BF16), pltpu.SemaphoreType.DMA],
               sem=("arbitrary",))(dout, wd, gu, *deps)


def ffn_bwd_input(dgu, wgu, x, dout, nw, name, deps=()):
    T = x.shape[0]
    tm = _tile(T, 512)

    def body(dgu_ref, w_ref, x_ref, d_ref, nw_ref, *rest):
        dx_ref, dnw_ref = rest[-2:]
        dxn = jnp.zeros((tm, D), F32)
        for j in range(2 * N_FB):
            dxn = dxn + _dot(dgu_ref[j], w_ref[j], NT)
        dx, dw = _rms_bwd(x_ref[...], nw_ref[...], dxn)
        dx_ref[...] = d_ref[...] + dx

        @pl.when(pl.program_id(0) == 0)
        def _():
            dnw_ref[...] = jnp.zeros_like(dnw_ref)
        dnw_ref[...] += dw

    row = pl.BlockSpec((tm, D), lambda i: (i, 0))
    vec = pl.BlockSpec((1, D), lambda i: (0, 0))
    return _pc(body, name=name, out_shape=(_sds((T, D), F32), _sds((1, D), F32)), grid=(T // tm,),
               in_specs=[pl.BlockSpec((2 * N_FB, tm, FB), lambda i: (0, i, 0)), _resident((2 * N_FB, D, FB)),
                         row, row, vec] + [DEP_SPEC] * len(deps),
               out_specs=(row, vec), sem=("arbitrary",))(dgu, wgu, x, dout, nw, *deps)


def ffn_wgrad_gu(xn, dgu, name):
    T = xn.shape[0]
    tt = _tile(T, 1024)
    n_t = T // tt

    def body(x_ref, d_ref, dw_hbm, acc, stage, sem):
        t = pl.program_id(0)

        @pl.when(t == 0)
        def _():
            acc[...] = jnp.zeros_like(acc)
        xn_tile = x_ref[...]
        for j in range(2 * N_FB):
            acc[j] += _dot(xn_tile, d_ref[j], TN)

        @pl.when(t == n_t - 1)
        def _():
            _store_blocks_bf16(acc, dw_hbm, stage, sem)

    return _pc(body, name=name, out_shape=_sds((2 * N_FB, D, FB), BF16), grid=(n_t,),
               in_specs=[pl.BlockSpec((tt, D), lambda t: (t, 0)), pl.BlockSpec((2 * N_FB, tt, FB), lambda t: (0, t, 0))],
               out_specs=pl.BlockSpec(memory_space=pl.ANY),
               scratch=[pltpu.VMEM((2 * N_FB, D, FB), F32), pltpu.VMEM((D, FB), BF16), pltpu.SemaphoreType.DMA],
               sem=("arbitrary",))(xn, dgu)


def ffn_forward(x, nw, wgu, wd, tag):
    T = x.shape[0]
    tm = _tile(T, 512)

    def body(x_ref, nw_ref, wgu_ref, wd_ref, o_ref, xn_ref, gu_ref):
        xv = x_ref[...]
        xn = _rms_fwd(xv, nw_ref[...]).astype(BF16)
        xn_ref[...] = xn
        for j in range(2 * N_FB):
            gu_ref[j] = _dot(xn, wgu_ref[j]).astype(BF16)
        acc = jnp.zeros((tm, D), F32)
        for g in range(N_FB):
            gate, up = gu_ref[g], gu_ref[N_FB + g]
            acc = acc + _dot(gate * _sigmoid(gate) * up, wd_ref[g])
        o_ref[...] = xv + 0.5 * acc

    row = pl.BlockSpec((tm, D), lambda i: (i, 0))
    out, xn, gu = _pc(body, name=f"{tag}_fwd",
                      out_shape=(_sds((T, D), F32), _sds((T, D), BF16), _sds((2 * N_FB, T, FB), BF16)), grid=(T // tm,),
                      in_specs=[row, pl.BlockSpec((1, D), lambda i: (0, 0)), _resident((2 * N_FB, D, FB)),
                                _resident((N_FB, FB, D))],
                      out_specs=(row, row, pl.BlockSpec((2 * N_FB, tm, FB), lambda i: (0, i, 0))),
                      sem=("parallel",))(x, nw, wgu, wd)
    return out, (x, xn, gu)


def ffn_backward(dout, saved, nw, wgu, wd, tag, deps=(), on_grads=None):
    x, xn, gu = saved
    dgu, dwd = ffn_bwd_hidden(dout, wd, gu, f"{tag}_bwd_hidden", deps)
    dwgu = ffn_wgrad_gu(xn, dgu, f"{tag}_wgrad_gu")
    late = on_grads(dwgu, dwd) if on_grads else ()
    dx, dnw = ffn_bwd_input(dgu, wgu, x, dout, nw, f"{tag}_bwd_input", late)
    return dx, dnw, dwgu, dwd


N_QKV_BLK = 3 * HEADS_A
Z_BLK0 = N_QKV_BLK
MAIN_COLS = 4 * D
HALO = 16


def _conv_taps(xcat, w):
    c = xcat[HALO:] * w[3:4]
    for k in range(3):
        c = c + pltpu.roll(xcat, 3 - k, 0)[HALO:] * w[k:k + 1]
    return c


def _head_cols(h):
    return slice(128 * h, 128 * (h + 1))


def gdn_conv_fwd(proj, wconv, name):
    T = proj.shape[0]
    tm = _tile(T, 512)

    def body(cur_ref, prev_ref, w_ref, c_ref, y_ref):
        kind, t = pl.program_id(0), pl.program_id(1)
        prev = jnp.where(t > 0, prev_ref[...].astype(F32), 0.0)
        c = _conv_taps(jnp.concatenate([prev, cur_ref[...].astype(F32)], axis=0), w_ref[...])
        c_ref[...] = c.astype(BF16)
        s = c * _sigmoid(c)
        scale = jnp.where(kind == 0, DK ** -0.5, 1.0)
        for h in range(HEADS_A):
            sh = s[:, _head_cols(h)]
            r = lax.rsqrt(jnp.sum(sh * sh, axis=-1, keepdims=True) + EPS)
            y_ref[h] = (sh * jnp.where(kind < 2, r * scale, 1.0)).astype(BF16)

    return _pc(body, name=name, out_shape=(_sds((T, 3 * D), BF16), _sds((N_QKV_BLK, T, 128), BF16)),
               grid=(3, T // tm),
               in_specs=[pl.BlockSpec((tm, D), lambda kd, t: (t, kd)),
                         pl.BlockSpec((HALO, D), lambda kd, t: (jnp.maximum(t * (tm // HALO) - 1, 0), kd)),
                         pl.BlockSpec((4, D), lambda kd, t: (0, kd))],
               out_specs=(pl.BlockSpec((tm, D), lambda kd, t: (t, kd)),
                          pl.BlockSpec((HEADS_A, tm, 128), lambda kd, t: (kd, t, 0))),
               sem=("parallel", "parallel"))(proj, proj, wconv)


def gdn_conv_bwd(dqkv, c, proj, wconv, name):
    T = c.shape[0]
    tm = _tile(T, 512)
    n_t = T // tm

    def body(dy_ref, dyn_ref, c_ref, cn_ref, x_ref, xp_ref, w_ref, dx_ref, dw_ref):
        kind, t = pl.program_id(0), pl.program_id(1)
        scale = jnp.where(kind == 0, DK ** -0.5, 1.0)

        def act_bwd(dy, cv):
            sg = _sigmoid(cv)
            s = cv * sg
            parts = []
            for h in range(HEADS_A):
                sh, dyh = s[:, _head_cols(h)], dy[h]
                r = lax.rsqrt(jnp.sum(sh * sh, axis=-1, keepdims=True) + EPS)
                ds_norm = scale * r * (dyh - (r * r) * sh * jnp.sum(dyh * sh, axis=-1, keepdims=True))
                parts.append(jnp.where(kind < 2, ds_norm, dyh))
            return jnp.concatenate(parts, axis=1) * (sg * (1.0 + cv * (1.0 - sg)))

        w = w_ref[...]
        dcur = act_bwd(dy_ref[...].astype(F32), c_ref[...].astype(F32))
        dnext = jnp.where(t < n_t - 1, act_bwd(dyn_ref[...].astype(F32), cn_ref[...].astype(F32)), 0.0)
        dcat = jnp.concatenate([dcur, dnext], axis=0)
        dx = dcur * w[3:4]
        for k in range(3):
            dx = dx + pltpu.roll(dcat, tm + HALO - (3 - k), 0)[:tm] * w[k:k + 1]
        dx_ref[...] = dx.astype(BF16)
        xprev = jnp.where(t > 0, xp_ref[...].astype(F32), 0.0)
        xcat = jnp.concatenate([xprev, x_ref[...].astype(F32)], axis=0)
        rows = [jnp.sum(dcur * pltpu.roll(xcat, 3 - k, 0)[HALO:], axis=0, keepdims=True) for k in range(3)]
        rows.append(jnp.sum(dcur * xcat[HALO:], axis=0, keepdims=True))

        @pl.when(t == 0)
        def _():
            dw_ref[...] = jnp.zeros_like(dw_ref)
        dw_ref[...] += jnp.concatenate(rows, axis=0)

    def nxt(t):
        return jnp.minimum((t + 1) * (tm // HALO), T // HALO - 1)

    cur = pl.BlockSpec((tm, D), lambda kd, t: (t, kd))
    return _pc(body, name=name, out_shape=(_sds((T, 3 * D), BF16), _sds((4, 3 * D), F32)), grid=(3, n_t),
               in_specs=[pl.BlockSpec((HEADS_A, tm, 128), lambda kd, t: (kd, t, 0)),
                         pl.BlockSpec((HEADS_A, HALO, 128), lambda kd, t: (kd, nxt(t), 0)),
                         cur, pl.BlockSpec((HALO, D), lambda kd, t: (nxt(t), kd)),
                         cur, pl.BlockSpec((HALO, D), lambda kd, t: (jnp.maximum(t * (tm // HALO) - 1, 0), kd)),
                         pl.BlockSpec((4, D), lambda kd, t: (0, kd))],
               out_specs=(cur, pl.BlockSpec((4, D), lambda kd, t: (0, kd))),
               sem=("parallel", "arbitrary"))(dqkv, dqkv, c, c, proj, proj, wconv)


def _chunk_masks(n):
    ri = lax.broadcasted_iota(jnp.int32, (n, n), 0)
    ci = lax.broadcasted_iota(jnp.int32, (n, n), 1)
    same = (ri // CHUNK) == (ci // CHUNK)
    return same & (ri >= ci), same & (ri <= ci)


def gdn_gate_fwd(ba, al, dtb, name):
    T = ba.shape[0]
    tg = _tile(T, PREP_T)

    def body(ba_ref, al_ref, dtb_ref, o_ref):
        x = ba_ref[...]
        lane = lax.broadcasted_iota(jnp.int32, x.shape, 1)
        is_a = (lane >= HEADS_A) & (lane < 2 * HEADS_A)
        g = jnp.where(is_a, -jnp.exp(al_ref[...]) * _softplus(x + dtb_ref[...]), 0.0)
        lower, _ = _chunk_masks(tg)
        gc = _dot(lower.astype(F32), g, precision=HI)
        o_ref[...] = jnp.where(lane < HEADS_A, _sigmoid(x), gc)

    vec = pl.BlockSpec((1, 128), lambda i: (0, 0))
    return _pc(body, name=name, out_shape=_sds((T, 128), F32), grid=(T // tg,),
               in_specs=[pl.BlockSpec((tg, 128), lambda i: (i, 0)), vec, vec],
               out_specs=pl.BlockSpec((tg, 128), lambda i: (i, 0)), sem=("parallel",))(ba, al, dtb)


def gdn_gate_bwd(ba, al, dtb, dgb, name):
    T = ba.shape[0]
    tg = _tile(T, PREP_T)

    def body(ba_ref, al_ref, dtb_ref, dgb_ref, dba_ref, dal_ref, ddt_ref):
        x, d = ba_ref[...], dgb_ref[...]
        lane = lax.broadcasted_iota(jnp.int32, x.shape, 1)
        is_b = lane < HEADS_A
        is_a = (lane >= HEADS_A) & (lane < 2 * HEADS_A)
        beta = _sigmoid(x)
        e_a = jnp.exp(al_ref[...])
        z = x + dtb_ref[...]
        g = jnp.where(is_a, -e_a * _softplus(z), 0.0)
        _, upper = _chunk_masks(tg)
        dg = _dot(upper.astype(F32), jnp.where(is_a, d, 0.0), precision=HI)
        da = jnp.where(is_a, dg * (-e_a) * _sigmoid(z), 0.0)
        db = jnp.where(is_b, d * beta * (1.0 - beta), 0.0)
        dba_ref[...] = (da + db).astype(BF16)

        @pl.when(pl.program_id(0) == 0)
        def _():
            dal_ref[...] = jnp.zeros_like(dal_ref)
            ddt_ref[...] = jnp.zeros_like(ddt_ref)
        dal_ref[...] += jnp.sum(dg * g, axis=0, keepdims=True)
        ddt_ref[...] += jnp.sum(da, axis=0, keepdims=True)

    vec = pl.BlockSpec((1, 128), lambda i: (0, 0))
    blk = pl.BlockSpec((tg, 128), lambda i: (i, 0))
    return _pc(body, name=name, out_shape=(_sds((T, 128), BF16), _sds((1, 128), F32), _sds((1, 128), F32)),
               grid=(T // tg,), in_specs=[blk, vec, vec, blk],
               out_specs=(blk, vec, vec), sem=("arbitrary",))(ba, al, dtb, dgb)


def _bmm(a, b, dims, precision=None):
    return lax.dot_general(a, b, dims, preferred_element_type=F32, precision=precision)


B_NN = (((2,), (1,)), ((0,), (0,)))
B_NT = (((2,), (2,)), ((0,), (0,)))


def _select_lane(x, lane_index):
    lane = lax.broadcasted_iota(jnp.int32, x.shape, x.ndim - 1)
    return jnp.sum(jnp.where(lane == lane_index, x, 0.0), axis=-1, keepdims=True)


B_TN = (((1,), (1,)), ((0,), (0,)))


def _bmm_split(a, b, dims):
    ah, bh = a.astype(BF16), b.astype(BF16)
    al, bl = (a - ah.astype(F32)).astype(BF16), (b - bh.astype(F32)).astype(BF16)
    return _bmm(ah, bh, dims) + (_bmm(ah, bl, dims) + _bmm(al, bh, dims))


@jax.custom_vjp
def _bmm_f32(a, b):
    return _bmm_split(a, b, B_NN)


def _bmm_f32_fwd(a, b):
    return _bmm_split(a, b, B_NN), (a, b)


def _bmm_bf16(a, b, dims):
    return _bmm(a.astype(BF16), b.astype(BF16), dims)


def _bmm_f32_bwd(res, dc):
    a, b = res
    return _bmm_bf16(dc, b, B_NT), _bmm_bf16(a, dc, B_TN)


_bmm_f32.defvjp(_bmm_f32_fwd, _bmm_f32_bwd)


def _tri_inverse(lmat):
    ri = lax.broadcasted_iota(jnp.int32, lmat.shape, 1)
    ci = lax.broadcasted_iota(jnp.int32, lmat.shape, 2)
    eye = jnp.where(ri == ci, 1.0, 0.0)
    inv = eye - lmat
    power = lmat
    for _ in range(5):
        power = _bmm_bf16(power, power, B_NN)
        inv = inv + _bmm_bf16(inv, power, B_NN)
    return _bmm_split(inv, 2.0 * eye - _bmm_split(eye + lmat, inv, B_NN), B_NN)


def _stored_inverse(x):
    @jax.custom_vjp
    def inverse(lmat):
        return x

    def fwd(lmat):
        return x, None

    def bwd(_, dx):
        return (-_bmm_bf16(_bmm_bf16(x, dx, B_TN), x, B_NT),)

    inverse.defvjp(fwd, bwd)
    return inverse


def _gdn_prep(q, k, v, gb, h, inverse):
    nb = q.shape[0]
    beta = _select_lane(gb, h)
    gc = _select_lane(gb, HEADS_A + h)
    ri = lax.broadcasted_iota(jnp.int32, (nb, CHUNK, CHUNK), 1)
    ci = lax.broadcasted_iota(jnp.int32, (nb, CHUNK, CHUNK), 2)
    lower, strict, eye = ri >= ci, ri > ci, ri == ci
    gcol = jnp.broadcast_to(gc, (nb, CHUNK, CHUNK))
    grow = jnp.swapaxes(gcol, 1, 2)
    decay = jnp.where(lower, jnp.exp(jnp.where(lower, gcol - grow, 0.0)), 0.0)
    kb = k * beta
    kbf = k.astype(BF16)
    inv = inverse(jnp.where(strict, _bmm(kb.astype(BF16), kbf, B_NT) * decay, 0.0))
    eg = jnp.exp(gc)
    sol = _bmm_f32(inv, jnp.concatenate([v * beta, kb * eg], axis=-1))
    aqk = _bmm(q.astype(BF16), kbf, B_NT) * decay
    g_last = gc[:, CHUNK - 1:CHUNK, :]
    gl = jnp.broadcast_to(jnp.exp(g_last), (nb, 1, 128))
    return (sol[..., :DK], sol[..., DK:], q * eg, k * jnp.exp(g_last - gc), aqk, gl), inv


def gdn_prep_fwd(qkv, gb, name):
    T = qkv.shape[1]
    tp = _tile(T, PREP_T)
    nb = tp // CHUNK

    def body(q_ref, k_ref, v_ref, gb_ref, u_ref, w_ref, qd_ref, kd_ref, a_ref, gl_ref, inv_ref):
        h = pl.program_id(1)
        shp = (nb, CHUNK, 128)
        q, k, v = (ref[0].astype(F32).reshape(shp) for ref in (q_ref, k_ref, v_ref))
        (u, w, qd, kd, aqk, gl), inv = _gdn_prep(q, k, v, gb_ref[...].reshape(shp), h, _tri_inverse)
        u_ref[0] = u.reshape(tp, 128)
        w_ref[0] = w.reshape(tp, 128).astype(BF16)
        qd_ref[0] = qd.reshape(tp, 128).astype(BF16)
        kd_ref[0] = kd.reshape(tp, 128).astype(BF16)
        a_ref[0] = aqk.reshape(tp, CHUNK).astype(BF16)
        gl_ref[0] = gl.reshape(nb, 1, 128)
        inv_ref[0] = inv.reshape(tp, CHUNK)

    def head(off):
        return pl.BlockSpec((1, tp, 128), lambda n, h: (h + off, n, 0))

    matmul_only = _sds((HEADS_A, T, 128), BF16)
    narrow = pl.BlockSpec((1, tp, CHUNK), lambda n, h: (h, n, 0))
    return _pc(body, name=name,
               out_shape=(_sds((HEADS_A, T, 128), F32), matmul_only, matmul_only, matmul_only, _sds((HEADS_A, T, CHUNK), BF16),
                          _sds((HEADS_A, T // CHUNK, 1, 128), F32), _sds((HEADS_A, T, CHUNK), F32)),
               grid=(T // tp, HEADS_A),
               in_specs=[head(0), head(HEADS_A), head(2 * HEADS_A), pl.BlockSpec((tp, 128), lambda n, h: (n, 0))],
               out_specs=(head(0), head(0), head(0), head(0), narrow,
                          pl.BlockSpec((1, nb, 1, 128), lambda n, h: (h, n, 0, 0)), narrow),
               sem=("parallel", "parallel"))(qkv, qkv, qkv, gb)


def gdn_prep_bwd(qkv, gb, inv, du, dw, dqd, dkd, da, dgl, name):
    T = qkv.shape[1]
    tp = _tile(T, PREP_T)
    nb = tp // CHUNK

    def body(q_ref, k_ref, v_ref, gb_ref, inv_ref, du_ref, dw_ref, dqd_ref, dkd_ref, da_ref, dgl_ref, dqkv_ref, dgb_ref):
        h = pl.program_id(1)
        shp = (nb, CHUNK, 128)
        stored = _stored_inverse(inv_ref[0].reshape(nb, CHUNK, CHUNK))
        q, k, v = (ref[0].astype(F32).reshape(shp) for ref in (q_ref, k_ref, v_ref))
        _, vjp = jax.vjp(lambda q, k, v, gb: _gdn_prep(q, k, v, gb, h, stored)[0], q, k, v, gb_ref[...].reshape(shp))
        dq, dk, dv, dgb = vjp((du_ref[0].reshape(shp), dw_ref[0].reshape(shp), dqd_ref[0].reshape(shp),
                               dkd_ref[0].reshape(shp), da_ref[0].reshape(nb, CHUNK, CHUNK), dgl_ref[0].reshape(nb, 1, 128)))
        dqkv_ref[h] = dq.reshape(tp, 128).astype(BF16)
        dqkv_ref[HEADS_A + h] = dk.reshape(tp, 128).astype(BF16)
        dqkv_ref[2 * HEADS_A + h] = dv.reshape(tp, 128).astype(BF16)

        @pl.when(h == 0)
        def _():
            dgb_ref[...] = jnp.zeros_like(dgb_ref)
        dgb_ref[...] += dgb.reshape(tp, 128)

    def head(off):
        return pl.BlockSpec((1, tp, 128), lambda n, h: (h + off, n, 0))

    narrow = pl.BlockSpec((1, tp, CHUNK), lambda n, h: (h, n, 0))
    return _pc(body, name=name, out_shape=(_sds((N_QKV_BLK, T, 128), BF16), _sds((T, 128), F32)),
               grid=(T // tp, HEADS_A),
               in_specs=[head(0), head(HEADS_A), head(2 * HEADS_A), pl.BlockSpec((tp, 128), lambda n, h: (n, 0)), narrow,
                         head(0), head(0), head(0), head(0), narrow,
                         pl.BlockSpec((1, nb, 1, 128), lambda n, h: (h, n, 0, 0))],
               out_specs=(pl.BlockSpec((N_QKV_BLK, tp, 128), lambda n, h: (0, n, 0)),
                          pl.BlockSpec((tp, 128), lambda n, h: (n, 0))),
               sem=("parallel", "arbitrary"))(qkv, qkv, qkv, gb, inv, du, dw, dqd, dkd, da, dgl)


def gdn_scan_fwd(u, w, qd, kd, aqk, gl, name):
    T = u.shape[1]
    n_chunks = T // CHUNK

    def body(u_ref, w_ref, qd_ref, kd_ref, a_ref, gl_ref, o_ref, sin_ref, state):
        @pl.when(pl.program_id(0) == 0)
        def _():
            state[...] = jnp.zeros_like(state)
        s = state[...]
        sb = s.astype(BF16)
        sin_ref[0] = sb
        both = _bmm(jnp.concatenate([w_ref[...], qd_ref[...]], axis=1).astype(BF16), sb, B_NN)
        vn = (u_ref[...] - both[:, :CHUNK]).astype(BF16)
        o_ref[...] = both[:, CHUNK:] + _bmm(a_ref[...].astype(BF16), vn, B_NN)
        state[...] = s * gl_ref[:, 0] + _bmm(kd_ref[...].astype(BF16), vn, B_TN)

    blk = pl.BlockSpec((HEADS_A, CHUNK, 128), lambda n: (0, n, 0))
    return _pc(body, name=name,
               out_shape=(_sds((HEADS_A, T, 128), F32), _sds((n_chunks, HEADS_A, DK, 128), BF16)), grid=(n_chunks,),
               in_specs=[blk, blk, blk, blk, pl.BlockSpec((HEADS_A, CHUNK, CHUNK), lambda n: (0, n, 0)),
                         pl.BlockSpec((HEADS_A, 1, 1, 128), lambda n: (0, n, 0, 0))],
               out_specs=(blk, pl.BlockSpec((1, HEADS_A, DK, 128), lambda n: (n, 0, 0, 0))),
               scratch=[pltpu.VMEM((HEADS_A, DK, 128), F32)], sem=("arbitrary",))(u, w, qd, kd, aqk, gl)


def gdn_scan_bwd(u, w, qd, kd, aqk, gl, sin, do, name):
    T = u.shape[1]
    n_chunks = T // CHUNK

    def body(u_ref, w_ref, qd_ref, kd_ref, a_ref, gl_ref, sin_ref, do_ref,
             du_ref, dw_ref, dqd_ref, dkd_ref, da_ref, dgl_ref, dstate):
        @pl.when(pl.program_id(0) == 0)
        def _():
            dstate[...] = jnp.zeros_like(dstate)
        lane0 = lax.broadcasted_iota(jnp.int32, (HEADS_A, 1, 128), 2) == 0
        sb = sin_ref[0]
        s = sb.astype(F32)
        wb, qdb, kdb = w_ref[...].astype(BF16), qd_ref[...].astype(BF16), kd_ref[...].astype(BF16)
        ab, dob = a_ref[...].astype(BF16), do_ref[...].astype(BF16)
        vn = (u_ref[...] - _bmm(wb, sb, B_NN)).astype(BF16)
        ds_out = dstate[...]
        dsb = ds_out.astype(BF16)
        dqd_ref[...] = _bmm(dob, sb, B_NT)
        da_ref[...] = _bmm(dob, vn, B_NT)
        dv = _bmm(ab, dob, B_TN) + _bmm(kdb, dsb, B_NN)
        dkd_ref[...] = _bmm(vn, dsb, B_NT)
        dgl = jnp.sum(jnp.sum(ds_out * s, axis=2, keepdims=True), axis=1, keepdims=True)
        dgl_ref[:, 0] = jnp.where(lane0, dgl, 0.0)
        du_ref[...] = dv
        dvb = dv.astype(BF16)
        dw_ref[...] = -_bmm(dvb, sb, B_NT)
        dstate[...] = ds_out * gl_ref[:, 0] + _bmm(qdb, dob, B_TN) - _bmm(wb, dvb, B_TN)

    last = n_chunks - 1
    blk = pl.BlockSpec((HEADS_A, CHUNK, 128), lambda n: (0, last - n, 0))
    ablk = pl.BlockSpec((HEADS_A, CHUNK, CHUNK), lambda n: (0, last - n, 0))
    glblk = pl.BlockSpec((HEADS_A, 1, 1, 128), lambda n: (0, last - n, 0, 0))
    per_head = _sds((HEADS_A, T, 128), F32)
    return _pc(body, name=name,
               out_shape=(per_head, per_head, per_head, per_head, _sds((HEADS_A, T, CHUNK), F32),
                          _sds((HEADS_A, n_chunks, 1, 128), F32)), grid=(n_chunks,),
               in_specs=[blk, blk, blk, blk, ablk, glblk,
                         pl.BlockSpec((1, HEADS_A, DK, 128), lambda n: (last - n, 0, 0, 0)), blk],
               out_specs=(blk, blk, blk, blk, ablk, glblk),
               scratch=[pltpu.VMEM((HEADS_A, DK, 128), F32)], sem=("arbitrary",))(u, w, qd, kd, aqk, gl, sin, do)


def gdn_outnorm_fwd(o, proj, wn, name):
    T = o.shape[1]
    tm = _tile(T, 512)

    def body(o_ref, z_ref, wn_ref, y_ref):
        for h in range(HEADS_A):
            z = z_ref[:, 128 * h:128 * (h + 1)].astype(F32)
            y_ref[:, 128 * h:128 * (h + 1)] = (_rms_fwd(o_ref[h], wn_ref[...]) * (z * _sigmoid(z))).astype(BF16)

    return _pc(body, name=name, out_shape=_sds((T, D), BF16), grid=(T // tm,),
               in_specs=[pl.BlockSpec((HEADS_A, tm, 128), lambda i: (0, i, 0)),
                         pl.BlockSpec((tm, D), lambda i: (i, Z_BLK0 * 128 // D)), pl.BlockSpec((1, 128), lambda i: (0, 0))],
               out_specs=pl.BlockSpec((tm, D), lambda i: (i, 0)), sem=("parallel",))(o, proj, wn)


def gdn_outnorm_bwd(o, proj, wn, dy, name):
    T = o.shape[1]
    tm = _tile(T, 512)

    def body(o_ref, z_ref, wn_ref, dy_ref, do_ref, dz_ref, dwn_ref):
        wn = wn_ref[...]
        acc = jnp.zeros((1, 128), F32)
        for h in range(HEADS_A):
            cols = slice(128 * h, 128 * (h + 1))
            z, dyh, ov = z_ref[:, cols].astype(F32), dy_ref[:, cols], o_ref[h]
            sg = _sigmoid(z)
            do, dwn = _rms_bwd(ov, wn, dyh * (z * sg))
            do_ref[h] = do
            acc = acc + dwn
            dz_ref[:, cols] = (dyh * _rms_fwd(ov, wn) * (sg * (1.0 + z * (1.0 - sg)))).astype(BF16)

        @pl.when(pl.program_id(0) == 0)
        def _():
            dwn_ref[...] = jnp.zeros_like(dwn_ref)
        dwn_ref[...] += acc

    row = pl.BlockSpec((tm, D), lambda i: (i, 0))
    vec = pl.BlockSpec((1, 128), lambda i: (0, 0))
    hblk = pl.BlockSpec((HEADS_A, tm, 128), lambda i: (0, i, 0))
    return _pc(body, name=name, out_shape=(_sds((HEADS_A, T, 128), F32), _sds((T, D), BF16), _sds((1, 128), F32)),
               grid=(T // tm,),
               in_specs=[hblk, pl.BlockSpec((tm, D), lambda i: (i, Z_BLK0 * 128 // D)), vec, row],
               out_specs=(hblk, row, vec), sem=("arbitrary",))(o, proj, wn, dy)


def gdn_forward(x, nw, w_in, wconv, al, dtb, wn, w_out, tag, deps=()):
    h = rmsnorm_bf16(x, nw, f"{tag}_norm", deps)
    proj = mm_nn(h, w_in, f"{tag}_proj", out_dtype=BF16, cols=(0, MAIN_COLS))
    ba = mm_nn(h, w_in, f"{tag}_proj_ba", cols=(MAIN_COLS, A_COLS))
    c, qkv = gdn_conv_fwd(proj, wconv, f"{tag}_conv")
    gb = gdn_gate_fwd(ba, al, dtb, f"{tag}_gate")
    u, w, qd, kd, aqk, gl, inv = gdn_prep_fwd(qkv, gb, f"{tag}_prep")
    o, sin = gdn_scan_fwd(u, w, qd, kd, aqk, gl, f"{tag}_scan")
    on = gdn_outnorm_fwd(o, proj, wn, f"{tag}_outnorm")
    y = mm_nn(on, w_out, f"{tag}_out", residual=x)
    return y, (x, h, proj, ba, c, qkv, gb, inv, (u, w, qd, kd, aqk, gl), sin, o, on)


def gdn_backward(dout, saved, nw, w_in, wconv, al, dtb, wn, w_out, tag):
    x, h, proj, ba, c, qkv, gb, inv, prep, sin, o, on = saved
    d_on = mm_nt(dout, w_out, f"{tag}_out_bwd")
    dw_out = mm_tn(on, dout, f"{tag}_out_wgrad")
    do, dz, dwn = gdn_outnorm_bwd(o, proj, wn, d_on, f"{tag}_outnorm_bwd")
    du, dw, dqd, dkd, da, dgl = gdn_scan_bwd(*prep, sin, do, f"{tag}_scan_bwd")
    dqkv, dgb = gdn_prep_bwd(qkv, gb, inv, du, dw, dqd, dkd, da, dgl, f"{tag}_prep_bwd")
    dba, dal, ddt = gdn_gate_bwd(ba, al, dtb, dgb, f"{tag}_gate_bwd")
    dpre, dwconv = gdn_conv_bwd(dqkv, c, proj, wconv, f"{tag}_conv_bwd")
    dproj = jnp.concatenate([dpre, dz, dba], axis=1)
    dw_in = mm_tn(h, dproj, f"{tag}_proj_wgrad")
    dh = mm_nt(dproj, w_in, f"{tag}_proj_bwd")
    dx, dnw = rmsnorm_bwd_add(x, nw, dh, dout, f"{tag}_norm_bwd")
    return dx, dnw, dw_in, dwconv, dal, ddt, dwn, dw_out


N_KV, GROUP = 4, 4
KV_COLS = 2 * N_KV * B_HD
B_COLS = D + KV_COLS


@jax.custom_vjp
def _swap_lane_halves(x):
    return pltpu.roll(x, 64, 1)


_swap_lane_halves.defvjp(lambda x: (pltpu.roll(x, 64, 1), None), lambda _, g: (pltpu.roll(g, 64, 1),))


def _swa_block(q, kp, kc, vp, vc, sk, first):
    rows = GROUP * B_BLK
    qi = lax.broadcasted_iota(jnp.int32, (N_KV, rows, B_BLK), 1) % B_BLK
    kj = lax.broadcasted_iota(jnp.int32, (N_KV, rows, B_BLK), 2)
    from_cur = kj <= qi

    def batch(parts):
        return jnp.concatenate([part[None] for part in parts], axis=0)

    def per_kv(cur, prev):
        return batch([jnp.concatenate([cur[:, j * B_HD:(j + 1) * B_HD], prev[:, j * B_HD:(j + 1) * B_HD]], axis=0)
                      for j in range(N_KV)]).astype(BF16)

    qs = batch([jnp.concatenate([q[:, hq * B_HD:(hq + 1) * B_HD] for hq in range(GROUP * j, GROUP * (j + 1))], axis=0)
                for j in range(N_KV)]).astype(BF16)
    sink = batch([jnp.concatenate([jnp.broadcast_to(sk[:, hq:hq + 1], (B_BLK, 1))
                                   for hq in range(GROUP * j, GROUP * (j + 1))], axis=0) for j in range(N_KV)])
    both = _bmm(qs, per_kv(kc, kp), B_NT)
    s = jnp.where(from_cur, both[..., :B_BLK], jnp.where(first, -1e30, both[..., B_BLK:])) * (B_HD ** -0.5)
    m = lax.stop_gradient(jnp.maximum(jnp.max(s, axis=-1, keepdims=True), sink))
    e = jnp.exp((s - m).astype(BF16))
    den = jnp.sum(e.astype(F32), axis=-1, keepdims=True) + jnp.exp(sink - m)
    p = e * (1.0 / den).astype(BF16)
    zero = jnp.zeros_like(p)
    p_both = jnp.concatenate([jnp.where(from_cur, p, zero), jnp.where(from_cur, zero, p)], axis=-1)
    o = _bmm(p_both, per_kv(vc, vp), B_NN)
    return jnp.concatenate([o[j, g * B_BLK:(g + 1) * B_BLK] for j in range(N_KV) for g in range(GROUP)], axis=1)


def swa_core_fwd(proj, sk, name):
    T = proj.shape[0]
    half = N_KV * B_HD

    def body(q_ref, kvc_ref, kvp_ref, sk_ref, o_ref):
        kvc, kvp = kvc_ref[...], kvp_ref[...]
        o_ref[...] = _swa_block(q_ref[...], kvp[:, :half], kvc[:, :half], kvp[:, half:], kvc[:, half:], sk_ref[...],
                                pl.program_id(0) == 0).astype(BF16)

    return _pc(body, name=name, out_shape=_sds((T, D), BF16), grid=(T // B_BLK,),
               in_specs=[pl.BlockSpec((B_BLK, D), lambda n: (n, 0)),
                         pl.BlockSpec((B_BLK, KV_COLS), lambda n: (n, D // KV_COLS)),
                         pl.BlockSpec((B_BLK, KV_COLS), lambda n: (jnp.maximum(n - 1, 0), D // KV_COLS)),
                         pl.BlockSpec((1, 128), lambda n: (0, 0))],
               out_specs=pl.BlockSpec((B_BLK, D), lambda n: (n, 0)), sem=("parallel",))(proj, proj, proj, sk)


def swa_core_bwd(proj, sk, do, name):
    T = proj.shape[0]
    last = T // B_BLK - 1
    half = N_KV * B_HD

    def body(q_ref, kvc_ref, kvp_ref, sk_ref, do_ref, dproj_ref, dbias_ref, dsk_ref, carry):
        step = pl.program_id(0)
        first = step == last

        @pl.when(step == 0)
        def _():
            carry[...] = jnp.zeros_like(carry)
            dbias_ref[...] = jnp.zeros_like(dbias_ref)
            dsk_ref[...] = jnp.zeros_like(dsk_ref)
        kvc, kvp = kvc_ref[...], kvp_ref[...]
        _, vjp = jax.vjp(functools.partial(_swa_block, first=first), q_ref[...], kvp[:, :half], kvc[:, :half],
                         kvp[:, half:], kvc[:, half:], sk_ref[...])
        dq, dkp, dkc, dvp, dvc, dsk = vjp(do_ref[...])
        dkv = jnp.concatenate([dkc, dvc], axis=1) + carry[...]
        carry[...] = jnp.concatenate([dkp, dvp], axis=1)
        row = jnp.concatenate([dq, dkv], axis=1)
        dproj_ref[...] = row.astype(BF16)
        dbias_ref[...] += jnp.sum(row, axis=0, keepdims=True)
        dsk_ref[...] += dsk

    return _pc(body, name=name, out_shape=(_sds((T, B_COLS), BF16), _sds((1, B_COLS), F32), _sds((1, 128), F32)),
               grid=(T // B_BLK,),
               in_specs=[pl.BlockSpec((B_BLK, D), lambda n: (last - n, 0)),
                         pl.BlockSpec((B_BLK, KV_COLS), lambda n: (last - n, D // KV_COLS)),
                         pl.BlockSpec((B_BLK, KV_COLS), lambda n: (jnp.maximum(last - n - 1, 0), D // KV_COLS)),
                         pl.BlockSpec((1, 128), lambda n: (0, 0)), pl.BlockSpec((B_BLK, D), lambda n: (last - n, 0))],
               out_specs=(pl.BlockSpec((B_BLK, B_COLS), lambda n: (last - n, 0)),
                          pl.BlockSpec((1, B_COLS), lambda n: (0, 0)), pl.BlockSpec((1, 128), lambda n: (0, 0))),
               scratch=[pltpu.VMEM((B_BLK, KV_COLS), F32)], sem=("arbitrary",))(proj, proj, proj, sk, do)


def col_sum(a, name):
    T, N = a.shape
    tm = _tile(T, 1024)

    def body(a_ref, o_ref):
        @pl.when(pl.program_id(0) == 0)
        def _():
            o_ref[...] = jnp.zeros_like(o_ref)
        o_ref[...] += jnp.sum(a_ref[...].astype(F32), axis=0, keepdims=True)

    return _pc(body, name=name, out_shape=_sds((1, N), F32), grid=(T // tm,),
               in_specs=[pl.BlockSpec((tm, N), lambda i: (i, 0))], out_specs=pl.BlockSpec((1, N), lambda i: (0, 0)),
               sem=("arbitrary",))(a)


def swa_forward(x, nw, w_in, b_in, sk, w_out, b_out, tag):
    h = rmsnorm_bf16(x, nw, f"{tag}_norm")
    proj = mm_nn(h, w_in, f"{tag}_proj", bias=b_in)
    o = swa_core_fwd(proj, sk, f"{tag}_core")
    y = mm_nn(o, w_out, f"{tag}_out", bias=b_out, residual=x)
    return y, (x, h, proj, o)


def swa_backward(dout, saved, nw, w_in, b_in, sk, w_out, b_out, tag):
    x, h, proj, o = saved
    do = mm_nt(dout, w_out, f"{tag}_out_bwd")
    dw_out = mm_tn(o, dout, f"{tag}_out_wgrad")
    db_out = col_sum(dout, f"{tag}_out_bias_grad")
    dproj, db_in, dsk = swa_core_bwd(proj, sk, do, f"{tag}_core_bwd")
    dw_in = mm_tn(h, dproj, f"{tag}_proj_wgrad")
    dh = mm_nt(dproj, w_in, f"{tag}_proj_bwd")
    dx, dnw = rmsnorm_bwd_add(x, nw, dh, dout, f"{tag}_norm_bwd")
    return dx, dnw, dw_in, db_in, dsk, dw_out, db_out


MESH = pl.DeviceIdType.MESH


def _position():
    return lax.axis_index("x"), lax.axis_index("y"), lax.axis_index("c")


def _slot(x, y, c):
    return 4 * x + 2 * y + c


def _peer(x, y, c, k):
    return (1 - x if k & 4 else x, 1 - y if k & 2 else y, 1 - c if k & 1 else c)


HBM_SPEC = pl.BlockSpec(memory_space=pltpu.HBM)
SEM_SPEC = pl.BlockSpec(memory_space=pltpu.SEMAPHORE)
DEP_SPEC = pl.BlockSpec(memory_space=pl.ANY)
SIDE_EFFECT = pltpu.SideEffectType.DATAFLOW_SIDE_EFFECTING
N_PEERS = N_DEV - 1


def _push_copies(srcs, lands, send_sems, recv_sems, scatter):
    x, y, c = _position()
    me = _slot(x, y, c)
    copies = []
    for k in (1, 2, 4, 3, 5, 6, 7):
        peer = _peer(x, y, c, k)
        for a in range(len(srcs)):
            copies.append(pltpu.make_async_remote_copy(
                src_ref=srcs[a].at[_slot(*peer)] if scatter else srcs[a], dst_ref=lands[a].at[me],
                send_sem=send_sems.at[N_PEERS * a + k - 1], recv_sem=recv_sems.at[N_PEERS * a + k - 1],
                device_id=peer, device_id_type=MESH))
    return copies


def push_start(srcs, lands, name, scatter, deps=()):
    n = len(srcs)
    first_out = 2 * n + len(deps)

    def body(*refs):
        for cp in _push_copies(refs[:n], refs[n:2 * n], refs[first_out], refs[first_out + 1], scatter):
            cp.start()
        refs[-1][...] = jnp.zeros_like(refs[-1])

    passed = [pltpu.HBM(t.shape, t.dtype) for t in list(srcs) + list(lands)]
    res = pl.pallas_call(
        body, name=name,
        out_shape=(pltpu.SemaphoreType.DMA((N_PEERS * n,)), pltpu.SemaphoreType.DMA((N_PEERS * n,)), *passed, _sds((8, 128), F32)),
        in_specs=[HBM_SPEC] * (2 * n) + [DEP_SPEC] * len(deps),
        out_specs=(SEM_SPEC, SEM_SPEC, *([HBM_SPEC] * (2 * n)), pl.BlockSpec(memory_space=pltpu.VMEM)),
        input_output_aliases={i: 2 + i for i in range(2 * n)},
        compiler_params=pltpu.CompilerParams(has_side_effects=SIDE_EFFECT),
    )(*[pltpu.with_memory_space_constraint(t, pltpu.HBM) for t in list(srcs) + list(lands)], *deps)
    return (res[0], res[1], list(res[2:2 + n]), list(res[2 + n:2 + 2 * n])), res[-1]


def push_wait(handles, after, name, scatter):
    send_sems, recv_sems, srcs, lands = handles
    n = len(srcs)
    after = tuple(after) if isinstance(after, (tuple, list)) else (after,)

    def body(*refs):
        for cp in _push_copies(refs[:n], refs[n:2 * n], refs[2 * n], refs[2 * n + 1], scatter):
            cp.wait_send()
            cp.wait_recv()

    res = pl.pallas_call(
        body, name=name, out_shape=tuple(pltpu.HBM(t.shape, t.dtype) for t in srcs + lands),
        in_specs=[HBM_SPEC] * (2 * n) + [SEM_SPEC, SEM_SPEC] + [DEP_SPEC] * len(after), out_specs=tuple([HBM_SPEC] * (2 * n)),
        input_output_aliases={i: i for i in range(2 * n)},
        compiler_params=pltpu.CompilerParams(has_side_effects=SIDE_EFFECT),
    )(*srcs, *lands, send_sems, recv_sems, *after)
    return list(res[n:])


def _landing_zones(arrays, name, scatter, deps=()):
    n = len(arrays)

    def body(*refs):
        outs, sems = refs[n + len(deps):2 * n + len(deps)], refs[-1]
        me = _slot(*_position())
        copies = [pltpu.make_async_copy(refs[a].at[me] if scatter else refs[a], outs[a].at[me], sems.at[a]) for a in range(n)]
        for cp in copies:
            cp.start()
        for cp in copies:
            cp.wait()

    shapes = [t.shape if scatter else (N_DEV,) + t.shape for t in arrays]
    return _pc(body, name=name, out_shape=[_sds(shape, t.dtype) for shape, t in zip(shapes, arrays)],
               in_specs=[DEP_SPEC] * (n + len(deps)), out_specs=[DEP_SPEC] * n,
               scratch=[pltpu.SemaphoreType.DMA((n,))])(*arrays, *deps)


def gather_start(shards, name, deps=()):
    return push_start(shards, _landing_zones(shards, f"{name}_own", False, deps), name, scatter=False, deps=deps)


def exchange_start(parts, name):
    return push_start(parts, _landing_zones(parts, f"{name}_own", True), name, scatter=True)


def _row_tile(rows, cols):
    best = rows
    for t in range(16, rows, 16):
        if rows % t == 0 and t * cols * 4 <= (1 << 20):
            best = t
    return best


def adam_update(parts, w, m, v, name):
    n_layers = len(parts)
    P, R, C = parts[0].shape
    tr = _row_tile(R, C)
    n_t = R // tr

    def body(*refs):
        p_refs = refs[:n_layers]
        w_ref, m_ref, v_ref, g_ref, d_ref, nm_ref, nv_ref = refs[n_layers:]
        for layer in range(n_layers):
            @pl.when(pl.program_id(0) == layer)
            def _(p_ref=p_refs[layer]):
                g = p_ref[0].astype(F32)
                for s in range(1, P):
                    g = g + p_ref[s].astype(F32)
                new_m = ADAM_B1 * m_ref[0] + (1.0 - ADAM_B1) * g
                new_v = ADAM_B2 * v_ref[0] + (1.0 - ADAM_B2) * (g * g)
                m_hat = new_m / (1.0 - ADAM_B1 ** ADAM_STEP)
                v_hat = new_v / (1.0 - ADAM_B2 ** ADAM_STEP)
                g_ref[0] = g
                d_ref[0] = -ADAM_LR * (m_hat / (jnp.sqrt(v_hat) + ADAM_EPS) + ADAM_WD * w_ref[0])
                nm_ref[0] = new_m
                nv_ref[0] = new_v

    def part_spec(layer):
        return pl.BlockSpec((P, tr, C), lambda l_, i: (0, jnp.where(l_ == layer, i, jnp.where(l_ < layer, 0, n_t - 1)), 0))

    blk = pl.BlockSpec((1, tr, C), lambda l_, i: (l_, i, 0))
    out = _sds((n_layers, R, C), F32)
    return _pc(body, name=name, out_shape=(out, out, out, out), grid=(n_layers, n_t),
               in_specs=[part_spec(layer) for layer in range(n_layers)] + [blk, blk, blk],
               out_specs=(blk, blk, blk, blk), sem=("arbitrary", "arbitrary"))(*parts, w, m, v)


WEIGHTS = ("ffn1_norm", "ffn1_w_gu", "ffn1_w_down", "mix_norm", "ffn2_norm", "ffn2_w_gu", "ffn2_w_down", "a_w_in",
           "a_w_conv", "a_A_log", "a_dt_bias", "a_out_norm", "a_w_out", "b_w_in", "b_b_in", "b_sinks", "b_w_out",
           "b_b_out", "final_norm")
SHARDED = ("ffn1_w_gu", "ffn1_w_down", "ffn2_w_gu", "ffn2_w_down", "a_w_in", "a_w_conv", "a_w_out", "b_w_in", "b_b_in",
           "b_w_out", "b_b_out")
MISC_LANES = dict(a_A_log=(0, 8), a_dt_bias=(8, 16), b_sinks=(16, 32), a_out_norm=(128, 256))
LOSS_LANE = 256


def _pack_small(t):
    misc = jnp.zeros((D,), F32)
    for key, (lo, hi) in MISC_LANES.items():
        misc = misc.at[lo:hi].set(t[key].reshape(-1))
    if "loss" in t:
        misc = misc.at[LOSS_LANE].set(t["loss"])
    return jnp.concatenate([t["ffn1_norm"], t["mix_norm"], t["ffn2_norm"], t["final_norm"].reshape(1, D), misc[None]], axis=0)


def _unpack_small(p, like):
    out = dict(ffn1_norm=p[0:2], mix_norm=p[2:4], ffn2_norm=p[4:6], final_norm=p[6])
    for key, (lo, hi) in MISC_LANES.items():
        out[key] = p[7, lo:hi].reshape(like[key].shape)
    return out


def kernel(x, ffn1_norm, ffn1_w_gu, ffn1_w_down, mix_norm, ffn2_norm, ffn2_w_gu, ffn2_w_down, a_w_in, a_w_conv, a_A_log, a_dt_bias, a_out_norm, a_w_out, b_w_in, b_b_in, b_sinks, b_w_out, b_b_out, final_norm, loss_target, m_ffn1_norm, m_ffn1_w_gu, m_ffn1_w_down, m_mix_norm, m_ffn2_norm, m_ffn2_w_gu, m_ffn2_w_down, m_a_w_in, m_a_w_conv, m_a_A_log, m_a_dt_bias, m_a_out_norm, m_a_w_out, m_b_w_in, m_b_b_in, m_b_sinks, m_b_w_out, m_b_b_out, m_final_norm, v_ffn1_norm, v_ffn1_w_gu, v_ffn1_w_down, v_mix_norm, v_ffn2_norm, v_ffn2_w_gu, v_ffn2_w_down, v_a_w_in, v_a_w_conv, v_a_A_log, v_a_dt_bias, v_a_out_norm, v_a_w_out, v_b_w_in, v_b_b_in, v_b_sinks, v_b_w_out, v_b_b_out, v_final_norm):
    w = dict(ffn1_norm=ffn1_norm, ffn1_w_gu=ffn1_w_gu, ffn1_w_down=ffn1_w_down, mix_norm=mix_norm, ffn2_norm=ffn2_norm, ffn2_w_gu=ffn2_w_gu, ffn2_w_down=ffn2_w_down, a_w_in=a_w_in, a_w_conv=a_w_conv, a_A_log=a_A_log, a_dt_bias=a_dt_bias, a_out_norm=a_out_norm, a_w_out=a_w_out, b_w_in=b_w_in, b_b_in=b_b_in, b_sinks=b_sinks, b_w_out=b_w_out, b_b_out=b_b_out, final_norm=final_norm)
    m = dict(ffn1_norm=m_ffn1_norm, ffn1_w_gu=m_ffn1_w_gu, ffn1_w_down=m_ffn1_w_down, mix_norm=m_mix_norm, ffn2_norm=m_ffn2_norm, ffn2_w_gu=m_ffn2_w_gu, ffn2_w_down=m_ffn2_w_down, a_w_in=m_a_w_in, a_w_conv=m_a_w_conv, a_A_log=m_a_A_log, a_dt_bias=m_a_dt_bias, a_out_norm=m_a_out_norm, a_w_out=m_a_w_out, b_w_in=m_b_w_in, b_b_in=m_b_b_in, b_sinks=m_b_sinks, b_w_out=m_b_w_out, b_b_out=m_b_b_out, final_norm=m_final_norm)
    v = dict(ffn1_norm=v_ffn1_norm, ffn1_w_gu=v_ffn1_w_gu, ffn1_w_down=v_ffn1_w_down, mix_norm=v_mix_norm, ffn2_norm=v_ffn2_norm, ffn2_w_gu=v_ffn2_w_gu, ffn2_w_down=v_ffn2_w_down, a_w_in=v_a_w_in, a_w_conv=v_a_w_conv, a_A_log=v_a_A_log, a_dt_bias=v_a_dt_bias, a_out_norm=v_a_out_norm, a_w_out=v_a_w_out, b_w_in=v_b_w_in, b_b_in=v_b_b_in, b_sinks=v_b_sinks, b_w_out=v_b_w_out, b_b_out=v_b_b_out, final_norm=v_final_norm)
    T = x.shape[1]
    x0, tgt = x.reshape(T, D), loss_target.reshape(T, D)

    def cast(t):
        return t.astype(BF16)

    h0, t0 = gather_start([cast(ffn1_w_gu[0])], "gather0_start")
    a_log_row = jnp.zeros((1, 128), F32).at[0, HEADS_A:2 * HEADS_A].set(a_A_log[0])
    dt_row = jnp.zeros((1, 128), F32).at[0, HEADS_A:2 * HEADS_A].set(a_dt_bias[0])
    sink_row = jnp.zeros((1, 128), F32).at[0, :b_sinks.shape[1]].set(b_sinks[0])
    a_in_cols = a_w_in.shape[-1] * N_DEV

    def down_blocks(t):
        return t.reshape(N_FB, FB, D)

    wgu, wdn, saved = {}, {}, []
    xn = rmsnorm_bf16(x0, ffn1_norm[0:1], "l0_ffn1_norm", (t0,))
    wgu["ffn1", 0] = push_wait(h0, xn, "gather0_wait", scatter=False)[0]
    h0d, t0d = gather_start([cast(ffn1_w_down[0])], "gather0d_start", deps=(wgu["ffn1", 0],))
    h1, t1 = gather_start([cast(a_w_in[0]), a_w_conv[0], cast(a_w_out[0])], "gather1_start", deps=(t0d,))
    gu = ffn_up(xn, wgu["ffn1", 0], "l0_ffn1_up", deps=(t0d, t1))
    wdn["ffn1", 0] = down_blocks(push_wait(h0d, gu, "gather0d_wait", scatter=False)[0])
    xs, s1 = ffn_down(gu, wdn["ffn1", 0], x0, "l0_ffn1_down"), (x0, xn, gu)
    got = push_wait(h1, xs, "gather1_wait", scatter=False)
    h1f, t1f = gather_start([cast(ffn2_w_gu[0]), cast(ffn2_w_down[0])], "gather1f_start", deps=(got[0],))
    g2 = [cast(ffn1_w_gu[1]), cast(ffn1_w_down[1]), cast(b_w_in[0]), b_b_in, cast(b_w_out[0]), b_b_out,
          cast(ffn2_w_gu[1]), cast(ffn2_w_down[1])]
    h2, t2 = gather_start(g2, "gather2_start", deps=(t1f,))
    a_in_full = jnp.pad(got[0].transpose(1, 0, 2).reshape(D, a_in_cols), ((0, 0), (0, A_COLS - a_in_cols)))
    gdn_args = (mix_norm[0:1], a_in_full, got[1].transpose(1, 0, 2).reshape(4, 3 * D), a_log_row, dt_row, a_out_norm,
                got[2].reshape(D, D))
    xs, sm = gdn_forward(xs, *gdn_args, "gdn", deps=(t1f, t2))
    got = push_wait(h1f, xs, "gather1f_wait", scatter=False)
    wgu["ffn2", 0], wdn["ffn2", 0] = got[0], down_blocks(got[1])
    xs, s2 = ffn_forward(xs, ffn2_norm[0:1], wgu["ffn2", 0], wdn["ffn2", 0], "l0_ffn2")
    saved.append((s1, sm, s2))
    got = push_wait(h2, xs, "gather2_wait", scatter=False)
    wgu["ffn1", 1], wdn["ffn1", 1] = got[0], down_blocks(got[1])
    swa_args = (mix_norm[1:2], got[2].transpose(1, 0, 2).reshape(D, B_COLS), got[3].reshape(1, B_COLS), sink_row,
                got[4].reshape(D, D), got[5].reshape(1, D))
    wgu["ffn2", 1], wdn["ffn2", 1] = got[6], down_blocks(got[7])
    xs, s1 = ffn_forward(xs, ffn1_norm[1:2], wgu["ffn1", 1], wdn["ffn1", 1], "l1_ffn1")
    xs, sm = swa_forward(xs, *swa_args, "swa")
    xs, s2 = ffn_forward(xs, ffn2_norm[1:2], wgu["ffn2", 1], wdn["ffn2", 1], "l1_ffn2")
    saved.append((s1, sm, s2))
    loss_row, dx, d_final_norm = final_loss(xs, final_norm.reshape(1, D), tgt, "final_loss")

    def down_slots(t):
        return cast(t.reshape(N_DEV, FB // 2, D))

    def col_slots(t, dtype=BF16):
        return t.reshape(t.shape[0], N_DEV, -1).transpose(1, 0, 2).astype(dtype)

    d_norm = {"ffn1_norm": [None, None], "mix_norm": [None, None], "ffn2_norm": [None, None]}
    s1, sm, s2 = saved[1]
    dx, d_norm["ffn2_norm"][1], d_gu, d_dn = ffn_backward(dx, s2, ffn2_norm[1:2], wgu["ffn2", 1], wdn["ffn2", 1], "l1_ffn2")
    sent1 = [cast(d_gu), down_slots(d_dn)]
    dx, d_norm["mix_norm"][1], d_b_in, d_b_bias_in, d_sinks, d_b_out, d_b_bias_out = swa_backward(dx, sm, *swa_args, "swa")
    sent1 += [col_slots(d_b_in), d_b_bias_in.reshape(N_DEV, 1, -1), cast(d_b_out.reshape(N_DEV, D // N_DEV, D)),
              d_b_bias_out.reshape(N_DEV, 1, -1)]
    dx, d_norm["ffn1_norm"][1], d_gu, d_dn = ffn_backward(dx, s1, ffn1_norm[1:2], wgu["ffn1", 1], wdn["ffn1", 1], "l1_ffn1")
    sent1 += [cast(d_gu), down_slots(d_dn)]
    x1, tx1 = exchange_start(sent1, "exchange1_start")

    s1, sm, s2 = saved[0]
    dx, d_norm["ffn2_norm"][0], d_gu, d_dn = ffn_backward(dx, s2, ffn2_norm[0:1], wgu["ffn2", 0], wdn["ffn2", 0], "l0_ffn2",
                                                           deps=(tx1,))
    sent2 = [cast(d_gu), down_slots(d_dn)]
    dx, d_norm["mix_norm"][0], d_a_in, d_a_conv, d_alog, d_dt, d_onorm, d_a_out = gdn_backward(dx, sm, *gdn_args, "gdn")
    sent2 += [col_slots(d_a_in[:, :a_in_cols]), col_slots(d_a_conv, F32), cast(d_a_out.reshape(N_DEV, D // N_DEV, D))]
    x2, tx2 = exchange_start(sent2, "exchange2_start")
    last = {}

    def send_last(d_gu, d_dn):
        last["handles"], token = exchange_start([cast(d_gu), down_slots(d_dn)], "exchange3_start")
        return (token,)

    dx, d_norm["ffn1_norm"][0], _, _ = ffn_backward(dx, s1, ffn1_norm[0:1], wgu["ffn1", 0], wdn["ffn1", 0], "l0_ffn1",
                                                    deps=(tx2,), on_grads=send_last)
    grad_x = dx.reshape(x.shape)
    r1 = push_wait(x1, dx, "exchange1_wait", scatter=True)
    r2 = push_wait(x2, dx, "exchange2_wait", scatter=True)
    received = dict(ffn2_w_gu=[r2[0], r1[0]], ffn2_w_down=[r2[1], r1[1]],
                    b_w_in=[r1[2]], b_b_in=[r1[3]], b_w_out=[r1[4]], b_b_out=[r1[5]],
                    a_w_in=[r2[2]], a_w_conv=[r2[3]], a_w_out=[r2[4]])

    grads, deltas, new_m, new_v = {}, {}, {}, {}

    def update(key):
        shape = w[key].shape
        cols = shape[-1]
        layers = lambda t: t.reshape(shape[0], -1, cols)
        out = adam_update([r.reshape(N_DEV, -1, cols) for r in received[key]], layers(w[key]), layers(m[key]), layers(v[key]),
                          f"adam_{key}")
        grads[key], deltas[key], new_m[key], new_v[key] = (t.reshape(shape) for t in out)

    for key in SHARDED:
        if key in received:
            update(key)
    done_first = [deltas[key] for key in received]

    small = dict(ffn1_norm=jnp.concatenate(d_norm["ffn1_norm"], axis=0), mix_norm=jnp.concatenate(d_norm["mix_norm"], axis=0),
                 ffn2_norm=jnp.concatenate(d_norm["ffn2_norm"], axis=0), final_norm=d_final_norm,
                 a_A_log=d_alog[0, HEADS_A:2 * HEADS_A], a_dt_bias=d_dt[0, HEADS_A:2 * HEADS_A],
                 b_sinks=d_sinks[0, :b_sinks.shape[1]], a_out_norm=d_onorm, loss=loss_row[0, 0])
    hs, ts = gather_start([_pack_small(small)], "gather_small_start")
    r3 = push_wait(last["handles"], done_first + [ts], "exchange3_wait", scatter=True)
    received.update(ffn1_w_gu=[r3[0], r1[6]], ffn1_w_down=[r3[1], r1[7]])
    update("ffn1_w_gu")
    update("ffn1_w_down")
    every = push_wait(hs, deltas["ffn1_w_down"], "gather_small_wait", scatter=False)[0]
    out = adam_update([every], _pack_small(w)[None], _pack_small(m)[None], _pack_small(v)[None], "adam_small")
    for dst, packed in zip((grads, deltas, new_m, new_v), out):
        dst.update(_unpack_small(packed[0], w))
    loss = out[0][0, 7, LOSS_LANE]

    return (loss, grad_x, *[grads[k_] for k_ in WEIGHTS], *[deltas[k_] for k_ in WEIGHTS],
            *[new_m[k_] for k_ in WEIGHTS], *[new_v[k_] for k_ in WEIGHTS])
```

```python
import functools

import jax
import jax.numpy as jnp
from jax import lax
from jax.experimental import pallas as pl
from jax.experimental.pallas import tpu as pltpu

F32, BF16 = jnp.float32, jnp.bfloat16
HI = lax.Precision.HIGHEST
EPS = 1e-6

N_DEV = 8
D = 1024
FB = 704
N_FB = 4
HEADS_A, DK = 8, 128
CHUNK = 64
PREP_T = 512
A_COLS = 4224
B_HD, B_BLK = 64, 128
VMEM_LIMIT_V7X = 60 * 1024 * 1024

ADAM_LR, ADAM_B1, ADAM_B2, ADAM_EPS, ADAM_WD, ADAM_STEP = 0.001, 0.9, 0.999, 1e-08, 0.01, 10

NT = (((1,), (1,)), ((), ()))
TN = (((0,), (0,)), ((), ()))


def _pc(body, *, name, out_shape, grid=(), in_specs=None, out_specs=None, scratch=(), sem=None, **kw):
    params = pltpu.CompilerParams(dimension_semantics=sem, vmem_limit_bytes=VMEM_LIMIT_V7X)
    return pl.pallas_call(body, name=name, out_shape=out_shape, grid=grid, in_specs=in_specs, out_specs=out_specs,
                          scratch_shapes=list(scratch), compiler_params=params, **kw)


def _sds(shape, dtype):
    return jax.ShapeDtypeStruct(tuple(shape), dtype)


def _dot(a, b, dims=None, precision=None):
    if dims is None:
        return jnp.dot(a, b, preferred_element_type=F32, precision=precision)
    return lax.dot_general(a, b, dims, preferred_element_type=F32, precision=precision)


def _sigmoid(x):
    return 1.0 / (1.0 + jnp.exp(-x))


def _softplus(x):
    return jnp.maximum(x, 0.0) + jnp.log(1.0 + jnp.exp(-jnp.abs(x)))


def _rms_fwd(x, w):
    r = lax.rsqrt(jnp.mean(x * x, axis=-1, keepdims=True) + EPS)
    return x * r * w


def _rms_bwd(x, w, dy):
    r = lax.rsqrt(jnp.mean(x * x, axis=-1, keepdims=True) + EPS)
    xh = x * r
    dxh = dy * w
    dx = r * (dxh - xh * jnp.mean(dxh * xh, axis=-1, keepdims=True))
    return dx, jnp.sum(dy * xh, axis=0, keepdims=True)


def _tile(n, want):
    t = min(n, want)
    assert n % t == 0, (n, want)
    return t


def rmsnorm_bf16(x, w, name, deps=()):
    T = x.shape[0]
    tm = _tile(T, 1024)

    def body(x_ref, w_ref, *rest):
        rest[-1][...] = _rms_fwd(x_ref[...], w_ref[...]).astype(BF16)

    return _pc(body, name=name, out_shape=_sds((T, D), BF16), grid=(T // tm,),
               in_specs=[pl.BlockSpec((tm, D), lambda i: (i, 0)), pl.BlockSpec((1, D), lambda i: (0, 0))] + [DEP_SPEC] * len(deps),
               out_specs=pl.BlockSpec((tm, D), lambda i: (i, 0)), sem=("parallel",))(x, w, *deps)


def rmsnorm_bwd_add(x, w, dxn, dres, name):
    T = x.shape[0]
    tm = _tile(T, 512)

    def body(x_ref, w_ref, dxn_ref, dres_ref, dx_ref, dw_ref):
        dx, dw = _rms_bwd(x_ref[...], w_ref[...], dxn_ref[...])
        dx_ref[...] = dres_ref[...] + dx

        @pl.when(pl.program_id(0) == 0)
        def _():
            dw_ref[...] = jnp.zeros_like(dw_ref)
        dw_ref[...] += dw

    row = pl.BlockSpec((tm, D), lambda i: (i, 0))
    vec = pl.BlockSpec((1, D), lambda i: (0, 0))
    return _pc(body, name=name, out_shape=(_sds((T, D), F32), _sds((1, D), F32)), grid=(T // tm,),
               in_specs=[row, vec, row, row], out_specs=(row, vec), sem=("arbitrary",))(x, w, dxn, dres)


def final_loss(x, w, tgt, name):
    T = x.shape[0]
    tm = _tile(T, 512)

    def body(x_ref, w_ref, t_ref, loss_ref, dx_ref, dw_ref):
        xv, wv = x_ref[...], w_ref[...]
        err = _rms_fwd(xv, wv) - t_ref[...]
        dx, dw = _rms_bwd(xv, wv, err * (1.0 / D))
        dx_ref[...] = dx

        @pl.when(pl.program_id(0) == 0)
        def _():
            dw_ref[...] = jnp.zeros_like(dw_ref)
            loss_ref[...] = jnp.zeros_like(loss_ref)
        dw_ref[...] += dw
        loss_ref[...] += jnp.full((1, 128), 0.5 / D, F32) * jnp.sum(err * err)

    row = pl.BlockSpec((tm, D), lambda i: (i, 0))
    vec = pl.BlockSpec((1, D), lambda i: (0, 0))
    return _pc(body, name=name, out_shape=(_sds((1, 128), F32), _sds((T, D), F32), _sds((1, D), F32)),
               grid=(T // tm,), in_specs=[row, vec, row],
               out_specs=(pl.BlockSpec((1, 128), lambda i: (0, 0)), row, vec), sem=("arbitrary",))(x, w, tgt)


def _col_tile(n):
    for t in (1536, 1408, 1024, 768, 512, 384, 256, 128):
        if n % t == 0:
            return t
    return n


def mm_nn(a, b, name, bias=None, residual=None, out_dtype=F32, cols=None):
    T, K = a.shape
    first, end = cols or (0, b.shape[1])
    N = end - first
    tm, tn = _tile(T, 512), _col_tile(N)
    assert first % tn == 0 and (cols is None or (bias is None and residual is None))
    j0 = first // tn

    def body(a_ref, b_ref, *rest):
        o_ref = rest[-1]
        acc = _dot(a_ref[...].astype(BF16), b_ref[...])
        for extra in rest[:-1]:
            acc = acc + extra[...]
        o_ref[...] = acc.astype(out_dtype)

    in_specs = [pl.BlockSpec((tm, K), lambda j, i: (i, 0)), pl.BlockSpec((K, tn), lambda j, i: (0, j0 + j))]
    args = [a, b]
    if bias is not None:
        in_specs.append(pl.BlockSpec((1, tn), lambda j, i: (0, j)))
        args.append(bias)
    if residual is not None:
        in_specs.append(pl.BlockSpec((tm, tn), lambda j, i: (i, j)))
        args.append(residual)
    return _pc(body, name=name, out_shape=_sds((T, N), out_dtype), grid=(N // tn, T // tm), in_specs=in_specs,
               out_specs=pl.BlockSpec((tm, tn), lambda j, i: (i, j)), sem=("parallel", "parallel"))(*args)


def mm_nt(a, b, name, out_dtype=F32):
    T, N = a.shape
    K = b.shape[0]
    tm = _tile(T, 512)

    def body(a_ref, b_ref, o_ref):
        o_ref[...] = _dot(a_ref[...].astype(BF16), b_ref[...], NT).astype(out_dtype)

    return _pc(body, name=name, out_shape=_sds((T, K), out_dtype), grid=(T // tm,),
               in_specs=[pl.BlockSpec((tm, N), lambda i: (i, 0)), pl.BlockSpec((K, N), lambda i: (0, 0))],
               out_specs=pl.BlockSpec((tm, K), lambda i: (i, 0)), sem=("parallel",))(a, b)


def mm_tn(a, b, name):
    T, K = a.shape
    N = b.shape[1]
    tt, tn = _tile(T, 1024), _col_tile(N)

    def body(a_ref, b_ref, o_ref):
        @pl.when(pl.program_id(1) == 0)
        def _():
            o_ref[...] = jnp.zeros_like(o_ref)
        o_ref[...] += _dot(a_ref[...].astype(BF16), b_ref[...].astype(BF16), TN)

    return _pc(body, name=name, out_shape=_sds((K, N), F32), grid=(N // tn, T // tt),
               in_specs=[pl.BlockSpec((tt, K), lambda j, t: (t, 0)), pl.BlockSpec((tt, tn), lambda j, t: (t, j))],
               out_specs=pl.BlockSpec((K, tn), lambda j, t: (0, j)), sem=("parallel", "arbitrary"))(a, b)


def ffn_up(xn, wgu, name, deps=()):
    T = xn.shape[0]
    tm = _tile(T, 1024)

    def body(x_ref, w_ref, *rest):
        xv = x_ref[...]
        for j in range(2 * N_FB):
            rest[-1][j] = _dot(xv, w_ref[j]).astype(BF16)

    return _pc(body, name=name, out_shape=_sds((2 * N_FB, T, FB), BF16), grid=(T // tm,),
               in_specs=[pl.BlockSpec((tm, D), lambda i: (i, 0)), _resident((2 * N_FB, D, FB))] + [DEP_SPEC] * len(deps),
               out_specs=pl.BlockSpec((2 * N_FB, tm, FB), lambda i: (0, i, 0)), sem=("parallel",))(xn, wgu, *deps)


def ffn_down(gu, wd, x, name):
    T = x.shape[0]
    tm = _tile(T, 512)

    def body(gu_ref, w_ref, x_ref, o_ref):
        acc = jnp.zeros((tm, D), F32)
        for g in range(N_FB):
            gate, up = gu_ref[g], gu_ref[N_FB + g]
            acc = acc + _dot(gate * _sigmoid(gate) * up, w_ref[g])
        o_ref[...] = x_ref[...] + 0.5 * acc

    row = pl.BlockSpec((tm, D), lambda i: (i, 0))
    return _pc(body, name=name, out_shape=_sds((T, D), F32), grid=(T // tm,),
               in_specs=[pl.BlockSpec((2 * N_FB, tm, FB), lambda i: (0, i, 0)),
                         _resident((N_FB, FB, D)), row],
               out_specs=row, sem=("parallel",))(gu, wd, x)


def _resident(shape):
    return pl.BlockSpec(shape, lambda *_: (0,) * len(shape), pipeline_mode=pl.Buffered(1))


def _store_blocks_bf16(acc, out_hbm, stage, sem):
    for j in range(acc.shape[0]):
        stage[...] = acc[j].astype(BF16)
        copy = pltpu.make_async_copy(stage, out_hbm.at[j], sem)
        copy.start()
        copy.wait()


def ffn_bwd_hidden(dout, wd, gu, name, deps=()):
    T = dout.shape[0]
    tm = _tile(T, 512)
    n_t = T // tm

    def body(d_ref, w_ref, gu_ref, *rest):
        dgu_ref, dwd_hbm, acc, stage, sem = rest[-5:]
        t = pl.program_id(0)

        @pl.when(t == 0)
        def _():
            acc[...] = jnp.zeros_like(acc)
        dy = (0.5 * d_ref[...]).astype(BF16)
        for g in range(N_FB):
            gate, up = gu_ref[g], gu_ref[N_FB + g]
            sg = _sigmoid(gate)
            silu = gate * sg
            dact = _dot(dy, w_ref[g], NT).astype(BF16)
            acc[g] += _dot(silu * up, dy, TN)
            dgu_ref[g] = dact * up * (sg * (1.0 + gate * (1.0 - sg)))
            dgu_ref[N_FB + g] = dact * silu

        @pl.when(t == n_t - 1)
        def _():
            _store_blocks_bf16(acc, dwd_hbm, stage, sem)

    return _pc(body, name=name, out_shape=(_sds((2 * N_FB, T, FB), BF16), _sds((N_FB, FB, D), BF16)), grid=(n_t,),
               in_specs=[pl.BlockSpec((tm, D), lambda i: (i, 0)), _resident((N_FB, FB, D)),
                         pl.BlockSpec((2 * N_FB, tm, FB), lambda i: (0, i, 0))] + [DEP_SPEC] * len(deps),
               out_specs=(pl.BlockSpec((2 * N_FB, tm, FB), lambda i: (0, i, 0)), pl.BlockSpec(memory_space=pl.ANY)),
               scratch=[pltpu.VMEM((N_FB, FB, D), F32), pltpu.VMEM((FB, D), BF16), pltpu.SemaphoreType.DMA],
               sem=("arbitrary",))(dout, wd, gu, *deps)


def ffn_bwd_input(dgu, wgu, x, dout, nw, name, deps=()):
    T = x.shape[0]
    tm = _tile(T, 512)

    def body(dgu_ref, w_ref, x_ref, d_ref, nw_ref, *rest):
        dx_ref, dnw_ref = rest[-2:]
        dxn = jnp.zeros((tm, D), F32)
        for j in range(2 * N_FB):
            dxn = dxn + _dot(dgu_ref[j], w_ref[j], NT)
        dx, dw = _rms_bwd(x_ref[...], nw_ref[...], dxn)
        dx_ref[...] = d_ref[...] + dx

        @pl.when(pl.program_id(0) == 0)
        def _():
            dnw_ref[...] = jnp.zeros_like(dnw_ref)
        dnw_ref[...] += dw

    row = pl.BlockSpec((tm, D), lambda i: (i, 0))
    vec = pl.BlockSpec((1, D), lambda i: (0, 0))
    return _pc(body, name=name, out_shape=(_sds((T, D), F32), _sds((1, D), F32)), grid=(T // tm,),
               in_specs=[pl.BlockSpec((2 * N_FB, tm, FB), lambda i: (0, i, 0)), _resident((2 * N_FB, D, FB)),
                         row, row, vec] + [DEP_SPEC] * len(deps),
               out_specs=(row, vec), sem=("arbitrary",))(dgu, wgu, x, dout, nw, *deps)


def ffn_wgrad_gu(xn, dgu, name):
    T = xn.shape[0]
    tt = _tile(T, 1024)
    n_t = T // tt

    def body(x_ref, d_ref, dw_hbm, acc, stage, sem):
        t = pl.program_id(0)

        @pl.when(t == 0)
        def _():
            acc[...] = jnp.zeros_like(acc)
        xn_tile = x_ref[...]
        for j in range(2 * N_FB):
            acc[j] += _dot(xn_tile, d_ref[j], TN)

        @pl.when(t == n_t - 1)
        def _():
            _store_blocks_bf16(acc, dw_hbm, stage, sem)

    return _pc(body, name=name, out_shape=_sds((2 * N_FB, D, FB), BF16), grid=(n_t,),
               in_specs=[pl.BlockSpec((tt, D), lambda t: (t, 0)), pl.BlockSpec((2 * N_FB, tt, FB), lambda t: (0, t, 0))],
               out_specs=pl.BlockSpec(memory_space=pl.ANY),
               scratch=[pltpu.VMEM((2 * N_FB, D, FB), F32), pltpu.VMEM((D, FB), BF16), pltpu.SemaphoreType.DMA],
               sem=("arbitrary",))(xn, dgu)


def ffn_forward(x, nw, wgu, wd, tag):
    T = x.shape[0]
    tm = _tile(T, 512)

    def body(x_ref, nw_ref, wgu_ref, wd_ref, o_ref, xn_ref, gu_ref):
        xv = x_ref[...]
        xn = _rms_fwd(xv, nw_ref[...]).astype(BF16)
        xn_ref[...] = xn
        for j in range(2 * N_FB):
            gu_ref[j] = _dot(xn, wgu_ref[j]).astype(BF16)
        acc = jnp.zeros((tm, D), F32)
        for g in range(N_FB):
            gate, up = gu_ref[g], gu_ref[N_FB + g]
            acc = acc + _dot(gate * _sigmoid(gate) * up, wd_ref[g])
        o_ref[...] = xv + 0.5 * acc

    row = pl.BlockSpec((tm, D), lambda i: (i, 0))
    out, xn, gu = _pc(body, name=f"{tag}_fwd",
                      out_shape=(_sds((T, D), F32), _sds((T, D), BF16), _sds((2 * N_FB, T, FB), BF16)), grid=(T // tm,),
                      in_specs=[row, pl.BlockSpec((1, D), lambda i: (0, 0)), _resident((2 * N_FB, D, FB)),
                                _resident((N_FB, FB, D))],
                      out_specs=(row, row, pl.BlockSpec((2 * N_FB, tm, FB), lambda i: (0, i, 0))),
                      sem=("parallel",))(x, nw, wgu, wd)
    return out, (x, xn, gu)


def ffn_backward(dout, saved, nw, wgu, wd, tag, deps=(), on_grads=None):
    x, xn, gu = saved
    dgu, dwd = ffn_bwd_hidden(dout, wd, gu, f"{tag}_bwd_hidden", deps)
    dwgu = ffn_wgrad_gu(xn, dgu, f"{tag}_wgrad_gu")
    late = on_grads(dwgu, dwd) if on_grads else ()
    dx, dnw = ffn_bwd_input(dgu, wgu, x, dout, nw, f"{tag}_bwd_input", late)
    return dx, dnw, dwgu, dwd


N_QKV_BLK = 3 * HEADS_A
Z_BLK0 = N_QKV_BLK
MAIN_COLS = 4 * D
HALO = 16


def _conv_taps(xcat, w):
    c = xcat[HALO:] * w[3:4]
    for k in range(3):
        c = c + pltpu.roll(xcat, 3 - k, 0)[HALO:] * w[k:k + 1]
    return c


def _head_cols(h):
    return slice(128 * h, 128 * (h + 1))


def gdn_conv_fwd(proj, wconv, name):
    T = proj.shape[0]
    tm = _tile(T, 512)

    def body(cur_ref, prev_ref, w_ref, c_ref, y_ref):
        kind, t = pl.program_id(0), pl.program_id(1)
        prev = jnp.where(t > 0, prev_ref[...].astype(F32), 0.0)
        c = _conv_taps(jnp.concatenate([prev, cur_ref[...].astype(F32)], axis=0), w_ref[...])
        c_ref[...] = c.astype(BF16)
        s = c * _sigmoid(c)
        scale = jnp.where(kind == 0, DK ** -0.5, 1.0)
        for h in range(HEADS_A):
            sh = s[:, _head_cols(h)]
            r = lax.rsqrt(jnp.sum(sh * sh, axis=-1, keepdims=True) + EPS)
            y_ref[h] = (sh * jnp.where(kind < 2, r * scale, 1.0)).astype(BF16)

    return _pc(body, name=name, out_shape=(_sds((T, 3 * D), BF16), _sds((N_QKV_BLK, T, 128), BF16)),
               grid=(3, T // tm),
               in_specs=[pl.BlockSpec((tm, D), lambda kd, t: (t, kd)),
                         pl.BlockSpec((HALO, D), lambda kd, t: (jnp.maximum(t * (tm // HALO) - 1, 0), kd)),
                         pl.BlockSpec((4, D), lambda kd, t: (0, kd))],
               out_specs=(pl.BlockSpec((tm, D), lambda kd, t: (t, kd)),
                          pl.BlockSpec((HEADS_A, tm, 128), lambda kd, t: (kd, t, 0))),
               sem=("parallel", "parallel"))(proj, proj, wconv)


def gdn_conv_bwd(dqkv, c, proj, wconv, name):
    T = c.shape[0]
    tm = _tile(T, 512)
    n_t = T // tm

    def body(dy_ref, dyn_ref, c_ref, cn_ref, x_ref, xp_ref, w_ref, dx_ref, dw_ref):
        kind, t = pl.program_id(0), pl.program_id(1)
        scale = jnp.where(kind == 0, DK ** -0.5, 1.0)

        def act_bwd(dy, cv):
            sg = _sigmoid(cv)
            s = cv * sg
            parts = []
            for h in range(HEADS_A):
                sh, dyh = s[:, _head_cols(h)], dy[h]
                r = lax.rsqrt(jnp.sum(sh * sh, axis=-1, keepdims=True) + EPS)
                ds_norm = scale * r * (dyh - (r * r) * sh * jnp.sum(dyh * sh, axis=-1, keepdims=True))
                parts.append(jnp.where(kind < 2, ds_norm, dyh))
            return jnp.concatenate(parts, axis=1) * (sg * (1.0 + cv * (1.0 - sg)))

        w = w_ref[...]
        dcur = act_bwd(dy_ref[...].astype(F32), c_ref[...].astype(F32))
        dnext = jnp.where(t < n_t - 1, act_bwd(dyn_ref[...].astype(F32), cn_ref[...].astype(F32)), 0.0)
        dcat = jnp.concatenate([dcur, dnext], axis=0)
        dx = dcur * w[3:4]
        for k in range(3):
            dx = dx + pltpu.roll(dcat, tm + HALO - (3 - k), 0)[:tm] * w[k:k + 1]
        dx_ref[...] = dx.astype(BF16)
        xprev = jnp.where(t > 0, xp_ref[...].astype(F32), 0.0)
        xcat = jnp.concatenate([xprev, x_ref[...].astype(F32)], axis=0)
        rows = [jnp.sum(dcur * pltpu.roll(xcat, 3 - k, 0)[HALO:], axis=0, keepdims=True) for k in range(3)]
        rows.append(jnp.sum(dcur * xcat[HALO:], axis=0, keepdims=True))

        @pl.when(t == 0)
        def _():
            dw_ref[...] = jnp.zeros_like(dw_ref)
        dw_ref[...] += jnp.concatenate(rows, axis=0)

    def nxt(t):
        return jnp.minimum((t + 1) * (tm // HALO), T // HALO - 1)

    cur = pl.BlockSpec((tm, D), lambda kd, t: (t, kd))
    return _pc(body, name=name, out_shape=(_sds((T, 3 * D), BF16), _sds((4, 3 * D), F32)), grid=(3, n_t),
               in_specs=[pl.BlockSpec((HEADS_A, tm, 128), lambda kd, t: (kd, t, 0)),
                         pl.BlockSpec((HEADS_A, HALO, 128), lambda kd, t: (kd, nxt(t), 0)),
                         cur, pl.BlockSpec((HALO, D), lambda kd, t: (nxt(t), kd)),
                         cur, pl.BlockSpec((HALO, D), lambda kd, t: (jnp.maximum(t * (tm // HALO) - 1, 0), kd)),
                         pl.BlockSpec((4, D), lambda kd, t: (0, kd))],
               out_specs=(cur, pl.BlockSpec((4, D), lambda kd, t: (0, kd))),
               sem=("parallel", "arbitrary"))(dqkv, dqkv, c, c, proj, proj, wconv)


def _chunk_masks(n):
    ri = lax.broadcasted_iota(jnp.int32, (n, n), 0)
    ci = lax.broadcasted_iota(jnp.int32, (n, n), 1)
    same = (ri // CHUNK) == (ci // CHUNK)
    return same & (ri >= ci), same & (ri <= ci)


def gdn_gate_fwd(ba, al, dtb, name):
    T = ba.shape[0]
    tg = _tile(T, PREP_T)

    def body(ba_ref, al_ref, dtb_ref, o_ref):
        x = ba_ref[...]
        lane = lax.broadcasted_iota(jnp.int32, x.shape, 1)
        is_a = (lane >= HEADS_A) & (lane < 2 * HEADS_A)
        g = jnp.where(is_a, -jnp.exp(al_ref[...]) * _softplus(x + dtb_ref[...]), 0.0)
        lower, _ = _chunk_masks(tg)
        gc = _dot(lower.astype(F32), g, precision=HI)
        o_ref[...] = jnp.where(lane < HEADS_A, _sigmoid(x), gc)

    vec = pl.BlockSpec((1, 128), lambda i: (0, 0))
    return _pc(body, name=name, out_shape=_sds((T, 128), F32), grid=(T // tg,),
               in_specs=[pl.BlockSpec((tg, 128), lambda i: (i, 0)), vec, vec],
               out_specs=pl.BlockSpec((tg, 128), lambda i: (i, 0)), sem=("parallel",))(ba, al, dtb)


def gdn_gate_bwd(ba, al, dtb, dgb, name):
    T = ba.shape[0]
    tg = _tile(T, PREP_T)

    def body(ba_ref, al_ref, dtb_ref, dgb_ref, dba_ref, dal_ref, ddt_ref):
        x, d = ba_ref[...], dgb_ref[...]
        lane = lax.broadcasted_iota(jnp.int32, x.shape, 1)
        is_b = lane < HEADS_A
        is_a = (lane >= HEADS_A) & (lane < 2 * HEADS_A)
        beta = _sigmoid(x)
        e_a = jnp.exp(al_ref[...])
        z = x + dtb_ref[...]
        g = jnp.where(is_a, -e_a * _softplus(z), 0.0)
        _, upper = _chunk_masks(tg)
        dg = _dot(upper.astype(F32), jnp.where(is_a, d, 0.0), precision=HI)
        da = jnp.where(is_a, dg * (-e_a) * _sigmoid(z), 0.0)
        db = jnp.where(is_b, d * beta * (1.0 - beta), 0.0)
        dba_ref[...] = (da + db).astype(BF16)

        @pl.when(pl.program_id(0) == 0)
        def _():
            dal_ref[...] = jnp.zeros_like(dal_ref)
            ddt_ref[...] = jnp.zeros_like(ddt_ref)
        dal_ref[...] += jnp.sum(dg * g, axis=0, keepdims=True)
        ddt_ref[...] += jnp.sum(da, axis=0, keepdims=True)

    vec = pl.BlockSpec((1, 128), lambda i: (0, 0))
    blk = pl.BlockSpec((tg, 128), lambda i: (i, 0))
    return _pc(body, name=name, out_shape=(_sds((T, 128), BF16), _sds((1, 128), F32), _sds((1, 128), F32)),
               grid=(T // tg,), in_specs=[blk, vec, vec, blk],
               out_specs=(blk, vec, vec), sem=("arbitrary",))(ba, al, dtb, dgb)


def _bmm(a, b, dims, precision=None):
    return lax.dot_general(a, b, dims, preferred_element_type=F32, precision=precision)


B_NN = (((2,), (1,)), ((0,), (0,)))
B_NT = (((2,), (2,)), ((0,), (0,)))


def _select_lane(x, lane_index):
    lane = lax.broadcasted_iota(jnp.int32, x.shape, x.ndim - 1)
    return jnp.sum(jnp.where(lane == lane_index, x, 0.0), axis=-1, keepdims=True)


B_TN = (((1,), (1,)), ((0,), (0,)))


def _bmm_split(a, b, dims):
    ah, bh = a.astype(BF16), b.astype(BF16)
    al, bl = (a - ah.astype(F32)).astype(BF16), (b - bh.astype(F32)).astype(BF16)
    return _bmm(ah, bh, dims) + (_bmm(ah, bl, dims) + _bmm(al, bh, dims))


@jax.custom_vjp
def _bmm_f32(a, b):
    return _bmm_split(a, b, B_NN)


def _bmm_f32_fwd(a, b):
    return _bmm_split(a, b, B_NN), (a, b)


def _bmm_bf16(a, b, dims):
    return _bmm(a.astype(BF16), b.astype(BF16), dims)


def _bmm_f32_bwd(res, dc):
    a, b = res
    return _bmm_bf16(dc, b, B_NT), _bmm_bf16(a, dc, B_TN)


_bmm_f32.defvjp(_bmm_f32_fwd, _bmm_f32_bwd)


def _tri_inverse(lmat):
    ri = lax.broadcasted_iota(jnp.int32, lmat.shape, 1)
    ci = lax.broadcasted_iota(jnp.int32, lmat.shape, 2)
    eye = jnp.where(ri == ci, 1.0, 0.0)
    inv = eye - lmat
    power = lmat
    for _ in range(5):
        power = _bmm_bf16(power, power, B_NN)
        inv = inv + _bmm_bf16(inv, power, B_NN)
    return _bmm_split(inv, 2.0 * eye - _bmm_split(eye + lmat, inv, B_NN), B_NN)


def _stored_inverse(x):
    @jax.custom_vjp
    def inverse(lmat):
        return x

    def fwd(lmat):
        return x, None

    def bwd(_, dx):
        return (-_bmm_bf16(_bmm_bf16(x, dx, B_TN), x, B_NT),)

    inverse.defvjp(fwd, bwd)
    return inverse


def _gdn_prep(q, k, v, gb, h, inverse):
    nb = q.shape[0]
    beta = _select_lane(gb, h)
    gc = _select_lane(gb, HEADS_A + h)
    ri = lax.broadcasted_iota(jnp.int32, (nb, CHUNK, CHUNK), 1)
    ci = lax.broadcasted_iota(jnp.int32, (nb, CHUNK, CHUNK), 2)
    lower, strict, eye = ri >= ci, ri > ci, ri == ci
    gcol = jnp.broadcast_to(gc, (nb, CHUNK, CHUNK))
    grow = jnp.swapaxes(gcol, 1, 2)
    decay = jnp.where(lower, jnp.exp(jnp.where(lower, gcol - grow, 0.0)), 0.0)
    kb = k * beta
    kbf = k.astype(BF16)
    inv = inverse(jnp.where(strict, _bmm(kb.astype(BF16), kbf, B_NT) * decay, 0.0))
    eg = jnp.exp(gc)
    sol = _bmm_f32(inv, jnp.concatenate([v * beta, kb * eg], axis=-1))
    aqk = _bmm(q.astype(BF16), kbf, B_NT) * decay
    g_last = gc[:, CHUNK - 1:CHUNK, :]
    gl = jnp.broadcast_to(jnp.exp(g_last), (nb, 1, 128))
    return (sol[..., :DK], sol[..., DK:], q * eg, k * jnp.exp(g_last - gc), aqk, gl), inv


def gdn_prep_fwd(qkv, gb, name):
    T = qkv.shape[1]
    tp = _tile(T, PREP_T)
    nb = tp // CHUNK

    def body(q_ref, k_ref, v_ref, gb_ref, u_ref, w_ref, qd_ref, kd_ref, a_ref, gl_ref, inv_ref):
        h = pl.program_id(1)
        shp = (nb, CHUNK, 128)
        q, k, v = (ref[0].astype(F32).reshape(shp) for ref in (q_ref, k_ref, v_ref))
        (u, w, qd, kd, aqk, gl), inv = _gdn_prep(q, k, v, gb_ref[...].reshape(shp), h, _tri_inverse)
        u_ref[0] = u.reshape(tp, 128)
        w_ref[0] = w.reshape(tp, 128).astype(BF16)
        qd_ref[0] = qd.reshape(tp, 128).astype(BF16)
        kd_ref[0] = kd.reshape(tp, 128).astype(BF16)
        a_ref[0] = aqk.reshape(tp, CHUNK).astype(BF16)
        gl_ref[0] = gl.reshape(nb, 1, 128)
        inv_ref[0] = inv.reshape(tp, CHUNK)

    def head(off):
        return pl.BlockSpec((1, tp, 128), lambda n, h: (h + off, n, 0))

    matmul_only = _sds((HEADS_A, T, 128), BF16)
    narrow = pl.BlockSpec((1, tp, CHUNK), lambda n, h: (h, n, 0))
    return _pc(body, name=name,
               out_shape=(_sds((HEADS_A, T, 128), F32), matmul_only, matmul_only, matmul_only, _sds((HEADS_A, T, CHUNK), BF16),
                          _sds((HEADS_A, T // CHUNK, 1, 128), F32), _sds((HEADS_A, T, CHUNK), F32)),
               grid=(T // tp, HEADS_A),
               in_specs=[head(0), head(HEADS_A), head(2 * HEADS_A), pl.BlockSpec((tp, 128), lambda n, h: (n, 0))],
               out_specs=(head(0), head(0), head(0), head(0), narrow,
                          pl.BlockSpec((1, nb, 1, 128), lambda n, h: (h, n, 0, 0)), narrow),
               sem=("parallel", "parallel"))(qkv, qkv, qkv, gb)


def gdn_prep_bwd(qkv, gb, inv, du, dw, dqd, dkd, da, dgl, name):
    T = qkv.shape[1]
    tp = _tile(T, PREP_T)
    nb = tp // CHUNK

    def body(q_ref, k_ref, v_ref, gb_ref, inv_ref, du_ref, dw_ref, dqd_ref, dkd_ref, da_ref, dgl_ref, dqkv_ref, dgb_ref):
        h = pl.program_id(1)
        shp = (nb, CHUNK, 128)
        stored = _stored_inverse(inv_ref[0].reshape(nb, CHUNK, CHUNK))
        q, k, v = (ref[0].astype(F32).reshape(shp) for ref in (q_ref, k_ref, v_ref))
        _, vjp = jax.vjp(lambda q, k, v, gb: _gdn_prep(q, k, v, gb, h, stored)[0], q, k, v, gb_ref[...].reshape(shp))
        dq, dk, dv, dgb = vjp((du_ref[0].reshape(shp), dw_ref[0].reshape(shp), dqd_ref[0].reshape(shp),
                               dkd_ref[0].reshape(shp), da_ref[0].reshape(nb, CHUNK, CHUNK), dgl_ref[0].reshape(nb, 1, 128)))
        dqkv_ref[h] = dq.reshape(tp, 128).astype(BF16)
        dqkv_ref[HEADS_A + h] = dk.reshape(tp, 128).astype(BF16)
        dqkv_ref[2 * HEADS_A + h] = dv.reshape(tp, 128).astype(BF16)

        @pl.when(h == 0)
        def _():
            dgb_ref[...] = jnp.zeros_like(dgb_ref)
        dgb_ref[...] += dgb.reshape(tp, 128)

    def head(off):
        return pl.BlockSpec((1, tp, 128), lambda n, h: (h + off, n, 0))

    narrow = pl.BlockSpec((1, tp, CHUNK), lambda n, h: (h, n, 0))
    return _pc(body, name=name, out_shape=(_sds((N_QKV_BLK, T, 128), BF16), _sds((T, 128), F32)),
               grid=(T // tp, HEADS_A),
               in_specs=[head(0), head(HEADS_A), head(2 * HEADS_A), pl.BlockSpec((tp, 128), lambda n, h: (n, 0)), narrow,
                         head(0), head(0), head(0), head(0), narrow,
                         pl.BlockSpec((1, nb, 1, 128), lambda n, h: (h, n, 0, 0))],
               out_specs=(pl.BlockSpec((N_QKV_BLK, tp, 128), lambda n, h: (0, n, 0)),
                          pl.BlockSpec((tp, 128), lambda n, h: (n, 0))),
               sem=("parallel", "arbitrary"))(qkv, qkv, qkv, gb, inv, du, dw, dqd, dkd, da, dgl)


def gdn_scan_fwd(u, w, qd, kd, aqk, gl, name):
    T = u.shape[1]
    n_chunks = T // CHUNK

    def body(u_ref, w_ref, qd_ref, kd_ref, a_ref, gl_ref, o_ref, sin_ref, state):
        @pl.when(pl.program_id(0) == 0)
        def _():
            state[...] = jnp.zeros_like(state)
        s = state[...]
        sb = s.astype(BF16)
        sin_ref[0] = sb
        both = _bmm(jnp.concatenate([w_ref[...], qd_ref[...]], axis=1).astype(BF16), sb, B_NN)
        vn = (u_ref[...] - both[:, :CHUNK]).astype(BF16)
        o_ref[...] = both[:, CHUNK:] + _bmm(a_ref[...].astype(BF16), vn, B_NN)
        state[...] = s * gl_ref[:, 0] + _bmm(kd_ref[...].astype(BF16), vn, B_TN)

    blk = pl.BlockSpec((HEADS_A, CHUNK, 128), lambda n: (0, n, 0))
    return _pc(body, name=name,
               out_shape=(_sds((HEADS_A, T, 128), F32), _sds((n_chunks, HEADS_A, DK, 128), BF16)), grid=(n_chunks,),
               in_specs=[blk, blk, blk, blk, pl.BlockSpec((HEADS_A, CHUNK, CHUNK), lambda n: (0, n, 0)),
                         pl.BlockSpec((HEADS_A, 1, 1, 128), lambda n: (0, n, 0, 0))],
               out_specs=(blk, pl.BlockSpec((1, HEADS_A, DK, 128), lambda n: (n, 0, 0, 0))),
               scratch=[pltpu.VMEM((HEADS_A, DK, 128), F32)], sem=("arbitrary",))(u, w, qd, kd, aqk, gl)


def gdn_scan_bwd(u, w, qd, kd, aqk, gl, sin, do, name):
    T = u.shape[1]
    n_chunks = T // CHUNK

    def body(u_ref, w_ref, qd_ref, kd_ref, a_ref, gl_ref, sin_ref, do_ref,
             du_ref, dw_ref, dqd_ref, dkd_ref, da_ref, dgl_ref, dstate):
        @pl.when(pl.program_id(0) == 0)
        def _():
            dstate[...] = jnp.zeros_like(dstate)
        lane0 = lax.broadcasted_iota(jnp.int32, (HEADS_A, 1, 128), 2) == 0
        sb = sin_ref[0]
        s = sb.astype(F32)
        wb, qdb, kdb = w_ref[...].astype(BF16), qd_ref[...].astype(BF16), kd_ref[...].astype(BF16)
        ab, dob = a_ref[...].astype(BF16), do_ref[...].astype(BF16)
        vn = (u_ref[...] - _bmm(wb, sb, B_NN)).astype(BF16)
        ds_out = dstate[...]
        dsb = ds_out.astype(BF16)
        dqd_ref[...] = _bmm(dob, sb, B_NT)
        da_ref[...] = _bmm(dob, vn, B_NT)
        dv = _bmm(ab, dob, B_TN) + _bmm(kdb, dsb, B_NN)
        dkd_ref[...] = _bmm(vn, dsb, B_NT)
        dgl = jnp.sum(jnp.sum(ds_out * s, axis=2, keepdims=True), axis=1, keepdims=True)
        dgl_ref[:, 0] = jnp.where(lane0, dgl, 0.0)
        du_ref[...] = dv
        dvb = dv.astype(BF16)
        dw_ref[...] = -_bmm(dvb, sb, B_NT)
        dstate[...] = ds_out * gl_ref[:, 0] + _bmm(qdb, dob, B_TN) - _bmm(wb, dvb, B_TN)

    last = n_chunks - 1
    blk = pl.BlockSpec((HEADS_A, CHUNK, 128), lambda n: (0, last - n, 0))
    ablk = pl.BlockSpec((HEADS_A, CHUNK, CHUNK), lambda n: (0, last - n, 0))
    glblk = pl.BlockSpec((HEADS_A, 1, 1, 128), lambda n: (0, last - n, 0, 0))
    per_head = _sds((HEADS_A, T, 128), F32)
    return _pc(body, name=name,
               out_shape=(per_head, per_head, per_head, per_head, _sds((HEADS_A, T, CHUNK), F32),
                          _sds((HEADS_A, n_chunks, 1, 128), F32)), grid=(n_chunks,),
               in_specs=[blk, blk, blk, blk, ablk, glblk,
                         pl.BlockSpec((1, HEADS_A, DK, 128), lambda n: (last - n, 0, 0, 0)), blk],
               out_specs=(blk, blk, blk, blk, ablk, glblk),
               scratch=[pltpu.VMEM((HEADS_A, DK, 128), F32)], sem=("arbitrary",))(u, w, qd, kd, aqk, gl, sin, do)


def gdn_outnorm_fwd(o, proj, wn, name):
    T = o.shape[1]
    tm = _tile(T, 512)

    def body(o_ref, z_ref, wn_ref, y_ref):
        for h in range(HEADS_A):
            z = z_ref[:, 128 * h:128 * (h + 1)].astype(F32)
            y_ref[:, 128 * h:128 * (h + 1)] = (_rms_fwd(o_ref[h], wn_ref[...]) * (z * _sigmoid(z))).astype(BF16)

    return _pc(body, name=name, out_shape=_sds((T, D), BF16), grid=(T // tm,),
               in_specs=[pl.BlockSpec((HEADS_A, tm, 128), lambda i: (0, i, 0)),
                         pl.BlockSpec((tm, D), lambda i: (i, Z_BLK0 * 128 // D)), pl.BlockSpec((1, 128), lambda i: (0, 0))],
               out_specs=pl.BlockSpec((tm, D), lambda i: (i, 0)), sem=("parallel",))(o, proj, wn)


def gdn_outnorm_bwd(o, proj, wn, dy, name):
    T = o.shape[1]
    tm = _tile(T, 512)

    def body(o_ref, z_ref, wn_ref, dy_ref, do_ref, dz_ref, dwn_ref):
        wn = wn_ref[...]
        acc = jnp.zeros((1, 128), F32)
        for h in range(HEADS_A):
            cols = slice(128 * h, 128 * (h + 1))
            z, dyh, ov = z_ref[:, cols].astype(F32), dy_ref[:, cols], o_ref[h]
            sg = _sigmoid(z)
            do, dwn = _rms_bwd(ov, wn, dyh * (z * sg))
            do_ref[h] = do
            acc = acc + dwn
            dz_ref[:, cols] = (dyh * _rms_fwd(ov, wn) * (sg * (1.0 + z * (1.0 - sg)))).astype(BF16)

        @pl.when(pl.program_id(0) == 0)
        def _():
            dwn_ref[...] = jnp.zeros_like(dwn_ref)
        dwn_ref[...] += acc

    row = pl.BlockSpec((tm, D), lambda i: (i, 0))
    vec = pl.BlockSpec((1, 128), lambda i: (0, 0))
    hblk = pl.BlockSpec((HEADS_A, tm, 128), lambda i: (0, i, 0))
    return _pc(body, name=name, out_shape=(_sds((HEADS_A, T, 128), F32), _sds((T, D), BF16), _sds((1, 128), F32)),
               grid=(T // tm,),
               in_specs=[hblk, pl.BlockSpec((tm, D), lambda i: (i, Z_BLK0 * 128 // D)), vec, row],
               out_specs=(hblk, row, vec), sem=("arbitrary",))(o, proj, wn, dy)


def gdn_forward(x, nw, w_in, wconv, al, dtb, wn, w_out, tag, deps=()):
    h = rmsnorm_bf16(x, nw, f"{tag}_norm", deps)
    proj = mm_nn(h, w_in, f"{tag}_proj", out_dtype=BF16, cols=(0, MAIN_COLS))
    ba = mm_nn(h, w_in, f"{tag}_proj_ba", cols=(MAIN_COLS, A_COLS))
    c, qkv = gdn_conv_fwd(proj, wconv, f"{tag}_conv")
    gb = gdn_gate_fwd(ba, al, dtb, f"{tag}_gate")
    u, w, qd, kd, aqk, gl, inv = gdn_prep_fwd(qkv, gb, f"{tag}_prep")
    o, sin = gdn_scan_fwd(u, w, qd, kd, aqk, gl, f"{tag}_scan")
    on = gdn_outnorm_fwd(o, proj, wn, f"{tag}_outnorm")
    y = mm_nn(on, w_out, f"{tag}_out", residual=x)
    return y, (x, h, proj, ba, c, qkv, gb, inv, (u, w, qd, kd, aqk, gl), sin, o, on)


def gdn_backward(dout, saved, nw, w_in, wconv, al, dtb, wn, w_out, tag):
    x, h, proj, ba, c, qkv, gb, inv, prep, sin, o, on = saved
    d_on = mm_nt(dout, w_out, f"{tag}_out_bwd")
    dw_out = mm_tn(on, dout, f"{tag}_out_wgrad")
    do, dz, dwn = gdn_outnorm_bwd(o, proj, wn, d_on, f"{tag}_outnorm_bwd")
    du, dw, dqd, dkd, da, dgl = gdn_scan_bwd(*prep, sin, do, f"{tag}_scan_bwd")
    dqkv, dgb = gdn_prep_bwd(qkv, gb, inv, du, dw, dqd, dkd, da, dgl, f"{tag}_prep_bwd")
    dba, dal, ddt = gdn_gate_bwd(ba, al, dtb, dgb, f"{tag}_gate_bwd")
    dpre, dwconv = gdn_conv_bwd(dqkv, c, proj, wconv, f"{tag}_conv_bwd")
    dproj = jnp.concatenate([dpre, dz, dba], axis=1)
    dw_in = mm_tn(h, dproj, f"{tag}_proj_wgrad")
    dh = mm_nt(dproj, w_in, f"{tag}_proj_bwd")
    dx, dnw = rmsnorm_bwd_add(x, nw, dh, dout, f"{tag}_norm_bwd")
    return dx, dnw, dw_in, dwconv, dal, ddt, dwn, dw_out


N_KV, GROUP = 4, 4
KV_COLS = 2 * N_KV * B_HD
B_COLS = D + KV_COLS


@jax.custom_vjp
def _swap_lane_halves(x):
    return pltpu.roll(x, 64, 1)


_swap_lane_halves.defvjp(lambda x: (pltpu.roll(x, 64, 1), None), lambda _, g: (pltpu.roll(g, 64, 1),))


def _swa_block(q, kp, kc, vp, vc, sk, first):
    cols = GROUP * B_BLK
    ks = lax.broadcasted_iota(jnp.int32, (N_KV, B_BLK, cols), 1)
    qi = lax.broadcasted_iota(jnp.int32, (N_KV, B_BLK, cols), 2) % B_BLK
    from_cur = ks <= qi

    def batch(parts):
        return jnp.concatenate([part[None] for part in parts], axis=0)

    def per_kv(cur, prev):
        return batch([jnp.concatenate([cur[:, j * B_HD:(j + 1) * B_HD], prev[:, j * B_HD:(j + 1) * B_HD]], axis=0)
                      for j in range(N_KV)]).astype(BF16)

    qs = batch([jnp.concatenate([q[:, hq * B_HD:(hq + 1) * B_HD] for hq in range(GROUP * j, GROUP * (j + 1))], axis=0)
                for j in range(N_KV)])
    q_t = jnp.swapaxes(qs, 1, 2).astype(BF16)
    sink = batch([jnp.concatenate([jnp.broadcast_to(sk[:, hq:hq + 1], (1, B_BLK))
                                   for hq in range(GROUP * j, GROUP * (j + 1))], axis=1) for j in range(N_KV)])
    both = _bmm(per_kv(kc, kp), q_t, B_NN)
    s = jnp.where(from_cur, both[:, :B_BLK], jnp.where(first, -1e30, both[:, B_BLK:])) * (B_HD ** -0.5)
    m = lax.stop_gradient(jnp.maximum(jnp.max(s, axis=1, keepdims=True), sink))
    e = jnp.exp((s - m).astype(BF16))
    den = jnp.sum(e.astype(F32), axis=1, keepdims=True) + jnp.exp(sink - m)
    p = e * (1.0 / den).astype(BF16)
    zero = jnp.zeros_like(p)
    p_both = jnp.concatenate([jnp.where(from_cur, p, zero), jnp.where(from_cur, zero, p)], axis=1)
    o = jnp.swapaxes(_bmm(per_kv(vc, vp), p_both, B_TN), 1, 2)
    return jnp.concatenate([o[j, g * B_BLK:(g + 1) * B_BLK] for j in range(N_KV) for g in range(GROUP)], axis=1)


def swa_core_fwd(proj, sk, name):
    T = proj.shape[0]
    half = N_KV * B_HD

    def body(q_ref, kvc_ref, kvp_ref, sk_ref, o_ref):
        kvc, kvp = kvc_ref[...], kvp_ref[...]
        o_ref[...] = _swa_block(q_ref[...], kvp[:, :half], kvc[:, :half], kvp[:, half:], kvc[:, half:], sk_ref[...],
                                pl.program_id(0) == 0).astype(BF16)

    return _pc(body, name=name, out_shape=_sds((T, D), BF16), grid=(T // B_BLK,),
               in_specs=[pl.BlockSpec((B_BLK, D), lambda n: (n, 0)),
                         pl.BlockSpec((B_BLK, KV_COLS), lambda n: (n, D // KV_COLS)),
                         pl.BlockSpec((B_BLK, KV_COLS), lambda n: (jnp.maximum(n - 1, 0), D // KV_COLS)),
                         pl.BlockSpec((1, 128), lambda n: (0, 0))],
               out_specs=pl.BlockSpec((B_BLK, D), lambda n: (n, 0)), sem=("parallel",))(proj, proj, proj, sk)


def swa_core_bwd(proj, sk, do, name):
    T = proj.shape[0]
    last = T // B_BLK - 1
    half = N_KV * B_HD

    def body(q_ref, kvc_ref, kvp_ref, sk_ref, do_ref, dproj_ref, dbias_ref, dsk_ref, carry):
        step = pl.program_id(0)
        first = step == last

        @pl.when(step == 0)
        def _():
            carry[...] = jnp.zeros_like(carry)
            dbias_ref[...] = jnp.zeros_like(dbias_ref)
            dsk_ref[...] = jnp.zeros_like(dsk_ref)
        kvc, kvp = kvc_ref[...], kvp_ref[...]
        _, vjp = jax.vjp(functools.partial(_swa_block, first=first), q_ref[...], kvp[:, :half], kvc[:, :half],
                         kvp[:, half:], kvc[:, half:], sk_ref[...])
        dq, dkp, dkc, dvp, dvc, dsk = vjp(do_ref[...])
        dkv = jnp.concatenate([dkc, dvc], axis=1) + carry[...]
        carry[...] = jnp.concatenate([dkp, dvp], axis=1)
        row = jnp.concatenate([dq, dkv], axis=1)
        dproj_ref[...] = row.astype(BF16)
        dbias_ref[...] += jnp.sum(row, axis=0, keepdims=True)
        dsk_ref[...] += dsk

    return _pc(body, name=name, out_shape=(_sds((T, B_COLS), BF16), _sds((1, B_COLS), F32), _sds((1, 128), F32)),
               grid=(T // B_BLK,),
               in_specs=[pl.BlockSpec((B_BLK, D), lambda n: (last - n, 0)),
                         pl.BlockSpec((B_BLK, KV_COLS), lambda n: (last - n, D // KV_COLS)),
                         pl.BlockSpec((B_BLK, KV_COLS), lambda n: (jnp.maximum(last - n - 1, 0), D // KV_COLS)),
                         pl.BlockSpec((1, 128), lambda n: (0, 0)), pl.BlockSpec((B_BLK, D), lambda n: (last - n, 0))],
               out_specs=(pl.BlockSpec((B_BLK, B_COLS), lambda n: (last - n, 0)),
                          pl.BlockSpec((1, B_COLS), lambda n: (0, 0)), pl.BlockSpec((1, 128), lambda n: (0, 0))),
               scratch=[pltpu.VMEM((B_BLK, KV_COLS), F32)], sem=("arbitrary",))(proj, proj, proj, sk, do)


def col_sum(a, name):
    T, N = a.shape
    tm = _tile(T, 1024)

    def body(a_ref, o_ref):
        @pl.when(pl.program_id(0) == 0)
        def _():
            o_ref[...] = jnp.zeros_like(o_ref)
        o_ref[...] += jnp.sum(a_ref[...].astype(F32), axis=0, keepdims=True)

    return _pc(body, name=name, out_shape=_sds((1, N), F32), grid=(T // tm,),
               in_specs=[pl.BlockSpec((tm, N), lambda i: (i, 0))], out_specs=pl.BlockSpec((1, N), lambda i: (0, 0)),
               sem=("arbitrary",))(a)


def swa_forward(x, nw, w_in, b_in, sk, w_out, b_out, tag):
    h = rmsnorm_bf16(x, nw, f"{tag}_norm")
    proj = mm_nn(h, w_in, f"{tag}_proj", bias=b_in)
    o = swa_core_fwd(proj, sk, f"{tag}_core")
    y = mm_nn(o, w_out, f"{tag}_out", bias=b_out, residual=x)
    return y, (x, h, proj, o)


def swa_backward(dout, saved, nw, w_in, b_in, sk, w_out, b_out, tag):
    x, h, proj, o = saved
    do = mm_nt(dout, w_out, f"{tag}_out_bwd")
    dw_out = mm_tn(o, dout, f"{tag}_out_wgrad")
    db_out = col_sum(dout, f"{tag}_out_bias_grad")
    dproj, db_in, dsk = swa_core_bwd(proj, sk, do, f"{tag}_core_bwd")
    dw_in = mm_tn(h, dproj, f"{tag}_proj_wgrad")
    dh = mm_nt(dproj, w_in, f"{tag}_proj_bwd")
    dx, dnw = rmsnorm_bwd_add(x, nw, dh, dout, f"{tag}_norm_bwd")
    return dx, dnw, dw_in, db_in, dsk, dw_out, db_out


MESH = pl.DeviceIdType.MESH


def _position():
    return lax.axis_index("x"), lax.axis_index("y"), lax.axis_index("c")


def _slot(x, y, c):
    return 4 * x + 2 * y + c


def _peer(x, y, c, k):
    return (1 - x if k & 4 else x, 1 - y if k & 2 else y, 1 - c if k & 1 else c)


HBM_SPEC = pl.BlockSpec(memory_space=pltpu.HBM)
SEM_SPEC = pl.BlockSpec(memory_space=pltpu.SEMAPHORE)
DEP_SPEC = pl.BlockSpec(memory_space=pl.ANY)
SIDE_EFFECT = pltpu.SideEffectType.DATAFLOW_SIDE_EFFECTING
N_PEERS = N_DEV - 1


def _push_copies(srcs, lands, send_sems, recv_sems, scatter):
    x, y, c = _position()
    me = _slot(x, y, c)
    copies = []
    for k in (1, 2, 4, 3, 5, 6, 7):
        peer = _peer(x, y, c, k)
        for a in range(len(srcs)):
            copies.append(pltpu.make_async_remote_copy(
                src_ref=srcs[a].at[_slot(*peer)] if scatter else srcs[a], dst_ref=lands[a].at[me],
                send_sem=send_sems.at[N_PEERS * a + k - 1], recv_sem=recv_sems.at[N_PEERS * a + k - 1],
                device_id=peer, device_id_type=MESH))
    return copies


def push_start(srcs, lands, name, scatter, deps=()):
    n = len(srcs)
    first_out = 2 * n + len(deps)

    def body(*refs):
        for cp in _push_copies(refs[:n], refs[n:2 * n], refs[first_out], refs[first_out + 1], scatter):
            cp.start()
        refs[-1][...] = jnp.zeros_like(refs[-1])

    passed = [pltpu.HBM(t.shape, t.dtype) for t in list(srcs) + list(lands)]
    res = pl.pallas_call(
        body, name=name,
        out_shape=(pltpu.SemaphoreType.DMA((N_PEERS * n,)), pltpu.SemaphoreType.DMA((N_PEERS * n,)), *passed, _sds((8, 128), F32)),
        in_specs=[HBM_SPEC] * (2 * n) + [DEP_SPEC] * len(deps),
        out_specs=(SEM_SPEC, SEM_SPEC, *([HBM_SPEC] * (2 * n)), pl.BlockSpec(memory_space=pltpu.VMEM)),
        input_output_aliases={i: 2 + i for i in range(2 * n)},
        compiler_params=pltpu.CompilerParams(has_side_effects=SIDE_EFFECT),
    )(*[pltpu.with_memory_space_constraint(t, pltpu.HBM) for t in list(srcs) + list(lands)], *deps)
    return (res[0], res[1], list(res[2:2 + n]), list(res[2 + n:2 + 2 * n])), res[-1]


def push_wait(handles, after, name, scatter):
    send_sems, recv_sems, srcs, lands = handles
    n = len(srcs)
    after = tuple(after) if isinstance(after, (tuple, list)) else (after,)

    def body(*refs):
        for cp in _push_copies(refs[:n], refs[n:2 * n], refs[2 * n], refs[2 * n + 1], scatter):
            cp.wait_send()
            cp.wait_recv()

    res = pl.pallas_call(
        body, name=name, out_shape=tuple(pltpu.HBM(t.shape, t.dtype) for t in srcs + lands),
        in_specs=[HBM_SPEC] * (2 * n) + [SEM_SPEC, SEM_SPEC] + [DEP_SPEC] * len(after), out_specs=tuple([HBM_SPEC] * (2 * n)),
        input_output_aliases={i: i for i in range(2 * n)},
        compiler_params=pltpu.CompilerParams(has_side_effects=SIDE_EFFECT),
    )(*srcs, *lands, send_sems, recv_sems, *after)
    return list(res[n:])


def gather_start(shards, name, deps=()):
    me = _slot(*_position())
    lands = [lax.dynamic_update_slice(lax.empty((N_DEV,) + t.shape, t.dtype), t[None], (me,) + (0,) * t.ndim) for t in shards]
    return push_start(shards, lands, name, scatter=False, deps=deps)


def exchange_start(parts, name):
    me = _slot(*_position())
    lands = [lax.dynamic_update_slice(lax.empty(t.shape, t.dtype), lax.dynamic_index_in_dim(t, me, 0, keepdims=True),
                                      (me,) + (0,) * (t.ndim - 1)) for t in parts]
    return push_start(parts, lands, name, scatter=True)


def _row_tile(rows, cols):
    best = rows
    for t in range(16, rows, 16):
        if rows % t == 0 and t * cols * 4 <= (1 << 20):
            best = t
    return best


def adam_update(parts, w, m, v, name):
    n_layers = len(parts)
    P, R, C = parts[0].shape
    tr = _row_tile(R, C)
    n_t = R // tr

    def body(*refs):
        p_refs = refs[:n_layers]
        w_ref, m_ref, v_ref, g_ref, d_ref, nm_ref, nv_ref = refs[n_layers:]
        for layer in range(n_layers):
            @pl.when(pl.program_id(0) == layer)
            def _(p_ref=p_refs[layer]):
                g = p_ref[0].astype(F32)
                for s in range(1, P):
                    g = g + p_ref[s].astype(F32)
                new_m = ADAM_B1 * m_ref[0] + (1.0 - ADAM_B1) * g
                new_v = ADAM_B2 * v_ref[0] + (1.0 - ADAM_B2) * (g * g)
                m_hat = new_m / (1.0 - ADAM_B1 ** ADAM_STEP)
                v_hat = new_v / (1.0 - ADAM_B2 ** ADAM_STEP)
                g_ref[0] = g
                d_ref[0] = -ADAM_LR * (m_hat / (jnp.sqrt(v_hat) + ADAM_EPS) + ADAM_WD * w_ref[0])
                nm_ref[0] = new_m
                nv_ref[0] = new_v

    def part_spec(layer):
        return pl.BlockSpec((P, tr, C), lambda l_, i: (0, jnp.where(l_ == layer, i, jnp.where(l_ < layer, 0, n_t - 1)), 0))

    blk = pl.BlockSpec((1, tr, C), lambda l_, i: (l_, i, 0))
    out = _sds((n_layers, R, C), F32)
    return _pc(body, name=name, out_shape=(out, out, out, out), grid=(n_layers, n_t),
               in_specs=[part_spec(layer) for layer in range(n_layers)] + [blk, blk, blk],
               out_specs=(blk, blk, blk, blk), sem=("arbitrary", "arbitrary"))(*parts, w, m, v)


WEIGHTS = ("ffn1_norm", "ffn1_w_gu", "ffn1_w_down", "mix_norm", "ffn2_norm", "ffn2_w_gu", "ffn2_w_down", "a_w_in",
           "a_w_conv", "a_A_log", "a_dt_bias", "a_out_norm", "a_w_out", "b_w_in", "b_b_in", "b_sinks", "b_w_out",
           "b_b_out", "final_norm")
SHARDED = ("ffn1_w_gu", "ffn1_w_down", "ffn2_w_gu", "ffn2_w_down", "a_w_in", "a_w_conv", "a_w_out", "b_w_in", "b_b_in",
           "b_w_out", "b_b_out")
MISC_LANES = dict(a_A_log=(0, 8), a_dt_bias=(8, 16), b_sinks=(16, 32), a_out_norm=(128, 256))
LOSS_LANE = 256


def _pack_small(t):
    misc = jnp.zeros((D,), F32)
    for key, (lo, hi) in MISC_LANES.items():
        misc = misc.at[lo:hi].set(t[key].reshape(-1))
    if "loss" in t:
        misc = misc.at[LOSS_LANE].set(t["loss"])
    return jnp.concatenate([t["ffn1_norm"], t["mix_norm"], t["ffn2_norm"], t["final_norm"].reshape(1, D), misc[None]], axis=0)


def _unpack_small(p, like):
    out = dict(ffn1_norm=p[0:2], mix_norm=p[2:4], ffn2_norm=p[4:6], final_norm=p[6])
    for key, (lo, hi) in MISC_LANES.items():
        out[key] = p[7, lo:hi].reshape(like[key].shape)
    return out


def kernel(x, ffn1_norm, ffn1_w_gu, ffn1_w_down, mix_norm, ffn2_norm, ffn2_w_gu, ffn2_w_down, a_w_in, a_w_conv, a_A_log, a_dt_bias, a_out_norm, a_w_out, b_w_in, b_b_in, b_sinks, b_w_out, b_b_out, final_norm, loss_target, m_ffn1_norm, m_ffn1_w_gu, m_ffn1_w_down, m_mix_norm, m_ffn2_norm, m_ffn2_w_gu, m_ffn2_w_down, m_a_w_in, m_a_w_conv, m_a_A_log, m_a_dt_bias, m_a_out_norm, m_a_w_out, m_b_w_in, m_b_b_in, m_b_sinks, m_b_w_out, m_b_b_out, m_final_norm, v_ffn1_norm, v_ffn1_w_gu, v_ffn1_w_down, v_mix_norm, v_ffn2_norm, v_ffn2_w_gu, v_ffn2_w_down, v_a_w_in, v_a_w_conv, v_a_A_log, v_a_dt_bias, v_a_out_norm, v_a_w_out, v_b_w_in, v_b_b_in, v_b_sinks, v_b_w_out, v_b_b_out, v_final_norm):
    w = dict(ffn1_norm=ffn1_norm, ffn1_w_gu=ffn1_w_gu, ffn1_w_down=ffn1_w_down, mix_norm=mix_norm, ffn2_norm=ffn2_norm, ffn2_w_gu=ffn2_w_gu, ffn2_w_down=ffn2_w_down, a_w_in=a_w_in, a_w_conv=a_w_conv, a_A_log=a_A_log, a_dt_bias=a_dt_bias, a_out_norm=a_out_norm, a_w_out=a_w_out, b_w_in=b_w_in, b_b_in=b_b_in, b_sinks=b_sinks, b_w_out=b_w_out, b_b_out=b_b_out, final_norm=final_norm)
    m = dict(ffn1_norm=m_ffn1_norm, ffn1_w_gu=m_ffn1_w_gu, ffn1_w_down=m_ffn1_w_down, mix_norm=m_mix_norm, ffn2_norm=m_ffn2_norm, ffn2_w_gu=m_ffn2_w_gu, ffn2_w_down=m_ffn2_w_down, a_w_in=m_a_w_in, a_w_conv=m_a_w_conv, a_A_log=m_a_A_log, a_dt_bias=m_a_dt_bias, a_out_norm=m_a_out_norm, a_w_out=m_a_w_out, b_w_in=m_b_w_in, b_b_in=m_b_b_in, b_sinks=m_b_sinks, b_w_out=m_b_w_out, b_b_out=m_b_b_out, final_norm=m_final_norm)
    v = dict(ffn1_norm=v_ffn1_norm, ffn1_w_gu=v_ffn1_w_gu, ffn1_w_down=v_ffn1_w_down, mix_norm=v_mix_norm, ffn2_norm=v_ffn2_norm, ffn2_w_gu=v_ffn2_w_gu, ffn2_w_down=v_ffn2_w_down, a_w_in=v_a_w_in, a_w_conv=v_a_w_conv, a_A_log=v_a_A_log, a_dt_bias=v_a_dt_bias, a_out_norm=v_a_out_norm, a_w_out=v_a_w_out, b_w_in=v_b_w_in, b_b_in=v_b_b_in, b_sinks=v_b_sinks, b_w_out=v_b_w_out, b_b_out=v_b_b_out, final_norm=v_final_norm)
    T = x.shape[1]
    x0, tgt = x.reshape(T, D), loss_target.reshape(T, D)

    def cast(t):
        return t.astype(BF16)

    h0, t0 = gather_start([cast(ffn1_w_gu[0])], "gather0_start")
    a_log_row = jnp.zeros((1, 128), F32).at[0, HEADS_A:2 * HEADS_A].set(a_A_log[0])
    dt_row = jnp.zeros((1, 128), F32).at[0, HEADS_A:2 * HEADS_A].set(a_dt_bias[0])
    sink_row = jnp.zeros((1, 128), F32).at[0, :b_sinks.shape[1]].set(b_sinks[0])
    a_in_cols = a_w_in.shape[-1] * N_DEV

    def down_blocks(t):
        return t.reshape(N_FB, FB, D)

    wgu, wdn, saved = {}, {}, []
    xn = rmsnorm_bf16(x0, ffn1_norm[0:1], "l0_ffn1_norm", (t0,))
    wgu["ffn1", 0] = push_wait(h0, xn, "gather0_wait", scatter=False)[0]
    h0d, t0d = gather_start([cast(ffn1_w_down[0])], "gather0d_start", deps=(wgu["ffn1", 0],))
    h1, t1 = gather_start([cast(a_w_in[0]), a_w_conv[0], cast(a_w_out[0])], "gather1_start", deps=(t0d,))
    gu = ffn_up(xn, wgu["ffn1", 0], "l0_ffn1_up", deps=(t0d, t1))
    wdn["ffn1", 0] = down_blocks(push_wait(h0d, gu, "gather0d_wait", scatter=False)[0])
    xs, s1 = ffn_down(gu, wdn["ffn1", 0], x0, "l0_ffn1_down"), (x0, xn, gu)
    got = push_wait(h1, xs, "gather1_wait", scatter=False)
    h1f, t1f = gather_start([cast(ffn2_w_gu[0]), cast(ffn2_w_down[0])], "gather1f_start", deps=(got[0],))
    g2 = [cast(ffn1_w_gu[1]), cast(ffn1_w_down[1]), cast(b_w_in[0]), b_b_in, cast(b_w_out[0]), b_b_out,
          cast(ffn2_w_gu[1]), cast(ffn2_w_down[1])]
    h2, t2 = gather_start(g2, "gather2_start", deps=(t1f,))
    a_in_full = jnp.pad(got[0].transpose(1, 0, 2).reshape(D, a_in_cols), ((0, 0), (0, A_COLS - a_in_cols)))
    gdn_args = (mix_norm[0:1], a_in_full, got[1].transpose(1, 0, 2).reshape(4, 3 * D), a_log_row, dt_row, a_out_norm,
                got[2].reshape(D, D))
    xs, sm = gdn_forward(xs, *gdn_args, "gdn", deps=(t1f, t2))
    got = push_wait(h1f, xs, "gather1f_wait", scatter=False)
    wgu["ffn2", 0], wdn["ffn2", 0] = got[0], down_blocks(got[1])
    xs, s2 = ffn_forward(xs, ffn2_norm[0:1], wgu["ffn2", 0], wdn["ffn2", 0], "l0_ffn2")
    saved.append((s1, sm, s2))
    got = push_wait(h2, xs, "gather2_wait", scatter=False)
    wgu["ffn1", 1], wdn["ffn1", 1] = got[0], down_blocks(got[1])
    swa_args = (mix_norm[1:2], got[2].transpose(1, 0, 2).reshape(D, B_COLS), got[3].reshape(1, B_COLS), sink_row,
                got[4].reshape(D, D), got[5].reshape(1, D))
    wgu["ffn2", 1], wdn["ffn2", 1] = got[6], down_blocks(got[7])
    xs, s1 = ffn_forward(xs, ffn1_norm[1:2], wgu["ffn1", 1], wdn["ffn1", 1], "l1_ffn1")
    xs, sm = swa_forward(xs, *swa_args, "swa")
    xs, s2 = ffn_forward(xs, ffn2_norm[1:2], wgu["ffn2", 1], wdn["ffn2", 1], "l1_ffn2")
    saved.append((s1, sm, s2))
    loss_row, dx, d_final_norm = final_loss(xs, final_norm.reshape(1, D), tgt, "final_loss")

    def down_slots(t):
        return cast(t.reshape(N_DEV, FB // 2, D))

    def col_slots(t, dtype=BF16):
        return t.reshape(t.shape[0], N_DEV, -1).transpose(1, 0, 2).astype(dtype)

    d_norm = {"ffn1_norm": [None, None], "mix_norm": [None, None], "ffn2_norm": [None, None]}
    s1, sm, s2 = saved[1]
    dx, d_norm["ffn2_norm"][1], d_gu, d_dn = ffn_backward(dx, s2, ffn2_norm[1:2], wgu["ffn2", 1], wdn["ffn2", 1], "l1_ffn2")
    sent1 = [cast(d_gu), down_slots(d_dn)]
    dx, d_norm["mix_norm"][1], d_b_in, d_b_bias_in, d_sinks, d_b_out, d_b_bias_out = swa_backward(dx, sm, *swa_args, "swa")
    sent1 += [col_slots(d_b_in), d_b_bias_in.reshape(N_DEV, 1, -1), cast(d_b_out.reshape(N_DEV, D // N_DEV, D)),
              d_b_bias_out.reshape(N_DEV, 1, -1)]
    dx, d_norm["ffn1_norm"][1], d_gu, d_dn = ffn_backward(dx, s1, ffn1_norm[1:2], wgu["ffn1", 1], wdn["ffn1", 1], "l1_ffn1")
    sent1 += [cast(d_gu), down_slots(d_dn)]
    x1, tx1 = exchange_start(sent1, "exchange1_start")

    s1, sm, s2 = saved[0]
    dx, d_norm["ffn2_norm"][0], d_gu, d_dn = ffn_backward(dx, s2, ffn2_norm[0:1], wgu["ffn2", 0], wdn["ffn2", 0], "l0_ffn2",
                                                           deps=(tx1,))
    sent2 = [cast(d_gu), down_slots(d_dn)]
    dx, d_norm["mix_norm"][0], d_a_in, d_a_conv, d_alog, d_dt, d_onorm, d_a_out = gdn_backward(dx, sm, *gdn_args, "gdn")
    sent2 += [col_slots(d_a_in[:, :a_in_cols]), col_slots(d_a_conv, F32), cast(d_a_out.reshape(N_DEV, D // N_DEV, D))]
    x2, tx2 = exchange_start(sent2, "exchange2_start")
    last = {}

    def send_last(d_gu, d_dn):
        last["handles"], token = exchange_start([cast(d_gu), down_slots(d_dn)], "exchange3_start")
        return (token,)

    dx, d_norm["ffn1_norm"][0], _, _ = ffn_backward(dx, s1, ffn1_norm[0:1], wgu["ffn1", 0], wdn["ffn1", 0], "l0_ffn1",
                                                    deps=(tx2,), on_grads=send_last)
    grad_x = dx.reshape(x.shape)
    r1 = push_wait(x1, dx, "exchange1_wait", scatter=True)
    r2 = push_wait(x2, dx, "exchange2_wait", scatter=True)
    received = dict(ffn2_w_gu=[r2[0], r1[0]], ffn2_w_down=[r2[1], r1[1]],
                    b_w_in=[r1[2]], b_b_in=[r1[3]], b_w_out=[r1[4]], b_b_out=[r1[5]],
                    a_w_in=[r2[2]], a_w_conv=[r2[3]], a_w_out=[r2[4]])

    grads, deltas, new_m, new_v = {}, {}, {}, {}

    def update(key):
        shape = w[key].shape
        cols = shape[-1]
        layers = lambda t: t.reshape(shape[0], -1, cols)
        out = adam_update([r.reshape(N_DEV, -1, cols) for r in received[key]], layers(w[key]), layers(m[key]), layers(v[key]),
                          f"adam_{key}")
        grads[key], deltas[key], new_m[key], new_v[key] = (t.reshape(shape) for t in out)

    for key in SHARDED:
        if key in received:
            update(key)
    done_first = [deltas[key] for key in received]

    small = dict(ffn1_norm=jnp.concatenate(d_norm["ffn1_norm"], axis=0), mix_norm=jnp.concatenate(d_norm["mix_norm"], axis=0),
                 ffn2_norm=jnp.concatenate(d_norm["ffn2_norm"], axis=0), final_norm=d_final_norm,
                 a_A_log=d_alog[0, HEADS_A:2 * HEADS_A], a_dt_bias=d_dt[0, HEADS_A:2 * HEADS_A],
                 b_sinks=d_sinks[0, :b_sinks.shape[1]], a_out_norm=d_onorm, loss=loss_row[0, 0])
    hs, ts = gather_start([_pack_small(small)], "gather_small_start")
    r3 = push_wait(last["handles"], done_first + [ts], "exchange3_wait", scatter=True)
    received.update(ffn1_w_gu=[r3[0], r1[6]], ffn1_w_down=[r3[1], r1[7]])
    update("ffn1_w_gu")
    update("ffn1_w_down")
    every = push_wait(hs, deltas["ffn1_w_down"], "gather_small_wait", scatter=False)[0]
    out = adam_update([every], _pack_small(w)[None], _pack_small(m)[None], _pack_small(v)[None], "adam_small")
    for dst, packed in zip((grads, deltas, new_m, new_v), out):
        dst.update(_unpack_small(packed[0], w))
    loss = out[0][0, 7, LOSS_LANE]

    return (loss, grad_x, *[grads[k_] for k_ in WEIGHTS], *[deltas[k_] for k_ in WEIGHTS],
            *[new_m[k_] for k_ in WEIGHTS], *[new_v[k_] for k_ in WEIGHTS])
```

```python
import functools

import jax
import jax.numpy as jnp
from jax import lax
from jax.experimental import pallas as pl
from jax.experimental.pallas import tpu as pltpu

F32, BF16 = jnp.float32, jnp.bfloat16
HI = lax.Precision.HIGHEST
EPS = 1e-6

N_DEV = 8
D = 1024
FB = 704
N_FB = 4
HEADS_A, DK = 8, 128
CHUNK = 64
PREP_T = 512
A_COLS = 4224
B_HD, B_BLK = 64, 128
VMEM_LIMIT_V7X = 60 * 1024 * 1024

ADAM_LR, ADAM_B1, ADAM_B2, ADAM_EPS, ADAM_WD, ADAM_STEP = 0.001, 0.9, 0.999, 1e-08, 0.01, 10

NT = (((1,), (1,)), ((), ()))
TN = (((0,), (0,)), ((), ()))


def _pc(body, *, name, out_shape, grid=(), in_specs=None, out_specs=None, scratch=(), sem=None, **kw):
    params = pltpu.CompilerParams(dimension_semantics=sem, vmem_limit_bytes=VMEM_LIMIT_V7X)
    return pl.pallas_call(body, name=name, out_shape=out_shape, grid=grid, in_specs=in_specs, out_specs=out_specs,
                          scratch_shapes=list(scratch), compiler_params=params, **kw)


def _sds(shape, dtype):
    return jax.ShapeDtypeStruct(tuple(shape), dtype)


def _dot(a, b, dims=None, precision=None):
    if dims is None:
        return jnp.dot(a, b, preferred_element_type=F32, precision=precision)
    return lax.dot_general(a, b, dims, preferred_element_type=F32, precision=precision)


def _sigmoid(x):
    return 1.0 / (1.0 + jnp.exp(-x))


def _softplus(x):
    return jnp.maximum(x, 0.0) + jnp.log(1.0 + jnp.exp(-jnp.abs(x)))


def _rms_fwd(x, w):
    r = lax.rsqrt(jnp.mean(x * x, axis=-1, keepdims=True) + EPS)
    return x * r * w


def _rms_bwd(x, w, dy):
    r = lax.rsqrt(jnp.mean(x * x, axis=-1, keepdims=True) + EPS)
    xh = x * r
    dxh = dy * w
    dx = r * (dxh - xh * jnp.mean(dxh * xh, axis=-1, keepdims=True))
    return dx, jnp.sum(dy * xh, axis=0, keepdims=True)


def _tile(n, want):
    t = min(n, want)
    assert n % t == 0, (n, want)
    return t


def rmsnorm_bf16(x, w, name, deps=()):
    T = x.shape[0]
    tm = _tile(T, 1024)

    def body(x_ref, w_ref, *rest):
        rest[-1][...] = _rms_fwd(x_ref[...], w_ref[...]).astype(BF16)

    return _pc(body, name=name, out_shape=_sds((T, D), BF16), grid=(T // tm,),
               in_specs=[pl.BlockSpec((tm, D), lambda i: (i, 0)), pl.BlockSpec((1, D), lambda i: (0, 0))] + [DEP_SPEC] * len(deps),
               out_specs=pl.BlockSpec((tm, D), lambda i: (i, 0)), sem=("parallel",))(x, w, *deps)


def rmsnorm_bwd_add(x, w, dxn, dres, name):
    T = x.shape[0]
    tm = _tile(T, 512)

    def body(x_ref, w_ref, dxn_ref, dres_ref, dx_ref, dw_ref):
        dx, dw = _rms_bwd(x_ref[...], w_ref[...], dxn_ref[...])
        dx_ref[...] = dres_ref[...] + dx

        @pl.when(pl.program_id(0) == 0)
        def _():
            dw_ref[...] = jnp.zeros_like(dw_ref)
        dw_ref[...] += dw

    row = pl.BlockSpec((tm, D), lambda i: (i, 0))
    vec = pl.BlockSpec((1, D), lambda i: (0, 0))
    return _pc(body, name=name, out_shape=(_sds((T, D), F32), _sds((1, D), F32)), grid=(T // tm,),
               in_specs=[row, vec, row, row], out_specs=(row, vec), sem=("arbitrary",))(x, w, dxn, dres)


def final_loss(x, w, tgt, name):
    T = x.shape[0]
    tm = _tile(T, 512)

    def body(x_ref, w_ref, t_ref, loss_ref, dx_ref, dw_ref):
        xv, wv = x_ref[...], w_ref[...]
        err = _rms_fwd(xv, wv) - t_ref[...]
        dx, dw = _rms_bwd(xv, wv, err * (1.0 / D))
        dx_ref[...] = dx

        @pl.when(pl.program_id(0) == 0)
        def _():
            dw_ref[...] = jnp.zeros_like(dw_ref)
            loss_ref[...] = jnp.zeros_like(loss_ref)
        dw_ref[...] += dw
        loss_ref[...] += jnp.full((1, 128), 0.5 / D, F32) * jnp.sum(err * err)

    row = pl.BlockSpec((tm, D), lambda i: (i, 0))
    vec = pl.BlockSpec((1, D), lambda i: (0, 0))
    return _pc(body, name=name, out_shape=(_sds((1, 128), F32), _sds((T, D), F32), _sds((1, D), F32)),
               grid=(T // tm,), in_specs=[row, vec, row],
               out_specs=(pl.BlockSpec((1, 128), lambda i: (0, 0)), row, vec), sem=("arbitrary",))(x, w, tgt)


def _col_tile(n):
    for t in (1536, 1408, 1024, 768, 512, 384, 256, 128):
        if n % t == 0:
            return t
    return n


def mm_nn(a, b, name, bias=None, residual=None, out_dtype=F32, cols=None):
    T, K = a.shape
    first, end = cols or (0, b.shape[1])
    N = end - first
    tm, tn = _tile(T, 512), _col_tile(N)
    assert first % tn == 0 and (cols is None or (bias is None and residual is None))
    j0 = first // tn

    def body(a_ref, b_ref, *rest):
        o_ref = rest[-1]
        acc = _dot(a_ref[...].astype(BF16), b_ref[...])
        for extra in rest[:-1]:
            acc = acc + extra[...]
        o_ref[...] = acc.astype(out_dtype)

    in_specs = [pl.BlockSpec((tm, K), lambda j, i: (i, 0)), pl.BlockSpec((K, tn), lambda j, i: (0, j0 + j))]
    args = [a, b]
    if bias is not None:
        in_specs.append(pl.BlockSpec((1, tn), lambda j, i: (0, j)))
        args.append(bias)
    if residual is not None:
        in_specs.append(pl.BlockSpec((tm, tn), lambda j, i: (i, j)))
        args.append(residual)
    return _pc(body, name=name, out_shape=_sds((T, N), out_dtype), grid=(N // tn, T // tm), in_specs=in_specs,
               out_specs=pl.BlockSpec((tm, tn), lambda j, i: (i, j)), sem=("parallel", "parallel"))(*args)


def mm_nt(a, b, name, out_dtype=F32):
    T, N = a.shape
    K = b.shape[0]
    tm = _tile(T, 512)

    def body(a_ref, b_ref, o_ref):
        o_ref[...] = _dot(a_ref[...].astype(BF16), b_ref[...], NT).astype(out_dtype)

    return _pc(body, name=name, out_shape=_sds((T, K), out_dtype), grid=(T // tm,),
               in_specs=[pl.BlockSpec((tm, N), lambda i: (i, 0)), pl.BlockSpec((K, N), lambda i: (0, 0))],
               out_specs=pl.BlockSpec((tm, K), lambda i: (i, 0)), sem=("parallel",))(a, b)


def mm_tn(a, b, name):
    T, K = a.shape
    N = b.shape[1]
    tt, tn = _tile(T, 1024), _col_tile(N)

    def body(a_ref, b_ref, o_ref):
        @pl.when(pl.program_id(1) == 0)
        def _():
            o_ref[...] = jnp.zeros_like(o_ref)
        o_ref[...] += _dot(a_ref[...].astype(BF16), b_ref[...].astype(BF16), TN)

    return _pc(body, name=name, out_shape=_sds((K, N), F32), grid=(N // tn, T // tt),
               in_specs=[pl.BlockSpec((tt, K), lambda j, t: (t, 0)), pl.BlockSpec((tt, tn), lambda j, t: (t, j))],
               out_specs=pl.BlockSpec((K, tn), lambda j, t: (0, j)), sem=("parallel", "arbitrary"))(a, b)


def ffn_up(xn, wgu, name, deps=()):
    T = xn.shape[0]
    tm = _tile(T, 1024)

    def body(x_ref, w_ref, *rest):
        xv = x_ref[...]
        for j in range(2 * N_FB):
            rest[-1][j] = _dot(xv, w_ref[j]).astype(BF16)

    return _pc(body, name=name, out_shape=_sds((2 * N_FB, T, FB), BF16), grid=(T // tm,),
               in_specs=[pl.BlockSpec((tm, D), lambda i: (i, 0)), _resident((2 * N_FB, D, FB))] + [DEP_SPEC] * len(deps),
               out_specs=pl.BlockSpec((2 * N_FB, tm, FB), lambda i: (0, i, 0)), sem=("parallel",))(xn, wgu, *deps)


def ffn_down(gu, wd, x, name):
    T = x.shape[0]
    tm = _tile(T, 512)

    def body(gu_ref, w_ref, x_ref, o_ref):
        acc = jnp.zeros((tm, D), F32)
        for g in range(N_FB):
            gate, up = gu_ref[g], gu_ref[N_FB + g]
            acc = acc + _dot(gate * _sigmoid(gate) * up, w_ref[g])
        o_ref[...] = x_ref[...] + 0.5 * acc

    row = pl.BlockSpec((tm, D), lambda i: (i, 0))
    return _pc(body, name=name, out_shape=_sds((T, D), F32), grid=(T // tm,),
               in_specs=[pl.BlockSpec((2 * N_FB, tm, FB), lambda i: (0, i, 0)),
                         _resident((N_FB, FB, D)), row],
               out_specs=row, sem=("parallel",))(gu, wd, x)


def _resident(shape):
    return pl.BlockSpec(shape, lambda *_: (0,) * len(shape), pipeline_mode=pl.Buffered(1))


def _store_blocks_bf16(acc, out_hbm, stage, sem):
    for j in range(acc.shape[0]):
        stage[...] = acc[j].astype(BF16)
        copy = pltpu.make_async_copy(stage, out_hbm.at[j], sem)
        copy.start()
        copy.wait()


def ffn_bwd_hidden(dout, wd, gu, name, deps=()):
    T = dout.shape[0]
    tm = _tile(T, 512)
    n_t = T // tm

    def body(d_ref, w_ref, gu_ref, *rest):
        dgu_ref, dwd_hbm, acc, stage, sem = rest[-5:]
        t = pl.program_id(0)

        @pl.when(t == 0)
        def _():
            acc[...] = jnp.zeros_like(acc)
        dy = (0.5 * d_ref[...]).astype(BF16)
        for g in range(N_FB):
            gate, up = gu_ref[g], gu_ref[N_FB + g]
            sg = _sigmoid(gate)
            silu = gate * sg
            dact = _dot(dy, w_ref[g], NT).astype(BF16)
            acc[g] += _dot(silu * up, dy, TN)
            dgu_ref[g] = dact * up * (sg * (1.0 + gate * (1.0 - sg)))
            dgu_ref[N_FB + g] = dact * silu

        @pl.when(t == n_t - 1)
        def _():
            _store_blocks_bf16(acc, dwd_hbm, stage, sem)

    return _pc(body, name=name, out_shape=(_sds((2 * N_FB, T, FB), BF16), _sds((N_FB, FB, D), BF16)), grid=(n_t,),
               in_specs=[pl.BlockSpec((tm, D), lambda i: (i, 0)), _resident((N_FB, FB, D)),
                         pl.BlockSpec((2 * N_FB, tm, FB), lambda i: (0, i, 0))] + [DEP_SPEC] * len(deps),
               out_specs=(pl.BlockSpec((2 * N_FB, tm, FB), lambda i: (0, i, 0)), pl.BlockSpec(memory_space=pl.ANY)),
               scratch=[pltpu.VMEM((N_FB, FB, D), F32), pltpu.VMEM((FB, D), BF16), pltpu.SemaphoreType.DMA],
               sem=("arbitrary",))(dout, wd, gu, *deps)


def ffn_bwd_input(dgu, wgu, x, dout, nw, name, deps=()):
    T = x.shape[0]
    tm = _tile(T, 512)

    def body(dgu_ref, w_ref, x_ref, d_ref, nw_ref, *rest):
        dx_ref, dnw_ref = rest[-2:]
        dxn = jnp.zeros((tm, D), F32)
        for j in range(2 * N_FB):
            dxn = dxn + _dot(dgu_ref[j], w_ref[j], NT)
        dx, dw = _rms_bwd(x_ref[...], nw_ref[...], dxn)
        dx_ref[...] = d_ref[...] + dx

        @pl.when(pl.program_id(0) == 0)
        def _():
            dnw_ref[...] = jnp.zeros_like(dnw_ref)
        dnw_ref[...] += dw

    row = pl.BlockSpec((tm, D), lambda i: (i, 0))
    vec = pl.BlockSpec((1, D), lambda i: (0, 0))
    return _pc(body, name=name, out_shape=(_sds((T, D), F32), _sds((1, D), F32)), grid=(T // tm,),
               in_specs=[pl.BlockSpec((2 * N_FB, tm, FB), lambda i: (0, i, 0)), _resident((2 * N_FB, D, FB)),
                         row, row, vec] + [DEP_SPEC] * len(deps),
               out_specs=(row, vec), sem=("arbitrary",))(dgu, wgu, x, dout, nw, *deps)


def ffn_wgrad_gu(xn, dgu, name):
    T = xn.shape[0]
    tt = _tile(T, 1024)
    n_t = T // tt

    def body(x_ref, d_ref, dw_hbm, acc, stage, sem):
        t = pl.program_id(0)

        @pl.when(t == 0)
        def _():
            acc[...] = jnp.zeros_like(acc)
        xn_tile = x_ref[...]
        for j in range(2 * N_FB):
            acc[j] += _dot(xn_tile, d_ref[j], TN)

        @pl.when(t == n_t - 1)
        def _():
            _store_blocks_bf16(acc, dw_hbm, stage, sem)

    return _pc(body, name=name, out_shape=_sds((2 * N_FB, D, FB), BF16), grid=(n_t,),
               in_specs=[pl.BlockSpec((tt, D), lambda t: (t, 0)), pl.BlockSpec((2 * N_FB, tt, FB), lambda t: (0, t, 0))],
               out_specs=pl.BlockSpec(memory_space=pl.ANY),
               scratch=[pltpu.VMEM((2 * N_FB, D, FB), F32), pltpu.VMEM((D, FB), BF16), pltpu.SemaphoreType.DMA],
               sem=("arbitrary",))(xn, dgu)


def ffn_forward(x, nw, wgu, wd, tag):
    T = x.shape[0]
    tm = _tile(T, 512)

    def body(x_ref, nw_ref, wgu_ref, wd_ref, o_ref, xn_ref, gu_ref):
        xv = x_ref[...]
        xn = _rms_fwd(xv, nw_ref[...]).astype(BF16)
        xn_ref[...] = xn
        for j in range(2 * N_FB):
            gu_ref[j] = _dot(xn, wgu_ref[j]).astype(BF16)
        acc = jnp.zeros((tm, D), F32)
        for g in range(N_FB):
            gate, up = gu_ref[g], gu_ref[N_FB + g]
            acc = acc + _dot(gate * _sigmoid(gate) * up, wd_ref[g])
        o_ref[...] = xv + 0.5 * acc

    row = pl.BlockSpec((tm, D), lambda i: (i, 0))
    out, xn, gu = _pc(body, name=f"{tag}_fwd",
                      out_shape=(_sds((T, D), F32), _sds((T, D), BF16), _sds((2 * N_FB, T, FB), BF16)), grid=(T // tm,),
                      in_specs=[row, pl.BlockSpec((1, D), lambda i: (0, 0)), _resident((2 * N_FB, D, FB)),
                                _resident((N_FB, FB, D))],
                      out_specs=(row, row, pl.BlockSpec((2 * N_FB, tm, FB), lambda i: (0, i, 0))),
                      sem=("parallel",))(x, nw, wgu, wd)
    return out, (x, xn, gu)


def ffn_backward(dout, saved, nw, wgu, wd, tag, deps=(), on_grads=None):
    x, xn, gu = saved
    dgu, dwd = ffn_bwd_hidden(dout, wd, gu, f"{tag}_bwd_hidden", deps)
    dwgu = ffn_wgrad_gu(xn, dgu, f"{tag}_wgrad_gu")
    late = on_grads(dwgu, dwd) if on_grads else ()
    dx, dnw = ffn_bwd_input(dgu, wgu, x, dout, nw, f"{tag}_bwd_input", late)
    return dx, dnw, dwgu, dwd


N_QKV_BLK = 3 * HEADS_A
Z_BLK0 = N_QKV_BLK
MAIN_COLS = 4 * D
HALO = 16


def _shifted(stage, tm, back):
    return stage[pl.ds(HALO - back, tm), :]


def _conv_taps(stage, w, tm):
    c = _shifted(stage, tm, 0) * w[3:4]
    for k in range(3):
        c = c + _shifted(stage, tm, 3 - k) * w[k:k + 1]
    return c


def _head_cols(h):
    return slice(128 * h, 128 * (h + 1))


def gdn_conv_fwd(proj, wconv, name):
    T = proj.shape[0]
    tm = _tile(T, 512)

    def body(cur_ref, prev_ref, w_ref, c_ref, y_ref, stage):
        kind, t = pl.program_id(0), pl.program_id(1)
        stage[pl.ds(0, HALO), :] = jnp.where(t > 0, prev_ref[...].astype(F32), 0.0)
        stage[pl.ds(HALO, tm), :] = cur_ref[...].astype(F32)
        c = _conv_taps(stage, w_ref[...], tm)
        c_ref[...] = c.astype(BF16)
        s = c * _sigmoid(c)
        scale = jnp.where(kind == 0, DK ** -0.5, 1.0)
        for h in range(HEADS_A):
            sh = s[:, _head_cols(h)]
            r = lax.rsqrt(jnp.sum(sh * sh, axis=-1, keepdims=True) + EPS)
            y_ref[h] = (sh * jnp.where(kind < 2, r * scale, 1.0)).astype(BF16)

    return _pc(body, name=name, out_shape=(_sds((T, 3 * D), BF16), _sds((N_QKV_BLK, T, 128), BF16)),
               grid=(3, T // tm),
               in_specs=[pl.BlockSpec((tm, D), lambda kd, t: (t, kd)),
                         pl.BlockSpec((HALO, D), lambda kd, t: (jnp.maximum(t * (tm // HALO) - 1, 0), kd)),
                         pl.BlockSpec((4, D), lambda kd, t: (0, kd))],
               out_specs=(pl.BlockSpec((tm, D), lambda kd, t: (t, kd)),
                          pl.BlockSpec((HEADS_A, tm, 128), lambda kd, t: (kd, t, 0))),
               scratch=[pltpu.VMEM((tm + HALO, D), F32)], sem=("parallel", "parallel"))(proj, proj, wconv)


def gdn_conv_bwd(dqkv, c, proj, wconv, name):
    T = c.shape[0]
    tm = _tile(T, 512)
    n_t = T // tm

    def body(dy_ref, dyn_ref, c_ref, cn_ref, x_ref, xp_ref, w_ref, dx_ref, dw_ref, d_stage, x_stage):
        kind, t = pl.program_id(0), pl.program_id(1)
        scale = jnp.where(kind == 0, DK ** -0.5, 1.0)

        def act_bwd(dy, cv):
            sg = _sigmoid(cv)
            s = cv * sg
            parts = []
            for h in range(HEADS_A):
                sh, dyh = s[:, _head_cols(h)], dy[h]
                r = lax.rsqrt(jnp.sum(sh * sh, axis=-1, keepdims=True) + EPS)
                ds_norm = scale * r * (dyh - (r * r) * sh * jnp.sum(dyh * sh, axis=-1, keepdims=True))
                parts.append(jnp.where(kind < 2, ds_norm, dyh))
            return jnp.concatenate(parts, axis=1) * (sg * (1.0 + cv * (1.0 - sg)))

        w = w_ref[...]
        dcur = act_bwd(dy_ref[...].astype(F32), c_ref[...].astype(F32))
        dnext = jnp.where(t < n_t - 1, act_bwd(dyn_ref[...].astype(F32), cn_ref[...].astype(F32)), 0.0)
        d_stage[pl.ds(0, tm), :] = dcur
        d_stage[pl.ds(tm, HALO), :] = dnext
        dx = dcur * w[3:4]
        for k in range(3):
            dx = dx + d_stage[pl.ds(3 - k, tm), :] * w[k:k + 1]
        dx_ref[...] = dx.astype(BF16)
        x_stage[pl.ds(0, HALO), :] = jnp.where(t > 0, xp_ref[...].astype(F32), 0.0)
        x_stage[pl.ds(HALO, tm), :] = x_ref[...].astype(F32)
        rows = [jnp.sum(dcur * _shifted(x_stage, tm, 3 - k), axis=0, keepdims=True) for k in range(4)]

        @pl.when(t == 0)
        def _():
            dw_ref[...] = jnp.zeros_like(dw_ref)
        dw_ref[...] += jnp.concatenate(rows, axis=0)

    def nxt(t):
        return jnp.minimum((t + 1) * (tm // HALO), T // HALO - 1)

    cur = pl.BlockSpec((tm, D), lambda kd, t: (t, kd))
    return _pc(body, name=name, out_shape=(_sds((T, 3 * D), BF16), _sds((4, 3 * D), F32)), grid=(3, n_t),
               in_specs=[pl.BlockSpec((HEADS_A, tm, 128), lambda kd, t: (kd, t, 0)),
                         pl.BlockSpec((HEADS_A, HALO, 128), lambda kd, t: (kd, nxt(t), 0)),
                         cur, pl.BlockSpec((HALO, D), lambda kd, t: (nxt(t), kd)),
                         cur, pl.BlockSpec((HALO, D), lambda kd, t: (jnp.maximum(t * (tm // HALO) - 1, 0), kd)),
                         pl.BlockSpec((4, D), lambda kd, t: (0, kd))],
               out_specs=(cur, pl.BlockSpec((4, D), lambda kd, t: (0, kd))),
               scratch=[pltpu.VMEM((tm + HALO, D), F32), pltpu.VMEM((tm + HALO, D), F32)],
               sem=("parallel", "arbitrary"))(dqkv, dqkv, c, c, proj, proj, wconv)


def _chunk_masks(n):
    ri = lax.broadcasted_iota(jnp.int32, (n, n), 0)
    ci = lax.broadcasted_iota(jnp.int32, (n, n), 1)
    same = (ri // CHUNK) == (ci // CHUNK)
    return same & (ri >= ci), same & (ri <= ci)


def gdn_gate_fwd(ba, al, dtb, name):
    T = ba.shape[0]
    tg = _tile(T, PREP_T)

    def body(ba_ref, al_ref, dtb_ref, o_ref):
        x = ba_ref[...]
        lane = lax.broadcasted_iota(jnp.int32, x.shape, 1)
        is_a = (lane >= HEADS_A) & (lane < 2 * HEADS_A)
        g = jnp.where(is_a, -jnp.exp(al_ref[...]) * _softplus(x + dtb_ref[...]), 0.0)
        lower, _ = _chunk_masks(tg)
        gc = _dot(lower.astype(F32), g, precision=HI)
        o_ref[...] = jnp.where(lane < HEADS_A, _sigmoid(x), gc)

    vec = pl.BlockSpec((1, 128), lambda i: (0, 0))
    return _pc(body, name=name, out_shape=_sds((T, 128), F32), grid=(T // tg,),
               in_specs=[pl.BlockSpec((tg, 128), lambda i: (i, 0)), vec, vec],
               out_specs=pl.BlockSpec((tg, 128), lambda i: (i, 0)), sem=("parallel",))(ba, al, dtb)


def gdn_gate_bwd(ba, al, dtb, dgb, name):
    T = ba.shape[0]
    tg = _tile(T, PREP_T)

    def body(ba_ref, al_ref, dtb_ref, dgb_ref, dba_ref, dal_ref, ddt_ref):
        x, d = ba_ref[...], dgb_ref[...]
        lane = lax.broadcasted_iota(jnp.int32, x.shape, 1)
        is_b = lane < HEADS_A
        is_a = (lane >= HEADS_A) & (lane < 2 * HEADS_A)
        beta = _sigmoid(x)
        e_a = jnp.exp(al_ref[...])
        z = x + dtb_ref[...]
        g = jnp.where(is_a, -e_a * _softplus(z), 0.0)
        _, upper = _chunk_masks(tg)
        dg = _dot(upper.astype(F32), jnp.where(is_a, d, 0.0), precision=HI)
        da = jnp.where(is_a, dg * (-e_a) * _sigmoid(z), 0.0)
        db = jnp.where(is_b, d * beta * (1.0 - beta), 0.0)
        dba_ref[...] = (da + db).astype(BF16)

        @pl.when(pl.program_id(0) == 0)
        def _():
            dal_ref[...] = jnp.zeros_like(dal_ref)
            ddt_ref[...] = jnp.zeros_like(ddt_ref)
        dal_ref[...] += jnp.sum(dg * g, axis=0, keepdims=True)
        ddt_ref[...] += jnp.sum(da, axis=0, keepdims=True)

    vec = pl.BlockSpec((1, 128), lambda i: (0, 0))
    blk = pl.BlockSpec((tg, 128), lambda i: (i, 0))
    return _pc(body, name=name, out_shape=(_sds((T, 128), BF16), _sds((1, 128), F32), _sds((1, 128), F32)),
               grid=(T // tg,), in_specs=[blk, vec, vec, blk],
               out_specs=(blk, vec, vec), sem=("arbitrary",))(ba, al, dtb, dgb)


def _bmm(a, b, dims, precision=None):
    return lax.dot_general(a, b, dims, preferred_element_type=F32, precision=precision)


B_NN = (((2,), (1,)), ((0,), (0,)))
B_NT = (((2,), (2,)), ((0,), (0,)))


def _select_lane(x, lane_index):
    lane = lax.broadcasted_iota(jnp.int32, x.shape, x.ndim - 1)
    return jnp.sum(jnp.where(lane == lane_index, x, 0.0), axis=-1, keepdims=True)


B_TN = (((1,), (1,)), ((0,), (0,)))


def _bmm_split(a, b, dims):
    ah, bh = a.astype(BF16), b.astype(BF16)
    al, bl = (a - ah.astype(F32)).astype(BF16), (b - bh.astype(F32)).astype(BF16)
    return _bmm(ah, bh, dims) + (_bmm(ah, bl, dims) + _bmm(al, bh, dims))


@jax.custom_vjp
def _bmm_f32(a, b):
    return _bmm_split(a, b, B_NN)


def _bmm_f32_fwd(a, b):
    return _bmm_split(a, b, B_NN), (a, b)


def _bmm_bf16(a, b, dims):
    return _bmm(a.astype(BF16), b.astype(BF16), dims)


def _bmm_f32_bwd(res, dc):
    a, b = res
    return _bmm_bf16(dc, b, B_NT), _bmm_bf16(a, dc, B_TN)


_bmm_f32.defvjp(_bmm_f32_fwd, _bmm_f32_bwd)


def _tri_inverse(lmat):
    ri = lax.broadcasted_iota(jnp.int32, lmat.shape, 1)
    ci = lax.broadcasted_iota(jnp.int32, lmat.shape, 2)
    eye = jnp.where(ri == ci, 1.0, 0.0)
    inv = eye - lmat
    power = lmat
    for _ in range(5):
        power = _bmm_bf16(power, power, B_NN)
        inv = inv + _bmm_bf16(inv, power, B_NN)
    return _bmm_split(inv, 2.0 * eye - _bmm_split(eye + lmat, inv, B_NN), B_NN)


def _stored_inverse(x):
    @jax.custom_vjp
    def inverse(lmat):
        return x

    def fwd(lmat):
        return x, None

    def bwd(_, dx):
        return (-_bmm_bf16(_bmm_bf16(x, dx, B_TN), x, B_NT),)

    inverse.defvjp(fwd, bwd)
    return inverse


def _gdn_prep(q, k, v, gb, h, inverse):
    nb = q.shape[0]
    beta = _select_lane(gb, h)
    gc = _select_lane(gb, HEADS_A + h)
    ri = lax.broadcasted_iota(jnp.int32, (nb, CHUNK, CHUNK), 1)
    ci = lax.broadcasted_iota(jnp.int32, (nb, CHUNK, CHUNK), 2)
    lower, strict, eye = ri >= ci, ri > ci, ri == ci
    gcol = jnp.broadcast_to(gc, (nb, CHUNK, CHUNK))
    grow = jnp.swapaxes(gcol, 1, 2)
    decay = jnp.where(lower, jnp.exp(jnp.where(lower, gcol - grow, 0.0)), 0.0)
    kb = k * beta
    kbf = k.astype(BF16)
    inv = inverse(jnp.where(strict, _bmm(kb.astype(BF16), kbf, B_NT) * decay, 0.0))
    eg = jnp.exp(gc)
    sol = _bmm_f32(inv, jnp.concatenate([v * beta, kb * eg], axis=-1))
    aqk = _bmm(q.astype(BF16), kbf, B_NT) * decay
    g_last = gc[:, CHUNK - 1:CHUNK, :]
    gl = jnp.broadcast_to(jnp.exp(g_last), (nb, 1, 128))
    return (sol[..., :DK], sol[..., DK:], q * eg, k * jnp.exp(g_last - gc), aqk, gl), inv


def gdn_prep_fwd(qkv, gb, name):
    T = qkv.shape[1]
    tp = _tile(T, PREP_T)
    nb = tp // CHUNK

    def body(q_ref, k_ref, v_ref, gb_ref, u_ref, w_ref, qd_ref, kd_ref, a_ref, gl_ref, inv_ref):
        h = pl.program_id(1)
        shp = (nb, CHUNK, 128)
        q, k, v = (ref[0].astype(F32).reshape(shp) for ref in (q_ref, k_ref, v_ref))
        (u, w, qd, kd, aqk, gl), inv = _gdn_prep(q, k, v, gb_ref[...].reshape(shp), h, _tri_inverse)
        u_ref[0] = u.reshape(tp, 128)
        w_ref[0] = w.reshape(tp, 128).astype(BF16)
        qd_ref[0] = qd.reshape(tp, 128).astype(BF16)
        kd_ref[0] = kd.reshape(tp, 128).astype(BF16)
        a_ref[0] = aqk.reshape(tp, CHUNK).astype(BF16)
        gl_ref[0] = gl.reshape(nb, 1, 128)
        inv_ref[0] = inv.reshape(tp, CHUNK)

    def head(off):
        return pl.BlockSpec((1, tp, 128), lambda n, h: (h + off, n, 0))

    matmul_only = _sds((HEADS_A, T, 128), BF16)
    narrow = pl.BlockSpec((1, tp, CHUNK), lambda n, h: (h, n, 0))
    return _pc(body, name=name,
               out_shape=(_sds((HEADS_A, T, 128), F32), matmul_only, matmul_only, matmul_only, _sds((HEADS_A, T, CHUNK), BF16),
                          _sds((HEADS_A, T // CHUNK, 1, 128), F32), _sds((HEADS_A, T, CHUNK), F32)),
               grid=(T // tp, HEADS_A),
               in_specs=[head(0), head(HEADS_A), head(2 * HEADS_A), pl.BlockSpec((tp, 128), lambda n, h: (n, 0))],
               out_specs=(head(0), head(0), head(0), head(0), narrow,
                          pl.BlockSpec((1, nb, 1, 128), lambda n, h: (h, n, 0, 0)), narrow),
               sem=("parallel", "parallel"))(qkv, qkv, qkv, gb)


def gdn_prep_bwd(qkv, gb, inv, du, dw, dqd, dkd, da, dgl, name):
    T = qkv.shape[1]
    tp = _tile(T, PREP_T)
    nb = tp // CHUNK

    def body(q_ref, k_ref, v_ref, gb_ref, inv_ref, du_ref, dw_ref, dqd_ref, dkd_ref, da_ref, dgl_ref, dqkv_ref, dgb_ref):
        h = pl.program_id(1)
        shp = (nb, CHUNK, 128)
        stored = _stored_inverse(inv_ref[0].reshape(nb, CHUNK, CHUNK))
        q, k, v = (ref[0].astype(F32).reshape(shp) for ref in (q_ref, k_ref, v_ref))
        _, vjp = jax.vjp(lambda q, k, v, gb: _gdn_prep(q, k, v, gb, h, stored)[0], q, k, v, gb_ref[...].reshape(shp))
        dq, dk, dv, dgb = vjp((du_ref[0].reshape(shp), dw_ref[0].reshape(shp), dqd_ref[0].reshape(shp),
                               dkd_ref[0].reshape(shp), da_ref[0].reshape(nb, CHUNK, CHUNK), dgl_ref[0].reshape(nb, 1, 128)))
        dqkv_ref[h] = dq.reshape(tp, 128).astype(BF16)
        dqkv_ref[HEADS_A + h] = dk.reshape(tp, 128).astype(BF16)
        dqkv_ref[2 * HEADS_A + h] = dv.reshape(tp, 128).astype(BF16)

        @pl.when(h == 0)
        def _():
            dgb_ref[...] = jnp.zeros_like(dgb_ref)
        dgb_ref[...] += dgb.reshape(tp, 128)

    def head(off):
        return pl.BlockSpec((1, tp, 128), lambda n, h: (h + off, n, 0))

    narrow = pl.BlockSpec((1, tp, CHUNK), lambda n, h: (h, n, 0))
    return _pc(body, name=name, out_shape=(_sds((N_QKV_BLK, T, 128), BF16), _sds((T, 128), F32)),
               grid=(T // tp, HEADS_A),
               in_specs=[head(0), head(HEADS_A), head(2 * HEADS_A), pl.BlockSpec((tp, 128), lambda n, h: (n, 0)), narrow,
                         head(0), head(0), head(0), head(0), narrow,
                         pl.BlockSpec((1, nb, 1, 128), lambda n, h: (h, n, 0, 0))],
               out_specs=(pl.BlockSpec((N_QKV_BLK, tp, 128), lambda n, h: (0, n, 0)),
                          pl.BlockSpec((tp, 128), lambda n, h: (n, 0))),
               sem=("parallel", "arbitrary"))(qkv, qkv, qkv, gb, inv, du, dw, dqd, dkd, da, dgl)


def gdn_scan_fwd(u, w, qd, kd, aqk, gl, name):
    T = u.shape[1]
    n_chunks = T // CHUNK

    def body(u_ref, w_ref, qd_ref, kd_ref, a_ref, gl_ref, o_ref, sin_ref, state):
        @pl.when(pl.program_id(0) == 0)
        def _():
            state[...] = jnp.zeros_like(state)
        s = state[...]
        sb = s.astype(BF16)
        sin_ref[0] = sb
        both = _bmm(jnp.concatenate([w_ref[...], qd_ref[...]], axis=1).astype(BF16), sb, B_NN)
        vn = (u_ref[...] - both[:, :CHUNK]).astype(BF16)
        o_ref[...] = both[:, CHUNK:] + _bmm(a_ref[...].astype(BF16), vn, B_NN)
        state[...] = s * gl_ref[:, 0] + _bmm(kd_ref[...].astype(BF16), vn, B_TN)

    blk = pl.BlockSpec((HEADS_A, CHUNK, 128), lambda n: (0, n, 0))
    return _pc(body, name=name,
               out_shape=(_sds((HEADS_A, T, 128), F32), _sds((n_chunks, HEADS_A, DK, 128), BF16)), grid=(n_chunks,),
               in_specs=[blk, blk, blk, blk, pl.BlockSpec((HEADS_A, CHUNK, CHUNK), lambda n: (0, n, 0)),
                         pl.BlockSpec((HEADS_A, 1, 1, 128), lambda n: (0, n, 0, 0))],
               out_specs=(blk, pl.BlockSpec((1, HEADS_A, DK, 128), lambda n: (n, 0, 0, 0))),
               scratch=[pltpu.VMEM((HEADS_A, DK, 128), F32)], sem=("arbitrary",))(u, w, qd, kd, aqk, gl)


def gdn_scan_bwd(u, w, qd, kd, aqk, gl, sin, do, name):
    T = u.shape[1]
    n_chunks = T // CHUNK

    def body(u_ref, w_ref, qd_ref, kd_ref, a_ref, gl_ref, sin_ref, do_ref,
             du_ref, dw_ref, dqd_ref, dkd_ref, da_ref, dgl_ref, dstate):
        @pl.when(pl.program_id(0) == 0)
        def _():
            dstate[...] = jnp.zeros_like(dstate)
        lane0 = lax.broadcasted_iota(jnp.int32, (HEADS_A, 1, 128), 2) == 0
        sb = sin_ref[0]
        s = sb.astype(F32)
        wb, qdb, kdb = w_ref[...].astype(BF16), qd_ref[...].astype(BF16), kd_ref[...].astype(BF16)
        ab, dob = a_ref[...].astype(BF16), do_ref[...].astype(BF16)
        vn = (u_ref[...] - _bmm(wb, sb, B_NN)).astype(BF16)
        ds_out = dstate[...]
        dsb = ds_out.astype(BF16)
        dqd_ref[...] = _bmm(dob, sb, B_NT)
        da_ref[...] = _bmm(dob, vn, B_NT)
        dv = _bmm(ab, dob, B_TN) + _bmm(kdb, dsb, B_NN)
        dkd_ref[...] = _bmm(vn, dsb, B_NT)
        dgl = jnp.sum(jnp.sum(ds_out * s, axis=2, keepdims=True), axis=1, keepdims=True)
        dgl_ref[:, 0] = jnp.where(lane0, dgl, 0.0)
        du_ref[...] = dv
        dvb = dv.astype(BF16)
        dw_ref[...] = -_bmm(dvb, sb, B_NT)
        dstate[...] = ds_out * gl_ref[:, 0] + _bmm(qdb, dob, B_TN) - _bmm(wb, dvb, B_TN)

    last = n_chunks - 1
    blk = pl.BlockSpec((HEADS_A, CHUNK, 128), lambda n: (0, last - n, 0))
    ablk = pl.BlockSpec((HEADS_A, CHUNK, CHUNK), lambda n: (0, last - n, 0))
    glblk = pl.BlockSpec((HEADS_A, 1, 1, 128), lambda n: (0, last - n, 0, 0))
    per_head = _sds((HEADS_A, T, 128), F32)
    return _pc(body, name=name,
               out_shape=(per_head, per_head, per_head, per_head, _sds((HEADS_A, T, CHUNK), F32),
                          _sds((HEADS_A, n_chunks, 1, 128), F32)), grid=(n_chunks,),
               in_specs=[blk, blk, blk, blk, ablk, glblk,
                         pl.BlockSpec((1, HEADS_A, DK, 128), lambda n: (last - n, 0, 0, 0)), blk],
               out_specs=(blk, blk, blk, blk, ablk, glblk),
               scratch=[pltpu.VMEM((HEADS_A, DK, 128), F32)], sem=("arbitrary",))(u, w, qd, kd, aqk, gl, sin, do)


def gdn_outnorm_fwd(o, proj, wn, name):
    T = o.shape[1]
    tm = _tile(T, 512)

    def body(o_ref, z_ref, wn_ref, y_ref):
        for h in range(HEADS_A):
            z = z_ref[:, 128 * h:128 * (h + 1)].astype(F32)
            y_ref[:, 128 * h:128 * (h + 1)] = (_rms_fwd(o_ref[h], wn_ref[...]) * (z * _sigmoid(z))).astype(BF16)

    return _pc(body, name=name, out_shape=_sds((T, D), BF16), grid=(T // tm,),
               in_specs=[pl.BlockSpec((HEADS_A, tm, 128), lambda i: (0, i, 0)),
                         pl.BlockSpec((tm, D), lambda i: (i, Z_BLK0 * 128 // D)), pl.BlockSpec((1, 128), lambda i: (0, 0))],
               out_specs=pl.BlockSpec((tm, D), lambda i: (i, 0)), sem=("parallel",))(o, proj, wn)


def gdn_outnorm_bwd(o, proj, wn, dy, name):
    T = o.shape[1]
    tm = _tile(T, 512)

    def body(o_ref, z_ref, wn_ref, dy_ref, do_ref, dz_ref, dwn_ref):
        wn = wn_ref[...]
        acc = jnp.zeros((1, 128), F32)
        for h in range(HEADS_A):
            cols = slice(128 * h, 128 * (h + 1))
            z, dyh, ov = z_ref[:, cols].astype(F32), dy_ref[:, cols], o_ref[h]
            sg = _sigmoid(z)
            do, dwn = _rms_bwd(ov, wn, dyh * (z * sg))
            do_ref[h] = do
            acc = acc + dwn
            dz_ref[:, cols] = (dyh * _rms_fwd(ov, wn) * (sg * (1.0 + z * (1.0 - sg)))).astype(BF16)

        @pl.when(pl.program_id(0) == 0)
        def _():
            dwn_ref[...] = jnp.zeros_like(dwn_ref)
        dwn_ref[...] += acc

    row = pl.BlockSpec((tm, D), lambda i: (i, 0))
    vec = pl.BlockSpec((1, 128), lambda i: (0, 0))
    hblk = pl.BlockSpec((HEADS_A, tm, 128), lambda i: (0, i, 0))
    return _pc(body, name=name, out_shape=(_sds((HEADS_A, T, 128), F32), _sds((T, D), BF16), _sds((1, 128), F32)),
               grid=(T // tm,),
               in_specs=[hblk, pl.BlockSpec((tm, D), lambda i: (i, Z_BLK0 * 128 // D)), vec, row],
               out_specs=(hblk, row, vec), sem=("arbitrary",))(o, proj, wn, dy)


def gdn_forward(x, nw, w_in, wconv, al, dtb, wn, w_out, tag, deps=()):
    h = rmsnorm_bf16(x, nw, f"{tag}_norm", deps)
    proj = mm_nn(h, w_in, f"{tag}_proj", out_dtype=BF16, cols=(0, MAIN_COLS))
    ba = mm_nn(h, w_in, f"{tag}_proj_ba", cols=(MAIN_COLS, A_COLS))
    c, qkv = gdn_conv_fwd(proj, wconv, f"{tag}_conv")
    gb = gdn_gate_fwd(ba, al, dtb, f"{tag}_gate")
    u, w, qd, kd, aqk, gl, inv = gdn_prep_fwd(qkv, gb, f"{tag}_prep")
    o, sin = gdn_scan_fwd(u, w, qd, kd, aqk, gl, f"{tag}_scan")
    on = gdn_outnorm_fwd(o, proj, wn, f"{tag}_outnorm")
    y = mm_nn(on, w_out, f"{tag}_out", residual=x)
    return y, (x, h, proj, ba, c, qkv, gb, inv, (u, w, qd, kd, aqk, gl), sin, o, on)


def gdn_backward(dout, saved, nw, w_in, wconv, al, dtb, wn, w_out, tag):
    x, h, proj, ba, c, qkv, gb, inv, prep, sin, o, on = saved
    d_on = mm_nt(dout, w_out, f"{tag}_out_bwd")
    dw_out = mm_tn(on, dout, f"{tag}_out_wgrad")
    do, dz, dwn = gdn_outnorm_bwd(o, proj, wn, d_on, f"{tag}_outnorm_bwd")
    du, dw, dqd, dkd, da, dgl = gdn_scan_bwd(*prep, sin, do, f"{tag}_scan_bwd")
    dqkv, dgb = gdn_prep_bwd(qkv, gb, inv, du, dw, dqd, dkd, da, dgl, f"{tag}_prep_bwd")
    dba, dal, ddt = gdn_gate_bwd(ba, al, dtb, dgb, f"{tag}_gate_bwd")
    dpre, dwconv = gdn_conv_bwd(dqkv, c, proj, wconv, f"{tag}_conv_bwd")
    dproj = jnp.concatenate([dpre, dz, dba], axis=1)
    dw_in = mm_tn(h, dproj, f"{tag}_proj_wgrad")
    dh = mm_nt(dproj, w_in, f"{tag}_proj_bwd")
    dx, dnw = rmsnorm_bwd_add(x, nw, dh, dout, f"{tag}_norm_bwd")
    return dx, dnw, dw_in, dwconv, dal, ddt, dwn, dw_out


N_KV, GROUP = 4, 4
KV_COLS = 2 * N_KV * B_HD
B_COLS = D + KV_COLS


@jax.custom_vjp
def _swap_lane_halves(x):
    return pltpu.roll(x, 64, 1)


_swap_lane_halves.defvjp(lambda x: (pltpu.roll(x, 64, 1), None), lambda _, g: (pltpu.roll(g, 64, 1),))


def _swa_block(q, kp, kc, vp, vc, sk, first):
    cols = GROUP * B_BLK
    ks = lax.broadcasted_iota(jnp.int32, (N_KV, B_BLK, cols), 1)
    qi = lax.broadcasted_iota(jnp.int32, (N_KV, B_BLK, cols), 2) % B_BLK
    from_cur = ks <= qi

    def batch(parts):
        return jnp.concatenate([part[None] for part in parts], axis=0)

    def per_kv(cur, prev):
        return batch([jnp.concatenate([cur[:, j * B_HD:(j + 1) * B_HD], prev[:, j * B_HD:(j + 1) * B_HD]], axis=0)
                      for j in range(N_KV)]).astype(BF16)

    qs = batch([jnp.concatenate([q[:, hq * B_HD:(hq + 1) * B_HD] for hq in range(GROUP * j, GROUP * (j + 1))], axis=0)
                for j in range(N_KV)])
    q_t = jnp.swapaxes(qs, 1, 2).astype(BF16)
    sink = batch([jnp.concatenate([jnp.broadcast_to(sk[:, hq:hq + 1], (1, B_BLK))
                                   for hq in range(GROUP * j, GROUP * (j + 1))], axis=1) for j in range(N_KV)])
    both = _bmm(per_kv(kc, kp), q_t, B_NN)
    s = jnp.where(from_cur, both[:, :B_BLK], jnp.where(first, -1e30, both[:, B_BLK:])) * (B_HD ** -0.5)
    m = lax.stop_gradient(jnp.maximum(jnp.max(s, axis=1, keepdims=True), sink))
    e = jnp.exp((s - m).astype(BF16))
    den = jnp.sum(e.astype(F32), axis=1, keepdims=True) + jnp.exp(sink - m)
    p = e * (1.0 / den).astype(BF16)
    zero = jnp.zeros_like(p)
    p_both = jnp.concatenate([jnp.where(from_cur, p, zero), jnp.where(from_cur, zero, p)], axis=1)
    o = jnp.swapaxes(_bmm(per_kv(vc, vp), p_both, B_TN), 1, 2)
    return jnp.concatenate([o[j, g * B_BLK:(g + 1) * B_BLK] for j in range(N_KV) for g in range(GROUP)], axis=1)


def swa_core_fwd(proj, sk, name):
    T = proj.shape[0]
    half = N_KV * B_HD

    def body(q_ref, kvc_ref, kvp_ref, sk_ref, o_ref):
        kvc, kvp = kvc_ref[...], kvp_ref[...]
        o_ref[...] = _swa_block(q_ref[...], kvp[:, :half], kvc[:, :half], kvp[:, half:], kvc[:, half:], sk_ref[...],
                                pl.program_id(0) == 0).astype(BF16)

    return _pc(body, name=name, out_shape=_sds((T, D), BF16), grid=(T // B_BLK,),
               in_specs=[pl.BlockSpec((B_BLK, D), lambda n: (n, 0)),
                         pl.BlockSpec((B_BLK, KV_COLS), lambda n: (n, D // KV_COLS)),
                         pl.BlockSpec((B_BLK, KV_COLS), lambda n: (jnp.maximum(n - 1, 0), D // KV_COLS)),
                         pl.BlockSpec((1, 128), lambda n: (0, 0))],
               out_specs=pl.BlockSpec((B_BLK, D), lambda n: (n, 0)), sem=("parallel",))(proj, proj, proj, sk)


def swa_core_bwd(proj, sk, do, name):
    T = proj.shape[0]
    last = T // B_BLK - 1
    half = N_KV * B_HD

    def body(q_ref, kvc_ref, kvp_ref, sk_ref, do_ref, dproj_ref, dbias_ref, dsk_ref, carry):
        step = pl.program_id(0)
        first = step == last

        @pl.when(step == 0)
        def _():
            carry[...] = jnp.zeros_like(carry)
            dbias_ref[...] = jnp.zeros_like(dbias_ref)
            dsk_ref[...] = jnp.zeros_like(dsk_ref)
        kvc, kvp = kvc_ref[...], kvp_ref[...]
        _, vjp = jax.vjp(functools.partial(_swa_block, first=first), q_ref[...], kvp[:, :half], kvc[:, :half],
                         kvp[:, half:], kvc[:, half:], sk_ref[...])
        dq, dkp, dkc, dvp, dvc, dsk = vjp(do_ref[...])
        dkv = jnp.concatenate([dkc, dvc], axis=1) + carry[...]
        carry[...] = jnp.concatenate([dkp, dvp], axis=1)
        row = jnp.concatenate([dq, dkv], axis=1)
        dproj_ref[...] = row.astype(BF16)
        dbias_ref[...] += jnp.sum(row, axis=0, keepdims=True)
        dsk_ref[...] += dsk

    return _pc(body, name=name, out_shape=(_sds((T, B_COLS), BF16), _sds((1, B_COLS), F32), _sds((1, 128), F32)),
               grid=(T // B_BLK,),
               in_specs=[pl.BlockSpec((B_BLK, D), lambda n: (last - n, 0)),
                         pl.BlockSpec((B_BLK, KV_COLS), lambda n: (last - n, D // KV_COLS)),
                         pl.BlockSpec((B_BLK, KV_COLS), lambda n: (jnp.maximum(last - n - 1, 0), D // KV_COLS)),
                         pl.BlockSpec((1, 128), lambda n: (0, 0)), pl.BlockSpec((B_BLK, D), lambda n: (last - n, 0))],
               out_specs=(pl.BlockSpec((B_BLK, B_COLS), lambda n: (last - n, 0)),
                          pl.BlockSpec((1, B_COLS), lambda n: (0, 0)), pl.BlockSpec((1, 128), lambda n: (0, 0))),
               scratch=[pltpu.VMEM((B_BLK, KV_COLS), F32)], sem=("arbitrary",))(proj, proj, proj, sk, do)


def col_sum(a, name):
    T, N = a.shape
    tm = _tile(T, 1024)

    def body(a_ref, o_ref):
        @pl.when(pl.program_id(0) == 0)
        def _():
            o_ref[...] = jnp.zeros_like(o_ref)
        o_ref[...] += jnp.sum(a_ref[...].astype(F32), axis=0, keepdims=True)

    return _pc(body, name=name, out_shape=_sds((1, N), F32), grid=(T // tm,),
               in_specs=[pl.BlockSpec((tm, N), lambda i: (i, 0))], out_specs=pl.BlockSpec((1, N), lambda i: (0, 0)),
               sem=("arbitrary",))(a)


def swa_forward(x, nw, w_in, b_in, sk, w_out, b_out, tag):
    h = rmsnorm_bf16(x, nw, f"{tag}_norm")
    proj = mm_nn(h, w_in, f"{tag}_proj", bias=b_in)
    o = swa_core_fwd(proj, sk, f"{tag}_core")
    y = mm_nn(o, w_out, f"{tag}_out", bias=b_out, residual=x)
    return y, (x, h, proj, o)


def swa_backward(dout, saved, nw, w_in, b_in, sk, w_out, b_out, tag):
    x, h, proj, o = saved
    do = mm_nt(dout, w_out, f"{tag}_out_bwd")
    dw_out = mm_tn(o, dout, f"{tag}_out_wgrad")
    db_out = col_sum(dout, f"{tag}_out_bias_grad")
    dproj, db_in, dsk = swa_core_bwd(proj, sk, do, f"{tag}_core_bwd")
    dw_in = mm_tn(h, dproj, f"{tag}_proj_wgrad")
    dh = mm_nt(dproj, w_in, f"{tag}_proj_bwd")
    dx, dnw = rmsnorm_bwd_add(x, nw, dh, dout, f"{tag}_norm_bwd")
    return dx, dnw, dw_in, db_in, dsk, dw_out, db_out


MESH = pl.DeviceIdType.MESH


def _position():
    return lax.axis_index("x"), lax.axis_index("y"), lax.axis_index("c")


def _slot(x, y, c):
    return 4 * x + 2 * y + c


def _peer(x, y, c, k):
    return (1 - x if k & 4 else x, 1 - y if k & 2 else y, 1 - c if k & 1 else c)


HBM_SPEC = pl.BlockSpec(memory_space=pltpu.HBM)
SEM_SPEC = pl.BlockSpec(memory_space=pltpu.SEMAPHORE)
DEP_SPEC = pl.BlockSpec(memory_space=pl.ANY)
SIDE_EFFECT = pltpu.SideEffectType.DATAFLOW_SIDE_EFFECTING
N_PEERS = N_DEV - 1


def _push_copies(srcs, lands, send_sems, recv_sems, scatter):
    x, y, c = _position()
    me = _slot(x, y, c)
    copies = []
    for k in (1, 2, 4, 3, 5, 6, 7):
        peer = _peer(x, y, c, k)
        for a in range(len(srcs)):
            copies.append(pltpu.make_async_remote_copy(
                src_ref=srcs[a].at[_slot(*peer)] if scatter else srcs[a], dst_ref=lands[a].at[me],
                send_sem=send_sems.at[N_PEERS * a + k - 1], recv_sem=recv_sems.at[N_PEERS * a + k - 1],
                device_id=peer, device_id_type=MESH))
    return copies


def push_start(srcs, lands, name, scatter, deps=()):
    n = len(srcs)
    first_out = 2 * n + len(deps)

    def body(*refs):
        for cp in _push_copies(refs[:n], refs[n:2 * n], refs[first_out], refs[first_out + 1], scatter):
            cp.start()
        refs[-1][...] = jnp.zeros_like(refs[-1])

    passed = [pltpu.HBM(t.shape, t.dtype) for t in list(srcs) + list(lands)]
    res = pl.pallas_call(
        body, name=name,
        out_shape=(pltpu.SemaphoreType.DMA((N_PEERS * n,)), pltpu.SemaphoreType.DMA((N_PEERS * n,)), *passed, _sds((8, 128), F32)),
        in_specs=[HBM_SPEC] * (2 * n) + [DEP_SPEC] * len(deps),
        out_specs=(SEM_SPEC, SEM_SPEC, *([HBM_SPEC] * (2 * n)), pl.BlockSpec(memory_space=pltpu.VMEM)),
        input_output_aliases={i: 2 + i for i in range(2 * n)},
        compiler_params=pltpu.CompilerParams(has_side_effects=SIDE_EFFECT),
    )(*[pltpu.with_memory_space_constraint(t, pltpu.HBM) for t in list(srcs) + list(lands)], *deps)
    return (res[0], res[1], list(res[2:2 + n]), list(res[2 + n:2 + 2 * n])), res[-1]


def push_wait(handles, after, name, scatter):
    send_sems, recv_sems, srcs, lands = handles
    n = len(srcs)
    after = tuple(after) if isinstance(after, (tuple, list)) else (after,)

    def body(*refs):
        for cp in _push_copies(refs[:n], refs[n:2 * n], refs[2 * n], refs[2 * n + 1], scatter):
            cp.wait_send()
            cp.wait_recv()

    res = pl.pallas_call(
        body, name=name, out_shape=tuple(pltpu.HBM(t.shape, t.dtype) for t in srcs + lands),
        in_specs=[HBM_SPEC] * (2 * n) + [SEM_SPEC, SEM_SPEC] + [DEP_SPEC] * len(after), out_specs=tuple([HBM_SPEC] * (2 * n)),
        input_output_aliases={i: i for i in range(2 * n)},
        compiler_params=pltpu.CompilerParams(has_side_effects=SIDE_EFFECT),
    )(*srcs, *lands, send_sems, recv_sems, *after)
    return list(res[n:])


def gather_start(shards, name, deps=()):
    me = _slot(*_position())
    lands = [lax.dynamic_update_slice(lax.empty((N_DEV,) + t.shape, t.dtype), t[None], (me,) + (0,) * t.ndim) for t in shards]
    return push_start(shards, lands, name, scatter=False, deps=deps)


def exchange_start(parts, name):
    me = _slot(*_position())
    lands = [lax.dynamic_update_slice(lax.empty(t.shape, t.dtype), lax.dynamic_index_in_dim(t, me, 0, keepdims=True),
                                      (me,) + (0,) * (t.ndim - 1)) for t in parts]
    return push_start(parts, lands, name, scatter=True)


def _row_tile(rows, cols):
    best = rows
    for t in range(16, rows, 16):
        if rows % t == 0 and t * cols * 4 <= (1 << 20):
            best = t
    return best


def adam_update(parts, w, m, v, name):
    n_layers = len(parts)
    P, R, C = parts[0].shape
    tr = _row_tile(R, C)
    n_t = R // tr

    def body(*refs):
        p_refs = refs[:n_layers]
        w_ref, m_ref, v_ref, g_ref, d_ref, nm_ref, nv_ref = refs[n_layers:]
        for layer in range(n_layers):
            @pl.when(pl.program_id(0) == layer)
            def _(p_ref=p_refs[layer]):
                g = p_ref[0].astype(F32)
                for s in range(1, P):
                    g = g + p_ref[s].astype(F32)
                new_m = ADAM_B1 * m_ref[0] + (1.0 - ADAM_B1) * g
                new_v = ADAM_B2 * v_ref[0] + (1.0 - ADAM_B2) * (g * g)
                m_hat = new_m / (1.0 - ADAM_B1 ** ADAM_STEP)
                v_hat = new_v / (1.0 - ADAM_B2 ** ADAM_STEP)
                g_ref[0] = g
                d_ref[0] = -ADAM_LR * (m_hat / (jnp.sqrt(v_hat) + ADAM_EPS) + ADAM_WD * w_ref[0])
                nm_ref[0] = new_m
                nv_ref[0] = new_v

    def part_spec(layer):
        return pl.BlockSpec((P, tr, C), lambda l_, i: (0, jnp.where(l_ == layer, i, jnp.where(l_ < layer, 0, n_t - 1)), 0))

    blk = pl.BlockSpec((1, tr, C), lambda l_, i: (l_, i, 0))
    out = _sds((n_layers, R, C), F32)
    return _pc(body, name=name, out_shape=(out, out, out, out), grid=(n_layers, n_t),
               in_specs=[part_spec(layer) for layer in range(n_layers)] + [blk, blk, blk],
               out_specs=(blk, blk, blk, blk), sem=("arbitrary", "arbitrary"))(*parts, w, m, v)


WEIGHTS = ("ffn1_norm", "ffn1_w_gu", "ffn1_w_down", "mix_norm", "ffn2_norm", "ffn2_w_gu", "ffn2_w_down", "a_w_in",
           "a_w_conv", "a_A_log", "a_dt_bias", "a_out_norm", "a_w_out", "b_w_in", "b_b_in", "b_sinks", "b_w_out",
           "b_b_out", "final_norm")
SHARDED = ("ffn1_w_gu", "ffn1_w_down", "ffn2_w_gu", "ffn2_w_down", "a_w_in", "a_w_conv", "a_w_out", "b_w_in", "b_b_in",
           "b_w_out", "b_b_out")
MISC_LANES = dict(a_A_log=(0, 8), a_dt_bias=(8, 16), b_sinks=(16, 32), a_out_norm=(128, 256))
LOSS_LANE = 256


def _pack_small(t):
    misc = jnp.zeros((D,), F32)
    for key, (lo, hi) in MISC_LANES.items():
        misc = misc.at[lo:hi].set(t[key].reshape(-1))
    if "loss" in t:
        misc = misc.at[LOSS_LANE].set(t["loss"])
    return jnp.concatenate([t["ffn1_norm"], t["mix_norm"], t["ffn2_norm"], t["final_norm"].reshape(1, D), misc[None]], axis=0)


def _unpack_small(p, like):
    out = dict(ffn1_norm=p[0:2], mix_norm=p[2:4], ffn2_norm=p[4:6], final_norm=p[6])
    for key, (lo, hi) in MISC_LANES.items():
        out[key] = p[7, lo:hi].reshape(like[key].shape)
    return out


def kernel(x, ffn1_norm, ffn1_w_gu, ffn1_w_down, mix_norm, ffn2_norm, ffn2_w_gu, ffn2_w_down, a_w_in, a_w_conv, a_A_log, a_dt_bias, a_out_norm, a_w_out, b_w_in, b_b_in, b_sinks, b_w_out, b_b_out, final_norm, loss_target, m_ffn1_norm, m_ffn1_w_gu, m_ffn1_w_down, m_mix_norm, m_ffn2_norm, m_ffn2_w_gu, m_ffn2_w_down, m_a_w_in, m_a_w_conv, m_a_A_log, m_a_dt_bias, m_a_out_norm, m_a_w_out, m_b_w_in, m_b_b_in, m_b_sinks, m_b_w_out, m_b_b_out, m_final_norm, v_ffn1_norm, v_ffn1_w_gu, v_ffn1_w_down, v_mix_norm, v_ffn2_norm, v_ffn2_w_gu, v_ffn2_w_down, v_a_w_in, v_a_w_conv, v_a_A_log, v_a_dt_bias, v_a_out_norm, v_a_w_out, v_b_w_in, v_b_b_in, v_b_sinks, v_b_w_out, v_b_b_out, v_final_norm):
    w = dict(ffn1_norm=ffn1_norm, ffn1_w_gu=ffn1_w_gu, ffn1_w_down=ffn1_w_down, mix_norm=mix_norm, ffn2_norm=ffn2_norm, ffn2_w_gu=ffn2_w_gu, ffn2_w_down=ffn2_w_down, a_w_in=a_w_in, a_w_conv=a_w_conv, a_A_log=a_A_log, a_dt_bias=a_dt_bias, a_out_norm=a_out_norm, a_w_out=a_w_out, b_w_in=b_w_in, b_b_in=b_b_in, b_sinks=b_sinks, b_w_out=b_w_out, b_b_out=b_b_out, final_norm=final_norm)
    m = dict(ffn1_norm=m_ffn1_norm, ffn1_w_gu=m_ffn1_w_gu, ffn1_w_down=m_ffn1_w_down, mix_norm=m_mix_norm, ffn2_norm=m_ffn2_norm, ffn2_w_gu=m_ffn2_w_gu, ffn2_w_down=m_ffn2_w_down, a_w_in=m_a_w_in, a_w_conv=m_a_w_conv, a_A_log=m_a_A_log, a_dt_bias=m_a_dt_bias, a_out_norm=m_a_out_norm, a_w_out=m_a_w_out, b_w_in=m_b_w_in, b_b_in=m_b_b_in, b_sinks=m_b_sinks, b_w_out=m_b_w_out, b_b_out=m_b_b_out, final_norm=m_final_norm)
    v = dict(ffn1_norm=v_ffn1_norm, ffn1_w_gu=v_ffn1_w_gu, ffn1_w_down=v_ffn1_w_down, mix_norm=v_mix_norm, ffn2_norm=v_ffn2_norm, ffn2_w_gu=v_ffn2_w_gu, ffn2_w_down=v_ffn2_w_down, a_w_in=v_a_w_in, a_w_conv=v_a_w_conv, a_A_log=v_a_A_log, a_dt_bias=v_a_dt_bias, a_out_norm=v_a_out_norm, a_w_out=v_a_w_out, b_w_in=v_b_w_in, b_b_in=v_b_b_in, b_sinks=v_b_sinks, b_w_out=v_b_w_out, b_b_out=v_b_b_out, final_norm=v_final_norm)
    T = x.shape[1]
    x0, tgt = x.reshape(T, D), loss_target.reshape(T, D)

    def cast(t):
        return t.astype(BF16)

    h0, t0 = gather_start([cast(ffn1_w_gu[0])], "gather0_start")
    a_log_row = jnp.zeros((1, 128), F32).at[0, HEADS_A:2 * HEADS_A].set(a_A_log[0])
    dt_row = jnp.zeros((1, 128), F32).at[0, HEADS_A:2 * HEADS_A].set(a_dt_bias[0])
    sink_row = jnp.zeros((1, 128), F32).at[0, :b_sinks.shape[1]].set(b_sinks[0])
    a_in_cols = a_w_in.shape[-1] * N_DEV

    def down_blocks(t):
        return t.reshape(N_FB, FB, D)

    wgu, wdn, saved = {}, {}, []
    xn = rmsnorm_bf16(x0, ffn1_norm[0:1], "l0_ffn1_norm", (t0,))
    wgu["ffn1", 0] = push_wait(h0, xn, "gather0_wait", scatter=False)[0]
    h0d, t0d = gather_start([cast(ffn1_w_down[0])], "gather0d_start", deps=(wgu["ffn1", 0],))
    h1, t1 = gather_start([cast(a_w_in[0]), a_w_conv[0], cast(a_w_out[0])], "gather1_start", deps=(t0d,))
    gu = ffn_up(xn, wgu["ffn1", 0], "l0_ffn1_up", deps=(t0d, t1))
    wdn["ffn1", 0] = down_blocks(push_wait(h0d, gu, "gather0d_wait", scatter=False)[0])
    xs, s1 = ffn_down(gu, wdn["ffn1", 0], x0, "l0_ffn1_down"), (x0, xn, gu)
    got = push_wait(h1, xs, "gather1_wait", scatter=False)
    h1f, t1f = gather_start([cast(ffn2_w_gu[0]), cast(ffn2_w_down[0])], "gather1f_start", deps=(got[0],))
    g2 = [cast(ffn1_w_gu[1]), cast(ffn1_w_down[1]), cast(b_w_in[0]), b_b_in, cast(b_w_out[0]), b_b_out,
          cast(ffn2_w_gu[1]), cast(ffn2_w_down[1])]
    h2, t2 = gather_start(g2, "gather2_start", deps=(t1f,))
    a_in_full = jnp.pad(got[0].transpose(1, 0, 2).reshape(D, a_in_cols), ((0, 0), (0, A_COLS - a_in_cols)))
    gdn_args = (mix_norm[0:1], a_in_full, got[1].transpose(1, 0, 2).reshape(4, 3 * D), a_log_row, dt_row, a_out_norm,
                got[2].reshape(D, D))
    xs, sm = gdn_forward(xs, *gdn_args, "gdn", deps=(t1f, t2))
    got = push_wait(h1f, xs, "gather1f_wait", scatter=False)
    wgu["ffn2", 0], wdn["ffn2", 0] = got[0], down_blocks(got[1])
    xs, s2 = ffn_forward(xs, ffn2_norm[0:1], wgu["ffn2", 0], wdn["ffn2", 0], "l0_ffn2")
    saved.append((s1, sm, s2))
    got = push_wait(h2, xs, "gather2_wait", scatter=False)
    wgu["ffn1", 1], wdn["ffn1", 1] = got[0], down_blocks(got[1])
    swa_args = (mix_norm[1:2], got[2].transpose(1, 0, 2).reshape(D, B_COLS), got[3].reshape(1, B_COLS), sink_row,
                got[4].reshape(D, D), got[5].reshape(1, D))
    wgu["ffn2", 1], wdn["ffn2", 1] = got[6], down_blocks(got[7])
    xs, s1 = ffn_forward(xs, ffn1_norm[1:2], wgu["ffn1", 1], wdn["ffn1", 1], "l1_ffn1")
    xs, sm = swa_forward(xs, *swa_args, "swa")
    xs, s2 = ffn_forward(xs, ffn2_norm[1:2], wgu["ffn2", 1], wdn["ffn2", 1], "l1_ffn2")
    saved.append((s1, sm, s2))
    loss_row, dx, d_final_norm = final_loss(xs, final_norm.reshape(1, D), tgt, "final_loss")

    def down_slots(t):
        return cast(t.reshape(N_DEV, FB // 2, D))

    def col_slots(t, dtype=BF16):
        return t.reshape(t.shape[0], N_DEV, -1).transpose(1, 0, 2).astype(dtype)

    d_norm = {"ffn1_norm": [None, None], "mix_norm": [None, None], "ffn2_norm": [None, None]}
    exchanges = {}

    def sender(tag):
        def on_grads(d_gu, d_dn):
            exchanges[tag], token = exchange_start([cast(d_gu), down_slots(d_dn)], f"exchange_{tag}_start")
            return (token,)
        return on_grads

    s1, sm, s2 = saved[1]
    dx, d_norm["ffn2_norm"][1], _, _ = ffn_backward(dx, s2, ffn2_norm[1:2], wgu["ffn2", 1], wdn["ffn2", 1], "l1_ffn2",
                                                    on_grads=sender("l1_ffn2"))
    dx, d_norm["mix_norm"][1], d_b_in, d_b_bias_in, d_sinks, d_b_out, d_b_bias_out = swa_backward(dx, sm, *swa_args, "swa")
    exchanges["swa"], t_swa = exchange_start(
        [col_slots(d_b_in), d_b_bias_in.reshape(N_DEV, 1, -1), cast(d_b_out.reshape(N_DEV, D // N_DEV, D)),
         d_b_bias_out.reshape(N_DEV, 1, -1)], "exchange_swa_start")
    dx, d_norm["ffn1_norm"][1], _, _ = ffn_backward(dx, s1, ffn1_norm[1:2], wgu["ffn1", 1], wdn["ffn1", 1], "l1_ffn1",
                                                    deps=(t_swa,), on_grads=sender("l1_ffn1"))

    s1, sm, s2 = saved[0]
    dx, d_norm["ffn2_norm"][0], _, _ = ffn_backward(dx, s2, ffn2_norm[0:1], wgu["ffn2", 0], wdn["ffn2", 0], "l0_ffn2",
                                                    on_grads=sender("l0_ffn2"))
    dx, d_norm["mix_norm"][0], d_a_in, d_a_conv, d_alog, d_dt, d_onorm, d_a_out = gdn_backward(dx, sm, *gdn_args, "gdn")
    exchanges["gdn"], t_gdn = exchange_start(
        [col_slots(d_a_in[:, :a_in_cols]), col_slots(d_a_conv, F32), cast(d_a_out.reshape(N_DEV, D // N_DEV, D))],
        "exchange_gdn_start")
    dx, d_norm["ffn1_norm"][0], _, _ = ffn_backward(dx, s1, ffn1_norm[0:1], wgu["ffn1", 0], wdn["ffn1", 0], "l0_ffn1",
                                                    deps=(t_gdn,), on_grads=sender("l0_ffn1"))
    grad_x = dx.reshape(x.shape)
    got = {tag: push_wait(exchanges[tag], dx, f"exchange_{tag}_wait", scatter=True)
           for tag in ("l1_ffn2", "swa", "l1_ffn1", "l0_ffn2", "gdn")}
    received = dict(ffn2_w_gu=[got["l0_ffn2"][0], got["l1_ffn2"][0]], ffn2_w_down=[got["l0_ffn2"][1], got["l1_ffn2"][1]],
                    b_w_in=[got["swa"][0]], b_b_in=[got["swa"][1]], b_w_out=[got["swa"][2]], b_b_out=[got["swa"][3]],
                    a_w_in=[got["gdn"][0]], a_w_conv=[got["gdn"][1]], a_w_out=[got["gdn"][2]])

    grads, deltas, new_m, new_v = {}, {}, {}, {}

    def update(key):
        shape = w[key].shape
        cols = shape[-1]
        layers = lambda t: t.reshape(shape[0], -1, cols)
        out = adam_update([r.reshape(N_DEV, -1, cols) for r in received[key]], layers(w[key]), layers(m[key]), layers(v[key]),
                          f"adam_{key}")
        grads[key], deltas[key], new_m[key], new_v[key] = (t.reshape(shape) for t in out)

    for key in SHARDED:
        if key in received:
            update(key)
    done_first = [deltas[key] for key in received]

    small = dict(ffn1_norm=jnp.concatenate(d_norm["ffn1_norm"], axis=0), mix_norm=jnp.concatenate(d_norm["mix_norm"], axis=0),
                 ffn2_norm=jnp.concatenate(d_norm["ffn2_norm"], axis=0), final_norm=d_final_norm,
                 a_A_log=d_alog[0, HEADS_A:2 * HEADS_A], a_dt_bias=d_dt[0, HEADS_A:2 * HEADS_A],
                 b_sinks=d_sinks[0, :b_sinks.shape[1]], a_out_norm=d_onorm, loss=loss_row[0, 0])
    hs, ts = gather_start([_pack_small(small)], "gather_small_start")
    r3 = push_wait(exchanges["l0_ffn1"], done_first + [ts], "exchange_l0_ffn1_wait", scatter=True)
    received.update(ffn1_w_gu=[r3[0], got["l1_ffn1"][0]], ffn1_w_down=[r3[1], got["l1_ffn1"][1]])
    update("ffn1_w_gu")
    update("ffn1_w_down")
    every = push_wait(hs, deltas["ffn1_w_down"], "gather_small_wait", scatter=False)[0]
    out = adam_update([every], _pack_small(w)[None], _pack_small(m)[None], _pack_small(v)[None], "adam_small")
    for dst, packed in zip((grads, deltas, new_m, new_v), out):
        dst.update(_unpack_small(packed[0], w))
    loss = out[0][0, 7, LOSS_LANE]

    return (loss, grad_x, *[grads[k_] for k_ in WEIGHTS], *[deltas[k_] for k_ in WEIGHTS],
            *[new_m[k_] for k_ in WEIGHTS], *[new_v[k_] for k_ in WEIGHTS])
```

```python
import functools

import jax
import jax.numpy as jnp
from jax import lax
from jax.experimental import pallas as pl
from jax.experimental.pallas import tpu as pltpu

F32, BF16 = jnp.float32, jnp.bfloat16
HI = lax.Precision.HIGHEST
EPS = 1e-6

N_DEV = 8
D = 1024
FB = 704
N_FB = 4
HEADS_A, DK = 8, 128
CHUNK = 64
PREP_T = 512
A_COLS = 4224
B_HD, B_BLK = 64, 128
VMEM_LIMIT_V7X = 60 * 1024 * 1024

ADAM_LR, ADAM_B1, ADAM_B2, ADAM_EPS, ADAM_WD, ADAM_STEP = 0.001, 0.9, 0.999, 1e-08, 0.01, 10

NT = (((1,), (1,)), ((), ()))
TN = (((0,), (0,)), ((), ()))


def _pc(body, *, name, out_shape, grid=(), in_specs=None, out_specs=None, scratch=(), sem=None, **kw):
    params = pltpu.CompilerParams(dimension_semantics=sem, vmem_limit_bytes=VMEM_LIMIT_V7X)
    return pl.pallas_call(body, name=name, out_shape=out_shape, grid=grid, in_specs=in_specs, out_specs=out_specs,
                          scratch_shapes=list(scratch), compiler_params=params, **kw)


def _sds(shape, dtype):
    return jax.ShapeDtypeStruct(tuple(shape), dtype)


def _dot(a, b, dims=None, precision=None):
    if dims is None:
        return jnp.dot(a, b, preferred_element_type=F32, precision=precision)
    return lax.dot_general(a, b, dims, preferred_element_type=F32, precision=precision)


def _sigmoid(x):
    return 1.0 / (1.0 + jnp.exp(-x))


def _softplus(x):
    return jnp.maximum(x, 0.0) + jnp.log(1.0 + jnp.exp(-jnp.abs(x)))


def _rms_fwd(x, w):
    r = lax.rsqrt(jnp.mean(x * x, axis=-1, keepdims=True) + EPS)
    return x * r * w


def _rms_bwd(x, w, dy):
    r = lax.rsqrt(jnp.mean(x * x, axis=-1, keepdims=True) + EPS)
    xh = x * r
    dxh = dy * w
    dx = r * (dxh - xh * jnp.mean(dxh * xh, axis=-1, keepdims=True))
    return dx, jnp.sum(dy * xh, axis=0, keepdims=True)


def _tile(n, want):
    t = min(n, want)
    assert n % t == 0, (n, want)
    return t


def rmsnorm_bf16(x, w, name, deps=()):
    T = x.shape[0]
    tm = _tile(T, 1024)

    def body(x_ref, w_ref, *rest):
        rest[-1][...] = _rms_fwd(x_ref[...], w_ref[...]).astype(BF16)

    return _pc(body, name=name, out_shape=_sds((T, D), BF16), grid=(T // tm,),
               in_specs=[pl.BlockSpec((tm, D), lambda i: (i, 0)), pl.BlockSpec((1, D), lambda i: (0, 0))] + [DEP_SPEC] * len(deps),
               out_specs=pl.BlockSpec((tm, D), lambda i: (i, 0)), sem=("parallel",))(x, w, *deps)


def final_loss(x, w, tgt, name):
    T = x.shape[0]
    tm = _tile(T, 512)

    def body(x_ref, w_ref, t_ref, loss_ref, dx_ref, dw_ref):
        xv, wv = x_ref[...], w_ref[...]
        err = _rms_fwd(xv, wv) - t_ref[...]
        dx, dw = _rms_bwd(xv, wv, err * (1.0 / D))
        dx_ref[...] = dx

        @pl.when(pl.program_id(0) == 0)
        def _():
            dw_ref[...] = jnp.zeros_like(dw_ref)
            loss_ref[...] = jnp.zeros_like(loss_ref)
        dw_ref[...] += dw
        loss_ref[...] += jnp.full((1, 128), 0.5 / D, F32) * jnp.sum(err * err)

    row = pl.BlockSpec((tm, D), lambda i: (i, 0))
    vec = pl.BlockSpec((1, D), lambda i: (0, 0))
    return _pc(body, name=name, out_shape=(_sds((1, 128), F32), _sds((T, D), F32), _sds((1, D), F32)),
               grid=(T // tm,), in_specs=[row, vec, row],
               out_specs=(pl.BlockSpec((1, 128), lambda i: (0, 0)), row, vec), sem=("arbitrary",))(x, w, tgt)


def _col_tile(n):
    for t in (1536, 1408, 1024, 768, 512, 384, 256, 128):
        if n % t == 0:
            return t
    return n


def mm_nn(a, b, name, bias=None, residual=None, out_dtype=F32, cols=None):
    T, K = a.shape
    first, end = cols or (0, b.shape[1])
    N = end - first
    tm, tn = _tile(T, 512), _col_tile(N)
    assert first % tn == 0 and (cols is None or (bias is None and residual is None))
    j0 = first // tn

    def body(a_ref, b_ref, *rest):
        o_ref = rest[-1]
        acc = _dot(a_ref[...].astype(BF16), b_ref[...])
        for extra in rest[:-1]:
            acc = acc + extra[...]
        o_ref[...] = acc.astype(out_dtype)

    in_specs = [pl.BlockSpec((tm, K), lambda j, i: (i, 0)), pl.BlockSpec((K, tn), lambda j, i: (0, j0 + j))]
    args = [a, b]
    if bias is not None:
        in_specs.append(pl.BlockSpec((1, tn), lambda j, i: (0, j)))
        args.append(bias)
    if residual is not None:
        in_specs.append(pl.BlockSpec((tm, tn), lambda j, i: (i, j)))
        args.append(residual)
    return _pc(body, name=name, out_shape=_sds((T, N), out_dtype), grid=(N // tn, T // tm), in_specs=in_specs,
               out_specs=pl.BlockSpec((tm, tn), lambda j, i: (i, j)), sem=("parallel", "parallel"))(*args)


def mm_nt(a, b, name, out_dtype=F32, norm_bwd=None):
    T, N = a.shape
    K = b.shape[0]
    tm = _tile(T, 512)
    row = pl.BlockSpec((tm, K), lambda i: (i, 0))
    in_specs = [pl.BlockSpec((tm, N), lambda i: (i, 0)), _resident((K, N))]

    if norm_bwd is None:
        def body(a_ref, b_ref, o_ref):
            o_ref[...] = _dot(a_ref[...].astype(BF16), b_ref[...], NT).astype(out_dtype)

        return _pc(body, name=name, out_shape=_sds((T, K), out_dtype), grid=(T // tm,), in_specs=in_specs,
                   out_specs=row, sem=("parallel",))(a, b)

    def body(a_ref, b_ref, x_ref, w_ref, dres_ref, dx_ref, dw_ref):
        dx, dw = _rms_bwd(x_ref[...], w_ref[...], _dot(a_ref[...].astype(BF16), b_ref[...], NT))
        dx_ref[...] = dres_ref[...] + dx

        @pl.when(pl.program_id(0) == 0)
        def _():
            dw_ref[...] = jnp.zeros_like(dw_ref)
        dw_ref[...] += dw

    vec = pl.BlockSpec((1, K), lambda i: (0, 0))
    return _pc(body, name=name, out_shape=(_sds((T, K), F32), _sds((1, K), F32)), grid=(T // tm,),
               in_specs=in_specs + [row, vec, row], out_specs=(row, vec), sem=("arbitrary",))(a, b, *norm_bwd)


def mm_tn(a, b, name):
    T, K = a.shape
    N = b.shape[1]
    tt, tn = _tile(T, 1024), _col_tile(N)

    def body(a_ref, b_ref, o_ref):
        @pl.when(pl.program_id(1) == 0)
        def _():
            o_ref[...] = jnp.zeros_like(o_ref)
        o_ref[...] += _dot(a_ref[...].astype(BF16), b_ref[...].astype(BF16), TN)

    return _pc(body, name=name, out_shape=_sds((K, N), F32), grid=(N // tn, T // tt),
               in_specs=[pl.BlockSpec((tt, K), lambda j, t: (t, 0)), pl.BlockSpec((tt, tn), lambda j, t: (t, j))],
               out_specs=pl.BlockSpec((K, tn), lambda j, t: (0, j)), sem=("parallel", "arbitrary"))(a, b)


def ffn_up(xn, wgu, name, deps=()):
    T = xn.shape[0]
    tm = _tile(T, 1024)

    def body(x_ref, w_ref, *rest):
        xv = x_ref[...]
        for j in range(2 * N_FB):
            rest[-1][j] = _dot(xv, w_ref[j]).astype(BF16)

    return _pc(body, name=name, out_shape=_sds((2 * N_FB, T, FB), BF16), grid=(T // tm,),
               in_specs=[pl.BlockSpec((tm, D), lambda i: (i, 0)), _resident((2 * N_FB, D, FB))] + [DEP_SPEC] * len(deps),
               out_specs=pl.BlockSpec((2 * N_FB, tm, FB), lambda i: (0, i, 0)), sem=("parallel",))(xn, wgu, *deps)


def ffn_down(gu, wd, x, name):
    T = x.shape[0]
    tm = _tile(T, 512)

    def body(gu_ref, w_ref, x_ref, o_ref):
        acc = jnp.zeros((tm, D), F32)
        for g in range(N_FB):
            gate, up = gu_ref[g], gu_ref[N_FB + g]
            acc = acc + _dot(gate * _sigmoid(gate) * up, w_ref[g])
        o_ref[...] = x_ref[...] + 0.5 * acc

    row = pl.BlockSpec((tm, D), lambda i: (i, 0))
    return _pc(body, name=name, out_shape=_sds((T, D), F32), grid=(T // tm,),
               in_specs=[pl.BlockSpec((2 * N_FB, tm, FB), lambda i: (0, i, 0)),
                         _resident((N_FB, FB, D)), row],
               out_specs=row, sem=("parallel",))(gu, wd, x)


def _resident(shape):
    return pl.BlockSpec(shape, lambda *_: (0,) * len(shape), pipeline_mode=pl.Buffered(1))


def _store_blocks_bf16(acc, out_hbm, stage, sem):
    for j in range(acc.shape[0]):
        stage[...] = acc[j].astype(BF16)
        copy = pltpu.make_async_copy(stage, out_hbm.at[j], sem)
        copy.start()
        copy.wait()


def ffn_bwd_hidden(dout, wd, gu, name, deps=()):
    T = dout.shape[0]
    tm = _tile(T, 512)
    n_t = T // tm

    def body(d_ref, w_ref, gu_ref, *rest):
        dgu_ref, dwd_hbm, acc, stage, sem = rest[-5:]
        t = pl.program_id(0)

        @pl.when(t == 0)
        def _():
            acc[...] = jnp.zeros_like(acc)
        dy = (0.5 * d_ref[...]).astype(BF16)
        for g in range(N_FB):
            gate, up = gu_ref[g], gu_ref[N_FB + g]
            sg = _sigmoid(gate)
            silu = gate * sg
            dact = _dot(dy, w_ref[g], NT).astype(BF16)
            acc[g] += _dot(silu * up, dy, TN)
            dgu_ref[g] = dact * up * (sg * (1.0 + gate * (1.0 - sg)))
            dgu_ref[N_FB + g] = dact * silu

        @pl.when(t == n_t - 1)
        def _():
            _store_blocks_bf16(acc, dwd_hbm, stage, sem)

    return _pc(body, name=name, out_shape=(_sds((2 * N_FB, T, FB), BF16), _sds((N_FB, FB, D), BF16)), grid=(n_t,),
               in_specs=[pl.BlockSpec((tm, D), lambda i: (i, 0)), _resident((N_FB, FB, D)),
                         pl.BlockSpec((2 * N_FB, tm, FB), lambda i: (0, i, 0))] + [DEP_SPEC] * len(deps),
               out_specs=(pl.BlockSpec((2 * N_FB, tm, FB), lambda i: (0, i, 0)), pl.BlockSpec(memory_space=pl.ANY)),
               scratch=[pltpu.VMEM((N_FB, FB, D), F32), pltpu.VMEM((FB, D), BF16), pltpu.SemaphoreType.DMA],
               sem=("arbitrary",))(dout, wd, gu, *deps)


def ffn_bwd_input(dgu, wgu, x, dout, nw, name, deps=()):
    T = x.shape[0]
    tm = _tile(T, 512)

    def body(dgu_ref, w_ref, x_ref, d_ref, nw_ref, *rest):
        dx_ref, dnw_ref = rest[-2:]
        dxn = jnp.zeros((tm, D), F32)
        for j in range(2 * N_FB):
            dxn = dxn + _dot(dgu_ref[j], w_ref[j], NT)
        dx, dw = _rms_bwd(x_ref[...], nw_ref[...], dxn)
        dx_ref[...] = d_ref[...] + dx

        @pl.when(pl.program_id(0) == 0)
        def _():
            dnw_ref[...] = jnp.zeros_like(dnw_ref)
        dnw_ref[...] += dw

    row = pl.BlockSpec((tm, D), lambda i: (i, 0))
    vec = pl.BlockSpec((1, D), lambda i: (0, 0))
    return _pc(body, name=name, out_shape=(_sds((T, D), F32), _sds((1, D), F32)), grid=(T // tm,),
               in_specs=[pl.BlockSpec((2 * N_FB, tm, FB), lambda i: (0, i, 0)), _resident((2 * N_FB, D, FB)),
                         row, row, vec] + [DEP_SPEC] * len(deps),
               out_specs=(row, vec), sem=("arbitrary",))(dgu, wgu, x, dout, nw, *deps)


def ffn_wgrad_gu(xn, dgu, name):
    T = xn.shape[0]
    tt = _tile(T, 1024)
    n_t = T // tt

    def body(x_ref, d_ref, dw_hbm, acc, stage, sem):
        t = pl.program_id(0)

        @pl.when(t == 0)
        def _():
            acc[...] = jnp.zeros_like(acc)
        xn_tile = x_ref[...]
        for j in range(2 * N_FB):
            acc[j] += _dot(xn_tile, d_ref[j], TN)

        @pl.when(t == n_t - 1)
        def _():
            _store_blocks_bf16(acc, dw_hbm, stage, sem)

    return _pc(body, name=name, out_shape=_sds((2 * N_FB, D, FB), BF16), grid=(n_t,),
               in_specs=[pl.BlockSpec((tt, D), lambda t: (t, 0)), pl.BlockSpec((2 * N_FB, tt, FB), lambda t: (0, t, 0))],
               out_specs=pl.BlockSpec(memory_space=pl.ANY),
               scratch=[pltpu.VMEM((2 * N_FB, D, FB), F32), pltpu.VMEM((D, FB), BF16), pltpu.SemaphoreType.DMA],
               sem=("arbitrary",))(xn, dgu)


def ffn_forward(x, nw, wgu, wd, tag):
    T = x.shape[0]
    tm = _tile(T, 512)

    def body(x_ref, nw_ref, wgu_ref, wd_ref, o_ref, xn_ref, gu_ref):
        xv = x_ref[...]
        xn = _rms_fwd(xv, nw_ref[...]).astype(BF16)
        xn_ref[...] = xn
        for j in range(2 * N_FB):
            gu_ref[j] = _dot(xn, wgu_ref[j]).astype(BF16)
        acc = jnp.zeros((tm, D), F32)
        for g in range(N_FB):
            gate, up = gu_ref[g], gu_ref[N_FB + g]
            acc = acc + _dot(gate * _sigmoid(gate) * up, wd_ref[g])
        o_ref[...] = xv + 0.5 * acc

    row = pl.BlockSpec((tm, D), lambda i: (i, 0))
    out, xn, gu = _pc(body, name=f"{tag}_fwd",
                      out_shape=(_sds((T, D), F32), _sds((T, D), BF16), _sds((2 * N_FB, T, FB), BF16)), grid=(T // tm,),
                      in_specs=[row, pl.BlockSpec((1, D), lambda i: (0, 0)), _resident((2 * N_FB, D, FB)),
                                _resident((N_FB, FB, D))],
                      out_specs=(row, row, pl.BlockSpec((2 * N_FB, tm, FB), lambda i: (0, i, 0))),
                      sem=("parallel",))(x, nw, wgu, wd)
    return out, (x, xn, gu)


def ffn_backward(dout, saved, nw, wgu, wd, tag, deps=(), on_grads=None):
    x, xn, gu = saved
    dgu, dwd = ffn_bwd_hidden(dout, wd, gu, f"{tag}_bwd_hidden", deps)
    dwgu = ffn_wgrad_gu(xn, dgu, f"{tag}_wgrad_gu")
    late = on_grads(dwgu, dwd) if on_grads else ()
    dx, dnw = ffn_bwd_input(dgu, wgu, x, dout, nw, f"{tag}_bwd_input", late)
    return dx, dnw, dwgu, dwd


N_QKV_BLK = 3 * HEADS_A
Z_BLK0 = N_QKV_BLK
MAIN_COLS = 4 * D
HALO = 16


def _conv_taps(xcat, w):
    c = xcat[HALO:] * w[3:4]
    for k in range(3):
        c = c + pltpu.roll(xcat, 3 - k, 0)[HALO:] * w[k:k + 1]
    return c


def _head_cols(h):
    return slice(128 * h, 128 * (h + 1))


def gdn_conv_fwd(proj, wconv, name):
    T = proj.shape[0]
    tm = _tile(T, 512)

    def body(cur_ref, prev_ref, w_ref, c_ref, y_ref):
        kind, t = pl.program_id(0), pl.program_id(1)
        prev = jnp.where(t > 0, prev_ref[...].astype(F32), 0.0)
        c = _conv_taps(jnp.concatenate([prev, cur_ref[...].astype(F32)], axis=0), w_ref[...])
        c_ref[...] = c.astype(BF16)
        s = c * _sigmoid(c)
        scale = jnp.where(kind == 0, DK ** -0.5, 1.0)
        for h in range(HEADS_A):
            sh = s[:, _head_cols(h)]
            r = lax.rsqrt(jnp.sum(sh * sh, axis=-1, keepdims=True) + EPS)
            y_ref[h] = (sh * jnp.where(kind < 2, r * scale, 1.0)).astype(BF16)

    return _pc(body, name=name, out_shape=(_sds((T, 3 * D), BF16), _sds((N_QKV_BLK, T, 128), BF16)),
               grid=(3, T // tm),
               in_specs=[pl.BlockSpec((tm, D), lambda kd, t: (t, kd)),
                         pl.BlockSpec((HALO, D), lambda kd, t: (jnp.maximum(t * (tm // HALO) - 1, 0), kd)),
                         pl.BlockSpec((4, D), lambda kd, t: (0, kd))],
               out_specs=(pl.BlockSpec((tm, D), lambda kd, t: (t, kd)),
                          pl.BlockSpec((HEADS_A, tm, 128), lambda kd, t: (kd, t, 0))),
               sem=("parallel", "parallel"))(proj, proj, wconv)


def gdn_conv_bwd(dqkv, c, proj, wconv, name):
    T = c.shape[0]
    tm = _tile(T, 512)
    n_t = T // tm

    def body(dy_ref, dyn_ref, c_ref, cn_ref, x_ref, w_ref, dx_ref, dw_ref):
        kind, t = pl.program_id(0), pl.program_id(1)
        scale = jnp.where(kind == 0, DK ** -0.5, 1.0)

        def act_bwd(dy, cv):
            sg = _sigmoid(cv)
            s = cv * sg
            parts = []
            for h in range(HEADS_A):
                sh, dyh = s[:, _head_cols(h)], dy[h]
                r = lax.rsqrt(jnp.sum(sh * sh, axis=-1, keepdims=True) + EPS)
                ds_norm = scale * r * (dyh - (r * r) * sh * jnp.sum(dyh * sh, axis=-1, keepdims=True))
                parts.append(jnp.where(kind < 2, ds_norm, dyh))
            return jnp.concatenate(parts, axis=1) * (sg * (1.0 + cv * (1.0 - sg)))

        w = w_ref[...]
        dcur = act_bwd(dy_ref[...].astype(F32), c_ref[...].astype(F32))
        dnext = jnp.where(t < n_t - 1, act_bwd(dyn_ref[...].astype(F32), cn_ref[...].astype(F32)), 0.0)
        dcat = jnp.concatenate([dcur, dnext], axis=0)
        xcur = x_ref[...].astype(F32)
        dx = dcur * w[3:4]
        rows = [None, None, None, jnp.sum(dcur * xcur, axis=0, keepdims=True)]
        for k in range(3):
            ahead = pltpu.roll(dcat, tm + HALO - (3 - k), 0)[:tm]
            dx = dx + ahead * w[k:k + 1]
            rows[k] = jnp.sum(ahead * xcur, axis=0, keepdims=True)
        dx_ref[...] = dx.astype(BF16)

        @pl.when(t == 0)
        def _():
            dw_ref[...] = jnp.zeros_like(dw_ref)
        dw_ref[...] += jnp.concatenate(rows, axis=0)

    def nxt(t):
        return jnp.minimum((t + 1) * (tm // HALO), T // HALO - 1)

    cur = pl.BlockSpec((tm, D), lambda kd, t: (t, kd))
    return _pc(body, name=name, out_shape=(_sds((T, 3 * D), BF16), _sds((4, 3 * D), F32)), grid=(3, n_t),
               in_specs=[pl.BlockSpec((HEADS_A, tm, 128), lambda kd, t: (kd, t, 0)),
                         pl.BlockSpec((HEADS_A, HALO, 128), lambda kd, t: (kd, nxt(t), 0)),
                         cur, pl.BlockSpec((HALO, D), lambda kd, t: (nxt(t), kd)),
                         cur, pl.BlockSpec((4, D), lambda kd, t: (0, kd))],
               out_specs=(cur, pl.BlockSpec((4, D), lambda kd, t: (0, kd))),
               sem=("parallel", "arbitrary"))(dqkv, dqkv, c, c, proj, wconv)


def _chunk_masks(n):
    ri = lax.broadcasted_iota(jnp.int32, (n, n), 0)
    ci = lax.broadcasted_iota(jnp.int32, (n, n), 1)
    same = (ri // CHUNK) == (ci // CHUNK)
    return same & (ri >= ci), same & (ri <= ci)


def gdn_gate_fwd(ba, al, dtb, name):
    T = ba.shape[0]
    tg = _tile(T, PREP_T)

    def body(ba_ref, al_ref, dtb_ref, o_ref):
        x = ba_ref[...]
        lane = lax.broadcasted_iota(jnp.int32, x.shape, 1)
        is_a = (lane >= HEADS_A) & (lane < 2 * HEADS_A)
        g = jnp.where(is_a, -jnp.exp(al_ref[...]) * _softplus(x + dtb_ref[...]), 0.0)
        lower, _ = _chunk_masks(tg)
        gc = _dot(lower.astype(F32), g, precision=HI)
        o_ref[...] = jnp.where(lane < HEADS_A, _sigmoid(x), gc)

    vec = pl.BlockSpec((1, 128), lambda i: (0, 0))
    return _pc(body, name=name, out_shape=_sds((T, 128), F32), grid=(T // tg,),
               in_specs=[pl.BlockSpec((tg, 128), lambda i: (i, 0)), vec, vec],
               out_specs=pl.BlockSpec((tg, 128), lambda i: (i, 0)), sem=("parallel",))(ba, al, dtb)


def gdn_gate_bwd(ba, al, dtb, dgb, name):
    T = ba.shape[0]
    tg = _tile(T, PREP_T)

    def body(ba_ref, al_ref, dtb_ref, dgb_ref, dba_ref, dal_ref, ddt_ref):
        x, d = ba_ref[...], dgb_ref[...]
        lane = lax.broadcasted_iota(jnp.int32, x.shape, 1)
        is_b = lane < HEADS_A
        is_a = (lane >= HEADS_A) & (lane < 2 * HEADS_A)
        beta = _sigmoid(x)
        e_a = jnp.exp(al_ref[...])
        z = x + dtb_ref[...]
        g = jnp.where(is_a, -e_a * _softplus(z), 0.0)
        _, upper = _chunk_masks(tg)
        dg = _dot(upper.astype(F32), jnp.where(is_a, d, 0.0), precision=HI)
        da = jnp.where(is_a, dg * (-e_a) * _sigmoid(z), 0.0)
        db = jnp.where(is_b, d * beta * (1.0 - beta), 0.0)
        dba_ref[...] = (da + db).astype(BF16)

        @pl.when(pl.program_id(0) == 0)
        def _():
            dal_ref[...] = jnp.zeros_like(dal_ref)
            ddt_ref[...] = jnp.zeros_like(ddt_ref)
        dal_ref[...] += jnp.sum(dg * g, axis=0, keepdims=True)
        ddt_ref[...] += jnp.sum(da, axis=0, keepdims=True)

    vec = pl.BlockSpec((1, 128), lambda i: (0, 0))
    blk = pl.BlockSpec((tg, 128), lambda i: (i, 0))
    return _pc(body, name=name, out_shape=(_sds((T, 128), BF16), _sds((1, 128), F32), _sds((1, 128), F32)),
               grid=(T // tg,), in_specs=[blk, vec, vec, blk],
               out_specs=(blk, vec, vec), sem=("arbitrary",))(ba, al, dtb, dgb)


def _bmm(a, b, dims, precision=None):
    return lax.dot_general(a, b, dims, preferred_element_type=F32, precision=precision)


B_NN = (((2,), (1,)), ((0,), (0,)))
B_NT = (((2,), (2,)), ((0,), (0,)))


def _select_lane(x, lane_index):
    lane = lax.broadcasted_iota(jnp.int32, x.shape, x.ndim - 1)
    return jnp.sum(jnp.where(lane == lane_index, x, 0.0), axis=-1, keepdims=True)


B_TN = (((1,), (1,)), ((0,), (0,)))


def _bmm_split(a, b, dims):
    ah, bh = a.astype(BF16), b.astype(BF16)
    al, bl = (a - ah.astype(F32)).astype(BF16), (b - bh.astype(F32)).astype(BF16)
    return _bmm(ah, bh, dims) + (_bmm(ah, bl, dims) + _bmm(al, bh, dims))


@jax.custom_vjp
def _bmm_f32(a, b):
    return _bmm_split(a, b, B_NN)


def _bmm_f32_fwd(a, b):
    return _bmm_split(a, b, B_NN), (a, b)


def _bmm_bf16(a, b, dims):
    return _bmm(a.astype(BF16), b.astype(BF16), dims)


def _bmm_f32_bwd(res, dc):
    a, b = res
    return _bmm_bf16(dc, b, B_NT), _bmm_bf16(a, dc, B_TN)


_bmm_f32.defvjp(_bmm_f32_fwd, _bmm_f32_bwd)


def _tri_inverse(lmat):
    ri = lax.broadcasted_iota(jnp.int32, lmat.shape, 1)
    ci = lax.broadcasted_iota(jnp.int32, lmat.shape, 2)
    eye = jnp.where(ri == ci, 1.0, 0.0)
    inv = eye - lmat
    power = lmat
    for _ in range(5):
        power = _bmm_bf16(power, power, B_NN)
        inv = inv + _bmm_bf16(inv, power, B_NN)
    return _bmm_split(inv, 2.0 * eye - _bmm_split(eye + lmat, inv, B_NN), B_NN)


def _stored_inverse(x):
    @jax.custom_vjp
    def inverse(lmat):
        return x

    def fwd(lmat):
        return x, None

    def bwd(_, dx):
        return (-_bmm_bf16(_bmm_bf16(x, dx, B_TN), x, B_NT),)

    inverse.defvjp(fwd, bwd)
    return inverse


def _gdn_prep(q, k, v, gb, h, inverse):
    nb = q.shape[0]
    beta = _select_lane(gb, h)
    gc = _select_lane(gb, HEADS_A + h)
    ri = lax.broadcasted_iota(jnp.int32, (nb, CHUNK, CHUNK), 1)
    ci = lax.broadcasted_iota(jnp.int32, (nb, CHUNK, CHUNK), 2)
    lower, strict, eye = ri >= ci, ri > ci, ri == ci
    gcol = jnp.broadcast_to(gc, (nb, CHUNK, CHUNK))
    grow = jnp.swapaxes(gcol, 1, 2)
    decay = jnp.where(lower, jnp.exp(jnp.where(lower, gcol - grow, 0.0)), 0.0)
    kb = k * beta
    kbf = k.astype(BF16)
    inv = inverse(jnp.where(strict, _bmm(kb.astype(BF16), kbf, B_NT) * decay, 0.0))
    eg = jnp.exp(gc)
    sol = _bmm_f32(inv, jnp.concatenate([v * beta, kb * eg], axis=-1))
    aqk = _bmm(q.astype(BF16), kbf, B_NT) * decay
    g_last = gc[:, CHUNK - 1:CHUNK, :]
    gl = jnp.broadcast_to(jnp.exp(g_last), (nb, 1, 128))
    return (sol[..., :DK], sol[..., DK:], q * eg, k * jnp.exp(g_last - gc), aqk, gl), inv


def gdn_prep_fwd(qkv, gb, name):
    T = qkv.shape[1]
    tp = _tile(T, PREP_T)
    nb = tp // CHUNK

    def body(q_ref, k_ref, v_ref, gb_ref, u_ref, w_ref, qd_ref, kd_ref, a_ref, gl_ref, inv_ref):
        h = pl.program_id(1)
        shp = (nb, CHUNK, 128)
        q, k, v = (ref[0].astype(F32).reshape(shp) for ref in (q_ref, k_ref, v_ref))
        (u, w, qd, kd, aqk, gl), inv = _gdn_prep(q, k, v, gb_ref[...].reshape(shp), h, _tri_inverse)
        u_ref[0] = u.reshape(tp, 128)
        w_ref[0] = w.reshape(tp, 128).astype(BF16)
        qd_ref[0] = qd.reshape(tp, 128).astype(BF16)
        kd_ref[0] = kd.reshape(tp, 128).astype(BF16)
        a_ref[0] = aqk.reshape(tp, CHUNK).astype(BF16)
        gl_ref[0] = gl.reshape(nb, 1, 128)
        inv_ref[0] = inv.reshape(tp, CHUNK)

    def head(off):
        return pl.BlockSpec((1, tp, 128), lambda n, h: (h + off, n, 0))

    matmul_only = _sds((HEADS_A, T, 128), BF16)
    narrow = pl.BlockSpec((1, tp, CHUNK), lambda n, h: (h, n, 0))
    return _pc(body, name=name,
               out_shape=(_sds((HEADS_A, T, 128), F32), matmul_only, matmul_only, matmul_only, _sds((HEADS_A, T, CHUNK), BF16),
                          _sds((HEADS_A, T // CHUNK, 1, 128), F32), _sds((HEADS_A, T, CHUNK), F32)),
               grid=(T // tp, HEADS_A),
               in_specs=[head(0), head(HEADS_A), head(2 * HEADS_A), pl.BlockSpec((tp, 128), lambda n, h: (n, 0))],
               out_specs=(head(0), head(0), head(0), head(0), narrow,
                          pl.BlockSpec((1, nb, 1, 128), lambda n, h: (h, n, 0, 0)), narrow),
               sem=("parallel", "parallel"))(qkv, qkv, qkv, gb)


def gdn_prep_bwd(qkv, gb, inv, du, dw, dqd, dkd, da, dgl, name):
    T = qkv.shape[1]
    tp = _tile(T, PREP_T)
    nb = tp // CHUNK

    def body(q_ref, k_ref, v_ref, gb_ref, inv_ref, du_ref, dw_ref, dqd_ref, dkd_ref, da_ref, dgl_ref, dqkv_ref, dgb_ref):
        h = pl.program_id(1)
        shp = (nb, CHUNK, 128)
        stored = _stored_inverse(inv_ref[0].reshape(nb, CHUNK, CHUNK))
        q, k, v = (ref[0].astype(F32).reshape(shp) for ref in (q_ref, k_ref, v_ref))
        _, vjp = jax.vjp(lambda q, k, v, gb: _gdn_prep(q, k, v, gb, h, stored)[0], q, k, v, gb_ref[...].reshape(shp))
        dq, dk, dv, dgb = vjp((du_ref[0].reshape(shp), dw_ref[0].reshape(shp), dqd_ref[0].reshape(shp),
                               dkd_ref[0].reshape(shp), da_ref[0].reshape(nb, CHUNK, CHUNK), dgl_ref[0].reshape(nb, 1, 128)))
        dqkv_ref[h] = dq.reshape(tp, 128).astype(BF16)
        dqkv_ref[HEADS_A + h] = dk.reshape(tp, 128).astype(BF16)
        dqkv_ref[2 * HEADS_A + h] = dv.reshape(tp, 128).astype(BF16)

        @pl.when(h == 0)
        def _():
            dgb_ref[...] = jnp.zeros_like(dgb_ref)
        dgb_ref[...] += dgb.reshape(tp, 128)

    def head(off):
        return pl.BlockSpec((1, tp, 128), lambda n, h: (h + off, n, 0))

    narrow = pl.BlockSpec((1, tp, CHUNK), lambda n, h: (h, n, 0))
    return _pc(body, name=name, out_shape=(_sds((N_QKV_BLK, T, 128), BF16), _sds((T, 128), F32)),
               grid=(T // tp, HEADS_A),
               in_specs=[head(0), head(HEADS_A), head(2 * HEADS_A), pl.BlockSpec((tp, 128), lambda n, h: (n, 0)), narrow,
                         head(0), head(0), head(0), head(0), narrow,
                         pl.BlockSpec((1, nb, 1, 128), lambda n, h: (h, n, 0, 0))],
               out_specs=(pl.BlockSpec((N_QKV_BLK, tp, 128), lambda n, h: (0, n, 0)),
                          pl.BlockSpec((tp, 128), lambda n, h: (n, 0))),
               sem=("parallel", "arbitrary"))(qkv, qkv, qkv, gb, inv, du, dw, dqd, dkd, da, dgl)


def gdn_scan_fwd(u, w, qd, kd, aqk, gl, name):
    T = u.shape[1]
    n_chunks = T // CHUNK

    def body(u_ref, w_ref, qd_ref, kd_ref, a_ref, gl_ref, o_ref, sin_ref, state):
        @pl.when(pl.program_id(0) == 0)
        def _():
            state[...] = jnp.zeros_like(state)
        s = state[...]
        sb = s.astype(BF16)
        sin_ref[0] = sb
        both = _bmm(jnp.concatenate([w_ref[...], qd_ref[...]], axis=1).astype(BF16), sb, B_NN)
        vn = (u_ref[...] - both[:, :CHUNK]).astype(BF16)
        o_ref[...] = both[:, CHUNK:] + _bmm(a_ref[...].astype(BF16), vn, B_NN)
        state[...] = s * gl_ref[:, 0] + _bmm(kd_ref[...].astype(BF16), vn, B_TN)

    blk = pl.BlockSpec((HEADS_A, CHUNK, 128), lambda n: (0, n, 0))
    return _pc(body, name=name,
               out_shape=(_sds((HEADS_A, T, 128), F32), _sds((n_chunks, HEADS_A, DK, 128), BF16)), grid=(n_chunks,),
               in_specs=[blk, blk, blk, blk, pl.BlockSpec((HEADS_A, CHUNK, CHUNK), lambda n: (0, n, 0)),
                         pl.BlockSpec((HEADS_A, 1, 1, 128), lambda n: (0, n, 0, 0))],
               out_specs=(blk, pl.BlockSpec((1, HEADS_A, DK, 128), lambda n: (n, 0, 0, 0))),
               scratch=[pltpu.VMEM((HEADS_A, DK, 128), F32)], sem=("arbitrary",))(u, w, qd, kd, aqk, gl)


def gdn_scan_bwd(u, w, qd, kd, aqk, gl, sin, do, name):
    T = u.shape[1]
    n_chunks = T // CHUNK

    def body(u_ref, w_ref, qd_ref, kd_ref, a_ref, gl_ref, sin_ref, do_ref,
             du_ref, dw_ref, dqd_ref, dkd_ref, da_ref, dgl_ref, dstate):
        @pl.when(pl.program_id(0) == 0)
        def _():
            dstate[...] = jnp.zeros_like(dstate)
        lane0 = lax.broadcasted_iota(jnp.int32, (HEADS_A, 1, 128), 2) == 0
        sb = sin_ref[0]
        s = sb.astype(F32)
        wb, qdb, kdb = w_ref[...].astype(BF16), qd_ref[...].astype(BF16), kd_ref[...].astype(BF16)
        ab, dob = a_ref[...].astype(BF16), do_ref[...].astype(BF16)
        vn = (u_ref[...] - _bmm(wb, sb, B_NN)).astype(BF16)
        ds_out = dstate[...]
        dsb = ds_out.astype(BF16)
        dqd_ref[...] = _bmm(dob, sb, B_NT)
        da_ref[...] = _bmm(dob, vn, B_NT)
        dv = _bmm(ab, dob, B_TN) + _bmm(kdb, dsb, B_NN)
        dkd_ref[...] = _bmm(vn, dsb, B_NT)
        dgl = jnp.sum(jnp.sum(ds_out * s, axis=2, keepdims=True), axis=1, keepdims=True)
        dgl_ref[:, 0] = jnp.where(lane0, dgl, 0.0)
        du_ref[...] = dv
        dvb = dv.astype(BF16)
        dw_ref[...] = -_bmm(dvb, sb, B_NT)
        dstate[...] = ds_out * gl_ref[:, 0] + _bmm(qdb, dob, B_TN) - _bmm(wb, dvb, B_TN)

    last = n_chunks - 1
    blk = pl.BlockSpec((HEADS_A, CHUNK, 128), lambda n: (0, last - n, 0))
    ablk = pl.BlockSpec((HEADS_A, CHUNK, CHUNK), lambda n: (0, last - n, 0))
    glblk = pl.BlockSpec((HEADS_A, 1, 1, 128), lambda n: (0, last - n, 0, 0))
    per_head = _sds((HEADS_A, T, 128), F32)
    return _pc(body, name=name,
               out_shape=(per_head, per_head, per_head, per_head, _sds((HEADS_A, T, CHUNK), F32),
                          _sds((HEADS_A, n_chunks, 1, 128), F32)), grid=(n_chunks,),
               in_specs=[blk, blk, blk, blk, ablk, glblk,
                         pl.BlockSpec((1, HEADS_A, DK, 128), lambda n: (last - n, 0, 0, 0)), blk],
               out_specs=(blk, blk, blk, blk, ablk, glblk),
               scratch=[pltpu.VMEM((HEADS_A, DK, 128), F32)], sem=("arbitrary",))(u, w, qd, kd, aqk, gl, sin, do)


def gdn_outnorm_fwd(o, proj, wn, name):
    T = o.shape[1]
    tm = _tile(T, 512)

    def body(o_ref, z_ref, wn_ref, y_ref):
        for h in range(HEADS_A):
            z = z_ref[:, 128 * h:128 * (h + 1)].astype(F32)
            y_ref[:, 128 * h:128 * (h + 1)] = (_rms_fwd(o_ref[h], wn_ref[...]) * (z * _sigmoid(z))).astype(BF16)

    return _pc(body, name=name, out_shape=_sds((T, D), BF16), grid=(T // tm,),
               in_specs=[pl.BlockSpec((HEADS_A, tm, 128), lambda i: (0, i, 0)),
                         pl.BlockSpec((tm, D), lambda i: (i, Z_BLK0 * 128 // D)), pl.BlockSpec((1, 128), lambda i: (0, 0))],
               out_specs=pl.BlockSpec((tm, D), lambda i: (i, 0)), sem=("parallel",))(o, proj, wn)


def gdn_outnorm_bwd(o, proj, wn, dy, name):
    T = o.shape[1]
    tm = _tile(T, 512)

    def body(o_ref, z_ref, wn_ref, dy_ref, do_ref, dz_ref, dwn_ref):
        wn = wn_ref[...]
        acc = jnp.zeros((1, 128), F32)
        for h in range(HEADS_A):
            cols = slice(128 * h, 128 * (h + 1))
            z, dyh, ov = z_ref[:, cols].astype(F32), dy_ref[:, cols], o_ref[h]
            sg = _sigmoid(z)
            do, dwn = _rms_bwd(ov, wn, dyh * (z * sg))
            do_ref[h] = do
            acc = acc + dwn
            dz_ref[:, cols] = (dyh * _rms_fwd(ov, wn) * (sg * (1.0 + z * (1.0 - sg)))).astype(BF16)

        @pl.when(pl.program_id(0) == 0)
        def _():
            dwn_ref[...] = jnp.zeros_like(dwn_ref)
        dwn_ref[...] += acc

    row = pl.BlockSpec((tm, D), lambda i: (i, 0))
    vec = pl.BlockSpec((1, 128), lambda i: (0, 0))
    hblk = pl.BlockSpec((HEADS_A, tm, 128), lambda i: (0, i, 0))
    return _pc(body, name=name, out_shape=(_sds((HEADS_A, T, 128), F32), _sds((T, D), BF16), _sds((1, 128), F32)),
               grid=(T // tm,),
               in_specs=[hblk, pl.BlockSpec((tm, D), lambda i: (i, Z_BLK0 * 128 // D)), vec, row],
               out_specs=(hblk, row, vec), sem=("arbitrary",))(o, proj, wn, dy)


def gdn_forward(x, nw, w_in, wconv, al, dtb, wn, w_out, tag, deps=()):
    h = rmsnorm_bf16(x, nw, f"{tag}_norm", deps)
    proj = mm_nn(h, w_in, f"{tag}_proj", out_dtype=BF16, cols=(0, MAIN_COLS))
    ba = mm_nn(h, w_in, f"{tag}_proj_ba", cols=(MAIN_COLS, A_COLS))
    c, qkv = gdn_conv_fwd(proj, wconv, f"{tag}_conv")
    gb = gdn_gate_fwd(ba, al, dtb, f"{tag}_gate")
    u, w, qd, kd, aqk, gl, inv = gdn_prep_fwd(qkv, gb, f"{tag}_prep")
    o, sin = gdn_scan_fwd(u, w, qd, kd, aqk, gl, f"{tag}_scan")
    on = gdn_outnorm_fwd(o, proj, wn, f"{tag}_outnorm")
    y = mm_nn(on, w_out, f"{tag}_out", residual=x)
    return y, (x, h, proj, ba, c, qkv, gb, inv, (u, w, qd, kd, aqk, gl), sin, o, on)


def gdn_backward(dout, saved, nw, w_in, wconv, al, dtb, wn, w_out, tag):
    x, h, proj, ba, c, qkv, gb, inv, prep, sin, o, on = saved
    d_on = mm_nt(dout, w_out, f"{tag}_out_bwd")
    dw_out = mm_tn(on, dout, f"{tag}_out_wgrad")
    do, dz, dwn = gdn_outnorm_bwd(o, proj, wn, d_on, f"{tag}_outnorm_bwd")
    du, dw, dqd, dkd, da, dgl = gdn_scan_bwd(*prep, sin, do, f"{tag}_scan_bwd")
    dqkv, dgb = gdn_prep_bwd(qkv, gb, inv, du, dw, dqd, dkd, da, dgl, f"{tag}_prep_bwd")
    dba, dal, ddt = gdn_gate_bwd(ba, al, dtb, dgb, f"{tag}_gate_bwd")
    dpre, dwconv = gdn_conv_bwd(dqkv, c, proj, wconv, f"{tag}_conv_bwd")
    dproj = jnp.concatenate([dpre, dz, dba], axis=1)
    dw_in = mm_tn(h, dproj, f"{tag}_proj_wgrad")
    dx, dnw = mm_nt(dproj, w_in, f"{tag}_proj_bwd", norm_bwd=(x, nw, dout))
    return dx, dnw, dw_in, dwconv, dal, ddt, dwn, dw_out


N_KV, GROUP = 4, 4
KV_COLS = 2 * N_KV * B_HD
B_COLS = D + KV_COLS


@jax.custom_vjp
def _swap_lane_halves(x):
    return pltpu.roll(x, 64, 1)


_swap_lane_halves.defvjp(lambda x: (pltpu.roll(x, 64, 1), None), lambda _, g: (pltpu.roll(g, 64, 1),))


def _swa_block(q, kp, kc, vp, vc, sk, first):
    cols = GROUP * B_BLK
    ks = lax.broadcasted_iota(jnp.int32, (N_KV, B_BLK, cols), 1)
    qi = lax.broadcasted_iota(jnp.int32, (N_KV, B_BLK, cols), 2) % B_BLK
    from_cur = ks <= qi

    def batch(parts):
        return jnp.concatenate([part[None] for part in parts], axis=0)

    def per_kv(cur, prev):
        return batch([jnp.concatenate([cur[:, j * B_HD:(j + 1) * B_HD], prev[:, j * B_HD:(j + 1) * B_HD]], axis=0)
                      for j in range(N_KV)]).astype(BF16)

    qs = batch([jnp.concatenate([q[:, hq * B_HD:(hq + 1) * B_HD] for hq in range(GROUP * j, GROUP * (j + 1))], axis=0)
                for j in range(N_KV)])
    q_t = jnp.swapaxes(qs, 1, 2).astype(BF16)
    sink = batch([jnp.concatenate([jnp.broadcast_to(sk[:, hq:hq + 1], (1, B_BLK))
                                   for hq in range(GROUP * j, GROUP * (j + 1))], axis=1) for j in range(N_KV)])
    both = _bmm(per_kv(kc, kp), q_t, B_NN)
    s = jnp.where(from_cur, both[:, :B_BLK], jnp.where(first, -1e30, both[:, B_BLK:])) * (B_HD ** -0.5)
    m = lax.stop_gradient(jnp.maximum(jnp.max(s, axis=1, keepdims=True), sink))
    e = jnp.exp((s - m).astype(BF16))
    den = jnp.sum(e.astype(F32), axis=1, keepdims=True) + jnp.exp(sink - m)
    p = e * (1.0 / den).astype(BF16)
    zero = jnp.zeros_like(p)
    p_both = jnp.concatenate([jnp.where(from_cur, p, zero), jnp.where(from_cur, zero, p)], axis=1)
    o = jnp.swapaxes(_bmm(per_kv(vc, vp), p_both, B_TN), 1, 2)
    return jnp.concatenate([o[j, g * B_BLK:(g + 1) * B_BLK] for j in range(N_KV) for g in range(GROUP)], axis=1)


def swa_core_fwd(proj, sk, name):
    T = proj.shape[0]
    half = N_KV * B_HD

    def body(q_ref, kvc_ref, kvp_ref, sk_ref, o_ref):
        kvc, kvp = kvc_ref[...], kvp_ref[...]
        o_ref[...] = _swa_block(q_ref[...], kvp[:, :half], kvc[:, :half], kvp[:, half:], kvc[:, half:], sk_ref[...],
                                pl.program_id(0) == 0).astype(BF16)

    return _pc(body, name=name, out_shape=_sds((T, D), BF16), grid=(T // B_BLK,),
               in_specs=[pl.BlockSpec((B_BLK, D), lambda n: (n, 0)),
                         pl.BlockSpec((B_BLK, KV_COLS), lambda n: (n, D // KV_COLS)),
                         pl.BlockSpec((B_BLK, KV_COLS), lambda n: (jnp.maximum(n - 1, 0), D // KV_COLS)),
                         pl.BlockSpec((1, 128), lambda n: (0, 0))],
               out_specs=pl.BlockSpec((B_BLK, D), lambda n: (n, 0)), sem=("parallel",))(proj, proj, proj, sk)


def swa_core_bwd(proj, sk, do, name):
    T = proj.shape[0]
    last = T // B_BLK - 1
    half = N_KV * B_HD

    def body(q_ref, kvc_ref, kvp_ref, sk_ref, do_ref, dproj_ref, dbias_ref, dsk_ref, carry):
        step = pl.program_id(0)
        first = step == last

        @pl.when(step == 0)
        def _():
            carry[...] = jnp.zeros_like(carry)
            dbias_ref[...] = jnp.zeros_like(dbias_ref)
            dsk_ref[...] = jnp.zeros_like(dsk_ref)
        kvc, kvp = kvc_ref[...], kvp_ref[...]
        _, vjp = jax.vjp(functools.partial(_swa_block, first=first), q_ref[...], kvp[:, :half], kvc[:, :half],
                         kvp[:, half:], kvc[:, half:], sk_ref[...])
        dq, dkp, dkc, dvp, dvc, dsk = vjp(do_ref[...])
        dkv = jnp.concatenate([dkc, dvc], axis=1) + carry[...]
        carry[...] = jnp.concatenate([dkp, dvp], axis=1)
        row = jnp.concatenate([dq, dkv], axis=1)
        dproj_ref[...] = row.astype(BF16)
        dbias_ref[...] += jnp.sum(row, axis=0, keepdims=True)
        dsk_ref[...] += dsk

    return _pc(body, name=name, out_shape=(_sds((T, B_COLS), BF16), _sds((1, B_COLS), F32), _sds((1, 128), F32)),
               grid=(T // B_BLK,),
               in_specs=[pl.BlockSpec((B_BLK, D), lambda n: (last - n, 0)),
                         pl.BlockSpec((B_BLK, KV_COLS), lambda n: (last - n, D // KV_COLS)),
                         pl.BlockSpec((B_BLK, KV_COLS), lambda n: (jnp.maximum(last - n - 1, 0), D // KV_COLS)),
                         pl.BlockSpec((1, 128), lambda n: (0, 0)), pl.BlockSpec((B_BLK, D), lambda n: (last - n, 0))],
               out_specs=(pl.BlockSpec((B_BLK, B_COLS), lambda n: (last - n, 0)),
                          pl.BlockSpec((1, B_COLS), lambda n: (0, 0)), pl.BlockSpec((1, 128), lambda n: (0, 0))),
               scratch=[pltpu.VMEM((B_BLK, KV_COLS), F32)], sem=("arbitrary",))(proj, proj, proj, sk, do)


def col_sum(a, name):
    T, N = a.shape
    tm = _tile(T, 1024)

    def body(a_ref, o_ref):
        @pl.when(pl.program_id(0) == 0)
        def _():
            o_ref[...] = jnp.zeros_like(o_ref)
        o_ref[...] += jnp.sum(a_ref[...].astype(F32), axis=0, keepdims=True)

    return _pc(body, name=name, out_shape=_sds((1, N), F32), grid=(T // tm,),
               in_specs=[pl.BlockSpec((tm, N), lambda i: (i, 0))], out_specs=pl.BlockSpec((1, N), lambda i: (0, 0)),
               sem=("arbitrary",))(a)


def swa_forward(x, nw, w_in, b_in, sk, w_out, b_out, tag):
    h = rmsnorm_bf16(x, nw, f"{tag}_norm")
    proj = mm_nn(h, w_in, f"{tag}_proj", bias=b_in)
    o = swa_core_fwd(proj, sk, f"{tag}_core")
    y = mm_nn(o, w_out, f"{tag}_out", bias=b_out, residual=x)
    return y, (x, h, proj, o)


def swa_backward(dout, saved, nw, w_in, b_in, sk, w_out, b_out, tag):
    x, h, proj, o = saved
    do = mm_nt(dout, w_out, f"{tag}_out_bwd")
    dw_out = mm_tn(o, dout, f"{tag}_out_wgrad")
    db_out = col_sum(dout, f"{tag}_out_bias_grad")
    dproj, db_in, dsk = swa_core_bwd(proj, sk, do, f"{tag}_core_bwd")
    dw_in = mm_tn(h, dproj, f"{tag}_proj_wgrad")
    dx, dnw = mm_nt(dproj, w_in, f"{tag}_proj_bwd", norm_bwd=(x, nw, dout))
    return dx, dnw, dw_in, db_in, dsk, dw_out, db_out


MESH = pl.DeviceIdType.MESH


def _position():
    return lax.axis_index("x"), lax.axis_index("y"), lax.axis_index("c")


def _slot(x, y, c):
    return 4 * x + 2 * y + c


def _peer(x, y, c, k):
    return (1 - x if k & 4 else x, 1 - y if k & 2 else y, 1 - c if k & 1 else c)


HBM_SPEC = pl.BlockSpec(memory_space=pltpu.HBM)
SEM_SPEC = pl.BlockSpec(memory_space=pltpu.SEMAPHORE)
DEP_SPEC = pl.BlockSpec(memory_space=pl.ANY)
SIDE_EFFECT = pltpu.SideEffectType.DATAFLOW_SIDE_EFFECTING
N_PEERS = N_DEV - 1


def _push_copies(srcs, lands, send_sems, recv_sems, scatter):
    x, y, c = _position()
    me = _slot(x, y, c)
    copies = []
    for k in (1, 2, 4, 3, 5, 6, 7):
        peer = _peer(x, y, c, k)
        for a in range(len(srcs)):
            copies.append(pltpu.make_async_remote_copy(
                src_ref=srcs[a].at[_slot(*peer)] if scatter else srcs[a], dst_ref=lands[a].at[me],
                send_sem=send_sems.at[N_PEERS * a + k - 1], recv_sem=recv_sems.at[N_PEERS * a + k - 1],
                device_id=peer, device_id_type=MESH))
    return copies


def push_start(srcs, lands, name, scatter, deps=()):
    n = len(srcs)
    first_out = 2 * n + len(deps)

    def body(*refs):
        for cp in _push_copies(refs[:n], refs[n:2 * n], refs[first_out], refs[first_out + 1], scatter):
            cp.start()
        refs[-1][...] = jnp.zeros_like(refs[-1])

    passed = [pltpu.HBM(t.shape, t.dtype) for t in list(srcs) + list(lands)]
    res = pl.pallas_call(
        body, name=name,
        out_shape=(pltpu.SemaphoreType.DMA((N_PEERS * n,)), pltpu.SemaphoreType.DMA((N_PEERS * n,)), *passed, _sds((8, 128), F32)),
        in_specs=[HBM_SPEC] * (2 * n) + [DEP_SPEC] * len(deps),
        out_specs=(SEM_SPEC, SEM_SPEC, *([HBM_SPEC] * (2 * n)), pl.BlockSpec(memory_space=pltpu.VMEM)),
        input_output_aliases={i: 2 + i for i in range(2 * n)},
        compiler_params=pltpu.CompilerParams(has_side_effects=SIDE_EFFECT),
    )(*[pltpu.with_memory_space_constraint(t, pltpu.HBM) for t in list(srcs) + list(lands)], *deps)
    return (res[0], res[1], list(res[2:2 + n]), list(res[2 + n:2 + 2 * n])), res[-1]


def push_wait(handles, after, name, scatter):
    send_sems, recv_sems, srcs, lands = handles
    n = len(srcs)
    after = tuple(after) if isinstance(after, (tuple, list)) else (after,)

    def body(*refs):
        for cp in _push_copies(refs[:n], refs[n:2 * n], refs[2 * n], refs[2 * n + 1], scatter):
            cp.wait_send()
            cp.wait_recv()

    res = pl.pallas_call(
        body, name=name, out_shape=tuple(pltpu.HBM(t.shape, t.dtype) for t in srcs + lands),
        in_specs=[HBM_SPEC] * (2 * n) + [SEM_SPEC, SEM_SPEC] + [DEP_SPEC] * len(after), out_specs=tuple([HBM_SPEC] * (2 * n)),
        input_output_aliases={i: i for i in range(2 * n)},
        compiler_params=pltpu.CompilerParams(has_side_effects=SIDE_EFFECT),
    )(*srcs, *lands, send_sems, recv_sems, *after)
    return list(res[n:])


def gather_start(shards, name, deps=()):
    me = _slot(*_position())
    lands = [lax.dynamic_update_slice(lax.empty((N_DEV,) + t.shape, t.dtype), t[None], (me,) + (0,) * t.ndim) for t in shards]
    return push_start(shards, lands, name, scatter=False, deps=deps)


def exchange_start(parts, name):
    me = _slot(*_position())
    lands = [lax.dynamic_update_slice(lax.empty(t.shape, t.dtype), lax.dynamic_index_in_dim(t, me, 0, keepdims=True),
                                      (me,) + (0,) * (t.ndim - 1)) for t in parts]
    return push_start(parts, lands, name, scatter=True)


def _row_tile(rows, cols):
    best = rows
    for t in range(16, rows, 16):
        if rows % t == 0 and t * cols * 4 <= (1 << 20):
            best = t
    return best


def adam_update(parts, w, m, v, name):
    n_layers = len(parts)
    P, R, C = parts[0].shape
    tr = _row_tile(R, C)
    n_t = R // tr

    def body(*refs):
        p_refs = refs[:n_layers]
        w_ref, m_ref, v_ref, g_ref, d_ref, nm_ref, nv_ref = refs[n_layers:]
        for layer in range(n_layers):
            @pl.when(pl.program_id(0) == layer)
            def _(p_ref=p_refs[layer]):
                g = p_ref[0].astype(F32)
                for s in range(1, P):
                    g = g + p_ref[s].astype(F32)
                new_m = ADAM_B1 * m_ref[0] + (1.0 - ADAM_B1) * g
                new_v = ADAM_B2 * v_ref[0] + (1.0 - ADAM_B2) * (g * g)
                m_hat = new_m / (1.0 - ADAM_B1 ** ADAM_STEP)
                v_hat = new_v / (1.0 - ADAM_B2 ** ADAM_STEP)
                g_ref[0] = g
                d_ref[0] = -ADAM_LR * (m_hat / (jnp.sqrt(v_hat) + ADAM_EPS) + ADAM_WD * w_ref[0])
                nm_ref[0] = new_m
                nv_ref[0] = new_v

    def part_spec(layer):
        return pl.BlockSpec((P, tr, C), lambda l_, i: (0, jnp.where(l_ == layer, i, jnp.where(l_ < layer, 0, n_t - 1)), 0))

    blk = pl.BlockSpec((1, tr, C), lambda l_, i: (l_, i, 0))
    out = _sds((n_layers, R, C), F32)
    return _pc(body, name=name, out_shape=(out, out, out, out), grid=(n_layers, n_t),
               in_specs=[part_spec(layer) for layer in range(n_layers)] + [blk, blk, blk],
               out_specs=(blk, blk, blk, blk), sem=("arbitrary", "arbitrary"))(*parts, w, m, v)


WEIGHTS = ("ffn1_norm", "ffn1_w_gu", "ffn1_w_down", "mix_norm", "ffn2_norm", "ffn2_w_gu", "ffn2_w_down", "a_w_in",
           "a_w_conv", "a_A_log", "a_dt_bias", "a_out_norm", "a_w_out", "b_w_in", "b_b_in", "b_sinks", "b_w_out",
           "b_b_out", "final_norm")
SHARDED = ("ffn1_w_gu", "ffn1_w_down", "ffn2_w_gu", "ffn2_w_down", "a_w_in", "a_w_conv", "a_w_out", "b_w_in", "b_b_in",
           "b_w_out", "b_b_out")
MISC_LANES = dict(a_A_log=(0, 8), a_dt_bias=(8, 16), b_sinks=(16, 32), a_out_norm=(128, 256))
LOSS_LANE = 256


def _pack_small(t):
    misc = jnp.zeros((D,), F32)
    for key, (lo, hi) in MISC_LANES.items():
        misc = misc.at[lo:hi].set(t[key].reshape(-1))
    if "loss" in t:
        misc = misc.at[LOSS_LANE].set(t["loss"])
    return jnp.concatenate([t["ffn1_norm"], t["mix_norm"], t["ffn2_norm"], t["final_norm"].reshape(1, D), misc[None]], axis=0)


def _unpack_small(p, like):
    out = dict(ffn1_norm=p[0:2], mix_norm=p[2:4], ffn2_norm=p[4:6], final_norm=p[6])
    for key, (lo, hi) in MISC_LANES.items():
        out[key] = p[7, lo:hi].reshape(like[key].shape)
    return out


def kernel(x, ffn1_norm, ffn1_w_gu, ffn1_w_down, mix_norm, ffn2_norm, ffn2_w_gu, ffn2_w_down, a_w_in, a_w_conv, a_A_log, a_dt_bias, a_out_norm, a_w_out, b_w_in, b_b_in, b_sinks, b_w_out, b_b_out, final_norm, loss_target, m_ffn1_norm, m_ffn1_w_gu, m_ffn1_w_down, m_mix_norm, m_ffn2_norm, m_ffn2_w_gu, m_ffn2_w_down, m_a_w_in, m_a_w_conv, m_a_A_log, m_a_dt_bias, m_a_out_norm, m_a_w_out, m_b_w_in, m_b_b_in, m_b_sinks, m_b_w_out, m_b_b_out, m_final_norm, v_ffn1_norm, v_ffn1_w_gu, v_ffn1_w_down, v_mix_norm, v_ffn2_norm, v_ffn2_w_gu, v_ffn2_w_down, v_a_w_in, v_a_w_conv, v_a_A_log, v_a_dt_bias, v_a_out_norm, v_a_w_out, v_b_w_in, v_b_b_in, v_b_sinks, v_b_w_out, v_b_b_out, v_final_norm):
    w = dict(ffn1_norm=ffn1_norm, ffn1_w_gu=ffn1_w_gu, ffn1_w_down=ffn1_w_down, mix_norm=mix_norm, ffn2_norm=ffn2_norm, ffn2_w_gu=ffn2_w_gu, ffn2_w_down=ffn2_w_down, a_w_in=a_w_in, a_w_conv=a_w_conv, a_A_log=a_A_log, a_dt_bias=a_dt_bias, a_out_norm=a_out_norm, a_w_out=a_w_out, b_w_in=b_w_in, b_b_in=b_b_in, b_sinks=b_sinks, b_w_out=b_w_out, b_b_out=b_b_out, final_norm=final_norm)
    m = dict(ffn1_norm=m_ffn1_norm, ffn1_w_gu=m_ffn1_w_gu, ffn1_w_down=m_ffn1_w_down, mix_norm=m_mix_norm, ffn2_norm=m_ffn2_norm, ffn2_w_gu=m_ffn2_w_gu, ffn2_w_down=m_ffn2_w_down, a_w_in=m_a_w_in, a_w_conv=m_a_w_conv, a_A_log=m_a_A_log, a_dt_bias=m_a_dt_bias, a_out_norm=m_a_out_norm, a_w_out=m_a_w_out, b_w_in=m_b_w_in, b_b_in=m_b_b_in, b_sinks=m_b_sinks, b_w_out=m_b_w_out, b_b_out=m_b_b_out, final_norm=m_final_norm)
    v = dict(ffn1_norm=v_ffn1_norm, ffn1_w_gu=v_ffn1_w_gu, ffn1_w_down=v_ffn1_w_down, mix_norm=v_mix_norm, ffn2_norm=v_ffn2_norm, ffn2_w_gu=v_ffn2_w_gu, ffn2_w_down=v_ffn2_w_down, a_w_in=v_a_w_in, a_w_conv=v_a_w_conv, a_A_log=v_a_A_log, a_dt_bias=v_a_dt_bias, a_out_norm=v_a_out_norm, a_w_out=v_a_w_out, b_w_in=v_b_w_in, b_b_in=v_b_b_in, b_sinks=v_b_sinks, b_w_out=v_b_w_out, b_b_out=v_b_b_out, final_norm=v_final_norm)
    T = x.shape[1]
    x0, tgt = x.reshape(T, D), loss_target.reshape(T, D)

    def cast(t):
        return t.astype(BF16)

    h0, t0 = gather_start([cast(ffn1_w_gu[0])], "gather0_start")
    a_log_row = jnp.zeros((1, 128), F32).at[0, HEADS_A:2 * HEADS_A].set(a_A_log[0])
    dt_row = jnp.zeros((1, 128), F32).at[0, HEADS_A:2 * HEADS_A].set(a_dt_bias[0])
    sink_row = jnp.zeros((1, 128), F32).at[0, :b_sinks.shape[1]].set(b_sinks[0])
    a_in_cols = a_w_in.shape[-1] * N_DEV

    def down_blocks(t):
        return t.reshape(N_FB, FB, D)

    wgu, wdn, saved = {}, {}, []
    xn = rmsnorm_bf16(x0, ffn1_norm[0:1], "l0_ffn1_norm", (t0,))
    wgu["ffn1", 0] = push_wait(h0, xn, "gather0_wait", scatter=False)[0]
    h0d, t0d = gather_start([cast(ffn1_w_down[0])], "gather0d_start", deps=(wgu["ffn1", 0],))
    h1, t1 = gather_start([cast(a_w_in[0]), a_w_conv[0], cast(a_w_out[0])], "gather1_start", deps=(t0d,))
    gu = ffn_up(xn, wgu["ffn1", 0], "l0_ffn1_up", deps=(t0d, t1))
    wdn["ffn1", 0] = down_blocks(push_wait(h0d, gu, "gather0d_wait", scatter=False)[0])
    xs, s1 = ffn_down(gu, wdn["ffn1", 0], x0, "l0_ffn1_down"), (x0, xn, gu)
    got = push_wait(h1, xs, "gather1_wait", scatter=False)
    h1f, t1f = gather_start([cast(ffn2_w_gu[0]), cast(ffn2_w_down[0])], "gather1f_start", deps=(got[0],))
    g2 = [cast(ffn1_w_gu[1]), cast(ffn1_w_down[1]), cast(b_w_in[0]), b_b_in, cast(b_w_out[0]), b_b_out,
          cast(ffn2_w_gu[1]), cast(ffn2_w_down[1])]
    h2, t2 = gather_start(g2, "gather2_start", deps=(t1f,))
    a_in_full = jnp.pad(got[0].transpose(1, 0, 2).reshape(D, a_in_cols), ((0, 0), (0, A_COLS - a_in_cols)))
    gdn_args = (mix_norm[0:1], a_in_full, got[1].transpose(1, 0, 2).reshape(4, 3 * D), a_log_row, dt_row, a_out_norm,
                got[2].reshape(D, D))
    xs, sm = gdn_forward(xs, *gdn_args, "gdn", deps=(t1f, t2))
    got = push_wait(h1f, xs, "gather1f_wait", scatter=False)
    wgu["ffn2", 0], wdn["ffn2", 0] = got[0], down_blocks(got[1])
    xs, s2 = ffn_forward(xs, ffn2_norm[0:1], wgu["ffn2", 0], wdn["ffn2", 0], "l0_ffn2")
    saved.append((s1, sm, s2))
    got = push_wait(h2, xs, "gather2_wait", scatter=False)
    wgu["ffn1", 1], wdn["ffn1", 1] = got[0], down_blocks(got[1])
    swa_args = (mix_norm[1:2], got[2].transpose(1, 0, 2).reshape(D, B_COLS), got[3].reshape(1, B_COLS), sink_row,
                got[4].reshape(D, D), got[5].reshape(1, D))
    wgu["ffn2", 1], wdn["ffn2", 1] = got[6], down_blocks(got[7])
    xs, s1 = ffn_forward(xs, ffn1_norm[1:2], wgu["ffn1", 1], wdn["ffn1", 1], "l1_ffn1")
    xs, sm = swa_forward(xs, *swa_args, "swa")
    xs, s2 = ffn_forward(xs, ffn2_norm[1:2], wgu["ffn2", 1], wdn["ffn2", 1], "l1_ffn2")
    saved.append((s1, sm, s2))
    loss_row, dx, d_final_norm = final_loss(xs, final_norm.reshape(1, D), tgt, "final_loss")

    def down_slots(t):
        return cast(t.reshape(N_DEV, FB // 2, D))

    def col_slots(t, dtype=BF16):
        return t.reshape(t.shape[0], N_DEV, -1).transpose(1, 0, 2).astype(dtype)

    d_norm = {"ffn1_norm": [None, None], "mix_norm": [None, None], "ffn2_norm": [None, None]}
    exchanges = {}

    def sender(tag):
        def on_grads(d_gu, d_dn):
            exchanges[tag], token = exchange_start([cast(d_gu), down_slots(d_dn)], f"exchange_{tag}_start")
            return (token,)
        return on_grads

    s1, sm, s2 = saved[1]
    dx, d_norm["ffn2_norm"][1], _, _ = ffn_backward(dx, s2, ffn2_norm[1:2], wgu["ffn2", 1], wdn["ffn2", 1], "l1_ffn2",
                                                    on_grads=sender("l1_ffn2"))
    dx, d_norm["mix_norm"][1], d_b_in, d_b_bias_in, d_sinks, d_b_out, d_b_bias_out = swa_backward(dx, sm, *swa_args, "swa")
    exchanges["swa"], t_swa = exchange_start(
        [col_slots(d_b_in), d_b_bias_in.reshape(N_DEV, 1, -1), cast(d_b_out.reshape(N_DEV, D // N_DEV, D)),
         d_b_bias_out.reshape(N_DEV, 1, -1)], "exchange_swa_start")
    dx, d_norm["ffn1_norm"][1], _, _ = ffn_backward(dx, s1, ffn1_norm[1:2], wgu["ffn1", 1], wdn["ffn1", 1], "l1_ffn1",
                                                    deps=(t_swa,), on_grads=sender("l1_ffn1"))

    s1, sm, s2 = saved[0]
    dx, d_norm["ffn2_norm"][0], _, _ = ffn_backward(dx, s2, ffn2_norm[0:1], wgu["ffn2", 0], wdn["ffn2", 0], "l0_ffn2",
                                                    on_grads=sender("l0_ffn2"))
    dx, d_norm["mix_norm"][0], d_a_in, d_a_conv, d_alog, d_dt, d_onorm, d_a_out = gdn_backward(dx, sm, *gdn_args, "gdn")
    exchanges["gdn"], t_gdn = exchange_start(
        [col_slots(d_a_in[:, :a_in_cols]), col_slots(d_a_conv, F32), cast(d_a_out.reshape(N_DEV, D // N_DEV, D))],
        "exchange_gdn_start")
    dx, d_norm["ffn1_norm"][0], _, _ = ffn_backward(dx, s1, ffn1_norm[0:1], wgu["ffn1", 0], wdn["ffn1", 0], "l0_ffn1",
                                                    deps=(t_gdn,), on_grads=sender("l0_ffn1"))
    grad_x = dx.reshape(x.shape)
    got = {tag: push_wait(exchanges[tag], dx, f"exchange_{tag}_wait", scatter=True)
           for tag in ("l1_ffn2", "swa", "l1_ffn1", "l0_ffn2", "gdn")}
    received = dict(ffn2_w_gu=[got["l0_ffn2"][0], got["l1_ffn2"][0]], ffn2_w_down=[got["l0_ffn2"][1], got["l1_ffn2"][1]],
                    b_w_in=[got["swa"][0]], b_b_in=[got["swa"][1]], b_w_out=[got["swa"][2]], b_b_out=[got["swa"][3]],
                    a_w_in=[got["gdn"][0]], a_w_conv=[got["gdn"][1]], a_w_out=[got["gdn"][2]])

    grads, deltas, new_m, new_v = {}, {}, {}, {}

    def update(key):
        shape = w[key].shape
        cols = shape[-1]
        layers = lambda t: t.reshape(shape[0], -1, cols)
        out = adam_update([r.reshape(N_DEV, -1, cols) for r in received[key]], layers(w[key]), layers(m[key]), layers(v[key]),
                          f"adam_{key}")
        grads[key], deltas[key], new_m[key], new_v[key] = (t.reshape(shape) for t in out)

    for key in SHARDED:
        if key in received:
            update(key)
    done_first = [deltas[key] for key in received]

    small = dict(ffn1_norm=jnp.concatenate(d_norm["ffn1_norm"], axis=0), mix_norm=jnp.concatenate(d_norm["mix_norm"], axis=0),
                 ffn2_norm=jnp.concatenate(d_norm["ffn2_norm"], axis=0), final_norm=d_final_norm,
                 a_A_log=d_alog[0, HEADS_A:2 * HEADS_A], a_dt_bias=d_dt[0, HEADS_A:2 * HEADS_A],
                 b_sinks=d_sinks[0, :b_sinks.shape[1]], a_out_norm=d_onorm, loss=loss_row[0, 0])
    hs, ts = gather_start([_pack_small(small)], "gather_small_start")
    r3 = push_wait(exchanges["l0_ffn1"], done_first + [ts], "exchange_l0_ffn1_wait", scatter=True)
    received.update(ffn1_w_gu=[r3[0], got["l1_ffn1"][0]], ffn1_w_down=[r3[1], got["l1_ffn1"][1]])
    update("ffn1_w_gu")
    update("ffn1_w_down")
    every = push_wait(hs, deltas["ffn1_w_down"], "gather_small_wait", scatter=False)[0]
    out = adam_update([every], _pack_small(w)[None], _pack_small(m)[None], _pack_small(v)[None], "adam_small")
    for dst, packed in zip((grads, deltas, new_m, new_v), out):
        dst.update(_unpack_small(packed[0], w))
    loss = out[0][0, 7, LOSS_LANE]

    return (loss, grad_x, *[grads[k_] for k_ in WEIGHTS], *[deltas[k_] for k_ in WEIGHTS],
            *[new_m[k_] for k_ in WEIGHTS], *[new_v[k_] for k_ in WEIGHTS])
```

```python
import functools

import jax
import jax.numpy as jnp
from jax import lax
from jax.experimental import pallas as pl
from jax.experimental.pallas import tpu as pltpu

F32, BF16 = jnp.float32, jnp.bfloat16
HI = lax.Precision.HIGHEST
EPS = 1e-6

N_DEV = 8
D = 1024
FB = 704
N_FB = 4
HEADS_A, DK = 8, 128
CHUNK = 64
PREP_T = 512
A_COLS = 4224
B_HD, B_BLK = 64, 128
VMEM_LIMIT_V7X = 60 * 1024 * 1024

ADAM_LR, ADAM_B1, ADAM_B2, ADAM_EPS, ADAM_WD, ADAM_STEP = 0.001, 0.9, 0.999, 1e-08, 0.01, 10

NT = (((1,), (1,)), ((), ()))
TN = (((0,), (0,)), ((), ()))


def _pc(body, *, name, out_shape, grid=(), in_specs=None, out_specs=None, scratch=(), sem=None, **kw):
    params = pltpu.CompilerParams(dimension_semantics=sem, vmem_limit_bytes=VMEM_LIMIT_V7X)
    return pl.pallas_call(body, name=name, out_shape=out_shape, grid=grid, in_specs=in_specs, out_specs=out_specs,
                          scratch_shapes=list(scratch), compiler_params=params, **kw)


def _sds(shape, dtype):
    return jax.ShapeDtypeStruct(tuple(shape), dtype)


def _dot(a, b, dims=None, precision=None):
    if dims is None:
        return jnp.dot(a, b, preferred_element_type=F32, precision=precision)
    return lax.dot_general(a, b, dims, preferred_element_type=F32, precision=precision)


def _sigmoid(x):
    return 1.0 / (1.0 + jnp.exp(-x))


def _softplus(x):
    return jnp.maximum(x, 0.0) + jnp.log(1.0 + jnp.exp(-jnp.abs(x)))


def _rms_fwd(x, w):
    r = lax.rsqrt(jnp.mean(x * x, axis=-1, keepdims=True) + EPS)
    return x * r * w


def _rms_bwd(x, w, dy):
    r = lax.rsqrt(jnp.mean(x * x, axis=-1, keepdims=True) + EPS)
    xh = x * r
    dxh = dy * w
    dx = r * (dxh - xh * jnp.mean(dxh * xh, axis=-1, keepdims=True))
    return dx, jnp.sum(dy * xh, axis=0, keepdims=True)


def _tile(n, want):
    t = min(n, want)
    assert n % t == 0, (n, want)
    return t


def rmsnorm_bf16(x, w, name, deps=()):
    T = x.shape[0]
    tm = _tile(T, 1024)

    def body(x_ref, w_ref, *rest):
        rest[-1][...] = _rms_fwd(x_ref[...], w_ref[...]).astype(BF16)

    return _pc(body, name=name, out_shape=_sds((T, D), BF16), grid=(T // tm,),
               in_specs=[pl.BlockSpec((tm, D), lambda i: (i, 0)), pl.BlockSpec((1, D), lambda i: (0, 0))] + [DEP_SPEC] * len(deps),
               out_specs=pl.BlockSpec((tm, D), lambda i: (i, 0)), sem=("parallel",))(x, w, *deps)


def final_loss(x, w, tgt, name):
    T = x.shape[0]
    tm = _tile(T, 512)

    def body(x_ref, w_ref, t_ref, loss_ref, dx_ref, dw_ref):
        xv, wv = x_ref[...], w_ref[...]
        err = _rms_fwd(xv, wv) - t_ref[...]
        dx, dw = _rms_bwd(xv, wv, err * (1.0 / D))
        dx_ref[...] = dx

        @pl.when(pl.program_id(0) == 0)
        def _():
            dw_ref[...] = jnp.zeros_like(dw_ref)
            loss_ref[...] = jnp.zeros_like(loss_ref)
        dw_ref[...] += dw
        loss_ref[...] += jnp.full((1, 128), 0.5 / D, F32) * jnp.sum(err * err)

    row = pl.BlockSpec((tm, D), lambda i: (i, 0))
    vec = pl.BlockSpec((1, D), lambda i: (0, 0))
    return _pc(body, name=name, out_shape=(_sds((1, 128), F32), _sds((T, D), F32), _sds((1, D), F32)),
               grid=(T // tm,), in_specs=[row, vec, row],
               out_specs=(pl.BlockSpec((1, 128), lambda i: (0, 0)), row, vec), sem=("arbitrary",))(x, w, tgt)


def _col_tile(n):
    for t in (1536, 1408, 1024, 768, 512, 384, 256, 128):
        if n % t == 0:
            return t
    return n


def mm_nn(a, b, name, bias=None, residual=None, out_dtype=F32, cols=None):
    T, K = a.shape
    first, end = cols or (0, b.shape[1])
    N = end - first
    tm, tn = _tile(T, 512), _col_tile(N)
    assert first % tn == 0 and (cols is None or (bias is None and residual is None))
    j0 = first // tn

    def body(a_ref, b_ref, *rest):
        o_ref = rest[-1]
        acc = _dot(a_ref[...].astype(BF16), b_ref[...])
        for extra in rest[:-1]:
            acc = acc + extra[...]
        o_ref[...] = acc.astype(out_dtype)

    in_specs = [pl.BlockSpec((tm, K), lambda j, i: (i, 0)), pl.BlockSpec((K, tn), lambda j, i: (0, j0 + j))]
    args = [a, b]
    if bias is not None:
        in_specs.append(pl.BlockSpec((1, tn), lambda j, i: (0, j)))
        args.append(bias)
    if residual is not None:
        in_specs.append(pl.BlockSpec((tm, tn), lambda j, i: (i, j)))
        args.append(residual)
    return _pc(body, name=name, out_shape=_sds((T, N), out_dtype), grid=(N // tn, T // tm), in_specs=in_specs,
               out_specs=pl.BlockSpec((tm, tn), lambda j, i: (i, j)), sem=("parallel", "parallel"))(*args)


def mm_nt(a, b, name, out_dtype=F32, norm_bwd=None):
    T, N = a.shape
    K = b.shape[0]
    tm = _tile(T, 512)
    row = pl.BlockSpec((tm, K), lambda i: (i, 0))
    in_specs = [pl.BlockSpec((tm, N), lambda i: (i, 0)), _resident((K, N))]

    if norm_bwd is None:
        def body(a_ref, b_ref, o_ref):
            o_ref[...] = _dot(a_ref[...].astype(BF16), b_ref[...], NT).astype(out_dtype)

        return _pc(body, name=name, out_shape=_sds((T, K), out_dtype), grid=(T // tm,), in_specs=in_specs,
                   out_specs=row, sem=("parallel",))(a, b)

    def body(a_ref, b_ref, x_ref, w_ref, dres_ref, dx_ref, dw_ref):
        dx, dw = _rms_bwd(x_ref[...], w_ref[...], _dot(a_ref[...].astype(BF16), b_ref[...], NT))
        dx_ref[...] = dres_ref[...] + dx

        @pl.when(pl.program_id(0) == 0)
        def _():
            dw_ref[...] = jnp.zeros_like(dw_ref)
        dw_ref[...] += dw

    vec = pl.BlockSpec((1, K), lambda i: (0, 0))
    return _pc(body, name=name, out_shape=(_sds((T, K), F32), _sds((1, K), F32)), grid=(T // tm,),
               in_specs=in_specs + [row, vec, row], out_specs=(row, vec), sem=("arbitrary",))(a, b, *norm_bwd)


def mm_tn(a, b, name):
    T, K = a.shape
    N = b.shape[1]
    tt, tn = _tile(T, 1024), _col_tile(N)

    def body(a_ref, b_ref, o_ref):
        @pl.when(pl.program_id(1) == 0)
        def _():
            o_ref[...] = jnp.zeros_like(o_ref)
        o_ref[...] += _dot(a_ref[...].astype(BF16), b_ref[...].astype(BF16), TN)

    return _pc(body, name=name, out_shape=_sds((K, N), F32), grid=(N // tn, T // tt),
               in_specs=[pl.BlockSpec((tt, K), lambda j, t: (t, 0)), pl.BlockSpec((tt, tn), lambda j, t: (t, j))],
               out_specs=pl.BlockSpec((K, tn), lambda j, t: (0, j)), sem=("parallel", "arbitrary"))(a, b)


def ffn_up(xn, wgu, name, deps=()):
    T = xn.shape[0]
    tm = _tile(T, 1024)

    def body(x_ref, w_ref, *rest):
        xv = x_ref[...]
        for j in range(2 * N_FB):
            rest[-1][j] = _dot(xv, w_ref[j]).astype(BF16)

    return _pc(body, name=name, out_shape=_sds((2 * N_FB, T, FB), BF16), grid=(T // tm,),
               in_specs=[pl.BlockSpec((tm, D), lambda i: (i, 0)), _resident((2 * N_FB, D, FB))] + [DEP_SPEC] * len(deps),
               out_specs=pl.BlockSpec((2 * N_FB, tm, FB), lambda i: (0, i, 0)), sem=("parallel",))(xn, wgu, *deps)


def ffn_down(gu, wd, x, name):
    T = x.shape[0]
    tm = _tile(T, 512)

    def body(gu_ref, w_ref, x_ref, o_ref):
        acc = jnp.zeros((tm, D), F32)
        for g in range(N_FB):
            gate, up = gu_ref[g], gu_ref[N_FB + g]
            acc = acc + _dot(gate * _sigmoid(gate) * up, w_ref[g])
        o_ref[...] = x_ref[...] + 0.5 * acc

    row = pl.BlockSpec((tm, D), lambda i: (i, 0))
    return _pc(body, name=name, out_shape=_sds((T, D), F32), grid=(T // tm,),
               in_specs=[pl.BlockSpec((2 * N_FB, tm, FB), lambda i: (0, i, 0)),
                         _resident((N_FB, FB, D)), row],
               out_specs=row, sem=("parallel",))(gu, wd, x)


def _resident(shape):
    return pl.BlockSpec(shape, lambda *_: (0,) * len(shape), pipeline_mode=pl.Buffered(1))


def _store_blocks_bf16(acc, out_hbm, stage, sem):
    for j in range(acc.shape[0]):
        stage[...] = acc[j].astype(BF16)
        copy = pltpu.make_async_copy(stage, out_hbm.at[j], sem)
        copy.start()
        copy.wait()


def ffn_bwd_hidden(dout, wd, gu, name, deps=()):
    T = dout.shape[0]
    tm = _tile(T, 512)
    n_t = T // tm

    def body(d_ref, w_ref, gu_ref, *rest):
        dgu_ref, dwd_hbm, acc, stage, sem = rest[-5:]
        t = pl.program_id(0)

        @pl.when(t == 0)
        def _():
            acc[...] = jnp.zeros_like(acc)
        dy = (0.5 * d_ref[...]).astype(BF16)
        for g in range(N_FB):
            gate, up = gu_ref[g], gu_ref[N_FB + g]
            sg = _sigmoid(gate)
            silu = gate * sg
            dact = _dot(dy, w_ref[g], NT).astype(BF16)
            acc[g] += _dot(silu * up, dy, TN)
            dgu_ref[g] = dact * up * (sg * (1.0 + gate * (1.0 - sg)))
            dgu_ref[N_FB + g] = dact * silu

        @pl.when(t == n_t - 1)
        def _():
            _store_blocks_bf16(acc, dwd_hbm, stage, sem)

    return _pc(body, name=name, out_shape=(_sds((2 * N_FB, T, FB), BF16), _sds((N_FB, FB, D), BF16)), grid=(n_t,),
               in_specs=[pl.BlockSpec((tm, D), lambda i: (i, 0)), _resident((N_FB, FB, D)),
                         pl.BlockSpec((2 * N_FB, tm, FB), lambda i: (0, i, 0))] + [DEP_SPEC] * len(deps),
               out_specs=(pl.BlockSpec((2 * N_FB, tm, FB), lambda i: (0, i, 0)), pl.BlockSpec(memory_space=pl.ANY)),
               scratch=[pltpu.VMEM((N_FB, FB, D), F32), pltpu.VMEM((FB, D), BF16), pltpu.SemaphoreType.DMA],
               sem=("arbitrary",))(dout, wd, gu, *deps)


def ffn_bwd_input(dgu, wgu, x, dout, nw, name, deps=()):
    T = x.shape[0]
    tm = _tile(T, 512)

    def body(dgu_ref, w_ref, x_ref, d_ref, nw_ref, *rest):
        dx_ref, dnw_ref = rest[-2:]
        dxn = jnp.zeros((tm, D), F32)
        for j in range(2 * N_FB):
            dxn = dxn + _dot(dgu_ref[j], w_ref[j], NT)
        dx, dw = _rms_bwd(x_ref[...], nw_ref[...], dxn)
        dx_ref[...] = d_ref[...] + dx

        @pl.when(pl.program_id(0) == 0)
        def _():
            dnw_ref[...] = jnp.zeros_like(dnw_ref)
        dnw_ref[...] += dw

    row = pl.BlockSpec((tm, D), lambda i: (i, 0))
    vec = pl.BlockSpec((1, D), lambda i: (0, 0))
    return _pc(body, name=name, out_shape=(_sds((T, D), F32), _sds((1, D), F32)), grid=(T // tm,),
               in_specs=[pl.BlockSpec((2 * N_FB, tm, FB), lambda i: (0, i, 0)), _resident((2 * N_FB, D, FB)),
                         row, row, vec] + [DEP_SPEC] * len(deps),
               out_specs=(row, vec), sem=("arbitrary",))(dgu, wgu, x, dout, nw, *deps)


def ffn_wgrad_gu(xn, dgu, name):
    T = xn.shape[0]
    tt = _tile(T, 1024)
    n_t = T // tt

    def body(x_ref, d_ref, dw_hbm, acc, stage, sem):
        t = pl.program_id(0)

        @pl.when(t == 0)
        def _():
            acc[...] = jnp.zeros_like(acc)
        xn_tile = x_ref[...]
        for j in range(2 * N_FB):
            acc[j] += _dot(xn_tile, d_ref[j], TN)

        @pl.when(t == n_t - 1)
        def _():
            _store_blocks_bf16(acc, dw_hbm, stage, sem)

    return _pc(body, name=name, out_shape=_sds((2 * N_FB, D, FB), BF16), grid=(n_t,),
               in_specs=[pl.BlockSpec((tt, D), lambda t: (t, 0)), pl.BlockSpec((2 * N_FB, tt, FB), lambda t: (0, t, 0))],
               out_specs=pl.BlockSpec(memory_space=pl.ANY),
               scratch=[pltpu.VMEM((2 * N_FB, D, FB), F32), pltpu.VMEM((D, FB), BF16), pltpu.SemaphoreType.DMA],
               sem=("arbitrary",))(xn, dgu)


def ffn_forward(x, nw, wgu, wd, tag):
    T = x.shape[0]
    tm = _tile(T, 512)

    def body(x_ref, nw_ref, wgu_ref, wd_ref, o_ref, xn_ref, gu_ref):
        xv = x_ref[...]
        xn = _rms_fwd(xv, nw_ref[...]).astype(BF16)
        xn_ref[...] = xn
        for j in range(2 * N_FB):
            gu_ref[j] = _dot(xn, wgu_ref[j]).astype(BF16)
        acc = jnp.zeros((tm, D), F32)
        for g in range(N_FB):
            gate, up = gu_ref[g], gu_ref[N_FB + g]
            acc = acc + _dot(gate * _sigmoid(gate) * up, wd_ref[g])
        o_ref[...] = xv + 0.5 * acc

    row = pl.BlockSpec((tm, D), lambda i: (i, 0))
    out, xn, gu = _pc(body, name=f"{tag}_fwd",
                      out_shape=(_sds((T, D), F32), _sds((T, D), BF16), _sds((2 * N_FB, T, FB), BF16)), grid=(T // tm,),
                      in_specs=[row, pl.BlockSpec((1, D), lambda i: (0, 0)), _resident((2 * N_FB, D, FB)),
                                _resident((N_FB, FB, D))],
                      out_specs=(row, row, pl.BlockSpec((2 * N_FB, tm, FB), lambda i: (0, i, 0))),
                      sem=("parallel",))(x, nw, wgu, wd)
    return out, (x, xn, gu)


def ffn_backward(dout, saved, nw, wgu, wd, tag, deps=(), on_grads=None):
    x, xn, gu = saved
    dgu, dwd = ffn_bwd_hidden(dout, wd, gu, f"{tag}_bwd_hidden", deps)
    dwgu = ffn_wgrad_gu(xn, dgu, f"{tag}_wgrad_gu")
    late = on_grads(dwgu, dwd) if on_grads else ()
    dx, dnw = ffn_bwd_input(dgu, wgu, x, dout, nw, f"{tag}_bwd_input", late)
    return dx, dnw, dwgu, dwd


N_QKV_BLK = 3 * HEADS_A
Z_BLK0 = N_QKV_BLK
MAIN_COLS = 4 * D
HALO = 16


def _conv_taps(xcat, w):
    c = xcat[HALO:] * w[3:4]
    for k in range(3):
        c = c + pltpu.roll(xcat, 3 - k, 0)[HALO:] * w[k:k + 1]
    return c


def _head_cols(h):
    return slice(128 * h, 128 * (h + 1))


def gdn_conv_fwd(proj, wconv, name):
    T = proj.shape[0]
    tm = _tile(T, 512)

    def body(cur_ref, prev_ref, w_ref, c_ref, y_ref):
        kind, t = pl.program_id(0), pl.program_id(1)
        prev = jnp.where(t > 0, prev_ref[...].astype(F32), 0.0)
        c = _conv_taps(jnp.concatenate([prev, cur_ref[...].astype(F32)], axis=0), w_ref[...])
        c_ref[...] = c.astype(BF16)
        s = c * _sigmoid(c)
        scale = jnp.where(kind == 0, DK ** -0.5, 1.0)
        for h in range(HEADS_A):
            sh = s[:, _head_cols(h)]
            r = lax.rsqrt(jnp.sum(sh * sh, axis=-1, keepdims=True) + EPS)
            y_ref[h] = (sh * jnp.where(kind < 2, r * scale, 1.0)).astype(BF16)

    return _pc(body, name=name, out_shape=(_sds((T, 3 * D), BF16), _sds((N_QKV_BLK, T, 128), BF16)),
               grid=(3, T // tm),
               in_specs=[pl.BlockSpec((tm, D), lambda kd, t: (t, kd)),
                         pl.BlockSpec((HALO, D), lambda kd, t: (jnp.maximum(t * (tm // HALO) - 1, 0), kd)),
                         pl.BlockSpec((4, D), lambda kd, t: (0, kd))],
               out_specs=(pl.BlockSpec((tm, D), lambda kd, t: (t, kd)),
                          pl.BlockSpec((HEADS_A, tm, 128), lambda kd, t: (kd, t, 0))),
               sem=("parallel", "parallel"))(proj, proj, wconv)


def gdn_conv_bwd(dqkv, c, proj, wconv, dproj, name):
    T = c.shape[0]
    tm = _tile(T, 512)
    n_t = T // tm

    def body(dy_ref, dyn_ref, c_ref, cn_ref, x_ref, w_ref, _, dx_ref, dw_ref):
        kind, t = pl.program_id(0), pl.program_id(1)
        scale = jnp.where(kind == 0, DK ** -0.5, 1.0)

        def act_bwd(dy, cv):
            sg = _sigmoid(cv)
            s = cv * sg
            parts = []
            for h in range(HEADS_A):
                sh, dyh = s[:, _head_cols(h)], dy[h]
                r = lax.rsqrt(jnp.sum(sh * sh, axis=-1, keepdims=True) + EPS)
                ds_norm = scale * r * (dyh - (r * r) * sh * jnp.sum(dyh * sh, axis=-1, keepdims=True))
                parts.append(jnp.where(kind < 2, ds_norm, dyh))
            return jnp.concatenate(parts, axis=1) * (sg * (1.0 + cv * (1.0 - sg)))

        w = w_ref[...]
        dcur = act_bwd(dy_ref[...].astype(F32), c_ref[...].astype(F32))
        dnext = jnp.where(t < n_t - 1, act_bwd(dyn_ref[...].astype(F32), cn_ref[...].astype(F32)), 0.0)
        dcat = jnp.concatenate([dcur, dnext], axis=0)
        xcur = x_ref[...].astype(F32)
        dx = dcur * w[3:4]
        rows = [None, None, None, jnp.sum(dcur * xcur, axis=0, keepdims=True)]
        for k in range(3):
            ahead = pltpu.roll(dcat, tm + HALO - (3 - k), 0)[:tm]
            dx = dx + ahead * w[k:k + 1]
            rows[k] = jnp.sum(ahead * xcur, axis=0, keepdims=True)
        dx_ref[...] = dx.astype(BF16)

        @pl.when(t == 0)
        def _():
            dw_ref[...] = jnp.zeros_like(dw_ref)
        dw_ref[...] += jnp.concatenate(rows, axis=0)

    def nxt(t):
        return jnp.minimum((t + 1) * (tm // HALO), T // HALO - 1)

    cur = pl.BlockSpec((tm, D), lambda kd, t: (t, kd))
    return _pc(body, name=name, out_shape=(_sds(dproj.shape, BF16), _sds((4, 3 * D), F32)), grid=(3, n_t),
               in_specs=[pl.BlockSpec((HEADS_A, tm, 128), lambda kd, t: (kd, t, 0)),
                         pl.BlockSpec((HEADS_A, HALO, 128), lambda kd, t: (kd, nxt(t), 0)),
                         cur, pl.BlockSpec((HALO, D), lambda kd, t: (nxt(t), kd)),
                         cur, pl.BlockSpec((4, D), lambda kd, t: (0, kd)), DEP_SPEC],
               out_specs=(cur, pl.BlockSpec((4, D), lambda kd, t: (0, kd))), input_output_aliases={6: 0},
               sem=("parallel", "arbitrary"))(dqkv, dqkv, c, c, proj, wconv, dproj)


def _chunk_masks(n):
    ri = lax.broadcasted_iota(jnp.int32, (n, n), 0)
    ci = lax.broadcasted_iota(jnp.int32, (n, n), 1)
    same = (ri // CHUNK) == (ci // CHUNK)
    return same & (ri >= ci), same & (ri <= ci)


def gdn_gate_fwd(ba, al, dtb, name):
    T = ba.shape[0]
    tg = _tile(T, PREP_T)

    def body(ba_ref, al_ref, dtb_ref, o_ref):
        x = ba_ref[...]
        lane = lax.broadcasted_iota(jnp.int32, x.shape, 1)
        is_a = (lane >= HEADS_A) & (lane < 2 * HEADS_A)
        g = jnp.where(is_a, -jnp.exp(al_ref[...]) * _softplus(x + dtb_ref[...]), 0.0)
        lower, _ = _chunk_masks(tg)
        gc = _dot(lower.astype(F32), g, precision=HI)
        o_ref[...] = jnp.where(lane < HEADS_A, _sigmoid(x), gc)

    vec = pl.BlockSpec((1, 128), lambda i: (0, 0))
    return _pc(body, name=name, out_shape=_sds((T, 128), F32), grid=(T // tg,),
               in_specs=[pl.BlockSpec((tg, 128), lambda i: (i, 0)), vec, vec],
               out_specs=pl.BlockSpec((tg, 128), lambda i: (i, 0)), sem=("parallel",))(ba, al, dtb)


def gdn_gate_bwd(ba, al, dtb, dgb, dproj, name):
    T = ba.shape[0]
    tg = _tile(T, PREP_T)

    def body(ba_ref, al_ref, dtb_ref, dgb_ref, _, dba_ref, dal_ref, ddt_ref):
        x, d = ba_ref[...], dgb_ref[...]
        lane = lax.broadcasted_iota(jnp.int32, x.shape, 1)
        is_b = lane < HEADS_A
        is_a = (lane >= HEADS_A) & (lane < 2 * HEADS_A)
        beta = _sigmoid(x)
        e_a = jnp.exp(al_ref[...])
        z = x + dtb_ref[...]
        g = jnp.where(is_a, -e_a * _softplus(z), 0.0)
        _, upper = _chunk_masks(tg)
        dg = _dot(upper.astype(F32), jnp.where(is_a, d, 0.0), precision=HI)
        da = jnp.where(is_a, dg * (-e_a) * _sigmoid(z), 0.0)
        db = jnp.where(is_b, d * beta * (1.0 - beta), 0.0)
        dba_ref[...] = (da + db).astype(BF16)

        @pl.when(pl.program_id(0) == 0)
        def _():
            dal_ref[...] = jnp.zeros_like(dal_ref)
            ddt_ref[...] = jnp.zeros_like(ddt_ref)
        dal_ref[...] += jnp.sum(dg * g, axis=0, keepdims=True)
        ddt_ref[...] += jnp.sum(da, axis=0, keepdims=True)

    vec = pl.BlockSpec((1, 128), lambda i: (0, 0))
    blk = pl.BlockSpec((tg, 128), lambda i: (i, 0))
    ba_cols = pl.BlockSpec((tg, 128), lambda i: (i, A_COLS // 128 - 1))
    return _pc(body, name=name, out_shape=(_sds(dproj.shape, BF16), _sds((1, 128), F32), _sds((1, 128), F32)),
               grid=(T // tg,), in_specs=[blk, vec, vec, blk, DEP_SPEC],
               out_specs=(ba_cols, vec, vec), input_output_aliases={4: 0}, sem=("arbitrary",))(ba, al, dtb, dgb, dproj)


def _bmm(a, b, dims, precision=None):
    return lax.dot_general(a, b, dims, preferred_element_type=F32, precision=precision)


B_NN = (((2,), (1,)), ((0,), (0,)))
B_NT = (((2,), (2,)), ((0,), (0,)))


def _select_lane(x, lane_index):
    lane = lax.broadcasted_iota(jnp.int32, x.shape, x.ndim - 1)
    return jnp.sum(jnp.where(lane == lane_index, x, 0.0), axis=-1, keepdims=True)


B_TN = (((1,), (1,)), ((0,), (0,)))


def _bmm_split(a, b, dims):
    ah, bh = a.astype(BF16), b.astype(BF16)
    al, bl = (a - ah.astype(F32)).astype(BF16), (b - bh.astype(F32)).astype(BF16)
    return _bmm(ah, bh, dims) + (_bmm(ah, bl, dims) + _bmm(al, bh, dims))


@jax.custom_vjp
def _bmm_f32(a, b):
    return _bmm_split(a, b, B_NN)


def _bmm_f32_fwd(a, b):
    return _bmm_split(a, b, B_NN), (a, b)


def _bmm_bf16(a, b, dims):
    return _bmm(a.astype(BF16), b.astype(BF16), dims)


def _bmm_f32_bwd(res, dc):
    a, b = res
    return _bmm_bf16(dc, b, B_NT), _bmm_bf16(a, dc, B_TN)


_bmm_f32.defvjp(_bmm_f32_fwd, _bmm_f32_bwd)


def _tri_inverse(lmat):
    ri = lax.broadcasted_iota(jnp.int32, lmat.shape, 1)
    ci = lax.broadcasted_iota(jnp.int32, lmat.shape, 2)
    eye = jnp.where(ri == ci, 1.0, 0.0)
    inv = eye - lmat
    power = lmat
    for _ in range(5):
        power = _bmm_bf16(power, power, B_NN)
        inv = inv + _bmm_bf16(inv, power, B_NN)
    return _bmm_split(inv, 2.0 * eye - _bmm_split(eye + lmat, inv, B_NN), B_NN)


def _stored_inverse(x):
    @jax.custom_vjp
    def inverse(lmat):
        return x

    def fwd(lmat):
        return x, None

    def bwd(_, dx):
        return (-_bmm_bf16(_bmm_bf16(x, dx, B_TN), x, B_NT),)

    inverse.defvjp(fwd, bwd)
    return inverse


def _gdn_prep(q, k, v, gb, h, inverse):
    nb = q.shape[0]
    beta = _select_lane(gb, h)
    gc = _select_lane(gb, HEADS_A + h)
    ri = lax.broadcasted_iota(jnp.int32, (nb, CHUNK, CHUNK), 1)
    ci = lax.broadcasted_iota(jnp.int32, (nb, CHUNK, CHUNK), 2)
    lower, strict, eye = ri >= ci, ri > ci, ri == ci
    gcol = jnp.broadcast_to(gc, (nb, CHUNK, CHUNK))
    grow = jnp.swapaxes(gcol, 1, 2)
    decay = jnp.where(lower, jnp.exp(jnp.where(lower, gcol - grow, 0.0)), 0.0)
    kb = k * beta
    kbf = k.astype(BF16)
    inv = inverse(jnp.where(strict, _bmm(kb.astype(BF16), kbf, B_NT) * decay, 0.0))
    eg = jnp.exp(gc)
    sol = _bmm_f32(inv, jnp.concatenate([v * beta, kb * eg], axis=-1))
    aqk = _bmm(q.astype(BF16), kbf, B_NT) * decay
    g_last = gc[:, CHUNK - 1:CHUNK, :]
    gl = jnp.broadcast_to(jnp.exp(g_last), (nb, 1, 128))
    return (sol[..., :DK], sol[..., DK:], q * eg, k * jnp.exp(g_last - gc), aqk, gl), inv


def gdn_prep_fwd(qkv, gb, name):
    T = qkv.shape[1]
    tp = _tile(T, 4 * PREP_T)
    nb = tp // CHUNK

    def body(q_ref, k_ref, v_ref, gb_ref, u_ref, w_ref, qd_ref, kd_ref, a_ref, gl_ref, inv_ref):
        h = pl.program_id(1)
        shp = (nb, CHUNK, 128)
        q, k, v = (ref[0].astype(F32).reshape(shp) for ref in (q_ref, k_ref, v_ref))
        (u, w, qd, kd, aqk, gl), inv = _gdn_prep(q, k, v, gb_ref[...].reshape(shp), h, _tri_inverse)
        u_ref[0] = u.reshape(tp, 128)
        w_ref[0] = w.reshape(tp, 128).astype(BF16)
        qd_ref[0] = qd.reshape(tp, 128).astype(BF16)
        kd_ref[0] = kd.reshape(tp, 128).astype(BF16)
        a_ref[0] = aqk.reshape(tp, CHUNK).astype(BF16)
        gl_ref[0] = gl.reshape(nb, 1, 128)
        inv_ref[0] = inv.reshape(tp, CHUNK)

    def head(off):
        return pl.BlockSpec((1, tp, 128), lambda n, h: (h + off, n, 0))

    matmul_only = _sds((HEADS_A, T, 128), BF16)
    narrow = pl.BlockSpec((1, tp, CHUNK), lambda n, h: (h, n, 0))
    return _pc(body, name=name,
               out_shape=(_sds((HEADS_A, T, 128), F32), matmul_only, matmul_only, matmul_only, _sds((HEADS_A, T, CHUNK), BF16),
                          _sds((HEADS_A, T // CHUNK, 1, 128), F32), _sds((HEADS_A, T, CHUNK), F32)),
               grid=(T // tp, HEADS_A),
               in_specs=[head(0), head(HEADS_A), head(2 * HEADS_A), pl.BlockSpec((tp, 128), lambda n, h: (n, 0))],
               out_specs=(head(0), head(0), head(0), head(0), narrow,
                          pl.BlockSpec((1, nb, 1, 128), lambda n, h: (h, n, 0, 0)), narrow),
               sem=("parallel", "parallel"))(qkv, qkv, qkv, gb)


def gdn_prep_bwd(qkv, gb, inv, du, dw, dqd, dkd, da, dgl, name):
    T = qkv.shape[1]
    tp = _tile(T, 2 * PREP_T)
    nb = tp // CHUNK

    def body(q_ref, k_ref, v_ref, gb_ref, inv_ref, du_ref, dw_ref, dqd_ref, dkd_ref, da_ref, dgl_ref, dqkv_ref, dgb_ref):
        h = pl.program_id(1)
        shp = (nb, CHUNK, 128)
        stored = _stored_inverse(inv_ref[0].reshape(nb, CHUNK, CHUNK))
        q, k, v = (ref[0].astype(F32).reshape(shp) for ref in (q_ref, k_ref, v_ref))
        _, vjp = jax.vjp(lambda q, k, v, gb: _gdn_prep(q, k, v, gb, h, stored)[0], q, k, v, gb_ref[...].reshape(shp))
        dq, dk, dv, dgb = vjp((du_ref[0].reshape(shp), dw_ref[0].reshape(shp), dqd_ref[0].reshape(shp),
                               dkd_ref[0].reshape(shp), da_ref[0].reshape(nb, CHUNK, CHUNK), dgl_ref[0].reshape(nb, 1, 128)))
        dqkv_ref[h] = dq.reshape(tp, 128).astype(BF16)
        dqkv_ref[HEADS_A + h] = dk.reshape(tp, 128).astype(BF16)
        dqkv_ref[2 * HEADS_A + h] = dv.reshape(tp, 128).astype(BF16)

        @pl.when(h == 0)
        def _():
            dgb_ref[...] = jnp.zeros_like(dgb_ref)
        dgb_ref[...] += dgb.reshape(tp, 128)

    def head(off):
        return pl.BlockSpec((1, tp, 128), lambda n, h: (h + off, n, 0))

    narrow = pl.BlockSpec((1, tp, CHUNK), lambda n, h: (h, n, 0))
    return _pc(body, name=name, out_shape=(_sds((N_QKV_BLK, T, 128), BF16), _sds((T, 128), F32)),
               grid=(T // tp, HEADS_A),
               in_specs=[head(0), head(HEADS_A), head(2 * HEADS_A), pl.BlockSpec((tp, 128), lambda n, h: (n, 0)), narrow,
                         head(0), head(0), head(0), head(0), narrow,
                         pl.BlockSpec((1, nb, 1, 128), lambda n, h: (h, n, 0, 0))],
               out_specs=(pl.BlockSpec((N_QKV_BLK, tp, 128), lambda n, h: (0, n, 0)),
                          pl.BlockSpec((tp, 128), lambda n, h: (n, 0))),
               sem=("parallel", "arbitrary"))(qkv, qkv, qkv, gb, inv, du, dw, dqd, dkd, da, dgl)


def gdn_scan_fwd(u, w, qd, kd, aqk, gl, name):
    T = u.shape[1]
    n_chunks = T // CHUNK

    def body(u_ref, w_ref, qd_ref, kd_ref, a_ref, gl_ref, o_ref, sin_ref, state):
        @pl.when(pl.program_id(0) == 0)
        def _():
            state[...] = jnp.zeros_like(state)
        s = state[...]
        sb = s.astype(BF16)
        sin_ref[0] = sb
        both = _bmm(jnp.concatenate([w_ref[...], qd_ref[...]], axis=1).astype(BF16), sb, B_NN)
        vn = (u_ref[...] - both[:, :CHUNK]).astype(BF16)
        o_ref[...] = both[:, CHUNK:] + _bmm(a_ref[...].astype(BF16), vn, B_NN)
        state[...] = s * gl_ref[:, 0] + _bmm(kd_ref[...].astype(BF16), vn, B_TN)

    blk = pl.BlockSpec((HEADS_A, CHUNK, 128), lambda n: (0, n, 0))
    return _pc(body, name=name,
               out_shape=(_sds((HEADS_A, T, 128), F32), _sds((n_chunks, HEADS_A, DK, 128), BF16)), grid=(n_chunks,),
               in_specs=[blk, blk, blk, blk, pl.BlockSpec((HEADS_A, CHUNK, CHUNK), lambda n: (0, n, 0)),
                         pl.BlockSpec((HEADS_A, 1, 1, 128), lambda n: (0, n, 0, 0))],
               out_specs=(blk, pl.BlockSpec((1, HEADS_A, DK, 128), lambda n: (n, 0, 0, 0))),
               scratch=[pltpu.VMEM((HEADS_A, DK, 128), F32)], sem=("arbitrary",))(u, w, qd, kd, aqk, gl)


def gdn_scan_bwd(u, w, qd, kd, aqk, gl, sin, do, name):
    T = u.shape[1]
    n_chunks = T // CHUNK

    def body(u_ref, w_ref, qd_ref, kd_ref, a_ref, gl_ref, sin_ref, do_ref,
             du_ref, dw_ref, dqd_ref, dkd_ref, da_ref, dgl_ref, dstate):
        @pl.when(pl.program_id(0) == 0)
        def _():
            dstate[...] = jnp.zeros_like(dstate)
        lane0 = lax.broadcasted_iota(jnp.int32, (HEADS_A, 1, 128), 2) == 0
        sb = sin_ref[0]
        s = sb.astype(F32)
        wb, qdb, kdb = w_ref[...].astype(BF16), qd_ref[...].astype(BF16), kd_ref[...].astype(BF16)
        ab, dob = a_ref[...].astype(BF16), do_ref[...].astype(BF16)
        vn = (u_ref[...] - _bmm(wb, sb, B_NN)).astype(BF16)
        ds_out = dstate[...]
        dsb = ds_out.astype(BF16)
        dqd_ref[...] = _bmm(dob, sb, B_NT)
        da_ref[...] = _bmm(dob, vn, B_NT)
        dv = _bmm(ab, dob, B_TN) + _bmm(kdb, dsb, B_NN)
        dkd_ref[...] = _bmm(vn, dsb, B_NT)
        dgl = jnp.sum(jnp.sum(ds_out * s, axis=2, keepdims=True), axis=1, keepdims=True)
        dgl_ref[:, 0] = jnp.where(lane0, dgl, 0.0)
        du_ref[...] = dv
        dvb = dv.astype(BF16)
        dw_ref[...] = -_bmm(dvb, sb, B_NT)
        dstate[...] = ds_out * gl_ref[:, 0] + _bmm(qdb, dob, B_TN) - _bmm(wb, dvb, B_TN)

    last = n_chunks - 1
    blk = pl.BlockSpec((HEADS_A, CHUNK, 128), lambda n: (0, last - n, 0))
    ablk = pl.BlockSpec((HEADS_A, CHUNK, CHUNK), lambda n: (0, last - n, 0))
    glblk = pl.BlockSpec((HEADS_A, 1, 1, 128), lambda n: (0, last - n, 0, 0))
    per_head = _sds((HEADS_A, T, 128), F32)
    return _pc(body, name=name,
               out_shape=(per_head, per_head, per_head, per_head, _sds((HEADS_A, T, CHUNK), F32),
                          _sds((HEADS_A, n_chunks, 1, 128), F32)), grid=(n_chunks,),
               in_specs=[blk, blk, blk, blk, ablk, glblk,
                         pl.BlockSpec((1, HEADS_A, DK, 128), lambda n: (last - n, 0, 0, 0)), blk],
               out_specs=(blk, blk, blk, blk, ablk, glblk),
               scratch=[pltpu.VMEM((HEADS_A, DK, 128), F32)], sem=("arbitrary",))(u, w, qd, kd, aqk, gl, sin, do)


def gdn_outnorm_fwd(o, proj, wn, name):
    T = o.shape[1]
    tm = _tile(T, 512)

    def body(o_ref, z_ref, wn_ref, y_ref):
        for h in range(HEADS_A):
            z = z_ref[:, 128 * h:128 * (h + 1)].astype(F32)
            y_ref[:, 128 * h:128 * (h + 1)] = (_rms_fwd(o_ref[h], wn_ref[...]) * (z * _sigmoid(z))).astype(BF16)

    return _pc(body, name=name, out_shape=_sds((T, D), BF16), grid=(T // tm,),
               in_specs=[pl.BlockSpec((HEADS_A, tm, 128), lambda i: (0, i, 0)),
                         pl.BlockSpec((tm, D), lambda i: (i, Z_BLK0 * 128 // D)), pl.BlockSpec((1, 128), lambda i: (0, 0))],
               out_specs=pl.BlockSpec((tm, D), lambda i: (i, 0)), sem=("parallel",))(o, proj, wn)


def gdn_outnorm_bwd(o, proj, wn, dy, name):
    T = o.shape[1]
    tm = _tile(T, 512)

    def body(o_ref, z_ref, wn_ref, dy_ref, do_ref, dz_ref, dwn_ref):
        wn = wn_ref[...]
        acc = jnp.zeros((1, 128), F32)
        for h in range(HEADS_A):
            cols = slice(128 * h, 128 * (h + 1))
            z, dyh, ov = z_ref[:, cols].astype(F32), dy_ref[:, cols], o_ref[h]
            sg = _sigmoid(z)
            do, dwn = _rms_bwd(ov, wn, dyh * (z * sg))
            do_ref[h] = do
            acc = acc + dwn
            dz_ref[:, cols] = (dyh * _rms_fwd(ov, wn) * (sg * (1.0 + z * (1.0 - sg)))).astype(BF16)

        @pl.when(pl.program_id(0) == 0)
        def _():
            dwn_ref[...] = jnp.zeros_like(dwn_ref)
        dwn_ref[...] += acc

    row = pl.BlockSpec((tm, D), lambda i: (i, 0))
    vec = pl.BlockSpec((1, 128), lambda i: (0, 0))
    hblk = pl.BlockSpec((HEADS_A, tm, 128), lambda i: (0, i, 0))
    z_cols = pl.BlockSpec((tm, D), lambda i: (i, Z_BLK0 * 128 // D))
    return _pc(body, name=name, out_shape=(_sds((HEADS_A, T, 128), F32), _sds((T, A_COLS), BF16), _sds((1, 128), F32)),
               grid=(T // tm,), in_specs=[hblk, z_cols, vec, row],
               out_specs=(hblk, z_cols, vec), sem=("arbitrary",))(o, proj, wn, dy)


def gdn_forward(x, nw, w_in, wconv, al, dtb, wn, w_out, tag, deps=()):
    h = rmsnorm_bf16(x, nw, f"{tag}_norm", deps)
    proj = mm_nn(h, w_in, f"{tag}_proj", out_dtype=BF16, cols=(0, MAIN_COLS))
    ba = mm_nn(h, w_in, f"{tag}_proj_ba", cols=(MAIN_COLS, A_COLS))
    c, qkv = gdn_conv_fwd(proj, wconv, f"{tag}_conv")
    gb = gdn_gate_fwd(ba, al, dtb, f"{tag}_gate")
    u, w, qd, kd, aqk, gl, inv = gdn_prep_fwd(qkv, gb, f"{tag}_prep")
    o, sin = gdn_scan_fwd(u, w, qd, kd, aqk, gl, f"{tag}_scan")
    on = gdn_outnorm_fwd(o, proj, wn, f"{tag}_outnorm")
    y = mm_nn(on, w_out, f"{tag}_out", residual=x)
    return y, (x, h, proj, ba, c, qkv, gb, inv, (u, w, qd, kd, aqk, gl), sin, o, on)


def gdn_backward(dout, saved, nw, w_in, wconv, al, dtb, wn, w_out, tag):
    x, h, proj, ba, c, qkv, gb, inv, prep, sin, o, on = saved
    d_on = mm_nt(dout, w_out, f"{tag}_out_bwd")
    dw_out = mm_tn(on, dout, f"{tag}_out_wgrad")
    do, dproj, dwn = gdn_outnorm_bwd(o, proj, wn, d_on, f"{tag}_outnorm_bwd")
    du, dw, dqd, dkd, da, dgl = gdn_scan_bwd(*prep, sin, do, f"{tag}_scan_bwd")
    dqkv, dgb = gdn_prep_bwd(qkv, gb, inv, du, dw, dqd, dkd, da, dgl, f"{tag}_prep_bwd")
    dproj, dal, ddt = gdn_gate_bwd(ba, al, dtb, dgb, dproj, f"{tag}_gate_bwd")
    dproj, dwconv = gdn_conv_bwd(dqkv, c, proj, wconv, dproj, f"{tag}_conv_bwd")
    dw_in = mm_tn(h, dproj, f"{tag}_proj_wgrad")
    dx, dnw = mm_nt(dproj, w_in, f"{tag}_proj_bwd", norm_bwd=(x, nw, dout))
    return dx, dnw, dw_in, dwconv, dal, ddt, dwn, dw_out


N_KV, GROUP = 4, 4
KV_COLS = 2 * N_KV * B_HD
B_COLS = D + KV_COLS


@jax.custom_vjp
def _swap_lane_halves(x):
    return pltpu.roll(x, 64, 1)


_swap_lane_halves.defvjp(lambda x: (pltpu.roll(x, 64, 1), None), lambda _, g: (pltpu.roll(g, 64, 1),))


def _swa_block(q, kp, kc, vp, vc, sk, first):
    cols = GROUP * B_BLK
    ks = lax.broadcasted_iota(jnp.int32, (N_KV, B_BLK, cols), 1)
    qi = lax.broadcasted_iota(jnp.int32, (N_KV, B_BLK, cols), 2) % B_BLK
    from_cur = ks <= qi

    def batch(parts):
        return jnp.concatenate([part[None] for part in parts], axis=0)

    def per_kv(cur, prev):
        return batch([jnp.concatenate([cur[:, j * B_HD:(j + 1) * B_HD], prev[:, j * B_HD:(j + 1) * B_HD]], axis=0)
                      for j in range(N_KV)]).astype(BF16)

    qs = batch([jnp.concatenate([q[:, hq * B_HD:(hq + 1) * B_HD] for hq in range(GROUP * j, GROUP * (j + 1))], axis=0)
                for j in range(N_KV)])
    q_t = jnp.swapaxes(qs, 1, 2).astype(BF16)
    sink = batch([jnp.concatenate([jnp.broadcast_to(sk[:, hq:hq + 1], (1, B_BLK))
                                   for hq in range(GROUP * j, GROUP * (j + 1))], axis=1) for j in range(N_KV)])
    both = _bmm(per_kv(kc, kp), q_t, B_NN)
    s = jnp.where(from_cur, both[:, :B_BLK], jnp.where(first, -1e30, both[:, B_BLK:])) * (B_HD ** -0.5)
    m = lax.stop_gradient(jnp.maximum(jnp.max(s, axis=1, keepdims=True), sink))
    e = jnp.exp((s - m).astype(BF16))
    den = jnp.sum(e.astype(F32), axis=1, keepdims=True) + jnp.exp(sink - m)
    p = e * (1.0 / den).astype(BF16)
    zero = jnp.zeros_like(p)
    p_both = jnp.concatenate([jnp.where(from_cur, p, zero), jnp.where(from_cur, zero, p)], axis=1)
    o = jnp.swapaxes(_bmm(per_kv(vc, vp), p_both, B_TN), 1, 2)
    return jnp.concatenate([o[j, g * B_BLK:(g + 1) * B_BLK] for j in range(N_KV) for g in range(GROUP)], axis=1)


def swa_core_fwd(proj, sk, name):
    T = proj.shape[0]
    half = N_KV * B_HD

    def body(q_ref, kvc_ref, kvp_ref, sk_ref, o_ref):
        kvc, kvp = kvc_ref[...], kvp_ref[...]
        o_ref[...] = _swa_block(q_ref[...], kvp[:, :half], kvc[:, :half], kvp[:, half:], kvc[:, half:], sk_ref[...],
                                pl.program_id(0) == 0).astype(BF16)

    return _pc(body, name=name, out_shape=_sds((T, D), BF16), grid=(T // B_BLK,),
               in_specs=[pl.BlockSpec((B_BLK, D), lambda n: (n, 0)),
                         pl.BlockSpec((B_BLK, KV_COLS), lambda n: (n, D // KV_COLS)),
                         pl.BlockSpec((B_BLK, KV_COLS), lambda n: (jnp.maximum(n - 1, 0), D // KV_COLS)),
                         pl.BlockSpec((1, 128), lambda n: (0, 0))],
               out_specs=pl.BlockSpec((B_BLK, D), lambda n: (n, 0)), sem=("parallel",))(proj, proj, proj, sk)


def swa_core_bwd(proj, sk, do, name):
    T = proj.shape[0]
    last = T // B_BLK - 1
    half = N_KV * B_HD

    def body(q_ref, kvc_ref, kvp_ref, sk_ref, do_ref, dproj_ref, dbias_ref, dsk_ref, carry):
        step = pl.program_id(0)
        first = step == last

        @pl.when(step == 0)
        def _():
            carry[...] = jnp.zeros_like(carry)
            dbias_ref[...] = jnp.zeros_like(dbias_ref)
            dsk_ref[...] = jnp.zeros_like(dsk_ref)
        kvc, kvp = kvc_ref[...], kvp_ref[...]
        _, vjp = jax.vjp(functools.partial(_swa_block, first=first), q_ref[...], kvp[:, :half], kvc[:, :half],
                         kvp[:, half:], kvc[:, half:], sk_ref[...])
        dq, dkp, dkc, dvp, dvc, dsk = vjp(do_ref[...])
        dkv = jnp.concatenate([dkc, dvc], axis=1) + carry[...]
        carry[...] = jnp.concatenate([dkp, dvp], axis=1)
        row = jnp.concatenate([dq, dkv], axis=1)
        dproj_ref[...] = row.astype(BF16)
        dbias_ref[...] += jnp.sum(row, axis=0, keepdims=True)
        dsk_ref[...] += dsk

    return _pc(body, name=name, out_shape=(_sds((T, B_COLS), BF16), _sds((1, B_COLS), F32), _sds((1, 128), F32)),
               grid=(T // B_BLK,),
               in_specs=[pl.BlockSpec((B_BLK, D), lambda n: (last - n, 0)),
                         pl.BlockSpec((B_BLK, KV_COLS), lambda n: (last - n, D // KV_COLS)),
                         pl.BlockSpec((B_BLK, KV_COLS), lambda n: (jnp.maximum(last - n - 1, 0), D // KV_COLS)),
                         pl.BlockSpec((1, 128), lambda n: (0, 0)), pl.BlockSpec((B_BLK, D), lambda n: (last - n, 0))],
               out_specs=(pl.BlockSpec((B_BLK, B_COLS), lambda n: (last - n, 0)),
                          pl.BlockSpec((1, B_COLS), lambda n: (0, 0)), pl.BlockSpec((1, 128), lambda n: (0, 0))),
               scratch=[pltpu.VMEM((B_BLK, KV_COLS), F32)], sem=("arbitrary",))(proj, proj, proj, sk, do)


def col_sum(a, name):
    T, N = a.shape
    tm = _tile(T, 1024)

    def body(a_ref, o_ref):
        @pl.when(pl.program_id(0) == 0)
        def _():
            o_ref[...] = jnp.zeros_like(o_ref)
        o_ref[...] += jnp.sum(a_ref[...].astype(F32), axis=0, keepdims=True)

    return _pc(body, name=name, out_shape=_sds((1, N), F32), grid=(T // tm,),
               in_specs=[pl.BlockSpec((tm, N), lambda i: (i, 0))], out_specs=pl.BlockSpec((1, N), lambda i: (0, 0)),
               sem=("arbitrary",))(a)


def swa_forward(x, nw, w_in, b_in, sk, w_out, b_out, tag):
    h = rmsnorm_bf16(x, nw, f"{tag}_norm")
    proj = mm_nn(h, w_in, f"{tag}_proj", bias=b_in)
    o = swa_core_fwd(proj, sk, f"{tag}_core")
    y = mm_nn(o, w_out, f"{tag}_out", bias=b_out, residual=x)
    return y, (x, h, proj, o)


def swa_backward(dout, saved, nw, w_in, b_in, sk, w_out, b_out, tag):
    x, h, proj, o = saved
    do = mm_nt(dout, w_out, f"{tag}_out_bwd")
    dw_out = mm_tn(o, dout, f"{tag}_out_wgrad")
    db_out = col_sum(dout, f"{tag}_out_bias_grad")
    dproj, db_in, dsk = swa_core_bwd(proj, sk, do, f"{tag}_core_bwd")
    dw_in = mm_tn(h, dproj, f"{tag}_proj_wgrad")
    dx, dnw = mm_nt(dproj, w_in, f"{tag}_proj_bwd", norm_bwd=(x, nw, dout))
    return dx, dnw, dw_in, db_in, dsk, dw_out, db_out


MESH = pl.DeviceIdType.MESH


def _position():
    return lax.axis_index("x"), lax.axis_index("y"), lax.axis_index("c")


def _slot(x, y, c):
    return 4 * x + 2 * y + c


def _peer(x, y, c, k):
    return (1 - x if k & 4 else x, 1 - y if k & 2 else y, 1 - c if k & 1 else c)


HBM_SPEC = pl.BlockSpec(memory_space=pltpu.HBM)
SEM_SPEC = pl.BlockSpec(memory_space=pltpu.SEMAPHORE)
DEP_SPEC = pl.BlockSpec(memory_space=pl.ANY)
SIDE_EFFECT = pltpu.SideEffectType.DATAFLOW_SIDE_EFFECTING
N_PEERS = N_DEV - 1


def _push_copies(srcs, lands, send_sems, recv_sems, scatter):
    x, y, c = _position()
    me = _slot(x, y, c)
    copies = []
    for k in (1, 2, 4, 3, 5, 6, 7):
        peer = _peer(x, y, c, k)
        for a in range(len(srcs)):
            copies.append(pltpu.make_async_remote_copy(
                src_ref=srcs[a].at[_slot(*peer)] if scatter else srcs[a], dst_ref=lands[a].at[me],
                send_sem=send_sems.at[N_PEERS * a + k - 1], recv_sem=recv_sems.at[N_PEERS * a + k - 1],
                device_id=peer, device_id_type=MESH))
    return copies


def push_start(srcs, lands, name, scatter, deps=()):
    n = len(srcs)
    first_out = 2 * n + len(deps)

    def body(*refs):
        for cp in _push_copies(refs[:n], refs[n:2 * n], refs[first_out], refs[first_out + 1], scatter):
            cp.start()
        refs[-1][...] = jnp.zeros_like(refs[-1])

    passed = [pltpu.HBM(t.shape, t.dtype) for t in list(srcs) + list(lands)]
    res = pl.pallas_call(
        body, name=name,
        out_shape=(pltpu.SemaphoreType.DMA((N_PEERS * n,)), pltpu.SemaphoreType.DMA((N_PEERS * n,)), *passed, _sds((8, 128), F32)),
        in_specs=[HBM_SPEC] * (2 * n) + [DEP_SPEC] * len(deps),
        out_specs=(SEM_SPEC, SEM_SPEC, *([HBM_SPEC] * (2 * n)), pl.BlockSpec(memory_space=pltpu.VMEM)),
        input_output_aliases={i: 2 + i for i in range(2 * n)},
        compiler_params=pltpu.CompilerParams(has_side_effects=SIDE_EFFECT),
    )(*[pltpu.with_memory_space_constraint(t, pltpu.HBM) for t in list(srcs) + list(lands)], *deps)
    return (res[0], res[1], list(res[2:2 + n]), list(res[2 + n:2 + 2 * n])), res[-1]


def push_wait(handles, after, name, scatter):
    send_sems, recv_sems, srcs, lands = handles
    n = len(srcs)
    after = tuple(after) if isinstance(after, (tuple, list)) else (after,)

    def body(*refs):
        for cp in _push_copies(refs[:n], refs[n:2 * n], refs[2 * n], refs[2 * n + 1], scatter):
            cp.wait_send()
            cp.wait_recv()

    res = pl.pallas_call(
        body, name=name, out_shape=tuple(pltpu.HBM(t.shape, t.dtype) for t in srcs + lands),
        in_specs=[HBM_SPEC] * (2 * n) + [SEM_SPEC, SEM_SPEC] + [DEP_SPEC] * len(after), out_specs=tuple([HBM_SPEC] * (2 * n)),
        input_output_aliases={i: i for i in range(2 * n)},
        compiler_params=pltpu.CompilerParams(has_side_effects=SIDE_EFFECT),
    )(*srcs, *lands, send_sems, recv_sems, *after)
    return list(res[n:])


def gather_start(shards, name, deps=()):
    me = _slot(*_position())
    lands = [lax.dynamic_update_slice(lax.empty((N_DEV,) + t.shape, t.dtype), t[None], (me,) + (0,) * t.ndim) for t in shards]
    return push_start(shards, lands, name, scatter=False, deps=deps)


def exchange_start(parts, name):
    me = _slot(*_position())
    lands = [lax.dynamic_update_slice(lax.empty(t.shape, t.dtype), lax.dynamic_index_in_dim(t, me, 0, keepdims=True),
                                      (me,) + (0,) * (t.ndim - 1)) for t in parts]
    return push_start(parts, lands, name, scatter=True)


def _row_tile(rows, cols):
    best = rows
    for t in range(16, rows, 16):
        if rows % t == 0 and t * cols * 4 <= (1 << 20):
            best = t
    return best


def adam_update(parts, w, m, v, name):
    n_layers = len(parts)
    P, R, C = parts[0].shape
    tr = _row_tile(R, C)
    n_t = R // tr

    def body(*refs):
        p_refs = refs[:n_layers]
        w_ref, m_ref, v_ref, g_ref, d_ref, nm_ref, nv_ref = refs[n_layers:]
        for layer in range(n_layers):
            @pl.when(pl.program_id(0) == layer)
            def _(p_ref=p_refs[layer]):
                g = p_ref[0].astype(F32)
                for s in range(1, P):
                    g = g + p_ref[s].astype(F32)
                new_m = ADAM_B1 * m_ref[0] + (1.0 - ADAM_B1) * g
                new_v = ADAM_B2 * v_ref[0] + (1.0 - ADAM_B2) * (g * g)
                m_hat = new_m / (1.0 - ADAM_B1 ** ADAM_STEP)
                v_hat = new_v / (1.0 - ADAM_B2 ** ADAM_STEP)
                g_ref[0] = g
                d_ref[0] = -ADAM_LR * (m_hat / (jnp.sqrt(v_hat) + ADAM_EPS) + ADAM_WD * w_ref[0])
                nm_ref[0] = new_m
                nv_ref[0] = new_v

    def part_spec(layer):
        return pl.BlockSpec((P, tr, C), lambda l_, i: (0, jnp.where(l_ == layer, i, jnp.where(l_ < layer, 0, n_t - 1)), 0))

    blk = pl.BlockSpec((1, tr, C), lambda l_, i: (l_, i, 0))
    out = _sds((n_layers, R, C), F32)
    return _pc(body, name=name, out_shape=(out, out, out, out), grid=(n_layers, n_t),
               in_specs=[part_spec(layer) for layer in range(n_layers)] + [blk, blk, blk],
               out_specs=(blk, blk, blk, blk), sem=("arbitrary", "arbitrary"))(*parts, w, m, v)


WEIGHTS = ("ffn1_norm", "ffn1_w_gu", "ffn1_w_down", "mix_norm", "ffn2_norm", "ffn2_w_gu", "ffn2_w_down", "a_w_in",
           "a_w_conv", "a_A_log", "a_dt_bias", "a_out_norm", "a_w_out", "b_w_in", "b_b_in", "b_sinks", "b_w_out",
           "b_b_out", "final_norm")
SHARDED = ("ffn1_w_gu", "ffn1_w_down", "ffn2_w_gu", "ffn2_w_down", "a_w_in", "a_w_conv", "a_w_out", "b_w_in", "b_b_in",
           "b_w_out", "b_b_out")
MISC_LANES = dict(a_A_log=(0, 8), a_dt_bias=(8, 16), b_sinks=(16, 32), a_out_norm=(128, 256))
LOSS_LANE = 256


def _pack_small(t):
    misc = jnp.zeros((D,), F32)
    for key, (lo, hi) in MISC_LANES.items():
        misc = misc.at[lo:hi].set(t[key].reshape(-1))
    if "loss" in t:
        misc = misc.at[LOSS_LANE].set(t["loss"])
    return jnp.concatenate([t["ffn1_norm"], t["mix_norm"], t["ffn2_norm"], t["final_norm"].reshape(1, D), misc[None]], axis=0)


def _unpack_small(p, like):
    out = dict(ffn1_norm=p[0:2], mix_norm=p[2:4], ffn2_norm=p[4:6], final_norm=p[6])
    for key, (lo, hi) in MISC_LANES.items():
        out[key] = p[7, lo:hi].reshape(like[key].shape)
    return out


def kernel(x, ffn1_norm, ffn1_w_gu, ffn1_w_down, mix_norm, ffn2_norm, ffn2_w_gu, ffn2_w_down, a_w_in, a_w_conv, a_A_log, a_dt_bias, a_out_norm, a_w_out, b_w_in, b_b_in, b_sinks, b_w_out, b_b_out, final_norm, loss_target, m_ffn1_norm, m_ffn1_w_gu, m_ffn1_w_down, m_mix_norm, m_ffn2_norm, m_ffn2_w_gu, m_ffn2_w_down, m_a_w_in, m_a_w_conv, m_a_A_log, m_a_dt_bias, m_a_out_norm, m_a_w_out, m_b_w_in, m_b_b_in, m_b_sinks, m_b_w_out, m_b_b_out, m_final_norm, v_ffn1_norm, v_ffn1_w_gu, v_ffn1_w_down, v_mix_norm, v_ffn2_norm, v_ffn2_w_gu, v_ffn2_w_down, v_a_w_in, v_a_w_conv, v_a_A_log, v_a_dt_bias, v_a_out_norm, v_a_w_out, v_b_w_in, v_b_b_in, v_b_sinks, v_b_w_out, v_b_b_out, v_final_norm):
    w = dict(ffn1_norm=ffn1_norm, ffn1_w_gu=ffn1_w_gu, ffn1_w_down=ffn1_w_down, mix_norm=mix_norm, ffn2_norm=ffn2_norm, ffn2_w_gu=ffn2_w_gu, ffn2_w_down=ffn2_w_down, a_w_in=a_w_in, a_w_conv=a_w_conv, a_A_log=a_A_log, a_dt_bias=a_dt_bias, a_out_norm=a_out_norm, a_w_out=a_w_out, b_w_in=b_w_in, b_b_in=b_b_in, b_sinks=b_sinks, b_w_out=b_w_out, b_b_out=b_b_out, final_norm=final_norm)
    m = dict(ffn1_norm=m_ffn1_norm, ffn1_w_gu=m_ffn1_w_gu, ffn1_w_down=m_ffn1_w_down, mix_norm=m_mix_norm, ffn2_norm=m_ffn2_norm, ffn2_w_gu=m_ffn2_w_gu, ffn2_w_down=m_ffn2_w_down, a_w_in=m_a_w_in, a_w_conv=m_a_w_conv, a_A_log=m_a_A_log, a_dt_bias=m_a_dt_bias, a_out_norm=m_a_out_norm, a_w_out=m_a_w_out, b_w_in=m_b_w_in, b_b_in=m_b_b_in, b_sinks=m_b_sinks, b_w_out=m_b_w_out, b_b_out=m_b_b_out, final_norm=m_final_norm)
    v = dict(ffn1_norm=v_ffn1_norm, ffn1_w_gu=v_ffn1_w_gu, ffn1_w_down=v_ffn1_w_down, mix_norm=v_mix_norm, ffn2_norm=v_ffn2_norm, ffn2_w_gu=v_ffn2_w_gu, ffn2_w_down=v_ffn2_w_down, a_w_in=v_a_w_in, a_w_conv=v_a_w_conv, a_A_log=v_a_A_log, a_dt_bias=v_a_dt_bias, a_out_norm=v_a_out_norm, a_w_out=v_a_w_out, b_w_in=v_b_w_in, b_b_in=v_b_b_in, b_sinks=v_b_sinks, b_w_out=v_b_w_out, b_b_out=v_b_b_out, final_norm=v_final_norm)
    T = x.shape[1]
    x0, tgt = x.reshape(T, D), loss_target.reshape(T, D)

    def cast(t):
        return t.astype(BF16)

    h0, t0 = gather_start([cast(ffn1_w_gu[0])], "gather0_start")
    a_log_row = jnp.zeros((1, 128), F32).at[0, HEADS_A:2 * HEADS_A].set(a_A_log[0])
    dt_row = jnp.zeros((1, 128), F32).at[0, HEADS_A:2 * HEADS_A].set(a_dt_bias[0])
    sink_row = jnp.zeros((1, 128), F32).at[0, :b_sinks.shape[1]].set(b_sinks[0])
    a_in_cols = a_w_in.shape[-1] * N_DEV

    def down_blocks(t):
        return t.reshape(N_FB, FB, D)

    wgu, wdn, saved = {}, {}, []
    xn = rmsnorm_bf16(x0, ffn1_norm[0:1], "l0_ffn1_norm", (t0,))
    wgu["ffn1", 0] = push_wait(h0, xn, "gather0_wait", scatter=False)[0]
    h0d, t0d = gather_start([cast(ffn1_w_down[0])], "gather0d_start", deps=(wgu["ffn1", 0],))
    h1, t1 = gather_start([cast(a_w_in[0]), a_w_conv[0], cast(a_w_out[0])], "gather1_start", deps=(t0d,))
    gu = ffn_up(xn, wgu["ffn1", 0], "l0_ffn1_up", deps=(t0d, t1))
    wdn["ffn1", 0] = down_blocks(push_wait(h0d, gu, "gather0d_wait", scatter=False)[0])
    xs, s1 = ffn_down(gu, wdn["ffn1", 0], x0, "l0_ffn1_down"), (x0, xn, gu)
    got = push_wait(h1, xs, "gather1_wait", scatter=False)
    h1f, t1f = gather_start([cast(ffn2_w_gu[0]), cast(ffn2_w_down[0])], "gather1f_start", deps=(got[0],))
    g2 = [cast(ffn1_w_gu[1]), cast(ffn1_w_down[1]), cast(b_w_in[0]), b_b_in, cast(b_w_out[0]), b_b_out,
          cast(ffn2_w_gu[1]), cast(ffn2_w_down[1])]
    h2, t2 = gather_start(g2, "gather2_start", deps=(t1f,))
    a_in_full = jnp.pad(got[0].transpose(1, 0, 2).reshape(D, a_in_cols), ((0, 0), (0, A_COLS - a_in_cols)))
    gdn_args = (mix_norm[0:1], a_in_full, got[1].transpose(1, 0, 2).reshape(4, 3 * D), a_log_row, dt_row, a_out_norm,
                got[2].reshape(D, D))
    xs, sm = gdn_forward(xs, *gdn_args, "gdn", deps=(t1f, t2))
    got = push_wait(h1f, xs, "gather1f_wait", scatter=False)
    wgu["ffn2", 0], wdn["ffn2", 0] = got[0], down_blocks(got[1])
    xs, s2 = ffn_forward(xs, ffn2_norm[0:1], wgu["ffn2", 0], wdn["ffn2", 0], "l0_ffn2")
    saved.append((s1, sm, s2))
    got = push_wait(h2, xs, "gather2_wait", scatter=False)
    wgu["ffn1", 1], wdn["ffn1", 1] = got[0], down_blocks(got[1])
    swa_args = (mix_norm[1:2], got[2].transpose(1, 0, 2).reshape(D, B_COLS), got[3].reshape(1, B_COLS), sink_row,
                got[4].reshape(D, D), got[5].reshape(1, D))
    wgu["ffn2", 1], wdn["ffn2", 1] = got[6], down_blocks(got[7])
    xs, s1 = ffn_forward(xs, ffn1_norm[1:2], wgu["ffn1", 1], wdn["ffn1", 1], "l1_ffn1")
    xs, sm = swa_forward(xs, *swa_args, "swa")
    xs, s2 = ffn_forward(xs, ffn2_norm[1:2], wgu["ffn2", 1], wdn["ffn2", 1], "l1_ffn2")
    saved.append((s1, sm, s2))
    loss_row, dx, d_final_norm = final_loss(xs, final_norm.reshape(1, D), tgt, "final_loss")

    def down_slots(t):
        return cast(t.reshape(N_DEV, FB // 2, D))

    def col_slots(t, dtype=BF16):
        return t.reshape(t.shape[0], N_DEV, -1).transpose(1, 0, 2).astype(dtype)

    d_norm = {"ffn1_norm": [None, None], "mix_norm": [None, None], "ffn2_norm": [None, None]}
    exchanges = {}

    def sender(tag):
        def on_grads(d_gu, d_dn):
            exchanges[tag], token = exchange_start([cast(d_gu), down_slots(d_dn)], f"exchange_{tag}_start")
            return (token,)
        return on_grads

    s1, sm, s2 = saved[1]
    dx, d_norm["ffn2_norm"][1], _, _ = ffn_backward(dx, s2, ffn2_norm[1:2], wgu["ffn2", 1], wdn["ffn2", 1], "l1_ffn2",
                                                    on_grads=sender("l1_ffn2"))
    dx, d_norm["mix_norm"][1], d_b_in, d_b_bias_in, d_sinks, d_b_out, d_b_bias_out = swa_backward(dx, sm, *swa_args, "swa")
    exchanges["swa"], t_swa = exchange_start(
        [col_slots(d_b_in), d_b_bias_in.reshape(N_DEV, 1, -1), cast(d_b_out.reshape(N_DEV, D // N_DEV, D)),
         d_b_bias_out.reshape(N_DEV, 1, -1)], "exchange_swa_start")
    dx, d_norm["ffn1_norm"][1], _, _ = ffn_backward(dx, s1, ffn1_norm[1:2], wgu["ffn1", 1], wdn["ffn1", 1], "l1_ffn1",
                                                    deps=(t_swa,), on_grads=sender("l1_ffn1"))

    s1, sm, s2 = saved[0]
    dx, d_norm["ffn2_norm"][0], _, _ = ffn_backward(dx, s2, ffn2_norm[0:1], wgu["ffn2", 0], wdn["ffn2", 0], "l0_ffn2",
                                                    on_grads=sender("l0_ffn2"))
    dx, d_norm["mix_norm"][0], d_a_in, d_a_conv, d_alog, d_dt, d_onorm, d_a_out = gdn_backward(dx, sm, *gdn_args, "gdn")
    exchanges["gdn"], t_gdn = exchange_start(
        [col_slots(d_a_in[:, :a_in_cols]), col_slots(d_a_conv, F32), cast(d_a_out.reshape(N_DEV, D // N_DEV, D))],
        "exchange_gdn_start")
    dx, d_norm["ffn1_norm"][0], _, _ = ffn_backward(dx, s1, ffn1_norm[0:1], wgu["ffn1", 0], wdn["ffn1", 0], "l0_ffn1",
                                                    deps=(t_gdn,), on_grads=sender("l0_ffn1"))
    grad_x = dx.reshape(x.shape)
    got = {tag: push_wait(exchanges[tag], dx, f"exchange_{tag}_wait", scatter=True)
           for tag in ("l1_ffn2", "swa", "l1_ffn1", "l0_ffn2", "gdn")}
    received = dict(ffn2_w_gu=[got["l0_ffn2"][0], got["l1_ffn2"][0]], ffn2_w_down=[got["l0_ffn2"][1], got["l1_ffn2"][1]],
                    b_w_in=[got["swa"][0]], b_b_in=[got["swa"][1]], b_w_out=[got["swa"][2]], b_b_out=[got["swa"][3]],
                    a_w_in=[got["gdn"][0]], a_w_conv=[got["gdn"][1]], a_w_out=[got["gdn"][2]])

    grads, deltas, new_m, new_v = {}, {}, {}, {}

    def update(key):
        shape = w[key].shape
        cols = shape[-1]
        layers = lambda t: t.reshape(shape[0], -1, cols)
        out = adam_update([r.reshape(N_DEV, -1, cols) for r in received[key]], layers(w[key]), layers(m[key]), layers(v[key]),
                          f"adam_{key}")
        grads[key], deltas[key], new_m[key], new_v[key] = (t.reshape(shape) for t in out)

    for key in SHARDED:
        if key in received:
            update(key)
    done_first = [deltas[key] for key in received]

    small = dict(ffn1_norm=jnp.concatenate(d_norm["ffn1_norm"], axis=0), mix_norm=jnp.concatenate(d_norm["mix_norm"], axis=0),
                 ffn2_norm=jnp.concatenate(d_norm["ffn2_norm"], axis=0), final_norm=d_final_norm,
                 a_A_log=d_alog[0, HEADS_A:2 * HEADS_A], a_dt_bias=d_dt[0, HEADS_A:2 * HEADS_A],
                 b_sinks=d_sinks[0, :b_sinks.shape[1]], a_out_norm=d_onorm, loss=loss_row[0, 0])
    hs, ts = gather_start([_pack_small(small)], "gather_small_start")
    r3 = push_wait(exchanges["l0_ffn1"], done_first + [ts], "exchange_l0_ffn1_wait", scatter=True)
    received.update(ffn1_w_gu=[r3[0], got["l1_ffn1"][0]], ffn1_w_down=[r3[1], got["l1_ffn1"][1]])
    update("ffn1_w_gu")
    update("ffn1_w_down")
    every = push_wait(hs, deltas["ffn1_w_down"], "gather_small_wait", scatter=False)[0]
    out = adam_update([every], _pack_small(w)[None], _pack_small(m)[None], _pack_small(v)[None], "adam_small")
    for dst, packed in zip((grads, deltas, new_m, new_v), out):
        dst.update(_unpack_small(packed[0], w))
    loss = out[0][0, 7, LOSS_LANE]

    return (loss, grad_x, *[grads[k_] for k_ in WEIGHTS], *[deltas[k_] for k_ in WEIGHTS],
            *[new_m[k_] for k_ in WEIGHTS], *[new_v[k_] for k_ in WEIGHTS])
```

```python
import functools

import jax
import jax.numpy as jnp
from jax import lax
from jax.experimental import pallas as pl
from jax.experimental.pallas import tpu as pltpu

F32, BF16 = jnp.float32, jnp.bfloat16
HI = lax.Precision.HIGHEST
EPS = 1e-6

N_DEV = 8
D = 1024
FB = 704
N_FB = 4
HEADS_A, DK = 8, 128
CHUNK = 64
PREP_T = 512
A_COLS = 4224
B_HD, B_BLK = 64, 128
VMEM_LIMIT_V7X = 60 * 1024 * 1024

ADAM_LR, ADAM_B1, ADAM_B2, ADAM_EPS, ADAM_WD, ADAM_STEP = 0.001, 0.9, 0.999, 1e-08, 0.01, 10

NT = (((1,), (1,)), ((), ()))
TN = (((0,), (0,)), ((), ()))


def _pc(body, *, name, out_shape, grid=(), in_specs=None, out_specs=None, scratch=(), sem=None, **kw):
    params = pltpu.CompilerParams(dimension_semantics=sem, vmem_limit_bytes=VMEM_LIMIT_V7X)
    return pl.pallas_call(body, name=name, out_shape=out_shape, grid=grid, in_specs=in_specs, out_specs=out_specs,
                          scratch_shapes=list(scratch), compiler_params=params, **kw)


def _sds(shape, dtype):
    return jax.ShapeDtypeStruct(tuple(shape), dtype)


def _dot(a, b, dims=None, precision=None):
    if dims is None:
        return jnp.dot(a, b, preferred_element_type=F32, precision=precision)
    return lax.dot_general(a, b, dims, preferred_element_type=F32, precision=precision)


def _sigmoid(x):
    return 1.0 / (1.0 + jnp.exp(-x))


def _softplus(x):
    return jnp.maximum(x, 0.0) + jnp.log(1.0 + jnp.exp(-jnp.abs(x)))


def _rms_fwd(x, w):
    r = lax.rsqrt(jnp.mean(x * x, axis=-1, keepdims=True) + EPS)
    return x * r * w


def _rms_bwd(x, w, dy):
    r = lax.rsqrt(jnp.mean(x * x, axis=-1, keepdims=True) + EPS)
    xh = x * r
    dxh = dy * w
    dx = r * (dxh - xh * jnp.mean(dxh * xh, axis=-1, keepdims=True))
    return dx, jnp.sum(dy * xh, axis=0, keepdims=True)


def _tile(n, want):
    t = min(n, want)
    assert n % t == 0, (n, want)
    return t


def rmsnorm_bf16(x, w, name, deps=()):
    T = x.shape[0]
    tm = _tile(T, 1024)

    def body(x_ref, w_ref, *rest):
        rest[-1][...] = _rms_fwd(x_ref[...], w_ref[...]).astype(BF16)

    return _pc(body, name=name, out_shape=_sds((T, D), BF16), grid=(T // tm,),
               in_specs=[pl.BlockSpec((tm, D), lambda i: (i, 0)), pl.BlockSpec((1, D), lambda i: (0, 0))] + [DEP_SPEC] * len(deps),
               out_specs=pl.BlockSpec((tm, D), lambda i: (i, 0)), sem=("parallel",))(x, w, *deps)


def final_loss(x, w, tgt, name):
    T = x.shape[0]
    tm = _tile(T, 512)

    def body(x_ref, w_ref, t_ref, loss_ref, dx_ref, dw_ref):
        xv, wv = x_ref[...], w_ref[...]
        err = _rms_fwd(xv, wv) - t_ref[...]
        dx, dw = _rms_bwd(xv, wv, err * (1.0 / D))
        dx_ref[...] = dx

        @pl.when(pl.program_id(0) == 0)
        def _():
            dw_ref[...] = jnp.zeros_like(dw_ref)
            loss_ref[...] = jnp.zeros_like(loss_ref)
        dw_ref[...] += dw
        loss_ref[...] += jnp.full((1, 128), 0.5 / D, F32) * jnp.sum(err * err)

    row = pl.BlockSpec((tm, D), lambda i: (i, 0))
    vec = pl.BlockSpec((1, D), lambda i: (0, 0))
    return _pc(body, name=name, out_shape=(_sds((1, 128), F32), _sds((T, D), F32), _sds((1, D), F32)),
               grid=(T // tm,), in_specs=[row, vec, row],
               out_specs=(pl.BlockSpec((1, 128), lambda i: (0, 0)), row, vec), sem=("arbitrary",))(x, w, tgt)


def _col_tile(n):
    for t in (1536, 1408, 1024, 768, 512, 384, 256, 128):
        if n % t == 0:
            return t
    return n


def mm_nn(a, b, name, bias=None, residual=None, out_dtype=F32, cols=None):
    T, K = a.shape
    first, end = cols or (0, b.shape[1])
    N = end - first
    tm, tn = _tile(T, 512), _col_tile(N)
    assert first % tn == 0 and (cols is None or (bias is None and residual is None))
    j0 = first // tn

    def body(a_ref, b_ref, *rest):
        o_ref = rest[-1]
        acc = _dot(a_ref[...].astype(BF16), b_ref[...])
        for extra in rest[:-1]:
            acc = acc + extra[...]
        o_ref[...] = acc.astype(out_dtype)

    in_specs = [pl.BlockSpec((tm, K), lambda j, i: (i, 0)), pl.BlockSpec((K, tn), lambda j, i: (0, j0 + j))]
    args = [a, b]
    if bias is not None:
        in_specs.append(pl.BlockSpec((1, tn), lambda j, i: (0, j)))
        args.append(bias)
    if residual is not None:
        in_specs.append(pl.BlockSpec((tm, tn), lambda j, i: (i, j)))
        args.append(residual)
    return _pc(body, name=name, out_shape=_sds((T, N), out_dtype), grid=(N // tn, T // tm), in_specs=in_specs,
               out_specs=pl.BlockSpec((tm, tn), lambda j, i: (i, j)), sem=("parallel", "parallel"))(*args)


def mm_nt(a, b, name, out_dtype=F32, norm_bwd=None):
    T, N = a.shape
    K = b.shape[0]
    tm = _tile(T, 512)
    row = pl.BlockSpec((tm, K), lambda i: (i, 0))
    in_specs = [pl.BlockSpec((tm, N), lambda i: (i, 0)), _resident((K, N))]

    if norm_bwd is None:
        def body(a_ref, b_ref, o_ref):
            o_ref[...] = _dot(a_ref[...].astype(BF16), b_ref[...], NT).astype(out_dtype)

        return _pc(body, name=name, out_shape=_sds((T, K), out_dtype), grid=(T // tm,), in_specs=in_specs,
                   out_specs=row, sem=("parallel",))(a, b)

    def body(a_ref, b_ref, x_ref, w_ref, dres_ref, dx_ref, dw_ref):
        dx, dw = _rms_bwd(x_ref[...], w_ref[...], _dot(a_ref[...].astype(BF16), b_ref[...], NT))
        dx_ref[...] = dres_ref[...] + dx

        @pl.when(pl.program_id(0) == 0)
        def _():
            dw_ref[...] = jnp.zeros_like(dw_ref)
        dw_ref[...] += dw

    vec = pl.BlockSpec((1, K), lambda i: (0, 0))
    return _pc(body, name=name, out_shape=(_sds((T, K), F32), _sds((1, K), F32)), grid=(T // tm,),
               in_specs=in_specs + [row, vec, row], out_specs=(row, vec), sem=("arbitrary",))(a, b, *norm_bwd)


def mm_tn(a, b, name):
    T, K = a.shape
    N = b.shape[1]
    tt, tn = _tile(T, 1024), _col_tile(N)

    def body(a_ref, b_ref, o_ref):
        @pl.when(pl.program_id(1) == 0)
        def _():
            o_ref[...] = jnp.zeros_like(o_ref)
        o_ref[...] += _dot(a_ref[...].astype(BF16), b_ref[...].astype(BF16), TN)

    return _pc(body, name=name, out_shape=_sds((K, N), F32), grid=(N // tn, T // tt),
               in_specs=[pl.BlockSpec((tt, K), lambda j, t: (t, 0)), pl.BlockSpec((tt, tn), lambda j, t: (t, j))],
               out_specs=pl.BlockSpec((K, tn), lambda j, t: (0, j)), sem=("parallel", "arbitrary"))(a, b)


def ffn_up(xn, wgu, name, deps=()):
    T = xn.shape[0]
    tm = _tile(T, 1024)

    def body(x_ref, w_ref, *rest):
        xv = x_ref[...]
        for j in range(2 * N_FB):
            rest[-1][j] = _dot(xv, w_ref[j]).astype(BF16)

    return _pc(body, name=name, out_shape=_sds((2 * N_FB, T, FB), BF16), grid=(T // tm,),
               in_specs=[pl.BlockSpec((tm, D), lambda i: (i, 0)), _resident((2 * N_FB, D, FB))] + [DEP_SPEC] * len(deps),
               out_specs=pl.BlockSpec((2 * N_FB, tm, FB), lambda i: (0, i, 0)), sem=("parallel",))(xn, wgu, *deps)


def ffn_down(gu, wd, x, name):
    T = x.shape[0]
    tm = _tile(T, 512)

    def body(gu_ref, w_ref, x_ref, o_ref):
        acc = jnp.zeros((tm, D), F32)
        for g in range(N_FB):
            gate, up = gu_ref[g], gu_ref[N_FB + g]
            acc = acc + _dot(gate * _sigmoid(gate) * up, w_ref[g])
        o_ref[...] = x_ref[...] + 0.5 * acc

    row = pl.BlockSpec((tm, D), lambda i: (i, 0))
    return _pc(body, name=name, out_shape=_sds((T, D), F32), grid=(T // tm,),
               in_specs=[pl.BlockSpec((2 * N_FB, tm, FB), lambda i: (0, i, 0)),
                         _resident((N_FB, FB, D)), row],
               out_specs=row, sem=("parallel",))(gu, wd, x)


def _resident(shape):
    return pl.BlockSpec(shape, lambda *_: (0,) * len(shape), pipeline_mode=pl.Buffered(1))


def _store_blocks_bf16(acc, out_hbm, stage, sem):
    for j in range(acc.shape[0]):
        stage[...] = acc[j].astype(BF16)
        copy = pltpu.make_async_copy(stage, out_hbm.at[j], sem)
        copy.start()
        copy.wait()


def ffn_bwd_hidden(dout, wd, gu, name, deps=()):
    T = dout.shape[0]
    tm = _tile(T, 512)
    n_t = T // tm

    def body(d_ref, w_ref, gu_ref, *rest):
        dgu_ref, dwd_hbm, acc, stage, sem = rest[-5:]
        t = pl.program_id(0)

        @pl.when(t == 0)
        def _():
            acc[...] = jnp.zeros_like(acc)
        dy = (0.5 * d_ref[...]).astype(BF16)
        for g in range(N_FB):
            gate, up = gu_ref[g], gu_ref[N_FB + g]
            sg = _sigmoid(gate)
            silu = gate * sg
            dact = _dot(dy, w_ref[g], NT).astype(BF16)
            acc[g] += _dot(silu * up, dy, TN)
            dgu_ref[g] = dact * up * (sg * (1.0 + gate * (1.0 - sg)))
            dgu_ref[N_FB + g] = dact * silu

        @pl.when(t == n_t - 1)
        def _():
            _store_blocks_bf16(acc, dwd_hbm, stage, sem)

    return _pc(body, name=name, out_shape=(_sds((2 * N_FB, T, FB), BF16), _sds((N_FB, FB, D), BF16)), grid=(n_t,),
               in_specs=[pl.BlockSpec((tm, D), lambda i: (i, 0)), _resident((N_FB, FB, D)),
                         pl.BlockSpec((2 * N_FB, tm, FB), lambda i: (0, i, 0))] + [DEP_SPEC] * len(deps),
               out_specs=(pl.BlockSpec((2 * N_FB, tm, FB), lambda i: (0, i, 0)), pl.BlockSpec(memory_space=pl.ANY)),
               scratch=[pltpu.VMEM((N_FB, FB, D), F32), pltpu.VMEM((FB, D), BF16), pltpu.SemaphoreType.DMA],
               sem=("arbitrary",))(dout, wd, gu, *deps)


def ffn_bwd_input(dgu, wgu, x, dout, nw, name, deps=()):
    T = x.shape[0]
    tm = _tile(T, 512)

    def body(dgu_ref, w_ref, x_ref, d_ref, nw_ref, *rest):
        dx_ref, dnw_ref = rest[-2:]
        dxn = jnp.zeros((tm, D), F32)
        for j in range(2 * N_FB):
            dxn = dxn + _dot(dgu_ref[j], w_ref[j], NT)
        dx, dw = _rms_bwd(x_ref[...], nw_ref[...], dxn)
        dx_ref[...] = d_ref[...] + dx

        @pl.when(pl.program_id(0) == 0)
        def _():
            dnw_ref[...] = jnp.zeros_like(dnw_ref)
        dnw_ref[...] += dw

    row = pl.BlockSpec((tm, D), lambda i: (i, 0))
    vec = pl.BlockSpec((1, D), lambda i: (0, 0))
    return _pc(body, name=name, out_shape=(_sds((T, D), F32), _sds((1, D), F32)), grid=(T // tm,),
               in_specs=[pl.BlockSpec((2 * N_FB, tm, FB), lambda i: (0, i, 0)), _resident((2 * N_FB, D, FB)),
                         row, row, vec] + [DEP_SPEC] * len(deps),
               out_specs=(row, vec), sem=("arbitrary",))(dgu, wgu, x, dout, nw, *deps)


def ffn_wgrad_gu(xn, dgu, name):
    T = xn.shape[0]
    tt = _tile(T, 1024)
    n_t = T // tt

    def body(x_ref, d_ref, dw_hbm, acc, stage, sem):
        t = pl.program_id(0)

        @pl.when(t == 0)
        def _():
            acc[...] = jnp.zeros_like(acc)
        xn_tile = x_ref[...]
        for j in range(2 * N_FB):
            acc[j] += _dot(xn_tile, d_ref[j], TN)

        @pl.when(t == n_t - 1)
        def _():
            _store_blocks_bf16(acc, dw_hbm, stage, sem)

    return _pc(body, name=name, out_shape=_sds((2 * N_FB, D, FB), BF16), grid=(n_t,),
               in_specs=[pl.BlockSpec((tt, D), lambda t: (t, 0)), pl.BlockSpec((2 * N_FB, tt, FB), lambda t: (0, t, 0))],
               out_specs=pl.BlockSpec(memory_space=pl.ANY),
               scratch=[pltpu.VMEM((2 * N_FB, D, FB), F32), pltpu.VMEM((D, FB), BF16), pltpu.SemaphoreType.DMA],
               sem=("arbitrary",))(xn, dgu)


def ffn_forward(x, nw, wgu, wd, tag):
    T = x.shape[0]
    tm = _tile(T, 512)

    def body(x_ref, nw_ref, wgu_ref, wd_ref, o_ref, xn_ref, gu_ref):
        xv = x_ref[...]
        xn = _rms_fwd(xv, nw_ref[...]).astype(BF16)
        xn_ref[...] = xn
        for j in range(2 * N_FB):
            gu_ref[j] = _dot(xn, wgu_ref[j]).astype(BF16)
        acc = jnp.zeros((tm, D), F32)
        for g in range(N_FB):
            gate, up = gu_ref[g], gu_ref[N_FB + g]
            acc = acc + _dot(gate * _sigmoid(gate) * up, wd_ref[g])
        o_ref[...] = xv + 0.5 * acc

    row = pl.BlockSpec((tm, D), lambda i: (i, 0))
    out, xn, gu = _pc(body, name=f"{tag}_fwd",
                      out_shape=(_sds((T, D), F32), _sds((T, D), BF16), _sds((2 * N_FB, T, FB), BF16)), grid=(T // tm,),
                      in_specs=[row, pl.BlockSpec((1, D), lambda i: (0, 0)), _resident((2 * N_FB, D, FB)),
                                _resident((N_FB, FB, D))],
                      out_specs=(row, row, pl.BlockSpec((2 * N_FB, tm, FB), lambda i: (0, i, 0))),
                      sem=("parallel",))(x, nw, wgu, wd)
    return out, (x, xn, gu)


def ffn_backward(dout, saved, nw, wgu, wd, tag, deps=(), on_grads=None):
    x, xn, gu = saved
    dgu, dwd = ffn_bwd_hidden(dout, wd, gu, f"{tag}_bwd_hidden", deps)
    dwgu = ffn_wgrad_gu(xn, dgu, f"{tag}_wgrad_gu")
    late = on_grads(dwgu, dwd) if on_grads else ()
    dx, dnw = ffn_bwd_input(dgu, wgu, x, dout, nw, f"{tag}_bwd_input", late)
    return dx, dnw, dwgu, dwd


N_QKV_BLK = 3 * HEADS_A
Z_BLK0 = N_QKV_BLK
MAIN_COLS = 4 * D
HALO = 16


def _conv_taps(xcat, w):
    c = xcat[HALO:] * w[3:4]
    for k in range(3):
        c = c + pltpu.roll(xcat, 3 - k, 0)[HALO:] * w[k:k + 1]
    return c


def _head_cols(h):
    return slice(128 * h, 128 * (h + 1))


def gdn_conv_fwd(proj, wconv, name):
    T = proj.shape[0]
    tm = _tile(T, 512)

    def body(cur_ref, prev_ref, w_ref, c_ref, y_ref):
        kind, t = pl.program_id(0), pl.program_id(1)
        prev = jnp.where(t > 0, prev_ref[...].astype(F32), 0.0)
        c = _conv_taps(jnp.concatenate([prev, cur_ref[...].astype(F32)], axis=0), w_ref[...])
        c_ref[...] = c.astype(BF16)
        s = c * _sigmoid(c)
        scale = jnp.where(kind == 0, DK ** -0.5, 1.0)
        for h in range(HEADS_A):
            sh = s[:, _head_cols(h)]
            r = lax.rsqrt(jnp.sum(sh * sh, axis=-1, keepdims=True) + EPS)
            y_ref[h] = (sh * jnp.where(kind < 2, r * scale, 1.0)).astype(BF16)

    return _pc(body, name=name, out_shape=(_sds((T, 3 * D), BF16), _sds((N_QKV_BLK, T, 128), BF16)),
               grid=(3, T // tm),
               in_specs=[pl.BlockSpec((tm, D), lambda kd, t: (t, kd)),
                         pl.BlockSpec((HALO, D), lambda kd, t: (jnp.maximum(t * (tm // HALO) - 1, 0), kd)),
                         pl.BlockSpec((4, D), lambda kd, t: (0, kd))],
               out_specs=(pl.BlockSpec((tm, D), lambda kd, t: (t, kd)),
                          pl.BlockSpec((HEADS_A, tm, 128), lambda kd, t: (kd, t, 0))),
               sem=("parallel", "parallel"))(proj, proj, wconv)


def gdn_conv_bwd(dqkv, c, proj, wconv, dproj, name):
    T = c.shape[0]
    tm = _tile(T, 512)
    n_t = T // tm

    def body(dy_ref, dyn_ref, c_ref, cn_ref, x_ref, w_ref, _, dx_ref, dw_ref):
        kind, t = pl.program_id(0), pl.program_id(1)
        scale = jnp.where(kind == 0, DK ** -0.5, 1.0)

        def act_bwd(dy, cv):
            sg = _sigmoid(cv)
            s = cv * sg
            parts = []
            for h in range(HEADS_A):
                sh, dyh = s[:, _head_cols(h)], dy[h]
                r = lax.rsqrt(jnp.sum(sh * sh, axis=-1, keepdims=True) + EPS)
                ds_norm = scale * r * (dyh - (r * r) * sh * jnp.sum(dyh * sh, axis=-1, keepdims=True))
                parts.append(jnp.where(kind < 2, ds_norm, dyh))
            return jnp.concatenate(parts, axis=1) * (sg * (1.0 + cv * (1.0 - sg)))

        w = w_ref[...]
        dcur = act_bwd(dy_ref[...].astype(F32), c_ref[...].astype(F32))
        dnext = jnp.where(t < n_t - 1, act_bwd(dyn_ref[...].astype(F32), cn_ref[...].astype(F32)), 0.0)
        dcat = jnp.concatenate([dcur, dnext], axis=0)
        xcur = x_ref[...].astype(F32)
        dx = dcur * w[3:4]
        rows = [None, None, None, jnp.sum(dcur * xcur, axis=0, keepdims=True)]
        for k in range(3):
            ahead = pltpu.roll(dcat, tm + HALO - (3 - k), 0)[:tm]
            dx = dx + ahead * w[k:k + 1]
            rows[k] = jnp.sum(ahead * xcur, axis=0, keepdims=True)
        dx_ref[...] = dx.astype(BF16)

        @pl.when(t == 0)
        def _():
            dw_ref[...] = jnp.zeros_like(dw_ref)
        dw_ref[...] += jnp.concatenate(rows, axis=0)

    def nxt(t):
        return jnp.minimum((t + 1) * (tm // HALO), T // HALO - 1)

    cur = pl.BlockSpec((tm, D), lambda kd, t: (t, kd))
    return _pc(body, name=name, out_shape=(_sds(dproj.shape, BF16), _sds((4, 3 * D), F32)), grid=(3, n_t),
               in_specs=[pl.BlockSpec((HEADS_A, tm, 128), lambda kd, t: (kd, t, 0)),
                         pl.BlockSpec((HEADS_A, HALO, 128), lambda kd, t: (kd, nxt(t), 0)),
                         cur, pl.BlockSpec((HALO, D), lambda kd, t: (nxt(t), kd)),
                         cur, pl.BlockSpec((4, D), lambda kd, t: (0, kd)), DEP_SPEC],
               out_specs=(cur, pl.BlockSpec((4, D), lambda kd, t: (0, kd))), input_output_aliases={6: 0},
               sem=("parallel", "arbitrary"))(dqkv, dqkv, c, c, proj, wconv, dproj)


def _chunk_masks(n):
    ri = lax.broadcasted_iota(jnp.int32, (n, n), 0)
    ci = lax.broadcasted_iota(jnp.int32, (n, n), 1)
    same = (ri // CHUNK) == (ci // CHUNK)
    return same & (ri >= ci), same & (ri <= ci)


def gdn_gate_fwd(ba, al, dtb, name):
    T = ba.shape[0]
    tg = _tile(T, PREP_T)

    def body(ba_ref, al_ref, dtb_ref, o_ref):
        x = ba_ref[...]
        lane = lax.broadcasted_iota(jnp.int32, x.shape, 1)
        is_a = (lane >= HEADS_A) & (lane < 2 * HEADS_A)
        g = jnp.where(is_a, -jnp.exp(al_ref[...]) * _softplus(x + dtb_ref[...]), 0.0)
        lower, _ = _chunk_masks(tg)
        gc = _dot(lower.astype(F32), g, precision=HI)
        o_ref[...] = jnp.where(lane < HEADS_A, _sigmoid(x), gc)

    vec = pl.BlockSpec((1, 128), lambda i: (0, 0))
    return _pc(body, name=name, out_shape=_sds((T, 128), F32), grid=(T // tg,),
               in_specs=[pl.BlockSpec((tg, 128), lambda i: (i, 0)), vec, vec],
               out_specs=pl.BlockSpec((tg, 128), lambda i: (i, 0)), sem=("parallel",))(ba, al, dtb)


def gdn_gate_bwd(ba, al, dtb, dgb, dproj, name):
    T = ba.shape[0]
    tg = _tile(T, PREP_T)

    def body(ba_ref, al_ref, dtb_ref, dgb_ref, _, dba_ref, dal_ref, ddt_ref):
        x, d = ba_ref[...], dgb_ref[...]
        lane = lax.broadcasted_iota(jnp.int32, x.shape, 1)
        is_b = lane < HEADS_A
        is_a = (lane >= HEADS_A) & (lane < 2 * HEADS_A)
        beta = _sigmoid(x)
        e_a = jnp.exp(al_ref[...])
        z = x + dtb_ref[...]
        g = jnp.where(is_a, -e_a * _softplus(z), 0.0)
        _, upper = _chunk_masks(tg)
        dg = _dot(upper.astype(F32), jnp.where(is_a, d, 0.0), precision=HI)
        da = jnp.where(is_a, dg * (-e_a) * _sigmoid(z), 0.0)
        db = jnp.where(is_b, d * beta * (1.0 - beta), 0.0)
        dba_ref[...] = (da + db).astype(BF16)

        @pl.when(pl.program_id(0) == 0)
        def _():
            dal_ref[...] = jnp.zeros_like(dal_ref)
            ddt_ref[...] = jnp.zeros_like(ddt_ref)
        dal_ref[...] += jnp.sum(dg * g, axis=0, keepdims=True)
        ddt_ref[...] += jnp.sum(da, axis=0, keepdims=True)

    vec = pl.BlockSpec((1, 128), lambda i: (0, 0))
    blk = pl.BlockSpec((tg, 128), lambda i: (i, 0))
    ba_cols = pl.BlockSpec((tg, 128), lambda i: (i, A_COLS // 128 - 1))
    return _pc(body, name=name, out_shape=(_sds(dproj.shape, BF16), _sds((1, 128), F32), _sds((1, 128), F32)),
               grid=(T // tg,), in_specs=[blk, vec, vec, blk, DEP_SPEC],
               out_specs=(ba_cols, vec, vec), input_output_aliases={4: 0}, sem=("arbitrary",))(ba, al, dtb, dgb, dproj)


def _bmm(a, b, dims, precision=None):
    return lax.dot_general(a, b, dims, preferred_element_type=F32, precision=precision)


B_NN = (((2,), (1,)), ((0,), (0,)))
B_NT = (((2,), (2,)), ((0,), (0,)))


def _select_lane(x, lane_index):
    lane = lax.broadcasted_iota(jnp.int32, x.shape, x.ndim - 1)
    return jnp.sum(jnp.where(lane == lane_index, x, 0.0), axis=-1, keepdims=True)


B_TN = (((1,), (1,)), ((0,), (0,)))


def _bmm_split(a, b, dims):
    ah, bh = a.astype(BF16), b.astype(BF16)
    al, bl = (a - ah.astype(F32)).astype(BF16), (b - bh.astype(F32)).astype(BF16)
    return _bmm(ah, bh, dims) + (_bmm(ah, bl, dims) + _bmm(al, bh, dims))


@jax.custom_vjp
def _bmm_f32(a, b):
    return _bmm_split(a, b, B_NN)


def _bmm_f32_fwd(a, b):
    return _bmm_split(a, b, B_NN), (a, b)


def _bmm_bf16(a, b, dims):
    return _bmm(a.astype(BF16), b.astype(BF16), dims)


def _bmm_f32_bwd(res, dc):
    a, b = res
    return _bmm_bf16(dc, b, B_NT), _bmm_bf16(a, dc, B_TN)


_bmm_f32.defvjp(_bmm_f32_fwd, _bmm_f32_bwd)


def _tri_inverse(lmat):
    ri = lax.broadcasted_iota(jnp.int32, lmat.shape, 1)
    ci = lax.broadcasted_iota(jnp.int32, lmat.shape, 2)
    eye = jnp.where(ri == ci, 1.0, 0.0)
    inv = eye - lmat
    power = lmat
    for _ in range(5):
        power = _bmm_bf16(power, power, B_NN)
        inv = inv + _bmm_bf16(inv, power, B_NN)
    return _bmm_split(inv, 2.0 * eye - _bmm_split(eye + lmat, inv, B_NN), B_NN)


def _stored_inverse(x):
    @jax.custom_vjp
    def inverse(lmat):
        return x

    def fwd(lmat):
        return x, None

    def bwd(_, dx):
        return (-_bmm_bf16(_bmm_bf16(x, dx, B_TN), x, B_NT),)

    inverse.defvjp(fwd, bwd)
    return inverse


def _gdn_prep(q, k, v, gb, h, inverse):
    nb = q.shape[0]
    beta = _select_lane(gb, h)
    gc = _select_lane(gb, HEADS_A + h)
    ri = lax.broadcasted_iota(jnp.int32, (nb, CHUNK, CHUNK), 1)
    ci = lax.broadcasted_iota(jnp.int32, (nb, CHUNK, CHUNK), 2)
    lower, strict, eye = ri >= ci, ri > ci, ri == ci
    gcol = jnp.broadcast_to(gc, (nb, CHUNK, CHUNK))
    grow = jnp.swapaxes(gcol, 1, 2)
    decay = jnp.where(lower, jnp.exp(jnp.where(lower, gcol - grow, 0.0)), 0.0)
    kb = k * beta
    kbf = k.astype(BF16)
    inv = inverse(jnp.where(strict, _bmm(kb.astype(BF16), kbf, B_NT) * decay, 0.0))
    eg = jnp.exp(gc)
    sol = _bmm_f32(inv, jnp.concatenate([v * beta, kb * eg], axis=-1))
    aqk = _bmm(q.astype(BF16), kbf, B_NT) * decay
    g_last = gc[:, CHUNK - 1:CHUNK, :]
    gl = jnp.broadcast_to(jnp.exp(g_last), (nb, 1, 128))
    return (sol[..., :DK], sol[..., DK:], q * eg, k * jnp.exp(g_last - gc), aqk, gl), inv


def gdn_prep_fwd(qkv, gb, name):
    T = qkv.shape[1]
    tp = _tile(T, 4 * PREP_T)
    nb = tp // CHUNK

    def body(q_ref, k_ref, v_ref, gb_ref, u_ref, w_ref, qd_ref, kd_ref, a_ref, gl_ref, inv_ref):
        h = pl.program_id(1)
        shp = (nb, CHUNK, 128)
        q, k, v = (ref[0].astype(F32).reshape(shp) for ref in (q_ref, k_ref, v_ref))
        (u, w, qd, kd, aqk, gl), inv = _gdn_prep(q, k, v, gb_ref[...].reshape(shp), h, _tri_inverse)
        u_ref[0] = u.reshape(tp, 128)
        w_ref[0] = w.reshape(tp, 128).astype(BF16)
        qd_ref[0] = qd.reshape(tp, 128).astype(BF16)
        kd_ref[0] = kd.reshape(tp, 128).astype(BF16)
        a_ref[0] = aqk.reshape(tp, CHUNK).astype(BF16)
        gl_ref[0] = gl.reshape(nb, 1, 128)
        inv_ref[0] = inv.reshape(tp, CHUNK)

    def head(off):
        return pl.BlockSpec((1, tp, 128), lambda n, h: (h + off, n, 0))

    matmul_only = _sds((HEADS_A, T, 128), BF16)
    narrow = pl.BlockSpec((1, tp, CHUNK), lambda n, h: (h, n, 0))
    return _pc(body, name=name,
               out_shape=(_sds((HEADS_A, T, 128), F32), matmul_only, matmul_only, matmul_only, _sds((HEADS_A, T, CHUNK), BF16),
                          _sds((HEADS_A, T // CHUNK, 1, 128), F32), _sds((HEADS_A, T, CHUNK), F32)),
               grid=(T // tp, HEADS_A),
               in_specs=[head(0), head(HEADS_A), head(2 * HEADS_A), pl.BlockSpec((tp, 128), lambda n, h: (n, 0))],
               out_specs=(head(0), head(0), head(0), head(0), narrow,
                          pl.BlockSpec((1, nb, 1, 128), lambda n, h: (h, n, 0, 0)), narrow),
               sem=("parallel", "parallel"))(qkv, qkv, qkv, gb)


def gdn_prep_bwd(qkv, gb, inv, du, dw, dqd, dkd, da, dgl, name):
    T = qkv.shape[1]
    tp = _tile(T, 2 * PREP_T)
    nb = tp // CHUNK

    def body(q_ref, k_ref, v_ref, gb_ref, inv_ref, du_ref, dw_ref, dqd_ref, dkd_ref, da_ref, dgl_ref, dqkv_ref, dgb_ref):
        h = pl.program_id(1)
        shp = (nb, CHUNK, 128)
        stored = _stored_inverse(inv_ref[0].reshape(nb, CHUNK, CHUNK))
        q, k, v = (ref[0].astype(F32).reshape(shp) for ref in (q_ref, k_ref, v_ref))
        _, vjp = jax.vjp(lambda q, k, v, gb: _gdn_prep(q, k, v, gb, h, stored)[0], q, k, v, gb_ref[...].reshape(shp))
        du, dw, dqd, dkd = (ref[0].astype(F32).reshape(shp) for ref in (du_ref, dw_ref, dqd_ref, dkd_ref))
        dq, dk, dv, dgb = vjp((du, dw, dqd, dkd, da_ref[0].astype(F32).reshape(nb, CHUNK, CHUNK),
                               dgl_ref[0].reshape(nb, 1, 128)))
        dqkv_ref[h] = dq.reshape(tp, 128).astype(BF16)
        dqkv_ref[HEADS_A + h] = dk.reshape(tp, 128).astype(BF16)
        dqkv_ref[2 * HEADS_A + h] = dv.reshape(tp, 128).astype(BF16)

        @pl.when(h == 0)
        def _():
            dgb_ref[...] = jnp.zeros_like(dgb_ref)
        dgb_ref[...] += dgb.reshape(tp, 128)

    def head(off):
        return pl.BlockSpec((1, tp, 128), lambda n, h: (h + off, n, 0))

    narrow = pl.BlockSpec((1, tp, CHUNK), lambda n, h: (h, n, 0))
    return _pc(body, name=name, out_shape=(_sds((N_QKV_BLK, T, 128), BF16), _sds((T, 128), F32)),
               grid=(T // tp, HEADS_A),
               in_specs=[head(0), head(HEADS_A), head(2 * HEADS_A), pl.BlockSpec((tp, 128), lambda n, h: (n, 0)), narrow,
                         head(0), head(0), head(0), head(0), narrow,
                         pl.BlockSpec((1, nb, 1, 128), lambda n, h: (h, n, 0, 0))],
               out_specs=(pl.BlockSpec((N_QKV_BLK, tp, 128), lambda n, h: (0, n, 0)),
                          pl.BlockSpec((tp, 128), lambda n, h: (n, 0))),
               sem=("parallel", "arbitrary"))(qkv, qkv, qkv, gb, inv, du, dw, dqd, dkd, da, dgl)


def gdn_scan_fwd(u, w, qd, kd, aqk, gl, name):
    T = u.shape[1]
    n_chunks = T // CHUNK

    def body(u_ref, w_ref, qd_ref, kd_ref, a_ref, gl_ref, o_ref, sin_ref, state):
        @pl.when(pl.program_id(0) == 0)
        def _():
            state[...] = jnp.zeros_like(state)
        s = state[...]
        sb = s.astype(BF16)
        sin_ref[0] = sb
        both = _bmm(jnp.concatenate([w_ref[...], qd_ref[...]], axis=1).astype(BF16), sb, B_NN)
        vn = (u_ref[...] - both[:, :CHUNK]).astype(BF16)
        o_ref[...] = both[:, CHUNK:] + _bmm(a_ref[...].astype(BF16), vn, B_NN)
        state[...] = s * gl_ref[:, 0] + _bmm(kd_ref[...].astype(BF16), vn, B_TN)

    blk = pl.BlockSpec((HEADS_A, CHUNK, 128), lambda n: (0, n, 0))
    return _pc(body, name=name,
               out_shape=(_sds((HEADS_A, T, 128), F32), _sds((n_chunks, HEADS_A, DK, 128), BF16)), grid=(n_chunks,),
               in_specs=[blk, blk, blk, blk, pl.BlockSpec((HEADS_A, CHUNK, CHUNK), lambda n: (0, n, 0)),
                         pl.BlockSpec((HEADS_A, 1, 1, 128), lambda n: (0, n, 0, 0))],
               out_specs=(blk, pl.BlockSpec((1, HEADS_A, DK, 128), lambda n: (n, 0, 0, 0))),
               scratch=[pltpu.VMEM((HEADS_A, DK, 128), F32)], sem=("arbitrary",))(u, w, qd, kd, aqk, gl)


def gdn_scan_bwd(u, w, qd, kd, aqk, gl, sin, do, name):
    T = u.shape[1]
    n_chunks = T // CHUNK

    def body(u_ref, w_ref, qd_ref, kd_ref, a_ref, gl_ref, sin_ref, do_ref,
             du_ref, dw_ref, dqd_ref, dkd_ref, da_ref, dgl_ref, dstate):
        @pl.when(pl.program_id(0) == 0)
        def _():
            dstate[...] = jnp.zeros_like(dstate)
        lane0 = lax.broadcasted_iota(jnp.int32, (HEADS_A, 1, 128), 2) == 0
        sb = sin_ref[0]
        s = sb.astype(F32)
        wb, qdb, kdb = w_ref[...].astype(BF16), qd_ref[...].astype(BF16), kd_ref[...].astype(BF16)
        ab, dob = a_ref[...].astype(BF16), do_ref[...].astype(BF16)
        vn = (u_ref[...] - _bmm(wb, sb, B_NN)).astype(BF16)
        ds_out = dstate[...]
        dsb = ds_out.astype(BF16)
        dqd_ref[...] = _bmm(dob, sb, B_NT).astype(BF16)
        da_ref[...] = _bmm(dob, vn, B_NT).astype(BF16)
        dv = _bmm(ab, dob, B_TN) + _bmm(kdb, dsb, B_NN)
        dkd_ref[...] = _bmm(vn, dsb, B_NT).astype(BF16)
        dgl = jnp.sum(jnp.sum(ds_out * s, axis=2, keepdims=True), axis=1, keepdims=True)
        dgl_ref[:, 0] = jnp.where(lane0, dgl, 0.0)
        dvb = dv.astype(BF16)
        du_ref[...] = dvb
        dw_ref[...] = (-_bmm(dvb, sb, B_NT)).astype(BF16)
        dstate[...] = ds_out * gl_ref[:, 0] + _bmm(qdb, dob, B_TN) - _bmm(wb, dvb, B_TN)

    last = n_chunks - 1
    blk = pl.BlockSpec((HEADS_A, CHUNK, 128), lambda n: (0, last - n, 0))
    ablk = pl.BlockSpec((HEADS_A, CHUNK, CHUNK), lambda n: (0, last - n, 0))
    glblk = pl.BlockSpec((HEADS_A, 1, 1, 128), lambda n: (0, last - n, 0, 0))
    per_head = _sds((HEADS_A, T, 128), BF16)
    return _pc(body, name=name,
               out_shape=(per_head, per_head, per_head, per_head, _sds((HEADS_A, T, CHUNK), BF16),
                          _sds((HEADS_A, n_chunks, 1, 128), F32)), grid=(n_chunks,),
               in_specs=[blk, blk, blk, blk, ablk, glblk,
                         pl.BlockSpec((1, HEADS_A, DK, 128), lambda n: (last - n, 0, 0, 0)), blk],
               out_specs=(blk, blk, blk, blk, ablk, glblk),
               scratch=[pltpu.VMEM((HEADS_A, DK, 128), F32)], sem=("arbitrary",))(u, w, qd, kd, aqk, gl, sin, do)


def gdn_outnorm_fwd(o, proj, wn, name):
    T = o.shape[1]
    tm = _tile(T, 512)

    def body(o_ref, z_ref, wn_ref, y_ref):
        for h in range(HEADS_A):
            z = z_ref[:, 128 * h:128 * (h + 1)].astype(F32)
            y_ref[:, 128 * h:128 * (h + 1)] = (_rms_fwd(o_ref[h], wn_ref[...]) * (z * _sigmoid(z))).astype(BF16)

    return _pc(body, name=name, out_shape=_sds((T, D), BF16), grid=(T // tm,),
               in_specs=[pl.BlockSpec((HEADS_A, tm, 128), lambda i: (0, i, 0)),
                         pl.BlockSpec((tm, D), lambda i: (i, Z_BLK0 * 128 // D)), pl.BlockSpec((1, 128), lambda i: (0, 0))],
               out_specs=pl.BlockSpec((tm, D), lambda i: (i, 0)), sem=("parallel",))(o, proj, wn)


def gdn_outnorm_bwd(o, proj, wn, dy, name):
    T = o.shape[1]
    tm = _tile(T, 512)

    def body(o_ref, z_ref, wn_ref, dy_ref, do_ref, dz_ref, dwn_ref):
        wn = wn_ref[...]
        acc = jnp.zeros((1, 128), F32)
        for h in range(HEADS_A):
            cols = slice(128 * h, 128 * (h + 1))
            z, dyh, ov = z_ref[:, cols].astype(F32), dy_ref[:, cols], o_ref[h]
            sg = _sigmoid(z)
            do, dwn = _rms_bwd(ov, wn, dyh * (z * sg))
            do_ref[h] = do.astype(BF16)
            acc = acc + dwn
            dz_ref[:, cols] = (dyh * _rms_fwd(ov, wn) * (sg * (1.0 + z * (1.0 - sg)))).astype(BF16)

        @pl.when(pl.program_id(0) == 0)
        def _():
            dwn_ref[...] = jnp.zeros_like(dwn_ref)
        dwn_ref[...] += acc

    row = pl.BlockSpec((tm, D), lambda i: (i, 0))
    vec = pl.BlockSpec((1, 128), lambda i: (0, 0))
    hblk = pl.BlockSpec((HEADS_A, tm, 128), lambda i: (0, i, 0))
    z_cols = pl.BlockSpec((tm, D), lambda i: (i, Z_BLK0 * 128 // D))
    return _pc(body, name=name, out_shape=(_sds((HEADS_A, T, 128), BF16), _sds((T, A_COLS), BF16), _sds((1, 128), F32)),
               grid=(T // tm,), in_specs=[hblk, z_cols, vec, row],
               out_specs=(hblk, z_cols, vec), sem=("arbitrary",))(o, proj, wn, dy)


def gdn_forward(x, nw, w_in, wconv, al, dtb, wn, w_out, tag, deps=()):
    h = rmsnorm_bf16(x, nw, f"{tag}_norm", deps)
    proj = mm_nn(h, w_in, f"{tag}_proj", out_dtype=BF16, cols=(0, MAIN_COLS))
    ba = mm_nn(h, w_in, f"{tag}_proj_ba", cols=(MAIN_COLS, A_COLS))
    c, qkv = gdn_conv_fwd(proj, wconv, f"{tag}_conv")
    gb = gdn_gate_fwd(ba, al, dtb, f"{tag}_gate")
    u, w, qd, kd, aqk, gl, inv = gdn_prep_fwd(qkv, gb, f"{tag}_prep")
    o, sin = gdn_scan_fwd(u, w, qd, kd, aqk, gl, f"{tag}_scan")
    on = gdn_outnorm_fwd(o, proj, wn, f"{tag}_outnorm")
    y = mm_nn(on, w_out, f"{tag}_out", residual=x)
    return y, (x, h, proj, ba, c, qkv, gb, inv, (u, w, qd, kd, aqk, gl), sin, o, on)


def gdn_backward(dout, saved, nw, w_in, wconv, al, dtb, wn, w_out, tag):
    x, h, proj, ba, c, qkv, gb, inv, prep, sin, o, on = saved
    d_on = mm_nt(dout, w_out, f"{tag}_out_bwd")
    dw_out = mm_tn(on, dout, f"{tag}_out_wgrad")
    do, dproj, dwn = gdn_outnorm_bwd(o, proj, wn, d_on, f"{tag}_outnorm_bwd")
    du, dw, dqd, dkd, da, dgl = gdn_scan_bwd(*prep, sin, do, f"{tag}_scan_bwd")
    dqkv, dgb = gdn_prep_bwd(qkv, gb, inv, du, dw, dqd, dkd, da, dgl, f"{tag}_prep_bwd")
    dproj, dal, ddt = gdn_gate_bwd(ba, al, dtb, dgb, dproj, f"{tag}_gate_bwd")
    dproj, dwconv = gdn_conv_bwd(dqkv, c, proj, wconv, dproj, f"{tag}_conv_bwd")
    dw_in = mm_tn(h, dproj, f"{tag}_proj_wgrad")
    dx, dnw = mm_nt(dproj, w_in, f"{tag}_proj_bwd", norm_bwd=(x, nw, dout))
    return dx, dnw, dw_in, dwconv, dal, ddt, dwn, dw_out


N_KV, GROUP = 4, 4
KV_COLS = 2 * N_KV * B_HD
B_COLS = D + KV_COLS


@jax.custom_vjp
def _swap_lane_halves(x):
    return pltpu.roll(x, 64, 1)


_swap_lane_halves.defvjp(lambda x: (pltpu.roll(x, 64, 1), None), lambda _, g: (pltpu.roll(g, 64, 1),))


def _swa_block(q, kp, kc, vp, vc, sk, first):
    cols = GROUP * B_BLK
    ks = lax.broadcasted_iota(jnp.int32, (N_KV, B_BLK, cols), 1)
    qi = lax.broadcasted_iota(jnp.int32, (N_KV, B_BLK, cols), 2) % B_BLK
    from_cur = ks <= qi

    def batch(parts):
        return jnp.concatenate([part[None] for part in parts], axis=0)

    def per_kv(cur, prev):
        return batch([jnp.concatenate([cur[:, j * B_HD:(j + 1) * B_HD], prev[:, j * B_HD:(j + 1) * B_HD]], axis=0)
                      for j in range(N_KV)]).astype(BF16)

    qs = batch([jnp.concatenate([q[:, hq * B_HD:(hq + 1) * B_HD] for hq in range(GROUP * j, GROUP * (j + 1))], axis=0)
                for j in range(N_KV)])
    q_t = jnp.swapaxes(qs, 1, 2).astype(BF16)
    sink = batch([jnp.concatenate([jnp.broadcast_to(sk[:, hq:hq + 1], (1, B_BLK))
                                   for hq in range(GROUP * j, GROUP * (j + 1))], axis=1) for j in range(N_KV)])
    both = _bmm(per_kv(kc, kp), q_t, B_NN)
    s = jnp.where(from_cur, both[:, :B_BLK], jnp.where(first, -1e30, both[:, B_BLK:])) * (B_HD ** -0.5)
    m = lax.stop_gradient(jnp.maximum(jnp.max(s, axis=1, keepdims=True), sink))
    e = jnp.exp((s - m).astype(BF16))
    den = jnp.sum(e.astype(F32), axis=1, keepdims=True) + jnp.exp(sink - m)
    p = e * (1.0 / den).astype(BF16)
    zero = jnp.zeros_like(p)
    p_both = jnp.concatenate([jnp.where(from_cur, p, zero), jnp.where(from_cur, zero, p)], axis=1)
    o = jnp.swapaxes(_bmm(per_kv(vc, vp), p_both, B_TN), 1, 2)
    return jnp.concatenate([o[j, g * B_BLK:(g + 1) * B_BLK] for j in range(N_KV) for g in range(GROUP)], axis=1)


def swa_core_fwd(proj, sk, name):
    T = proj.shape[0]
    half = N_KV * B_HD

    def body(q_ref, kvc_ref, kvp_ref, sk_ref, o_ref):
        kvc, kvp = kvc_ref[...], kvp_ref[...]
        o_ref[...] = _swa_block(q_ref[...], kvp[:, :half], kvc[:, :half], kvp[:, half:], kvc[:, half:], sk_ref[...],
                                pl.program_id(0) == 0).astype(BF16)

    return _pc(body, name=name, out_shape=_sds((T, D), BF16), grid=(T // B_BLK,),
               in_specs=[pl.BlockSpec((B_BLK, D), lambda n: (n, 0)),
                         pl.BlockSpec((B_BLK, KV_COLS), lambda n: (n, D // KV_COLS)),
                         pl.BlockSpec((B_BLK, KV_COLS), lambda n: (jnp.maximum(n - 1, 0), D // KV_COLS)),
                         pl.BlockSpec((1, 128), lambda n: (0, 0))],
               out_specs=pl.BlockSpec((B_BLK, D), lambda n: (n, 0)), sem=("parallel",))(proj, proj, proj, sk)


def swa_core_bwd(proj, sk, do, name):
    T = proj.shape[0]
    last = T // B_BLK - 1
    half = N_KV * B_HD

    def body(q_ref, kvc_ref, kvp_ref, sk_ref, do_ref, dproj_ref, dbias_ref, dsk_ref, carry):
        step = pl.program_id(0)
        first = step == last

        @pl.when(step == 0)
        def _():
            carry[...] = jnp.zeros_like(carry)
            dbias_ref[...] = jnp.zeros_like(dbias_ref)
            dsk_ref[...] = jnp.zeros_like(dsk_ref)
        kvc, kvp = kvc_ref[...], kvp_ref[...]
        _, vjp = jax.vjp(functools.partial(_swa_block, first=first), q_ref[...], kvp[:, :half], kvc[:, :half],
                         kvp[:, half:], kvc[:, half:], sk_ref[...])
        dq, dkp, dkc, dvp, dvc, dsk = vjp(do_ref[...])
        dkv = jnp.concatenate([dkc, dvc], axis=1) + carry[...]
        carry[...] = jnp.concatenate([dkp, dvp], axis=1)
        row = jnp.concatenate([dq, dkv], axis=1)
        dproj_ref[...] = row.astype(BF16)
        dbias_ref[...] += jnp.sum(row, axis=0, keepdims=True)
        dsk_ref[...] += dsk

    return _pc(body, name=name, out_shape=(_sds((T, B_COLS), BF16), _sds((1, B_COLS), F32), _sds((1, 128), F32)),
               grid=(T // B_BLK,),
               in_specs=[pl.BlockSpec((B_BLK, D), lambda n: (last - n, 0)),
                         pl.BlockSpec((B_BLK, KV_COLS), lambda n: (last - n, D // KV_COLS)),
                         pl.BlockSpec((B_BLK, KV_COLS), lambda n: (jnp.maximum(last - n - 1, 0), D // KV_COLS)),
                         pl.BlockSpec((1, 128), lambda n: (0, 0)), pl.BlockSpec((B_BLK, D), lambda n: (last - n, 0))],
               out_specs=(pl.BlockSpec((B_BLK, B_COLS), lambda n: (last - n, 0)),
                          pl.BlockSpec((1, B_COLS), lambda n: (0, 0)), pl.BlockSpec((1, 128), lambda n: (0, 0))),
               scratch=[pltpu.VMEM((B_BLK, KV_COLS), F32)], sem=("arbitrary",))(proj, proj, proj, sk, do)


def col_sum(a, name):
    T, N = a.shape
    tm = _tile(T, 1024)

    def body(a_ref, o_ref):
        @pl.when(pl.program_id(0) == 0)
        def _():
            o_ref[...] = jnp.zeros_like(o_ref)
        o_ref[...] += jnp.sum(a_ref[...].astype(F32), axis=0, keepdims=True)

    return _pc(body, name=name, out_shape=_sds((1, N), F32), grid=(T // tm,),
               in_specs=[pl.BlockSpec((tm, N), lambda i: (i, 0))], out_specs=pl.BlockSpec((1, N), lambda i: (0, 0)),
               sem=("arbitrary",))(a)


def swa_forward(x, nw, w_in, b_in, sk, w_out, b_out, tag):
    h = rmsnorm_bf16(x, nw, f"{tag}_norm")
    proj = mm_nn(h, w_in, f"{tag}_proj", bias=b_in)
    o = swa_core_fwd(proj, sk, f"{tag}_core")
    y = mm_nn(o, w_out, f"{tag}_out", bias=b_out, residual=x)
    return y, (x, h, proj, o)


def swa_backward(dout, saved, nw, w_in, b_in, sk, w_out, b_out, tag):
    x, h, proj, o = saved
    do = mm_nt(dout, w_out, f"{tag}_out_bwd")
    dw_out = mm_tn(o, dout, f"{tag}_out_wgrad")
    db_out = col_sum(dout, f"{tag}_out_bias_grad")
    dproj, db_in, dsk = swa_core_bwd(proj, sk, do, f"{tag}_core_bwd")
    dw_in = mm_tn(h, dproj, f"{tag}_proj_wgrad")
    dx, dnw = mm_nt(dproj, w_in, f"{tag}_proj_bwd", norm_bwd=(x, nw, dout))
    return dx, dnw, dw_in, db_in, dsk, dw_out, db_out


MESH = pl.DeviceIdType.MESH


def _position():
    return lax.axis_index("x"), lax.axis_index("y"), lax.axis_index("c")


def _slot(x, y, c):
    return 4 * x + 2 * y + c


def _peer(x, y, c, k):
    return (1 - x if k & 4 else x, 1 - y if k & 2 else y, 1 - c if k & 1 else c)


HBM_SPEC = pl.BlockSpec(memory_space=pltpu.HBM)
SEM_SPEC = pl.BlockSpec(memory_space=pltpu.SEMAPHORE)
DEP_SPEC = pl.BlockSpec(memory_space=pl.ANY)
SIDE_EFFECT = pltpu.SideEffectType.DATAFLOW_SIDE_EFFECTING
N_PEERS = N_DEV - 1


def _push_copies(srcs, lands, send_sems, recv_sems, scatter):
    x, y, c = _position()
    me = _slot(x, y, c)
    copies = []
    for k in (1, 2, 4, 3, 5, 6, 7):
        peer = _peer(x, y, c, k)
        for a in range(len(srcs)):
            copies.append(pltpu.make_async_remote_copy(
                src_ref=srcs[a].at[_slot(*peer)] if scatter else srcs[a], dst_ref=lands[a].at[me],
                send_sem=send_sems.at[N_PEERS * a + k - 1], recv_sem=recv_sems.at[N_PEERS * a + k - 1],
                device_id=peer, device_id_type=MESH))
    return copies


def push_start(srcs, lands, name, scatter, deps=()):
    n = len(srcs)
    first_out = 2 * n + len(deps)

    def body(*refs):
        for cp in _push_copies(refs[:n], refs[n:2 * n], refs[first_out], refs[first_out + 1], scatter):
            cp.start()
        refs[-1][...] = jnp.zeros_like(refs[-1])

    passed = [pltpu.HBM(t.shape, t.dtype) for t in list(srcs) + list(lands)]
    res = pl.pallas_call(
        body, name=name,
        out_shape=(pltpu.SemaphoreType.DMA((N_PEERS * n,)), pltpu.SemaphoreType.DMA((N_PEERS * n,)), *passed, _sds((8, 128), F32)),
        in_specs=[HBM_SPEC] * (2 * n) + [DEP_SPEC] * len(deps),
        out_specs=(SEM_SPEC, SEM_SPEC, *([HBM_SPEC] * (2 * n)), pl.BlockSpec(memory_space=pltpu.VMEM)),
        input_output_aliases={i: 2 + i for i in range(2 * n)},
        compiler_params=pltpu.CompilerParams(has_side_effects=SIDE_EFFECT),
    )(*[pltpu.with_memory_space_constraint(t, pltpu.HBM) for t in list(srcs) + list(lands)], *deps)
    return (res[0], res[1], list(res[2:2 + n]), list(res[2 + n:2 + 2 * n])), res[-1]


def push_wait(handles, after, name, scatter):
    send_sems, recv_sems, srcs, lands = handles
    n = len(srcs)
    after = tuple(after) if isinstance(after, (tuple, list)) else (after,)

    def body(*refs):
        for cp in _push_copies(refs[:n], refs[n:2 * n], refs[2 * n], refs[2 * n + 1], scatter):
            cp.wait_send()
            cp.wait_recv()

    res = pl.pallas_call(
        body, name=name, out_shape=tuple(pltpu.HBM(t.shape, t.dtype) for t in srcs + lands),
        in_specs=[HBM_SPEC] * (2 * n) + [SEM_SPEC, SEM_SPEC] + [DEP_SPEC] * len(after), out_specs=tuple([HBM_SPEC] * (2 * n)),
        input_output_aliases={i: i for i in range(2 * n)},
        compiler_params=pltpu.CompilerParams(has_side_effects=SIDE_EFFECT),
    )(*srcs, *lands, send_sems, recv_sems, *after)
    return list(res[n:])


def gather_start(shards, name, deps=()):
    me = _slot(*_position())
    lands = [lax.dynamic_update_slice(lax.empty((N_DEV,) + t.shape, t.dtype), t[None], (me,) + (0,) * t.ndim) for t in shards]
    return push_start(shards, lands, name, scatter=False, deps=deps)


def exchange_start(parts, name):
    me = _slot(*_position())
    lands = [lax.dynamic_update_slice(lax.empty(t.shape, t.dtype), lax.dynamic_index_in_dim(t, me, 0, keepdims=True),
                                      (me,) + (0,) * (t.ndim - 1)) for t in parts]
    return push_start(parts, lands, name, scatter=True)


def _row_tile(rows, cols):
    best = rows
    for t in range(16, rows, 16):
        if rows % t == 0 and t * cols * 4 <= (1 << 20):
            best = t
    return best


def adam_update(parts, w, m, v, name):
    n_layers = len(parts)
    P, R, C = parts[0].shape
    tr = _row_tile(R, C)
    n_t = R // tr

    def body(*refs):
        p_refs = refs[:n_layers]
        w_ref, m_ref, v_ref, g_ref, d_ref, nm_ref, nv_ref = refs[n_layers:]
        for layer in range(n_layers):
            @pl.when(pl.program_id(0) == layer)
            def _(p_ref=p_refs[layer]):
                g = p_ref[0].astype(F32)
                for s in range(1, P):
                    g = g + p_ref[s].astype(F32)
                new_m = ADAM_B1 * m_ref[0] + (1.0 - ADAM_B1) * g
                new_v = ADAM_B2 * v_ref[0] + (1.0 - ADAM_B2) * (g * g)
                m_hat = new_m / (1.0 - ADAM_B1 ** ADAM_STEP)
                v_hat = new_v / (1.0 - ADAM_B2 ** ADAM_STEP)
                g_ref[0] = g
                d_ref[0] = -ADAM_LR * (m_hat / (jnp.sqrt(v_hat) + ADAM_EPS) + ADAM_WD * w_ref[0])
                nm_ref[0] = new_m
                nv_ref[0] = new_v

    def part_spec(layer):
        return pl.BlockSpec((P, tr, C), lambda l_, i: (0, jnp.where(l_ == layer, i, jnp.where(l_ < layer, 0, n_t - 1)), 0))

    blk = pl.BlockSpec((1, tr, C), lambda l_, i: (l_, i, 0))
    out = _sds((n_layers, R, C), F32)
    return _pc(body, name=name, out_shape=(out, out, out, out), grid=(n_layers, n_t),
               in_specs=[part_spec(layer) for layer in range(n_layers)] + [blk, blk, blk],
               out_specs=(blk, blk, blk, blk), sem=("arbitrary", "arbitrary"))(*parts, w, m, v)


WEIGHTS = ("ffn1_norm", "ffn1_w_gu", "ffn1_w_down", "mix_norm", "ffn2_norm", "ffn2_w_gu", "ffn2_w_down", "a_w_in",
           "a_w_conv", "a_A_log", "a_dt_bias", "a_out_norm", "a_w_out", "b_w_in", "b_b_in", "b_sinks", "b_w_out",
           "b_b_out", "final_norm")
SHARDED = ("ffn1_w_gu", "ffn1_w_down", "ffn2_w_gu", "ffn2_w_down", "a_w_in", "a_w_conv", "a_w_out", "b_w_in", "b_b_in",
           "b_w_out", "b_b_out")
MISC_LANES = dict(a_A_log=(0, 8), a_dt_bias=(8, 16), b_sinks=(16, 32), a_out_norm=(128, 256))
LOSS_LANE = 256


def _pack_small(t):
    misc = jnp.zeros((D,), F32)
    for key, (lo, hi) in MISC_LANES.items():
        misc = misc.at[lo:hi].set(t[key].reshape(-1))
    if "loss" in t:
        misc = misc.at[LOSS_LANE].set(t["loss"])
    return jnp.concatenate([t["ffn1_norm"], t["mix_norm"], t["ffn2_norm"], t["final_norm"].reshape(1, D), misc[None]], axis=0)


def _unpack_small(p, like):
    out = dict(ffn1_norm=p[0:2], mix_norm=p[2:4], ffn2_norm=p[4:6], final_norm=p[6])
    for key, (lo, hi) in MISC_LANES.items():
        out[key] = p[7, lo:hi].reshape(like[key].shape)
    return out


def kernel(x, ffn1_norm, ffn1_w_gu, ffn1_w_down, mix_norm, ffn2_norm, ffn2_w_gu, ffn2_w_down, a_w_in, a_w_conv, a_A_log, a_dt_bias, a_out_norm, a_w_out, b_w_in, b_b_in, b_sinks, b_w_out, b_b_out, final_norm, loss_target, m_ffn1_norm, m_ffn1_w_gu, m_ffn1_w_down, m_mix_norm, m_ffn2_norm, m_ffn2_w_gu, m_ffn2_w_down, m_a_w_in, m_a_w_conv, m_a_A_log, m_a_dt_bias, m_a_out_norm, m_a_w_out, m_b_w_in, m_b_b_in, m_b_sinks, m_b_w_out, m_b_b_out, m_final_norm, v_ffn1_norm, v_ffn1_w_gu, v_ffn1_w_down, v_mix_norm, v_ffn2_norm, v_ffn2_w_gu, v_ffn2_w_down, v_a_w_in, v_a_w_conv, v_a_A_log, v_a_dt_bias, v_a_out_norm, v_a_w_out, v_b_w_in, v_b_b_in, v_b_sinks, v_b_w_out, v_b_b_out, v_final_norm):
    w = dict(ffn1_norm=ffn1_norm, ffn1_w_gu=ffn1_w_gu, ffn1_w_down=ffn1_w_down, mix_norm=mix_norm, ffn2_norm=ffn2_norm, ffn2_w_gu=ffn2_w_gu, ffn2_w_down=ffn2_w_down, a_w_in=a_w_in, a_w_conv=a_w_conv, a_A_log=a_A_log, a_dt_bias=a_dt_bias, a_out_norm=a_out_norm, a_w_out=a_w_out, b_w_in=b_w_in, b_b_in=b_b_in, b_sinks=b_sinks, b_w_out=b_w_out, b_b_out=b_b_out, final_norm=final_norm)
    m = dict(ffn1_norm=m_ffn1_norm, ffn1_w_gu=m_ffn1_w_gu, ffn1_w_down=m_ffn1_w_down, mix_norm=m_mix_norm, ffn2_norm=m_ffn2_norm, ffn2_w_gu=m_ffn2_w_gu, ffn2_w_down=m_ffn2_w_down, a_w_in=m_a_w_in, a_w_conv=m_a_w_conv, a_A_log=m_a_A_log, a_dt_bias=m_a_dt_bias, a_out_norm=m_a_out_norm, a_w_out=m_a_w_out, b_w_in=m_b_w_in, b_b_in=m_b_b_in, b_sinks=m_b_sinks, b_w_out=m_b_w_out, b_b_out=m_b_b_out, final_norm=m_final_norm)
    v = dict(ffn1_norm=v_ffn1_norm, ffn1_w_gu=v_ffn1_w_gu, ffn1_w_down=v_ffn1_w_down, mix_norm=v_mix_norm, ffn2_norm=v_ffn2_norm, ffn2_w_gu=v_ffn2_w_gu, ffn2_w_down=v_ffn2_w_down, a_w_in=v_a_w_in, a_w_conv=v_a_w_conv, a_A_log=v_a_A_log, a_dt_bias=v_a_dt_bias, a_out_norm=v_a_out_norm, a_w_out=v_a_w_out, b_w_in=v_b_w_in, b_b_in=v_b_b_in, b_sinks=v_b_sinks, b_w_out=v_b_w_out, b_b_out=v_b_b_out, final_norm=v_final_norm)
    T = x.shape[1]
    x0, tgt = x.reshape(T, D), loss_target.reshape(T, D)

    def cast(t):
        return t.astype(BF16)

    h0, t0 = gather_start([cast(ffn1_w_gu[0])], "gather0_start")
    a_log_row = jnp.zeros((1, 128), F32).at[0, HEADS_A:2 * HEADS_A].set(a_A_log[0])
    dt_row = jnp.zeros((1, 128), F32).at[0, HEADS_A:2 * HEADS_A].set(a_dt_bias[0])
    sink_row = jnp.zeros((1, 128), F32).at[0, :b_sinks.shape[1]].set(b_sinks[0])
    a_in_cols = a_w_in.shape[-1] * N_DEV

    def down_blocks(t):
        return t.reshape(N_FB, FB, D)

    wgu, wdn, saved = {}, {}, []
    xn = rmsnorm_bf16(x0, ffn1_norm[0:1], "l0_ffn1_norm", (t0,))
    wgu["ffn1", 0] = push_wait(h0, xn, "gather0_wait", scatter=False)[0]
    h0d, t0d = gather_start([cast(ffn1_w_down[0])], "gather0d_start", deps=(wgu["ffn1", 0],))
    h1, t1 = gather_start([cast(a_w_in[0]), a_w_conv[0], cast(a_w_out[0])], "gather1_start", deps=(t0d,))
    gu = ffn_up(xn, wgu["ffn1", 0], "l0_ffn1_up", deps=(t0d, t1))
    wdn["ffn1", 0] = down_blocks(push_wait(h0d, gu, "gather0d_wait", scatter=False)[0])
    xs, s1 = ffn_down(gu, wdn["ffn1", 0], x0, "l0_ffn1_down"), (x0, xn, gu)
    got = push_wait(h1, xs, "gather1_wait", scatter=False)
    h1f, t1f = gather_start([cast(ffn2_w_gu[0]), cast(ffn2_w_down[0])], "gather1f_start", deps=(got[0],))
    g2 = [cast(ffn1_w_gu[1]), cast(ffn1_w_down[1]), cast(b_w_in[0]), b_b_in, cast(b_w_out[0]), b_b_out,
          cast(ffn2_w_gu[1]), cast(ffn2_w_down[1])]
    h2, t2 = gather_start(g2, "gather2_start", deps=(t1f,))
    a_in_full = jnp.pad(got[0].transpose(1, 0, 2).reshape(D, a_in_cols), ((0, 0), (0, A_COLS - a_in_cols)))
    gdn_args = (mix_norm[0:1], a_in_full, got[1].transpose(1, 0, 2).reshape(4, 3 * D), a_log_row, dt_row, a_out_norm,
                got[2].reshape(D, D))
    xs, sm = gdn_forward(xs, *gdn_args, "gdn", deps=(t1f, t2))
    got = push_wait(h1f, xs, "gather1f_wait", scatter=False)
    wgu["ffn2", 0], wdn["ffn2", 0] = got[0], down_blocks(got[1])
    xs, s2 = ffn_forward(xs, ffn2_norm[0:1], wgu["ffn2", 0], wdn["ffn2", 0], "l0_ffn2")
    saved.append((s1, sm, s2))
    got = push_wait(h2, xs, "gather2_wait", scatter=False)
    wgu["ffn1", 1], wdn["ffn1", 1] = got[0], down_blocks(got[1])
    swa_args = (mix_norm[1:2], got[2].transpose(1, 0, 2).reshape(D, B_COLS), got[3].reshape(1, B_COLS), sink_row,
                got[4].reshape(D, D), got[5].reshape(1, D))
    wgu["ffn2", 1], wdn["ffn2", 1] = got[6], down_blocks(got[7])
    xs, s1 = ffn_forward(xs, ffn1_norm[1:2], wgu["ffn1", 1], wdn["ffn1", 1], "l1_ffn1")
    xs, sm = swa_forward(xs, *swa_args, "swa")
    xs, s2 = ffn_forward(xs, ffn2_norm[1:2], wgu["ffn2", 1], wdn["ffn2", 1], "l1_ffn2")
    saved.append((s1, sm, s2))
    loss_row, dx, d_final_norm = final_loss(xs, final_norm.reshape(1, D), tgt, "final_loss")

    def down_slots(t):
        return cast(t.reshape(N_DEV, FB // 2, D))

    def col_slots(t, dtype=BF16):
        return t.reshape(t.shape[0], N_DEV, -1).transpose(1, 0, 2).astype(dtype)

    d_norm = {"ffn1_norm": [None, None], "mix_norm": [None, None], "ffn2_norm": [None, None]}
    exchanges = {}

    def sender(tag):
        def on_grads(d_gu, d_dn):
            exchanges[tag], token = exchange_start([cast(d_gu), down_slots(d_dn)], f"exchange_{tag}_start")
            return (token,)
        return on_grads

    s1, sm, s2 = saved[1]
    dx, d_norm["ffn2_norm"][1], _, _ = ffn_backward(dx, s2, ffn2_norm[1:2], wgu["ffn2", 1], wdn["ffn2", 1], "l1_ffn2",
                                                    on_grads=sender("l1_ffn2"))
    dx, d_norm["mix_norm"][1], d_b_in, d_b_bias_in, d_sinks, d_b_out, d_b_bias_out = swa_backward(dx, sm, *swa_args, "swa")
    exchanges["swa"], t_swa = exchange_start(
        [col_slots(d_b_in), d_b_bias_in.reshape(N_DEV, 1, -1), cast(d_b_out.reshape(N_DEV, D // N_DEV, D)),
         d_b_bias_out.reshape(N_DEV, 1, -1)], "exchange_swa_start")
    dx, d_norm["ffn1_norm"][1], _, _ = ffn_backward(dx, s1, ffn1_norm[1:2], wgu["ffn1", 1], wdn["ffn1", 1], "l1_ffn1",
                                                    deps=(t_swa,), on_grads=sender("l1_ffn1"))

    s1, sm, s2 = saved[0]
    dx, d_norm["ffn2_norm"][0], _, _ = ffn_backward(dx, s2, ffn2_norm[0:1], wgu["ffn2", 0], wdn["ffn2", 0], "l0_ffn2",
                                                    on_grads=sender("l0_ffn2"))
    dx, d_norm["mix_norm"][0], d_a_in, d_a_conv, d_alog, d_dt, d_onorm, d_a_out = gdn_backward(dx, sm, *gdn_args, "gdn")
    exchanges["gdn"], t_gdn = exchange_start(
        [col_slots(d_a_in[:, :a_in_cols]), col_slots(d_a_conv, F32), cast(d_a_out.reshape(N_DEV, D // N_DEV, D))],
        "exchange_gdn_start")
    dx, d_norm["ffn1_norm"][0], _, _ = ffn_backward(dx, s1, ffn1_norm[0:1], wgu["ffn1", 0], wdn["ffn1", 0], "l0_ffn1",
                                                    deps=(t_gdn,), on_grads=sender("l0_ffn1"))
    grad_x = dx.reshape(x.shape)
    got = {tag: push_wait(exchanges[tag], dx, f"exchange_{tag}_wait", scatter=True)
           for tag in ("l1_ffn2", "swa", "l1_ffn1", "l0_ffn2", "gdn")}
    received = dict(ffn2_w_gu=[got["l0_ffn2"][0], got["l1_ffn2"][0]], ffn2_w_down=[got["l0_ffn2"][1], got["l1_ffn2"][1]],
                    b_w_in=[got["swa"][0]], b_b_in=[got["swa"][1]], b_w_out=[got["swa"][2]], b_b_out=[got["swa"][3]],
                    a_w_in=[got["gdn"][0]], a_w_conv=[got["gdn"][1]], a_w_out=[got["gdn"][2]])

    grads, deltas, new_m, new_v = {}, {}, {}, {}

    def update(key):
        shape = w[key].shape
        cols = shape[-1]
        layers = lambda t: t.reshape(shape[0], -1, cols)
        out = adam_update([r.reshape(N_DEV, -1, cols) for r in received[key]], layers(w[key]), layers(m[key]), layers(v[key]),
                          f"adam_{key}")
        grads[key], deltas[key], new_m[key], new_v[key] = (t.reshape(shape) for t in out)

    for key in SHARDED:
        if key in received:
            update(key)
    done_first = [deltas[key] for key in received]

    small = dict(ffn1_norm=jnp.concatenate(d_norm["ffn1_norm"], axis=0), mix_norm=jnp.concatenate(d_norm["mix_norm"], axis=0),
                 ffn2_norm=jnp.concatenate(d_norm["ffn2_norm"], axis=0), final_norm=d_final_norm,
                 a_A_log=d_alog[0, HEADS_A:2 * HEADS_A], a_dt_bias=d_dt[0, HEADS_A:2 * HEADS_A],
                 b_sinks=d_sinks[0, :b_sinks.shape[1]], a_out_norm=d_onorm, loss=loss_row[0, 0])
    hs, ts = gather_start([_pack_small(small)], "gather_small_start")
    r3 = push_wait(exchanges["l0_ffn1"], done_first + [ts], "exchange_l0_ffn1_wait", scatter=True)
    received.update(ffn1_w_gu=[r3[0], got["l1_ffn1"][0]], ffn1_w_down=[r3[1], got["l1_ffn1"][1]])
    update("ffn1_w_gu")
    update("ffn1_w_down")
    every = push_wait(hs, deltas["ffn1_w_down"], "gather_small_wait", scatter=False)[0]
    out = adam_update([every], _pack_small(w)[None], _pack_small(m)[None], _pack_small(v)[None], "adam_small")
    for dst, packed in zip((grads, deltas, new_m, new_v), out):
        dst.update(_unpack_small(packed[0], w))
    loss = out[0][0, 7, LOSS_LANE]

    return (loss, grad_x, *[grads[k_] for k_ in WEIGHTS], *[deltas[k_] for k_ in WEIGHTS],
            *[new_m[k_] for k_ in WEIGHTS], *[new_v[k_] for k_ in WEIGHTS])
```

```python
import functools

import jax
import jax.numpy as jnp
from jax import lax
from jax.experimental import pallas as pl
from jax.experimental.pallas import tpu as pltpu

F32, BF16 = jnp.float32, jnp.bfloat16
HI = lax.Precision.HIGHEST
EPS = 1e-6

N_DEV = 8
D = 1024
FB = 704
N_FB = 4
HEADS_A, DK = 8, 128
CHUNK = 64
PREP_T = 512
A_COLS = 4224
B_HD, B_BLK = 64, 128
VMEM_LIMIT_V7X = 60 * 1024 * 1024

ADAM_LR, ADAM_B1, ADAM_B2, ADAM_EPS, ADAM_WD, ADAM_STEP = 0.001, 0.9, 0.999, 1e-08, 0.01, 10

NT = (((1,), (1,)), ((), ()))
TN = (((0,), (0,)), ((), ()))


def _pc(body, *, name, out_shape, grid=(), in_specs=None, out_specs=None, scratch=(), sem=None, **kw):
    params = pltpu.CompilerParams(dimension_semantics=sem, vmem_limit_bytes=VMEM_LIMIT_V7X)
    return pl.pallas_call(body, name=name, out_shape=out_shape, grid=grid, in_specs=in_specs, out_specs=out_specs,
                          scratch_shapes=list(scratch), compiler_params=params, **kw)


def _sds(shape, dtype):
    return jax.ShapeDtypeStruct(tuple(shape), dtype)


def _dot(a, b, dims=None, precision=None):
    if dims is None:
        return jnp.dot(a, b, preferred_element_type=F32, precision=precision)
    return lax.dot_general(a, b, dims, preferred_element_type=F32, precision=precision)


def _sigmoid(x):
    return 1.0 / (1.0 + jnp.exp(-x))


def _softplus(x):
    return jnp.maximum(x, 0.0) + jnp.log(1.0 + jnp.exp(-jnp.abs(x)))


def _rms_fwd(x, w):
    r = lax.rsqrt(jnp.mean(x * x, axis=-1, keepdims=True) + EPS)
    return x * r * w


def _rms_bwd(x, w, dy):
    r = lax.rsqrt(jnp.mean(x * x, axis=-1, keepdims=True) + EPS)
    xh = x * r
    dxh = dy * w
    dx = r * (dxh - xh * jnp.mean(dxh * xh, axis=-1, keepdims=True))
    return dx, jnp.sum(dy * xh, axis=0, keepdims=True)


def _tile(n, want):
    t = min(n, want)
    assert n % t == 0, (n, want)
    return t


def rmsnorm_bf16(x, w, name, deps=()):
    T = x.shape[0]
    tm = _tile(T, 1024)

    def body(x_ref, w_ref, *rest):
        rest[-1][...] = _rms_fwd(x_ref[...], w_ref[...]).astype(BF16)

    return _pc(body, name=name, out_shape=_sds((T, D), BF16), grid=(T // tm,),
               in_specs=[pl.BlockSpec((tm, D), lambda i: (i, 0)), pl.BlockSpec((1, D), lambda i: (0, 0))] + [DEP_SPEC] * len(deps),
               out_specs=pl.BlockSpec((tm, D), lambda i: (i, 0)), sem=("parallel",))(x, w, *deps)


def final_loss(x, w, tgt, name):
    T = x.shape[0]
    tm = _tile(T, 512)

    def body(x_ref, w_ref, t_ref, loss_ref, dx_ref, dw_ref):
        xv, wv = x_ref[...], w_ref[...]
        err = _rms_fwd(xv, wv) - t_ref[...]
        dx, dw = _rms_bwd(xv, wv, err * (1.0 / D))
        dx_ref[...] = dx

        @pl.when(pl.program_id(0) == 0)
        def _():
            dw_ref[...] = jnp.zeros_like(dw_ref)
            loss_ref[...] = jnp.zeros_like(loss_ref)
        dw_ref[...] += dw
        loss_ref[...] += jnp.full((1, 128), 0.5 / D, F32) * jnp.sum(err * err)

    row = pl.BlockSpec((tm, D), lambda i: (i, 0))
    vec = pl.BlockSpec((1, D), lambda i: (0, 0))
    return _pc(body, name=name, out_shape=(_sds((1, 128), F32), _sds((T, D), F32), _sds((1, D), F32)),
               grid=(T // tm,), in_specs=[row, vec, row],
               out_specs=(pl.BlockSpec((1, 128), lambda i: (0, 0)), row, vec), sem=("arbitrary",))(x, w, tgt)


def _col_tile(n):
    for t in (1536, 1408, 1024, 768, 512, 384, 256, 128):
        if n % t == 0:
            return t
    return n


def mm_nn(a, b, name, bias=None, residual=None, out_dtype=F32, cols=None):
    T, K = a.shape
    first, end = cols or (0, b.shape[1])
    N = end - first
    tm, tn = _tile(T, 512), _col_tile(N)
    assert first % tn == 0 and (cols is None or (bias is None and residual is None))
    j0 = first // tn

    def body(a_ref, b_ref, *rest):
        o_ref = rest[-1]
        acc = _dot(a_ref[...].astype(BF16), b_ref[...])
        for extra in rest[:-1]:
            acc = acc + extra[...]
        o_ref[...] = acc.astype(out_dtype)

    in_specs = [pl.BlockSpec((tm, K), lambda j, i: (i, 0)), pl.BlockSpec((K, tn), lambda j, i: (0, j0 + j))]
    args = [a, b]
    if bias is not None:
        in_specs.append(pl.BlockSpec((1, tn), lambda j, i: (0, j)))
        args.append(bias)
    if residual is not None:
        in_specs.append(pl.BlockSpec((tm, tn), lambda j, i: (i, j)))
        args.append(residual)
    return _pc(body, name=name, out_shape=_sds((T, N), out_dtype), grid=(N // tn, T // tm), in_specs=in_specs,
               out_specs=pl.BlockSpec((tm, tn), lambda j, i: (i, j)), sem=("parallel", "parallel"))(*args)


def mm_nt(a, b, name, out_dtype=F32, norm_bwd=None):
    T, N = a.shape
    K = b.shape[0]
    tm = _tile(T, 512)
    row = pl.BlockSpec((tm, K), lambda i: (i, 0))
    in_specs = [pl.BlockSpec((tm, N), lambda i: (i, 0)), _resident((K, N))]

    if norm_bwd is None:
        def body(a_ref, b_ref, o_ref):
            o_ref[...] = _dot(a_ref[...].astype(BF16), b_ref[...], NT).astype(out_dtype)

        return _pc(body, name=name, out_shape=_sds((T, K), out_dtype), grid=(T // tm,), in_specs=in_specs,
                   out_specs=row, sem=("parallel",))(a, b)

    def body(a_ref, b_ref, x_ref, w_ref, dres_ref, dx_ref, dw_ref):
        dx, dw = _rms_bwd(x_ref[...], w_ref[...], _dot(a_ref[...].astype(BF16), b_ref[...], NT))
        dx_ref[...] = dres_ref[...] + dx

        @pl.when(pl.program_id(0) == 0)
        def _():
            dw_ref[...] = jnp.zeros_like(dw_ref)
        dw_ref[...] += dw

    vec = pl.BlockSpec((1, K), lambda i: (0, 0))
    return _pc(body, name=name, out_shape=(_sds((T, K), F32), _sds((1, K), F32)), grid=(T // tm,),
               in_specs=in_specs + [row, vec, row], out_specs=(row, vec), sem=("arbitrary",))(a, b, *norm_bwd)


def mm_tn(a, b, name):
    T, K = a.shape
    N = b.shape[1]
    tt, tn = _tile(T, 1024), _col_tile(N)

    def body(a_ref, b_ref, o_ref):
        @pl.when(pl.program_id(1) == 0)
        def _():
            o_ref[...] = jnp.zeros_like(o_ref)
        o_ref[...] += _dot(a_ref[...].astype(BF16), b_ref[...].astype(BF16), TN)

    return _pc(body, name=name, out_shape=_sds((K, N), F32), grid=(N // tn, T // tt),
               in_specs=[pl.BlockSpec((tt, K), lambda j, t: (t, 0)), pl.BlockSpec((tt, tn), lambda j, t: (t, j))],
               out_specs=pl.BlockSpec((K, tn), lambda j, t: (0, j)), sem=("parallel", "arbitrary"))(a, b)


def ffn_up(xn, wgu, name, deps=()):
    T = xn.shape[0]
    tm = _tile(T, 1024)

    def body(x_ref, w_ref, *rest):
        xv = x_ref[...]
        for j in range(2 * N_FB):
            rest[-1][j] = _dot(xv, w_ref[j]).astype(BF16)

    return _pc(body, name=name, out_shape=_sds((2 * N_FB, T, FB), BF16), grid=(T // tm,),
               in_specs=[pl.BlockSpec((tm, D), lambda i: (i, 0)), _resident((2 * N_FB, D, FB))] + [DEP_SPEC] * len(deps),
               out_specs=pl.BlockSpec((2 * N_FB, tm, FB), lambda i: (0, i, 0)), sem=("parallel",))(xn, wgu, *deps)


def ffn_down(gu, wd, x, name):
    T = x.shape[0]
    tm = _tile(T, 512)

    def body(gu_ref, w_ref, x_ref, o_ref):
        acc = jnp.zeros((tm, D), F32)
        for g in range(N_FB):
            gate, up = gu_ref[g], gu_ref[N_FB + g]
            acc = acc + _dot(gate * _sigmoid(gate) * up, w_ref[g])
        o_ref[...] = x_ref[...] + 0.5 * acc

    row = pl.BlockSpec((tm, D), lambda i: (i, 0))
    return _pc(body, name=name, out_shape=_sds((T, D), F32), grid=(T // tm,),
               in_specs=[pl.BlockSpec((2 * N_FB, tm, FB), lambda i: (0, i, 0)),
                         _resident((N_FB, FB, D)), row],
               out_specs=row, sem=("parallel",))(gu, wd, x)


def _resident(shape):
    return pl.BlockSpec(shape, lambda *_: (0,) * len(shape), pipeline_mode=pl.Buffered(1))


def _store_blocks_bf16(acc, out_hbm, stage, sem):
    for j in range(acc.shape[0]):
        stage[...] = acc[j].astype(BF16)
        copy = pltpu.make_async_copy(stage, out_hbm.at[j], sem)
        copy.start()
        copy.wait()


def ffn_bwd_hidden(dout, wd, gu, name, deps=()):
    T = dout.shape[0]
    tm = _tile(T, 512)
    n_t = T // tm

    def body(d_ref, w_ref, gu_ref, *rest):
        dgu_ref, dwd_hbm, acc, stage, sem = rest[-5:]
        t = pl.program_id(0)

        @pl.when(t == 0)
        def _():
            acc[...] = jnp.zeros_like(acc)
        dy = (0.5 * d_ref[...]).astype(BF16)
        for g in range(N_FB):
            gate, up = gu_ref[g], gu_ref[N_FB + g]
            sg = _sigmoid(gate)
            silu = gate * sg
            dact = _dot(dy, w_ref[g], NT).astype(BF16)
            acc[g] += _dot(silu * up, dy, TN)
            dgu_ref[g] = dact * up * (sg * (1.0 + gate * (1.0 - sg)))
            dgu_ref[N_FB + g] = dact * silu

        @pl.when(t == n_t - 1)
        def _():
            _store_blocks_bf16(acc, dwd_hbm, stage, sem)

    return _pc(body, name=name, out_shape=(_sds((2 * N_FB, T, FB), BF16), _sds((N_FB, FB, D), BF16)), grid=(n_t,),
               in_specs=[pl.BlockSpec((tm, D), lambda i: (i, 0)), _resident((N_FB, FB, D)),
                         pl.BlockSpec((2 * N_FB, tm, FB), lambda i: (0, i, 0))] + [DEP_SPEC] * len(deps),
               out_specs=(pl.BlockSpec((2 * N_FB, tm, FB), lambda i: (0, i, 0)), pl.BlockSpec(memory_space=pl.ANY)),
               scratch=[pltpu.VMEM((N_FB, FB, D), F32), pltpu.VMEM((FB, D), BF16), pltpu.SemaphoreType.DMA],
               sem=("arbitrary",))(dout, wd, gu, *deps)


def ffn_bwd_input(dgu, wgu, x, dout, nw, name, deps=()):
    T = x.shape[0]
    tm = _tile(T, 512)

    def body(dgu_ref, w_ref, x_ref, d_ref, nw_ref, *rest):
        dx_ref, dnw_ref = rest[-2:]
        dxn = jnp.zeros((tm, D), F32)
        for j in range(2 * N_FB):
            dxn = dxn + _dot(dgu_ref[j], w_ref[j], NT)
        dx, dw = _rms_bwd(x_ref[...], nw_ref[...], dxn)
        dx_ref[...] = d_ref[...] + dx

        @pl.when(pl.program_id(0) == 0)
        def _():
            dnw_ref[...] = jnp.zeros_like(dnw_ref)
        dnw_ref[...] += dw

    row = pl.BlockSpec((tm, D), lambda i: (i, 0))
    vec = pl.BlockSpec((1, D), lambda i: (0, 0))
    return _pc(body, name=name, out_shape=(_sds((T, D), F32), _sds((1, D), F32)), grid=(T // tm,),
               in_specs=[pl.BlockSpec((2 * N_FB, tm, FB), lambda i: (0, i, 0)), _resident((2 * N_FB, D, FB)),
                         row, row, vec] + [DEP_SPEC] * len(deps),
               out_specs=(row, vec), sem=("arbitrary",))(dgu, wgu, x, dout, nw, *deps)


def ffn_wgrad_gu(xn, dgu, name):
    T = xn.shape[0]
    tt = _tile(T, 1024)
    n_t = T // tt

    def body(x_ref, d_ref, dw_hbm, acc, stage, sem):
        t = pl.program_id(0)

        @pl.when(t == 0)
        def _():
            acc[...] = jnp.zeros_like(acc)
        xn_tile = x_ref[...]
        for j in range(2 * N_FB):
            acc[j] += _dot(xn_tile, d_ref[j], TN)

        @pl.when(t == n_t - 1)
        def _():
            _store_blocks_bf16(acc, dw_hbm, stage, sem)

    return _pc(body, name=name, out_shape=_sds((2 * N_FB, D, FB), BF16), grid=(n_t,),
               in_specs=[pl.BlockSpec((tt, D), lambda t: (t, 0)), pl.BlockSpec((2 * N_FB, tt, FB), lambda t: (0, t, 0))],
               out_specs=pl.BlockSpec(memory_space=pl.ANY),
               scratch=[pltpu.VMEM((2 * N_FB, D, FB), F32), pltpu.VMEM((D, FB), BF16), pltpu.SemaphoreType.DMA],
               sem=("arbitrary",))(xn, dgu)


def ffn_forward(x, nw, wgu, wd, tag):
    T = x.shape[0]
    tm = _tile(T, 512)

    def body(x_ref, nw_ref, wgu_ref, wd_ref, o_ref, xn_ref, gu_ref):
        xv = x_ref[...]
        xn = _rms_fwd(xv, nw_ref[...]).astype(BF16)
        xn_ref[...] = xn
        for j in range(2 * N_FB):
            gu_ref[j] = _dot(xn, wgu_ref[j]).astype(BF16)
        acc = jnp.zeros((tm, D), F32)
        for g in range(N_FB):
            gate, up = gu_ref[g], gu_ref[N_FB + g]
            acc = acc + _dot(gate * _sigmoid(gate) * up, wd_ref[g])
        o_ref[...] = xv + 0.5 * acc

    row = pl.BlockSpec((tm, D), lambda i: (i, 0))
    out, xn, gu = _pc(body, name=f"{tag}_fwd",
                      out_shape=(_sds((T, D), F32), _sds((T, D), BF16), _sds((2 * N_FB, T, FB), BF16)), grid=(T // tm,),
                      in_specs=[row, pl.BlockSpec((1, D), lambda i: (0, 0)), _resident((2 * N_FB, D, FB)),
                                _resident((N_FB, FB, D))],
                      out_specs=(row, row, pl.BlockSpec((2 * N_FB, tm, FB), lambda i: (0, i, 0))),
                      sem=("parallel",))(x, nw, wgu, wd)
    return out, (x, xn, gu)


def ffn_backward(dout, saved, nw, wgu, wd, tag, deps=(), on_grads=None):
    x, xn, gu = saved
    dgu, dwd = ffn_bwd_hidden(dout, wd, gu, f"{tag}_bwd_hidden", deps)
    dwgu = ffn_wgrad_gu(xn, dgu, f"{tag}_wgrad_gu")
    late = on_grads(dwgu, dwd) if on_grads else ()
    dx, dnw = ffn_bwd_input(dgu, wgu, x, dout, nw, f"{tag}_bwd_input", late)
    return dx, dnw, dwgu, dwd


N_QKV_BLK = 3 * HEADS_A
Z_BLK0 = N_QKV_BLK
MAIN_COLS = 4 * D
HALO = 16


def _conv_taps(xcat, w):
    c = xcat[HALO:] * w[3:4]
    for k in range(3):
        c = c + pltpu.roll(xcat, 3 - k, 0)[HALO:] * w[k:k + 1]
    return c


def _head_cols(h):
    return slice(128 * h, 128 * (h + 1))


def gdn_conv_fwd(proj, wconv, name):
    T = proj.shape[0]
    tm = _tile(T, 512)

    def body(cur_ref, prev_ref, w_ref, c_ref, y_ref):
        kind, t = pl.program_id(0), pl.program_id(1)
        prev = jnp.where(t > 0, prev_ref[...].astype(F32), 0.0)
        c = _conv_taps(jnp.concatenate([prev, cur_ref[...].astype(F32)], axis=0), w_ref[...])
        c_ref[...] = c.astype(BF16)
        s = c * _sigmoid(c)
        scale = jnp.where(kind == 0, DK ** -0.5, 1.0)
        for h in range(HEADS_A):
            sh = s[:, _head_cols(h)]
            r = lax.rsqrt(jnp.sum(sh * sh, axis=-1, keepdims=True) + EPS)
            y_ref[h] = (sh * jnp.where(kind < 2, r * scale, 1.0)).astype(BF16)

    return _pc(body, name=name, out_shape=(_sds((T, 3 * D), BF16), _sds((N_QKV_BLK, T, 128), BF16)),
               grid=(3, T // tm),
               in_specs=[pl.BlockSpec((tm, D), lambda kd, t: (t, kd)),
                         pl.BlockSpec((HALO, D), lambda kd, t: (jnp.maximum(t * (tm // HALO) - 1, 0), kd)),
                         pl.BlockSpec((4, D), lambda kd, t: (0, kd))],
               out_specs=(pl.BlockSpec((tm, D), lambda kd, t: (t, kd)),
                          pl.BlockSpec((HEADS_A, tm, 128), lambda kd, t: (kd, t, 0))),
               sem=("parallel", "parallel"))(proj, proj, wconv)


def gdn_conv_bwd(dqkv, c, proj, wconv, dproj, name):
    T = c.shape[0]
    tm = _tile(T, 512)
    n_t = T // tm

    def body(dy_ref, dyn_ref, c_ref, cn_ref, x_ref, w_ref, _, dx_ref, dw_ref):
        kind, t = pl.program_id(0), pl.program_id(1)
        scale = jnp.where(kind == 0, DK ** -0.5, 1.0)

        def act_bwd(dy, cv):
            sg = _sigmoid(cv)
            s = cv * sg
            parts = []
            for h in range(HEADS_A):
                sh, dyh = s[:, _head_cols(h)], dy[h]
                r = lax.rsqrt(jnp.sum(sh * sh, axis=-1, keepdims=True) + EPS)
                ds_norm = scale * r * (dyh - (r * r) * sh * jnp.sum(dyh * sh, axis=-1, keepdims=True))
                parts.append(jnp.where(kind < 2, ds_norm, dyh))
            return jnp.concatenate(parts, axis=1) * (sg * (1.0 + cv * (1.0 - sg)))

        w = w_ref[...]
        dcur = act_bwd(dy_ref[...].astype(F32), c_ref[...].astype(F32))
        dnext = jnp.where(t < n_t - 1, act_bwd(dyn_ref[...].astype(F32), cn_ref[...].astype(F32)), 0.0)
        dcat = jnp.concatenate([dcur, dnext], axis=0)
        xcur = x_ref[...].astype(F32)
        dx = dcur * w[3:4]
        rows = [None, None, None, jnp.sum(dcur * xcur, axis=0, keepdims=True)]
        for k in range(3):
            ahead = pltpu.roll(dcat, tm + HALO - (3 - k), 0)[:tm]
            dx = dx + ahead * w[k:k + 1]
            rows[k] = jnp.sum(ahead * xcur, axis=0, keepdims=True)
        dx_ref[...] = dx.astype(BF16)

        @pl.when(t == 0)
        def _():
            dw_ref[...] = jnp.zeros_like(dw_ref)
        dw_ref[...] += jnp.concatenate(rows, axis=0)

    def nxt(t):
        return jnp.minimum((t + 1) * (tm // HALO), T // HALO - 1)

    cur = pl.BlockSpec((tm, D), lambda kd, t: (t, kd))
    return _pc(body, name=name, out_shape=(_sds(dproj.shape, BF16), _sds((4, 3 * D), F32)), grid=(3, n_t),
               in_specs=[pl.BlockSpec((HEADS_A, tm, 128), lambda kd, t: (kd, t, 0)),
                         pl.BlockSpec((HEADS_A, HALO, 128), lambda kd, t: (kd, nxt(t), 0)),
                         cur, pl.BlockSpec((HALO, D), lambda kd, t: (nxt(t), kd)),
                         cur, pl.BlockSpec((4, D), lambda kd, t: (0, kd)), DEP_SPEC],
               out_specs=(cur, pl.BlockSpec((4, D), lambda kd, t: (0, kd))), input_output_aliases={6: 0},
               sem=("parallel", "arbitrary"))(dqkv, dqkv, c, c, proj, wconv, dproj)


def _chunk_masks(n):
    ri = lax.broadcasted_iota(jnp.int32, (n, n), 0)
    ci = lax.broadcasted_iota(jnp.int32, (n, n), 1)
    same = (ri // CHUNK) == (ci // CHUNK)
    return same & (ri >= ci), same & (ri <= ci)


def gdn_gate_fwd(ba, al, dtb, name):
    T = ba.shape[0]
    tg = _tile(T, PREP_T)

    def body(ba_ref, al_ref, dtb_ref, o_ref):
        x = ba_ref[...]
        lane = lax.broadcasted_iota(jnp.int32, x.shape, 1)
        is_a = (lane >= HEADS_A) & (lane < 2 * HEADS_A)
        g = jnp.where(is_a, -jnp.exp(al_ref[...]) * _softplus(x + dtb_ref[...]), 0.0)
        lower, _ = _chunk_masks(tg)
        gc = _dot(lower.astype(F32), g, precision=HI)
        o_ref[...] = jnp.where(lane < HEADS_A, _sigmoid(x), gc)

    vec = pl.BlockSpec((1, 128), lambda i: (0, 0))
    return _pc(body, name=name, out_shape=_sds((T, 128), F32), grid=(T // tg,),
               in_specs=[pl.BlockSpec((tg, 128), lambda i: (i, 0)), vec, vec],
               out_specs=pl.BlockSpec((tg, 128), lambda i: (i, 0)), sem=("parallel",))(ba, al, dtb)


def gdn_gate_bwd(ba, al, dtb, dgb, dproj, name):
    T = ba.shape[0]
    tg = _tile(T, PREP_T)

    def body(ba_ref, al_ref, dtb_ref, dgb_ref, _, dba_ref, dal_ref, ddt_ref):
        x, d = ba_ref[...], dgb_ref[...]
        lane = lax.broadcasted_iota(jnp.int32, x.shape, 1)
        is_b = lane < HEADS_A
        is_a = (lane >= HEADS_A) & (lane < 2 * HEADS_A)
        beta = _sigmoid(x)
        e_a = jnp.exp(al_ref[...])
        z = x + dtb_ref[...]
        g = jnp.where(is_a, -e_a * _softplus(z), 0.0)
        _, upper = _chunk_masks(tg)
        dg = _dot(upper.astype(F32), jnp.where(is_a, d, 0.0), precision=HI)
        da = jnp.where(is_a, dg * (-e_a) * _sigmoid(z), 0.0)
        db = jnp.where(is_b, d * beta * (1.0 - beta), 0.0)
        dba_ref[...] = (da + db).astype(BF16)

        @pl.when(pl.program_id(0) == 0)
        def _():
            dal_ref[...] = jnp.zeros_like(dal_ref)
            ddt_ref[...] = jnp.zeros_like(ddt_ref)
        dal_ref[...] += jnp.sum(dg * g, axis=0, keepdims=True)
        ddt_ref[...] += jnp.sum(da, axis=0, keepdims=True)

    vec = pl.BlockSpec((1, 128), lambda i: (0, 0))
    blk = pl.BlockSpec((tg, 128), lambda i: (i, 0))
    ba_cols = pl.BlockSpec((tg, 128), lambda i: (i, A_COLS // 128 - 1))
    return _pc(body, name=name, out_shape=(_sds(dproj.shape, BF16), _sds((1, 128), F32), _sds((1, 128), F32)),
               grid=(T // tg,), in_specs=[blk, vec, vec, blk, DEP_SPEC],
               out_specs=(ba_cols, vec, vec), input_output_aliases={4: 0}, sem=("arbitrary",))(ba, al, dtb, dgb, dproj)


def _bmm(a, b, dims, precision=None):
    return lax.dot_general(a, b, dims, preferred_element_type=F32, precision=precision)


B_NN = (((2,), (1,)), ((0,), (0,)))
B_NT = (((2,), (2,)), ((0,), (0,)))


def _select_lane(x, lane_index):
    lane = lax.broadcasted_iota(jnp.int32, x.shape, x.ndim - 1)
    return jnp.sum(jnp.where(lane == lane_index, x, 0.0), axis=-1, keepdims=True)


B_TN = (((1,), (1,)), ((0,), (0,)))


def _bmm_split(a, b, dims):
    ah, bh = a.astype(BF16), b.astype(BF16)
    al, bl = (a - ah.astype(F32)).astype(BF16), (b - bh.astype(F32)).astype(BF16)
    return _bmm(ah, bh, dims) + (_bmm(ah, bl, dims) + _bmm(al, bh, dims))


@jax.custom_vjp
def _bmm_f32(a, b):
    return _bmm_split(a, b, B_NN)


def _bmm_f32_fwd(a, b):
    return _bmm_split(a, b, B_NN), (a, b)


def _bmm_bf16(a, b, dims):
    return _bmm(a.astype(BF16), b.astype(BF16), dims)


def _bmm_f32_bwd(res, dc):
    a, b = res
    return _bmm_bf16(dc, b, B_NT), _bmm_bf16(a, dc, B_TN)


_bmm_f32.defvjp(_bmm_f32_fwd, _bmm_f32_bwd)


def _tri_inverse(lmat):
    ri = lax.broadcasted_iota(jnp.int32, lmat.shape, 1)
    ci = lax.broadcasted_iota(jnp.int32, lmat.shape, 2)
    eye = jnp.where(ri == ci, 1.0, 0.0)
    inv = eye - lmat
    power = lmat
    for _ in range(5):
        power = _bmm_bf16(power, power, B_NN)
        inv = inv + _bmm_bf16(inv, power, B_NN)
    return _bmm_split(inv, 2.0 * eye - _bmm_split(eye + lmat, inv, B_NN), B_NN)


def _stored_inverse(x):
    @jax.custom_vjp
    def inverse(lmat):
        return x

    def fwd(lmat):
        return x, None

    def bwd(_, dx):
        return (-_bmm_bf16(_bmm_bf16(x, dx, B_TN), x, B_NT),)

    inverse.defvjp(fwd, bwd)
    return inverse


def _gdn_prep(q, k, v, gb, h, inverse):
    nb = q.shape[0]
    beta = _select_lane(gb, h)
    gc = _select_lane(gb, HEADS_A + h)
    ri = lax.broadcasted_iota(jnp.int32, (nb, CHUNK, CHUNK), 1)
    ci = lax.broadcasted_iota(jnp.int32, (nb, CHUNK, CHUNK), 2)
    lower, strict, eye = ri >= ci, ri > ci, ri == ci
    gcol = jnp.broadcast_to(gc, (nb, CHUNK, CHUNK))
    grow = jnp.swapaxes(gcol, 1, 2)
    decay = jnp.where(lower, jnp.exp(jnp.where(lower, gcol - grow, 0.0)), 0.0)
    kb = k * beta
    kbf = k.astype(BF16)
    inv = inverse(jnp.where(strict, _bmm(kb.astype(BF16), kbf, B_NT) * decay, 0.0))
    eg = jnp.exp(gc)
    sol = _bmm_f32(inv, jnp.concatenate([v * beta, kb * eg], axis=-1))
    aqk = _bmm(q.astype(BF16), kbf, B_NT) * decay
    g_last = gc[:, CHUNK - 1:CHUNK, :]
    gl = jnp.broadcast_to(jnp.exp(g_last), (nb, 1, 128))
    return (sol[..., :DK], sol[..., DK:], q * eg, k * jnp.exp(g_last - gc), aqk, gl), inv


def gdn_prep_fwd(qkv, gb, name):
    T = qkv.shape[1]
    tp = _tile(T, 4 * PREP_T)
    nb = tp // CHUNK

    def body(q_ref, k_ref, v_ref, gb_ref, u_ref, w_ref, qd_ref, kd_ref, a_ref, gl_ref, inv_ref):
        h = pl.program_id(1)
        shp = (nb, CHUNK, 128)
        q, k, v = (ref[0].astype(F32).reshape(shp) for ref in (q_ref, k_ref, v_ref))
        (u, w, qd, kd, aqk, gl), inv = _gdn_prep(q, k, v, gb_ref[...].reshape(shp), h, _tri_inverse)
        u_ref[0] = u.reshape(tp, 128)
        w_ref[0] = w.reshape(tp, 128).astype(BF16)
        qd_ref[0] = qd.reshape(tp, 128).astype(BF16)
        kd_ref[0] = kd.reshape(tp, 128).astype(BF16)
        a_ref[0] = aqk.reshape(tp, CHUNK).astype(BF16)
        gl_ref[0] = gl.reshape(nb, 1, 128)
        inv_ref[0] = inv.reshape(tp, CHUNK)

    def head(off):
        return pl.BlockSpec((1, tp, 128), lambda n, h: (h + off, n, 0))

    matmul_only = _sds((HEADS_A, T, 128), BF16)
    narrow = pl.BlockSpec((1, tp, CHUNK), lambda n, h: (h, n, 0))
    return _pc(body, name=name,
               out_shape=(_sds((HEADS_A, T, 128), F32), matmul_only, matmul_only, matmul_only, _sds((HEADS_A, T, CHUNK), BF16),
                          _sds((HEADS_A, T // CHUNK, 1, 128), F32), _sds((HEADS_A, T, CHUNK), F32)),
               grid=(T // tp, HEADS_A),
               in_specs=[head(0), head(HEADS_A), head(2 * HEADS_A), pl.BlockSpec((tp, 128), lambda n, h: (n, 0))],
               out_specs=(head(0), head(0), head(0), head(0), narrow,
                          pl.BlockSpec((1, nb, 1, 128), lambda n, h: (h, n, 0, 0)), narrow),
               sem=("parallel", "parallel"))(qkv, qkv, qkv, gb)


def gdn_prep_bwd(qkv, gb, inv, du, dw, dqd, dkd, da, dgl, name):
    T = qkv.shape[1]
    tp = _tile(T, 2 * PREP_T)
    nb = tp // CHUNK

    def body(q_ref, k_ref, v_ref, gb_ref, inv_ref, du_ref, dw_ref, dqd_ref, dkd_ref, da_ref, dgl_ref, dqkv_ref, dgb_ref):
        h = pl.program_id(1)
        shp = (nb, CHUNK, 128)
        stored = _stored_inverse(inv_ref[0].reshape(nb, CHUNK, CHUNK))
        q, k, v = (ref[0].astype(F32).reshape(shp) for ref in (q_ref, k_ref, v_ref))
        _, vjp = jax.vjp(lambda q, k, v, gb: _gdn_prep(q, k, v, gb, h, stored)[0], q, k, v, gb_ref[...].reshape(shp))
        du, dw, dqd, dkd = (ref[0].astype(F32).reshape(shp) for ref in (du_ref, dw_ref, dqd_ref, dkd_ref))
        dq, dk, dv, dgb = vjp((du, dw, dqd, dkd, da_ref[0].astype(F32).reshape(nb, CHUNK, CHUNK),
                               dgl_ref[0].reshape(nb, 1, 128)))
        dqkv_ref[h] = dq.reshape(tp, 128).astype(BF16)
        dqkv_ref[HEADS_A + h] = dk.reshape(tp, 128).astype(BF16)
        dqkv_ref[2 * HEADS_A + h] = dv.reshape(tp, 128).astype(BF16)

        @pl.when(h == 0)
        def _():
            dgb_ref[...] = jnp.zeros_like(dgb_ref)
        dgb_ref[...] += dgb.reshape(tp, 128)

    def head(off):
        return pl.BlockSpec((1, tp, 128), lambda n, h: (h + off, n, 0))

    narrow = pl.BlockSpec((1, tp, CHUNK), lambda n, h: (h, n, 0))
    return _pc(body, name=name, out_shape=(_sds((N_QKV_BLK, T, 128), BF16), _sds((T, 128), F32)),
               grid=(T // tp, HEADS_A),
               in_specs=[head(0), head(HEADS_A), head(2 * HEADS_A), pl.BlockSpec((tp, 128), lambda n, h: (n, 0)), narrow,
                         head(0), head(0), head(0), head(0), narrow,
                         pl.BlockSpec((1, nb, 1, 128), lambda n, h: (h, n, 0, 0))],
               out_specs=(pl.BlockSpec((N_QKV_BLK, tp, 128), lambda n, h: (0, n, 0)),
                          pl.BlockSpec((tp, 128), lambda n, h: (n, 0))),
               sem=("parallel", "arbitrary"))(qkv, qkv, qkv, gb, inv, du, dw, dqd, dkd, da, dgl)


SCAN_CHUNKS = 4


def gdn_scan_fwd(u, w, qd, kd, aqk, gl, name):
    T = u.shape[1]
    n_chunks = T // CHUNK
    rows_per_step = SCAN_CHUNKS * CHUNK

    def body(u_ref, w_ref, qd_ref, kd_ref, a_ref, gl_ref, o_ref, sin_ref, state):
        @pl.when(pl.program_id(0) == 0)
        def _():
            state[...] = jnp.zeros_like(state)
        s = state[...]
        for c in range(SCAN_CHUNKS):
            rows = slice(c * CHUNK, (c + 1) * CHUNK)
            sb = s.astype(BF16)
            sin_ref[c] = sb
            both = _bmm(jnp.concatenate([w_ref[:, rows], qd_ref[:, rows]], axis=1), sb, B_NN)
            vn = (u_ref[:, rows] - both[:, :CHUNK]).astype(BF16)
            o_ref[:, rows] = both[:, CHUNK:] + _bmm(a_ref[:, rows], vn, B_NN)
            s = s * gl_ref[:, c] + _bmm(kd_ref[:, rows], vn, B_TN)
        state[...] = s

    blk = pl.BlockSpec((HEADS_A, rows_per_step, 128), lambda n: (0, n, 0))
    return _pc(body, name=name,
               out_shape=(_sds((HEADS_A, T, 128), F32), _sds((n_chunks, HEADS_A, DK, 128), BF16)),
               grid=(n_chunks // SCAN_CHUNKS,),
               in_specs=[blk, blk, blk, blk, pl.BlockSpec((HEADS_A, rows_per_step, CHUNK), lambda n: (0, n, 0)),
                         pl.BlockSpec((HEADS_A, SCAN_CHUNKS, 1, 128), lambda n: (0, n, 0, 0))],
               out_specs=(blk, pl.BlockSpec((SCAN_CHUNKS, HEADS_A, DK, 128), lambda n: (n, 0, 0, 0))),
               scratch=[pltpu.VMEM((HEADS_A, DK, 128), F32)], sem=("arbitrary",))(u, w, qd, kd, aqk, gl)


def gdn_scan_bwd(u, w, qd, kd, aqk, gl, sin, do, name):
    T = u.shape[1]
    n_chunks = T // CHUNK
    rows_per_step = SCAN_CHUNKS * CHUNK

    def body(u_ref, w_ref, qd_ref, kd_ref, a_ref, gl_ref, sin_ref, do_ref,
             du_ref, dw_ref, dqd_ref, dkd_ref, da_ref, dgl_ref, dstate):
        @pl.when(pl.program_id(0) == 0)
        def _():
            dstate[...] = jnp.zeros_like(dstate)
        lane0 = lax.broadcasted_iota(jnp.int32, (HEADS_A, 1, 128), 2) == 0
        ds_out = dstate[...]
        for c in reversed(range(SCAN_CHUNKS)):
            rows = slice(c * CHUNK, (c + 1) * CHUNK)
            sb = sin_ref[c]
            wb, qdb, kdb, ab, dob = w_ref[:, rows], qd_ref[:, rows], kd_ref[:, rows], a_ref[:, rows], do_ref[:, rows]
            vn = (u_ref[:, rows] - _bmm(wb, sb, B_NN)).astype(BF16)
            dsb = ds_out.astype(BF16)
            dqd_ref[:, rows] = _bmm(dob, sb, B_NT).astype(BF16)
            da_ref[:, rows] = _bmm(dob, vn, B_NT).astype(BF16)
            dvb = (_bmm(ab, dob, B_TN) + _bmm(kdb, dsb, B_NN)).astype(BF16)
            dkd_ref[:, rows] = _bmm(vn, dsb, B_NT).astype(BF16)
            dgl = jnp.sum(jnp.sum(ds_out * sb.astype(F32), axis=2, keepdims=True), axis=1, keepdims=True)
            dgl_ref[:, c] = jnp.where(lane0, dgl, 0.0)
            du_ref[:, rows] = dvb
            dw_ref[:, rows] = (-_bmm(dvb, sb, B_NT)).astype(BF16)
            ds_out = ds_out * gl_ref[:, c] + _bmm(qdb, dob, B_TN) - _bmm(wb, dvb, B_TN)
        dstate[...] = ds_out

    last = n_chunks // SCAN_CHUNKS - 1
    blk = pl.BlockSpec((HEADS_A, rows_per_step, 128), lambda n: (0, last - n, 0))
    ablk = pl.BlockSpec((HEADS_A, rows_per_step, CHUNK), lambda n: (0, last - n, 0))
    glblk = pl.BlockSpec((HEADS_A, SCAN_CHUNKS, 1, 128), lambda n: (0, last - n, 0, 0))
    per_head = _sds((HEADS_A, T, 128), BF16)
    return _pc(body, name=name,
               out_shape=(per_head, per_head, per_head, per_head, _sds((HEADS_A, T, CHUNK), BF16),
                          _sds((HEADS_A, n_chunks, 1, 128), F32)), grid=(n_chunks // SCAN_CHUNKS,),
               in_specs=[blk, blk, blk, blk, ablk, glblk,
                         pl.BlockSpec((SCAN_CHUNKS, HEADS_A, DK, 128), lambda n: (last - n, 0, 0, 0)), blk],
               out_specs=(blk, blk, blk, blk, ablk, glblk),
               scratch=[pltpu.VMEM((HEADS_A, DK, 128), F32)], sem=("arbitrary",))(u, w, qd, kd, aqk, gl, sin, do)


def gdn_outnorm_fwd(o, proj, wn, name):
    T = o.shape[1]
    tm = _tile(T, 512)

    def body(o_ref, z_ref, wn_ref, y_ref):
        for h in range(HEADS_A):
            z = z_ref[:, 128 * h:128 * (h + 1)].astype(F32)
            y_ref[:, 128 * h:128 * (h + 1)] = (_rms_fwd(o_ref[h], wn_ref[...]) * (z * _sigmoid(z))).astype(BF16)

    return _pc(body, name=name, out_shape=_sds((T, D), BF16), grid=(T // tm,),
               in_specs=[pl.BlockSpec((HEADS_A, tm, 128), lambda i: (0, i, 0)),
                         pl.BlockSpec((tm, D), lambda i: (i, Z_BLK0 * 128 // D)), pl.BlockSpec((1, 128), lambda i: (0, 0))],
               out_specs=pl.BlockSpec((tm, D), lambda i: (i, 0)), sem=("parallel",))(o, proj, wn)


def gdn_outnorm_bwd(o, proj, wn, dy, name):
    T = o.shape[1]
    tm = _tile(T, 512)

    def body(o_ref, z_ref, wn_ref, dy_ref, do_ref, dz_ref, dwn_ref):
        wn = wn_ref[...]
        acc = jnp.zeros((1, 128), F32)
        for h in range(HEADS_A):
            cols = slice(128 * h, 128 * (h + 1))
            z, dyh, ov = z_ref[:, cols].astype(F32), dy_ref[:, cols], o_ref[h]
            sg = _sigmoid(z)
            do, dwn = _rms_bwd(ov, wn, dyh * (z * sg))
            do_ref[h] = do.astype(BF16)
            acc = acc + dwn
            dz_ref[:, cols] = (dyh * _rms_fwd(ov, wn) * (sg * (1.0 + z * (1.0 - sg)))).astype(BF16)

        @pl.when(pl.program_id(0) == 0)
        def _():
            dwn_ref[...] = jnp.zeros_like(dwn_ref)
        dwn_ref[...] += acc

    row = pl.BlockSpec((tm, D), lambda i: (i, 0))
    vec = pl.BlockSpec((1, 128), lambda i: (0, 0))
    hblk = pl.BlockSpec((HEADS_A, tm, 128), lambda i: (0, i, 0))
    z_cols = pl.BlockSpec((tm, D), lambda i: (i, Z_BLK0 * 128 // D))
    return _pc(body, name=name, out_shape=(_sds((HEADS_A, T, 128), BF16), _sds((T, A_COLS), BF16), _sds((1, 128), F32)),
               grid=(T // tm,), in_specs=[hblk, z_cols, vec, row],
               out_specs=(hblk, z_cols, vec), sem=("arbitrary",))(o, proj, wn, dy)


def gdn_forward(x, nw, w_in, wconv, al, dtb, wn, w_out, tag, deps=()):
    h = rmsnorm_bf16(x, nw, f"{tag}_norm", deps)
    proj = mm_nn(h, w_in, f"{tag}_proj", out_dtype=BF16, cols=(0, MAIN_COLS))
    ba = mm_nn(h, w_in, f"{tag}_proj_ba", cols=(MAIN_COLS, A_COLS))
    c, qkv = gdn_conv_fwd(proj, wconv, f"{tag}_conv")
    gb = gdn_gate_fwd(ba, al, dtb, f"{tag}_gate")
    u, w, qd, kd, aqk, gl, inv = gdn_prep_fwd(qkv, gb, f"{tag}_prep")
    o, sin = gdn_scan_fwd(u, w, qd, kd, aqk, gl, f"{tag}_scan")
    on = gdn_outnorm_fwd(o, proj, wn, f"{tag}_outnorm")
    y = mm_nn(on, w_out, f"{tag}_out", residual=x)
    return y, (x, h, proj, ba, c, qkv, gb, inv, (u, w, qd, kd, aqk, gl), sin, o, on)


def gdn_backward(dout, saved, nw, w_in, wconv, al, dtb, wn, w_out, tag):
    x, h, proj, ba, c, qkv, gb, inv, prep, sin, o, on = saved
    d_on = mm_nt(dout, w_out, f"{tag}_out_bwd")
    dw_out = mm_tn(on, dout, f"{tag}_out_wgrad")
    do, dproj, dwn = gdn_outnorm_bwd(o, proj, wn, d_on, f"{tag}_outnorm_bwd")
    du, dw, dqd, dkd, da, dgl = gdn_scan_bwd(*prep, sin, do, f"{tag}_scan_bwd")
    dqkv, dgb = gdn_prep_bwd(qkv, gb, inv, du, dw, dqd, dkd, da, dgl, f"{tag}_prep_bwd")
    dproj, dal, ddt = gdn_gate_bwd(ba, al, dtb, dgb, dproj, f"{tag}_gate_bwd")
    dproj, dwconv = gdn_conv_bwd(dqkv, c, proj, wconv, dproj, f"{tag}_conv_bwd")
    dw_in = mm_tn(h, dproj, f"{tag}_proj_wgrad")
    dx, dnw = mm_nt(dproj, w_in, f"{tag}_proj_bwd", norm_bwd=(x, nw, dout))
    return dx, dnw, dw_in, dwconv, dal, ddt, dwn, dw_out


N_KV, GROUP = 4, 4
KV_COLS = 2 * N_KV * B_HD
B_COLS = D + KV_COLS


@jax.custom_vjp
def _swap_lane_halves(x):
    return pltpu.roll(x, 64, 1)


_swap_lane_halves.defvjp(lambda x: (pltpu.roll(x, 64, 1), None), lambda _, g: (pltpu.roll(g, 64, 1),))


def _swa_block(q, kp, kc, vp, vc, sk, first):
    cols = GROUP * B_BLK
    ks = lax.broadcasted_iota(jnp.int32, (N_KV, B_BLK, cols), 1)
    qi = lax.broadcasted_iota(jnp.int32, (N_KV, B_BLK, cols), 2) % B_BLK
    from_cur = ks <= qi

    def batch(parts):
        return jnp.concatenate([part[None] for part in parts], axis=0)

    def per_kv(cur, prev):
        return batch([jnp.concatenate([cur[:, j * B_HD:(j + 1) * B_HD], prev[:, j * B_HD:(j + 1) * B_HD]], axis=0)
                      for j in range(N_KV)]).astype(BF16)

    qs = batch([jnp.concatenate([q[:, hq * B_HD:(hq + 1) * B_HD] for hq in range(GROUP * j, GROUP * (j + 1))], axis=0)
                for j in range(N_KV)])
    q_t = jnp.swapaxes(qs, 1, 2).astype(BF16)
    sink = batch([jnp.concatenate([jnp.broadcast_to(sk[:, hq:hq + 1], (1, B_BLK))
                                   for hq in range(GROUP * j, GROUP * (j + 1))], axis=1) for j in range(N_KV)])
    both = _bmm(per_kv(kc, kp), q_t, B_NN)
    s = jnp.where(from_cur, both[:, :B_BLK], jnp.where(first, -1e30, both[:, B_BLK:])) * (B_HD ** -0.5)
    m = lax.stop_gradient(jnp.maximum(jnp.max(s, axis=1, keepdims=True), sink))
    e = jnp.exp((s - m).astype(BF16))
    den = jnp.sum(e.astype(F32), axis=1, keepdims=True) + jnp.exp(sink - m)
    p = e * (1.0 / den).astype(BF16)
    zero = jnp.zeros_like(p)
    p_both = jnp.concatenate([jnp.where(from_cur, p, zero), jnp.where(from_cur, zero, p)], axis=1)
    o = jnp.swapaxes(_bmm(per_kv(vc, vp), p_both, B_TN), 1, 2)
    return jnp.concatenate([o[j, g * B_BLK:(g + 1) * B_BLK] for j in range(N_KV) for g in range(GROUP)], axis=1)


def swa_core_fwd(proj, sk, name):
    T = proj.shape[0]
    half = N_KV * B_HD

    def body(q_ref, kvc_ref, kvp_ref, sk_ref, o_ref):
        kvc, kvp = kvc_ref[...], kvp_ref[...]
        o_ref[...] = _swa_block(q_ref[...], kvp[:, :half], kvc[:, :half], kvp[:, half:], kvc[:, half:], sk_ref[...],
                                pl.program_id(0) == 0).astype(BF16)

    return _pc(body, name=name, out_shape=_sds((T, D), BF16), grid=(T // B_BLK,),
               in_specs=[pl.BlockSpec((B_BLK, D), lambda n: (n, 0)),
                         pl.BlockSpec((B_BLK, KV_COLS), lambda n: (n, D // KV_COLS)),
                         pl.BlockSpec((B_BLK, KV_COLS), lambda n: (jnp.maximum(n - 1, 0), D // KV_COLS)),
                         pl.BlockSpec((1, 128), lambda n: (0, 0))],
               out_specs=pl.BlockSpec((B_BLK, D), lambda n: (n, 0)), sem=("parallel",))(proj, proj, proj, sk)


def swa_core_bwd(proj, sk, do, name):
    T = proj.shape[0]
    last = T // B_BLK - 1
    half = N_KV * B_HD

    def body(q_ref, kvc_ref, kvp_ref, sk_ref, do_ref, dproj_ref, dbias_ref, dsk_ref, carry):
        step = pl.program_id(0)
        first = step == last

        @pl.when(step == 0)
        def _():
            carry[...] = jnp.zeros_like(carry)
            dbias_ref[...] = jnp.zeros_like(dbias_ref)
            dsk_ref[...] = jnp.zeros_like(dsk_ref)
        kvc, kvp = kvc_ref[...], kvp_ref[...]
        _, vjp = jax.vjp(functools.partial(_swa_block, first=first), q_ref[...], kvp[:, :half], kvc[:, :half],
                         kvp[:, half:], kvc[:, half:], sk_ref[...])
        dq, dkp, dkc, dvp, dvc, dsk = vjp(do_ref[...])
        dkv = jnp.concatenate([dkc, dvc], axis=1) + carry[...]
        carry[...] = jnp.concatenate([dkp, dvp], axis=1)
        row = jnp.concatenate([dq, dkv], axis=1)
        dproj_ref[...] = row.astype(BF16)
        dbias_ref[...] += jnp.sum(row, axis=0, keepdims=True)
        dsk_ref[...] += dsk

    return _pc(body, name=name, out_shape=(_sds((T, B_COLS), BF16), _sds((1, B_COLS), F32), _sds((1, 128), F32)),
               grid=(T // B_BLK,),
               in_specs=[pl.BlockSpec((B_BLK, D), lambda n: (last - n, 0)),
                         pl.BlockSpec((B_BLK, KV_COLS), lambda n: (last - n, D // KV_COLS)),
                         pl.BlockSpec((B_BLK, KV_COLS), lambda n: (jnp.maximum(last - n - 1, 0), D // KV_COLS)),
                         pl.BlockSpec((1, 128), lambda n: (0, 0)), pl.BlockSpec((B_BLK, D), lambda n: (last - n, 0))],
               out_specs=(pl.BlockSpec((B_BLK, B_COLS), lambda n: (last - n, 0)),
                          pl.BlockSpec((1, B_COLS), lambda n: (0, 0)), pl.BlockSpec((1, 128), lambda n: (0, 0))),
               scratch=[pltpu.VMEM((B_BLK, KV_COLS), F32)], sem=("arbitrary",))(proj, proj, proj, sk, do)


def col_sum(a, name):
    T, N = a.shape
    tm = _tile(T, 1024)

    def body(a_ref, o_ref):
        @pl.when(pl.program_id(0) == 0)
        def _():
            o_ref[...] = jnp.zeros_like(o_ref)
        o_ref[...] += jnp.sum(a_ref[...].astype(F32), axis=0, keepdims=True)

    return _pc(body, name=name, out_shape=_sds((1, N), F32), grid=(T // tm,),
               in_specs=[pl.BlockSpec((tm, N), lambda i: (i, 0))], out_specs=pl.BlockSpec((1, N), lambda i: (0, 0)),
               sem=("arbitrary",))(a)


def swa_forward(x, nw, w_in, b_in, sk, w_out, b_out, tag):
    h = rmsnorm_bf16(x, nw, f"{tag}_norm")
    proj = mm_nn(h, w_in, f"{tag}_proj", bias=b_in)
    o = swa_core_fwd(proj, sk, f"{tag}_core")
    y = mm_nn(o, w_out, f"{tag}_out", bias=b_out, residual=x)
    return y, (x, h, proj, o)


def swa_backward(dout, saved, nw, w_in, b_in, sk, w_out, b_out, tag):
    x, h, proj, o = saved
    do = mm_nt(dout, w_out, f"{tag}_out_bwd")
    dw_out = mm_tn(o, dout, f"{tag}_out_wgrad")
    db_out = col_sum(dout, f"{tag}_out_bias_grad")
    dproj, db_in, dsk = swa_core_bwd(proj, sk, do, f"{tag}_core_bwd")
    dw_in = mm_tn(h, dproj, f"{tag}_proj_wgrad")
    dx, dnw = mm_nt(dproj, w_in, f"{tag}_proj_bwd", norm_bwd=(x, nw, dout))
    return dx, dnw, dw_in, db_in, dsk, dw_out, db_out


MESH = pl.DeviceIdType.MESH


def _position():
    return lax.axis_index("x"), lax.axis_index("y"), lax.axis_index("c")


def _slot(x, y, c):
    return 4 * x + 2 * y + c


def _peer(x, y, c, k):
    return (1 - x if k & 4 else x, 1 - y if k & 2 else y, 1 - c if k & 1 else c)


HBM_SPEC = pl.BlockSpec(memory_space=pltpu.HBM)
SEM_SPEC = pl.BlockSpec(memory_space=pltpu.SEMAPHORE)
DEP_SPEC = pl.BlockSpec(memory_space=pl.ANY)
SIDE_EFFECT = pltpu.SideEffectType.DATAFLOW_SIDE_EFFECTING
N_PEERS = N_DEV - 1


def _push_copies(srcs, lands, send_sems, recv_sems, scatter):
    x, y, c = _position()
    me = _slot(x, y, c)
    copies = []
    for k in (1, 2, 4, 3, 5, 6, 7):
        peer = _peer(x, y, c, k)
        for a in range(len(srcs)):
            copies.append(pltpu.make_async_remote_copy(
                src_ref=srcs[a].at[_slot(*peer)] if scatter else srcs[a], dst_ref=lands[a].at[me],
                send_sem=send_sems.at[N_PEERS * a + k - 1], recv_sem=recv_sems.at[N_PEERS * a + k - 1],
                device_id=peer, device_id_type=MESH))
    return copies


def push_start(srcs, lands, name, scatter, deps=()):
    n = len(srcs)
    first_out = 2 * n + len(deps)

    def body(*refs):
        for cp in _push_copies(refs[:n], refs[n:2 * n], refs[first_out], refs[first_out + 1], scatter):
            cp.start()
        refs[-1][...] = jnp.zeros_like(refs[-1])

    passed = [pltpu.HBM(t.shape, t.dtype) for t in list(srcs) + list(lands)]
    res = pl.pallas_call(
        body, name=name,
        out_shape=(pltpu.SemaphoreType.DMA((N_PEERS * n,)), pltpu.SemaphoreType.DMA((N_PEERS * n,)), *passed, _sds((8, 128), F32)),
        in_specs=[HBM_SPEC] * (2 * n) + [DEP_SPEC] * len(deps),
        out_specs=(SEM_SPEC, SEM_SPEC, *([HBM_SPEC] * (2 * n)), pl.BlockSpec(memory_space=pltpu.VMEM)),
        input_output_aliases={i: 2 + i for i in range(2 * n)},
        compiler_params=pltpu.CompilerParams(has_side_effects=SIDE_EFFECT),
    )(*[pltpu.with_memory_space_constraint(t, pltpu.HBM) for t in list(srcs) + list(lands)], *deps)
    return (res[0], res[1], list(res[2:2 + n]), list(res[2 + n:2 + 2 * n])), res[-1]


def push_wait(handles, after, name, scatter):
    send_sems, recv_sems, srcs, lands = handles
    n = len(srcs)
    after = tuple(after) if isinstance(after, (tuple, list)) else (after,)

    def body(*refs):
        for cp in _push_copies(refs[:n], refs[n:2 * n], refs[2 * n], refs[2 * n + 1], scatter):
            cp.wait_send()
            cp.wait_recv()

    res = pl.pallas_call(
        body, name=name, out_shape=tuple(pltpu.HBM(t.shape, t.dtype) for t in srcs + lands),
        in_specs=[HBM_SPEC] * (2 * n) + [SEM_SPEC, SEM_SPEC] + [DEP_SPEC] * len(after), out_specs=tuple([HBM_SPEC] * (2 * n)),
        input_output_aliases={i: i for i in range(2 * n)},
        compiler_params=pltpu.CompilerParams(has_side_effects=SIDE_EFFECT),
    )(*srcs, *lands, send_sems, recv_sems, *after)
    return list(res[n:])


def gather_start(shards, name, deps=()):
    me = _slot(*_position())
    lands = [lax.dynamic_update_slice(lax.empty((N_DEV,) + t.shape, t.dtype), t[None], (me,) + (0,) * t.ndim) for t in shards]
    return push_start(shards, lands, name, scatter=False, deps=deps)


def exchange_start(parts, name):
    me = _slot(*_position())
    lands = [lax.dynamic_update_slice(lax.empty(t.shape, t.dtype), lax.dynamic_index_in_dim(t, me, 0, keepdims=True),
                                      (me,) + (0,) * (t.ndim - 1)) for t in parts]
    return push_start(parts, lands, name, scatter=True)


def _row_tile(rows, cols):
    best = rows
    for t in range(16, rows, 16):
        if rows % t == 0 and t * cols * 4 <= (1 << 20):
            best = t
    return best


def adam_update(parts, w, m, v, name):
    n_layers = len(parts)
    P, R, C = parts[0].shape
    tr = _row_tile(R, C)
    n_t = R // tr

    def body(*refs):
        p_refs = refs[:n_layers]
        w_ref, m_ref, v_ref, g_ref, d_ref, nm_ref, nv_ref = refs[n_layers:]
        for layer in range(n_layers):
            @pl.when(pl.program_id(0) == layer)
            def _(p_ref=p_refs[layer]):
                g = p_ref[0].astype(F32)
                for s in range(1, P):
                    g = g + p_ref[s].astype(F32)
                new_m = ADAM_B1 * m_ref[0] + (1.0 - ADAM_B1) * g
                new_v = ADAM_B2 * v_ref[0] + (1.0 - ADAM_B2) * (g * g)
                m_hat = new_m / (1.0 - ADAM_B1 ** ADAM_STEP)
                v_hat = new_v / (1.0 - ADAM_B2 ** ADAM_STEP)
                g_ref[0] = g
                d_ref[0] = -ADAM_LR * (m_hat / (jnp.sqrt(v_hat) + ADAM_EPS) + ADAM_WD * w_ref[0])
                nm_ref[0] = new_m
                nv_ref[0] = new_v

    def part_spec(layer):
        return pl.BlockSpec((P, tr, C), lambda l_, i: (0, jnp.where(l_ == layer, i, jnp.where(l_ < layer, 0, n_t - 1)), 0))

    blk = pl.BlockSpec((1, tr, C), lambda l_, i: (l_, i, 0))
    out = _sds((n_layers, R, C), F32)
    return _pc(body, name=name, out_shape=(out, out, out, out), grid=(n_layers, n_t),
               in_specs=[part_spec(layer) for layer in range(n_layers)] + [blk, blk, blk],
               out_specs=(blk, blk, blk, blk), sem=("arbitrary", "arbitrary"))(*parts, w, m, v)


WEIGHTS = ("ffn1_norm", "ffn1_w_gu", "ffn1_w_down", "mix_norm", "ffn2_norm", "ffn2_w_gu", "ffn2_w_down", "a_w_in",
           "a_w_conv", "a_A_log", "a_dt_bias", "a_out_norm", "a_w_out", "b_w_in", "b_b_in", "b_sinks", "b_w_out",
           "b_b_out", "final_norm")
SHARDED = ("ffn1_w_gu", "ffn1_w_down", "ffn2_w_gu", "ffn2_w_down", "a_w_in", "a_w_conv", "a_w_out", "b_w_in", "b_b_in",
           "b_w_out", "b_b_out")
MISC_LANES = dict(a_A_log=(0, 8), a_dt_bias=(8, 16), b_sinks=(16, 32), a_out_norm=(128, 256))
LOSS_LANE = 256


def _pack_small(t):
    misc = jnp.zeros((D,), F32)
    for key, (lo, hi) in MISC_LANES.items():
        misc = misc.at[lo:hi].set(t[key].reshape(-1))
    if "loss" in t:
        misc = misc.at[LOSS_LANE].set(t["loss"])
    return jnp.concatenate([t["ffn1_norm"], t["mix_norm"], t["ffn2_norm"], t["final_norm"].reshape(1, D), misc[None]], axis=0)


def _unpack_small(p, like):
    out = dict(ffn1_norm=p[0:2], mix_norm=p[2:4], ffn2_norm=p[4:6], final_norm=p[6])
    for key, (lo, hi) in MISC_LANES.items():
        out[key] = p[7, lo:hi].reshape(like[key].shape)
    return out


def kernel(x, ffn1_norm, ffn1_w_gu, ffn1_w_down, mix_norm, ffn2_norm, ffn2_w_gu, ffn2_w_down, a_w_in, a_w_conv, a_A_log, a_dt_bias, a_out_norm, a_w_out, b_w_in, b_b_in, b_sinks, b_w_out, b_b_out, final_norm, loss_target, m_ffn1_norm, m_ffn1_w_gu, m_ffn1_w_down, m_mix_norm, m_ffn2_norm, m_ffn2_w_gu, m_ffn2_w_down, m_a_w_in, m_a_w_conv, m_a_A_log, m_a_dt_bias, m_a_out_norm, m_a_w_out, m_b_w_in, m_b_b_in, m_b_sinks, m_b_w_out, m_b_b_out, m_final_norm, v_ffn1_norm, v_ffn1_w_gu, v_ffn1_w_down, v_mix_norm, v_ffn2_norm, v_ffn2_w_gu, v_ffn2_w_down, v_a_w_in, v_a_w_conv, v_a_A_log, v_a_dt_bias, v_a_out_norm, v_a_w_out, v_b_w_in, v_b_b_in, v_b_sinks, v_b_w_out, v_b_b_out, v_final_norm):
    w = dict(ffn1_norm=ffn1_norm, ffn1_w_gu=ffn1_w_gu, ffn1_w_down=ffn1_w_down, mix_norm=mix_norm, ffn2_norm=ffn2_norm, ffn2_w_gu=ffn2_w_gu, ffn2_w_down=ffn2_w_down, a_w_in=a_w_in, a_w_conv=a_w_conv, a_A_log=a_A_log, a_dt_bias=a_dt_bias, a_out_norm=a_out_norm, a_w_out=a_w_out, b_w_in=b_w_in, b_b_in=b_b_in, b_sinks=b_sinks, b_w_out=b_w_out, b_b_out=b_b_out, final_norm=final_norm)
    m = dict(ffn1_norm=m_ffn1_norm, ffn1_w_gu=m_ffn1_w_gu, ffn1_w_down=m_ffn1_w_down, mix_norm=m_mix_norm, ffn2_norm=m_ffn2_norm, ffn2_w_gu=m_ffn2_w_gu, ffn2_w_down=m_ffn2_w_down, a_w_in=m_a_w_in, a_w_conv=m_a_w_conv, a_A_log=m_a_A_log, a_dt_bias=m_a_dt_bias, a_out_norm=m_a_out_norm, a_w_out=m_a_w_out, b_w_in=m_b_w_in, b_b_in=m_b_b_in, b_sinks=m_b_sinks, b_w_out=m_b_w_out, b_b_out=m_b_b_out, final_norm=m_final_norm)
    v = dict(ffn1_norm=v_ffn1_norm, ffn1_w_gu=v_ffn1_w_gu, ffn1_w_down=v_ffn1_w_down, mix_norm=v_mix_norm, ffn2_norm=v_ffn2_norm, ffn2_w_gu=v_ffn2_w_gu, ffn2_w_down=v_ffn2_w_down, a_w_in=v_a_w_in, a_w_conv=v_a_w_conv, a_A_log=v_a_A_log, a_dt_bias=v_a_dt_bias, a_out_norm=v_a_out_norm, a_w_out=v_a_w_out, b_w_in=v_b_w_in, b_b_in=v_b_b_in, b_sinks=v_b_sinks, b_w_out=v_b_w_out, b_b_out=v_b_b_out, final_norm=v_final_norm)
    T = x.shape[1]
    x0, tgt = x.reshape(T, D), loss_target.reshape(T, D)

    def cast(t):
        return t.astype(BF16)

    h0, t0 = gather_start([cast(ffn1_w_gu[0])], "gather0_start")
    a_log_row = jnp.zeros((1, 128), F32).at[0, HEADS_A:2 * HEADS_A].set(a_A_log[0])
    dt_row = jnp.zeros((1, 128), F32).at[0, HEADS_A:2 * HEADS_A].set(a_dt_bias[0])
    sink_row = jnp.zeros((1, 128), F32).at[0, :b_sinks.shape[1]].set(b_sinks[0])
    a_in_cols = a_w_in.shape[-1] * N_DEV

    def down_blocks(t):
        return t.reshape(N_FB, FB, D)

    wgu, wdn, saved = {}, {}, []
    xn = rmsnorm_bf16(x0, ffn1_norm[0:1], "l0_ffn1_norm", (t0,))
    wgu["ffn1", 0] = push_wait(h0, xn, "gather0_wait", scatter=False)[0]
    h0d, t0d = gather_start([cast(ffn1_w_down[0])], "gather0d_start", deps=(wgu["ffn1", 0],))
    h1, t1 = gather_start([cast(a_w_in[0]), a_w_conv[0], cast(a_w_out[0])], "gather1_start", deps=(t0d,))
    gu = ffn_up(xn, wgu["ffn1", 0], "l0_ffn1_up", deps=(t0d, t1))
    wdn["ffn1", 0] = down_blocks(push_wait(h0d, gu, "gather0d_wait", scatter=False)[0])
    xs, s1 = ffn_down(gu, wdn["ffn1", 0], x0, "l0_ffn1_down"), (x0, xn, gu)
    got = push_wait(h1, xs, "gather1_wait", scatter=False)
    h1f, t1f = gather_start([cast(ffn2_w_gu[0]), cast(ffn2_w_down[0])], "gather1f_start", deps=(got[0],))
    g2 = [cast(ffn1_w_gu[1]), cast(ffn1_w_down[1]), cast(b_w_in[0]), b_b_in, cast(b_w_out[0]), b_b_out,
          cast(ffn2_w_gu[1]), cast(ffn2_w_down[1])]
    h2, t2 = gather_start(g2, "gather2_start", deps=(t1f,))
    a_in_full = jnp.pad(got[0].transpose(1, 0, 2).reshape(D, a_in_cols), ((0, 0), (0, A_COLS - a_in_cols)))
    gdn_args = (mix_norm[0:1], a_in_full, got[1].transpose(1, 0, 2).reshape(4, 3 * D), a_log_row, dt_row, a_out_norm,
                got[2].reshape(D, D))
    xs, sm = gdn_forward(xs, *gdn_args, "gdn", deps=(t1f, t2))
    got = push_wait(h1f, xs, "gather1f_wait", scatter=False)
    wgu["ffn2", 0], wdn["ffn2", 0] = got[0], down_blocks(got[1])
    xs, s2 = ffn_forward(xs, ffn2_norm[0:1], wgu["ffn2", 0], wdn["ffn2", 0], "l0_ffn2")
    saved.append((s1, sm, s2))
    got = push_wait(h2, xs, "gather2_wait", scatter=False)
    wgu["ffn1", 1], wdn["ffn1", 1] = got[0], down_blocks(got[1])
    swa_args = (mix_norm[1:2], got[2].transpose(1, 0, 2).reshape(D, B_COLS), got[3].reshape(1, B_COLS), sink_row,
                got[4].reshape(D, D), got[5].reshape(1, D))
    wgu["ffn2", 1], wdn["ffn2", 1] = got[6], down_blocks(got[7])
    xs, s1 = ffn_forward(xs, ffn1_norm[1:2], wgu["ffn1", 1], wdn["ffn1", 1], "l1_ffn1")
    xs, sm = swa_forward(xs, *swa_args, "swa")
    xs, s2 = ffn_forward(xs, ffn2_norm[1:2], wgu["ffn2", 1], wdn["ffn2", 1], "l1_ffn2")
    saved.append((s1, sm, s2))
    loss_row, dx, d_final_norm = final_loss(xs, final_norm.reshape(1, D), tgt, "final_loss")

    def down_slots(t):
        return cast(t.reshape(N_DEV, FB // 2, D))

    def col_slots(t, dtype=BF16):
        return t.reshape(t.shape[0], N_DEV, -1).transpose(1, 0, 2).astype(dtype)

    d_norm = {"ffn1_norm": [None, None], "mix_norm": [None, None], "ffn2_norm": [None, None]}
    exchanges = {}

    def sender(tag):
        def on_grads(d_gu, d_dn):
            exchanges[tag], token = exchange_start([cast(d_gu), down_slots(d_dn)], f"exchange_{tag}_start")
            return (token,)
        return on_grads

    s1, sm, s2 = saved[1]
    dx, d_norm["ffn2_norm"][1], _, _ = ffn_backward(dx, s2, ffn2_norm[1:2], wgu["ffn2", 1], wdn["ffn2", 1], "l1_ffn2",
                                                    on_grads=sender("l1_ffn2"))
    dx, d_norm["mix_norm"][1], d_b_in, d_b_bias_in, d_sinks, d_b_out, d_b_bias_out = swa_backward(dx, sm, *swa_args, "swa")
    exchanges["swa"], t_swa = exchange_start(
        [col_slots(d_b_in), d_b_bias_in.reshape(N_DEV, 1, -1), cast(d_b_out.reshape(N_DEV, D // N_DEV, D)),
         d_b_bias_out.reshape(N_DEV, 1, -1)], "exchange_swa_start")
    dx, d_norm["ffn1_norm"][1], _, _ = ffn_backward(dx, s1, ffn1_norm[1:2], wgu["ffn1", 1], wdn["ffn1", 1], "l1_ffn1",
                                                    deps=(t_swa,), on_grads=sender("l1_ffn1"))

    s1, sm, s2 = saved[0]
    dx, d_norm["ffn2_norm"][0], _, _ = ffn_backward(dx, s2, ffn2_norm[0:1], wgu["ffn2", 0], wdn["ffn2", 0], "l0_ffn2",
                                                    on_grads=sender("l0_ffn2"))
    dx, d_norm["mix_norm"][0], d_a_in, d_a_conv, d_alog, d_dt, d_onorm, d_a_out = gdn_backward(dx, sm, *gdn_args, "gdn")
    exchanges["gdn"], t_gdn = exchange_start(
        [col_slots(d_a_in[:, :a_in_cols]), col_slots(d_a_conv, F32), cast(d_a_out.reshape(N_DEV, D // N_DEV, D))],
        "exchange_gdn_start")
    dx, d_norm["ffn1_norm"][0], _, _ = ffn_backward(dx, s1, ffn1_norm[0:1], wgu["ffn1", 0], wdn["ffn1", 0], "l0_ffn1",
                                                    deps=(t_gdn,), on_grads=sender("l0_ffn1"))
    grad_x = dx.reshape(x.shape)
    got = {tag: push_wait(exchanges[tag], dx, f"exchange_{tag}_wait", scatter=True)
           for tag in ("l1_ffn2", "swa", "l1_ffn1", "l0_ffn2", "gdn")}
    received = dict(ffn2_w_gu=[got["l0_ffn2"][0], got["l1_ffn2"][0]], ffn2_w_down=[got["l0_ffn2"][1], got["l1_ffn2"][1]],
                    b_w_in=[got["swa"][0]], b_b_in=[got["swa"][1]], b_w_out=[got["swa"][2]], b_b_out=[got["swa"][3]],
                    a_w_in=[got["gdn"][0]], a_w_conv=[got["gdn"][1]], a_w_out=[got["gdn"][2]])

    grads, deltas, new_m, new_v = {}, {}, {}, {}

    def update(key):
        shape = w[key].shape
        cols = shape[-1]
        layers = lambda t: t.reshape(shape[0], -1, cols)
        out = adam_update([r.reshape(N_DEV, -1, cols) for r in received[key]], layers(w[key]), layers(m[key]), layers(v[key]),
                          f"adam_{key}")
        grads[key], deltas[key], new_m[key], new_v[key] = (t.reshape(shape) for t in out)

    for key in SHARDED:
        if key in received:
            update(key)
    done_first = [deltas[key] for key in received]

    small = dict(ffn1_norm=jnp.concatenate(d_norm["ffn1_norm"], axis=0), mix_norm=jnp.concatenate(d_norm["mix_norm"], axis=0),
                 ffn2_norm=jnp.concatenate(d_norm["ffn2_norm"], axis=0), final_norm=d_final_norm,
                 a_A_log=d_alog[0, HEADS_A:2 * HEADS_A], a_dt_bias=d_dt[0, HEADS_A:2 * HEADS_A],
                 b_sinks=d_sinks[0, :b_sinks.shape[1]], a_out_norm=d_onorm, loss=loss_row[0, 0])
    hs, ts = gather_start([_pack_small(small)], "gather_small_start")
    r3 = push_wait(exchanges["l0_ffn1"], done_first + [ts], "exchange_l0_ffn1_wait", scatter=True)
    received.update(ffn1_w_gu=[r3[0], got["l1_ffn1"][0]], ffn1_w_down=[r3[1], got["l1_ffn1"][1]])
    update("ffn1_w_gu")
    update("ffn1_w_down")
    every = push_wait(hs, deltas["ffn1_w_down"], "gather_small_wait", scatter=False)[0]
    out = adam_update([every], _pack_small(w)[None], _pack_small(m)[None], _pack_small(v)[None], "adam_small")
    for dst, packed in zip((grads, deltas, new_m, new_v), out):
        dst.update(_unpack_small(packed[0], w))
    loss = out[0][0, 7, LOSS_LANE]

    return (loss, grad_x, *[grads[k_] for k_ in WEIGHTS], *[deltas[k_] for k_ in WEIGHTS],
            *[new_m[k_] for k_ in WEIGHTS], *[new_v[k_] for k_ in WEIGHTS])
```

```python
import functools

import jax
import jax.numpy as jnp
from jax import lax
from jax.experimental import pallas as pl
from jax.experimental.pallas import tpu as pltpu

F32, BF16 = jnp.float32, jnp.bfloat16
HI = lax.Precision.HIGHEST
EPS = 1e-6

N_DEV = 8
D = 1024
FB = 704
N_FB = 4
HEADS_A, DK = 8, 128
CHUNK = 64
PREP_T = 512
A_COLS = 4224
B_HD, B_BLK = 64, 128
VMEM_LIMIT_V7X = 60 * 1024 * 1024

ADAM_LR, ADAM_B1, ADAM_B2, ADAM_EPS, ADAM_WD, ADAM_STEP = 0.001, 0.9, 0.999, 1e-08, 0.01, 10

NT = (((1,), (1,)), ((), ()))
TN = (((0,), (0,)), ((), ()))


def _pc(body, *, name, out_shape, grid=(), in_specs=None, out_specs=None, scratch=(), sem=None, **kw):
    params = pltpu.CompilerParams(dimension_semantics=sem, vmem_limit_bytes=VMEM_LIMIT_V7X)
    return pl.pallas_call(body, name=name, out_shape=out_shape, grid=grid, in_specs=in_specs, out_specs=out_specs,
                          scratch_shapes=list(scratch), compiler_params=params, **kw)


def _sds(shape, dtype):
    return jax.ShapeDtypeStruct(tuple(shape), dtype)


def _dot(a, b, dims=None, precision=None):
    if dims is None:
        return jnp.dot(a, b, preferred_element_type=F32, precision=precision)
    return lax.dot_general(a, b, dims, preferred_element_type=F32, precision=precision)


def _sigmoid(x):
    return 1.0 / (1.0 + jnp.exp(-x))


def _softplus(x):
    return jnp.maximum(x, 0.0) + jnp.log(1.0 + jnp.exp(-jnp.abs(x)))


def _rms_fwd(x, w):
    r = lax.rsqrt(jnp.mean(x * x, axis=-1, keepdims=True) + EPS)
    return x * r * w


def _rms_bwd(x, w, dy):
    r = lax.rsqrt(jnp.mean(x * x, axis=-1, keepdims=True) + EPS)
    xh = x * r
    dxh = dy * w
    dx = r * (dxh - xh * jnp.mean(dxh * xh, axis=-1, keepdims=True))
    return dx, jnp.sum(dy * xh, axis=0, keepdims=True)


def _tile(n, want):
    t = min(n, want)
    assert n % t == 0, (n, want)
    return t


def rmsnorm_bf16(x, w, name, deps=()):
    T = x.shape[0]
    tm = _tile(T, 1024)

    def body(x_ref, w_ref, *rest):
        rest[-1][...] = _rms_fwd(x_ref[...], w_ref[...]).astype(BF16)

    return _pc(body, name=name, out_shape=_sds((T, D), BF16), grid=(T // tm,),
               in_specs=[pl.BlockSpec((tm, D), lambda i: (i, 0)), pl.BlockSpec((1, D), lambda i: (0, 0))] + [DEP_SPEC] * len(deps),
               out_specs=pl.BlockSpec((tm, D), lambda i: (i, 0)), sem=("parallel",))(x, w, *deps)


def final_loss(x, w, tgt, name):
    T = x.shape[0]
    tm = _tile(T, 512)

    def body(x_ref, w_ref, t_ref, loss_ref, dx_ref, dw_ref):
        xv, wv = x_ref[...], w_ref[...]
        err = _rms_fwd(xv, wv) - t_ref[...]
        dx, dw = _rms_bwd(xv, wv, err * (1.0 / D))
        dx_ref[...] = dx

        @pl.when(pl.program_id(0) == 0)
        def _():
            dw_ref[...] = jnp.zeros_like(dw_ref)
            loss_ref[...] = jnp.zeros_like(loss_ref)
        dw_ref[...] += dw
        loss_ref[...] += jnp.full((1, 128), 0.5 / D, F32) * jnp.sum(err * err)

    row = pl.BlockSpec((tm, D), lambda i: (i, 0))
    vec = pl.BlockSpec((1, D), lambda i: (0, 0))
    return _pc(body, name=name, out_shape=(_sds((1, 128), F32), _sds((T, D), F32), _sds((1, D), F32)),
               grid=(T // tm,), in_specs=[row, vec, row],
               out_specs=(pl.BlockSpec((1, 128), lambda i: (0, 0)), row, vec), sem=("arbitrary",))(x, w, tgt)


def _col_tile(n):
    for t in (1536, 1408, 1024, 768, 512, 384, 256, 128):
        if n % t == 0:
            return t
    return n


def mm_nn(a, b, name, bias=None, residual=None, out_dtype=F32, cols=None):
    T, K = a.shape
    first, end = cols or (0, b.shape[1])
    N = end - first
    tm, tn = _tile(T, 512), _col_tile(N)
    assert first % tn == 0 and (cols is None or (bias is None and residual is None))
    j0 = first // tn

    def body(a_ref, b_ref, *rest):
        o_ref = rest[-1]
        acc = _dot(a_ref[...].astype(BF16), b_ref[...])
        for extra in rest[:-1]:
            acc = acc + extra[...]
        o_ref[...] = acc.astype(out_dtype)

    in_specs = [pl.BlockSpec((tm, K), lambda j, i: (i, 0)), pl.BlockSpec((K, tn), lambda j, i: (0, j0 + j))]
    args = [a, b]
    if bias is not None:
        in_specs.append(pl.BlockSpec((1, tn), lambda j, i: (0, j)))
        args.append(bias)
    if residual is not None:
        in_specs.append(pl.BlockSpec((tm, tn), lambda j, i: (i, j)))
        args.append(residual)
    return _pc(body, name=name, out_shape=_sds((T, N), out_dtype), grid=(N // tn, T // tm), in_specs=in_specs,
               out_specs=pl.BlockSpec((tm, tn), lambda j, i: (i, j)), sem=("parallel", "parallel"))(*args)


def mm_nt(a, b, name, out_dtype=F32, norm_bwd=None):
    T, N = a.shape
    K = b.shape[0]
    tm = _tile(T, 512)
    row = pl.BlockSpec((tm, K), lambda i: (i, 0))
    in_specs = [pl.BlockSpec((tm, N), lambda i: (i, 0)), _resident((K, N))]

    if norm_bwd is None:
        def body(a_ref, b_ref, o_ref):
            o_ref[...] = _dot(a_ref[...].astype(BF16), b_ref[...], NT).astype(out_dtype)

        return _pc(body, name=name, out_shape=_sds((T, K), out_dtype), grid=(T // tm,), in_specs=in_specs,
                   out_specs=row, sem=("parallel",))(a, b)

    def body(a_ref, b_ref, x_ref, w_ref, dres_ref, dx_ref, dw_ref):
        dx, dw = _rms_bwd(x_ref[...], w_ref[...], _dot(a_ref[...].astype(BF16), b_ref[...], NT))
        dx_ref[...] = dres_ref[...] + dx

        @pl.when(pl.program_id(0) == 0)
        def _():
            dw_ref[...] = jnp.zeros_like(dw_ref)
        dw_ref[...] += dw

    vec = pl.BlockSpec((1, K), lambda i: (0, 0))
    return _pc(body, name=name, out_shape=(_sds((T, K), F32), _sds((1, K), F32)), grid=(T // tm,),
               in_specs=in_specs + [row, vec, row], out_specs=(row, vec), sem=("arbitrary",))(a, b, *norm_bwd)


def mm_tn(a, b, name):
    T, K = a.shape
    N = b.shape[1]
    tt, tn = _tile(T, 1024), _col_tile(N)

    def body(a_ref, b_ref, o_ref):
        @pl.when(pl.program_id(1) == 0)
        def _():
            o_ref[...] = jnp.zeros_like(o_ref)
        o_ref[...] += _dot(a_ref[...].astype(BF16), b_ref[...].astype(BF16), TN)

    return _pc(body, name=name, out_shape=_sds((K, N), F32), grid=(N // tn, T // tt),
               in_specs=[pl.BlockSpec((tt, K), lambda j, t: (t, 0)), pl.BlockSpec((tt, tn), lambda j, t: (t, j))],
               out_specs=pl.BlockSpec((K, tn), lambda j, t: (0, j)), sem=("parallel", "arbitrary"))(a, b)


def ffn_up(xn, wgu, name, deps=()):
    T = xn.shape[0]
    tm = _tile(T, 1024)

    def body(x_ref, w_ref, *rest):
        xv = x_ref[...]
        for j in range(2 * N_FB):
            rest[-1][j] = _dot(xv, w_ref[j]).astype(BF16)

    return _pc(body, name=name, out_shape=_sds((2 * N_FB, T, FB), BF16), grid=(T // tm,),
               in_specs=[pl.BlockSpec((tm, D), lambda i: (i, 0)), _resident((2 * N_FB, D, FB))] + [DEP_SPEC] * len(deps),
               out_specs=pl.BlockSpec((2 * N_FB, tm, FB), lambda i: (0, i, 0)), sem=("parallel",))(xn, wgu, *deps)


def ffn_down(gu, wd, x, name):
    T = x.shape[0]
    tm = _tile(T, 512)

    def body(gu_ref, w_ref, x_ref, o_ref):
        acc = jnp.zeros((tm, D), F32)
        for g in range(N_FB):
            gate, up = gu_ref[g], gu_ref[N_FB + g]
            acc = acc + _dot(gate * _sigmoid(gate) * up, w_ref[g])
        o_ref[...] = x_ref[...] + 0.5 * acc

    row = pl.BlockSpec((tm, D), lambda i: (i, 0))
    return _pc(body, name=name, out_shape=_sds((T, D), F32), grid=(T // tm,),
               in_specs=[pl.BlockSpec((2 * N_FB, tm, FB), lambda i: (0, i, 0)),
                         _resident((N_FB, FB, D)), row],
               out_specs=row, sem=("parallel",))(gu, wd, x)


def _resident(shape):
    return pl.BlockSpec(shape, lambda *_: (0,) * len(shape), pipeline_mode=pl.Buffered(1))


def _store_blocks_bf16(acc, out_hbm, stage, sem):
    for j in range(acc.shape[0]):
        stage[...] = acc[j].astype(BF16)
        copy = pltpu.make_async_copy(stage, out_hbm.at[j], sem)
        copy.start()
        copy.wait()


def ffn_bwd_hidden(dout, wd, gu, name, deps=()):
    T = dout.shape[0]
    tm = _tile(T, 512)
    n_t = T // tm

    def body(d_ref, w_ref, gu_ref, *rest):
        dgu_ref, dwd_hbm, acc, stage, sem = rest[-5:]
        t = pl.program_id(0)

        @pl.when(t == 0)
        def _():
            acc[...] = jnp.zeros_like(acc)
        dy = (0.5 * d_ref[...]).astype(BF16)
        for g in range(N_FB):
            gate, up = gu_ref[g], gu_ref[N_FB + g]
            sg = _sigmoid(gate)
            silu = gate * sg
            dact = _dot(dy, w_ref[g], NT).astype(BF16)
            acc[g] += _dot(silu * up, dy, TN)
            dgu_ref[g] = dact * up * (sg * (1.0 + gate * (1.0 - sg)))
            dgu_ref[N_FB + g] = dact * silu

        @pl.when(t == n_t - 1)
        def _():
            _store_blocks_bf16(acc, dwd_hbm, stage, sem)

    return _pc(body, name=name, out_shape=(_sds((2 * N_FB, T, FB), BF16), _sds((N_FB, FB, D), BF16)), grid=(n_t,),
               in_specs=[pl.BlockSpec((tm, D), lambda i: (i, 0)), _resident((N_FB, FB, D)),
                         pl.BlockSpec((2 * N_FB, tm, FB), lambda i: (0, i, 0))] + [DEP_SPEC] * len(deps),
               out_specs=(pl.BlockSpec((2 * N_FB, tm, FB), lambda i: (0, i, 0)), pl.BlockSpec(memory_space=pl.ANY)),
               scratch=[pltpu.VMEM((N_FB, FB, D), F32), pltpu.VMEM((FB, D), BF16), pltpu.SemaphoreType.DMA],
               sem=("arbitrary",))(dout, wd, gu, *deps)


def ffn_bwd_input(dgu, wgu, x, dout, nw, name, deps=()):
    T = x.shape[0]
    tm = _tile(T, 512)

    def body(dgu_ref, w_ref, x_ref, d_ref, nw_ref, *rest):
        dx_ref, dnw_ref = rest[-2:]
        dxn = jnp.zeros((tm, D), F32)
        for j in range(2 * N_FB):
            dxn = dxn + _dot(dgu_ref[j], w_ref[j], NT)
        dx, dw = _rms_bwd(x_ref[...], nw_ref[...], dxn)
        dx_ref[...] = d_ref[...] + dx

        @pl.when(pl.program_id(0) == 0)
        def _():
            dnw_ref[...] = jnp.zeros_like(dnw_ref)
        dnw_ref[...] += dw

    row = pl.BlockSpec((tm, D), lambda i: (i, 0))
    vec = pl.BlockSpec((1, D), lambda i: (0, 0))
    return _pc(body, name=name, out_shape=(_sds((T, D), F32), _sds((1, D), F32)), grid=(T // tm,),
               in_specs=[pl.BlockSpec((2 * N_FB, tm, FB), lambda i: (0, i, 0)), _resident((2 * N_FB, D, FB)),
                         row, row, vec] + [DEP_SPEC] * len(deps),
               out_specs=(row, vec), sem=("arbitrary",))(dgu, wgu, x, dout, nw, *deps)


def ffn_wgrad_gu(xn, dgu, name):
    T = xn.shape[0]
    tt = _tile(T, 1024)
    n_t = T // tt

    def body(x_ref, d_ref, dw_hbm, acc, stage, sem):
        t = pl.program_id(0)

        @pl.when(t == 0)
        def _():
            acc[...] = jnp.zeros_like(acc)
        xn_tile = x_ref[...]
        for j in range(2 * N_FB):
            acc[j] += _dot(xn_tile, d_ref[j], TN)

        @pl.when(t == n_t - 1)
        def _():
            _store_blocks_bf16(acc, dw_hbm, stage, sem)

    return _pc(body, name=name, out_shape=_sds((2 * N_FB, D, FB), BF16), grid=(n_t,),
               in_specs=[pl.BlockSpec((tt, D), lambda t: (t, 0)), pl.BlockSpec((2 * N_FB, tt, FB), lambda t: (0, t, 0))],
               out_specs=pl.BlockSpec(memory_space=pl.ANY),
               scratch=[pltpu.VMEM((2 * N_FB, D, FB), F32), pltpu.VMEM((D, FB), BF16), pltpu.SemaphoreType.DMA],
               sem=("arbitrary",))(xn, dgu)


def ffn_forward(x, nw, wgu, wd, tag):
    T = x.shape[0]
    tm = _tile(T, 512)

    def body(x_ref, nw_ref, wgu_ref, wd_ref, o_ref, xn_ref, gu_ref):
        xv = x_ref[...]
        xn = _rms_fwd(xv, nw_ref[...]).astype(BF16)
        xn_ref[...] = xn
        for j in range(2 * N_FB):
            gu_ref[j] = _dot(xn, wgu_ref[j]).astype(BF16)
        acc = jnp.zeros((tm, D), F32)
        for g in range(N_FB):
            gate, up = gu_ref[g], gu_ref[N_FB + g]
            acc = acc + _dot(gate * _sigmoid(gate) * up, wd_ref[g])
        o_ref[...] = xv + 0.5 * acc

    row = pl.BlockSpec((tm, D), lambda i: (i, 0))
    out, xn, gu = _pc(body, name=f"{tag}_fwd",
                      out_shape=(_sds((T, D), F32), _sds((T, D), BF16), _sds((2 * N_FB, T, FB), BF16)), grid=(T // tm,),
                      in_specs=[row, pl.BlockSpec((1, D), lambda i: (0, 0)), _resident((2 * N_FB, D, FB)),
                                _resident((N_FB, FB, D))],
                      out_specs=(row, row, pl.BlockSpec((2 * N_FB, tm, FB), lambda i: (0, i, 0))),
                      sem=("parallel",))(x, nw, wgu, wd)
    return out, (x, xn, gu)


def ffn_backward(dout, saved, nw, wgu, wd, tag, deps=(), on_grads=None):
    x, xn, gu = saved
    dgu, dwd = ffn_bwd_hidden(dout, wd, gu, f"{tag}_bwd_hidden", deps)
    dwgu = ffn_wgrad_gu(xn, dgu, f"{tag}_wgrad_gu")
    late = on_grads(dwgu, dwd) if on_grads else ()
    dx, dnw = ffn_bwd_input(dgu, wgu, x, dout, nw, f"{tag}_bwd_input", late)
    return dx, dnw, dwgu, dwd


N_QKV_BLK = 3 * HEADS_A
Z_BLK0 = N_QKV_BLK
MAIN_COLS = 4 * D
HALO = 16


def _conv_taps(xcat, w):
    c = xcat[HALO:] * w[3:4]
    for k in range(3):
        c = c + pltpu.roll(xcat, 3 - k, 0)[HALO:] * w[k:k + 1]
    return c


def _head_cols(h):
    return slice(128 * h, 128 * (h + 1))


def gdn_conv_fwd(proj, wconv, name):
    T = proj.shape[0]
    tm = _tile(T, 512)

    def body(cur_ref, prev_ref, w_ref, c_ref, y_ref):
        kind, t = pl.program_id(0), pl.program_id(1)
        prev = jnp.where(t > 0, prev_ref[...].astype(F32), 0.0)
        c = _conv_taps(jnp.concatenate([prev, cur_ref[...].astype(F32)], axis=0), w_ref[...])
        c_ref[...] = c.astype(BF16)
        s = c * _sigmoid(c)
        scale = jnp.where(kind == 0, DK ** -0.5, 1.0)
        for h in range(HEADS_A):
            sh = s[:, _head_cols(h)]
            r = lax.rsqrt(jnp.sum(sh * sh, axis=-1, keepdims=True) + EPS)
            y_ref[h] = (sh * jnp.where(kind < 2, r * scale, 1.0)).astype(BF16)

    return _pc(body, name=name, out_shape=(_sds((T, 3 * D), BF16), _sds((N_QKV_BLK, T, 128), BF16)),
               grid=(3, T // tm),
               in_specs=[pl.BlockSpec((tm, D), lambda kd, t: (t, kd)),
                         pl.BlockSpec((HALO, D), lambda kd, t: (jnp.maximum(t * (tm // HALO) - 1, 0), kd)),
                         pl.BlockSpec((4, D), lambda kd, t: (0, kd))],
               out_specs=(pl.BlockSpec((tm, D), lambda kd, t: (t, kd)),
                          pl.BlockSpec((HEADS_A, tm, 128), lambda kd, t: (kd, t, 0))),
               sem=("parallel", "parallel"))(proj, proj, wconv)


def gdn_conv_bwd(dqkv, c, proj, wconv, dproj, name):
    T = c.shape[0]
    tm = _tile(T, 512)
    n_t = T // tm

    def body(dy_ref, dyn_ref, c_ref, cn_ref, x_ref, w_ref, _, dx_ref, dw_ref):
        kind, t = pl.program_id(0), pl.program_id(1)
        scale = jnp.where(kind == 0, DK ** -0.5, 1.0)

        def act_bwd(dy, cv):
            sg = _sigmoid(cv)
            s = cv * sg
            parts = []
            for h in range(HEADS_A):
                sh, dyh = s[:, _head_cols(h)], dy[h]
                r = lax.rsqrt(jnp.sum(sh * sh, axis=-1, keepdims=True) + EPS)
                ds_norm = scale * r * (dyh - (r * r) * sh * jnp.sum(dyh * sh, axis=-1, keepdims=True))
                parts.append(jnp.where(kind < 2, ds_norm, dyh))
            return jnp.concatenate(parts, axis=1) * (sg * (1.0 + cv * (1.0 - sg)))

        w = w_ref[...]
        dcur = act_bwd(dy_ref[...].astype(F32), c_ref[...].astype(F32))
        dnext = jnp.where(t < n_t - 1, act_bwd(dyn_ref[...].astype(F32), cn_ref[...].astype(F32)), 0.0)
        dcat = jnp.concatenate([dcur, dnext], axis=0)
        xcur = x_ref[...].astype(F32)
        dx = dcur * w[3:4]
        rows = [None, None, None, jnp.sum(dcur * xcur, axis=0, keepdims=True)]
        for k in range(3):
            ahead = pltpu.roll(dcat, tm + HALO - (3 - k), 0)[:tm]
            dx = dx + ahead * w[k:k + 1]
            rows[k] = jnp.sum(ahead * xcur, axis=0, keepdims=True)
        dx_ref[...] = dx.astype(BF16)

        @pl.when(t == 0)
        def _():
            dw_ref[...] = jnp.zeros_like(dw_ref)
        dw_ref[...] += jnp.concatenate(rows, axis=0)

    def nxt(t):
        return jnp.minimum((t + 1) * (tm // HALO), T // HALO - 1)

    cur = pl.BlockSpec((tm, D), lambda kd, t: (t, kd))
    return _pc(body, name=name, out_shape=(_sds(dproj.shape, BF16), _sds((4, 3 * D), F32)), grid=(3, n_t),
               in_specs=[pl.BlockSpec((HEADS_A, tm, 128), lambda kd, t: (kd, t, 0)),
                         pl.BlockSpec((HEADS_A, HALO, 128), lambda kd, t: (kd, nxt(t), 0)),
                         cur, pl.BlockSpec((HALO, D), lambda kd, t: (nxt(t), kd)),
                         cur, pl.BlockSpec((4, D), lambda kd, t: (0, kd)), DEP_SPEC],
               out_specs=(cur, pl.BlockSpec((4, D), lambda kd, t: (0, kd))), input_output_aliases={6: 0},
               sem=("parallel", "arbitrary"))(dqkv, dqkv, c, c, proj, wconv, dproj)


def _chunk_masks(n):
    ri = lax.broadcasted_iota(jnp.int32, (n, n), 0)
    ci = lax.broadcasted_iota(jnp.int32, (n, n), 1)
    same = (ri // CHUNK) == (ci // CHUNK)
    return same & (ri >= ci), same & (ri <= ci)


def gdn_gate_fwd(ba, al, dtb, name):
    T = ba.shape[0]
    tg = _tile(T, PREP_T)

    def body(ba_ref, al_ref, dtb_ref, o_ref):
        x = ba_ref[...]
        lane = lax.broadcasted_iota(jnp.int32, x.shape, 1)
        is_a = (lane >= HEADS_A) & (lane < 2 * HEADS_A)
        g = jnp.where(is_a, -jnp.exp(al_ref[...]) * _softplus(x + dtb_ref[...]), 0.0)
        lower, _ = _chunk_masks(tg)
        gc = _dot(lower.astype(F32), g, precision=HI)
        o_ref[...] = jnp.where(lane < HEADS_A, _sigmoid(x), gc)

    vec = pl.BlockSpec((1, 128), lambda i: (0, 0))
    return _pc(body, name=name, out_shape=_sds((T, 128), F32), grid=(T // tg,),
               in_specs=[pl.BlockSpec((tg, 128), lambda i: (i, 0)), vec, vec],
               out_specs=pl.BlockSpec((tg, 128), lambda i: (i, 0)), sem=("parallel",))(ba, al, dtb)


def gdn_gate_bwd(ba, al, dtb, dgb, dproj, name):
    T = ba.shape[0]
    tg = _tile(T, PREP_T)

    def body(ba_ref, al_ref, dtb_ref, dgb_ref, _, dba_ref, dal_ref, ddt_ref):
        x, d = ba_ref[...], dgb_ref[...]
        lane = lax.broadcasted_iota(jnp.int32, x.shape, 1)
        is_b = lane < HEADS_A
        is_a = (lane >= HEADS_A) & (lane < 2 * HEADS_A)
        beta = _sigmoid(x)
        e_a = jnp.exp(al_ref[...])
        z = x + dtb_ref[...]
        g = jnp.where(is_a, -e_a * _softplus(z), 0.0)
        _, upper = _chunk_masks(tg)
        dg = _dot(upper.astype(F32), jnp.where(is_a, d, 0.0), precision=HI)
        da = jnp.where(is_a, dg * (-e_a) * _sigmoid(z), 0.0)
        db = jnp.where(is_b, d * beta * (1.0 - beta), 0.0)
        dba_ref[...] = (da + db).astype(BF16)

        @pl.when(pl.program_id(0) == 0)
        def _():
            dal_ref[...] = jnp.zeros_like(dal_ref)
            ddt_ref[...] = jnp.zeros_like(ddt_ref)
        dal_ref[...] += jnp.sum(dg * g, axis=0, keepdims=True)
        ddt_ref[...] += jnp.sum(da, axis=0, keepdims=True)

    vec = pl.BlockSpec((1, 128), lambda i: (0, 0))
    blk = pl.BlockSpec((tg, 128), lambda i: (i, 0))
    ba_cols = pl.BlockSpec((tg, 128), lambda i: (i, A_COLS // 128 - 1))
    return _pc(body, name=name, out_shape=(_sds(dproj.shape, BF16), _sds((1, 128), F32), _sds((1, 128), F32)),
               grid=(T // tg,), in_specs=[blk, vec, vec, blk, DEP_SPEC],
               out_specs=(ba_cols, vec, vec), input_output_aliases={4: 0}, sem=("arbitrary",))(ba, al, dtb, dgb, dproj)


def _bmm(a, b, dims, precision=None):
    return lax.dot_general(a, b, dims, preferred_element_type=F32, precision=precision)


B_NN = (((2,), (1,)), ((0,), (0,)))
B_NT = (((2,), (2,)), ((0,), (0,)))


def _select_lane(x, lane_index):
    lane = lax.broadcasted_iota(jnp.int32, x.shape, x.ndim - 1)
    return jnp.sum(jnp.where(lane == lane_index, x, 0.0), axis=-1, keepdims=True)


B_TN = (((1,), (1,)), ((0,), (0,)))


def _bmm_split(a, b, dims):
    ah, bh = a.astype(BF16), b.astype(BF16)
    al, bl = (a - ah.astype(F32)).astype(BF16), (b - bh.astype(F32)).astype(BF16)
    return _bmm(ah, bh, dims) + (_bmm(ah, bl, dims) + _bmm(al, bh, dims))


@jax.custom_vjp
def _bmm_f32(a, b):
    return _bmm_split(a, b, B_NN)


def _bmm_f32_fwd(a, b):
    return _bmm_split(a, b, B_NN), (a, b)


def _bmm_bf16(a, b, dims):
    return _bmm(a.astype(BF16), b.astype(BF16), dims)


def _bmm_f32_bwd(res, dc):
    a, b = res
    return _bmm_bf16(dc, b, B_NT), _bmm_bf16(a, dc, B_TN)


_bmm_f32.defvjp(_bmm_f32_fwd, _bmm_f32_bwd)


def _tri_inverse(lmat):
    ri = lax.broadcasted_iota(jnp.int32, lmat.shape, 1)
    ci = lax.broadcasted_iota(jnp.int32, lmat.shape, 2)
    eye = jnp.where(ri == ci, 1.0, 0.0)
    inv = eye - lmat
    power = lmat
    for _ in range(5):
        power = _bmm_bf16(power, power, B_NN)
        inv = inv + _bmm_bf16(inv, power, B_NN)
    return _bmm_split(inv, 2.0 * eye - _bmm_split(eye + lmat, inv, B_NN), B_NN)


def _stored_inverse(x):
    @jax.custom_vjp
    def inverse(lmat):
        return x

    def fwd(lmat):
        return x, None

    def bwd(_, dx):
        return (-_bmm_bf16(_bmm_bf16(x, dx, B_TN), x, B_NT),)

    inverse.defvjp(fwd, bwd)
    return inverse


def _gdn_prep(q, k, v, gb, h, inverse):
    nb = q.shape[0]
    beta = _select_lane(gb, h)
    gc = _select_lane(gb, HEADS_A + h)
    ri = lax.broadcasted_iota(jnp.int32, (nb, CHUNK, CHUNK), 1)
    ci = lax.broadcasted_iota(jnp.int32, (nb, CHUNK, CHUNK), 2)
    lower, strict, eye = ri >= ci, ri > ci, ri == ci
    gcol = jnp.broadcast_to(gc, (nb, CHUNK, CHUNK))
    grow = jnp.swapaxes(gcol, 1, 2)
    decay = jnp.where(lower, jnp.exp(jnp.where(lower, gcol - grow, 0.0)), 0.0)
    kb = k * beta
    kbf = k.astype(BF16)
    inv = inverse(jnp.where(strict, _bmm(kb.astype(BF16), kbf, B_NT) * decay, 0.0))
    eg = jnp.exp(gc)
    sol = _bmm_f32(inv, jnp.concatenate([v * beta, kb * eg], axis=-1))
    aqk = _bmm(q.astype(BF16), kbf, B_NT) * decay
    g_last = gc[:, CHUNK - 1:CHUNK, :]
    gl = jnp.broadcast_to(jnp.exp(g_last), (nb, 1, 128))
    return (sol[..., :DK], sol[..., DK:], q * eg, k * jnp.exp(g_last - gc), aqk, gl), inv


def gdn_prep_fwd(qkv, gb, name):
    T = qkv.shape[1]
    tp = _tile(T, 4 * PREP_T)
    nb = tp // CHUNK

    def body(q_ref, k_ref, v_ref, gb_ref, u_ref, w_ref, qd_ref, kd_ref, a_ref, gl_ref, inv_ref):
        h = pl.program_id(1)
        shp = (nb, CHUNK, 128)
        q, k, v = (ref[0].astype(F32).reshape(shp) for ref in (q_ref, k_ref, v_ref))
        (u, w, qd, kd, aqk, gl), inv = _gdn_prep(q, k, v, gb_ref[...].reshape(shp), h, _tri_inverse)
        u_ref[0] = u.reshape(tp, 128)
        w_ref[0] = w.reshape(tp, 128).astype(BF16)
        qd_ref[0] = qd.reshape(tp, 128).astype(BF16)
        kd_ref[0] = kd.reshape(tp, 128).astype(BF16)
        a_ref[0] = aqk.reshape(tp, CHUNK).astype(BF16)
        gl_ref[0] = gl.reshape(nb, 1, 128)
        inv_ref[0] = inv.reshape(tp, CHUNK)

    def head(off):
        return pl.BlockSpec((1, tp, 128), lambda n, h: (h + off, n, 0))

    matmul_only = _sds((HEADS_A, T, 128), BF16)
    narrow = pl.BlockSpec((1, tp, CHUNK), lambda n, h: (h, n, 0))
    return _pc(body, name=name,
               out_shape=(_sds((HEADS_A, T, 128), F32), matmul_only, matmul_only, matmul_only, _sds((HEADS_A, T, CHUNK), BF16),
                          _sds((HEADS_A, T // CHUNK, 1, 128), F32), _sds((HEADS_A, T, CHUNK), F32)),
               grid=(T // tp, HEADS_A),
               in_specs=[head(0), head(HEADS_A), head(2 * HEADS_A), pl.BlockSpec((tp, 128), lambda n, h: (n, 0))],
               out_specs=(head(0), head(0), head(0), head(0), narrow,
                          pl.BlockSpec((1, nb, 1, 128), lambda n, h: (h, n, 0, 0)), narrow),
               sem=("parallel", "parallel"))(qkv, qkv, qkv, gb)


def gdn_prep_bwd(qkv, gb, inv, du, dw, dqd, dkd, da, dgl, name):
    T = qkv.shape[1]
    tp = _tile(T, 4 * PREP_T)
    nb = tp // CHUNK

    def body(q_ref, k_ref, v_ref, gb_ref, inv_ref, du_ref, dw_ref, dqd_ref, dkd_ref, da_ref, dgl_ref, dqkv_ref, dgb_ref):
        h = pl.program_id(1)
        shp = (nb, CHUNK, 128)
        stored = _stored_inverse(inv_ref[0].reshape(nb, CHUNK, CHUNK))
        q, k, v = (ref[0].astype(F32).reshape(shp) for ref in (q_ref, k_ref, v_ref))
        _, vjp = jax.vjp(lambda q, k, v, gb: _gdn_prep(q, k, v, gb, h, stored)[0], q, k, v, gb_ref[...].reshape(shp))
        du, dw, dqd, dkd = (ref[0].astype(F32).reshape(shp) for ref in (du_ref, dw_ref, dqd_ref, dkd_ref))
        dq, dk, dv, dgb = vjp((du, dw, dqd, dkd, da_ref[0].astype(F32).reshape(nb, CHUNK, CHUNK),
                               dgl_ref[0].reshape(nb, 1, 128)))
        dqkv_ref[h] = dq.reshape(tp, 128).astype(BF16)
        dqkv_ref[HEADS_A + h] = dk.reshape(tp, 128).astype(BF16)
        dqkv_ref[2 * HEADS_A + h] = dv.reshape(tp, 128).astype(BF16)

        @pl.when(h == 0)
        def _():
            dgb_ref[...] = jnp.zeros_like(dgb_ref)
        dgb_ref[...] += dgb.reshape(tp, 128)

    def head(off):
        return pl.BlockSpec((1, tp, 128), lambda n, h: (h + off, n, 0))

    narrow = pl.BlockSpec((1, tp, CHUNK), lambda n, h: (h, n, 0))
    return _pc(body, name=name, out_shape=(_sds((N_QKV_BLK, T, 128), BF16), _sds((T, 128), F32)),
               grid=(T // tp, HEADS_A),
               in_specs=[head(0), head(HEADS_A), head(2 * HEADS_A), pl.BlockSpec((tp, 128), lambda n, h: (n, 0)), narrow,
                         head(0), head(0), head(0), head(0), narrow,
                         pl.BlockSpec((1, nb, 1, 128), lambda n, h: (h, n, 0, 0))],
               out_specs=(pl.BlockSpec((N_QKV_BLK, tp, 128), lambda n, h: (0, n, 0)),
                          pl.BlockSpec((tp, 128), lambda n, h: (n, 0))),
               sem=("parallel", "arbitrary"))(qkv, qkv, qkv, gb, inv, du, dw, dqd, dkd, da, dgl)


SCAN_CHUNKS = 8


def gdn_scan_fwd(u, w, qd, kd, aqk, gl, name):
    T = u.shape[1]
    n_chunks = T // CHUNK
    rows_per_step = SCAN_CHUNKS * CHUNK

    def body(u_ref, w_ref, qd_ref, kd_ref, a_ref, gl_ref, o_ref, sin_ref, state):
        @pl.when(pl.program_id(0) == 0)
        def _():
            state[...] = jnp.zeros_like(state)
        s = state[...]
        for c in range(SCAN_CHUNKS):
            rows = slice(c * CHUNK, (c + 1) * CHUNK)
            sb = s.astype(BF16)
            sin_ref[c] = sb
            both = _bmm(jnp.concatenate([w_ref[:, rows], qd_ref[:, rows]], axis=1), sb, B_NN)
            vn = (u_ref[:, rows] - both[:, :CHUNK]).astype(BF16)
            o_ref[:, rows] = both[:, CHUNK:] + _bmm(a_ref[:, rows], vn, B_NN)
            s = s * gl_ref[:, c] + _bmm(kd_ref[:, rows], vn, B_TN)
        state[...] = s

    blk = pl.BlockSpec((HEADS_A, rows_per_step, 128), lambda n: (0, n, 0))
    return _pc(body, name=name,
               out_shape=(_sds((HEADS_A, T, 128), F32), _sds((n_chunks, HEADS_A, DK, 128), BF16)),
               grid=(n_chunks // SCAN_CHUNKS,),
               in_specs=[blk, blk, blk, blk, pl.BlockSpec((HEADS_A, rows_per_step, CHUNK), lambda n: (0, n, 0)),
                         pl.BlockSpec((HEADS_A, SCAN_CHUNKS, 1, 128), lambda n: (0, n, 0, 0))],
               out_specs=(blk, pl.BlockSpec((SCAN_CHUNKS, HEADS_A, DK, 128), lambda n: (n, 0, 0, 0))),
               scratch=[pltpu.VMEM((HEADS_A, DK, 128), F32)], sem=("arbitrary",))(u, w, qd, kd, aqk, gl)


def gdn_scan_bwd(u, w, qd, kd, aqk, gl, sin, do, name):
    T = u.shape[1]
    n_chunks = T // CHUNK
    rows_per_step = SCAN_CHUNKS * CHUNK

    def body(u_ref, w_ref, qd_ref, kd_ref, a_ref, gl_ref, sin_ref, do_ref,
             du_ref, dw_ref, dqd_ref, dkd_ref, da_ref, dgl_ref, dstate):
        @pl.when(pl.program_id(0) == 0)
        def _():
            dstate[...] = jnp.zeros_like(dstate)
        lane0 = lax.broadcasted_iota(jnp.int32, (HEADS_A, 1, 128), 2) == 0
        ds_out = dstate[...]
        for c in reversed(range(SCAN_CHUNKS)):
            rows = slice(c * CHUNK, (c + 1) * CHUNK)
            sb = sin_ref[c]
            wb, qdb, kdb, ab, dob = w_ref[:, rows], qd_ref[:, rows], kd_ref[:, rows], a_ref[:, rows], do_ref[:, rows]
            vn = (u_ref[:, rows] - _bmm(wb, sb, B_NN)).astype(BF16)
            dsb = ds_out.astype(BF16)
            dqd_ref[:, rows] = _bmm(dob, sb, B_NT).astype(BF16)
            da_ref[:, rows] = _bmm(dob, vn, B_NT).astype(BF16)
            dvb = (_bmm(ab, dob, B_TN) + _bmm(kdb, dsb, B_NN)).astype(BF16)
            dkd_ref[:, rows] = _bmm(vn, dsb, B_NT).astype(BF16)
            dgl = jnp.sum(jnp.sum(ds_out * sb.astype(F32), axis=2, keepdims=True), axis=1, keepdims=True)
            dgl_ref[:, c] = jnp.where(lane0, dgl, 0.0)
            du_ref[:, rows] = dvb
            dw_ref[:, rows] = (-_bmm(dvb, sb, B_NT)).astype(BF16)
            ds_out = ds_out * gl_ref[:, c] + _bmm(qdb, dob, B_TN) - _bmm(wb, dvb, B_TN)
        dstate[...] = ds_out

    last = n_chunks // SCAN_CHUNKS - 1
    blk = pl.BlockSpec((HEADS_A, rows_per_step, 128), lambda n: (0, last - n, 0))
    ablk = pl.BlockSpec((HEADS_A, rows_per_step, CHUNK), lambda n: (0, last - n, 0))
    glblk = pl.BlockSpec((HEADS_A, SCAN_CHUNKS, 1, 128), lambda n: (0, last - n, 0, 0))
    per_head = _sds((HEADS_A, T, 128), BF16)
    return _pc(body, name=name,
               out_shape=(per_head, per_head, per_head, per_head, _sds((HEADS_A, T, CHUNK), BF16),
                          _sds((HEADS_A, n_chunks, 1, 128), F32)), grid=(n_chunks // SCAN_CHUNKS,),
               in_specs=[blk, blk, blk, blk, ablk, glblk,
                         pl.BlockSpec((SCAN_CHUNKS, HEADS_A, DK, 128), lambda n: (last - n, 0, 0, 0)), blk],
               out_specs=(blk, blk, blk, blk, ablk, glblk),
               scratch=[pltpu.VMEM((HEADS_A, DK, 128), F32)], sem=("arbitrary",))(u, w, qd, kd, aqk, gl, sin, do)


def gdn_outnorm_fwd(o, proj, wn, name):
    T = o.shape[1]
    tm = _tile(T, 512)

    def body(o_ref, z_ref, wn_ref, y_ref):
        for h in range(HEADS_A):
            z = z_ref[:, 128 * h:128 * (h + 1)].astype(F32)
            y_ref[:, 128 * h:128 * (h + 1)] = (_rms_fwd(o_ref[h], wn_ref[...]) * (z * _sigmoid(z))).astype(BF16)

    return _pc(body, name=name, out_shape=_sds((T, D), BF16), grid=(T // tm,),
               in_specs=[pl.BlockSpec((HEADS_A, tm, 128), lambda i: (0, i, 0)),
                         pl.BlockSpec((tm, D), lambda i: (i, Z_BLK0 * 128 // D)), pl.BlockSpec((1, 128), lambda i: (0, 0))],
               out_specs=pl.BlockSpec((tm, D), lambda i: (i, 0)), sem=("parallel",))(o, proj, wn)


def gdn_outnorm_bwd(o, proj, wn, dy, name):
    T = o.shape[1]
    tm = _tile(T, 512)

    def body(o_ref, z_ref, wn_ref, dy_ref, do_ref, dz_ref, dwn_ref):
        wn = wn_ref[...]
        acc = jnp.zeros((1, 128), F32)
        for h in range(HEADS_A):
            cols = slice(128 * h, 128 * (h + 1))
            z, dyh, ov = z_ref[:, cols].astype(F32), dy_ref[:, cols], o_ref[h]
            sg = _sigmoid(z)
            do, dwn = _rms_bwd(ov, wn, dyh * (z * sg))
            do_ref[h] = do.astype(BF16)
            acc = acc + dwn
            dz_ref[:, cols] = (dyh * _rms_fwd(ov, wn) * (sg * (1.0 + z * (1.0 - sg)))).astype(BF16)

        @pl.when(pl.program_id(0) == 0)
        def _():
            dwn_ref[...] = jnp.zeros_like(dwn_ref)
        dwn_ref[...] += acc

    row = pl.BlockSpec((tm, D), lambda i: (i, 0))
    vec = pl.BlockSpec((1, 128), lambda i: (0, 0))
    hblk = pl.BlockSpec((HEADS_A, tm, 128), lambda i: (0, i, 0))
    z_cols = pl.BlockSpec((tm, D), lambda i: (i, Z_BLK0 * 128 // D))
    return _pc(body, name=name, out_shape=(_sds((HEADS_A, T, 128), BF16), _sds((T, A_COLS), BF16), _sds((1, 128), F32)),
               grid=(T // tm,), in_specs=[hblk, z_cols, vec, row],
               out_specs=(hblk, z_cols, vec), sem=("arbitrary",))(o, proj, wn, dy)


def gdn_forward(x, nw, w_in, wconv, al, dtb, wn, w_out, tag, deps=()):
    h = rmsnorm_bf16(x, nw, f"{tag}_norm", deps)
    proj = mm_nn(h, w_in, f"{tag}_proj", out_dtype=BF16, cols=(0, MAIN_COLS))
    ba = mm_nn(h, w_in, f"{tag}_proj_ba", cols=(MAIN_COLS, A_COLS))
    c, qkv = gdn_conv_fwd(proj, wconv, f"{tag}_conv")
    gb = gdn_gate_fwd(ba, al, dtb, f"{tag}_gate")
    u, w, qd, kd, aqk, gl, inv = gdn_prep_fwd(qkv, gb, f"{tag}_prep")
    o, sin = gdn_scan_fwd(u, w, qd, kd, aqk, gl, f"{tag}_scan")
    on = gdn_outnorm_fwd(o, proj, wn, f"{tag}_outnorm")
    y = mm_nn(on, w_out, f"{tag}_out", residual=x)
    return y, (x, h, proj, ba, c, qkv, gb, inv, (u, w, qd, kd, aqk, gl), sin, o, on)


def gdn_backward(dout, saved, nw, w_in, wconv, al, dtb, wn, w_out, tag):
    x, h, proj, ba, c, qkv, gb, inv, prep, sin, o, on = saved
    d_on = mm_nt(dout, w_out, f"{tag}_out_bwd")
    dw_out = mm_tn(on, dout, f"{tag}_out_wgrad")
    do, dproj, dwn = gdn_outnorm_bwd(o, proj, wn, d_on, f"{tag}_outnorm_bwd")
    du, dw, dqd, dkd, da, dgl = gdn_scan_bwd(*prep, sin, do, f"{tag}_scan_bwd")
    dqkv, dgb = gdn_prep_bwd(qkv, gb, inv, du, dw, dqd, dkd, da, dgl, f"{tag}_prep_bwd")
    dproj, dal, ddt = gdn_gate_bwd(ba, al, dtb, dgb, dproj, f"{tag}_gate_bwd")
    dproj, dwconv = gdn_conv_bwd(dqkv, c, proj, wconv, dproj, f"{tag}_conv_bwd")
    dw_in = mm_tn(h, dproj, f"{tag}_proj_wgrad")
    dx, dnw = mm_nt(dproj, w_in, f"{tag}_proj_bwd", norm_bwd=(x, nw, dout))
    return dx, dnw, dw_in, dwconv, dal, ddt, dwn, dw_out


N_KV, GROUP = 4, 4
KV_COLS = 2 * N_KV * B_HD
B_COLS = D + KV_COLS


@jax.custom_vjp
def _swap_lane_halves(x):
    return pltpu.roll(x, 64, 1)


_swap_lane_halves.defvjp(lambda x: (pltpu.roll(x, 64, 1), None), lambda _, g: (pltpu.roll(g, 64, 1),))


def _swa_block(q, kp, kc, vp, vc, sk, first):
    cols = GROUP * B_BLK
    ks = lax.broadcasted_iota(jnp.int32, (N_KV, B_BLK, cols), 1)
    qi = lax.broadcasted_iota(jnp.int32, (N_KV, B_BLK, cols), 2) % B_BLK
    from_cur = ks <= qi

    def batch(parts):
        return jnp.concatenate([part[None] for part in parts], axis=0)

    def per_kv(cur, prev):
        return batch([jnp.concatenate([cur[:, j * B_HD:(j + 1) * B_HD], prev[:, j * B_HD:(j + 1) * B_HD]], axis=0)
                      for j in range(N_KV)]).astype(BF16)

    qs = batch([jnp.concatenate([q[:, hq * B_HD:(hq + 1) * B_HD] for hq in range(GROUP * j, GROUP * (j + 1))], axis=0)
                for j in range(N_KV)])
    q_t = jnp.swapaxes(qs, 1, 2).astype(BF16)
    sink = batch([jnp.concatenate([jnp.broadcast_to(sk[:, hq:hq + 1], (1, B_BLK))
                                   for hq in range(GROUP * j, GROUP * (j + 1))], axis=1) for j in range(N_KV)])
    both = _bmm(per_kv(kc, kp), q_t, B_NN)
    s = jnp.where(from_cur, both[:, :B_BLK], jnp.where(first, -1e30, both[:, B_BLK:])) * (B_HD ** -0.5)
    m = lax.stop_gradient(jnp.maximum(jnp.max(s, axis=1, keepdims=True), sink))
    e = jnp.exp((s - m).astype(BF16))
    den = jnp.sum(e.astype(F32), axis=1, keepdims=True) + jnp.exp(sink - m)
    p = e * (1.0 / den).astype(BF16)
    zero = jnp.zeros_like(p)
    p_both = jnp.concatenate([jnp.where(from_cur, p, zero), jnp.where(from_cur, zero, p)], axis=1)
    o = jnp.swapaxes(_bmm(per_kv(vc, vp), p_both, B_TN), 1, 2)
    return jnp.concatenate([o[j, g * B_BLK:(g + 1) * B_BLK] for j in range(N_KV) for g in range(GROUP)], axis=1)


def swa_core_fwd(proj, sk, name):
    T = proj.shape[0]
    half = N_KV * B_HD

    def body(q_ref, kvc_ref, kvp_ref, sk_ref, o_ref):
        kvc, kvp = kvc_ref[...], kvp_ref[...]
        o_ref[...] = _swa_block(q_ref[...], kvp[:, :half], kvc[:, :half], kvp[:, half:], kvc[:, half:], sk_ref[...],
                                pl.program_id(0) == 0).astype(BF16)

    return _pc(body, name=name, out_shape=_sds((T, D), BF16), grid=(T // B_BLK,),
               in_specs=[pl.BlockSpec((B_BLK, D), lambda n: (n, 0)),
                         pl.BlockSpec((B_BLK, KV_COLS), lambda n: (n, D // KV_COLS)),
                         pl.BlockSpec((B_BLK, KV_COLS), lambda n: (jnp.maximum(n - 1, 0), D // KV_COLS)),
                         pl.BlockSpec((1, 128), lambda n: (0, 0))],
               out_specs=pl.BlockSpec((B_BLK, D), lambda n: (n, 0)), sem=("parallel",))(proj, proj, proj, sk)


def swa_core_bwd(proj, sk, do, name):
    T = proj.shape[0]
    last = T // B_BLK - 1
    half = N_KV * B_HD

    def body(q_ref, kvc_ref, kvp_ref, sk_ref, do_ref, dproj_ref, dbias_ref, dsk_ref, carry):
        step = pl.program_id(0)
        first = step == last

        @pl.when(step == 0)
        def _():
            carry[...] = jnp.zeros_like(carry)
            dbias_ref[...] = jnp.zeros_like(dbias_ref)
            dsk_ref[...] = jnp.zeros_like(dsk_ref)
        kvc, kvp = kvc_ref[...], kvp_ref[...]
        _, vjp = jax.vjp(functools.partial(_swa_block, first=first), q_ref[...], kvp[:, :half], kvc[:, :half],
                         kvp[:, half:], kvc[:, half:], sk_ref[...])
        dq, dkp, dkc, dvp, dvc, dsk = vjp(do_ref[...])
        dkv = jnp.concatenate([dkc, dvc], axis=1) + carry[...]
        carry[...] = jnp.concatenate([dkp, dvp], axis=1)
        row = jnp.concatenate([dq, dkv], axis=1)
        dproj_ref[...] = row.astype(BF16)
        dbias_ref[...] += jnp.sum(row, axis=0, keepdims=True)
        dsk_ref[...] += dsk

    return _pc(body, name=name, out_shape=(_sds((T, B_COLS), BF16), _sds((1, B_COLS), F32), _sds((1, 128), F32)),
               grid=(T // B_BLK,),
               in_specs=[pl.BlockSpec((B_BLK, D), lambda n: (last - n, 0)),
                         pl.BlockSpec((B_BLK, KV_COLS), lambda n: (last - n, D // KV_COLS)),
                         pl.BlockSpec((B_BLK, KV_COLS), lambda n: (jnp.maximum(last - n - 1, 0), D // KV_COLS)),
                         pl.BlockSpec((1, 128), lambda n: (0, 0)), pl.BlockSpec((B_BLK, D), lambda n: (last - n, 0))],
               out_specs=(pl.BlockSpec((B_BLK, B_COLS), lambda n: (last - n, 0)),
                          pl.BlockSpec((1, B_COLS), lambda n: (0, 0)), pl.BlockSpec((1, 128), lambda n: (0, 0))),
               scratch=[pltpu.VMEM((B_BLK, KV_COLS), F32)], sem=("arbitrary",))(proj, proj, proj, sk, do)


def col_sum(a, name):
    T, N = a.shape
    tm = _tile(T, 1024)

    def body(a_ref, o_ref):
        @pl.when(pl.program_id(0) == 0)
        def _():
            o_ref[...] = jnp.zeros_like(o_ref)
        o_ref[...] += jnp.sum(a_ref[...].astype(F32), axis=0, keepdims=True)

    return _pc(body, name=name, out_shape=_sds((1, N), F32), grid=(T // tm,),
               in_specs=[pl.BlockSpec((tm, N), lambda i: (i, 0))], out_specs=pl.BlockSpec((1, N), lambda i: (0, 0)),
               sem=("arbitrary",))(a)


def swa_forward(x, nw, w_in, b_in, sk, w_out, b_out, tag):
    h = rmsnorm_bf16(x, nw, f"{tag}_norm")
    proj = mm_nn(h, w_in, f"{tag}_proj", bias=b_in)
    o = swa_core_fwd(proj, sk, f"{tag}_core")
    y = mm_nn(o, w_out, f"{tag}_out", bias=b_out, residual=x)
    return y, (x, h, proj, o)


def swa_backward(dout, saved, nw, w_in, b_in, sk, w_out, b_out, tag):
    x, h, proj, o = saved
    do = mm_nt(dout, w_out, f"{tag}_out_bwd")
    dw_out = mm_tn(o, dout, f"{tag}_out_wgrad")
    db_out = col_sum(dout, f"{tag}_out_bias_grad")
    dproj, db_in, dsk = swa_core_bwd(proj, sk, do, f"{tag}_core_bwd")
    dw_in = mm_tn(h, dproj, f"{tag}_proj_wgrad")
    dx, dnw = mm_nt(dproj, w_in, f"{tag}_proj_bwd", norm_bwd=(x, nw, dout))
    return dx, dnw, dw_in, db_in, dsk, dw_out, db_out


MESH = pl.DeviceIdType.MESH


def _position():
    return lax.axis_index("x"), lax.axis_index("y"), lax.axis_index("c")


def _slot(x, y, c):
    return 4 * x + 2 * y + c


def _peer(x, y, c, k):
    return (1 - x if k & 4 else x, 1 - y if k & 2 else y, 1 - c if k & 1 else c)


HBM_SPEC = pl.BlockSpec(memory_space=pltpu.HBM)
SEM_SPEC = pl.BlockSpec(memory_space=pltpu.SEMAPHORE)
DEP_SPEC = pl.BlockSpec(memory_space=pl.ANY)
SIDE_EFFECT = pltpu.SideEffectType.DATAFLOW_SIDE_EFFECTING
N_PEERS = N_DEV - 1


def _push_copies(srcs, lands, send_sems, recv_sems, scatter):
    x, y, c = _position()
    me = _slot(x, y, c)
    copies = []
    for k in (1, 2, 4, 3, 5, 6, 7):
        peer = _peer(x, y, c, k)
        for a in range(len(srcs)):
            copies.append(pltpu.make_async_remote_copy(
                src_ref=srcs[a].at[_slot(*peer)] if scatter else srcs[a], dst_ref=lands[a].at[me],
                send_sem=send_sems.at[N_PEERS * a + k - 1], recv_sem=recv_sems.at[N_PEERS * a + k - 1],
                device_id=peer, device_id_type=MESH))
    return copies


def push_start(srcs, lands, name, scatter, deps=()):
    n = len(srcs)
    first_out = 2 * n + len(deps)

    def body(*refs):
        for cp in _push_copies(refs[:n], refs[n:2 * n], refs[first_out], refs[first_out + 1], scatter):
            cp.start()
        refs[-1][...] = jnp.zeros_like(refs[-1])

    passed = [pltpu.HBM(t.shape, t.dtype) for t in list(srcs) + list(lands)]
    res = pl.pallas_call(
        body, name=name,
        out_shape=(pltpu.SemaphoreType.DMA((N_PEERS * n,)), pltpu.SemaphoreType.DMA((N_PEERS * n,)), *passed, _sds((8, 128), F32)),
        in_specs=[HBM_SPEC] * (2 * n) + [DEP_SPEC] * len(deps),
        out_specs=(SEM_SPEC, SEM_SPEC, *([HBM_SPEC] * (2 * n)), pl.BlockSpec(memory_space=pltpu.VMEM)),
        input_output_aliases={i: 2 + i for i in range(2 * n)},
        compiler_params=pltpu.CompilerParams(has_side_effects=SIDE_EFFECT),
    )(*[pltpu.with_memory_space_constraint(t, pltpu.HBM) for t in list(srcs) + list(lands)], *deps)
    return (res[0], res[1], list(res[2:2 + n]), list(res[2 + n:2 + 2 * n])), res[-1]


def push_wait(handles, after, name, scatter):
    send_sems, recv_sems, srcs, lands = handles
    n = len(srcs)
    after = tuple(after) if isinstance(after, (tuple, list)) else (after,)

    def body(*refs):
        for cp in _push_copies(refs[:n], refs[n:2 * n], refs[2 * n], refs[2 * n + 1], scatter):
            cp.wait_send()
            cp.wait_recv()

    res = pl.pallas_call(
        body, name=name, out_shape=tuple(pltpu.HBM(t.shape, t.dtype) for t in srcs + lands),
        in_specs=[HBM_SPEC] * (2 * n) + [SEM_SPEC, SEM_SPEC] + [DEP_SPEC] * len(after), out_specs=tuple([HBM_SPEC] * (2 * n)),
        input_output_aliases={i: i for i in range(2 * n)},
        compiler_params=pltpu.CompilerParams(has_side_effects=SIDE_EFFECT),
    )(*srcs, *lands, send_sems, recv_sems, *after)
    return list(res[n:])


def gather_start(shards, name, deps=()):
    me = _slot(*_position())
    lands = [lax.dynamic_update_slice(lax.empty((N_DEV,) + t.shape, t.dtype), t[None], (me,) + (0,) * t.ndim) for t in shards]
    return push_start(shards, lands, name, scatter=False, deps=deps)


def exchange_start(parts, name):
    me = _slot(*_position())
    lands = [lax.dynamic_update_slice(lax.empty(t.shape, t.dtype), lax.dynamic_index_in_dim(t, me, 0, keepdims=True),
                                      (me,) + (0,) * (t.ndim - 1)) for t in parts]
    return push_start(parts, lands, name, scatter=True)


def _row_tile(rows, cols):
    best = rows
    for t in range(16, rows, 16):
        if rows % t == 0 and t * cols * 4 <= (1 << 20):
            best = t
    return best


def adam_update(parts, w, m, v, name):
    n_layers = len(parts)
    P, R, C = parts[0].shape
    tr = _row_tile(R, C)
    n_t = R // tr

    def body(*refs):
        p_refs = refs[:n_layers]
        w_ref, m_ref, v_ref, g_ref, d_ref, nm_ref, nv_ref = refs[n_layers:]
        for layer in range(n_layers):
            @pl.when(pl.program_id(0) == layer)
            def _(p_ref=p_refs[layer]):
                g = p_ref[0].astype(F32)
                for s in range(1, P):
                    g = g + p_ref[s].astype(F32)
                new_m = ADAM_B1 * m_ref[0] + (1.0 - ADAM_B1) * g
                new_v = ADAM_B2 * v_ref[0] + (1.0 - ADAM_B2) * (g * g)
                m_hat = new_m / (1.0 - ADAM_B1 ** ADAM_STEP)
                v_hat = new_v / (1.0 - ADAM_B2 ** ADAM_STEP)
                g_ref[0] = g
                d_ref[0] = -ADAM_LR * (m_hat / (jnp.sqrt(v_hat) + ADAM_EPS) + ADAM_WD * w_ref[0])
                nm_ref[0] = new_m
                nv_ref[0] = new_v

    def part_spec(layer):
        return pl.BlockSpec((P, tr, C), lambda l_, i: (0, jnp.where(l_ == layer, i, jnp.where(l_ < layer, 0, n_t - 1)), 0))

    blk = pl.BlockSpec((1, tr, C), lambda l_, i: (l_, i, 0))
    out = _sds((n_layers, R, C), F32)
    return _pc(body, name=name, out_shape=(out, out, out, out), grid=(n_layers, n_t),
               in_specs=[part_spec(layer) for layer in range(n_layers)] + [blk, blk, blk],
               out_specs=(blk, blk, blk, blk), sem=("arbitrary", "arbitrary"))(*parts, w, m, v)


WEIGHTS = ("ffn1_norm", "ffn1_w_gu", "ffn1_w_down", "mix_norm", "ffn2_norm", "ffn2_w_gu", "ffn2_w_down", "a_w_in",
           "a_w_conv", "a_A_log", "a_dt_bias", "a_out_norm", "a_w_out", "b_w_in", "b_b_in", "b_sinks", "b_w_out",
           "b_b_out", "final_norm")
SHARDED = ("ffn1_w_gu", "ffn1_w_down", "ffn2_w_gu", "ffn2_w_down", "a_w_in", "a_w_conv", "a_w_out", "b_w_in", "b_b_in",
           "b_w_out", "b_b_out")
MISC_LANES = dict(a_A_log=(0, 8), a_dt_bias=(8, 16), b_sinks=(16, 32), a_out_norm=(128, 256))
LOSS_LANE = 256


def _pack_small(t):
    misc = jnp.zeros((D,), F32)
    for key, (lo, hi) in MISC_LANES.items():
        misc = misc.at[lo:hi].set(t[key].reshape(-1))
    if "loss" in t:
        misc = misc.at[LOSS_LANE].set(t["loss"])
    return jnp.concatenate([t["ffn1_norm"], t["mix_norm"], t["ffn2_norm"], t["final_norm"].reshape(1, D), misc[None]], axis=0)


def _unpack_small(p, like):
    out = dict(ffn1_norm=p[0:2], mix_norm=p[2:4], ffn2_norm=p[4:6], final_norm=p[6])
    for key, (lo, hi) in MISC_LANES.items():
        out[key] = p[7, lo:hi].reshape(like[key].shape)
    return out


def kernel(x, ffn1_norm, ffn1_w_gu, ffn1_w_down, mix_norm, ffn2_norm, ffn2_w_gu, ffn2_w_down, a_w_in, a_w_conv, a_A_log, a_dt_bias, a_out_norm, a_w_out, b_w_in, b_b_in, b_sinks, b_w_out, b_b_out, final_norm, loss_target, m_ffn1_norm, m_ffn1_w_gu, m_ffn1_w_down, m_mix_norm, m_ffn2_norm, m_ffn2_w_gu, m_ffn2_w_down, m_a_w_in, m_a_w_conv, m_a_A_log, m_a_dt_bias, m_a_out_norm, m_a_w_out, m_b_w_in, m_b_b_in, m_b_sinks, m_b_w_out, m_b_b_out, m_final_norm, v_ffn1_norm, v_ffn1_w_gu, v_ffn1_w_down, v_mix_norm, v_ffn2_norm, v_ffn2_w_gu, v_ffn2_w_down, v_a_w_in, v_a_w_conv, v_a_A_log, v_a_dt_bias, v_a_out_norm, v_a_w_out, v_b_w_in, v_b_b_in, v_b_sinks, v_b_w_out, v_b_b_out, v_final_norm):
    w = dict(ffn1_norm=ffn1_norm, ffn1_w_gu=ffn1_w_gu, ffn1_w_down=ffn1_w_down, mix_norm=mix_norm, ffn2_norm=ffn2_norm, ffn2_w_gu=ffn2_w_gu, ffn2_w_down=ffn2_w_down, a_w_in=a_w_in, a_w_conv=a_w_conv, a_A_log=a_A_log, a_dt_bias=a_dt_bias, a_out_norm=a_out_norm, a_w_out=a_w_out, b_w_in=b_w_in, b_b_in=b_b_in, b_sinks=b_sinks, b_w_out=b_w_out, b_b_out=b_b_out, final_norm=final_norm)
    m = dict(ffn1_norm=m_ffn1_norm, ffn1_w_gu=m_ffn1_w_gu, ffn1_w_down=m_ffn1_w_down, mix_norm=m_mix_norm, ffn2_norm=m_ffn2_norm, ffn2_w_gu=m_ffn2_w_gu, ffn2_w_down=m_ffn2_w_down, a_w_in=m_a_w_in, a_w_conv=m_a_w_conv, a_A_log=m_a_A_log, a_dt_bias=m_a_dt_bias, a_out_norm=m_a_out_norm, a_w_out=m_a_w_out, b_w_in=m_b_w_in, b_b_in=m_b_b_in, b_sinks=m_b_sinks, b_w_out=m_b_w_out, b_b_out=m_b_b_out, final_norm=m_final_norm)
    v = dict(ffn1_norm=v_ffn1_norm, ffn1_w_gu=v_ffn1_w_gu, ffn1_w_down=v_ffn1_w_down, mix_norm=v_mix_norm, ffn2_norm=v_ffn2_norm, ffn2_w_gu=v_ffn2_w_gu, ffn2_w_down=v_ffn2_w_down, a_w_in=v_a_w_in, a_w_conv=v_a_w_conv, a_A_log=v_a_A_log, a_dt_bias=v_a_dt_bias, a_out_norm=v_a_out_norm, a_w_out=v_a_w_out, b_w_in=v_b_w_in, b_b_in=v_b_b_in, b_sinks=v_b_sinks, b_w_out=v_b_w_out, b_b_out=v_b_b_out, final_norm=v_final_norm)
    T = x.shape[1]
    x0, tgt = x.reshape(T, D), loss_target.reshape(T, D)

    def cast(t):
        return t.astype(BF16)

    h0, t0 = gather_start([cast(ffn1_w_gu[0])], "gather0_start")
    a_log_row = jnp.zeros((1, 128), F32).at[0, HEADS_A:2 * HEADS_A].set(a_A_log[0])
    dt_row = jnp.zeros((1, 128), F32).at[0, HEADS_A:2 * HEADS_A].set(a_dt_bias[0])
    sink_row = jnp.zeros((1, 128), F32).at[0, :b_sinks.shape[1]].set(b_sinks[0])
    a_in_cols = a_w_in.shape[-1] * N_DEV

    def down_blocks(t):
        return t.reshape(N_FB, FB, D)

    wgu, wdn, saved = {}, {}, []
    xn = rmsnorm_bf16(x0, ffn1_norm[0:1], "l0_ffn1_norm", (t0,))
    wgu["ffn1", 0] = push_wait(h0, xn, "gather0_wait", scatter=False)[0]
    h0d, t0d = gather_start([cast(ffn1_w_down[0])], "gather0d_start", deps=(wgu["ffn1", 0],))
    h1, t1 = gather_start([cast(a_w_in[0]), a_w_conv[0], cast(a_w_out[0])], "gather1_start", deps=(t0d,))
    gu = ffn_up(xn, wgu["ffn1", 0], "l0_ffn1_up", deps=(t0d, t1))
    wdn["ffn1", 0] = down_blocks(push_wait(h0d, gu, "gather0d_wait", scatter=False)[0])
    xs, s1 = ffn_down(gu, wdn["ffn1", 0], x0, "l0_ffn1_down"), (x0, xn, gu)
    got = push_wait(h1, xs, "gather1_wait", scatter=False)
    h1f, t1f = gather_start([cast(ffn2_w_gu[0]), cast(ffn2_w_down[0])], "gather1f_start", deps=(got[0],))
    g2 = [cast(ffn1_w_gu[1]), cast(ffn1_w_down[1]), cast(b_w_in[0]), b_b_in, cast(b_w_out[0]), b_b_out,
          cast(ffn2_w_gu[1]), cast(ffn2_w_down[1])]
    h2, t2 = gather_start(g2, "gather2_start", deps=(t1f,))
    a_in_full = jnp.pad(got[0].transpose(1, 0, 2).reshape(D, a_in_cols), ((0, 0), (0, A_COLS - a_in_cols)))
    gdn_args = (mix_norm[0:1], a_in_full, got[1].transpose(1, 0, 2).reshape(4, 3 * D), a_log_row, dt_row, a_out_norm,
                got[2].reshape(D, D))
    xs, sm = gdn_forward(xs, *gdn_args, "gdn", deps=(t1f, t2))
    got = push_wait(h1f, xs, "gather1f_wait", scatter=False)
    wgu["ffn2", 0], wdn["ffn2", 0] = got[0], down_blocks(got[1])
    xs, s2 = ffn_forward(xs, ffn2_norm[0:1], wgu["ffn2", 0], wdn["ffn2", 0], "l0_ffn2")
    saved.append((s1, sm, s2))
    got = push_wait(h2, xs, "gather2_wait", scatter=False)
    wgu["ffn1", 1], wdn["ffn1", 1] = got[0], down_blocks(got[1])
    swa_args = (mix_norm[1:2], got[2].transpose(1, 0, 2).reshape(D, B_COLS), got[3].reshape(1, B_COLS), sink_row,
                got[4].reshape(D, D), got[5].reshape(1, D))
    wgu["ffn2", 1], wdn["ffn2", 1] = got[6], down_blocks(got[7])
    xs, s1 = ffn_forward(xs, ffn1_norm[1:2], wgu["ffn1", 1], wdn["ffn1", 1], "l1_ffn1")
    xs, sm = swa_forward(xs, *swa_args, "swa")
    xs, s2 = ffn_forward(xs, ffn2_norm[1:2], wgu["ffn2", 1], wdn["ffn2", 1], "l1_ffn2")
    saved.append((s1, sm, s2))
    loss_row, dx, d_final_norm = final_loss(xs, final_norm.reshape(1, D), tgt, "final_loss")

    def down_slots(t):
        return cast(t.reshape(N_DEV, FB // 2, D))

    def col_slots(t, dtype=BF16):
        return t.reshape(t.shape[0], N_DEV, -1).transpose(1, 0, 2).astype(dtype)

    d_norm = {"ffn1_norm": [None, None], "mix_norm": [None, None], "ffn2_norm": [None, None]}
    exchanges = {}

    def sender(tag):
        def on_grads(d_gu, d_dn):
            exchanges[tag], token = exchange_start([cast(d_gu), down_slots(d_dn)], f"exchange_{tag}_start")
            return (token,)
        return on_grads

    s1, sm, s2 = saved[1]
    dx, d_norm["ffn2_norm"][1], _, _ = ffn_backward(dx, s2, ffn2_norm[1:2], wgu["ffn2", 1], wdn["ffn2", 1], "l1_ffn2",
                                                    on_grads=sender("l1_ffn2"))
    dx, d_norm["mix_norm"][1], d_b_in, d_b_bias_in, d_sinks, d_b_out, d_b_bias_out = swa_backward(dx, sm, *swa_args, "swa")
    exchanges["swa"], t_swa = exchange_start(
        [col_slots(d_b_in), d_b_bias_in.reshape(N_DEV, 1, -1), cast(d_b_out.reshape(N_DEV, D // N_DEV, D)),
         d_b_bias_out.reshape(N_DEV, 1, -1)], "exchange_swa_start")
    dx, d_norm["ffn1_norm"][1], _, _ = ffn_backward(dx, s1, ffn1_norm[1:2], wgu["ffn1", 1], wdn["ffn1", 1], "l1_ffn1",
                                                    deps=(t_swa,), on_grads=sender("l1_ffn1"))

    s1, sm, s2 = saved[0]
    dx, d_norm["ffn2_norm"][0], _, _ = ffn_backward(dx, s2, ffn2_norm[0:1], wgu["ffn2", 0], wdn["ffn2", 0], "l0_ffn2",
                                                    on_grads=sender("l0_ffn2"))
    dx, d_norm["mix_norm"][0], d_a_in, d_a_conv, d_alog, d_dt, d_onorm, d_a_out = gdn_backward(dx, sm, *gdn_args, "gdn")
    exchanges["gdn"], t_gdn = exchange_start(
        [col_slots(d_a_in[:, :a_in_cols]), col_slots(d_a_conv, F32), cast(d_a_out.reshape(N_DEV, D // N_DEV, D))],
        "exchange_gdn_start")
    dx, d_norm["ffn1_norm"][0], _, _ = ffn_backward(dx, s1, ffn1_norm[0:1], wgu["ffn1", 0], wdn["ffn1", 0], "l0_ffn1",
                                                    deps=(t_gdn,), on_grads=sender("l0_ffn1"))
    grad_x = dx.reshape(x.shape)
    got = {tag: push_wait(exchanges[tag], dx, f"exchange_{tag}_wait", scatter=True)
           for tag in ("l1_ffn2", "swa", "l1_ffn1", "l0_ffn2", "gdn")}
    received = dict(ffn2_w_gu=[got["l0_ffn2"][0], got["l1_ffn2"][0]], ffn2_w_down=[got["l0_ffn2"][1], got["l1_ffn2"][1]],
                    b_w_in=[got["swa"][0]], b_b_in=[got["swa"][1]], b_w_out=[got["swa"][2]], b_b_out=[got["swa"][3]],
                    a_w_in=[got["gdn"][0]], a_w_conv=[got["gdn"][1]], a_w_out=[got["gdn"][2]])

    grads, deltas, new_m, new_v = {}, {}, {}, {}

    def update(key):
        shape = w[key].shape
        cols = shape[-1]
        layers = lambda t: t.reshape(shape[0], -1, cols)
        out = adam_update([r.reshape(N_DEV, -1, cols) for r in received[key]], layers(w[key]), layers(m[key]), layers(v[key]),
                          f"adam_{key}")
        grads[key], deltas[key], new_m[key], new_v[key] = (t.reshape(shape) for t in out)

    for key in SHARDED:
        if key in received:
            update(key)
    done_first = [deltas[key] for key in received]

    small = dict(ffn1_norm=jnp.concatenate(d_norm["ffn1_norm"], axis=0), mix_norm=jnp.concatenate(d_norm["mix_norm"], axis=0),
                 ffn2_norm=jnp.concatenate(d_norm["ffn2_norm"], axis=0), final_norm=d_final_norm,
                 a_A_log=d_alog[0, HEADS_A:2 * HEADS_A], a_dt_bias=d_dt[0, HEADS_A:2 * HEADS_A],
                 b_sinks=d_sinks[0, :b_sinks.shape[1]], a_out_norm=d_onorm, loss=loss_row[0, 0])
    hs, ts = gather_start([_pack_small(small)], "gather_small_start")
    r3 = push_wait(exchanges["l0_ffn1"], done_first + [ts], "exchange_l0_ffn1_wait", scatter=True)
    received.update(ffn1_w_gu=[r3[0], got["l1_ffn1"][0]], ffn1_w_down=[r3[1], got["l1_ffn1"][1]])
    update("ffn1_w_gu")
    update("ffn1_w_down")
    every = push_wait(hs, deltas["ffn1_w_down"], "gather_small_wait", scatter=False)[0]
    out = adam_update([every], _pack_small(w)[None], _pack_small(m)[None], _pack_small(v)[None], "adam_small")
    for dst, packed in zip((grads, deltas, new_m, new_v), out):
        dst.update(_unpack_small(packed[0], w))
    loss = out[0][0, 7, LOSS_LANE]

    return (loss, grad_x, *[grads[k_] for k_ in WEIGHTS], *[deltas[k_] for k_ in WEIGHTS],
            *[new_m[k_] for k_ in WEIGHTS], *[new_v[k_] for k_ in WEIGHTS])
```

```python
import functools

import jax
import jax.numpy as jnp
from jax import lax
from jax.experimental import pallas as pl
from jax.experimental.pallas import tpu as pltpu

F32, BF16 = jnp.float32, jnp.bfloat16
HI = lax.Precision.HIGHEST
EPS = 1e-6

N_DEV = 8
D = 1024
FB = 704
N_FB = 4
HEADS_A, DK = 8, 128
CHUNK = 64
PREP_T = 512
A_COLS = 4224
B_HD, B_BLK = 64, 128
VMEM_LIMIT_V7X = 60 * 1024 * 1024

ADAM_LR, ADAM_B1, ADAM_B2, ADAM_EPS, ADAM_WD, ADAM_STEP = 0.001, 0.9, 0.999, 1e-08, 0.01, 10

NT = (((1,), (1,)), ((), ()))
TN = (((0,), (0,)), ((), ()))


def _pc(body, *, name, out_shape, grid=(), in_specs=None, out_specs=None, scratch=(), sem=None, **kw):
    params = pltpu.CompilerParams(dimension_semantics=sem, vmem_limit_bytes=VMEM_LIMIT_V7X)
    return pl.pallas_call(body, name=name, out_shape=out_shape, grid=grid, in_specs=in_specs, out_specs=out_specs,
                          scratch_shapes=list(scratch), compiler_params=params, **kw)


def _sds(shape, dtype):
    return jax.ShapeDtypeStruct(tuple(shape), dtype)


def _dot(a, b, dims=None, precision=None):
    if dims is None:
        return jnp.dot(a, b, preferred_element_type=F32, precision=precision)
    return lax.dot_general(a, b, dims, preferred_element_type=F32, precision=precision)


def _sigmoid(x):
    return 1.0 / (1.0 + jnp.exp(-x))


def _softplus(x):
    return jnp.maximum(x, 0.0) + jnp.log(1.0 + jnp.exp(-jnp.abs(x)))


def _rms_fwd(x, w):
    r = lax.rsqrt(jnp.mean(x * x, axis=-1, keepdims=True) + EPS)
    return x * r * w


def _rms_bwd(x, w, dy):
    r = lax.rsqrt(jnp.mean(x * x, axis=-1, keepdims=True) + EPS)
    xh = x * r
    dxh = dy * w
    dx = r * (dxh - xh * jnp.mean(dxh * xh, axis=-1, keepdims=True))
    return dx, jnp.sum(dy * xh, axis=0, keepdims=True)


def _tile(n, want):
    t = min(n, want)
    assert n % t == 0, (n, want)
    return t


def rmsnorm_bf16(x, w, name, deps=()):
    T = x.shape[0]
    tm = _tile(T, 1024)

    def body(x_ref, w_ref, *rest):
        rest[-1][...] = _rms_fwd(x_ref[...], w_ref[...]).astype(BF16)

    return _pc(body, name=name, out_shape=_sds((T, D), BF16), grid=(T // tm,),
               in_specs=[pl.BlockSpec((tm, D), lambda i: (i, 0)), pl.BlockSpec((1, D), lambda i: (0, 0))] + [DEP_SPEC] * len(deps),
               out_specs=pl.BlockSpec((tm, D), lambda i: (i, 0)), sem=("parallel",))(x, w, *deps)


def final_loss(x, w, tgt, name):
    T = x.shape[0]
    tm = _tile(T, 512)

    def body(x_ref, w_ref, t_ref, loss_ref, dx_ref, dw_ref):
        xv, wv = x_ref[...], w_ref[...]
        err = _rms_fwd(xv, wv) - t_ref[...]
        dx, dw = _rms_bwd(xv, wv, err * (1.0 / D))
        dx_ref[...] = dx

        @pl.when(pl.program_id(0) == 0)
        def _():
            dw_ref[...] = jnp.zeros_like(dw_ref)
            loss_ref[...] = jnp.zeros_like(loss_ref)
        dw_ref[...] += dw
        loss_ref[...] += jnp.full((1, 128), 0.5 / D, F32) * jnp.sum(err * err)

    row = pl.BlockSpec((tm, D), lambda i: (i, 0))
    vec = pl.BlockSpec((1, D), lambda i: (0, 0))
    return _pc(body, name=name, out_shape=(_sds((1, 128), F32), _sds((T, D), F32), _sds((1, D), F32)),
               grid=(T // tm,), in_specs=[row, vec, row],
               out_specs=(pl.BlockSpec((1, 128), lambda i: (0, 0)), row, vec), sem=("arbitrary",))(x, w, tgt)


def _col_tile(n):
    for t in (1536, 1408, 1024, 768, 512, 384, 256, 128):
        if n % t == 0:
            return t
    return n


def mm_nn(a, b, name, bias=None, residual=None, out_dtype=F32, cols=None):
    T, K = a.shape
    first, end = cols or (0, b.shape[1])
    N = end - first
    tm, tn = _tile(T, 512), _col_tile(N)
    assert first % tn == 0 and (cols is None or (bias is None and residual is None))
    j0 = first // tn

    def body(a_ref, b_ref, *rest):
        o_ref = rest[-1]
        acc = _dot(a_ref[...].astype(BF16), b_ref[...])
        for extra in rest[:-1]:
            acc = acc + extra[...]
        o_ref[...] = acc.astype(out_dtype)

    in_specs = [pl.BlockSpec((tm, K), lambda j, i: (i, 0)), pl.BlockSpec((K, tn), lambda j, i: (0, j0 + j))]
    args = [a, b]
    if bias is not None:
        in_specs.append(pl.BlockSpec((1, tn), lambda j, i: (0, j)))
        args.append(bias)
    if residual is not None:
        in_specs.append(pl.BlockSpec((tm, tn), lambda j, i: (i, j)))
        args.append(residual)
    return _pc(body, name=name, out_shape=_sds((T, N), out_dtype), grid=(N // tn, T // tm), in_specs=in_specs,
               out_specs=pl.BlockSpec((tm, tn), lambda j, i: (i, j)), sem=("parallel", "parallel"))(*args)


def mm_nt(a, b, name, out_dtype=F32, norm_bwd=None):
    T, N = a.shape
    K = b.shape[0]
    tm = _tile(T, 512)
    row = pl.BlockSpec((tm, K), lambda i: (i, 0))
    in_specs = [pl.BlockSpec((tm, N), lambda i: (i, 0)), _resident((K, N))]

    if norm_bwd is None:
        def body(a_ref, b_ref, o_ref):
            o_ref[...] = _dot(a_ref[...].astype(BF16), b_ref[...], NT).astype(out_dtype)

        return _pc(body, name=name, out_shape=_sds((T, K), out_dtype), grid=(T // tm,), in_specs=in_specs,
                   out_specs=row, sem=("parallel",))(a, b)

    def body(a_ref, b_ref, x_ref, w_ref, dres_ref, dx_ref, dw_ref):
        dx, dw = _rms_bwd(x_ref[...], w_ref[...], _dot(a_ref[...].astype(BF16), b_ref[...], NT))
        dx_ref[...] = dres_ref[...] + dx

        @pl.when(pl.program_id(0) == 0)
        def _():
            dw_ref[...] = jnp.zeros_like(dw_ref)
        dw_ref[...] += dw

    vec = pl.BlockSpec((1, K), lambda i: (0, 0))
    return _pc(body, name=name, out_shape=(_sds((T, K), F32), _sds((1, K), F32)), grid=(T // tm,),
               in_specs=in_specs + [row, vec, row], out_specs=(row, vec), sem=("arbitrary",))(a, b, *norm_bwd)


def mm_tn(a, b, name):
    T, K = a.shape
    N = b.shape[1]
    tt, tn = _tile(T, 1024), _col_tile(N)

    def body(a_ref, b_ref, o_ref):
        @pl.when(pl.program_id(1) == 0)
        def _():
            o_ref[...] = jnp.zeros_like(o_ref)
        o_ref[...] += _dot(a_ref[...].astype(BF16), b_ref[...].astype(BF16), TN)

    return _pc(body, name=name, out_shape=_sds((K, N), F32), grid=(N // tn, T // tt),
               in_specs=[pl.BlockSpec((tt, K), lambda j, t: (t, 0)), pl.BlockSpec((tt, tn), lambda j, t: (t, j))],
               out_specs=pl.BlockSpec((K, tn), lambda j, t: (0, j)), sem=("parallel", "arbitrary"))(a, b)


def ffn_up(xn, wgu, name, deps=()):
    T = xn.shape[0]
    tm = _tile(T, 1024)

    def body(x_ref, w_ref, *rest):
        xv = x_ref[...]
        for j in range(2 * N_FB):
            rest[-1][j] = _dot(xv, w_ref[j]).astype(BF16)

    return _pc(body, name=name, out_shape=_sds((2 * N_FB, T, FB), BF16), grid=(T // tm,),
               in_specs=[pl.BlockSpec((tm, D), lambda i: (i, 0)), _resident((2 * N_FB, D, FB))] + [DEP_SPEC] * len(deps),
               out_specs=pl.BlockSpec((2 * N_FB, tm, FB), lambda i: (0, i, 0)), sem=("parallel",))(xn, wgu, *deps)


def ffn_down(gu, wd, x, name):
    T = x.shape[0]
    tm = _tile(T, 512)

    def body(gu_ref, w_ref, x_ref, o_ref):
        acc = jnp.zeros((tm, D), F32)
        for g in range(N_FB):
            gate, up = gu_ref[g], gu_ref[N_FB + g]
            acc = acc + _dot(gate * _sigmoid(gate) * up, w_ref[g])
        o_ref[...] = x_ref[...] + 0.5 * acc

    row = pl.BlockSpec((tm, D), lambda i: (i, 0))
    return _pc(body, name=name, out_shape=_sds((T, D), F32), grid=(T // tm,),
               in_specs=[pl.BlockSpec((2 * N_FB, tm, FB), lambda i: (0, i, 0)),
                         _resident((N_FB, FB, D)), row],
               out_specs=row, sem=("parallel",))(gu, wd, x)


def _resident(shape):
    return pl.BlockSpec(shape, lambda *_: (0,) * len(shape), pipeline_mode=pl.Buffered(1))


def _store_blocks_bf16(acc, out_hbm, stage, sem):
    for j in range(acc.shape[0]):
        stage[...] = acc[j].astype(BF16)
        copy = pltpu.make_async_copy(stage, out_hbm.at[j], sem)
        copy.start()
        copy.wait()


def ffn_bwd_hidden(dout, wd, gu, name, deps=()):
    T = dout.shape[0]
    tm = _tile(T, 512)
    n_t = T // tm

    def body(d_ref, w_ref, gu_ref, *rest):
        dgu_ref, dwd_hbm, acc, stage, sem = rest[-5:]
        t = pl.program_id(0)

        @pl.when(t == 0)
        def _():
            acc[...] = jnp.zeros_like(acc)
        dy = (0.5 * d_ref[...]).astype(BF16)
        for g in range(N_FB):
            gate, up = gu_ref[g], gu_ref[N_FB + g]
            sg = _sigmoid(gate)
            silu = gate * sg
            dact = _dot(dy, w_ref[g], NT).astype(BF16)
            acc[g] += _dot(silu * up, dy, TN)
            dgu_ref[g] = dact * up * (sg * (1.0 + gate * (1.0 - sg)))
            dgu_ref[N_FB + g] = dact * silu

        @pl.when(t == n_t - 1)
        def _():
            _store_blocks_bf16(acc, dwd_hbm, stage, sem)

    return _pc(body, name=name, out_shape=(_sds((2 * N_FB, T, FB), BF16), _sds((N_FB, FB, D), BF16)), grid=(n_t,),
               in_specs=[pl.BlockSpec((tm, D), lambda i: (i, 0)), _resident((N_FB, FB, D)),
                         pl.BlockSpec((2 * N_FB, tm, FB), lambda i: (0, i, 0))] + [DEP_SPEC] * len(deps),
               out_specs=(pl.BlockSpec((2 * N_FB, tm, FB), lambda i: (0, i, 0)), pl.BlockSpec(memory_space=pl.ANY)),
               scratch=[pltpu.VMEM((N_FB, FB, D), F32), pltpu.VMEM((FB, D), BF16), pltpu.SemaphoreType.DMA],
               sem=("arbitrary",))(dout, wd, gu, *deps)


def ffn_bwd_input(dgu, wgu, x, dout, nw, name, deps=()):
    T = x.shape[0]
    tm = _tile(T, 512)

    def body(dgu_ref, w_ref, x_ref, d_ref, nw_ref, *rest):
        dx_ref, dnw_ref = rest[-2:]
        dxn = jnp.zeros((tm, D), F32)
        for j in range(2 * N_FB):
            dxn = dxn + _dot(dgu_ref[j], w_ref[j], NT)
        dx, dw = _rms_bwd(x_ref[...], nw_ref[...], dxn)
        dx_ref[...] = d_ref[...] + dx

        @pl.when(pl.program_id(0) == 0)
        def _():
            dnw_ref[...] = jnp.zeros_like(dnw_ref)
        dnw_ref[...] += dw

    row = pl.BlockSpec((tm, D), lambda i: (i, 0))
    vec = pl.BlockSpec((1, D), lambda i: (0, 0))
    return _pc(body, name=name, out_shape=(_sds((T, D), F32), _sds((1, D), F32)), grid=(T // tm,),
               in_specs=[pl.BlockSpec((2 * N_FB, tm, FB), lambda i: (0, i, 0)), _resident((2 * N_FB, D, FB)),
                         row, row, vec] + [DEP_SPEC] * len(deps),
               out_specs=(row, vec), sem=("arbitrary",))(dgu, wgu, x, dout, nw, *deps)


def ffn_wgrad_gu(xn, dgu, name):
    T = xn.shape[0]
    tt = _tile(T, 1024)
    n_t = T // tt

    def body(x_ref, d_ref, dw_hbm, acc, stage, sem):
        t = pl.program_id(0)

        @pl.when(t == 0)
        def _():
            acc[...] = jnp.zeros_like(acc)
        xn_tile = x_ref[...]
        for j in range(2 * N_FB):
            acc[j] += _dot(xn_tile, d_ref[j], TN)

        @pl.when(t == n_t - 1)
        def _():
            _store_blocks_bf16(acc, dw_hbm, stage, sem)

    return _pc(body, name=name, out_shape=_sds((2 * N_FB, D, FB), BF16), grid=(n_t,),
               in_specs=[pl.BlockSpec((tt, D), lambda t: (t, 0)), pl.BlockSpec((2 * N_FB, tt, FB), lambda t: (0, t, 0))],
               out_specs=pl.BlockSpec(memory_space=pl.ANY),
               scratch=[pltpu.VMEM((2 * N_FB, D, FB), F32), pltpu.VMEM((D, FB), BF16), pltpu.SemaphoreType.DMA],
               sem=("arbitrary",))(xn, dgu)


def ffn_forward(x, nw, wgu, wd, tag):
    T = x.shape[0]
    tm = _tile(T, 512)

    def body(x_ref, nw_ref, wgu_ref, wd_ref, o_ref, xn_ref, gu_ref):
        xv = x_ref[...]
        xn = _rms_fwd(xv, nw_ref[...]).astype(BF16)
        xn_ref[...] = xn
        for j in range(2 * N_FB):
            gu_ref[j] = _dot(xn, wgu_ref[j]).astype(BF16)
        acc = jnp.zeros((tm, D), F32)
        for g in range(N_FB):
            gate, up = gu_ref[g], gu_ref[N_FB + g]
            acc = acc + _dot(gate * _sigmoid(gate) * up, wd_ref[g])
        o_ref[...] = xv + 0.5 * acc

    row = pl.BlockSpec((tm, D), lambda i: (i, 0))
    out, xn, gu = _pc(body, name=f"{tag}_fwd",
                      out_shape=(_sds((T, D), F32), _sds((T, D), BF16), _sds((2 * N_FB, T, FB), BF16)), grid=(T // tm,),
                      in_specs=[row, pl.BlockSpec((1, D), lambda i: (0, 0)), _resident((2 * N_FB, D, FB)),
                                _resident((N_FB, FB, D))],
                      out_specs=(row, row, pl.BlockSpec((2 * N_FB, tm, FB), lambda i: (0, i, 0))),
                      sem=("parallel",))(x, nw, wgu, wd)
    return out, (x, xn, gu)


def ffn_backward(dout, saved, nw, wgu, wd, tag, deps=(), on_grads=None):
    x, xn, gu = saved
    dgu, dwd = ffn_bwd_hidden(dout, wd, gu, f"{tag}_bwd_hidden", deps)
    dwgu = ffn_wgrad_gu(xn, dgu, f"{tag}_wgrad_gu")
    late = on_grads(dwgu, dwd) if on_grads else ()
    dx, dnw = ffn_bwd_input(dgu, wgu, x, dout, nw, f"{tag}_bwd_input", late)
    return dx, dnw, dwgu, dwd


N_QKV_BLK = 3 * HEADS_A
Z_BLK0 = N_QKV_BLK
MAIN_COLS = 4 * D
HALO = 16


def _conv_taps(xcat, w):
    c = xcat[HALO:] * w[3:4]
    for k in range(3):
        c = c + pltpu.roll(xcat, 3 - k, 0)[HALO:] * w[k:k + 1]
    return c


def _head_cols(h):
    return slice(128 * h, 128 * (h + 1))


def gdn_conv_fwd(proj, wconv, name):
    T = proj.shape[0]
    tm = _tile(T, 512)

    def body(cur_ref, prev_ref, w_ref, c_ref, y_ref):
        kind, t = pl.program_id(0), pl.program_id(1)
        prev = jnp.where(t > 0, prev_ref[...].astype(F32), 0.0)
        c = _conv_taps(jnp.concatenate([prev, cur_ref[...].astype(F32)], axis=0), w_ref[...])
        c_ref[...] = c.astype(BF16)
        s = c * _sigmoid(c)
        scale = jnp.where(kind == 0, DK ** -0.5, 1.0)
        for h in range(HEADS_A):
            sh = s[:, _head_cols(h)]
            r = lax.rsqrt(jnp.sum(sh * sh, axis=-1, keepdims=True) + EPS)
            y_ref[h] = (sh * jnp.where(kind < 2, r * scale, 1.0)).astype(BF16)

    return _pc(body, name=name, out_shape=(_sds((T, 3 * D), BF16), _sds((N_QKV_BLK, T, 128), BF16)),
               grid=(3, T // tm),
               in_specs=[pl.BlockSpec((tm, D), lambda kd, t: (t, kd)),
                         pl.BlockSpec((HALO, D), lambda kd, t: (jnp.maximum(t * (tm // HALO) - 1, 0), kd)),
                         pl.BlockSpec((4, D), lambda kd, t: (0, kd))],
               out_specs=(pl.BlockSpec((tm, D), lambda kd, t: (t, kd)),
                          pl.BlockSpec((HEADS_A, tm, 128), lambda kd, t: (kd, t, 0))),
               sem=("parallel", "parallel"))(proj, proj, wconv)


def gdn_conv_bwd(dqkv, c, proj, wconv, dproj, name):
    T = c.shape[0]
    tm = _tile(T, 512)
    n_t = T // tm

    def body(dy_ref, dyn_ref, c_ref, cn_ref, x_ref, w_ref, _, dx_ref, dw_ref):
        kind, t = pl.program_id(0), pl.program_id(1)
        scale = jnp.where(kind == 0, DK ** -0.5, 1.0)

        def act_bwd(dy, cv):
            sg = _sigmoid(cv)
            s = cv * sg
            parts = []
            for h in range(HEADS_A):
                sh, dyh = s[:, _head_cols(h)], dy[h]
                r = lax.rsqrt(jnp.sum(sh * sh, axis=-1, keepdims=True) + EPS)
                ds_norm = scale * r * (dyh - (r * r) * sh * jnp.sum(dyh * sh, axis=-1, keepdims=True))
                parts.append(jnp.where(kind < 2, ds_norm, dyh))
            return jnp.concatenate(parts, axis=1) * (sg * (1.0 + cv * (1.0 - sg)))

        w = w_ref[...]
        dcur = act_bwd(dy_ref[...].astype(F32), c_ref[...].astype(F32))
        dnext = jnp.where(t < n_t - 1, act_bwd(dyn_ref[...].astype(F32), cn_ref[...].astype(F32)), 0.0)
        dcat = jnp.concatenate([dcur, dnext], axis=0)
        xcur = x_ref[...].astype(F32)
        dx = dcur * w[3:4]
        rows = [None, None, None, jnp.sum(dcur * xcur, axis=0, keepdims=True)]
        for k in range(3):
            ahead = pltpu.roll(dcat, tm + HALO - (3 - k), 0)[:tm]
            dx = dx + ahead * w[k:k + 1]
            rows[k] = jnp.sum(ahead * xcur, axis=0, keepdims=True)
        dx_ref[...] = dx.astype(BF16)

        @pl.when(t == 0)
        def _():
            dw_ref[...] = jnp.zeros_like(dw_ref)
        dw_ref[...] += jnp.concatenate(rows, axis=0)

    def nxt(t):
        return jnp.minimum((t + 1) * (tm // HALO), T // HALO - 1)

    cur = pl.BlockSpec((tm, D), lambda kd, t: (t, kd))
    return _pc(body, name=name, out_shape=(_sds(dproj.shape, BF16), _sds((4, 3 * D), F32)), grid=(3, n_t),
               in_specs=[pl.BlockSpec((HEADS_A, tm, 128), lambda kd, t: (kd, t, 0)),
                         pl.BlockSpec((HEADS_A, HALO, 128), lambda kd, t: (kd, nxt(t), 0)),
                         cur, pl.BlockSpec((HALO, D), lambda kd, t: (nxt(t), kd)),
                         cur, pl.BlockSpec((4, D), lambda kd, t: (0, kd)), DEP_SPEC],
               out_specs=(cur, pl.BlockSpec((4, D), lambda kd, t: (0, kd))), input_output_aliases={6: 0},
               sem=("parallel", "arbitrary"))(dqkv, dqkv, c, c, proj, wconv, dproj)


def _chunk_masks(n):
    ri = lax.broadcasted_iota(jnp.int32, (n, n), 0)
    ci = lax.broadcasted_iota(jnp.int32, (n, n), 1)
    same = (ri // CHUNK) == (ci // CHUNK)
    return same & (ri >= ci), same & (ri <= ci)


def gdn_gate_fwd(ba, al, dtb, name):
    T = ba.shape[0]
    tg = _tile(T, PREP_T)

    def body(ba_ref, al_ref, dtb_ref, o_ref):
        x = ba_ref[...]
        lane = lax.broadcasted_iota(jnp.int32, x.shape, 1)
        is_a = (lane >= HEADS_A) & (lane < 2 * HEADS_A)
        g = jnp.where(is_a, -jnp.exp(al_ref[...]) * _softplus(x + dtb_ref[...]), 0.0)
        lower, _ = _chunk_masks(tg)
        gc = _dot(lower.astype(F32), g, precision=HI)
        o_ref[...] = jnp.where(lane < HEADS_A, _sigmoid(x), gc)

    vec = pl.BlockSpec((1, 128), lambda i: (0, 0))
    return _pc(body, name=name, out_shape=_sds((T, 128), F32), grid=(T // tg,),
               in_specs=[pl.BlockSpec((tg, 128), lambda i: (i, 0)), vec, vec],
               out_specs=pl.BlockSpec((tg, 128), lambda i: (i, 0)), sem=("parallel",))(ba, al, dtb)


def gdn_gate_bwd(ba, al, dtb, dgb, dproj, name):
    T = ba.shape[0]
    tg = _tile(T, PREP_T)

    def body(ba_ref, al_ref, dtb_ref, dgb_ref, _, dba_ref, dal_ref, ddt_ref):
        x, d = ba_ref[...], dgb_ref[...]
        lane = lax.broadcasted_iota(jnp.int32, x.shape, 1)
        is_b = lane < HEADS_A
        is_a = (lane >= HEADS_A) & (lane < 2 * HEADS_A)
        beta = _sigmoid(x)
        e_a = jnp.exp(al_ref[...])
        z = x + dtb_ref[...]
        g = jnp.where(is_a, -e_a * _softplus(z), 0.0)
        _, upper = _chunk_masks(tg)
        dg = _dot(upper.astype(F32), jnp.where(is_a, d, 0.0), precision=HI)
        da = jnp.where(is_a, dg * (-e_a) * _sigmoid(z), 0.0)
        db = jnp.where(is_b, d * beta * (1.0 - beta), 0.0)
        dba_ref[...] = (da + db).astype(BF16)

        @pl.when(pl.program_id(0) == 0)
        def _():
            dal_ref[...] = jnp.zeros_like(dal_ref)
            ddt_ref[...] = jnp.zeros_like(ddt_ref)
        dal_ref[...] += jnp.sum(dg * g, axis=0, keepdims=True)
        ddt_ref[...] += jnp.sum(da, axis=0, keepdims=True)

    vec = pl.BlockSpec((1, 128), lambda i: (0, 0))
    blk = pl.BlockSpec((tg, 128), lambda i: (i, 0))
    ba_cols = pl.BlockSpec((tg, 128), lambda i: (i, A_COLS // 128 - 1))
    return _pc(body, name=name, out_shape=(_sds(dproj.shape, BF16), _sds((1, 128), F32), _sds((1, 128), F32)),
               grid=(T // tg,), in_specs=[blk, vec, vec, blk, DEP_SPEC],
               out_specs=(ba_cols, vec, vec), input_output_aliases={4: 0}, sem=("arbitrary",))(ba, al, dtb, dgb, dproj)


def _bmm(a, b, dims, precision=None):
    return lax.dot_general(a, b, dims, preferred_element_type=F32, precision=precision)


B_NN = (((2,), (1,)), ((0,), (0,)))
B_NT = (((2,), (2,)), ((0,), (0,)))


def _select_lane(x, lane_index):
    lane = lax.broadcasted_iota(jnp.int32, x.shape, x.ndim - 1)
    return jnp.sum(jnp.where(lane == lane_index, x, 0.0), axis=-1, keepdims=True)


B_TN = (((1,), (1,)), ((0,), (0,)))


def _bmm_split(a, b, dims):
    ah, bh = a.astype(BF16), b.astype(BF16)
    al, bl = (a - ah.astype(F32)).astype(BF16), (b - bh.astype(F32)).astype(BF16)
    return _bmm(ah, bh, dims) + (_bmm(ah, bl, dims) + _bmm(al, bh, dims))


@jax.custom_vjp
def _bmm_f32(a, b):
    return _bmm_split(a, b, B_NN)


def _bmm_f32_fwd(a, b):
    return _bmm_split(a, b, B_NN), (a, b)


def _bmm_bf16(a, b, dims):
    return _bmm(a.astype(BF16), b.astype(BF16), dims)


def _bmm_f32_bwd(res, dc):
    a, b = res
    return _bmm_bf16(dc, b, B_NT), _bmm_bf16(a, dc, B_TN)


_bmm_f32.defvjp(_bmm_f32_fwd, _bmm_f32_bwd)


def _tri_inverse(lmat):
    ri = lax.broadcasted_iota(jnp.int32, lmat.shape, 1)
    ci = lax.broadcasted_iota(jnp.int32, lmat.shape, 2)
    eye = jnp.where(ri == ci, 1.0, 0.0)
    inv = eye - lmat
    power = lmat
    for _ in range(5):
        power = _bmm_bf16(power, power, B_NN)
        inv = inv + _bmm_bf16(inv, power, B_NN)
    return _bmm_split(inv, 2.0 * eye - _bmm_split(eye + lmat, inv, B_NN), B_NN)


def _stored_inverse(x):
    @jax.custom_vjp
    def inverse(lmat):
        return x

    def fwd(lmat):
        return x, None

    def bwd(_, dx):
        return (-_bmm_bf16(_bmm_bf16(x, dx, B_TN), x, B_NT),)

    inverse.defvjp(fwd, bwd)
    return inverse


def _gdn_prep(q, k, v, gb, h, inverse):
    nb = q.shape[0]
    beta = _select_lane(gb, h)
    gc = _select_lane(gb, HEADS_A + h)
    ri = lax.broadcasted_iota(jnp.int32, (nb, CHUNK, CHUNK), 1)
    ci = lax.broadcasted_iota(jnp.int32, (nb, CHUNK, CHUNK), 2)
    lower, strict, eye = ri >= ci, ri > ci, ri == ci
    gcol = jnp.broadcast_to(gc, (nb, CHUNK, CHUNK))
    grow = jnp.swapaxes(gcol, 1, 2)
    decay = jnp.where(lower, jnp.exp(jnp.where(lower, gcol - grow, 0.0)), 0.0)
    kb = k * beta
    kbf = k.astype(BF16)
    inv = inverse(jnp.where(strict, _bmm(kb.astype(BF16), kbf, B_NT) * decay, 0.0))
    eg = jnp.exp(gc)
    sol = _bmm_f32(inv, jnp.concatenate([v * beta, kb * eg], axis=-1))
    aqk = _bmm(q.astype(BF16), kbf, B_NT) * decay
    g_last = gc[:, CHUNK - 1:CHUNK, :]
    gl = jnp.broadcast_to(jnp.exp(g_last), (nb, 1, 128))
    return (sol[..., :DK], sol[..., DK:], q * eg, k * jnp.exp(g_last - gc), aqk, gl), inv


def gdn_prep_fwd(qkv, gb, name):
    T = qkv.shape[1]
    tp = _tile(T, 4 * PREP_T)
    nb = tp // CHUNK

    def body(q_ref, k_ref, v_ref, gb_ref, u_ref, w_ref, qd_ref, kd_ref, a_ref, gl_ref, inv_ref):
        h = pl.program_id(1)
        shp = (nb, CHUNK, 128)
        q, k, v = (ref[0].astype(F32).reshape(shp) for ref in (q_ref, k_ref, v_ref))
        (u, w, qd, kd, aqk, gl), inv = _gdn_prep(q, k, v, gb_ref[...].reshape(shp), h, _tri_inverse)
        u_ref[0] = u.reshape(tp, 128)
        w_ref[0] = w.reshape(tp, 128).astype(BF16)
        qd_ref[0] = qd.reshape(tp, 128).astype(BF16)
        kd_ref[0] = kd.reshape(tp, 128).astype(BF16)
        a_ref[0] = aqk.reshape(tp, CHUNK).astype(BF16)
        gl_ref[0] = gl.reshape(nb, 1, 128)
        inv_ref[0] = inv.reshape(tp, CHUNK)

    def head(off):
        return pl.BlockSpec((1, tp, 128), lambda n, h: (h + off, n, 0))

    matmul_only = _sds((HEADS_A, T, 128), BF16)
    narrow = pl.BlockSpec((1, tp, CHUNK), lambda n, h: (h, n, 0))
    return _pc(body, name=name,
               out_shape=(_sds((HEADS_A, T, 128), F32), matmul_only, matmul_only, matmul_only, _sds((HEADS_A, T, CHUNK), BF16),
                          _sds((HEADS_A, T // CHUNK, 1, 128), F32), _sds((HEADS_A, T, CHUNK), F32)),
               grid=(T // tp, HEADS_A),
               in_specs=[head(0), head(HEADS_A), head(2 * HEADS_A), pl.BlockSpec((tp, 128), lambda n, h: (n, 0))],
               out_specs=(head(0), head(0), head(0), head(0), narrow,
                          pl.BlockSpec((1, nb, 1, 128), lambda n, h: (h, n, 0, 0)), narrow),
               sem=("parallel", "parallel"))(qkv, qkv, qkv, gb)


def gdn_prep_bwd(qkv, gb, inv, du, dw, dqd, dkd, da, dgl, name):
    T = qkv.shape[1]
    tp = _tile(T, 4 * PREP_T)
    nb = tp // CHUNK

    def body(q_ref, k_ref, v_ref, gb_ref, inv_ref, du_ref, dw_ref, dqd_ref, dkd_ref, da_ref, dgl_ref, dqkv_ref, dgb_ref):
        h = pl.program_id(1)
        shp = (nb, CHUNK, 128)
        stored = _stored_inverse(inv_ref[0].reshape(nb, CHUNK, CHUNK))
        q, k, v = (ref[0].astype(F32).reshape(shp) for ref in (q_ref, k_ref, v_ref))
        _, vjp = jax.vjp(lambda q, k, v, gb: _gdn_prep(q, k, v, gb, h, stored)[0], q, k, v, gb_ref[...].reshape(shp))
        du, dw, dqd, dkd = (ref[0].astype(F32).reshape(shp) for ref in (du_ref, dw_ref, dqd_ref, dkd_ref))
        dq, dk, dv, dgb = vjp((du, dw, dqd, dkd, da_ref[0].astype(F32).reshape(nb, CHUNK, CHUNK),
                               dgl_ref[0].reshape(nb, 1, 128)))
        dqkv_ref[h] = dq.reshape(tp, 128).astype(BF16)
        dqkv_ref[HEADS_A + h] = dk.reshape(tp, 128).astype(BF16)
        dqkv_ref[2 * HEADS_A + h] = dv.reshape(tp, 128).astype(BF16)

        @pl.when(h == 0)
        def _():
            dgb_ref[...] = jnp.zeros_like(dgb_ref)
        dgb_ref[...] += dgb.reshape(tp, 128)

    def head(off):
        return pl.BlockSpec((1, tp, 128), lambda n, h: (h + off, n, 0))

    narrow = pl.BlockSpec((1, tp, CHUNK), lambda n, h: (h, n, 0))
    return _pc(body, name=name, out_shape=(_sds((N_QKV_BLK, T, 128), BF16), _sds((T, 128), F32)),
               grid=(T // tp, HEADS_A),
               in_specs=[head(0), head(HEADS_A), head(2 * HEADS_A), pl.BlockSpec((tp, 128), lambda n, h: (n, 0)), narrow,
                         head(0), head(0), head(0), head(0), narrow,
                         pl.BlockSpec((1, nb, 1, 128), lambda n, h: (h, n, 0, 0))],
               out_specs=(pl.BlockSpec((N_QKV_BLK, tp, 128), lambda n, h: (0, n, 0)),
                          pl.BlockSpec((tp, 128), lambda n, h: (n, 0))),
               sem=("parallel", "arbitrary"))(qkv, qkv, qkv, gb, inv, du, dw, dqd, dkd, da, dgl)


SCAN_CHUNKS = 8


def gdn_scan_fwd(u, w, qd, kd, aqk, gl, name):
    T = u.shape[1]
    n_chunks = T // CHUNK
    rows_per_step = SCAN_CHUNKS * CHUNK

    def body(u_ref, w_ref, qd_ref, kd_ref, a_ref, gl_ref, o_ref, sin_ref, state):
        @pl.when(pl.program_id(0) == 0)
        def _():
            state[...] = jnp.zeros_like(state)
        s = state[...]
        for c in range(SCAN_CHUNKS):
            rows = slice(c * CHUNK, (c + 1) * CHUNK)
            sb = s.astype(BF16)
            sin_ref[c] = sb
            both = _bmm(jnp.concatenate([w_ref[:, rows], qd_ref[:, rows]], axis=1), sb, B_NN)
            vn = (u_ref[:, rows] - both[:, :CHUNK]).astype(BF16)
            o_ref[:, rows] = both[:, CHUNK:] + _bmm(a_ref[:, rows], vn, B_NN)
            s = s * gl_ref[:, c] + _bmm(kd_ref[:, rows], vn, B_TN)
        state[...] = s

    blk = pl.BlockSpec((HEADS_A, rows_per_step, 128), lambda n: (0, n, 0))
    return _pc(body, name=name,
               out_shape=(_sds((HEADS_A, T, 128), F32), _sds((n_chunks, HEADS_A, DK, 128), BF16)),
               grid=(n_chunks // SCAN_CHUNKS,),
               in_specs=[blk, blk, blk, blk, pl.BlockSpec((HEADS_A, rows_per_step, CHUNK), lambda n: (0, n, 0)),
                         pl.BlockSpec((HEADS_A, SCAN_CHUNKS, 1, 128), lambda n: (0, n, 0, 0))],
               out_specs=(blk, pl.BlockSpec((SCAN_CHUNKS, HEADS_A, DK, 128), lambda n: (n, 0, 0, 0))),
               scratch=[pltpu.VMEM((HEADS_A, DK, 128), F32)], sem=("arbitrary",))(u, w, qd, kd, aqk, gl)


def gdn_scan_bwd(u, w, qd, kd, aqk, gl, sin, do, name):
    T = u.shape[1]
    n_chunks = T // CHUNK
    rows_per_step = SCAN_CHUNKS * CHUNK

    def body(u_ref, w_ref, qd_ref, kd_ref, a_ref, gl_ref, sin_ref, do_ref,
             du_ref, dw_ref, dqd_ref, dkd_ref, da_ref, dgl_ref, dstate):
        @pl.when(pl.program_id(0) == 0)
        def _():
            dstate[...] = jnp.zeros_like(dstate)
        lane0 = lax.broadcasted_iota(jnp.int32, (HEADS_A, 1, 128), 2) == 0
        ds_out = dstate[...]
        for c in reversed(range(SCAN_CHUNKS)):
            rows = slice(c * CHUNK, (c + 1) * CHUNK)
            sb = sin_ref[c]
            wb, qdb, kdb, ab, dob = w_ref[:, rows], qd_ref[:, rows], kd_ref[:, rows], a_ref[:, rows], do_ref[:, rows]
            vn = (u_ref[:, rows] - _bmm(wb, sb, B_NN)).astype(BF16)
            dsb = ds_out.astype(BF16)
            dqd_ref[:, rows] = _bmm(dob, sb, B_NT).astype(BF16)
            da_ref[:, rows] = _bmm(dob, vn, B_NT).astype(BF16)
            dvb = (_bmm(ab, dob, B_TN) + _bmm(kdb, dsb, B_NN)).astype(BF16)
            dkd_ref[:, rows] = _bmm(vn, dsb, B_NT).astype(BF16)
            dgl = jnp.sum(jnp.sum(ds_out * sb.astype(F32), axis=2, keepdims=True), axis=1, keepdims=True)
            dgl_ref[:, c] = jnp.where(lane0, dgl, 0.0)
            du_ref[:, rows] = dvb
            dw_ref[:, rows] = (-_bmm(dvb, sb, B_NT)).astype(BF16)
            ds_out = ds_out * gl_ref[:, c] + _bmm(qdb, dob, B_TN) - _bmm(wb, dvb, B_TN)
        dstate[...] = ds_out

    last = n_chunks // SCAN_CHUNKS - 1
    blk = pl.BlockSpec((HEADS_A, rows_per_step, 128), lambda n: (0, last - n, 0))
    ablk = pl.BlockSpec((HEADS_A, rows_per_step, CHUNK), lambda n: (0, last - n, 0))
    glblk = pl.BlockSpec((HEADS_A, SCAN_CHUNKS, 1, 128), lambda n: (0, last - n, 0, 0))
    per_head = _sds((HEADS_A, T, 128), BF16)
    return _pc(body, name=name,
               out_shape=(per_head, per_head, per_head, per_head, _sds((HEADS_A, T, CHUNK), BF16),
                          _sds((HEADS_A, n_chunks, 1, 128), F32)), grid=(n_chunks // SCAN_CHUNKS,),
               in_specs=[blk, blk, blk, blk, ablk, glblk,
                         pl.BlockSpec((SCAN_CHUNKS, HEADS_A, DK, 128), lambda n: (last - n, 0, 0, 0)), blk],
               out_specs=(blk, blk, blk, blk, ablk, glblk),
               scratch=[pltpu.VMEM((HEADS_A, DK, 128), F32)], sem=("arbitrary",))(u, w, qd, kd, aqk, gl, sin, do)


def gdn_outnorm_fwd(o, proj, wn, name):
    T = o.shape[1]
    tm = _tile(T, 512)

    def body(o_ref, z_ref, wn_ref, y_ref):
        for h in range(HEADS_A):
            z = z_ref[:, 128 * h:128 * (h + 1)].astype(F32)
            y_ref[:, 128 * h:128 * (h + 1)] = (_rms_fwd(o_ref[h], wn_ref[...]) * (z * _sigmoid(z))).astype(BF16)

    return _pc(body, name=name, out_shape=_sds((T, D), BF16), grid=(T // tm,),
               in_specs=[pl.BlockSpec((HEADS_A, tm, 128), lambda i: (0, i, 0)),
                         pl.BlockSpec((tm, D), lambda i: (i, Z_BLK0 * 128 // D)), pl.BlockSpec((1, 128), lambda i: (0, 0))],
               out_specs=pl.BlockSpec((tm, D), lambda i: (i, 0)), sem=("parallel",))(o, proj, wn)


def gdn_outnorm_bwd(o, proj, wn, dy, name):
    T = o.shape[1]
    tm = _tile(T, 512)

    def body(o_ref, z_ref, wn_ref, dy_ref, do_ref, dz_ref, dwn_ref):
        wn = wn_ref[...]
        acc = jnp.zeros((1, 128), F32)
        for h in range(HEADS_A):
            cols = slice(128 * h, 128 * (h + 1))
            z, dyh, ov = z_ref[:, cols].astype(F32), dy_ref[:, cols], o_ref[h]
            sg = _sigmoid(z)
            do, dwn = _rms_bwd(ov, wn, dyh * (z * sg))
            do_ref[h] = do.astype(BF16)
            acc = acc + dwn
            dz_ref[:, cols] = (dyh * _rms_fwd(ov, wn) * (sg * (1.0 + z * (1.0 - sg)))).astype(BF16)

        @pl.when(pl.program_id(0) == 0)
        def _():
            dwn_ref[...] = jnp.zeros_like(dwn_ref)
        dwn_ref[...] += acc

    row = pl.BlockSpec((tm, D), lambda i: (i, 0))
    vec = pl.BlockSpec((1, 128), lambda i: (0, 0))
    hblk = pl.BlockSpec((HEADS_A, tm, 128), lambda i: (0, i, 0))
    z_cols = pl.BlockSpec((tm, D), lambda i: (i, Z_BLK0 * 128 // D))
    return _pc(body, name=name, out_shape=(_sds((HEADS_A, T, 128), BF16), _sds((T, A_COLS), BF16), _sds((1, 128), F32)),
               grid=(T // tm,), in_specs=[hblk, z_cols, vec, row],
               out_specs=(hblk, z_cols, vec), sem=("arbitrary",))(o, proj, wn, dy)


def gdn_forward(x, nw, w_in, wconv, al, dtb, wn, w_out, tag, deps=()):
    h = rmsnorm_bf16(x, nw, f"{tag}_norm", deps)
    proj = mm_nn(h, w_in, f"{tag}_proj", out_dtype=BF16, cols=(0, MAIN_COLS))
    ba = mm_nn(h, w_in, f"{tag}_proj_ba", cols=(MAIN_COLS, A_COLS))
    c, qkv = gdn_conv_fwd(proj, wconv, f"{tag}_conv")
    gb = gdn_gate_fwd(ba, al, dtb, f"{tag}_gate")
    u, w, qd, kd, aqk, gl, inv = gdn_prep_fwd(qkv, gb, f"{tag}_prep")
    o, sin = gdn_scan_fwd(u, w, qd, kd, aqk, gl, f"{tag}_scan")
    on = gdn_outnorm_fwd(o, proj, wn, f"{tag}_outnorm")
    y = mm_nn(on, w_out, f"{tag}_out", residual=x)
    return y, (x, h, proj, ba, c, qkv, gb, inv, (u, w, qd, kd, aqk, gl), sin, o, on)


def gdn_backward(dout, saved, nw, w_in, wconv, al, dtb, wn, w_out, tag):
    x, h, proj, ba, c, qkv, gb, inv, prep, sin, o, on = saved
    d_on = mm_nt(dout, w_out, f"{tag}_out_bwd")
    dw_out = mm_tn(on, dout, f"{tag}_out_wgrad")
    do, dproj, dwn = gdn_outnorm_bwd(o, proj, wn, d_on, f"{tag}_outnorm_bwd")
    du, dw, dqd, dkd, da, dgl = gdn_scan_bwd(*prep, sin, do, f"{tag}_scan_bwd")
    dqkv, dgb = gdn_prep_bwd(qkv, gb, inv, du, dw, dqd, dkd, da, dgl, f"{tag}_prep_bwd")
    dproj, dal, ddt = gdn_gate_bwd(ba, al, dtb, dgb, dproj, f"{tag}_gate_bwd")
    dproj, dwconv = gdn_conv_bwd(dqkv, c, proj, wconv, dproj, f"{tag}_conv_bwd")
    dw_in = mm_tn(h, dproj, f"{tag}_proj_wgrad")
    dx, dnw = mm_nt(dproj, w_in, f"{tag}_proj_bwd", norm_bwd=(x, nw, dout))
    return dx, dnw, dw_in, dwconv, dal, ddt, dwn, dw_out


N_KV, GROUP = 4, 4
KV_COLS = 2 * N_KV * B_HD
B_COLS = D + KV_COLS


@jax.custom_vjp
def _swap_lane_halves(x):
    return pltpu.roll(x, 64, 1)


_swap_lane_halves.defvjp(lambda x: (pltpu.roll(x, 64, 1), None), lambda _, g: (pltpu.roll(g, 64, 1),))


def _swa_block(q, kp, kc, vp, vc, sk, first):
    cols = GROUP * B_BLK
    ks = lax.broadcasted_iota(jnp.int32, (N_KV, B_BLK, cols), 1)
    qi = lax.broadcasted_iota(jnp.int32, (N_KV, B_BLK, cols), 2) % B_BLK
    from_cur = ks <= qi

    def batch(parts):
        return jnp.concatenate([part[None] for part in parts], axis=0)

    def per_kv(cur, prev):
        return batch([jnp.concatenate([cur[:, j * B_HD:(j + 1) * B_HD], prev[:, j * B_HD:(j + 1) * B_HD]], axis=0)
                      for j in range(N_KV)]).astype(BF16)

    qs = batch([jnp.concatenate([q[:, hq * B_HD:(hq + 1) * B_HD] for hq in range(GROUP * j, GROUP * (j + 1))], axis=0)
                for j in range(N_KV)])
    q_t = jnp.swapaxes(qs, 1, 2).astype(BF16)
    sink = batch([jnp.concatenate([jnp.broadcast_to(sk[:, hq:hq + 1], (1, B_BLK))
                                   for hq in range(GROUP * j, GROUP * (j + 1))], axis=1) for j in range(N_KV)])
    both = _bmm(per_kv(kc, kp), q_t, B_NN)
    s = jnp.where(from_cur, both[:, :B_BLK], jnp.where(first, -1e30, both[:, B_BLK:])) * (B_HD ** -0.5)
    m = lax.stop_gradient(jnp.maximum(jnp.max(s, axis=1, keepdims=True), sink))
    e = jnp.exp((s - m).astype(BF16))
    den = jnp.sum(e.astype(F32), axis=1, keepdims=True) + jnp.exp(sink - m)
    p = e * (1.0 / den).astype(BF16)
    zero = jnp.zeros_like(p)
    p_both = jnp.concatenate([jnp.where(from_cur, p, zero), jnp.where(from_cur, zero, p)], axis=1)
    o = jnp.swapaxes(_bmm(per_kv(vc, vp), p_both, B_TN), 1, 2)
    return jnp.concatenate([o[j, g * B_BLK:(g + 1) * B_BLK] for j in range(N_KV) for g in range(GROUP)], axis=1)


def swa_core_fwd(proj, sk, name):
    T = proj.shape[0]
    half = N_KV * B_HD

    def body(q_ref, kvc_ref, kvp_ref, sk_ref, o_ref):
        kv = jnp.concatenate([kvp_ref[...], kvc_ref[...]], axis=0)
        for b in range(2):
            prev, cur = kv[b * B_BLK:(b + 1) * B_BLK], kv[(b + 1) * B_BLK:(b + 2) * B_BLK]
            first = (pl.program_id(0) == 0) if b == 0 else False
            o_ref[b * B_BLK:(b + 1) * B_BLK, :] = _swa_block(
                q_ref[b * B_BLK:(b + 1) * B_BLK, :], prev[:, :half], cur[:, :half], prev[:, half:], cur[:, half:],
                sk_ref[...], first).astype(BF16)

    pair = 2 * B_BLK
    return _pc(body, name=name, out_shape=_sds((T, D), BF16), grid=(T // pair,),
               in_specs=[pl.BlockSpec((pair, D), lambda n: (n, 0)),
                         pl.BlockSpec((pair, KV_COLS), lambda n: (n, D // KV_COLS)),
                         pl.BlockSpec((B_BLK, KV_COLS), lambda n: (jnp.maximum(2 * n - 1, 0), D // KV_COLS)),
                         pl.BlockSpec((1, 128), lambda n: (0, 0))],
               out_specs=pl.BlockSpec((pair, D), lambda n: (n, 0)), sem=("parallel",))(proj, proj, proj, sk)


def swa_core_bwd(proj, sk, do, name):
    T = proj.shape[0]
    pair = 2 * B_BLK
    last = T // pair - 1
    half = N_KV * B_HD

    def body(q_ref, kvc_ref, kvp_ref, sk_ref, do_ref, dproj_ref, dbias_ref, dsk_ref, carry):
        step = pl.program_id(0)

        @pl.when(step == 0)
        def _():
            carry[...] = jnp.zeros_like(carry)
            dbias_ref[...] = jnp.zeros_like(dbias_ref)
            dsk_ref[...] = jnp.zeros_like(dsk_ref)
        kv = jnp.concatenate([kvp_ref[...], kvc_ref[...]], axis=0)
        from_later = carry[...]
        rows, dsk_sum = [None, None], jnp.zeros((1, 128), F32)
        for b in (1, 0):
            prev, cur = kv[b * B_BLK:(b + 1) * B_BLK], kv[(b + 1) * B_BLK:(b + 2) * B_BLK]
            first = (step == last) if b == 0 else False
            _, vjp = jax.vjp(functools.partial(_swa_block, first=first), q_ref[b * B_BLK:(b + 1) * B_BLK, :],
                             prev[:, :half], cur[:, :half], prev[:, half:], cur[:, half:], sk_ref[...])
            dq, dkp, dkc, dvp, dvc, dsk = vjp(do_ref[b * B_BLK:(b + 1) * B_BLK, :])
            rows[b] = jnp.concatenate([dq, jnp.concatenate([dkc, dvc], axis=1) + from_later], axis=1)
            from_later = jnp.concatenate([dkp, dvp], axis=1)
            dsk_sum = dsk_sum + dsk
        carry[...] = from_later
        both = jnp.concatenate(rows, axis=0)
        dproj_ref[...] = both.astype(BF16)
        dbias_ref[...] += jnp.sum(both, axis=0, keepdims=True)
        dsk_ref[...] += dsk_sum

    return _pc(body, name=name, out_shape=(_sds((T, B_COLS), BF16), _sds((1, B_COLS), F32), _sds((1, 128), F32)),
               grid=(T // pair,),
               in_specs=[pl.BlockSpec((pair, D), lambda n: (last - n, 0)),
                         pl.BlockSpec((pair, KV_COLS), lambda n: (last - n, D // KV_COLS)),
                         pl.BlockSpec((B_BLK, KV_COLS), lambda n: (jnp.maximum(2 * (last - n) - 1, 0), D // KV_COLS)),
                         pl.BlockSpec((1, 128), lambda n: (0, 0)), pl.BlockSpec((pair, D), lambda n: (last - n, 0))],
               out_specs=(pl.BlockSpec((pair, B_COLS), lambda n: (last - n, 0)),
                          pl.BlockSpec((1, B_COLS), lambda n: (0, 0)), pl.BlockSpec((1, 128), lambda n: (0, 0))),
               scratch=[pltpu.VMEM((B_BLK, KV_COLS), F32)], sem=("arbitrary",))(proj, proj, proj, sk, do)


def col_sum(a, name):
    T, N = a.shape
    tm = _tile(T, 1024)

    def body(a_ref, o_ref):
        @pl.when(pl.program_id(0) == 0)
        def _():
            o_ref[...] = jnp.zeros_like(o_ref)
        o_ref[...] += jnp.sum(a_ref[...].astype(F32), axis=0, keepdims=True)

    return _pc(body, name=name, out_shape=_sds((1, N), F32), grid=(T // tm,),
               in_specs=[pl.BlockSpec((tm, N), lambda i: (i, 0))], out_specs=pl.BlockSpec((1, N), lambda i: (0, 0)),
               sem=("arbitrary",))(a)


def swa_forward(x, nw, w_in, b_in, sk, w_out, b_out, tag):
    h = rmsnorm_bf16(x, nw, f"{tag}_norm")
    proj = mm_nn(h, w_in, f"{tag}_proj", bias=b_in)
    o = swa_core_fwd(proj, sk, f"{tag}_core")
    y = mm_nn(o, w_out, f"{tag}_out", bias=b_out, residual=x)
    return y, (x, h, proj, o)


def swa_backward(dout, saved, nw, w_in, b_in, sk, w_out, b_out, tag):
    x, h, proj, o = saved
    do = mm_nt(dout, w_out, f"{tag}_out_bwd")
    dw_out = mm_tn(o, dout, f"{tag}_out_wgrad")
    db_out = col_sum(dout, f"{tag}_out_bias_grad")
    dproj, db_in, dsk = swa_core_bwd(proj, sk, do, f"{tag}_core_bwd")
    dw_in = mm_tn(h, dproj, f"{tag}_proj_wgrad")
    dx, dnw = mm_nt(dproj, w_in, f"{tag}_proj_bwd", norm_bwd=(x, nw, dout))
    return dx, dnw, dw_in, db_in, dsk, dw_out, db_out


MESH = pl.DeviceIdType.MESH


def _position():
    return lax.axis_index("x"), lax.axis_index("y"), lax.axis_index("c")


def _slot(x, y, c):
    return 4 * x + 2 * y + c


def _peer(x, y, c, k):
    return (1 - x if k & 4 else x, 1 - y if k & 2 else y, 1 - c if k & 1 else c)


HBM_SPEC = pl.BlockSpec(memory_space=pltpu.HBM)
SEM_SPEC = pl.BlockSpec(memory_space=pltpu.SEMAPHORE)
DEP_SPEC = pl.BlockSpec(memory_space=pl.ANY)
SIDE_EFFECT = pltpu.SideEffectType.DATAFLOW_SIDE_EFFECTING
N_PEERS = N_DEV - 1


def _push_copies(srcs, lands, send_sems, recv_sems, scatter):
    x, y, c = _position()
    me = _slot(x, y, c)
    copies = []
    for k in (1, 2, 4, 3, 5, 6, 7):
        peer = _peer(x, y, c, k)
        for a in range(len(srcs)):
            copies.append(pltpu.make_async_remote_copy(
                src_ref=srcs[a].at[_slot(*peer)] if scatter else srcs[a], dst_ref=lands[a].at[me],
                send_sem=send_sems.at[N_PEERS * a + k - 1], recv_sem=recv_sems.at[N_PEERS * a + k - 1],
                device_id=peer, device_id_type=MESH))
    return copies


def push_start(srcs, lands, name, scatter, deps=()):
    n = len(srcs)
    first_out = 2 * n + len(deps)

    def body(*refs):
        for cp in _push_copies(refs[:n], refs[n:2 * n], refs[first_out], refs[first_out + 1], scatter):
            cp.start()
        refs[-1][...] = jnp.zeros_like(refs[-1])

    passed = [pltpu.HBM(t.shape, t.dtype) for t in list(srcs) + list(lands)]
    res = pl.pallas_call(
        body, name=name,
        out_shape=(pltpu.SemaphoreType.DMA((N_PEERS * n,)), pltpu.SemaphoreType.DMA((N_PEERS * n,)), *passed, _sds((8, 128), F32)),
        in_specs=[HBM_SPEC] * (2 * n) + [DEP_SPEC] * len(deps),
        out_specs=(SEM_SPEC, SEM_SPEC, *([HBM_SPEC] * (2 * n)), pl.BlockSpec(memory_space=pltpu.VMEM)),
        input_output_aliases={i: 2 + i for i in range(2 * n)},
        compiler_params=pltpu.CompilerParams(has_side_effects=SIDE_EFFECT),
    )(*[pltpu.with_memory_space_constraint(t, pltpu.HBM) for t in list(srcs) + list(lands)], *deps)
    return (res[0], res[1], list(res[2:2 + n]), list(res[2 + n:2 + 2 * n])), res[-1]


def push_wait(handles, after, name, scatter):
    send_sems, recv_sems, srcs, lands = handles
    n = len(srcs)
    after = tuple(after) if isinstance(after, (tuple, list)) else (after,)

    def body(*refs):
        for cp in _push_copies(refs[:n], refs[n:2 * n], refs[2 * n], refs[2 * n + 1], scatter):
            cp.wait_send()
            cp.wait_recv()

    res = pl.pallas_call(
        body, name=name, out_shape=tuple(pltpu.HBM(t.shape, t.dtype) for t in srcs + lands),
        in_specs=[HBM_SPEC] * (2 * n) + [SEM_SPEC, SEM_SPEC] + [DEP_SPEC] * len(after), out_specs=tuple([HBM_SPEC] * (2 * n)),
        input_output_aliases={i: i for i in range(2 * n)},
        compiler_params=pltpu.CompilerParams(has_side_effects=SIDE_EFFECT),
    )(*srcs, *lands, send_sems, recv_sems, *after)
    return list(res[n:])


def gather_start(shards, name, deps=()):
    me = _slot(*_position())
    lands = [lax.dynamic_update_slice(lax.empty((N_DEV,) + t.shape, t.dtype), t[None], (me,) + (0,) * t.ndim) for t in shards]
    return push_start(shards, lands, name, scatter=False, deps=deps)


def exchange_start(parts, name):
    me = _slot(*_position())
    lands = [lax.dynamic_update_slice(lax.empty(t.shape, t.dtype), lax.dynamic_index_in_dim(t, me, 0, keepdims=True),
                                      (me,) + (0,) * (t.ndim - 1)) for t in parts]
    return push_start(parts, lands, name, scatter=True)


def _row_tile(rows, cols):
    best = rows
    for t in range(16, rows, 16):
        if rows % t == 0 and t * cols * 4 <= (1 << 20):
            best = t
    return best


def adam_update(parts, w, m, v, name):
    n_layers = len(parts)
    P, R, C = parts[0].shape
    tr = _row_tile(R, C)
    n_t = R // tr

    def body(*refs):
        p_refs = refs[:n_layers]
        w_ref, m_ref, v_ref, g_ref, d_ref, nm_ref, nv_ref = refs[n_layers:]
        for layer in range(n_layers):
            @pl.when(pl.program_id(0) == layer)
            def _(p_ref=p_refs[layer]):
                g = p_ref[0].astype(F32)
                for s in range(1, P):
                    g = g + p_ref[s].astype(F32)
                new_m = ADAM_B1 * m_ref[0] + (1.0 - ADAM_B1) * g
                new_v = ADAM_B2 * v_ref[0] + (1.0 - ADAM_B2) * (g * g)
                m_hat = new_m / (1.0 - ADAM_B1 ** ADAM_STEP)
                v_hat = new_v / (1.0 - ADAM_B2 ** ADAM_STEP)
                g_ref[0] = g
                d_ref[0] = -ADAM_LR * (m_hat / (jnp.sqrt(v_hat) + ADAM_EPS) + ADAM_WD * w_ref[0])
                nm_ref[0] = new_m
                nv_ref[0] = new_v

    def part_spec(layer):
        return pl.BlockSpec((P, tr, C), lambda l_, i: (0, jnp.where(l_ == layer, i, jnp.where(l_ < layer, 0, n_t - 1)), 0))

    blk = pl.BlockSpec((1, tr, C), lambda l_, i: (l_, i, 0))
    out = _sds((n_layers, R, C), F32)
    return _pc(body, name=name, out_shape=(out, out, out, out), grid=(n_layers, n_t),
               in_specs=[part_spec(layer) for layer in range(n_layers)] + [blk, blk, blk],
               out_specs=(blk, blk, blk, blk), sem=("arbitrary", "arbitrary"))(*parts, w, m, v)


WEIGHTS = ("ffn1_norm", "ffn1_w_gu", "ffn1_w_down", "mix_norm", "ffn2_norm", "ffn2_w_gu", "ffn2_w_down", "a_w_in",
           "a_w_conv", "a_A_log", "a_dt_bias", "a_out_norm", "a_w_out", "b_w_in", "b_b_in", "b_sinks", "b_w_out",
           "b_b_out", "final_norm")
SHARDED = ("ffn1_w_gu", "ffn1_w_down", "ffn2_w_gu", "ffn2_w_down", "a_w_in", "a_w_conv", "a_w_out", "b_w_in", "b_b_in",
           "b_w_out", "b_b_out")
MISC_LANES = dict(a_A_log=(0, 8), a_dt_bias=(8, 16), b_sinks=(16, 32), a_out_norm=(128, 256))
LOSS_LANE = 256


def _pack_small(t):
    misc = jnp.zeros((D,), F32)
    for key, (lo, hi) in MISC_LANES.items():
        misc = misc.at[lo:hi].set(t[key].reshape(-1))
    if "loss" in t:
        misc = misc.at[LOSS_LANE].set(t["loss"])
    return jnp.concatenate([t["ffn1_norm"], t["mix_norm"], t["ffn2_norm"], t["final_norm"].reshape(1, D), misc[None]], axis=0)


def _unpack_small(p, like):
    out = dict(ffn1_norm=p[0:2], mix_norm=p[2:4], ffn2_norm=p[4:6], final_norm=p[6])
    for key, (lo, hi) in MISC_LANES.items():
        out[key] = p[7, lo:hi].reshape(like[key].shape)
    return out


def kernel(x, ffn1_norm, ffn1_w_gu, ffn1_w_down, mix_norm, ffn2_norm, ffn2_w_gu, ffn2_w_down, a_w_in, a_w_conv, a_A_log, a_dt_bias, a_out_norm, a_w_out, b_w_in, b_b_in, b_sinks, b_w_out, b_b_out, final_norm, loss_target, m_ffn1_norm, m_ffn1_w_gu, m_ffn1_w_down, m_mix_norm, m_ffn2_norm, m_ffn2_w_gu, m_ffn2_w_down, m_a_w_in, m_a_w_conv, m_a_A_log, m_a_dt_bias, m_a_out_norm, m_a_w_out, m_b_w_in, m_b_b_in, m_b_sinks, m_b_w_out, m_b_b_out, m_final_norm, v_ffn1_norm, v_ffn1_w_gu, v_ffn1_w_down, v_mix_norm, v_ffn2_norm, v_ffn2_w_gu, v_ffn2_w_down, v_a_w_in, v_a_w_conv, v_a_A_log, v_a_dt_bias, v_a_out_norm, v_a_w_out, v_b_w_in, v_b_b_in, v_b_sinks, v_b_w_out, v_b_b_out, v_final_norm):
    w = dict(ffn1_norm=ffn1_norm, ffn1_w_gu=ffn1_w_gu, ffn1_w_down=ffn1_w_down, mix_norm=mix_norm, ffn2_norm=ffn2_norm, ffn2_w_gu=ffn2_w_gu, ffn2_w_down=ffn2_w_down, a_w_in=a_w_in, a_w_conv=a_w_conv, a_A_log=a_A_log, a_dt_bias=a_dt_bias, a_out_norm=a_out_norm, a_w_out=a_w_out, b_w_in=b_w_in, b_b_in=b_b_in, b_sinks=b_sinks, b_w_out=b_w_out, b_b_out=b_b_out, final_norm=final_norm)
    m = dict(ffn1_norm=m_ffn1_norm, ffn1_w_gu=m_ffn1_w_gu, ffn1_w_down=m_ffn1_w_down, mix_norm=m_mix_norm, ffn2_norm=m_ffn2_norm, ffn2_w_gu=m_ffn2_w_gu, ffn2_w_down=m_ffn2_w_down, a_w_in=m_a_w_in, a_w_conv=m_a_w_conv, a_A_log=m_a_A_log, a_dt_bias=m_a_dt_bias, a_out_norm=m_a_out_norm, a_w_out=m_a_w_out, b_w_in=m_b_w_in, b_b_in=m_b_b_in, b_sinks=m_b_sinks, b_w_out=m_b_w_out, b_b_out=m_b_b_out, final_norm=m_final_norm)
    v = dict(ffn1_norm=v_ffn1_norm, ffn1_w_gu=v_ffn1_w_gu, ffn1_w_down=v_ffn1_w_down, mix_norm=v_mix_norm, ffn2_norm=v_ffn2_norm, ffn2_w_gu=v_ffn2_w_gu, ffn2_w_down=v_ffn2_w_down, a_w_in=v_a_w_in, a_w_conv=v_a_w_conv, a_A_log=v_a_A_log, a_dt_bias=v_a_dt_bias, a_out_norm=v_a_out_norm, a_w_out=v_a_w_out, b_w_in=v_b_w_in, b_b_in=v_b_b_in, b_sinks=v_b_sinks, b_w_out=v_b_w_out, b_b_out=v_b_b_out, final_norm=v_final_norm)
    T = x.shape[1]
    x0, tgt = x.reshape(T, D), loss_target.reshape(T, D)

    def cast(t):
        return t.astype(BF16)

    h0, t0 = gather_start([cast(ffn1_w_gu[0])], "gather0_start")
    a_log_row = jnp.zeros((1, 128), F32).at[0, HEADS_A:2 * HEADS_A].set(a_A_log[0])
    dt_row = jnp.zeros((1, 128), F32).at[0, HEADS_A:2 * HEADS_A].set(a_dt_bias[0])
    sink_row = jnp.zeros((1, 128), F32).at[0, :b_sinks.shape[1]].set(b_sinks[0])
    a_in_cols = a_w_in.shape[-1] * N_DEV

    def down_blocks(t):
        return t.reshape(N_FB, FB, D)

    wgu, wdn, saved = {}, {}, []
    xn = rmsnorm_bf16(x0, ffn1_norm[0:1], "l0_ffn1_norm", (t0,))
    wgu["ffn1", 0] = push_wait(h0, xn, "gather0_wait", scatter=False)[0]
    h0d, t0d = gather_start([cast(ffn1_w_down[0])], "gather0d_start", deps=(wgu["ffn1", 0],))
    h1, t1 = gather_start([cast(a_w_in[0]), a_w_conv[0], cast(a_w_out[0])], "gather1_start", deps=(t0d,))
    gu = ffn_up(xn, wgu["ffn1", 0], "l0_ffn1_up", deps=(t0d, t1))
    wdn["ffn1", 0] = down_blocks(push_wait(h0d, gu, "gather0d_wait", scatter=False)[0])
    xs, s1 = ffn_down(gu, wdn["ffn1", 0], x0, "l0_ffn1_down"), (x0, xn, gu)
    got = push_wait(h1, xs, "gather1_wait", scatter=False)
    h1f, t1f = gather_start([cast(ffn2_w_gu[0]), cast(ffn2_w_down[0])], "gather1f_start", deps=(got[0],))
    g2 = [cast(ffn1_w_gu[1]), cast(ffn1_w_down[1]), cast(b_w_in[0]), b_b_in, cast(b_w_out[0]), b_b_out,
          cast(ffn2_w_gu[1]), cast(ffn2_w_down[1])]
    h2, t2 = gather_start(g2, "gather2_start", deps=(t1f,))
    a_in_full = jnp.pad(got[0].transpose(1, 0, 2).reshape(D, a_in_cols), ((0, 0), (0, A_COLS - a_in_cols)))
    gdn_args = (mix_norm[0:1], a_in_full, got[1].transpose(1, 0, 2).reshape(4, 3 * D), a_log_row, dt_row, a_out_norm,
                got[2].reshape(D, D))
    xs, sm = gdn_forward(xs, *gdn_args, "gdn", deps=(t1f, t2))
    got = push_wait(h1f, xs, "gather1f_wait", scatter=False)
    wgu["ffn2", 0], wdn["ffn2", 0] = got[0], down_blocks(got[1])
    xs, s2 = ffn_forward(xs, ffn2_norm[0:1], wgu["ffn2", 0], wdn["ffn2", 0], "l0_ffn2")
    saved.append((s1, sm, s2))
    got = push_wait(h2, xs, "gather2_wait", scatter=False)
    wgu["ffn1", 1], wdn["ffn1", 1] = got[0], down_blocks(got[1])
    swa_args = (mix_norm[1:2], got[2].transpose(1, 0, 2).reshape(D, B_COLS), got[3].reshape(1, B_COLS), sink_row,
                got[4].reshape(D, D), got[5].reshape(1, D))
    wgu["ffn2", 1], wdn["ffn2", 1] = got[6], down_blocks(got[7])
    xs, s1 = ffn_forward(xs, ffn1_norm[1:2], wgu["ffn1", 1], wdn["ffn1", 1], "l1_ffn1")
    xs, sm = swa_forward(xs, *swa_args, "swa")
    xs, s2 = ffn_forward(xs, ffn2_norm[1:2], wgu["ffn2", 1], wdn["ffn2", 1], "l1_ffn2")
    saved.append((s1, sm, s2))
    loss_row, dx, d_final_norm = final_loss(xs, final_norm.reshape(1, D), tgt, "final_loss")

    def down_slots(t):
        return cast(t.reshape(N_DEV, FB // 2, D))

    def col_slots(t, dtype=BF16):
        return t.reshape(t.shape[0], N_DEV, -1).transpose(1, 0, 2).astype(dtype)

    d_norm = {"ffn1_norm": [None, None], "mix_norm": [None, None], "ffn2_norm": [None, None]}
    exchanges = {}

    def sender(tag):
        def on_grads(d_gu, d_dn):
            exchanges[tag], token = exchange_start([cast(d_gu), down_slots(d_dn)], f"exchange_{tag}_start")
            return (token,)
        return on_grads

    s1, sm, s2 = saved[1]
    dx, d_norm["ffn2_norm"][1], _, _ = ffn_backward(dx, s2, ffn2_norm[1:2], wgu["ffn2", 1], wdn["ffn2", 1], "l1_ffn2",
                                                    on_grads=sender("l1_ffn2"))
    dx, d_norm["mix_norm"][1], d_b_in, d_b_bias_in, d_sinks, d_b_out, d_b_bias_out = swa_backward(dx, sm, *swa_args, "swa")
    exchanges["swa"], t_swa = exchange_start(
        [col_slots(d_b_in), d_b_bias_in.reshape(N_DEV, 1, -1), cast(d_b_out.reshape(N_DEV, D // N_DEV, D)),
         d_b_bias_out.reshape(N_DEV, 1, -1)], "exchange_swa_start")
    dx, d_norm["ffn1_norm"][1], _, _ = ffn_backward(dx, s1, ffn1_norm[1:2], wgu["ffn1", 1], wdn["ffn1", 1], "l1_ffn1",
                                                    deps=(t_swa,), on_grads=sender("l1_ffn1"))

    s1, sm, s2 = saved[0]
    dx, d_norm["ffn2_norm"][0], _, _ = ffn_backward(dx, s2, ffn2_norm[0:1], wgu["ffn2", 0], wdn["ffn2", 0], "l0_ffn2",
                                                    on_grads=sender("l0_ffn2"))
    dx, d_norm["mix_norm"][0], d_a_in, d_a_conv, d_alog, d_dt, d_onorm, d_a_out = gdn_backward(dx, sm, *gdn_args, "gdn")
    exchanges["gdn"], t_gdn = exchange_start(
        [col_slots(d_a_in[:, :a_in_cols]), col_slots(d_a_conv, F32), cast(d_a_out.reshape(N_DEV, D // N_DEV, D))],
        "exchange_gdn_start")
    dx, d_norm["ffn1_norm"][0], _, _ = ffn_backward(dx, s1, ffn1_norm[0:1], wgu["ffn1", 0], wdn["ffn1", 0], "l0_ffn1",
                                                    deps=(t_gdn,), on_grads=sender("l0_ffn1"))
    grad_x = dx.reshape(x.shape)
    got = {tag: push_wait(exchanges[tag], dx, f"exchange_{tag}_wait", scatter=True)
           for tag in ("l1_ffn2", "swa", "l1_ffn1", "l0_ffn2", "gdn")}
    received = dict(ffn2_w_gu=[got["l0_ffn2"][0], got["l1_ffn2"][0]], ffn2_w_down=[got["l0_ffn2"][1], got["l1_ffn2"][1]],
                    b_w_in=[got["swa"][0]], b_b_in=[got["swa"][1]], b_w_out=[got["swa"][2]], b_b_out=[got["swa"][3]],
                    a_w_in=[got["gdn"][0]], a_w_conv=[got["gdn"][1]], a_w_out=[got["gdn"][2]])

    grads, deltas, new_m, new_v = {}, {}, {}, {}

    def update(key):
        shape = w[key].shape
        cols = shape[-1]
        layers = lambda t: t.reshape(shape[0], -1, cols)
        out = adam_update([r.reshape(N_DEV, -1, cols) for r in received[key]], layers(w[key]), layers(m[key]), layers(v[key]),
                          f"adam_{key}")
        grads[key], deltas[key], new_m[key], new_v[key] = (t.reshape(shape) for t in out)

    for key in SHARDED:
        if key in received:
            update(key)
    done_first = [deltas[key] for key in received]

    small = dict(ffn1_norm=jnp.concatenate(d_norm["ffn1_norm"], axis=0), mix_norm=jnp.concatenate(d_norm["mix_norm"], axis=0),
                 ffn2_norm=jnp.concatenate(d_norm["ffn2_norm"], axis=0), final_norm=d_final_norm,
                 a_A_log=d_alog[0, HEADS_A:2 * HEADS_A], a_dt_bias=d_dt[0, HEADS_A:2 * HEADS_A],
                 b_sinks=d_sinks[0, :b_sinks.shape[1]], a_out_norm=d_onorm, loss=loss_row[0, 0])
    hs, ts = gather_start([_pack_small(small)], "gather_small_start")
    r3 = push_wait(exchanges["l0_ffn1"], done_first + [ts], "exchange_l0_ffn1_wait", scatter=True)
    received.update(ffn1_w_gu=[r3[0], got["l1_ffn1"][0]], ffn1_w_down=[r3[1], got["l1_ffn1"][1]])
    update("ffn1_w_gu")
    update("ffn1_w_down")
    every = push_wait(hs, deltas["ffn1_w_down"], "gather_small_wait", scatter=False)[0]
    out = adam_update([every], _pack_small(w)[None], _pack_small(m)[None], _pack_small(v)[None], "adam_small")
    for dst, packed in zip((grads, deltas, new_m, new_v), out):
        dst.update(_unpack_small(packed[0], w))
    loss = out[0][0, 7, LOSS_LANE]

    return (loss, grad_x, *[grads[k_] for k_ in WEIGHTS], *[deltas[k_] for k_ in WEIGHTS],
            *[new_m[k_] for k_ in WEIGHTS], *[new_v[k_] for k_ in WEIGHTS])
```

```python
import functools

import jax
import jax.numpy as jnp
from jax import lax
from jax.experimental import pallas as pl
from jax.experimental.pallas import tpu as pltpu

F32, BF16 = jnp.float32, jnp.bfloat16
HI = lax.Precision.HIGHEST
EPS = 1e-6

N_DEV = 8
D = 1024
FB = 704
N_FB = 4
HEADS_A, DK = 8, 128
CHUNK = 64
PREP_T = 512
A_COLS = 4224
B_HD, B_BLK = 64, 128
VMEM_LIMIT_V7X = 60 * 1024 * 1024

ADAM_LR, ADAM_B1, ADAM_B2, ADAM_EPS, ADAM_WD, ADAM_STEP = 0.001, 0.9, 0.999, 1e-08, 0.01, 10

NT = (((1,), (1,)), ((), ()))
TN = (((0,), (0,)), ((), ()))


def _pc(body, *, name, out_shape, grid=(), in_specs=None, out_specs=None, scratch=(), sem=None, **kw):
    params = pltpu.CompilerParams(dimension_semantics=sem, vmem_limit_bytes=VMEM_LIMIT_V7X)
    return pl.pallas_call(body, name=name, out_shape=out_shape, grid=grid, in_specs=in_specs, out_specs=out_specs,
                          scratch_shapes=list(scratch), compiler_params=params, **kw)


def _sds(shape, dtype):
    return jax.ShapeDtypeStruct(tuple(shape), dtype)


def _dot(a, b, dims=None, precision=None):
    if dims is None:
        return jnp.dot(a, b, preferred_element_type=F32, precision=precision)
    return lax.dot_general(a, b, dims, preferred_element_type=F32, precision=precision)


def _sigmoid(x):
    return 1.0 / (1.0 + jnp.exp(-x))


def _softplus(x):
    return jnp.maximum(x, 0.0) + jnp.log(1.0 + jnp.exp(-jnp.abs(x)))


def _rms_fwd(x, w):
    r = lax.rsqrt(jnp.mean(x * x, axis=-1, keepdims=True) + EPS)
    return x * r * w


def _rms_bwd(x, w, dy):
    r = lax.rsqrt(jnp.mean(x * x, axis=-1, keepdims=True) + EPS)
    xh = x * r
    dxh = dy * w
    dx = r * (dxh - xh * jnp.mean(dxh * xh, axis=-1, keepdims=True))
    return dx, jnp.sum(dy * xh, axis=0, keepdims=True)


def _tile(n, want):
    t = min(n, want)
    assert n % t == 0, (n, want)
    return t


def rmsnorm_bf16(x, w, name, deps=()):
    T = x.shape[0]
    tm = _tile(T, 1024)

    def body(x_ref, w_ref, *rest):
        rest[-1][...] = _rms_fwd(x_ref[...], w_ref[...]).astype(BF16)

    return _pc(body, name=name, out_shape=_sds((T, D), BF16), grid=(T // tm,),
               in_specs=[pl.BlockSpec((tm, D), lambda i: (i, 0)), pl.BlockSpec((1, D), lambda i: (0, 0))] + [DEP_SPEC] * len(deps),
               out_specs=pl.BlockSpec((tm, D), lambda i: (i, 0)), sem=("parallel",))(x, w, *deps)


def final_loss(x, w, tgt, name):
    T = x.shape[0]
    tm = _tile(T, 512)

    def body(x_ref, w_ref, t_ref, loss_ref, dx_ref, dw_ref):
        xv, wv = x_ref[...], w_ref[...]
        err = _rms_fwd(xv, wv) - t_ref[...]
        dx, dw = _rms_bwd(xv, wv, err * (1.0 / D))
        dx_ref[...] = dx

        @pl.when(pl.program_id(0) == 0)
        def _():
            dw_ref[...] = jnp.zeros_like(dw_ref)
            loss_ref[...] = jnp.zeros_like(loss_ref)
        dw_ref[...] += dw
        loss_ref[...] += jnp.full((1, 128), 0.5 / D, F32) * jnp.sum(err * err)

    row = pl.BlockSpec((tm, D), lambda i: (i, 0))
    vec = pl.BlockSpec((1, D), lambda i: (0, 0))
    return _pc(body, name=name, out_shape=(_sds((1, 128), F32), _sds((T, D), F32), _sds((1, D), F32)),
               grid=(T // tm,), in_specs=[row, vec, row],
               out_specs=(pl.BlockSpec((1, 128), lambda i: (0, 0)), row, vec), sem=("arbitrary",))(x, w, tgt)


def _col_tile(n):
    for t in (1536, 1408, 1024, 768, 512, 384, 256, 128):
        if n % t == 0:
            return t
    return n


def mm_nn(a, b, name, bias=None, residual=None, out_dtype=F32, cols=None):
    T, K = a.shape
    first, end = cols or (0, b.shape[1])
    N = end - first
    tm, tn = _tile(T, 1024), _col_tile(N)
    assert first % tn == 0 and (cols is None or (bias is None and residual is None))
    j0 = first // tn

    def body(a_ref, b_ref, *rest):
        o_ref = rest[-1]
        acc = _dot(a_ref[...].astype(BF16), b_ref[...])
        for extra in rest[:-1]:
            acc = acc + extra[...]
        o_ref[...] = acc.astype(out_dtype)

    in_specs = [pl.BlockSpec((tm, K), lambda j, i: (i, 0)), pl.BlockSpec((K, tn), lambda j, i: (0, j0 + j))]
    args = [a, b]
    if bias is not None:
        in_specs.append(pl.BlockSpec((1, tn), lambda j, i: (0, j)))
        args.append(bias)
    if residual is not None:
        in_specs.append(pl.BlockSpec((tm, tn), lambda j, i: (i, j)))
        args.append(residual)
    return _pc(body, name=name, out_shape=_sds((T, N), out_dtype), grid=(N // tn, T // tm), in_specs=in_specs,
               out_specs=pl.BlockSpec((tm, tn), lambda j, i: (i, j)), sem=("parallel", "parallel"))(*args)


def mm_nt(a, b, name, out_dtype=F32, norm_bwd=None):
    T, N = a.shape
    K = b.shape[0]
    tm = _tile(T, 512)
    row = pl.BlockSpec((tm, K), lambda i: (i, 0))
    in_specs = [pl.BlockSpec((tm, N), lambda i: (i, 0)), _resident((K, N))]

    if norm_bwd is None:
        def body(a_ref, b_ref, o_ref):
            o_ref[...] = _dot(a_ref[...].astype(BF16), b_ref[...], NT).astype(out_dtype)

        return _pc(body, name=name, out_shape=_sds((T, K), out_dtype), grid=(T // tm,), in_specs=in_specs,
                   out_specs=row, sem=("parallel",))(a, b)

    def body(a_ref, b_ref, x_ref, w_ref, dres_ref, dx_ref, dw_ref):
        dx, dw = _rms_bwd(x_ref[...], w_ref[...], _dot(a_ref[...].astype(BF16), b_ref[...], NT))
        dx_ref[...] = dres_ref[...] + dx

        @pl.when(pl.program_id(0) == 0)
        def _():
            dw_ref[...] = jnp.zeros_like(dw_ref)
        dw_ref[...] += dw

    vec = pl.BlockSpec((1, K), lambda i: (0, 0))
    return _pc(body, name=name, out_shape=(_sds((T, K), F32), _sds((1, K), F32)), grid=(T // tm,),
               in_specs=in_specs + [row, vec, row], out_specs=(row, vec), sem=("arbitrary",))(a, b, *norm_bwd)


def mm_tn(a, b, name):
    T, K = a.shape
    N = b.shape[1]
    tt, tn = _tile(T, 1024), _col_tile(N)

    def body(a_ref, b_ref, o_ref):
        @pl.when(pl.program_id(1) == 0)
        def _():
            o_ref[...] = jnp.zeros_like(o_ref)
        o_ref[...] += _dot(a_ref[...].astype(BF16), b_ref[...].astype(BF16), TN)

    return _pc(body, name=name, out_shape=_sds((K, N), F32), grid=(N // tn, T // tt),
               in_specs=[pl.BlockSpec((tt, K), lambda j, t: (t, 0)), pl.BlockSpec((tt, tn), lambda j, t: (t, j))],
               out_specs=pl.BlockSpec((K, tn), lambda j, t: (0, j)), sem=("parallel", "arbitrary"))(a, b)


def ffn_up(xn, wgu, name, deps=()):
    T = xn.shape[0]
    tm = _tile(T, 1024)

    def body(x_ref, w_ref, *rest):
        xv = x_ref[...]
        for j in range(2 * N_FB):
            rest[-1][j] = _dot(xv, w_ref[j]).astype(BF16)

    return _pc(body, name=name, out_shape=_sds((2 * N_FB, T, FB), BF16), grid=(T // tm,),
               in_specs=[pl.BlockSpec((tm, D), lambda i: (i, 0)), _resident((2 * N_FB, D, FB))] + [DEP_SPEC] * len(deps),
               out_specs=pl.BlockSpec((2 * N_FB, tm, FB), lambda i: (0, i, 0)), sem=("parallel",))(xn, wgu, *deps)


def ffn_down(gu, wd, x, name):
    T = x.shape[0]
    tm = _tile(T, 512)

    def body(gu_ref, w_ref, x_ref, o_ref):
        acc = jnp.zeros((tm, D), F32)
        for g in range(N_FB):
            gate, up = gu_ref[g], gu_ref[N_FB + g]
            acc = acc + _dot(gate * _sigmoid(gate) * up, w_ref[g])
        o_ref[...] = x_ref[...] + 0.5 * acc

    row = pl.BlockSpec((tm, D), lambda i: (i, 0))
    return _pc(body, name=name, out_shape=_sds((T, D), F32), grid=(T // tm,),
               in_specs=[pl.BlockSpec((2 * N_FB, tm, FB), lambda i: (0, i, 0)),
                         _resident((N_FB, FB, D)), row],
               out_specs=row, sem=("parallel",))(gu, wd, x)


def _resident(shape):
    return pl.BlockSpec(shape, lambda *_: (0,) * len(shape), pipeline_mode=pl.Buffered(1))


def _store_blocks_bf16(acc, out_hbm, stage, sem):
    for j in range(acc.shape[0]):
        stage[...] = acc[j].astype(BF16)
        copy = pltpu.make_async_copy(stage, out_hbm.at[j], sem)
        copy.start()
        copy.wait()


def ffn_bwd_hidden(dout, wd, gu, name, deps=()):
    T = dout.shape[0]
    tm = _tile(T, 512)
    n_t = T // tm

    def body(d_ref, w_ref, gu_ref, *rest):
        dgu_ref, dwd_hbm, acc, stage, sem = rest[-5:]
        t = pl.program_id(0)

        @pl.when(t == 0)
        def _():
            acc[...] = jnp.zeros_like(acc)
        dy = (0.5 * d_ref[...]).astype(BF16)
        for g in range(N_FB):
            gate, up = gu_ref[g], gu_ref[N_FB + g]
            sg = _sigmoid(gate)
            silu = gate * sg
            dact = _dot(dy, w_ref[g], NT).astype(BF16)
            acc[g] += _dot(silu * up, dy, TN)
            dgu_ref[g] = dact * up * (sg * (1.0 + gate * (1.0 - sg)))
            dgu_ref[N_FB + g] = dact * silu

        @pl.when(t == n_t - 1)
        def _():
            _store_blocks_bf16(acc, dwd_hbm, stage, sem)

    return _pc(body, name=name, out_shape=(_sds((2 * N_FB, T, FB), BF16), _sds((N_FB, FB, D), BF16)), grid=(n_t,),
               in_specs=[pl.BlockSpec((tm, D), lambda i: (i, 0)), _resident((N_FB, FB, D)),
                         pl.BlockSpec((2 * N_FB, tm, FB), lambda i: (0, i, 0))] + [DEP_SPEC] * len(deps),
               out_specs=(pl.BlockSpec((2 * N_FB, tm, FB), lambda i: (0, i, 0)), pl.BlockSpec(memory_space=pl.ANY)),
               scratch=[pltpu.VMEM((N_FB, FB, D), F32), pltpu.VMEM((FB, D), BF16), pltpu.SemaphoreType.DMA],
               sem=("arbitrary",))(dout, wd, gu, *deps)


def ffn_bwd_input(dgu, wgu, x, dout, nw, name, deps=()):
    T = x.shape[0]
    tm = _tile(T, 512)

    def body(dgu_ref, w_ref, x_ref, d_ref, nw_ref, *rest):
        dx_ref, dnw_ref = rest[-2:]
        dxn = jnp.zeros((tm, D), F32)
        for j in range(2 * N_FB):
            dxn = dxn + _dot(dgu_ref[j], w_ref[j], NT)
        dx, dw = _rms_bwd(x_ref[...], nw_ref[...], dxn)
        dx_ref[...] = d_ref[...] + dx

        @pl.when(pl.program_id(0) == 0)
        def _():
            dnw_ref[...] = jnp.zeros_like(dnw_ref)
        dnw_ref[...] += dw

    row = pl.BlockSpec((tm, D), lambda i: (i, 0))
    vec = pl.BlockSpec((1, D), lambda i: (0, 0))
    return _pc(body, name=name, out_shape=(_sds((T, D), F32), _sds((1, D), F32)), grid=(T // tm,),
               in_specs=[pl.BlockSpec((2 * N_FB, tm, FB), lambda i: (0, i, 0)), _resident((2 * N_FB, D, FB)),
                         row, row, vec] + [DEP_SPEC] * len(deps),
               out_specs=(row, vec), sem=("arbitrary",))(dgu, wgu, x, dout, nw, *deps)


def ffn_wgrad_gu(xn, dgu, name):
    T = xn.shape[0]
    tt = _tile(T, 1024)
    n_t = T // tt

    def body(x_ref, d_ref, dw_hbm, acc, stage, sem):
        t = pl.program_id(0)

        @pl.when(t == 0)
        def _():
            acc[...] = jnp.zeros_like(acc)
        xn_tile = x_ref[...]
        for j in range(2 * N_FB):
            acc[j] += _dot(xn_tile, d_ref[j], TN)

        @pl.when(t == n_t - 1)
        def _():
            _store_blocks_bf16(acc, dw_hbm, stage, sem)

    return _pc(body, name=name, out_shape=_sds((2 * N_FB, D, FB), BF16), grid=(n_t,),
               in_specs=[pl.BlockSpec((tt, D), lambda t: (t, 0)), pl.BlockSpec((2 * N_FB, tt, FB), lambda t: (0, t, 0))],
               out_specs=pl.BlockSpec(memory_space=pl.ANY),
               scratch=[pltpu.VMEM((2 * N_FB, D, FB), F32), pltpu.VMEM((D, FB), BF16), pltpu.SemaphoreType.DMA],
               sem=("arbitrary",))(xn, dgu)


def ffn_forward(x, nw, wgu, wd, tag):
    T = x.shape[0]
    tm = _tile(T, 512)

    def body(x_ref, nw_ref, wgu_ref, wd_ref, o_ref, xn_ref, gu_ref):
        xv = x_ref[...]
        xn = _rms_fwd(xv, nw_ref[...]).astype(BF16)
        xn_ref[...] = xn
        for j in range(2 * N_FB):
            gu_ref[j] = _dot(xn, wgu_ref[j]).astype(BF16)
        acc = jnp.zeros((tm, D), F32)
        for g in range(N_FB):
            gate, up = gu_ref[g], gu_ref[N_FB + g]
            acc = acc + _dot(gate * _sigmoid(gate) * up, wd_ref[g])
        o_ref[...] = xv + 0.5 * acc

    row = pl.BlockSpec((tm, D), lambda i: (i, 0))
    out, xn, gu = _pc(body, name=f"{tag}_fwd",
                      out_shape=(_sds((T, D), F32), _sds((T, D), BF16), _sds((2 * N_FB, T, FB), BF16)), grid=(T // tm,),
                      in_specs=[row, pl.BlockSpec((1, D), lambda i: (0, 0)), _resident((2 * N_FB, D, FB)),
                                _resident((N_FB, FB, D))],
                      out_specs=(row, row, pl.BlockSpec((2 * N_FB, tm, FB), lambda i: (0, i, 0))),
                      sem=("parallel",))(x, nw, wgu, wd)
    return out, (x, xn, gu)


def ffn_backward(dout, saved, nw, wgu, wd, tag, deps=(), on_grads=None):
    x, xn, gu = saved
    dgu, dwd = ffn_bwd_hidden(dout, wd, gu, f"{tag}_bwd_hidden", deps)
    dwgu = ffn_wgrad_gu(xn, dgu, f"{tag}_wgrad_gu")
    late = on_grads(dwgu, dwd) if on_grads else ()
    dx, dnw = ffn_bwd_input(dgu, wgu, x, dout, nw, f"{tag}_bwd_input", late)
    return dx, dnw, dwgu, dwd


N_QKV_BLK = 3 * HEADS_A
Z_BLK0 = N_QKV_BLK
MAIN_COLS = 4 * D
HALO = 16


def _conv_taps(xcat, w):
    c = xcat[HALO:] * w[3:4]
    for k in range(3):
        c = c + pltpu.roll(xcat, 3 - k, 0)[HALO:] * w[k:k + 1]
    return c


def _head_cols(h):
    return slice(128 * h, 128 * (h + 1))


def gdn_conv_fwd(proj, wconv, name):
    T = proj.shape[0]
    tm = _tile(T, 512)

    def body(cur_ref, prev_ref, w_ref, c_ref, y_ref):
        kind, t = pl.program_id(0), pl.program_id(1)
        prev = jnp.where(t > 0, prev_ref[...].astype(F32), 0.0)
        c = _conv_taps(jnp.concatenate([prev, cur_ref[...].astype(F32)], axis=0), w_ref[...])
        c_ref[...] = c.astype(BF16)
        s = c * _sigmoid(c)
        scale = jnp.where(kind == 0, DK ** -0.5, 1.0)
        for h in range(HEADS_A):
            sh = s[:, _head_cols(h)]
            r = lax.rsqrt(jnp.sum(sh * sh, axis=-1, keepdims=True) + EPS)
            y_ref[h] = (sh * jnp.where(kind < 2, r * scale, 1.0)).astype(BF16)

    return _pc(body, name=name, out_shape=(_sds((T, 3 * D), BF16), _sds((N_QKV_BLK, T, 128), BF16)),
               grid=(3, T // tm),
               in_specs=[pl.BlockSpec((tm, D), lambda kd, t: (t, kd)),
                         pl.BlockSpec((HALO, D), lambda kd, t: (jnp.maximum(t * (tm // HALO) - 1, 0), kd)),
                         pl.BlockSpec((4, D), lambda kd, t: (0, kd))],
               out_specs=(pl.BlockSpec((tm, D), lambda kd, t: (t, kd)),
                          pl.BlockSpec((HEADS_A, tm, 128), lambda kd, t: (kd, t, 0))),
               sem=("parallel", "parallel"))(proj, proj, wconv)


def gdn_conv_bwd(dqkv, c, proj, wconv, dproj, name):
    T = c.shape[0]
    tm = _tile(T, 512)
    n_t = T // tm

    def body(dy_ref, dyn_ref, c_ref, cn_ref, x_ref, w_ref, _, dx_ref, dw_ref):
        kind, t = pl.program_id(0), pl.program_id(1)
        scale = jnp.where(kind == 0, DK ** -0.5, 1.0)

        def act_bwd(dy, cv):
            sg = _sigmoid(cv)
            s = cv * sg
            parts = []
            for h in range(HEADS_A):
                sh, dyh = s[:, _head_cols(h)], dy[h]
                r = lax.rsqrt(jnp.sum(sh * sh, axis=-1, keepdims=True) + EPS)
                ds_norm = scale * r * (dyh - (r * r) * sh * jnp.sum(dyh * sh, axis=-1, keepdims=True))
                parts.append(jnp.where(kind < 2, ds_norm, dyh))
            return jnp.concatenate(parts, axis=1) * (sg * (1.0 + cv * (1.0 - sg)))

        w = w_ref[...]
        dcur = act_bwd(dy_ref[...].astype(F32), c_ref[...].astype(F32))
        dnext = jnp.where(t < n_t - 1, act_bwd(dyn_ref[...].astype(F32), cn_ref[...].astype(F32)), 0.0)
        dcat = jnp.concatenate([dcur, dnext], axis=0)
        xcur = x_ref[...].astype(F32)
        dx = dcur * w[3:4]
        rows = [None, None, None, jnp.sum(dcur * xcur, axis=0, keepdims=True)]
        for k in range(3):
            ahead = pltpu.roll(dcat, tm + HALO - (3 - k), 0)[:tm]
            dx = dx + ahead * w[k:k + 1]
            rows[k] = jnp.sum(ahead * xcur, axis=0, keepdims=True)
        dx_ref[...] = dx.astype(BF16)

        @pl.when(t == 0)
        def _():
            dw_ref[...] = jnp.zeros_like(dw_ref)
        dw_ref[...] += jnp.concatenate(rows, axis=0)

    def nxt(t):
        return jnp.minimum((t + 1) * (tm // HALO), T // HALO - 1)

    cur = pl.BlockSpec((tm, D), lambda kd, t: (t, kd))
    return _pc(body, name=name, out_shape=(_sds(dproj.shape, BF16), _sds((4, 3 * D), F32)), grid=(3, n_t),
               in_specs=[pl.BlockSpec((HEADS_A, tm, 128), lambda kd, t: (kd, t, 0)),
                         pl.BlockSpec((HEADS_A, HALO, 128), lambda kd, t: (kd, nxt(t), 0)),
                         cur, pl.BlockSpec((HALO, D), lambda kd, t: (nxt(t), kd)),
                         cur, pl.BlockSpec((4, D), lambda kd, t: (0, kd)), DEP_SPEC],
               out_specs=(cur, pl.BlockSpec((4, D), lambda kd, t: (0, kd))), input_output_aliases={6: 0},
               sem=("parallel", "arbitrary"))(dqkv, dqkv, c, c, proj, wconv, dproj)


def _chunk_masks(n):
    ri = lax.broadcasted_iota(jnp.int32, (n, n), 0)
    ci = lax.broadcasted_iota(jnp.int32, (n, n), 1)
    same = (ri // CHUNK) == (ci // CHUNK)
    return same & (ri >= ci), same & (ri <= ci)


def gdn_gate_fwd(ba, al, dtb, name):
    T = ba.shape[0]
    tg = _tile(T, PREP_T)

    def body(ba_ref, al_ref, dtb_ref, o_ref):
        x = ba_ref[...]
        lane = lax.broadcasted_iota(jnp.int32, x.shape, 1)
        is_a = (lane >= HEADS_A) & (lane < 2 * HEADS_A)
        g = jnp.where(is_a, -jnp.exp(al_ref[...]) * _softplus(x + dtb_ref[...]), 0.0)
        lower, _ = _chunk_masks(tg)
        gc = _dot(lower.astype(F32), g, precision=HI)
        o_ref[...] = jnp.where(lane < HEADS_A, _sigmoid(x), gc)

    vec = pl.BlockSpec((1, 128), lambda i: (0, 0))
    return _pc(body, name=name, out_shape=_sds((T, 128), F32), grid=(T // tg,),
               in_specs=[pl.BlockSpec((tg, 128), lambda i: (i, 0)), vec, vec],
               out_specs=pl.BlockSpec((tg, 128), lambda i: (i, 0)), sem=("parallel",))(ba, al, dtb)


def gdn_gate_bwd(ba, al, dtb, dgb, dproj, name):
    T = ba.shape[0]
    tg = _tile(T, PREP_T)

    def body(ba_ref, al_ref, dtb_ref, dgb_ref, _, dba_ref, dal_ref, ddt_ref):
        x, d = ba_ref[...], dgb_ref[...]
        lane = lax.broadcasted_iota(jnp.int32, x.shape, 1)
        is_b = lane < HEADS_A
        is_a = (lane >= HEADS_A) & (lane < 2 * HEADS_A)
        beta = _sigmoid(x)
        e_a = jnp.exp(al_ref[...])
        z = x + dtb_ref[...]
        g = jnp.where(is_a, -e_a * _softplus(z), 0.0)
        _, upper = _chunk_masks(tg)
        dg = _dot(upper.astype(F32), jnp.where(is_a, d, 0.0), precision=HI)
        da = jnp.where(is_a, dg * (-e_a) * _sigmoid(z), 0.0)
        db = jnp.where(is_b, d * beta * (1.0 - beta), 0.0)
        dba_ref[...] = (da + db).astype(BF16)

        @pl.when(pl.program_id(0) == 0)
        def _():
            dal_ref[...] = jnp.zeros_like(dal_ref)
            ddt_ref[...] = jnp.zeros_like(ddt_ref)
        dal_ref[...] += jnp.sum(dg * g, axis=0, keepdims=True)
        ddt_ref[...] += jnp.sum(da, axis=0, keepdims=True)

    vec = pl.BlockSpec((1, 128), lambda i: (0, 0))
    blk = pl.BlockSpec((tg, 128), lambda i: (i, 0))
    ba_cols = pl.BlockSpec((tg, 128), lambda i: (i, A_COLS // 128 - 1))
    return _pc(body, name=name, out_shape=(_sds(dproj.shape, BF16), _sds((1, 128), F32), _sds((1, 128), F32)),
               grid=(T // tg,), in_specs=[blk, vec, vec, blk, DEP_SPEC],
               out_specs=(ba_cols, vec, vec), input_output_aliases={4: 0}, sem=("arbitrary",))(ba, al, dtb, dgb, dproj)


def _bmm(a, b, dims, precision=None):
    return lax.dot_general(a, b, dims, preferred_element_type=F32, precision=precision)


B_NN = (((2,), (1,)), ((0,), (0,)))
B_NT = (((2,), (2,)), ((0,), (0,)))


def _select_lane(x, lane_index):
    lane = lax.broadcasted_iota(jnp.int32, x.shape, x.ndim - 1)
    return jnp.sum(jnp.where(lane == lane_index, x, 0.0), axis=-1, keepdims=True)


B_TN = (((1,), (1,)), ((0,), (0,)))


def _bmm_split(a, b, dims):
    ah, bh = a.astype(BF16), b.astype(BF16)
    al, bl = (a - ah.astype(F32)).astype(BF16), (b - bh.astype(F32)).astype(BF16)
    return _bmm(ah, bh, dims) + (_bmm(ah, bl, dims) + _bmm(al, bh, dims))


@jax.custom_vjp
def _bmm_f32(a, b):
    return _bmm_split(a, b, B_NN)


def _bmm_f32_fwd(a, b):
    return _bmm_split(a, b, B_NN), (a, b)


def _bmm_bf16(a, b, dims):
    return _bmm(a.astype(BF16), b.astype(BF16), dims)


def _bmm_f32_bwd(res, dc):
    a, b = res
    return _bmm_bf16(dc, b, B_NT), _bmm_bf16(a, dc, B_TN)


_bmm_f32.defvjp(_bmm_f32_fwd, _bmm_f32_bwd)


def _tri_inverse(lmat):
    ri = lax.broadcasted_iota(jnp.int32, lmat.shape, 1)
    ci = lax.broadcasted_iota(jnp.int32, lmat.shape, 2)
    eye = jnp.where(ri == ci, 1.0, 0.0)
    inv = eye - lmat
    power = lmat
    for _ in range(5):
        power = _bmm_bf16(power, power, B_NN)
        inv = inv + _bmm_bf16(inv, power, B_NN)
    return _bmm_split(inv, 2.0 * eye - _bmm_split(eye + lmat, inv, B_NN), B_NN)


def _stored_inverse(x):
    @jax.custom_vjp
    def inverse(lmat):
        return x

    def fwd(lmat):
        return x, None

    def bwd(_, dx):
        return (-_bmm_bf16(_bmm_bf16(x, dx, B_TN), x, B_NT),)

    inverse.defvjp(fwd, bwd)
    return inverse


def _gdn_prep(q, k, v, gb, h, inverse):
    nb = q.shape[0]
    beta = _select_lane(gb, h)
    gc = _select_lane(gb, HEADS_A + h)
    ri = lax.broadcasted_iota(jnp.int32, (nb, CHUNK, CHUNK), 1)
    ci = lax.broadcasted_iota(jnp.int32, (nb, CHUNK, CHUNK), 2)
    lower, strict, eye = ri >= ci, ri > ci, ri == ci
    gcol = jnp.broadcast_to(gc, (nb, CHUNK, CHUNK))
    grow = jnp.swapaxes(gcol, 1, 2)
    decay = jnp.where(lower, jnp.exp(jnp.where(lower, gcol - grow, 0.0)), 0.0)
    kb = k * beta
    kbf = k.astype(BF16)
    inv = inverse(jnp.where(strict, _bmm(kb.astype(BF16), kbf, B_NT) * decay, 0.0))
    eg = jnp.exp(gc)
    sol = _bmm_f32(inv, jnp.concatenate([v * beta, kb * eg], axis=-1))
    aqk = _bmm(q.astype(BF16), kbf, B_NT) * decay
    g_last = gc[:, CHUNK - 1:CHUNK, :]
    gl = jnp.broadcast_to(jnp.exp(g_last), (nb, 1, 128))
    return (sol[..., :DK], sol[..., DK:], q * eg, k * jnp.exp(g_last - gc), aqk, gl), inv


def gdn_prep_fwd(qkv, gb, name):
    T = qkv.shape[1]
    tp = _tile(T, 4 * PREP_T)
    nb = tp // CHUNK

    def body(q_ref, k_ref, v_ref, gb_ref, u_ref, w_ref, qd_ref, kd_ref, a_ref, gl_ref, inv_ref):
        h = pl.program_id(1)
        shp = (nb, CHUNK, 128)
        q, k, v = (ref[0].astype(F32).reshape(shp) for ref in (q_ref, k_ref, v_ref))
        (u, w, qd, kd, aqk, gl), inv = _gdn_prep(q, k, v, gb_ref[...].reshape(shp), h, _tri_inverse)
        u_ref[0] = u.reshape(tp, 128)
        w_ref[0] = w.reshape(tp, 128).astype(BF16)
        qd_ref[0] = qd.reshape(tp, 128).astype(BF16)
        kd_ref[0] = kd.reshape(tp, 128).astype(BF16)
        a_ref[0] = aqk.reshape(tp, CHUNK).astype(BF16)
        gl_ref[0] = gl.reshape(nb, 1, 128)
        inv_ref[0] = inv.reshape(tp, CHUNK)

    def head(off):
        return pl.BlockSpec((1, tp, 128), lambda n, h: (h + off, n, 0))

    matmul_only = _sds((HEADS_A, T, 128), BF16)
    narrow = pl.BlockSpec((1, tp, CHUNK), lambda n, h: (h, n, 0))
    return _pc(body, name=name,
               out_shape=(_sds((HEADS_A, T, 128), F32), matmul_only, matmul_only, matmul_only, _sds((HEADS_A, T, CHUNK), BF16),
                          _sds((HEADS_A, T // CHUNK, 1, 128), F32), _sds((HEADS_A, T, CHUNK), F32)),
               grid=(T // tp, HEADS_A),
               in_specs=[head(0), head(HEADS_A), head(2 * HEADS_A), pl.BlockSpec((tp, 128), lambda n, h: (n, 0))],
               out_specs=(head(0), head(0), head(0), head(0), narrow,
                          pl.BlockSpec((1, nb, 1, 128), lambda n, h: (h, n, 0, 0)), narrow),
               sem=("parallel", "parallel"))(qkv, qkv, qkv, gb)


def gdn_prep_bwd(qkv, gb, inv, du, dw, dqd, dkd, da, dgl, name):
    T = qkv.shape[1]
    tp = _tile(T, 4 * PREP_T)
    nb = tp // CHUNK

    def body(q_ref, k_ref, v_ref, gb_ref, inv_ref, du_ref, dw_ref, dqd_ref, dkd_ref, da_ref, dgl_ref, dqkv_ref, dgb_ref):
        h = pl.program_id(1)
        shp = (nb, CHUNK, 128)
        stored = _stored_inverse(inv_ref[0].reshape(nb, CHUNK, CHUNK))
        q, k, v = (ref[0].astype(F32).reshape(shp) for ref in (q_ref, k_ref, v_ref))
        _, vjp = jax.vjp(lambda q, k, v, gb: _gdn_prep(q, k, v, gb, h, stored)[0], q, k, v, gb_ref[...].reshape(shp))
        du, dw, dqd, dkd = (ref[0].astype(F32).reshape(shp) for ref in (du_ref, dw_ref, dqd_ref, dkd_ref))
        dq, dk, dv, dgb = vjp((du, dw, dqd, dkd, da_ref[0].astype(F32).reshape(nb, CHUNK, CHUNK),
                               dgl_ref[0].reshape(nb, 1, 128)))
        dqkv_ref[h] = dq.reshape(tp, 128).astype(BF16)
        dqkv_ref[HEADS_A + h] = dk.reshape(tp, 128).astype(BF16)
        dqkv_ref[2 * HEADS_A + h] = dv.reshape(tp, 128).astype(BF16)

        @pl.when(h == 0)
        def _():
            dgb_ref[...] = jnp.zeros_like(dgb_ref)
        dgb_ref[...] += dgb.reshape(tp, 128)

    def head(off):
        return pl.BlockSpec((1, tp, 128), lambda n, h: (h + off, n, 0))

    narrow = pl.BlockSpec((1, tp, CHUNK), lambda n, h: (h, n, 0))
    return _pc(body, name=name, out_shape=(_sds((N_QKV_BLK, T, 128), BF16), _sds((T, 128), F32)),
               grid=(T // tp, HEADS_A),
               in_specs=[head(0), head(HEADS_A), head(2 * HEADS_A), pl.BlockSpec((tp, 128), lambda n, h: (n, 0)), narrow,
                         head(0), head(0), head(0), head(0), narrow,
                         pl.BlockSpec((1, nb, 1, 128), lambda n, h: (h, n, 0, 0))],
               out_specs=(pl.BlockSpec((N_QKV_BLK, tp, 128), lambda n, h: (0, n, 0)),
                          pl.BlockSpec((tp, 128), lambda n, h: (n, 0))),
               sem=("parallel", "arbitrary"))(qkv, qkv, qkv, gb, inv, du, dw, dqd, dkd, da, dgl)


SCAN_CHUNKS = 8


def gdn_scan_fwd(u, w, qd, kd, aqk, gl, name):
    T = u.shape[1]
    n_chunks = T // CHUNK
    rows_per_step = SCAN_CHUNKS * CHUNK

    def body(u_ref, w_ref, qd_ref, kd_ref, a_ref, gl_ref, o_ref, sin_ref, state):
        @pl.when(pl.program_id(0) == 0)
        def _():
            state[...] = jnp.zeros_like(state)
        s = state[...]
        for c in range(SCAN_CHUNKS):
            rows = slice(c * CHUNK, (c + 1) * CHUNK)
            sb = s.astype(BF16)
            sin_ref[c] = sb
            both = _bmm(jnp.concatenate([w_ref[:, rows], qd_ref[:, rows]], axis=1), sb, B_NN)
            vn = (u_ref[:, rows] - both[:, :CHUNK]).astype(BF16)
            o_ref[:, rows] = both[:, CHUNK:] + _bmm(a_ref[:, rows], vn, B_NN)
            s = s * gl_ref[:, c] + _bmm(kd_ref[:, rows], vn, B_TN)
        state[...] = s

    blk = pl.BlockSpec((HEADS_A, rows_per_step, 128), lambda n: (0, n, 0))
    return _pc(body, name=name,
               out_shape=(_sds((HEADS_A, T, 128), F32), _sds((n_chunks, HEADS_A, DK, 128), BF16)),
               grid=(n_chunks // SCAN_CHUNKS,),
               in_specs=[blk, blk, blk, blk, pl.BlockSpec((HEADS_A, rows_per_step, CHUNK), lambda n: (0, n, 0)),
                         pl.BlockSpec((HEADS_A, SCAN_CHUNKS, 1, 128), lambda n: (0, n, 0, 0))],
               out_specs=(blk, pl.BlockSpec((SCAN_CHUNKS, HEADS_A, DK, 128), lambda n: (n, 0, 0, 0))),
               scratch=[pltpu.VMEM((HEADS_A, DK, 128), F32)], sem=("arbitrary",))(u, w, qd, kd, aqk, gl)


def gdn_scan_bwd(u, w, qd, kd, aqk, gl, sin, do, name):
    T = u.shape[1]
    n_chunks = T // CHUNK
    rows_per_step = SCAN_CHUNKS * CHUNK

    def body(u_ref, w_ref, qd_ref, kd_ref, a_ref, gl_ref, sin_ref, do_ref,
             du_ref, dw_ref, dqd_ref, dkd_ref, da_ref, dgl_ref, dstate):
        @pl.when(pl.program_id(0) == 0)
        def _():
            dstate[...] = jnp.zeros_like(dstate)
        lane0 = lax.broadcasted_iota(jnp.int32, (HEADS_A, 1, 128), 2) == 0
        ds_out = dstate[...]
        for c in reversed(range(SCAN_CHUNKS)):
            rows = slice(c * CHUNK, (c + 1) * CHUNK)
            sb = sin_ref[c]
            wb, qdb, kdb, ab, dob = w_ref[:, rows], qd_ref[:, rows], kd_ref[:, rows], a_ref[:, rows], do_ref[:, rows]
            vn = (u_ref[:, rows] - _bmm(wb, sb, B_NN)).astype(BF16)
            dsb = ds_out.astype(BF16)
            dqd_ref[:, rows] = _bmm(dob, sb, B_NT).astype(BF16)
            da_ref[:, rows] = _bmm(dob, vn, B_NT).astype(BF16)
            dvb = (_bmm(ab, dob, B_TN) + _bmm(kdb, dsb, B_NN)).astype(BF16)
            dkd_ref[:, rows] = _bmm(vn, dsb, B_NT).astype(BF16)
            dgl = jnp.sum(jnp.sum(ds_out * sb.astype(F32), axis=2, keepdims=True), axis=1, keepdims=True)
            dgl_ref[:, c] = jnp.where(lane0, dgl, 0.0)
            du_ref[:, rows] = dvb
            dw_ref[:, rows] = (-_bmm(dvb, sb, B_NT)).astype(BF16)
            ds_out = ds_out * gl_ref[:, c] + _bmm(qdb, dob, B_TN) - _bmm(wb, dvb, B_TN)
        dstate[...] = ds_out

    last = n_chunks // SCAN_CHUNKS - 1
    blk = pl.BlockSpec((HEADS_A, rows_per_step, 128), lambda n: (0, last - n, 0))
    ablk = pl.BlockSpec((HEADS_A, rows_per_step, CHUNK), lambda n: (0, last - n, 0))
    glblk = pl.BlockSpec((HEADS_A, SCAN_CHUNKS, 1, 128), lambda n: (0, last - n, 0, 0))
    per_head = _sds((HEADS_A, T, 128), BF16)
    return _pc(body, name=name,
               out_shape=(per_head, per_head, per_head, per_head, _sds((HEADS_A, T, CHUNK), BF16),
                          _sds((HEADS_A, n_chunks, 1, 128), F32)), grid=(n_chunks // SCAN_CHUNKS,),
               in_specs=[blk, blk, blk, blk, ablk, glblk,
                         pl.BlockSpec((SCAN_CHUNKS, HEADS_A, DK, 128), lambda n: (last - n, 0, 0, 0)), blk],
               out_specs=(blk, blk, blk, blk, ablk, glblk),
               scratch=[pltpu.VMEM((HEADS_A, DK, 128), F32)], sem=("arbitrary",))(u, w, qd, kd, aqk, gl, sin, do)


def gdn_outnorm_fwd(o, proj, wn, name):
    T = o.shape[1]
    tm = _tile(T, 512)

    def body(o_ref, z_ref, wn_ref, y_ref):
        for h in range(HEADS_A):
            z = z_ref[:, 128 * h:128 * (h + 1)].astype(F32)
            y_ref[:, 128 * h:128 * (h + 1)] = (_rms_fwd(o_ref[h], wn_ref[...]) * (z * _sigmoid(z))).astype(BF16)

    return _pc(body, name=name, out_shape=_sds((T, D), BF16), grid=(T // tm,),
               in_specs=[pl.BlockSpec((HEADS_A, tm, 128), lambda i: (0, i, 0)),
                         pl.BlockSpec((tm, D), lambda i: (i, Z_BLK0 * 128 // D)), pl.BlockSpec((1, 128), lambda i: (0, 0))],
               out_specs=pl.BlockSpec((tm, D), lambda i: (i, 0)), sem=("parallel",))(o, proj, wn)


def gdn_outnorm_bwd(o, proj, wn, dy, name):
    T = o.shape[1]
    tm = _tile(T, 512)

    def body(o_ref, z_ref, wn_ref, dy_ref, do_ref, dz_ref, dwn_ref):
        wn = wn_ref[...]
        acc = jnp.zeros((1, 128), F32)
        for h in range(HEADS_A):
            cols = slice(128 * h, 128 * (h + 1))
            z, dyh, ov = z_ref[:, cols].astype(F32), dy_ref[:, cols], o_ref[h]
            sg = _sigmoid(z)
            do, dwn = _rms_bwd(ov, wn, dyh * (z * sg))
            do_ref[h] = do.astype(BF16)
            acc = acc + dwn
            dz_ref[:, cols] = (dyh * _rms_fwd(ov, wn) * (sg * (1.0 + z * (1.0 - sg)))).astype(BF16)

        @pl.when(pl.program_id(0) == 0)
        def _():
            dwn_ref[...] = jnp.zeros_like(dwn_ref)
        dwn_ref[...] += acc

    row = pl.BlockSpec((tm, D), lambda i: (i, 0))
    vec = pl.BlockSpec((1, 128), lambda i: (0, 0))
    hblk = pl.BlockSpec((HEADS_A, tm, 128), lambda i: (0, i, 0))
    z_cols = pl.BlockSpec((tm, D), lambda i: (i, Z_BLK0 * 128 // D))
    return _pc(body, name=name, out_shape=(_sds((HEADS_A, T, 128), BF16), _sds((T, A_COLS), BF16), _sds((1, 128), F32)),
               grid=(T // tm,), in_specs=[hblk, z_cols, vec, row],
               out_specs=(hblk, z_cols, vec), sem=("arbitrary",))(o, proj, wn, dy)


def gdn_forward(x, nw, w_in, wconv, al, dtb, wn, w_out, tag, deps=()):
    h = rmsnorm_bf16(x, nw, f"{tag}_norm", deps)
    proj = mm_nn(h, w_in, f"{tag}_proj", out_dtype=BF16, cols=(0, MAIN_COLS))
    ba = mm_nn(h, w_in, f"{tag}_proj_ba", cols=(MAIN_COLS, A_COLS))
    c, qkv = gdn_conv_fwd(proj, wconv, f"{tag}_conv")
    gb = gdn_gate_fwd(ba, al, dtb, f"{tag}_gate")
    u, w, qd, kd, aqk, gl, inv = gdn_prep_fwd(qkv, gb, f"{tag}_prep")
    o, sin = gdn_scan_fwd(u, w, qd, kd, aqk, gl, f"{tag}_scan")
    on = gdn_outnorm_fwd(o, proj, wn, f"{tag}_outnorm")
    y = mm_nn(on, w_out, f"{tag}_out", residual=x)
    return y, (x, h, proj, ba, c, qkv, gb, inv, (u, w, qd, kd, aqk, gl), sin, o, on)


def gdn_backward(dout, saved, nw, w_in, wconv, al, dtb, wn, w_out, tag):
    x, h, proj, ba, c, qkv, gb, inv, prep, sin, o, on = saved
    d_on = mm_nt(dout, w_out, f"{tag}_out_bwd")
    dw_out = mm_tn(on, dout, f"{tag}_out_wgrad")
    do, dproj, dwn = gdn_outnorm_bwd(o, proj, wn, d_on, f"{tag}_outnorm_bwd")
    du, dw, dqd, dkd, da, dgl = gdn_scan_bwd(*prep, sin, do, f"{tag}_scan_bwd")
    dqkv, dgb = gdn_prep_bwd(qkv, gb, inv, du, dw, dqd, dkd, da, dgl, f"{tag}_prep_bwd")
    dproj, dal, ddt = gdn_gate_bwd(ba, al, dtb, dgb, dproj, f"{tag}_gate_bwd")
    dproj, dwconv = gdn_conv_bwd(dqkv, c, proj, wconv, dproj, f"{tag}_conv_bwd")
    dw_in = mm_tn(h, dproj, f"{tag}_proj_wgrad")
    dx, dnw = mm_nt(dproj, w_in, f"{tag}_proj_bwd", norm_bwd=(x, nw, dout))
    return dx, dnw, dw_in, dwconv, dal, ddt, dwn, dw_out


N_KV, GROUP = 4, 4
KV_COLS = 2 * N_KV * B_HD
B_COLS = D + KV_COLS


@jax.custom_vjp
def _swap_lane_halves(x):
    return pltpu.roll(x, 64, 1)


_swap_lane_halves.defvjp(lambda x: (pltpu.roll(x, 64, 1), None), lambda _, g: (pltpu.roll(g, 64, 1),))


def _swa_block(q, kp, kc, vp, vc, sk, first):
    cols = GROUP * B_BLK
    ks = lax.broadcasted_iota(jnp.int32, (N_KV, B_BLK, cols), 1)
    qi = lax.broadcasted_iota(jnp.int32, (N_KV, B_BLK, cols), 2) % B_BLK
    from_cur = ks <= qi

    def batch(parts):
        return jnp.concatenate([part[None] for part in parts], axis=0)

    def per_kv(cur, prev):
        return batch([jnp.concatenate([cur[:, j * B_HD:(j + 1) * B_HD], prev[:, j * B_HD:(j + 1) * B_HD]], axis=0)
                      for j in range(N_KV)]).astype(BF16)

    qs = batch([jnp.concatenate([q[:, hq * B_HD:(hq + 1) * B_HD] for hq in range(GROUP * j, GROUP * (j + 1))], axis=0)
                for j in range(N_KV)])
    q_t = jnp.swapaxes(qs, 1, 2).astype(BF16)
    sink = batch([jnp.concatenate([jnp.broadcast_to(sk[:, hq:hq + 1], (1, B_BLK))
                                   for hq in range(GROUP * j, GROUP * (j + 1))], axis=1) for j in range(N_KV)])
    both = _bmm(per_kv(kc, kp), q_t, B_NN)
    s = jnp.where(from_cur, both[:, :B_BLK], jnp.where(first, -1e30, both[:, B_BLK:])) * (B_HD ** -0.5)
    m = lax.stop_gradient(jnp.maximum(jnp.max(s, axis=1, keepdims=True), sink))
    e = jnp.exp((s - m).astype(BF16))
    den = jnp.sum(e.astype(F32), axis=1, keepdims=True) + jnp.exp(sink - m)
    p = e * (1.0 / den).astype(BF16)
    zero = jnp.zeros_like(p)
    p_both = jnp.concatenate([jnp.where(from_cur, p, zero), jnp.where(from_cur, zero, p)], axis=1)
    o = jnp.swapaxes(_bmm(per_kv(vc, vp), p_both, B_TN), 1, 2)
    return jnp.concatenate([o[j, g * B_BLK:(g + 1) * B_BLK] for j in range(N_KV) for g in range(GROUP)], axis=1)


def swa_core_fwd(proj, sk, name):
    T = proj.shape[0]
    half = N_KV * B_HD

    def body(q_ref, kvc_ref, kvp_ref, sk_ref, o_ref):
        kv = jnp.concatenate([kvp_ref[...], kvc_ref[...]], axis=0)
        for b in range(2):
            prev, cur = kv[b * B_BLK:(b + 1) * B_BLK], kv[(b + 1) * B_BLK:(b + 2) * B_BLK]
            first = (pl.program_id(0) == 0) if b == 0 else False
            o_ref[b * B_BLK:(b + 1) * B_BLK, :] = _swa_block(
                q_ref[b * B_BLK:(b + 1) * B_BLK, :], prev[:, :half], cur[:, :half], prev[:, half:], cur[:, half:],
                sk_ref[...], first).astype(BF16)

    pair = 2 * B_BLK
    return _pc(body, name=name, out_shape=_sds((T, D), BF16), grid=(T // pair,),
               in_specs=[pl.BlockSpec((pair, D), lambda n: (n, 0)),
                         pl.BlockSpec((pair, KV_COLS), lambda n: (n, D // KV_COLS)),
                         pl.BlockSpec((B_BLK, KV_COLS), lambda n: (jnp.maximum(2 * n - 1, 0), D // KV_COLS)),
                         pl.BlockSpec((1, 128), lambda n: (0, 0))],
               out_specs=pl.BlockSpec((pair, D), lambda n: (n, 0)), sem=("parallel",))(proj, proj, proj, sk)


def swa_core_bwd(proj, sk, do, name):
    T = proj.shape[0]
    pair = 2 * B_BLK
    last = T // pair - 1
    half = N_KV * B_HD

    def body(q_ref, kvc_ref, kvp_ref, sk_ref, do_ref, dproj_ref, dbias_ref, dsk_ref, carry):
        step = pl.program_id(0)

        @pl.when(step == 0)
        def _():
            carry[...] = jnp.zeros_like(carry)
            dbias_ref[...] = jnp.zeros_like(dbias_ref)
            dsk_ref[...] = jnp.zeros_like(dsk_ref)
        kv = jnp.concatenate([kvp_ref[...], kvc_ref[...]], axis=0)
        from_later = carry[...]
        rows, dsk_sum = [None, None], jnp.zeros((1, 128), F32)
        for b in (1, 0):
            prev, cur = kv[b * B_BLK:(b + 1) * B_BLK], kv[(b + 1) * B_BLK:(b + 2) * B_BLK]
            first = (step == last) if b == 0 else False
            _, vjp = jax.vjp(functools.partial(_swa_block, first=first), q_ref[b * B_BLK:(b + 1) * B_BLK, :],
                             prev[:, :half], cur[:, :half], prev[:, half:], cur[:, half:], sk_ref[...])
            dq, dkp, dkc, dvp, dvc, dsk = vjp(do_ref[b * B_BLK:(b + 1) * B_BLK, :])
            rows[b] = jnp.concatenate([dq, jnp.concatenate([dkc, dvc], axis=1) + from_later], axis=1)
            from_later = jnp.concatenate([dkp, dvp], axis=1)
            dsk_sum = dsk_sum + dsk
        carry[...] = from_later
        both = jnp.concatenate(rows, axis=0)
        dproj_ref[...] = both.astype(BF16)
        dbias_ref[...] += jnp.sum(both, axis=0, keepdims=True)
        dsk_ref[...] += dsk_sum

    return _pc(body, name=name, out_shape=(_sds((T, B_COLS), BF16), _sds((1, B_COLS), F32), _sds((1, 128), F32)),
               grid=(T // pair,),
               in_specs=[pl.BlockSpec((pair, D), lambda n: (last - n, 0)),
                         pl.BlockSpec((pair, KV_COLS), lambda n: (last - n, D // KV_COLS)),
                         pl.BlockSpec((B_BLK, KV_COLS), lambda n: (jnp.maximum(2 * (last - n) - 1, 0), D // KV_COLS)),
                         pl.BlockSpec((1, 128), lambda n: (0, 0)), pl.BlockSpec((pair, D), lambda n: (last - n, 0))],
               out_specs=(pl.BlockSpec((pair, B_COLS), lambda n: (last - n, 0)),
                          pl.BlockSpec((1, B_COLS), lambda n: (0, 0)), pl.BlockSpec((1, 128), lambda n: (0, 0))),
               scratch=[pltpu.VMEM((B_BLK, KV_COLS), F32)], sem=("arbitrary",))(proj, proj, proj, sk, do)


def col_sum(a, name):
    T, N = a.shape
    tm = _tile(T, 1024)

    def body(a_ref, o_ref):
        @pl.when(pl.program_id(0) == 0)
        def _():
            o_ref[...] = jnp.zeros_like(o_ref)
        o_ref[...] += jnp.sum(a_ref[...].astype(F32), axis=0, keepdims=True)

    return _pc(body, name=name, out_shape=_sds((1, N), F32), grid=(T // tm,),
               in_specs=[pl.BlockSpec((tm, N), lambda i: (i, 0))], out_specs=pl.BlockSpec((1, N), lambda i: (0, 0)),
               sem=("arbitrary",))(a)


def swa_forward(x, nw, w_in, b_in, sk, w_out, b_out, tag):
    h = rmsnorm_bf16(x, nw, f"{tag}_norm")
    proj = mm_nn(h, w_in, f"{tag}_proj", bias=b_in)
    o = swa_core_fwd(proj, sk, f"{tag}_core")
    y = mm_nn(o, w_out, f"{tag}_out", bias=b_out, residual=x)
    return y, (x, h, proj, o)


def swa_backward(dout, saved, nw, w_in, b_in, sk, w_out, b_out, tag):
    x, h, proj, o = saved
    do = mm_nt(dout, w_out, f"{tag}_out_bwd")
    dw_out = mm_tn(o, dout, f"{tag}_out_wgrad")
    db_out = col_sum(dout, f"{tag}_out_bias_grad")
    dproj, db_in, dsk = swa_core_bwd(proj, sk, do, f"{tag}_core_bwd")
    dw_in = mm_tn(h, dproj, f"{tag}_proj_wgrad")
    dx, dnw = mm_nt(dproj, w_in, f"{tag}_proj_bwd", norm_bwd=(x, nw, dout))
    return dx, dnw, dw_in, db_in, dsk, dw_out, db_out


MESH = pl.DeviceIdType.MESH


def _position():
    return lax.axis_index("x"), lax.axis_index("y"), lax.axis_index("c")


def _slot(x, y, c):
    return 4 * x + 2 * y + c


def _peer(x, y, c, k):
    return (1 - x if k & 4 else x, 1 - y if k & 2 else y, 1 - c if k & 1 else c)


HBM_SPEC = pl.BlockSpec(memory_space=pltpu.HBM)
SEM_SPEC = pl.BlockSpec(memory_space=pltpu.SEMAPHORE)
DEP_SPEC = pl.BlockSpec(memory_space=pl.ANY)
SIDE_EFFECT = pltpu.SideEffectType.DATAFLOW_SIDE_EFFECTING
N_PEERS = N_DEV - 1


def _push_copies(srcs, lands, send_sems, recv_sems, scatter):
    x, y, c = _position()
    me = _slot(x, y, c)
    copies = []
    for k in (1, 2, 4, 3, 5, 6, 7):
        peer = _peer(x, y, c, k)
        for a in range(len(srcs)):
            copies.append(pltpu.make_async_remote_copy(
                src_ref=srcs[a].at[_slot(*peer)] if scatter else srcs[a], dst_ref=lands[a].at[me],
                send_sem=send_sems.at[N_PEERS * a + k - 1], recv_sem=recv_sems.at[N_PEERS * a + k - 1],
                device_id=peer, device_id_type=MESH))
    return copies


def push_start(srcs, lands, name, scatter, deps=()):
    n = len(srcs)
    first_out = 2 * n + len(deps)

    def body(*refs):
        for cp in _push_copies(refs[:n], refs[n:2 * n], refs[first_out], refs[first_out + 1], scatter):
            cp.start()
        refs[-1][...] = jnp.zeros_like(refs[-1])

    passed = [pltpu.HBM(t.shape, t.dtype) for t in list(srcs) + list(lands)]
    res = pl.pallas_call(
        body, name=name,
        out_shape=(pltpu.SemaphoreType.DMA((N_PEERS * n,)), pltpu.SemaphoreType.DMA((N_PEERS * n,)), *passed, _sds((8, 128), F32)),
        in_specs=[HBM_SPEC] * (2 * n) + [DEP_SPEC] * len(deps),
        out_specs=(SEM_SPEC, SEM_SPEC, *([HBM_SPEC] * (2 * n)), pl.BlockSpec(memory_space=pltpu.VMEM)),
        input_output_aliases={i: 2 + i for i in range(2 * n)},
        compiler_params=pltpu.CompilerParams(has_side_effects=SIDE_EFFECT),
    )(*[pltpu.with_memory_space_constraint(t, pltpu.HBM) for t in list(srcs) + list(lands)], *deps)
    return (res[0], res[1], list(res[2:2 + n]), list(res[2 + n:2 + 2 * n])), res[-1]


def push_wait(handles, after, name, scatter):
    send_sems, recv_sems, srcs, lands = handles
    n = len(srcs)
    after = tuple(after) if isinstance(after, (tuple, list)) else (after,)

    def body(*refs):
        for cp in _push_copies(refs[:n], refs[n:2 * n], refs[2 * n], refs[2 * n + 1], scatter):
            cp.wait_send()
            cp.wait_recv()

    res = pl.pallas_call(
        body, name=name, out_shape=tuple(pltpu.HBM(t.shape, t.dtype) for t in srcs + lands),
        in_specs=[HBM_SPEC] * (2 * n) + [SEM_SPEC, SEM_SPEC] + [DEP_SPEC] * len(after), out_specs=tuple([HBM_SPEC] * (2 * n)),
        input_output_aliases={i: i for i in range(2 * n)},
        compiler_params=pltpu.CompilerParams(has_side_effects=SIDE_EFFECT),
    )(*srcs, *lands, send_sems, recv_sems, *after)
    return list(res[n:])


def gather_start(shards, name, deps=()):
    me = _slot(*_position())
    lands = [lax.dynamic_update_slice(lax.empty((N_DEV,) + t.shape, t.dtype), t[None], (me,) + (0,) * t.ndim) for t in shards]
    return push_start(shards, lands, name, scatter=False, deps=deps)


def exchange_start(parts, name):
    me = _slot(*_position())
    lands = [lax.dynamic_update_slice(lax.empty(t.shape, t.dtype), lax.dynamic_index_in_dim(t, me, 0, keepdims=True),
                                      (me,) + (0,) * (t.ndim - 1)) for t in parts]
    return push_start(parts, lands, name, scatter=True)


def _row_tile(rows, cols):
    best = rows
    for t in range(16, rows, 16):
        if rows % t == 0 and t * cols * 4 <= (1 << 20):
            best = t
    return best


def adam_update(parts, w, m, v, name):
    n_layers = len(parts)
    P, R, C = parts[0].shape
    tr = _row_tile(R, C)
    n_t = R // tr

    def body(*refs):
        p_refs = refs[:n_layers]
        w_ref, m_ref, v_ref, g_ref, d_ref, nm_ref, nv_ref = refs[n_layers:]
        for layer in range(n_layers):
            @pl.when(pl.program_id(0) == layer)
            def _(p_ref=p_refs[layer]):
                g = p_ref[0].astype(F32)
                for s in range(1, P):
                    g = g + p_ref[s].astype(F32)
                new_m = ADAM_B1 * m_ref[0] + (1.0 - ADAM_B1) * g
                new_v = ADAM_B2 * v_ref[0] + (1.0 - ADAM_B2) * (g * g)
                m_hat = new_m / (1.0 - ADAM_B1 ** ADAM_STEP)
                v_hat = new_v / (1.0 - ADAM_B2 ** ADAM_STEP)
                g_ref[0] = g
                d_ref[0] = -ADAM_LR * (m_hat / (jnp.sqrt(v_hat) + ADAM_EPS) + ADAM_WD * w_ref[0])
                nm_ref[0] = new_m
                nv_ref[0] = new_v

    def part_spec(layer):
        return pl.BlockSpec((P, tr, C), lambda l_, i: (0, jnp.where(l_ == layer, i, jnp.where(l_ < layer, 0, n_t - 1)), 0))

    blk = pl.BlockSpec((1, tr, C), lambda l_, i: (l_, i, 0))
    out = _sds((n_layers, R, C), F32)
    return _pc(body, name=name, out_shape=(out, out, out, out), grid=(n_layers, n_t),
               in_specs=[part_spec(layer) for layer in range(n_layers)] + [blk, blk, blk],
               out_specs=(blk, blk, blk, blk), sem=("arbitrary", "arbitrary"))(*parts, w, m, v)


WEIGHTS = ("ffn1_norm", "ffn1_w_gu", "ffn1_w_down", "mix_norm", "ffn2_norm", "ffn2_w_gu", "ffn2_w_down", "a_w_in",
           "a_w_conv", "a_A_log", "a_dt_bias", "a_out_norm", "a_w_out", "b_w_in", "b_b_in", "b_sinks", "b_w_out",
           "b_b_out", "final_norm")
SHARDED = ("ffn1_w_gu", "ffn1_w_down", "ffn2_w_gu", "ffn2_w_down", "a_w_in", "a_w_conv", "a_w_out", "b_w_in", "b_b_in",
           "b_w_out", "b_b_out")
MISC_LANES = dict(a_A_log=(0, 8), a_dt_bias=(8, 16), b_sinks=(16, 32), a_out_norm=(128, 256))
LOSS_LANE = 256


def _pack_small(t):
    misc = jnp.zeros((D,), F32)
    for key, (lo, hi) in MISC_LANES.items():
        misc = misc.at[lo:hi].set(t[key].reshape(-1))
    if "loss" in t:
        misc = misc.at[LOSS_LANE].set(t["loss"])
    return jnp.concatenate([t["ffn1_norm"], t["mix_norm"], t["ffn2_norm"], t["final_norm"].reshape(1, D), misc[None]], axis=0)


def _unpack_small(p, like):
    out = dict(ffn1_norm=p[0:2], mix_norm=p[2:4], ffn2_norm=p[4:6], final_norm=p[6])
    for key, (lo, hi) in MISC_LANES.items():
        out[key] = p[7, lo:hi].reshape(like[key].shape)
    return out


def kernel(x, ffn1_norm, ffn1_w_gu, ffn1_w_down, mix_norm, ffn2_norm, ffn2_w_gu, ffn2_w_down, a_w_in, a_w_conv, a_A_log, a_dt_bias, a_out_norm, a_w_out, b_w_in, b_b_in, b_sinks, b_w_out, b_b_out, final_norm, loss_target, m_ffn1_norm, m_ffn1_w_gu, m_ffn1_w_down, m_mix_norm, m_ffn2_norm, m_ffn2_w_gu, m_ffn2_w_down, m_a_w_in, m_a_w_conv, m_a_A_log, m_a_dt_bias, m_a_out_norm, m_a_w_out, m_b_w_in, m_b_b_in, m_b_sinks, m_b_w_out, m_b_b_out, m_final_norm, v_ffn1_norm, v_ffn1_w_gu, v_ffn1_w_down, v_mix_norm, v_ffn2_norm, v_ffn2_w_gu, v_ffn2_w_down, v_a_w_in, v_a_w_conv, v_a_A_log, v_a_dt_bias, v_a_out_norm, v_a_w_out, v_b_w_in, v_b_b_in, v_b_sinks, v_b_w_out, v_b_b_out, v_final_norm):
    w = dict(ffn1_norm=ffn1_norm, ffn1_w_gu=ffn1_w_gu, ffn1_w_down=ffn1_w_down, mix_norm=mix_norm, ffn2_norm=ffn2_norm, ffn2_w_gu=ffn2_w_gu, ffn2_w_down=ffn2_w_down, a_w_in=a_w_in, a_w_conv=a_w_conv, a_A_log=a_A_log, a_dt_bias=a_dt_bias, a_out_norm=a_out_norm, a_w_out=a_w_out, b_w_in=b_w_in, b_b_in=b_b_in, b_sinks=b_sinks, b_w_out=b_w_out, b_b_out=b_b_out, final_norm=final_norm)
    m = dict(ffn1_norm=m_ffn1_norm, ffn1_w_gu=m_ffn1_w_gu, ffn1_w_down=m_ffn1_w_down, mix_norm=m_mix_norm, ffn2_norm=m_ffn2_norm, ffn2_w_gu=m_ffn2_w_gu, ffn2_w_down=m_ffn2_w_down, a_w_in=m_a_w_in, a_w_conv=m_a_w_conv, a_A_log=m_a_A_log, a_dt_bias=m_a_dt_bias, a_out_norm=m_a_out_norm, a_w_out=m_a_w_out, b_w_in=m_b_w_in, b_b_in=m_b_b_in, b_sinks=m_b_sinks, b_w_out=m_b_w_out, b_b_out=m_b_b_out, final_norm=m_final_norm)
    v = dict(ffn1_norm=v_ffn1_norm, ffn1_w_gu=v_ffn1_w_gu, ffn1_w_down=v_ffn1_w_down, mix_norm=v_mix_norm, ffn2_norm=v_ffn2_norm, ffn2_w_gu=v_ffn2_w_gu, ffn2_w_down=v_ffn2_w_down, a_w_in=v_a_w_in, a_w_conv=v_a_w_conv, a_A_log=v_a_A_log, a_dt_bias=v_a_dt_bias, a_out_norm=v_a_out_norm, a_w_out=v_a_w_out, b_w_in=v_b_w_in, b_b_in=v_b_b_in, b_sinks=v_b_sinks, b_w_out=v_b_w_out, b_b_out=v_b_b_out, final_norm=v_final_norm)
    T = x.shape[1]
    x0, tgt = x.reshape(T, D), loss_target.reshape(T, D)

    def cast(t):
        return t.astype(BF16)

    h0, t0 = gather_start([cast(ffn1_w_gu[0])], "gather0_start")
    a_log_row = jnp.zeros((1, 128), F32).at[0, HEADS_A:2 * HEADS_A].set(a_A_log[0])
    dt_row = jnp.zeros((1, 128), F32).at[0, HEADS_A:2 * HEADS_A].set(a_dt_bias[0])
    sink_row = jnp.zeros((1, 128), F32).at[0, :b_sinks.shape[1]].set(b_sinks[0])
    a_in_cols = a_w_in.shape[-1] * N_DEV

    def down_blocks(t):
        return t.reshape(N_FB, FB, D)

    wgu, wdn, saved = {}, {}, []
    xn = rmsnorm_bf16(x0, ffn1_norm[0:1], "l0_ffn1_norm", (t0,))
    wgu["ffn1", 0] = push_wait(h0, xn, "gather0_wait", scatter=False)[0]
    h0d, t0d = gather_start([cast(ffn1_w_down[0])], "gather0d_start", deps=(wgu["ffn1", 0],))
    h1, t1 = gather_start([cast(a_w_in[0]), a_w_conv[0], cast(a_w_out[0])], "gather1_start", deps=(t0d,))
    gu = ffn_up(xn, wgu["ffn1", 0], "l0_ffn1_up", deps=(t0d, t1))
    wdn["ffn1", 0] = down_blocks(push_wait(h0d, gu, "gather0d_wait", scatter=False)[0])
    xs, s1 = ffn_down(gu, wdn["ffn1", 0], x0, "l0_ffn1_down"), (x0, xn, gu)
    got = push_wait(h1, xs, "gather1_wait", scatter=False)
    h1f, t1f = gather_start([cast(ffn2_w_gu[0]), cast(ffn2_w_down[0])], "gather1f_start", deps=(got[0],))
    g2 = [cast(ffn1_w_gu[1]), cast(ffn1_w_down[1]), cast(b_w_in[0]), b_b_in, cast(b_w_out[0]), b_b_out,
          cast(ffn2_w_gu[1]), cast(ffn2_w_down[1])]
    h2, t2 = gather_start(g2, "gather2_start", deps=(t1f,))
    a_in_full = jnp.pad(got[0].transpose(1, 0, 2).reshape(D, a_in_cols), ((0, 0), (0, A_COLS - a_in_cols)))
    gdn_args = (mix_norm[0:1], a_in_full, got[1].transpose(1, 0, 2).reshape(4, 3 * D), a_log_row, dt_row, a_out_norm,
                got[2].reshape(D, D))
    xs, sm = gdn_forward(xs, *gdn_args, "gdn", deps=(t1f, t2))
    got = push_wait(h1f, xs, "gather1f_wait", scatter=False)
    wgu["ffn2", 0], wdn["ffn2", 0] = got[0], down_blocks(got[1])
    xs, s2 = ffn_forward(xs, ffn2_norm[0:1], wgu["ffn2", 0], wdn["ffn2", 0], "l0_ffn2")
    saved.append((s1, sm, s2))
    got = push_wait(h2, xs, "gather2_wait", scatter=False)
    wgu["ffn1", 1], wdn["ffn1", 1] = got[0], down_blocks(got[1])
    swa_args = (mix_norm[1:2], got[2].transpose(1, 0, 2).reshape(D, B_COLS), got[3].reshape(1, B_COLS), sink_row,
                got[4].reshape(D, D), got[5].reshape(1, D))
    wgu["ffn2", 1], wdn["ffn2", 1] = got[6], down_blocks(got[7])
    xs, s1 = ffn_forward(xs, ffn1_norm[1:2], wgu["ffn1", 1], wdn["ffn1", 1], "l1_ffn1")
    xs, sm = swa_forward(xs, *swa_args, "swa")
    xs, s2 = ffn_forward(xs, ffn2_norm[1:2], wgu["ffn2", 1], wdn["ffn2", 1], "l1_ffn2")
    saved.append((s1, sm, s2))
    loss_row, dx, d_final_norm = final_loss(xs, final_norm.reshape(1, D), tgt, "final_loss")

    def down_slots(t):
        return cast(t.reshape(N_DEV, FB // 2, D))

    def col_slots(t, dtype=BF16):
        return t.reshape(t.shape[0], N_DEV, -1).transpose(1, 0, 2).astype(dtype)

    d_norm = {"ffn1_norm": [None, None], "mix_norm": [None, None], "ffn2_norm": [None, None]}
    exchanges = {}

    def sender(tag):
        def on_grads(d_gu, d_dn):
            exchanges[tag], token = exchange_start([cast(d_gu), down_slots(d_dn)], f"exchange_{tag}_start")
            return (token,)
        return on_grads

    s1, sm, s2 = saved[1]
    dx, d_norm["ffn2_norm"][1], _, _ = ffn_backward(dx, s2, ffn2_norm[1:2], wgu["ffn2", 1], wdn["ffn2", 1], "l1_ffn2",
                                                    on_grads=sender("l1_ffn2"))
    dx, d_norm["mix_norm"][1], d_b_in, d_b_bias_in, d_sinks, d_b_out, d_b_bias_out = swa_backward(dx, sm, *swa_args, "swa")
    exchanges["swa"], t_swa = exchange_start(
        [col_slots(d_b_in), d_b_bias_in.reshape(N_DEV, 1, -1), cast(d_b_out.reshape(N_DEV, D // N_DEV, D)),
         d_b_bias_out.reshape(N_DEV, 1, -1)], "exchange_swa_start")
    dx, d_norm["ffn1_norm"][1], _, _ = ffn_backward(dx, s1, ffn1_norm[1:2], wgu["ffn1", 1], wdn["ffn1", 1], "l1_ffn1",
                                                    deps=(t_swa,), on_grads=sender("l1_ffn1"))

    s1, sm, s2 = saved[0]
    dx, d_norm["ffn2_norm"][0], _, _ = ffn_backward(dx, s2, ffn2_norm[0:1], wgu["ffn2", 0], wdn["ffn2", 0], "l0_ffn2",
                                                    on_grads=sender("l0_ffn2"))
    dx, d_norm["mix_norm"][0], d_a_in, d_a_conv, d_alog, d_dt, d_onorm, d_a_out = gdn_backward(dx, sm, *gdn_args, "gdn")
    exchanges["gdn"], t_gdn = exchange_start(
        [col_slots(d_a_in[:, :a_in_cols]), col_slots(d_a_conv, F32), cast(d_a_out.reshape(N_DEV, D // N_DEV, D))],
        "exchange_gdn_start")
    dx, d_norm["ffn1_norm"][0], _, _ = ffn_backward(dx, s1, ffn1_norm[0:1], wgu["ffn1", 0], wdn["ffn1", 0], "l0_ffn1",
                                                    deps=(t_gdn,), on_grads=sender("l0_ffn1"))
    grad_x = dx.reshape(x.shape)
    got = {tag: push_wait(exchanges[tag], dx, f"exchange_{tag}_wait", scatter=True)
           for tag in ("l1_ffn2", "swa", "l1_ffn1", "l0_ffn2", "gdn")}
    received = dict(ffn2_w_gu=[got["l0_ffn2"][0], got["l1_ffn2"][0]], ffn2_w_down=[got["l0_ffn2"][1], got["l1_ffn2"][1]],
                    b_w_in=[got["swa"][0]], b_b_in=[got["swa"][1]], b_w_out=[got["swa"][2]], b_b_out=[got["swa"][3]],
                    a_w_in=[got["gdn"][0]], a_w_conv=[got["gdn"][1]], a_w_out=[got["gdn"][2]])

    grads, deltas, new_m, new_v = {}, {}, {}, {}

    def update(key):
        shape = w[key].shape
        cols = shape[-1]
        layers = lambda t: t.reshape(shape[0], -1, cols)
        out = adam_update([r.reshape(N_DEV, -1, cols) for r in received[key]], layers(w[key]), layers(m[key]), layers(v[key]),
                          f"adam_{key}")
        grads[key], deltas[key], new_m[key], new_v[key] = (t.reshape(shape) for t in out)

    for key in SHARDED:
        if key in received:
            update(key)
    done_first = [deltas[key] for key in received]

    small = dict(ffn1_norm=jnp.concatenate(d_norm["ffn1_norm"], axis=0), mix_norm=jnp.concatenate(d_norm["mix_norm"], axis=0),
                 ffn2_norm=jnp.concatenate(d_norm["ffn2_norm"], axis=0), final_norm=d_final_norm,
                 a_A_log=d_alog[0, HEADS_A:2 * HEADS_A], a_dt_bias=d_dt[0, HEADS_A:2 * HEADS_A],
                 b_sinks=d_sinks[0, :b_sinks.shape[1]], a_out_norm=d_onorm, loss=loss_row[0, 0])
    hs, ts = gather_start([_pack_small(small)], "gather_small_start")
    r3 = push_wait(exchanges["l0_ffn1"], done_first + [ts], "exchange_l0_ffn1_wait", scatter=True)
    received.update(ffn1_w_gu=[r3[0], got["l1_ffn1"][0]], ffn1_w_down=[r3[1], got["l1_ffn1"][1]])
    update("ffn1_w_gu")
    update("ffn1_w_down")
    every = push_wait(hs, deltas["ffn1_w_down"], "gather_small_wait", scatter=False)[0]
    out = adam_update([every], _pack_small(w)[None], _pack_small(m)[None], _pack_small(v)[None], "adam_small")
    for dst, packed in zip((grads, deltas, new_m, new_v), out):
        dst.update(_unpack_small(packed[0], w))
    loss = out[0][0, 7, LOSS_LANE]

    return (loss, grad_x, *[grads[k_] for k_ in WEIGHTS], *[deltas[k_] for k_ in WEIGHTS],
            *[new_m[k_] for k_ in WEIGHTS], *[new_v[k_] for k_ in WEIGHTS])
```
